```python
import jax
import jax.numpy as jnp
from jax import lax
import numpy as np

D_MODEL = 2048
BATCH = 2
SEQ = 8192
DEPTH = 2

GRID_W = 64
CTX_LEN = 256
N_MOD = 6
D_FF = 4 * D_MODEL
NORM_EPS = 1e-6

A_WIDTH = D_MODEL // 2
A_HEAD_DIM = 64
A_HEADS = A_WIDTH // A_HEAD_DIM
A_LORA_W = 96
A_LORA_A = 96
A_LORA_G = 256
A_LN_EPS = 64e-5
A_COLS = 3 * A_WIDTH + A_LORA_G + 2 * A_LORA_W + 2 * A_LORA_A

B_HEADS = 4
B_WIDTH_V = D_MODEL // 2
B_DV = B_WIDTH_V // B_HEADS
B_DK = B_DV // 2
B_WIDTH_K = B_HEADS * B_DK
B_LORA = 16
B_GATE_TAU = 16.0
B_CHUNK = 64
B_COLS = 2 * B_WIDTH_K + 2 * B_WIDTH_V + 2 * B_LORA
AB_COLS = A_COLS + B_COLS

C_WIDTH = D_MODEL
C_BLOCKS = 8
C_BLOCK = C_WIDTH // C_BLOCKS
C_CONV = 4
C_CONST = 8.0

kernel_name = "hybrid_rwkv7_gla_rglru_dit_block"


def rms_norm(x, g):
    xf = x.astype(jnp.float32)
    y = xf * lax.rsqrt(jnp.mean(xf * xf, axis=-1, keepdims=True) + NORM_EPS)
    return (y * g.astype(jnp.float32)).astype(x.dtype)


def seg_flip(t, n_ctx):
    return jnp.concatenate([jnp.flip(t[:, :n_ctx], 1), jnp.flip(t[:, n_ctx:], 1)], axis=1)


def shift_mix(p, mu):
    prev = jnp.pad(p[:, :-1], ((0, 0), (1, 0), (0, 0)))
    nxt = jnp.pad(p[:, 1:], ((0, 0), (0, 1), (0, 0)))
    return p + mu[0] * (prev - p) + mu[1] * (nxt - p)


def rwkv7_scan(r, w, k, v, a, b):
    bsz, _, h, n = r.shape

    def step(state, inp):
        r_t, w_t, k_t, v_t, a_t, b_t = inp
        sa = jnp.einsum('bhvk,bhk->bhv', state, a_t)
        state = (state * w_t[:, :, None, :] + sa[..., None] * b_t[:, :, None, :]
                 + v_t[..., None] * k_t[:, :, None, :])
        return state, jnp.einsum('bhvk,bhk->bhv', state, r_t)

    s0 = jnp.zeros((bsz, h, n, n), jnp.float32)
    xs = tuple(jnp.moveaxis(z, 1, 0) for z in (r, w, k, v, a, b))
    _, y = lax.scan(step, s0, xs)
    return jnp.moveaxis(y, 0, 1)


def rwkv7_mixer(pc, pl, mu, w0, w2, a0, a2, g2, k_k, k_a, r_k, ln_w, ln_b):
    n_ctx = pc.shape[1]
    p = jnp.concatenate([shift_mix(pc, mu), shift_mix(pl, mu)], axis=1).astype(jnp.float32)
    bsz, t = p.shape[:2]
    hs = (bsz, t, A_HEADS, A_HEAD_DIM)
    idx = [A_WIDTH, 2 * A_WIDTH, 3 * A_WIDTH, 3 * A_WIDTH + A_LORA_G,
           3 * A_WIDTH + A_LORA_G + 2 * A_LORA_W]
    r, k, v, gd, wd, ad = jnp.split(p, idx, axis=-1)
    wd = wd.reshape(bsz, t, 2, A_LORA_W)
    ad = ad.reshape(bsz, t, 2, A_LORA_A)
    g = jax.nn.sigmoid(gd) @ g2
    kk = (k * k_k).reshape(hs)
    kk = kk / jnp.maximum(jnp.sqrt(jnp.sum(kk * kk, -1, keepdims=True)), 1e-12)
    rh = r.reshape(hs)
    vh = v.reshape(hs)

    def direction(d):
        w_log = -jax.nn.softplus(-(w0[d] + jnp.tanh(wd[:, :, d]) @ w2[d])) - 0.5
        decay = jnp.exp(-jnp.exp(w_log)).reshape(hs)
        a = jax.nn.sigmoid(a0[d] + ad[:, :, d] @ a2[d])
        kd = (k * (1.0 + (a - 1.0) * k_a)).reshape(hs)
        ins = (rh, decay, kd, vh, -kk, kk * a.reshape(hs))
        if d == 1:
            yd = seg_flip(rwkv7_scan(*(seg_flip(z, n_ctx) for z in ins)), n_ctx)
        else:
            yd = rwkv7_scan(*ins)
        bonus = jnp.sum(rh * kd * r_k, -1, keepdims=True) * vh
        return yd, bonus

    y_f, bonus_f = direction(0)
    y_b, bonus_b = direction(1)
    y = y_f + y_b
    mean = jnp.mean(y, -1, keepdims=True)
    var = jnp.mean(jnp.square(y - mean), -1, keepdims=True)
    y = ((y - mean) * lax.rsqrt(var + A_LN_EPS)).reshape(bsz, t, A_WIDTH) * ln_w + ln_b
    y = (y + (bonus_f + bonus_b).reshape(bsz, t, A_WIDTH)) * g
    return y[:, :n_ctx], y[:, n_ctx:]


def gla_chunked(q, k, v, log_a):
    bsz, t, h, dk = q.shape
    dv = v.shape[-1]
    n = t // B_CHUNK
    q, k, log_a = (z.reshape(bsz, n, B_CHUNK, h, dk) for z in (q, k, log_a))
    v = v.reshape(bsz, n, B_CHUNK, h, dv)
    cum = jnp.cumsum(log_a, axis=2)
    q_in = q * jnp.exp(cum)
    k_in = k * jnp.exp(-cum)
    lower = jnp.tril(jnp.ones((B_CHUNK, B_CHUNK), dtype=bool))
    scores = jnp.where(lower, jnp.einsum('bnihk,bnjhk->bnhij', q_in, k_in), 0.0)
    intra = jnp.einsum('bnhij,bnjhv->bnihv', scores, v)
    cum_last = cum[:, :, -1]
    k_end = k * jnp.exp(cum_last[:, :, None] - cum)
    chunk_kv = jnp.einsum('bnjhk,bnjhv->bnhkv', k_end, v)
    chunk_decay = jnp.exp(cum_last)

    def step(state, inp):
        kv, dec = inp
        return state * dec[..., None] + kv, state

    s0 = jnp.zeros((bsz, h, dk, dv), jnp.float32)
    _, s_in = lax.scan(step, s0, (jnp.moveaxis(chunk_kv, 1, 0), jnp.moveaxis(chunk_decay, 1, 0)))
    inter = jnp.einsum('bnihk,nbhkv->bnihv', q_in, s_in)
    return (intra + inter).reshape(bsz, t, h, dv)


def gla_mixer(pc, pl, gw2, gb, norm_g):
    n_ctx = pc.shape[1]
    p = jnp.concatenate([pc, pl], axis=1).astype(jnp.float32)
    bsz, t = p.shape[:2]
    idx = [B_WIDTH_K, 2 * B_WIDTH_K, 2 * B_WIDTH_K + B_WIDTH_V, 2 * B_WIDTH_K + 2 * B_WIDTH_V]
    q, k, v, g, gd = jnp.split(p, idx, axis=-1)
    q = q.reshape(bsz, t, B_HEADS, B_DK) * (B_DK ** -0.5)
    k = k.reshape(bsz, t, B_HEADS, B_DK)
    v = v.reshape(bsz, t, B_HEADS, B_DV)
    gd = gd.reshape(bsz, t, 2, B_LORA)

    def log_gate(d):
        z = gd[:, :, d] @ gw2[d] + gb[d]
        return (jax.nn.log_sigmoid(z) / B_GATE_TAU).reshape(bsz, t, B_HEADS, B_DK)

    o_f = gla_chunked(q, k, v, log_gate(0))
    o_b = seg_flip(gla_chunked(*(seg_flip(z, n_ctx) for z in (q, k, v, log_gate(1)))), n_ctx)
    o = o_f + o_b
    o = o * lax.rsqrt(jnp.mean(o * o, -1, keepdims=True) + NORM_EPS) * norm_g
    o = o.reshape(bsz, t, B_WIDTH_V) * jax.nn.silu(g)
    return o[:, :n_ctx], o[:, n_ctx:]


def dwconv_centred(x, w, b):
    t = x.shape[1]
    left = C_CONV // 2
    xp = jnp.pad(x, ((0, 0), (left, C_CONV - 1 - left), (0, 0)))
    y = b + xp[:, 0:t] * w[0]
    for j in range(1, C_CONV):
        y = y + xp[:, j:j + t] * w[j]
    return y


def linear_scan(a, u):
    def combine(left, right):
        a_l, u_l = left
        a_r, u_r = right
        return a_l * a_r, a_r * u_l + u_r
    return lax.associative_scan(combine, (a, u), axis=1)[1]


def to_col_major(t, rows):
    b, _, ch = t.shape
    return t.reshape(b, rows, GRID_W, ch).transpose(0, 2, 1, 3).reshape(b, rows * GRID_W, ch)


def from_col_major(t, rows):
    b, _, ch = t.shape
    return t.reshape(b, GRID_W, rows, ch).transpose(0, 2, 1, 3).reshape(b, rows * GRID_W, ch)


def rglru_mixer(hc, hl, w_in, conv_w, conv_b, wa, ba, wx, bx, lam, rows):
    n_ctx = hc.shape[1]
    gate_c, xc = jnp.split(hc @ w_in, 2, axis=-1)
    gate_l, xl = jnp.split(hl @ w_in, 2, axis=-1)
    xs = jnp.concatenate([dwconv_centred(xc, conv_w, conv_b),
                          dwconv_centred(to_col_major(xl, rows), conv_w, conv_b)],
                         axis=1).astype(jnp.float32)
    bsz, t, _ = xs.shape
    xb = xs.reshape(bsz, t, C_BLOCKS, C_BLOCK)

    def direction(d):
        r = jax.nn.sigmoid(jnp.einsum('btnc,ncd->btnd', xb, wa[d]).reshape(bsz, t, C_WIDTH) + ba[d])
        i = jax.nn.sigmoid(jnp.einsum('btnc,ncd->btnd', xb, wx[d]).reshape(bsz, t, C_WIDTH) + bx[d])
        log_a = -C_CONST * r * jax.nn.softplus(-lam[d])
        a = jnp.exp(log_a)
        u = jnp.sqrt(-jnp.expm1(2.0 * log_a)) * (i * xs)
        if d == 1:
            return seg_flip(linear_scan(seg_flip(a, n_ctx), seg_flip(u, n_ctx)), n_ctx)
        return linear_scan(a, u)

    h = direction(0) + direction(1)
    yc = h[:, :n_ctx] * jax.nn.gelu(gate_c.astype(jnp.float32))
    yl = from_col_major(h[:, n_ctx:], rows) * jax.nn.gelu(gate_l.astype(jnp.float32))
    return yc, yl


def sq_relu_mlp(h, w1, w2):
    return jnp.square(jax.nn.relu(h @ w1)) @ w2


def setup_inputs(seed: int = 0) -> dict:
    key = jax.random.key(seed)
    ks = iter(jax.random.split(key, 48))
    D = D_MODEL
    ne = (DEPTH + 1) // 2
    no = DEPTH // 2

    def nrm(shape, scale):
        return jax.random.normal(next(ks), shape, jnp.float32) * scale

    def uni(shape, lo, hi):
        return jax.random.uniform(next(ks), shape, jnp.float32, minval=lo, maxval=hi)

    inp = {}
    inp["x"] = nrm((BATCH, SEQ, D), 1.0)
    inp["c"] = nrm((BATCH, D), 1.0)
    inp["ctx"] = nrm((BATCH, CTX_LEN, D), 1.0)
    inp["c_ctx"] = nrm((D,), 1.0)
    inp["mod_w"] = nrm((DEPTH, D, N_MOD * D), 0.5 * D ** -0.5)
    inp["mod_b"] = nrm((DEPTH, N_MOD * D), 0.02)
    inp["norm1"] = 1.0 + nrm((DEPTH, D), 0.02)
    inp["norm2"] = 1.0 + nrm((DEPTH, D), 0.02)
    inp["mlp_w1"] = nrm((DEPTH, D, D_FF), D ** -0.5)
    inp["mlp_w2"] = nrm((DEPTH, D_FF, D), D_FF ** -0.5)
    inp["ab_w_in"] = nrm((ne, D, AB_COLS), D ** -0.5)
    inp["ab_w_out"] = nrm((ne, A_WIDTH + B_WIDTH_V, D), (A_WIDTH + B_WIDTH_V) ** -0.5)
    inp["rw_mu"] = uni((ne, 2, A_COLS), 0.0, 0.5)
    inp["rw_w0"] = uni((ne, 2, A_WIDTH), -7.0, -2.0)
    inp["rw_w2"] = nrm((ne, 2, A_LORA_W, A_WIDTH), 0.5 * A_LORA_W ** -0.5)
    inp["rw_a0"] = nrm((ne, 2, A_WIDTH), 0.1)
    inp["rw_a2"] = nrm((ne, 2, A_LORA_A, A_WIDTH), 0.5 * A_LORA_A ** -0.5)
    inp["rw_g2"] = nrm((ne, A_LORA_G, A_WIDTH), A_LORA_G ** -0.5)
    inp["rw_kk"] = 0.85 + nrm((ne, A_WIDTH), 0.05)
    inp["rw_ka"] = 1.0 + nrm((ne, A_WIDTH), 0.05)
    inp["rw_rk"] = nrm((ne, A_HEADS, A_HEAD_DIM), 0.1)
    inp["rw_ln_w"] = 1.0 + nrm((ne, A_WIDTH), 0.02)
    inp["rw_ln_b"] = nrm((ne, A_WIDTH), 0.02)
    inp["gla_gw2"] = nrm((ne, 2, B_LORA, B_WIDTH_K), 0.5 * B_LORA ** -0.5)
    inp["gla_gb"] = uni((ne, 2, B_WIDTH_K), 1.0, 4.0)
    inp["gla_norm"] = 1.0 + nrm((ne, B_DV), 0.02)
    inp["lru_w_in"] = nrm((no, D, 2 * C_WIDTH), D ** -0.5)
    inp["lru_w_out"] = nrm((no, C_WIDTH, D), C_WIDTH ** -0.5)
    inp["lru_conv_w"] = nrm((no, C_CONV, C_WIDTH), C_CONV ** -0.5)
    inp["lru_conv_b"] = nrm((no, C_WIDTH), 0.02)
    inp["lru_wa"] = nrm((no, 2, C_BLOCKS, C_BLOCK, C_BLOCK), C_BLOCK ** -0.5)
    inp["lru_ba"] = nrm((no, 2, C_WIDTH), 0.02)
    inp["lru_wx"] = nrm((no, 2, C_BLOCKS, C_BLOCK, C_BLOCK), C_BLOCK ** -0.5)
    inp["lru_bx"] = nrm((no, 2, C_WIDTH), 0.02)
    a8 = uni((no, 2, C_WIDTH), 0.9, 0.999)
    s = a8 ** (1.0 / C_CONST)
    inp["lru_lam"] = jnp.log(s) - jnp.log1p(-s)
    inp["final_norm"] = 1.0 + nrm((D,), 0.02)
    return inp


def reference(x, c, ctx, c_ctx, mod_w, mod_b, norm1, norm2, mlp_w1, mlp_w2,
              ab_w_in, ab_w_out, rw_mu, rw_w0, rw_w2, rw_a0, rw_a2, rw_g2, rw_kk, rw_ka, rw_rk,
              rw_ln_w, rw_ln_b, gla_gw2, gla_gb, gla_norm,
              lru_w_in, lru_w_out, lru_conv_w, lru_conv_b, lru_wa, lru_ba, lru_wx, lru_bx, lru_lam,
              final_norm):
    dt = x.dtype
    rows = x.shape[1] // GRID_W
    silu_c = jax.nn.silu(c)[:, None, :]
    silu_cc = jax.nn.silu(c_ctx)[None, None, :]
    xl, xc = x, ctx
    for i in range(DEPTH):
        last = i == DEPTH - 1
        j = i // 2
        m_l = jnp.split(silu_c @ mod_w[i] + mod_b[i], N_MOD, axis=-1)
        m_c = jnp.split(silu_cc @ mod_w[i] + mod_b[i], N_MOD, axis=-1)
        hl = rms_norm(xl, norm1[i]) * (1.0 + m_l[1]) + m_l[0]
        hc = rms_norm(xc, norm1[i]) * (1.0 + m_c[1]) + m_c[0]
        if i % 2 == 0:
            pc = hc @ ab_w_in[j]
            pl = hl @ ab_w_in[j]
            ac, al = rwkv7_mixer(pc[..., :A_COLS], pl[..., :A_COLS], rw_mu[j], rw_w0[j], rw_w2[j],
                                 rw_a0[j], rw_a2[j], rw_g2[j], rw_kk[j], rw_ka[j], rw_rk[j],
                                 rw_ln_w[j], rw_ln_b[j])
            bc, bl = gla_mixer(pc[..., A_COLS:], pl[..., A_COLS:], gla_gw2[j], gla_gb[j], gla_norm[j])
            out_l = jnp.concatenate([al, bl], axis=-1).astype(dt) @ ab_w_out[j]
            if not last:
                out_c = jnp.concatenate([ac, bc], axis=-1).astype(dt) @ ab_w_out[j]
        else:
            yc, yl = rglru_mixer(hc, hl, lru_w_in[j], lru_conv_w[j], lru_conv_b[j], lru_wa[j],
                                 lru_ba[j], lru_wx[j], lru_bx[j], lru_lam[j], rows)
            out_l = yl.astype(dt) @ lru_w_out[j]
            if not last:
                out_c = yc.astype(dt) @ lru_w_out[j]
        xl = xl + m_l[2] * out_l
        hl = rms_norm(xl, norm2[i]) * (1.0 + m_l[4]) + m_l[3]
        xl = xl + m_l[5] * sq_relu_mlp(hl, mlp_w1[i], mlp_w2[i])
        if not last:
            xc = xc + m_c[2] * out_c
            hc = rms_norm(xc, norm2[i]) * (1.0 + m_c[4]) + m_c[3]
            xc = xc + m_c[5] * sq_relu_mlp(hc, mlp_w1[i], mlp_w2[i])
    return rms_norm(xl, final_norm)
```

```python
import functools

import jax
import jax.numpy as jnp
from jax import lax
from jax.experimental import pallas as pl
from jax.experimental.pallas import tpu as pltpu

F32 = jnp.float32
BF16 = jnp.bfloat16

NORM_EPS = 1e-6
GRID_W = 64
N_MOD = 6

A_HEAD_DIM = 64
A_LORA_W = 96
A_LORA_A = 96
A_LORA_G = 256
A_LN_EPS = 64e-5
HEADS_PER_GROUP = 4
GROUP_W = HEADS_PER_GROUP * A_HEAD_DIM
CHUNK = 64

B_HEADS = 4
B_LORA = 16
B_GATE_TAU = 16.0

C_BLOCKS = 8
C_CONV = 4
C_CONST = 8.0

LANE = 128
SUBLANE = 8
ROW_TILE = 512
PREP_TILE = 128
VMEM_LIMIT = 56 * 1024 * 1024


def _cparams(sem):
    return pltpu.CompilerParams(dimension_semantics=sem, vmem_limit_bytes=VMEM_LIMIT)


def _bf(x):
    return x.astype(BF16)


def _dot(a, b):
    return jnp.dot(a, b, preferred_element_type=F32)


def _dot_nt(a, b):
    return lax.dot_general(a, b, (((1,), (1,)), ((), ())), preferred_element_type=F32)


def _dot_tn(a, b):
    return lax.dot_general(a, b, (((0,), (0,)), ((), ())), preferred_element_type=F32)


def _split3(x):
    hi = _bf(x)
    r1 = x - hi.astype(F32)
    mid = _bf(r1)
    lo = _bf(r1 - mid.astype(F32))
    return hi, mid, lo


def _dot_exact_lhs(m_bf, x):
    hi, mid, lo = _split3(x)
    return _dot(m_bf, hi) + _dot(m_bf, mid) + _dot(m_bf, lo)


def _dot_exact_rhs(x, m_bf):
    hi, mid, lo = _split3(x)
    return _dot(hi, m_bf) + _dot(mid, m_bf) + _dot(lo, m_bf)


def _softplus(x):
    return jnp.maximum(x, 0.0) + jnp.log1p(jnp.exp(-jnp.abs(x)))


def _sigmoid(x):
    return jax.nn.sigmoid(x)


def _silu(x):
    return x * _sigmoid(x)


def _norm_mod(x, g, shift, scale):
    ms = jnp.mean(x * x, axis=-1, keepdims=True)
    return (x * lax.rsqrt(ms + NORM_EPS) * g) * (1.0 + scale) + shift


def _mods_kernel(c_ref, w_ref, b_ref, o_ref):
    s = _silu(c_ref[...])
    o_ref[...] = _dot(_bf(s), _bf(w_ref[...])) + b_ref[...]


def _mods(cvec, mod_w, mod_b):
    depth, d, n = mod_w.shape
    tn = 1024
    out = pl.pallas_call(
        _mods_kernel,
        grid=(depth, n // tn),
        in_specs=[
            pl.BlockSpec((SUBLANE, d), lambda l, j: (0, 0)),
            pl.BlockSpec((None, d, tn), lambda l, j: (l, 0, j)),
            pl.BlockSpec((None, 1, tn), lambda l, j: (l, 0, j)),
        ],
        out_specs=pl.BlockSpec((None, SUBLANE, tn), lambda l, j: (l, 0, j)),
        out_shape=jax.ShapeDtypeStruct((depth, SUBLANE, n), F32),
        compiler_params=_cparams(("parallel", "parallel")),
    )(cvec, mod_w, mod_b.reshape(depth, 1, n))
    return out.reshape(depth, SUBLANE, N_MOD, d)


def _norm_matmul_kernel(x_ref, g_ref, mod_ref, w_ref, o_ref, h_ref, *, shift_row, scale_row):
    @pl.when(pl.program_id(1) == 0)
    def _():
        h = _norm_mod(x_ref[...], g_ref[...], mod_ref[shift_row:shift_row + 1, :],
                      mod_ref[scale_row:scale_row + 1, :])
        h_ref[...] = _bf(h)

    o_ref[...] = _dot(h_ref[...], w_ref[...])


def _norm_matmul(x, g, mods_l, w_bf, seg_of_tile, *, tn, shift_row=0, scale_row=1):
    r, d = x.shape
    n = w_bf.shape[1]
    tm = ROW_TILE
    return pl.pallas_call(
        functools.partial(_norm_matmul_kernel, shift_row=shift_row, scale_row=scale_row),
        grid=(r // tm, n // tn),
        in_specs=[
            pl.BlockSpec((tm, d), lambda i, j: (i, 0)),
            pl.BlockSpec((1, d), lambda i, j: (0, 0)),
            pl.BlockSpec((None, N_MOD, d), lambda i, j: (seg_of_tile(i), 0, 0)),
            pl.BlockSpec((d, tn), lambda i, j: (0, j)),
        ],
        out_specs=pl.BlockSpec((tm, tn), lambda i, j: (i, j)),
        out_shape=jax.ShapeDtypeStruct((r, n), F32),
        scratch_shapes=[pltpu.VMEM((tm, d), BF16)],
        compiler_params=_cparams(("parallel", "arbitrary")),
    )(x, g.reshape(1, d), mods_l, w_bf)


def _mlp_kernel(x_ref, g_ref, mod_ref, w1_ref, w2_ref, fn_ref, o_ref, h_ref, acc_ref, *, final_norm):
    j = pl.program_id(1)

    @pl.when(j == 0)
    def _():
        h = _norm_mod(x_ref[...], g_ref[...], mod_ref[3:4, :], mod_ref[4:5, :])
        h_ref[...] = _bf(h)
        acc_ref[...] = jnp.zeros_like(acc_ref)

    a = _dot(h_ref[...], w1_ref[...])
    a = jnp.square(jnp.maximum(a, 0.0))
    acc_ref[...] += _dot(_bf(a), w2_ref[...])

    @pl.when(j == pl.num_programs(1) - 1)
    def _():
        xo = x_ref[...] + mod_ref[5:6, :] * acc_ref[...]
        if final_norm:
            ms = jnp.mean(xo * xo, axis=-1, keepdims=True)
            xo = xo * lax.rsqrt(ms + NORM_EPS) * fn_ref[...]
        o_ref[...] = xo


def _mlp(x, rows_out, g, mods_l, w1_bf, w2_bf, fn, seg_of_tile, *, final_norm):
    d = x.shape[1]
    dff = w1_bf.shape[1]
    tm, tf = ROW_TILE, 512
    return pl.pallas_call(
        functools.partial(_mlp_kernel, final_norm=final_norm),
        grid=(rows_out // tm, dff // tf),
        in_specs=[
            pl.BlockSpec((tm, d), lambda i, j: (i, 0)),
            pl.BlockSpec((1, d), lambda i, j: (0, 0)),
            pl.BlockSpec((None, N_MOD, d), lambda i, j: (seg_of_tile(i), 0, 0)),
            pl.BlockSpec((d, tf), lambda i, j: (0, j)),
            pl.BlockSpec((tf, d), lambda i, j: (j, 0)),
            pl.BlockSpec((1, d), lambda i, j: (0, 0)),
        ],
        out_specs=pl.BlockSpec((tm, d), lambda i, j: (i, 0)),
        out_shape=jax.ShapeDtypeStruct((rows_out, d), F32),
        scratch_shapes=[pltpu.VMEM((tm, d), BF16), pltpu.VMEM((tm, d), F32)],
        compiler_params=_cparams(("parallel", "arbitrary")),
    )(x, g.reshape(1, d), mods_l, w1_bf, w2_bf, fn.reshape(1, d))


def _proj_res_kernel(y_ref, x_ref, mod_ref, w_ref, o_ref):
    o_ref[...] = x_ref[...] + mod_ref[2:3, :] * _dot(_bf(y_ref[...]), w_ref[...])


def _proj_res(y, x, rows_out, mods_l, w_bf, seg_of_tile):
    k = y.shape[1]
    d = w_bf.shape[1]
    tm = ROW_TILE
    return pl.pallas_call(
        _proj_res_kernel,
        grid=(rows_out // tm,),
        in_specs=[
            pl.BlockSpec((tm, k), lambda i: (i, 0)),
            pl.BlockSpec((tm, d), lambda i: (i, 0)),
            pl.BlockSpec((None, N_MOD, d), lambda i: (seg_of_tile(i), 0, 0)),
            pl.BlockSpec((k, d), lambda i: (0, 0)),
        ],
        out_specs=pl.BlockSpec((tm, d), lambda i: (i, 0)),
        out_shape=jax.ShapeDtypeStruct((rows_out, d), F32),
        compiler_params=_cparams(("parallel",)),
    )(y, x, mods_l, w_bf)


def _head_sum(z, e_bf):
    parts = []
    for gi in range(z.shape[1] // GROUP_W):
        zs = z[:, gi * GROUP_W:(gi + 1) * GROUP_W]
        parts.append(_dot_exact_rhs(zs, e_bf))
    return jnp.concatenate(parts, axis=1)


def _rwkv_prep_kernel(p_ref, pp_ref, pn_ref, mu_ref, w0_ref, a0_ref, w2_ref, a2_ref, g2_ref,
                      kk_ref, ka_ref, rk_ref, e_ref,
                      r_o, v_o, kk_o, lw0_o, lw1_o, kd0_o, kd1_o, b0_o, b1_o, g_o, bonus_o,
                      *, n_lat_tiles, lat_tiles_per_seq, ctx_tiles_per_seq, aw):
    i = pl.program_id(0)
    tp = p_ref.shape[0]
    in_ctx = i >= n_lat_tiles
    pos = jnp.where(in_ctx, (i - n_lat_tiles) % ctx_tiles_per_seq, i % lat_tiles_per_seq)
    last = jnp.where(in_ctx, ctx_tiles_per_seq - 1, lat_tiles_per_seq - 1)
    is_first = pos == 0
    is_last = pos == last
    rowid = lax.broadcasted_iota(jnp.int32, (tp, 1), 0)

    def shifted(c0, c1):
        x = p_ref[:, c0:c1]
        prow = jnp.where(is_first, 0.0, pp_ref[SUBLANE - 1:SUBLANE, c0:c1])
        nrow = jnp.where(is_last, 0.0, pn_ref[0:1, c0:c1])
        xp = jnp.where(rowid == 0, prow, pltpu.roll(x, 1, 0))
        xn = jnp.where(rowid == tp - 1, nrow, pltpu.roll(x, tp - 1, 0))
        return x + mu_ref[0:1, c0:c1] * (xp - x) + mu_ref[1:2, c0:c1] * (xn - x)

    e_bf = e_ref[...]
    r = shifted(0, aw)
    k = shifted(aw, 2 * aw)
    v = shifted(2 * aw, 3 * aw)
    gd = shifted(3 * aw, 3 * aw + A_LORA_G)
    lo0 = 3 * aw + A_LORA_G
    xl = shifted(lo0, lo0 + 2 * A_LORA_W + 2 * A_LORA_A)

    r_o[...] = r
    v_o[...] = v
    g_o[...] = _dot(_bf(_sigmoid(gd)), g2_ref[...])

    kx = k * kk_ref[...]
    kn = jnp.sqrt(_head_sum(kx * kx, e_bf))
    kk = kx / jnp.maximum(kn, 1e-12)
    kk_o[...] = kk

    xl_t = _bf(jnp.tanh(xl))
    xl_b = _bf(xl)
    kd_sum = None
    for d, (lw_o, kd_o, b_o) in enumerate(((lw0_o, kd0_o, b0_o), (lw1_o, kd1_o, b1_o))):
        w_log = -_softplus(-(w0_ref[d:d + 1, :] + _dot(xl_t, w2_ref[d]))) - 0.5
        lw_o[...] = -jnp.exp(w_log)
        asig = _sigmoid(a0_ref[d:d + 1, :] + _dot(xl_b, a2_ref[d]))
        kd = k * (1.0 + (asig - 1.0) * ka_ref[...])
        kd_o[...] = kd
        b_o[...] = kk * asig
        kd_sum = kd if kd_sum is None else kd_sum + kd
    bonus_o[...] = _head_sum(r * rk_ref[...] * kd_sum, e_bf) * v


def _rwkv_prep(pa, mu_p, w0, a0, w2p, a2p, g2, k_k, k_a, r_k, e_bf, *, n_lat, seq, ctx_len):
    r, ncol = pa.shape
    aw = w0.shape[1]
    tp = PREP_TILE
    hb = tp // SUBLANE
    nblk8 = r // SUBLANE
    lw = 2 * A_LORA_W + 2 * A_LORA_A
    row = lambda a: a.reshape(1, aw)
    full = lambda shape: pl.BlockSpec(shape, lambda i: (0,) * len(shape))
    out_spec = pl.BlockSpec((tp, aw), lambda i: (i, 0))
    out_sds = jax.ShapeDtypeStruct((r, aw), F32)
    return pl.pallas_call(
        functools.partial(_rwkv_prep_kernel, n_lat_tiles=n_lat // tp, lat_tiles_per_seq=seq // tp,
                          ctx_tiles_per_seq=ctx_len // tp, aw=aw),
        grid=(r // tp,),
        in_specs=[
            pl.BlockSpec((tp, ncol), lambda i: (i, 0)),
            pl.BlockSpec((SUBLANE, ncol), lambda i: (jnp.maximum(i * hb - 1, 0), 0)),
            pl.BlockSpec((SUBLANE, ncol), lambda i: (jnp.minimum((i + 1) * hb, nblk8 - 1), 0)),
            full((2, ncol)), full((2, aw)), full((2, aw)),
            full((2, lw, aw)), full((2, lw, aw)), full((A_LORA_G, aw)),
            full((1, aw)), full((1, aw)), full((1, aw)), full((GROUP_W, GROUP_W)),
        ],
        out_specs=[out_spec] * 11,
        out_shape=[out_sds] * 11,
        compiler_params=_cparams(("parallel",)),
    )(pa, pa, pa, mu_p, w0, a0, w2p, a2p, g2, row(k_k), row(k_a), row(r_k), e_bf)


def _rwkv_chunk(r, v, kk, lw, kd, b, h_prev, *, reverse):
    L = CHUNK
    n_rep = HEADS_PER_GROUP
    gw = GROUP_W
    row = lax.broadcasted_iota(jnp.int32, (gw, gw), 0)
    col = lax.broadcasted_iota(jnp.int32, (gw, gw), 1)
    same_head = (row >> 6) == (col >> 6)
    tr = row & (L - 1)
    tc = col & (L - 1)
    if reverse:
        strict = same_head & (tr < tc)
        incl = same_head & (tr <= tc)
    else:
        strict = same_head & (tr > tc)
        incl = same_head & (tr >= tc)
    eye = row == col

    t_r = lax.broadcasted_iota(jnp.int32, (L, L), 0)
    t_c = lax.broadcasted_iota(jnp.int32, (L, L), 1)
    tri = _bf(jnp.where((t_c >= t_r) if reverse else (t_c <= t_r), 1.0, 0.0))

    c = _dot_exact_lhs(tri, lw)
    last = 0 if reverse else L - 1
    c_l = c[last:last + 1, :]
    e_c = jnp.exp(c)
    e_cp = jnp.exp(c - lw)
    e_nc = jnp.exp(-c)
    e_end = jnp.exp(c_l - c)
    e_l = jnp.exp(c_l)

    rep = lambda x: jnp.concatenate([x] * n_rep, axis=0)
    bd = lambda x: jnp.where(same_head, rep(x), 0.0)
    nat = lambda x: x[0:L] + x[L:2 * L] + x[2 * L:3 * L] + x[3 * L:4 * L]

    at_bd = bd(-kk * e_cp)
    rt_bd = bd(r * e_c)
    v_bd = bd(v)
    bend_bd = _bf(bd(b * e_end))
    kend_bd = _bf(bd(kd * e_end))
    v_bf = _bf(v_bd)

    lhs = _bf(jnp.concatenate([at_bd, rt_bd], axis=0))
    rhs = _bf(jnp.concatenate([rep(b * e_nc), rep(kd * e_nc)], axis=0))
    s = _dot_nt(lhs, rhs)
    a_ab = jnp.where(strict, s[:gw, :gw], 0.0)
    a_ak = jnp.where(strict, s[:gw, gw:], 0.0)
    r_b = _bf(jnp.where(incl, s[gw:, :gw], 0.0))
    r_k = _bf(jnp.where(incl, s[gw:, gw:], 0.0))

    akv = _dot(_bf(a_ak), v_bf)
    t = jnp.where(eye, 1.0, 0.0) + a_ab
    a_pow = a_ab
    for _ in range(5):
        a_pow_bf = _bf(a_pow)
        a_pow = _dot(a_pow_bf, a_pow_bf)
        t = t + _dot(_bf(t), _bf(a_pow))
    pq = _dot(_bf(t), _bf(jnp.concatenate([at_bd, akv], axis=1)))
    p_bf = _bf(pq[:, :gw])
    q_bf = _bf(pq[:, gw:])

    m = jnp.where(eye, e_l, 0.0) + _dot_tn(bend_bd, p_bf)
    n = _dot_tn(bend_bd, q_bf) + _dot_tn(kend_bd, v_bf)
    g = rt_bd + _dot(r_b, p_bf)
    y0 = _dot(r_b, q_bf) + _dot(r_k, v_bf)

    h_bf = _bf(h_prev)
    y = _dot(_bf(nat(g)), h_bf) + nat(y0)
    h_new = _dot(_bf(m), h_bf) + n
    return y, h_new


def _rwkv_scan_kernel(rf, vf, kkf, lwf, kdf, bf_, rb, vb, kkb, lwb, kdb, bb,
                      yf_ref, yb_ref, hf_ref, hb_ref):
    @pl.when(pl.program_id(1) == 0)
    def _():
        hf_ref[...] = jnp.zeros_like(hf_ref)
        hb_ref[...] = jnp.zeros_like(hb_ref)

    y, h = _rwkv_chunk(rf[...], vf[...], kkf[...], lwf[...], kdf[...], bf_[...], hf_ref[...],
                       reverse=False)
    yf_ref[...] = y
    hf_ref[...] = h
    y, h = _rwkv_chunk(rb[...], vb[...], kkb[...], lwb[...], kdb[...], bb[...], hb_ref[...],
                       reverse=True)
    yb_ref[...] = y
    hb_ref[...] = h


def _chunk_row_block(b, i, *, reverse, n_lat, seq, ctx_len):
    nc = ctx_len // CHUNK
    nl = seq // CHUNK
    ci = (nc - 1 - i) if reverse else i
    li = (nl - 1 - (i - nc)) if reverse else (i - nc)
    return jnp.where(i < nc, (n_lat + b * ctx_len) // CHUNK + ci, b * nl + li)


def _rwkv_scan(r, v, kk, lw0, lw1, kd0, kd1, b0, b1, *, batch, n_lat, seq, ctx_len):
    rows, aw = r.shape
    ng = aw // GROUP_W
    geo = dict(n_lat=n_lat, seq=seq, ctx_len=ctx_len)

    def spec(reverse):
        return pl.BlockSpec(
            (CHUNK, GROUP_W),
            lambda pg, i: (_chunk_row_block(pg // ng, i, reverse=reverse, **geo), pg % ng))

    sf, sb = spec(False), spec(True)
    sds = jax.ShapeDtypeStruct((rows, aw), F32)
    return pl.pallas_call(
        _rwkv_scan_kernel,
        grid=(batch * ng, (seq + ctx_len) // CHUNK),
        in_specs=[sf] * 6 + [sb] * 6,
        out_specs=[sf, sb],
        out_shape=[sds, sds],
        scratch_shapes=[pltpu.VMEM((GROUP_W, GROUP_W), F32), pltpu.VMEM((GROUP_W, GROUP_W), F32)],
        compiler_params=_cparams(("parallel", "arbitrary")),
    )(r, v, kk, lw0, kd0, b0, r, v, kk, lw1, kd1, b1)


def _gla_chunk(qk, v, gd, gw2, gb, st_ref, *, reverse, dk, dv):
    L = CHUNK
    wk = B_HEADS * dk
    t_r = lax.broadcasted_iota(jnp.int32, (L, L), 0)
    t_c = lax.broadcasted_iota(jnp.int32, (L, L), 1)
    keep = (t_c >= t_r) if reverse else (t_c <= t_r)
    tri = _bf(jnp.where(keep, 1.0, 0.0))

    z = _dot(_bf(gd), gw2) + gb
    la = -_softplus(-z) * (1.0 / B_GATE_TAU)
    cum = _dot_exact_lhs(tri, la)
    last = 0 if reverse else L - 1
    cl = cum[last:last + 1, :]
    q_in = qk[:, :wk] * (dk ** -0.5) * jnp.exp(cum)
    k_in = qk[:, wk:] * jnp.exp(-cum)
    k_end = qk[:, wk:] * jnp.exp(cl - cum)
    dec = jnp.exp(cl)

    outs = []
    for h in range(B_HEADS):
        ks = slice(h * dk, (h + 1) * dk)
        vh = _bf(v[:, h * dv:(h + 1) * dv])
        qh = _bf(q_in[:, ks])
        sc = jnp.where(keep, _dot_nt(qh, _bf(k_in[:, ks])), 0.0)
        st = st_ref[h]
        outs.append(_dot(_bf(sc), vh) + _dot_nt(qh, _bf(st)))
        st_ref[h] = st * dec[:, ks] + _dot_tn(vh, _bf(k_end[:, ks]))
    return jnp.concatenate(outs, axis=1)


def _gla_scan_kernel(qkf, vf, gdf, qkb, vb, gdb, gw2_ref, gb_ref, of_ref, ob_ref, sf_ref, sb_ref,
                     *, dk, dv):
    @pl.when(pl.program_id(1) == 0)
    def _():
        sf_ref[...] = jnp.zeros_like(sf_ref)
        sb_ref[...] = jnp.zeros_like(sb_ref)

    of_ref[...] = _gla_chunk(qkf[...], vf[...], gdf[...], gw2_ref[0], gb_ref[0:1, :], sf_ref,
                             reverse=False, dk=dk, dv=dv)
    ob_ref[...] = _gla_chunk(qkb[...], vb[...], gdb[...], gw2_ref[1], gb_ref[1:2, :], sb_ref,
                             reverse=True, dk=dk, dv=dv)


def _gla_scan(pb, gw2p, gb, *, batch, n_lat, seq, ctx_len, wk, wv):
    rows = pb.shape[0]
    dk, dv = wk // B_HEADS, wv // B_HEADS
    geo = dict(n_lat=n_lat, seq=seq, ctx_len=ctx_len)
    gd_blk = (2 * wk + 2 * wv) // LANE

    def specs(reverse):
        rb = lambda b, i: _chunk_row_block(b, i, reverse=reverse, **geo)
        return [
            pl.BlockSpec((CHUNK, 2 * wk), lambda b, i: (rb(b, i), 0)),
            pl.BlockSpec((CHUNK, wv), lambda b, i: (rb(b, i), (2 * wk) // wv)),
            pl.BlockSpec((CHUNK, LANE), lambda b, i: (rb(b, i), gd_blk)),
        ]

    def out_spec(reverse):
        return pl.BlockSpec((CHUNK, wv),
                            lambda b, i: (_chunk_row_block(b, i, reverse=reverse, **geo), 0))

    sds = jax.ShapeDtypeStruct((rows, wv), F32)
    return pl.pallas_call(
        functools.partial(_gla_scan_kernel, dk=dk, dv=dv),
        grid=(batch, (seq + ctx_len) // CHUNK),
        in_specs=specs(False) + specs(True) + [
            pl.BlockSpec((2, LANE, wk), lambda b, i: (0, 0, 0)),
            pl.BlockSpec((2, wk), lambda b, i: (0, 0)),
        ],
        out_specs=[out_spec(False), out_spec(True)],
        out_shape=[sds, sds],
        scratch_shapes=[pltpu.VMEM((B_HEADS, dv, dk), F32), pltpu.VMEM((B_HEADS, dv, dk), F32)],
        compiler_params=_cparams(("parallel", "arbitrary")),
    )(pb, pb, pb, pb, pb, pb, gw2p, gb)


def _mixer_out_kernel(yf, yb, bonus, g, of, ob, gate, x_ref, mod_ref, lnw, lnb, gn, e_ref, w_ref,
                      o_ref, h_ref, *, aw, dv):
    e_bf = e_ref[...]
    y = yf[...] + yb[...]
    inv = 1.0 / A_HEAD_DIM
    mean = _head_sum(y, e_bf) * inv
    dlt = y - mean
    var = _head_sum(dlt * dlt, e_bf) * inv
    yn = dlt * lax.rsqrt(var + A_LN_EPS) * lnw[...] + lnb[...]
    h_ref[:, :aw] = _bf((yn + bonus[...]) * g[...])

    o = of[...] + ob[...]
    gt = _silu(gate[...])
    for h in range(B_HEADS):
        sl = slice(h * dv, (h + 1) * dv)
        oh = o[:, sl]
        ms = jnp.mean(oh * oh, axis=-1, keepdims=True)
        h_ref[:, aw + h * dv:aw + (h + 1) * dv] = _bf(oh * lax.rsqrt(ms + NORM_EPS) * gn[...] * gt[:, sl])

    o_ref[...] = x_ref[...] + mod_ref[2:3, :] * _dot(h_ref[...], w_ref[...])


def _mixer_out(yf, yb, bonus, g, of, ob, pb, x, mods_l, ln_w, ln_b, gla_norm, e_bf, w_bf,
               seg_of_tile, *, wk):
    rows, d = x.shape
    aw = yf.shape[1]
    wv = of.shape[1]
    dv = wv // B_HEADS
    tm = 256
    tiles_per_row_tile = ROW_TILE // tm
    rs = lambda w: pl.BlockSpec((tm, w), lambda i: (i, 0))
    full = lambda shape: pl.BlockSpec(shape, lambda i: (0,) * len(shape))
    return pl.pallas_call(
        functools.partial(_mixer_out_kernel, aw=aw, dv=dv),
        grid=(rows // tm,),
        in_specs=[
            rs(aw), rs(aw), rs(aw), rs(aw), rs(wv), rs(wv),
            pl.BlockSpec((tm, wv), lambda i: (i, (2 * wk + wv) // wv)),
            rs(d),
            pl.BlockSpec((None, N_MOD, d), lambda i: (seg_of_tile(i // tiles_per_row_tile), 0, 0)),
            full((1, aw)), full((1, aw)), full((1, dv)), full((GROUP_W, GROUP_W)),
            full((aw + wv, d)),
        ],
        out_specs=rs(d),
        out_shape=jax.ShapeDtypeStruct((rows, d), F32),
        scratch_shapes=[pltpu.VMEM((tm, aw + wv), BF16)],
        compiler_params=_cparams(("parallel",)),
    )(yf, yb, bonus, g, of, ob, pb, x, mods_l, ln_w.reshape(1, aw), ln_b.reshape(1, aw),
      gla_norm.reshape(1, dv), e_bf, w_bf)


def _lru_kernel(*refs, reverse, n_blk, combine):
    if combine:
        (x_ref, xp_ref, xn_ref, h0_ref, cw_ref, cb_ref, wc_ref, ba_ref, bx_ref, lam_ref,
         hf_ref, gate_ref, out_ref, hl_ref, a_sc, u_sc, hs_sc, carry_sc) = refs
    else:
        (x_ref, xp_ref, xn_ref, h0_ref, cw_ref, cb_ref, wc_ref, ba_ref, bx_ref, lam_ref,
         out_ref, hl_ref, a_sc, u_sc, hs_sc, carry_sc) = refs
    j = pl.program_id(2)
    blk = (n_blk - 1 - j) if reverse else j
    lb, ct = x_ref.shape
    cblk = wc_ref.shape[1]
    is_first = blk == 0
    is_last = blk == n_blk - 1
    rowid = lax.broadcasted_iota(jnp.int32, (lb, 1), 0)

    x = x_ref[...]
    p6 = jnp.where(is_first, 0.0, xp_ref[SUBLANE - 2:SUBLANE - 1, :])
    p7 = jnp.where(is_first, 0.0, xp_ref[SUBLANE - 1:SUBLANE, :])
    n0 = jnp.where(is_last, 0.0, xn_ref[0:1, :])
    xm1 = jnp.where(rowid == 0, p7, pltpu.roll(x, 1, 0))
    xm2 = jnp.where(rowid == 0, p6, jnp.where(rowid == 1, p7, pltpu.roll(x, 2, 0)))
    xp1 = jnp.where(rowid == lb - 1, n0, pltpu.roll(x, lb - 1, 0))
    xs = (cb_ref[...] + cw_ref[0:1, :] * xm2 + cw_ref[1:2, :] * xm1 + cw_ref[2:3, :] * x
          + cw_ref[3:4, :] * xp1)

    for n in range(ct // cblk):
        sl = slice(n * cblk, (n + 1) * cblk)
        xb = xs[:, sl]
        ri = _dot(_bf(xb), wc_ref[n])
        rg = _sigmoid(ri[:, :cblk] + ba_ref[:, sl])
        ig = _sigmoid(ri[:, cblk:] + bx_ref[:, sl])
        log_a = -C_CONST * rg * _softplus(-lam_ref[:, sl])
        a = jnp.exp(log_a)
        a_sc[:, sl] = a
        u_sc[:, sl] = jnp.sqrt(1.0 - a * a) * (ig * xb)

    @pl.when(j == 0)
    def _():
        carry_sc[...] = h0_ref[0:1, :]

    def step(t, h):
        tt = (lb - 1 - t) if reverse else t
        h = a_sc[pl.ds(tt, 1), :] * h + u_sc[pl.ds(tt, 1), :]
        hs_sc[pl.ds(tt, 1), :] = h
        return h

    h = lax.fori_loop(0, lb, step, carry_sc[...], unroll=8)
    carry_sc[...] = h
    hl_ref[...] = jnp.broadcast_to(h, hl_ref.shape)
    if combine:
        out_ref[...] = (hs_sc[...] + hf_ref[...]) * jax.nn.gelu(gate_ref[...])
    else:
        out_ref[...] = hs_sc[...]


def _lru_pass(x_arr, x_specs, h0, conv_w, conv_b, wcat, ba, bx, lam, out_shape, out_spec, extra,
              *, batch, n_blk, lb, reverse):
    c = conv_w.shape[1]
    ct = 1024
    nct = c // ct
    cblk = wcat.shape[1]
    chan = lambda rows_: pl.BlockSpec((rows_, ct), lambda b, k, j: (0, k))
    in_specs = list(x_specs) + [
        pl.BlockSpec((SUBLANE, ct), lambda b, k, j: (b, k)),
        chan(C_CONV), chan(1),
        pl.BlockSpec((ct // cblk, cblk, 2 * cblk), lambda b, k, j: (k, 0, 0)),
        chan(1), chan(1), chan(1),
    ] + [s for _, s in extra]
    args = [x_arr, x_arr, x_arr, h0, conv_w, conv_b.reshape(1, c), wcat, ba.reshape(1, c),
            bx.reshape(1, c), lam.reshape(1, c)] + [a for a, _ in extra]
    return pl.pallas_call(
        functools.partial(_lru_kernel, reverse=reverse, n_blk=n_blk, combine=bool(extra)),
        grid=(batch, nct, n_blk),
        in_specs=in_specs,
        out_specs=[out_spec, pl.BlockSpec((SUBLANE, ct), lambda b, k, j: (b, k))],
        out_shape=[out_shape, jax.ShapeDtypeStruct((batch * SUBLANE, c), F32)],
        scratch_shapes=[pltpu.VMEM((lb, ct), F32), pltpu.VMEM((lb, ct), F32),
                        pltpu.VMEM((lb, ct), F32), pltpu.VMEM((1, ct), F32)],
        compiler_params=_cparams(("parallel", "parallel", "arbitrary")),
    )(*args)


def _rglru(hw, conv_w, conv_b, wa, ba, wx, bx, lam, *, batch, n_lat, seq, ctx_len):
    rows_total, c2 = hw.shape
    c = c2 // 2
    ct = 1024
    rows = seq // GRID_W
    lbc = 128
    ncb = ctx_len // lbc
    hw_grid = hw.reshape(rows_total // GRID_W, GRID_W * c2)
    xcol = c // ct
    per_col = c2 // ct

    wcat = [_bf(jnp.concatenate([wa[d], wx[d]], axis=-1)) for d in range(2)]
    zeros_state = jnp.zeros((batch * SUBLANE, c), F32)

    def ctx_specs(reverse):
        blk = (lambda j: ncb - 1 - j) if reverse else (lambda j: j)
        base = n_lat // lbc
        base8 = n_lat // SUBLANE
        per8 = lbc // SUBLANE
        cur = pl.BlockSpec((lbc, ct), lambda b, k, j: (base + b * ncb + blk(j), xcol + k))
        prv = pl.BlockSpec((SUBLANE, ct), lambda b, k, j: (
            base8 + b * ncb * per8 + jnp.maximum(blk(j) * per8 - 1, 0), xcol + k))
        nxt = pl.BlockSpec((SUBLANE, ct), lambda b, k, j: (
            base8 + b * ncb * per8 + jnp.minimum((blk(j) + 1) * per8, ncb * per8 - 1), xcol + k))
        return cur, prv, nxt

    def lat_specs(reverse):
        blk = (lambda j: GRID_W - 1 - j) if reverse else (lambda j: j)
        r8 = rows // SUBLANE
        cur = pl.BlockSpec((rows, ct), lambda b, k, j: (b, blk(j) * per_col + xcol + k))
        prv = pl.BlockSpec((SUBLANE, ct), lambda b, k, j: (
            b * r8 + r8 - 1, jnp.maximum(blk(j) - 1, 0) * per_col + xcol + k))
        nxt = pl.BlockSpec((SUBLANE, ct), lambda b, k, j: (
            b * r8, jnp.minimum(blk(j) + 1, GRID_W - 1) * per_col + xcol + k))
        return cur, prv, nxt

    common = dict(batch=batch)
    ctx_scratch_sds = jax.ShapeDtypeStruct((batch * ctx_len, c), F32)
    ctx_out = lambda reverse: pl.BlockSpec(
        (lbc, ct), lambda b, k, j: (b * ncb + ((ncb - 1 - j) if reverse else j), k))
    _, st = _lru_pass(hw, ctx_specs(False), zeros_state, conv_w, conv_b, wcat[0], ba[0], bx[0],
                      lam[0], ctx_scratch_sds, ctx_out(False), [], n_blk=ncb, lb=lbc,
                      reverse=False, **common)
    hf_sds = jax.ShapeDtypeStruct((n_lat, c), F32)
    hf_spec = lambda reverse: pl.BlockSpec(
        (rows, ct), lambda b, k, j: (b * GRID_W + ((GRID_W - 1 - j) if reverse else j), k))
    hf, _ = _lru_pass(hw_grid, lat_specs(False), st, conv_w, conv_b, wcat[0], ba[0], bx[0], lam[0],
                      hf_sds, hf_spec(False), [], n_blk=GRID_W, lb=rows, reverse=False, **common)
    _, st = _lru_pass(hw, ctx_specs(True), zeros_state, conv_w, conv_b, wcat[1], ba[1], bx[1],
                      lam[1], ctx_scratch_sds, ctx_out(True), [], n_blk=ncb, lb=lbc,
                      reverse=True, **common)
    gate_spec = pl.BlockSpec((rows, ct), lambda b, k, j: (b, (GRID_W - 1 - j) * per_col + k))
    y_sds = jax.ShapeDtypeStruct((n_lat // GRID_W, GRID_W * c), F32)
    y_spec = pl.BlockSpec((rows, ct), lambda b, k, j: (b, (GRID_W - 1 - j) * (c // ct) + k))
    y, _ = _lru_pass(hw_grid, lat_specs(True), st, conv_w, conv_b, wcat[1], ba[1], bx[1], lam[1],
                     y_sds, y_spec, [(hf, hf_spec(True)), (hw_grid, gate_spec)],
                     n_blk=GRID_W, lb=rows, reverse=True, **common)
    return y.reshape(n_lat, c)


def _pad_cols(w, n):
    return jnp.pad(w, ((0, 0),) * (w.ndim - 1) + ((0, n - w.shape[-1]),))


def kernel(x, c, ctx, c_ctx, mod_w, mod_b, norm1, norm2, mlp_w1, mlp_w2, ab_w_in, ab_w_out, rw_mu,
           rw_w0, rw_w2, rw_a0, rw_a2, rw_g2, rw_kk, rw_ka, rw_rk, rw_ln_w, rw_ln_b, gla_gw2,
           gla_gb, gla_norm, lru_w_in, lru_w_out, lru_conv_w, lru_conv_b, lru_wa, lru_ba, lru_wx,
           lru_bx, lru_lam, final_norm):
    batch, seq, d = x.shape
    ctx_len = ctx.shape[1]
    depth = mod_w.shape[0]
    n_lat = batch * seq
    aw = rw_w0.shape[-1]
    wk = gla_gb.shape[-1]
    wv = ab_w_out.shape[1] - aw
    a_cols = rw_mu.shape[-1]
    b_cols = ab_w_in.shape[-1] - a_cols
    assert depth == 2 and batch + 1 <= SUBLANE
    assert seq % ROW_TILE == 0 and (batch * ctx_len) % ROW_TILE == 0
    assert ctx_len % PREP_TILE == 0 and ctx_len % 128 == 0 and seq % (GRID_W * SUBLANE) == 0
    assert aw % GROUP_W == 0 and wv == aw and 2 * wk == wv

    tiles_per_seq = seq // ROW_TILE
    seg_of_tile = lambda i: jnp.minimum(i // tiles_per_seq, batch)

    xs = jnp.concatenate([x.reshape(n_lat, d), ctx.reshape(batch * ctx_len, d)], axis=0)
    cvec = jnp.concatenate(
        [c, c_ctx[None, :], jnp.zeros((SUBLANE - batch - 1, d), F32)], axis=0)
    mods = _mods(cvec, mod_w, mod_b)

    hid = jnp.arange(GROUP_W) // A_HEAD_DIM
    e_bf = _bf(hid[:, None] == hid[None, :])

    a_pad = 3840
    b_pad = 3200
    w_a = _bf(_pad_cols(ab_w_in[0][:, :a_cols], a_pad))
    w_b = _bf(_pad_cols(ab_w_in[0][:, a_cols:], b_pad))
    pa = _norm_matmul(xs, norm1[0], mods[0], w_a, seg_of_tile, tn=768)
    pb = _norm_matmul(xs, norm1[0], mods[0], w_b, seg_of_tile, tn=640)

    lora_w = 2 * A_LORA_W + 2 * A_LORA_A
    w2p = jnp.zeros((2, lora_w, aw), F32)
    a2p = jnp.zeros((2, lora_w, aw), F32)
    for dd in range(2):
        w2p = w2p.at[dd, dd * A_LORA_W:(dd + 1) * A_LORA_W].set(rw_w2[0, dd])
        a2p = a2p.at[dd, 2 * A_LORA_W + dd * A_LORA_A:2 * A_LORA_W + (dd + 1) * A_LORA_A].set(
            rw_a2[0, dd])
    (r_, v_, kk_, lw0, lw1, kd0, kd1, b0, b1, g_, bonus) = _rwkv_prep(
        pa, _pad_cols(rw_mu[0], a_pad), rw_w0[0], rw_a0[0], _bf(w2p), _bf(a2p), _bf(rw_g2[0]),
        rw_kk[0], rw_ka[0], rw_rk[0].reshape(-1), e_bf, n_lat=n_lat, seq=seq, ctx_len=ctx_len)
    yf, yb = _rwkv_scan(r_, v_, kk_, lw0, lw1, kd0, kd1, b0, b1, batch=batch, n_lat=n_lat, seq=seq,
                        ctx_len=ctx_len)

    gw2p = jnp.zeros((2, LANE, wk), F32)
    for dd in range(2):
        gw2p = gw2p.at[dd, dd * B_LORA:(dd + 1) * B_LORA].set(gla_gw2[0, dd])
    of, ob = _gla_scan(pb, _bf(gw2p), gla_gb[0], batch=batch, n_lat=n_lat, seq=seq,
                       ctx_len=ctx_len, wk=wk, wv=wv)

    xs = _mixer_out(yf, yb, bonus, g_, of, ob, pb, xs, mods[0], rw_ln_w[0], rw_ln_b[0],
                    gla_norm[0], e_bf, _bf(ab_w_out[0]), seg_of_tile, wk=wk)
    xs = _mlp(xs, xs.shape[0], norm2[0], mods[0], _bf(mlp_w1[0]), _bf(mlp_w2[0]), final_norm,
              seg_of_tile, final_norm=False)

    hw = _norm_matmul(xs, norm1[1], mods[1], _bf(lru_w_in[0]), seg_of_tile, tn=1024)
    y = _rglru(hw, lru_conv_w[0], lru_conv_b[0], lru_wa[0], lru_ba[0], lru_wx[0], lru_bx[0],
               lru_lam[0], batch=batch, n_lat=n_lat, seq=seq, ctx_len=ctx_len)
    xl = _proj_res(y, xs, n_lat, mods[1], _bf(lru_w_out[0]), seg_of_tile)
    out = _mlp(xl, n_lat, norm2[1], mods[1], _bf(mlp_w1[1]), _bf(mlp_w2[1]), final_norm,
               seg_of_tile, final_norm=True)
    return out.reshape(batch, seq, d)
```

```python
import functools

import jax
import jax.numpy as jnp
from jax import lax
from jax.experimental import pallas as pl
from jax.experimental.pallas import tpu as pltpu

F32 = jnp.float32
BF16 = jnp.bfloat16

NORM_EPS = 1e-6
GRID_W = 64
N_MOD = 6

A_HEAD_DIM = 64
A_LORA_W = 96
A_LORA_A = 96
A_LORA_G = 256
A_LN_EPS = 64e-5
HEADS_PER_GROUP = 4
GROUP_W = HEADS_PER_GROUP * A_HEAD_DIM
CHUNK = 64
RWKV_GROUPS_PER_STEP = 4

B_HEADS = 4
B_LORA = 16
B_GATE_TAU = 16.0

C_BLOCKS = 8
C_CONV = 4
C_CONST = 8.0

LANE = 128
SUBLANE = 8
ROW_TILE = 512
PREP_TILE = 128
VMEM_LIMIT = 56 * 1024 * 1024


def _cparams(sem):
    return pltpu.CompilerParams(dimension_semantics=sem, vmem_limit_bytes=VMEM_LIMIT)


def _bf(x):
    return x.astype(BF16)


def _dot(a, b):
    return jnp.dot(a, b, preferred_element_type=F32)


def _dot_nt(a, b):
    return lax.dot_general(a, b, (((1,), (1,)), ((), ())), preferred_element_type=F32)


def _dot_tn(a, b):
    return lax.dot_general(a, b, (((0,), (0,)), ((), ())), preferred_element_type=F32)


def _split3(x):
    hi = _bf(x)
    r1 = x - hi.astype(F32)
    mid = _bf(r1)
    lo = _bf(r1 - mid.astype(F32))
    return hi, mid, lo


def _dot_exact_lhs(m_bf, x):
    hi, mid, lo = _split3(x)
    return _dot(m_bf, hi) + _dot(m_bf, mid) + _dot(m_bf, lo)


def _dot_exact_rhs(x, m_bf):
    hi, mid, lo = _split3(x)
    return _dot(hi, m_bf) + _dot(mid, m_bf) + _dot(lo, m_bf)


def _softplus(x):
    return jnp.maximum(x, 0.0) + jnp.log1p(jnp.exp(-jnp.abs(x)))


def _sigmoid(x):
    return jax.nn.sigmoid(x)


def _silu(x):
    return x * _sigmoid(x)


def _norm_mod(x, g, shift, scale):
    ms = jnp.mean(x * x, axis=-1, keepdims=True)
    return (x * lax.rsqrt(ms + NORM_EPS) * g) * (1.0 + scale) + shift


def _mods_kernel(c_ref, w_ref, b_ref, o_ref):
    s = _silu(c_ref[...])
    o_ref[...] = _dot(_bf(s), _bf(w_ref[...])) + b_ref[...]


def _mods(cvec, mod_w, mod_b):
    depth, d, n = mod_w.shape
    tn = 1024
    out = pl.pallas_call(
        _mods_kernel,
        grid=(depth, n // tn),
        in_specs=[
            pl.BlockSpec((SUBLANE, d), lambda l, j: (0, 0)),
            pl.BlockSpec((None, d, tn), lambda l, j: (l, 0, j)),
            pl.BlockSpec((None, 1, tn), lambda l, j: (l, 0, j)),
        ],
        out_specs=pl.BlockSpec((None, SUBLANE, tn), lambda l, j: (l, 0, j)),
        out_shape=jax.ShapeDtypeStruct((depth, SUBLANE, n), F32),
        compiler_params=_cparams(("parallel", "parallel")),
    )(cvec, mod_w, mod_b.reshape(depth, 1, n))
    return out.reshape(depth, SUBLANE, N_MOD, d)


def _norm_matmul_kernel(x_ref, g_ref, mod_ref, w_ref, o_ref, h_ref, *, shift_row, scale_row):
    @pl.when(pl.program_id(1) == 0)
    def _():
        h = _norm_mod(x_ref[...], g_ref[...], mod_ref[shift_row:shift_row + 1, :],
                      mod_ref[scale_row:scale_row + 1, :])
        h_ref[...] = _bf(h)

    o_ref[...] = _dot(h_ref[...], w_ref[...])


def _norm_matmul(x, g, mods_l, w_bf, seg_of_tile, *, tn, shift_row=0, scale_row=1):
    r, d = x.shape
    n = w_bf.shape[1]
    tm = ROW_TILE
    return pl.pallas_call(
        functools.partial(_norm_matmul_kernel, shift_row=shift_row, scale_row=scale_row),
        grid=(r // tm, n // tn),
        in_specs=[
            pl.BlockSpec((tm, d), lambda i, j: (i, 0)),
            pl.BlockSpec((1, d), lambda i, j: (0, 0)),
            pl.BlockSpec((None, N_MOD, d), lambda i, j: (seg_of_tile(i), 0, 0)),
            pl.BlockSpec((d, tn), lambda i, j: (0, j)),
        ],
        out_specs=pl.BlockSpec((tm, tn), lambda i, j: (i, j)),
        out_shape=jax.ShapeDtypeStruct((r, n), F32),
        scratch_shapes=[pltpu.VMEM((tm, d), BF16)],
        compiler_params=_cparams(("parallel", "arbitrary")),
    )(x, g.reshape(1, d), mods_l, w_bf)


def _mlp_kernel(x_ref, g_ref, mod_ref, w1_ref, w2_ref, fn_ref, o_ref, h_ref, acc_ref, *, final_norm):
    j = pl.program_id(1)

    @pl.when(j == 0)
    def _():
        h = _norm_mod(x_ref[...], g_ref[...], mod_ref[3:4, :], mod_ref[4:5, :])
        h_ref[...] = _bf(h)
        acc_ref[...] = jnp.zeros_like(acc_ref)

    a = _dot(h_ref[...], w1_ref[...])
    a = jnp.square(jnp.maximum(a, 0.0))
    acc_ref[...] += _dot(_bf(a), w2_ref[...])

    @pl.when(j == pl.num_programs(1) - 1)
    def _():
        xo = x_ref[...] + mod_ref[5:6, :] * acc_ref[...]
        if final_norm:
            ms = jnp.mean(xo * xo, axis=-1, keepdims=True)
            xo = xo * lax.rsqrt(ms + NORM_EPS) * fn_ref[...]
        o_ref[...] = xo


def _mlp(x, rows_out, g, mods_l, w1_bf, w2_bf, fn, seg_of_tile, *, final_norm):
    d = x.shape[1]
    dff = w1_bf.shape[1]
    tm, tf = ROW_TILE, 512
    return pl.pallas_call(
        functools.partial(_mlp_kernel, final_norm=final_norm),
        grid=(rows_out // tm, dff // tf),
        in_specs=[
            pl.BlockSpec((tm, d), lambda i, j: (i, 0)),
            pl.BlockSpec((1, d), lambda i, j: (0, 0)),
            pl.BlockSpec((None, N_MOD, d), lambda i, j: (seg_of_tile(i), 0, 0)),
            pl.BlockSpec((d, tf), lambda i, j: (0, j)),
            pl.BlockSpec((tf, d), lambda i, j: (j, 0)),
            pl.BlockSpec((1, d), lambda i, j: (0, 0)),
        ],
        out_specs=pl.BlockSpec((tm, d), lambda i, j: (i, 0)),
        out_shape=jax.ShapeDtypeStruct((rows_out, d), F32),
        scratch_shapes=[pltpu.VMEM((tm, d), BF16), pltpu.VMEM((tm, d), F32)],
        compiler_params=_cparams(("parallel", "arbitrary")),
    )(x, g.reshape(1, d), mods_l, w1_bf, w2_bf, fn.reshape(1, d))


def _proj_res_kernel(y_ref, x_ref, mod_ref, w_ref, o_ref):
    o_ref[...] = x_ref[...] + mod_ref[2:3, :] * _dot(_bf(y_ref[...]), w_ref[...])


def _proj_res(y, x, rows_out, mods_l, w_bf, seg_of_tile):
    k = y.shape[1]
    d = w_bf.shape[1]
    tm = ROW_TILE
    return pl.pallas_call(
        _proj_res_kernel,
        grid=(rows_out // tm,),
        in_specs=[
            pl.BlockSpec((tm, k), lambda i: (i, 0)),
            pl.BlockSpec((tm, d), lambda i: (i, 0)),
            pl.BlockSpec((None, N_MOD, d), lambda i: (seg_of_tile(i), 0, 0)),
            pl.BlockSpec((k, d), lambda i: (0, 0)),
        ],
        out_specs=pl.BlockSpec((tm, d), lambda i: (i, 0)),
        out_shape=jax.ShapeDtypeStruct((rows_out, d), F32),
        compiler_params=_cparams(("parallel",)),
    )(y, x, mods_l, w_bf)


def _head_sum(z, e_bf):
    parts = []
    for gi in range(z.shape[1] // GROUP_W):
        zs = z[:, gi * GROUP_W:(gi + 1) * GROUP_W]
        parts.append(_dot_exact_rhs(zs, e_bf))
    return jnp.concatenate(parts, axis=1)


def _rwkv_prep_kernel(p_ref, pp_ref, pn_ref, mu_ref, w0_ref, a0_ref, w2_ref, a2_ref, g2_ref,
                      kk_ref, ka_ref, rk_ref, e_ref,
                      r_o, v_o, kk_o, lw0_o, lw1_o, kd0_o, kd1_o, b0_o, b1_o, g_o, bonus_o,
                      *, n_lat_tiles, lat_tiles_per_seq, ctx_tiles_per_seq, aw):
    i = pl.program_id(0)
    tp = p_ref.shape[0]
    in_ctx = i >= n_lat_tiles
    pos = jnp.where(in_ctx, (i - n_lat_tiles) % ctx_tiles_per_seq, i % lat_tiles_per_seq)
    last = jnp.where(in_ctx, ctx_tiles_per_seq - 1, lat_tiles_per_seq - 1)
    is_first = pos == 0
    is_last = pos == last
    rowid = lax.broadcasted_iota(jnp.int32, (tp, 1), 0)

    def shifted(c0, c1):
        x = p_ref[:, c0:c1]
        prow = jnp.where(is_first, 0.0, pp_ref[SUBLANE - 1:SUBLANE, c0:c1])
        nrow = jnp.where(is_last, 0.0, pn_ref[0:1, c0:c1])
        xp = jnp.where(rowid == 0, prow, pltpu.roll(x, 1, 0))
        xn = jnp.where(rowid == tp - 1, nrow, pltpu.roll(x, tp - 1, 0))
        return x + mu_ref[0:1, c0:c1] * (xp - x) + mu_ref[1:2, c0:c1] * (xn - x)

    e_bf = e_ref[...]
    r = shifted(0, aw)
    k = shifted(aw, 2 * aw)
    v = shifted(2 * aw, 3 * aw)
    gd = shifted(3 * aw, 3 * aw + A_LORA_G)
    lo0 = 3 * aw + A_LORA_G
    xl = shifted(lo0, lo0 + 2 * A_LORA_W + 2 * A_LORA_A)

    r_o[...] = r
    v_o[...] = v
    g_o[...] = _dot(_bf(_sigmoid(gd)), g2_ref[...])

    kx = k * kk_ref[...]
    kn = jnp.sqrt(_head_sum(kx * kx, e_bf))
    kk = kx / jnp.maximum(kn, 1e-12)
    kk_o[...] = kk

    xl_t = _bf(jnp.tanh(xl))
    xl_b = _bf(xl)
    kd_sum = None
    for d, (lw_o, kd_o, b_o) in enumerate(((lw0_o, kd0_o, b0_o), (lw1_o, kd1_o, b1_o))):
        w_log = -_softplus(-(w0_ref[d:d + 1, :] + _dot(xl_t, w2_ref[d]))) - 0.5
        lw_o[...] = -jnp.exp(w_log)
        asig = _sigmoid(a0_ref[d:d + 1, :] + _dot(xl_b, a2_ref[d]))
        kd = k * (1.0 + (asig - 1.0) * ka_ref[...])
        kd_o[...] = kd
        b_o[...] = kk * asig
        kd_sum = kd if kd_sum is None else kd_sum + kd
    bonus_o[...] = _head_sum(r * rk_ref[...] * kd_sum, e_bf) * v


def _rwkv_prep(pa, mu_p, w0, a0, w2p, a2p, g2, k_k, k_a, r_k, e_bf, *, n_lat, seq, ctx_len):
    r = pa.shape[0]
    ncol = mu_p.shape[1]
    aw = w0.shape[1]
    tp = PREP_TILE
    hb = tp // SUBLANE
    nblk8 = r // SUBLANE
    lw = 2 * A_LORA_W + 2 * A_LORA_A
    row = lambda a: a.reshape(1, aw)
    full = lambda shape: pl.BlockSpec(shape, lambda i: (0,) * len(shape))
    out_spec = pl.BlockSpec((tp, aw), lambda i: (i, 0))
    out_sds = jax.ShapeDtypeStruct((r, aw), F32)
    return pl.pallas_call(
        functools.partial(_rwkv_prep_kernel, n_lat_tiles=n_lat // tp, lat_tiles_per_seq=seq // tp,
                          ctx_tiles_per_seq=ctx_len // tp, aw=aw),
        grid=(r // tp,),
        in_specs=[
            pl.BlockSpec((tp, ncol), lambda i: (i, 0)),
            pl.BlockSpec((SUBLANE, ncol), lambda i: (jnp.maximum(i * hb - 1, 0), 0)),
            pl.BlockSpec((SUBLANE, ncol), lambda i: (jnp.minimum((i + 1) * hb, nblk8 - 1), 0)),
            full((2, ncol)), full((2, aw)), full((2, aw)),
            full((2, lw, aw)), full((2, lw, aw)), full((A_LORA_G, aw)),
            full((1, aw)), full((1, aw)), full((1, aw)), full((GROUP_W, GROUP_W)),
        ],
        out_specs=[out_spec] * 11,
        out_shape=[out_sds] * 11,
        compiler_params=_cparams(("parallel",)),
    )(pa, pa, pa, mu_p, w0, a0, w2p, a2p, g2, row(k_k), row(k_a), row(r_k), e_bf)


def _rwkv_masks(reverse):
    L = CHUNK
    gw = GROUP_W
    row = lax.broadcasted_iota(jnp.int32, (gw, gw), 0)
    col = lax.broadcasted_iota(jnp.int32, (gw, gw), 1)
    same_head = (row >> 6) == (col >> 6)
    tr = row & (L - 1)
    tc = col & (L - 1)
    strict = same_head & ((tr < tc) if reverse else (tr > tc))
    incl = same_head & ((tr <= tc) if reverse else (tr >= tc))
    t_r = lax.broadcasted_iota(jnp.int32, (L, L), 0)
    t_c = lax.broadcasted_iota(jnp.int32, (L, L), 1)
    tri = _bf(jnp.where((t_c >= t_r) if reverse else (t_c <= t_r), 1.0, 0.0))
    return dict(same_head=same_head, strict=strict, incl=incl, eye=row == col, tri=tri)


def _rwkv_chunks(probs):
    L = CHUNK
    gw = GROUP_W
    masks = {rv: _rwkv_masks(rv) for rv in sorted({p[7] for p in probs})}
    mk = [masks[p[7]] for p in probs]
    each = lambda f, *cols: [f(*args) for args in zip(*cols)]
    rep = lambda x: jnp.concatenate([x] * HEADS_PER_GROUP, axis=0)
    nat = lambda x: x[0:L] + x[L:2 * L] + x[2 * L:3 * L] + x[3 * L:4 * L]

    c = [_dot_exact_lhs(m["tri"], p[3]) for m, p in zip(mk, probs)]
    c_l = [ci[(0 if p[7] else L - 1):(1 if p[7] else L), :] for ci, p in zip(c, probs)]

    at_bd, rt_bd, v_bf, bend_bd, kend_bd, lhs, rhs, e_l = [], [], [], [], [], [], [], []
    for (r, v, kk, lw, kd, b, _, _), m, ci, cl in zip(probs, mk, c, c_l):
        bd = lambda x, sh=m["same_head"]: jnp.where(sh, rep(x), 0.0)
        e_nc = jnp.exp(-ci)
        e_end = jnp.exp(cl - ci)
        a_ = bd(-kk * jnp.exp(ci - lw))
        r_ = bd(r * jnp.exp(ci))
        at_bd.append(a_)
        rt_bd.append(r_)
        v_bf.append(_bf(bd(v)))
        bend_bd.append(_bf(bd(b * e_end)))
        kend_bd.append(_bf(bd(kd * e_end)))
        lhs.append(_bf(jnp.concatenate([a_, r_], axis=0)))
        rhs.append(_bf(jnp.concatenate([rep(b * e_nc), rep(kd * e_nc)], axis=0)))
        e_l.append(jnp.exp(cl))

    s = each(_dot_nt, lhs, rhs)
    a_ab = [jnp.where(m["strict"], si[:gw, :gw], 0.0) for m, si in zip(mk, s)]
    a_ak = [_bf(jnp.where(m["strict"], si[:gw, gw:], 0.0)) for m, si in zip(mk, s)]
    r_b = [_bf(jnp.where(m["incl"], si[gw:, :gw], 0.0)) for m, si in zip(mk, s)]
    r_k = [_bf(jnp.where(m["incl"], si[gw:, gw:], 0.0)) for m, si in zip(mk, s)]

    akv = each(_dot, a_ak, v_bf)
    t = [jnp.where(m["eye"], 1.0, 0.0) + a for m, a in zip(mk, a_ab)]
    a_pow = a_ab
    for _ in range(5):
        a_pow_bf = [_bf(a) for a in a_pow]
        a_pow = each(_dot, a_pow_bf, a_pow_bf)
        t = [ti + _dot(_bf(ti), _bf(ap)) for ti, ap in zip(t, a_pow)]
    pq = [_dot(_bf(ti), _bf(jnp.concatenate([a_, ak], axis=1)))
          for ti, a_, ak in zip(t, at_bd, akv)]
    p_bf = [_bf(x[:, :gw]) for x in pq]
    q_bf = [_bf(x[:, gw:]) for x in pq]

    m_ = [jnp.where(m["eye"], el, 0.0) + _dot_tn(be, p)
          for m, el, be, p in zip(mk, e_l, bend_bd, p_bf)]
    n_ = [_dot_tn(be, q) + _dot_tn(ke, v) for be, q, ke, v in zip(bend_bd, q_bf, kend_bd, v_bf)]
    g_ = [r_ + _dot(rb, p) for r_, rb, p in zip(rt_bd, r_b, p_bf)]
    y0 = [_dot(rb, q) + _dot(rk, v) for rb, q, rk, v in zip(r_b, q_bf, r_k, v_bf)]

    h_bf = [_bf(p[6]) for p in probs]
    y = [_dot(_bf(nat(g)), h) + nat(y0i) for g, h, y0i in zip(g_, h_bf, y0)]
    h_new = [_dot(_bf(mi), h) + ni for mi, h, ni in zip(m_, h_bf, n_)]
    return list(zip(y, h_new))


def _rwkv_scan_kernel(rf, vf, kkf, lwf, kdf, bf_, rb, vb, kkb, lwb, kdb, bb,
                      yf_ref, yb_ref, hf_ref, hb_ref):
    @pl.when(pl.program_id(1) == 0)
    def _():
        hf_ref[...] = jnp.zeros_like(hf_ref)
        hb_ref[...] = jnp.zeros_like(hb_ref)

    gs = hf_ref.shape[0]
    probs = []
    for gi in range(gs):
        sl = slice(gi * GROUP_W, (gi + 1) * GROUP_W)
        probs.append((rf[:, sl], vf[:, sl], kkf[:, sl], lwf[:, sl], kdf[:, sl], bf_[:, sl],
                      hf_ref[gi], False))
        probs.append((rb[:, sl], vb[:, sl], kkb[:, sl], lwb[:, sl], kdb[:, sl], bb[:, sl],
                      hb_ref[gi], True))
    res = _rwkv_chunks(probs)
    for gi in range(gs):
        sl = slice(gi * GROUP_W, (gi + 1) * GROUP_W)
        yf_ref[:, sl], hf_ref[gi] = res[2 * gi]
        yb_ref[:, sl], hb_ref[gi] = res[2 * gi + 1]


def _chunk_row_block(b, i, *, reverse, n_lat, seq, ctx_len):
    nc = ctx_len // CHUNK
    nl = seq // CHUNK
    ci = (nc - 1 - i) if reverse else i
    li = (nl - 1 - (i - nc)) if reverse else (i - nc)
    return jnp.where(i < nc, (n_lat + b * ctx_len) // CHUNK + ci, b * nl + li)


def _rwkv_scan(r, v, kk, lw0, lw1, kd0, kd1, b0, b1, *, batch, n_lat, seq, ctx_len):
    rows, aw = r.shape
    gs = RWKV_GROUPS_PER_STEP
    ng = aw // (GROUP_W * gs)
    geo = dict(n_lat=n_lat, seq=seq, ctx_len=ctx_len)

    def spec(reverse):
        return pl.BlockSpec(
            (CHUNK, GROUP_W * gs),
            lambda pg, i: (_chunk_row_block(pg // ng, i, reverse=reverse, **geo), pg % ng))

    sf, sb = spec(False), spec(True)
    sds = jax.ShapeDtypeStruct((rows, aw), F32)
    return pl.pallas_call(
        _rwkv_scan_kernel,
        grid=(batch * ng, (seq + ctx_len) // CHUNK),
        in_specs=[sf] * 6 + [sb] * 6,
        out_specs=[sf, sb],
        out_shape=[sds, sds],
        scratch_shapes=[pltpu.VMEM((gs, GROUP_W, GROUP_W), F32),
                        pltpu.VMEM((gs, GROUP_W, GROUP_W), F32)],
        compiler_params=_cparams(("parallel", "arbitrary")),
    )(r, v, kk, lw0, kd0, b0, r, v, kk, lw1, kd1, b1)


def _gla_scan_kernel(qkf, vf, gdf, qkb, vb, gdb, gw2_ref, gb_ref, of_ref, ob_ref, sf_ref, sb_ref,
                     *, dk, dv):
    @pl.when(pl.program_id(1) == 0)
    def _():
        sf_ref[...] = jnp.zeros_like(sf_ref)
        sb_ref[...] = jnp.zeros_like(sb_ref)

    L = CHUNK
    wk = B_HEADS * dk
    t_r = lax.broadcasted_iota(jnp.int32, (L, L), 0)
    t_c = lax.broadcasted_iota(jnp.int32, (L, L), 1)

    qh, kh, keh, vh, dech, keeph, sth = [], [], [], [], [], [], []
    for d, (qk_ref, v_ref, gd_ref, st_ref) in enumerate(((qkf, vf, gdf, sf_ref),
                                                         (qkb, vb, gdb, sb_ref))):
        reverse = d == 1
        keep = (t_c >= t_r) if reverse else (t_c <= t_r)
        tri = _bf(jnp.where(keep, 1.0, 0.0))
        z = _dot(_bf(gd_ref[...]), gw2_ref[d]) + gb_ref[d:d + 1, :]
        la = -_softplus(-z) * (1.0 / B_GATE_TAU)
        cum = _dot_exact_lhs(tri, la)
        last = 0 if reverse else L - 1
        cl = cum[last:last + 1, :]
        q_in = _bf(qk_ref[:, :wk] * (dk ** -0.5) * jnp.exp(cum))
        k_in = _bf(qk_ref[:, wk:] * jnp.exp(-cum))
        k_end = _bf(qk_ref[:, wk:] * jnp.exp(cl - cum))
        dec = jnp.exp(cl)
        for h in range(B_HEADS):
            ks = slice(h * dk, (h + 1) * dk)
            qh.append(q_in[:, ks])
            kh.append(k_in[:, ks])
            keh.append(k_end[:, ks])
            vh.append(_bf(v_ref[:, h * dv:(h + 1) * dv]))
            dech.append(dec[:, ks])
            keeph.append(keep)
            sth.append(st_ref[h])

    sc = [_bf(jnp.where(kp, _dot_nt(q, k), 0.0)) for kp, q, k in zip(keeph, qh, kh)]
    inter = [_dot_nt(q, _bf(st)) for q, st in zip(qh, sth)]
    kvt = [_dot_tn(v, ke) for v, ke in zip(vh, keh)]
    out = [_dot(s_, v) + it for s_, v, it in zip(sc, vh, inter)]
    for d, (o_ref, st_ref) in enumerate(((of_ref, sf_ref), (ob_ref, sb_ref))):
        for h in range(B_HEADS):
            i = d * B_HEADS + h
            o_ref[:, h * dv:(h + 1) * dv] = out[i]
            st_ref[h] = sth[i] * dech[i] + kvt[i]


def _gla_scan(pb, gw2p, gb, *, batch, n_lat, seq, ctx_len, wk, wv, col0):
    rows = pb.shape[0]
    dk, dv = wk // B_HEADS, wv // B_HEADS
    geo = dict(n_lat=n_lat, seq=seq, ctx_len=ctx_len)
    gd_blk = (col0 + 2 * wk + 2 * wv) // LANE

    def specs(reverse):
        rb = lambda b, i: _chunk_row_block(b, i, reverse=reverse, **geo)
        return [
            pl.BlockSpec((CHUNK, 2 * wk), lambda b, i: (rb(b, i), col0 // (2 * wk))),
            pl.BlockSpec((CHUNK, wv), lambda b, i: (rb(b, i), (col0 + 2 * wk) // wv)),
            pl.BlockSpec((CHUNK, LANE), lambda b, i: (rb(b, i), gd_blk)),
        ]

    def out_spec(reverse):
        return pl.BlockSpec((CHUNK, wv),
                            lambda b, i: (_chunk_row_block(b, i, reverse=reverse, **geo), 0))

    sds = jax.ShapeDtypeStruct((rows, wv), F32)
    return pl.pallas_call(
        functools.partial(_gla_scan_kernel, dk=dk, dv=dv),
        grid=(batch, (seq + ctx_len) // CHUNK),
        in_specs=specs(False) + specs(True) + [
            pl.BlockSpec((2, LANE, wk), lambda b, i: (0, 0, 0)),
            pl.BlockSpec((2, wk), lambda b, i: (0, 0)),
        ],
        out_specs=[out_spec(False), out_spec(True)],
        out_shape=[sds, sds],
        scratch_shapes=[pltpu.VMEM((B_HEADS, dv, dk), F32), pltpu.VMEM((B_HEADS, dv, dk), F32)],
        compiler_params=_cparams(("parallel", "arbitrary")),
    )(pb, pb, pb, pb, pb, pb, gw2p, gb)


def _mixer_out_kernel(yf, yb, bonus, g, of, ob, gate, x_ref, mod_ref, lnw, lnb, gn, e_ref, w_ref,
                      o_ref, h_ref, *, aw, dv):
    e_bf = e_ref[...]
    y = yf[...] + yb[...]
    inv = 1.0 / A_HEAD_DIM
    mean = _head_sum(y, e_bf) * inv
    dlt = y - mean
    var = _head_sum(dlt * dlt, e_bf) * inv
    yn = dlt * lax.rsqrt(var + A_LN_EPS) * lnw[...] + lnb[...]
    h_ref[:, :aw] = _bf((yn + bonus[...]) * g[...])

    o = of[...] + ob[...]
    gt = _silu(gate[...])
    for h in range(B_HEADS):
        sl = slice(h * dv, (h + 1) * dv)
        oh = o[:, sl]
        ms = jnp.mean(oh * oh, axis=-1, keepdims=True)
        h_ref[:, aw + h * dv:aw + (h + 1) * dv] = _bf(oh * lax.rsqrt(ms + NORM_EPS) * gn[...] * gt[:, sl])

    o_ref[...] = x_ref[...] + mod_ref[2:3, :] * _dot(h_ref[...], w_ref[...])


def _mixer_out(yf, yb, bonus, g, of, ob, pb, x, mods_l, ln_w, ln_b, gla_norm, e_bf, w_bf,
               seg_of_tile, *, wk, col0):
    rows, d = x.shape
    aw = yf.shape[1]
    wv = of.shape[1]
    dv = wv // B_HEADS
    tm = 256
    tiles_per_row_tile = ROW_TILE // tm
    rs = lambda w: pl.BlockSpec((tm, w), lambda i: (i, 0))
    full = lambda shape: pl.BlockSpec(shape, lambda i: (0,) * len(shape))
    return pl.pallas_call(
        functools.partial(_mixer_out_kernel, aw=aw, dv=dv),
        grid=(rows // tm,),
        in_specs=[
            rs(aw), rs(aw), rs(aw), rs(aw), rs(wv), rs(wv),
            pl.BlockSpec((tm, wv), lambda i: (i, (col0 + 2 * wk + wv) // wv)),
            rs(d),
            pl.BlockSpec((None, N_MOD, d), lambda i: (seg_of_tile(i // tiles_per_row_tile), 0, 0)),
            full((1, aw)), full((1, aw)), full((1, dv)), full((GROUP_W, GROUP_W)),
            full((aw + wv, d)),
        ],
        out_specs=rs(d),
        out_shape=jax.ShapeDtypeStruct((rows, d), F32),
        scratch_shapes=[pltpu.VMEM((tm, aw + wv), BF16)],
        compiler_params=_cparams(("parallel",)),
    )(yf, yb, bonus, g, of, ob, pb, x, mods_l, ln_w.reshape(1, aw), ln_b.reshape(1, aw),
      gla_norm.reshape(1, dv), e_bf, w_bf)


def _lru_kernel(*refs, reverse, n_blk, combine):
    if combine:
        (x_ref, xp_ref, xn_ref, h0_ref, cw_ref, cb_ref, wc_ref, ba_ref, bx_ref, lam_ref,
         hf_ref, gate_ref, out_ref, hl_ref, a_sc, u_sc, hs_sc, carry_sc) = refs
    else:
        (x_ref, xp_ref, xn_ref, h0_ref, cw_ref, cb_ref, wc_ref, ba_ref, bx_ref, lam_ref,
         out_ref, hl_ref, a_sc, u_sc, hs_sc, carry_sc) = refs
    j = pl.program_id(2)
    blk = (n_blk - 1 - j) if reverse else j
    lb, ct = x_ref.shape
    cblk = wc_ref.shape[1]
    is_first = blk == 0
    is_last = blk == n_blk - 1
    rowid = lax.broadcasted_iota(jnp.int32, (lb, 1), 0)

    x = x_ref[...]
    p6 = jnp.where(is_first, 0.0, xp_ref[SUBLANE - 2:SUBLANE - 1, :])
    p7 = jnp.where(is_first, 0.0, xp_ref[SUBLANE - 1:SUBLANE, :])
    n0 = jnp.where(is_last, 0.0, xn_ref[0:1, :])
    xm1 = jnp.where(rowid == 0, p7, pltpu.roll(x, 1, 0))
    xm2 = jnp.where(rowid == 0, p6, jnp.where(rowid == 1, p7, pltpu.roll(x, 2, 0)))
    xp1 = jnp.where(rowid == lb - 1, n0, pltpu.roll(x, lb - 1, 0))
    xs = (cb_ref[...] + cw_ref[0:1, :] * xm2 + cw_ref[1:2, :] * xm1 + cw_ref[2:3, :] * x
          + cw_ref[3:4, :] * xp1)

    for n in range(ct // cblk):
        sl = slice(n * cblk, (n + 1) * cblk)
        xb = xs[:, sl]
        ri = _dot(_bf(xb), wc_ref[n])
        rg = _sigmoid(ri[:, :cblk] + ba_ref[:, sl])
        ig = _sigmoid(ri[:, cblk:] + bx_ref[:, sl])
        log_a = -C_CONST * rg * _softplus(-lam_ref[:, sl])
        a = jnp.exp(log_a)
        a_sc[:, sl] = a
        u_sc[:, sl] = jnp.sqrt(1.0 - a * a) * (ig * xb)

    @pl.when(j == 0)
    def _():
        carry_sc[...] = h0_ref[0:1, :]

    def step(t, h):
        tt = (lb - 1 - t) if reverse else t
        h = a_sc[pl.ds(tt, 1), :] * h + u_sc[pl.ds(tt, 1), :]
        hs_sc[pl.ds(tt, 1), :] = h
        return h

    h = lax.fori_loop(0, lb, step, carry_sc[...], unroll=8)
    carry_sc[...] = h
    hl_ref[...] = jnp.broadcast_to(h, hl_ref.shape)
    if combine:
        out_ref[...] = (hs_sc[...] + hf_ref[...]) * jax.nn.gelu(gate_ref[...])
    else:
        out_ref[...] = hs_sc[...]


def _lru_pass(x_arr, x_specs, h0, conv_w, conv_b, wcat, ba, bx, lam, out_shape, out_spec, extra,
              *, batch, n_blk, lb, reverse):
    c = conv_w.shape[1]
    ct = 1024
    nct = c // ct
    cblk = wcat.shape[1]
    chan = lambda rows_: pl.BlockSpec((rows_, ct), lambda b, k, j: (0, k))
    in_specs = list(x_specs) + [
        pl.BlockSpec((SUBLANE, ct), lambda b, k, j: (b, k)),
        chan(C_CONV), chan(1),
        pl.BlockSpec((ct // cblk, cblk, 2 * cblk), lambda b, k, j: (k, 0, 0)),
        chan(1), chan(1), chan(1),
    ] + [s for _, s in extra]
    args = [x_arr, x_arr, x_arr, h0, conv_w, conv_b.reshape(1, c), wcat, ba.reshape(1, c),
            bx.reshape(1, c), lam.reshape(1, c)] + [a for a, _ in extra]
    return pl.pallas_call(
        functools.partial(_lru_kernel, reverse=reverse, n_blk=n_blk, combine=bool(extra)),
        grid=(batch, nct, n_blk),
        in_specs=in_specs,
        out_specs=[out_spec, pl.BlockSpec((SUBLANE, ct), lambda b, k, j: (b, k))],
        out_shape=[out_shape, jax.ShapeDtypeStruct((batch * SUBLANE, c), F32)],
        scratch_shapes=[pltpu.VMEM((lb, ct), F32), pltpu.VMEM((lb, ct), F32),
                        pltpu.VMEM((lb, ct), F32), pltpu.VMEM((1, ct), F32)],
        compiler_params=_cparams(("parallel", "parallel", "arbitrary")),
    )(*args)


def _rglru(hw, conv_w, conv_b, wa, ba, wx, bx, lam, *, batch, n_lat, seq, ctx_len):
    rows_total, c2 = hw.shape
    c = c2 // 2
    ct = 1024
    rows = seq // GRID_W
    lbc = 128
    ncb = ctx_len // lbc
    hw_grid = hw.reshape(rows_total // GRID_W, GRID_W * c2)
    xcol = c // ct
    per_col = c2 // ct

    wcat = [_bf(jnp.concatenate([wa[d], wx[d]], axis=-1)) for d in range(2)]
    zeros_state = jnp.zeros((batch * SUBLANE, c), F32)

    def ctx_specs(reverse):
        blk = (lambda j: ncb - 1 - j) if reverse else (lambda j: j)
        base = n_lat // lbc
        base8 = n_lat // SUBLANE
        per8 = lbc // SUBLANE
        cur = pl.BlockSpec((lbc, ct), lambda b, k, j: (base + b * ncb + blk(j), xcol + k))
        prv = pl.BlockSpec((SUBLANE, ct), lambda b, k, j: (
            base8 + b * ncb * per8 + jnp.maximum(blk(j) * per8 - 1, 0), xcol + k))
        nxt = pl.BlockSpec((SUBLANE, ct), lambda b, k, j: (
            base8 + b * ncb * per8 + jnp.minimum((blk(j) + 1) * per8, ncb * per8 - 1), xcol + k))
        return cur, prv, nxt

    def lat_specs(reverse):
        blk = (lambda j: GRID_W - 1 - j) if reverse else (lambda j: j)
        r8 = rows // SUBLANE
        cur = pl.BlockSpec((rows, ct), lambda b, k, j: (b, blk(j) * per_col + xcol + k))
        prv = pl.BlockSpec((SUBLANE, ct), lambda b, k, j: (
            b * r8 + r8 - 1, jnp.maximum(blk(j) - 1, 0) * per_col + xcol + k))
        nxt = pl.BlockSpec((SUBLANE, ct), lambda b, k, j: (
            b * r8, jnp.minimum(blk(j) + 1, GRID_W - 1) * per_col + xcol + k))
        return cur, prv, nxt

    common = dict(batch=batch)
    ctx_scratch_sds = jax.ShapeDtypeStruct((batch * ctx_len, c), F32)
    ctx_out = lambda reverse: pl.BlockSpec(
        (lbc, ct), lambda b, k, j: (b * ncb + ((ncb - 1 - j) if reverse else j), k))
    _, st = _lru_pass(hw, ctx_specs(False), zeros_state, conv_w, conv_b, wcat[0], ba[0], bx[0],
                      lam[0], ctx_scratch_sds, ctx_out(False), [], n_blk=ncb, lb=lbc,
                      reverse=False, **common)
    hf_sds = jax.ShapeDtypeStruct((n_lat, c), F32)
    hf_spec = lambda reverse: pl.BlockSpec(
        (rows, ct), lambda b, k, j: (b * GRID_W + ((GRID_W - 1 - j) if reverse else j), k))
    hf, _ = _lru_pass(hw_grid, lat_specs(False), st, conv_w, conv_b, wcat[0], ba[0], bx[0], lam[0],
                      hf_sds, hf_spec(False), [], n_blk=GRID_W, lb=rows, reverse=False, **common)
    _, st = _lru_pass(hw, ctx_specs(True), zeros_state, conv_w, conv_b, wcat[1], ba[1], bx[1],
                      lam[1], ctx_scratch_sds, ctx_out(True), [], n_blk=ncb, lb=lbc,
                      reverse=True, **common)
    gate_spec = pl.BlockSpec((rows, ct), lambda b, k, j: (b, (GRID_W - 1 - j) * per_col + k))
    y_sds = jax.ShapeDtypeStruct((n_lat // GRID_W, GRID_W * c), F32)
    y_spec = pl.BlockSpec((rows, ct), lambda b, k, j: (b, (GRID_W - 1 - j) * (c // ct) + k))
    y, _ = _lru_pass(hw_grid, lat_specs(True), st, conv_w, conv_b, wcat[1], ba[1], bx[1], lam[1],
                     y_sds, y_spec, [(hf, hf_spec(True)), (hw_grid, gate_spec)],
                     n_blk=GRID_W, lb=rows, reverse=True, **common)
    return y.reshape(n_lat, c)


def _pad_cols(w, n):
    return jnp.pad(w, ((0, 0),) * (w.ndim - 1) + ((0, n - w.shape[-1]),))


def kernel(x, c, ctx, c_ctx, mod_w, mod_b, norm1, norm2, mlp_w1, mlp_w2, ab_w_in, ab_w_out, rw_mu,
           rw_w0, rw_w2, rw_a0, rw_a2, rw_g2, rw_kk, rw_ka, rw_rk, rw_ln_w, rw_ln_b, gla_gw2,
           gla_gb, gla_norm, lru_w_in, lru_w_out, lru_conv_w, lru_conv_b, lru_wa, lru_ba, lru_wx,
           lru_bx, lru_lam, final_norm):
    batch, seq, d = x.shape
    ctx_len = ctx.shape[1]
    depth = mod_w.shape[0]
    n_lat = batch * seq
    aw = rw_w0.shape[-1]
    wk = gla_gb.shape[-1]
    wv = ab_w_out.shape[1] - aw
    a_cols = rw_mu.shape[-1]
    b_cols = ab_w_in.shape[-1] - a_cols
    assert depth == 2 and batch + 1 <= SUBLANE
    assert seq % ROW_TILE == 0 and (batch * ctx_len) % ROW_TILE == 0
    assert ctx_len % PREP_TILE == 0 and ctx_len % 128 == 0 and seq % (GRID_W * SUBLANE) == 0
    assert aw % GROUP_W == 0 and wv == aw and 2 * wk == wv

    tiles_per_seq = seq // ROW_TILE
    seg_of_tile = lambda i: jnp.minimum(i // tiles_per_seq, batch)

    xs = jnp.concatenate([x.reshape(n_lat, d), ctx.reshape(batch * ctx_len, d)], axis=0)
    cvec = jnp.concatenate(
        [c, c_ctx[None, :], jnp.zeros((SUBLANE - batch - 1, d), F32)], axis=0)
    mods = _mods(cvec, mod_w, mod_b)

    hid = jnp.arange(GROUP_W) // A_HEAD_DIM
    e_bf = _bf(hid[:, None] == hid[None, :])

    a_pad = -(-a_cols // LANE) * LANE
    col0 = -(-a_cols // (2 * wk)) * (2 * wk)
    n_tiles = 3
    tn = -(-(col0 + b_cols) // (n_tiles * LANE)) * LANE
    w_ab = _bf(jnp.concatenate(
        [_pad_cols(ab_w_in[0][:, :a_cols], col0),
         _pad_cols(ab_w_in[0][:, a_cols:], n_tiles * tn - col0)], axis=1))
    pa = pb = _norm_matmul(xs, norm1[0], mods[0], w_ab, seg_of_tile, tn=tn)

    lora_w = 2 * A_LORA_W + 2 * A_LORA_A
    w2p = jnp.zeros((2, lora_w, aw), F32)
    a2p = jnp.zeros((2, lora_w, aw), F32)
    for dd in range(2):
        w2p = w2p.at[dd, dd * A_LORA_W:(dd + 1) * A_LORA_W].set(rw_w2[0, dd])
        a2p = a2p.at[dd, 2 * A_LORA_W + dd * A_LORA_A:2 * A_LORA_W + (dd + 1) * A_LORA_A].set(
            rw_a2[0, dd])
    (r_, v_, kk_, lw0, lw1, kd0, kd1, b0, b1, g_, bonus) = _rwkv_prep(
        pa, _pad_cols(rw_mu[0], a_pad), rw_w0[0], rw_a0[0], _bf(w2p), _bf(a2p), _bf(rw_g2[0]),
        rw_kk[0], rw_ka[0], rw_rk[0].reshape(-1), e_bf, n_lat=n_lat, seq=seq, ctx_len=ctx_len)
    yf, yb = _rwkv_scan(r_, v_, kk_, lw0, lw1, kd0, kd1, b0, b1, batch=batch, n_lat=n_lat, seq=seq,
                        ctx_len=ctx_len)

    gw2p = jnp.zeros((2, LANE, wk), F32)
    for dd in range(2):
        gw2p = gw2p.at[dd, dd * B_LORA:(dd + 1) * B_LORA].set(gla_gw2[0, dd])
    of, ob = _gla_scan(pb, _bf(gw2p), gla_gb[0], batch=batch, n_lat=n_lat, seq=seq,
                       ctx_len=ctx_len, wk=wk, wv=wv, col0=col0)

    xs = _mixer_out(yf, yb, bonus, g_, of, ob, pb, xs, mods[0], rw_ln_w[0], rw_ln_b[0],
                    gla_norm[0], e_bf, _bf(ab_w_out[0]), seg_of_tile, wk=wk, col0=col0)
    xs = _mlp(xs, xs.shape[0], norm2[0], mods[0], _bf(mlp_w1[0]), _bf(mlp_w2[0]), final_norm,
              seg_of_tile, final_norm=False)

    hw = _norm_matmul(xs, norm1[1], mods[1], _bf(lru_w_in[0]), seg_of_tile, tn=2048)
    y = _rglru(hw, lru_conv_w[0], lru_conv_b[0], lru_wa[0], lru_ba[0], lru_wx[0], lru_bx[0],
               lru_lam[0], batch=batch, n_lat=n_lat, seq=seq, ctx_len=ctx_len)
    xl = _proj_res(y, xs, n_lat, mods[1], _bf(lru_w_out[0]), seg_of_tile)
    out = _mlp(xl, n_lat, norm2[1], mods[1], _bf(mlp_w1[1]), _bf(mlp_w2[1]), final_norm,
               seg_of_tile, final_norm=True)
    return out.reshape(batch, seq, d)
```

```python
import functools

import jax
import jax.numpy as jnp
from jax import lax
from jax.experimental import pallas as pl
from jax.experimental.pallas import tpu as pltpu

F32 = jnp.float32
BF16 = jnp.bfloat16

NORM_EPS = 1e-6
GRID_W = 64
N_MOD = 6

A_HEAD_DIM = 64
A_LORA_W = 96
A_LORA_A = 96
A_LORA_G = 256
A_LN_EPS = 64e-5
HEADS_PER_GROUP = 4
GROUP_W = HEADS_PER_GROUP * A_HEAD_DIM
CHUNK = 64
RWKV_GROUPS_PER_STEP = 4

B_HEADS = 4
B_LORA = 16
B_GATE_TAU = 16.0

C_BLOCKS = 8
C_CONV = 4
C_CONST = 8.0

LANE = 128
SUBLANE = 8
ROW_TILE = 512
PREP_TILE = 128
VMEM_LIMIT = 56 * 1024 * 1024


def _cparams(sem):
    return pltpu.CompilerParams(dimension_semantics=sem, vmem_limit_bytes=VMEM_LIMIT)


def _bf(x):
    return x.astype(BF16)


def _dot(a, b):
    return jnp.dot(a, b, preferred_element_type=F32)


def _dot_nt(a, b):
    return lax.dot_general(a, b, (((1,), (1,)), ((), ())), preferred_element_type=F32)


def _dot_tn(a, b):
    return lax.dot_general(a, b, (((0,), (0,)), ((), ())), preferred_element_type=F32)


def _split3(x):
    hi = _bf(x)
    r1 = x - hi.astype(F32)
    mid = _bf(r1)
    lo = _bf(r1 - mid.astype(F32))
    return hi, mid, lo


def _dot_exact_lhs(m_bf, x):
    hi, mid, lo = _split3(x)
    return _dot(m_bf, hi) + _dot(m_bf, mid) + _dot(m_bf, lo)


def _dot_exact_rhs(x, m_bf):
    hi, mid, lo = _split3(x)
    return _dot(hi, m_bf) + _dot(mid, m_bf) + _dot(lo, m_bf)


def _softplus(x):
    return jnp.maximum(x, 0.0) + jnp.log1p(jnp.exp(-jnp.abs(x)))


def _sigmoid(x):
    return jax.nn.sigmoid(x)


def _silu(x):
    return x * _sigmoid(x)


def _norm_mod(x, g, shift, scale):
    ms = jnp.mean(x * x, axis=-1, keepdims=True)
    return (x * lax.rsqrt(ms + NORM_EPS) * g) * (1.0 + scale) + shift


def _mods_kernel(c_ref, w_ref, b_ref, o_ref):
    s = _silu(c_ref[...])
    o_ref[...] = _dot(_bf(s), _bf(w_ref[...])) + b_ref[...]


def _mods(cvec, mod_w, mod_b):
    depth, d, n = mod_w.shape
    tn = 1024
    out = pl.pallas_call(
        _mods_kernel,
        grid=(depth, n // tn),
        in_specs=[
            pl.BlockSpec((SUBLANE, d), lambda l, j: (0, 0)),
            pl.BlockSpec((None, d, tn), lambda l, j: (l, 0, j)),
            pl.BlockSpec((None, 1, tn), lambda l, j: (l, 0, j)),
        ],
        out_specs=pl.BlockSpec((None, SUBLANE, tn), lambda l, j: (l, 0, j)),
        out_shape=jax.ShapeDtypeStruct((depth, SUBLANE, n), F32),
        compiler_params=_cparams(("parallel", "parallel")),
    )(cvec, mod_w, mod_b.reshape(depth, 1, n))
    return out.reshape(depth, SUBLANE, N_MOD, d)


def _norm_matmul_kernel(x_ref, g_ref, mod_ref, w_ref, o_ref, h_ref, *, shift_row, scale_row):
    @pl.when(pl.program_id(1) == 0)
    def _():
        h = _norm_mod(x_ref[...], g_ref[...], mod_ref[shift_row:shift_row + 1, :],
                      mod_ref[scale_row:scale_row + 1, :])
        h_ref[...] = _bf(h)

    o_ref[...] = _dot(h_ref[...], w_ref[...])


def _norm_matmul(x, g, mods_l, w_bf, seg_of_tile, *, tn, tile0=0, n_tiles=None, shift_row=0,
                 scale_row=1):
    r, d = x.shape
    n = w_bf.shape[1]
    tm = ROW_TILE
    n_tiles = r // tm if n_tiles is None else n_tiles
    return pl.pallas_call(
        functools.partial(_norm_matmul_kernel, shift_row=shift_row, scale_row=scale_row),
        grid=(n_tiles, n // tn),
        in_specs=[
            pl.BlockSpec((tm, d), lambda i, j: (i + tile0, 0)),
            pl.BlockSpec((1, d), lambda i, j: (0, 0)),
            pl.BlockSpec((None, N_MOD, d), lambda i, j: (seg_of_tile(i + tile0), 0, 0)),
            pl.BlockSpec((d, tn), lambda i, j: (0, j)),
        ],
        out_specs=pl.BlockSpec((tm, tn), lambda i, j: (i, j)),
        out_shape=jax.ShapeDtypeStruct((n_tiles * tm, n), F32),
        scratch_shapes=[pltpu.VMEM((tm, d), BF16)],
        compiler_params=_cparams(("parallel", "arbitrary")),
    )(x, g.reshape(1, d), mods_l, w_bf)


def _norm_matmul_grid_kernel(x_ref, g_ref, mod_ref, w_ref, o0_ref, o1_ref, h_ref, res_ref):
    j = pl.program_id(1)

    @pl.when(j == 0)
    def _():
        h = _norm_mod(x_ref[...], g_ref[...], mod_ref[0:1, :], mod_ref[1:2, :])
        h_ref[...] = _bf(h)

    res = _dot(h_ref[...], w_ref[...])
    nh = w_ref.shape[1]
    gr = o0_ref.shape[0]
    nq = nh // LANE
    for q in range(nq):
        res_ref[q] = res[:, q * LANE:(q + 1) * LANE]

    def scatter(o_ref):
        for c in range(GRID_W):
            for q in range(nq):
                o_ref[:, c * nh + q * LANE:c * nh + (q + 1) * LANE] = (
                    res_ref[q, pl.ds(c, gr, stride=GRID_W), :])

    @pl.when(j == 0)
    def _():
        scatter(o0_ref)

    @pl.when(j == 1)
    def _():
        scatter(o1_ref)


def _norm_matmul_grid(x, n_lat, g, mods_l, w_bf, seg_of_tile):
    d = x.shape[1]
    nh = w_bf.shape[1] // 2
    tm = ROW_TILE
    gr = tm // GRID_W
    out_spec = pl.BlockSpec((gr, GRID_W * nh), lambda i, j: (i, 0))
    sds = jax.ShapeDtypeStruct((n_lat // GRID_W, GRID_W * nh), F32)
    return pl.pallas_call(
        _norm_matmul_grid_kernel,
        grid=(n_lat // tm, 2),
        in_specs=[
            pl.BlockSpec((tm, d), lambda i, j: (i, 0)),
            pl.BlockSpec((1, d), lambda i, j: (0, 0)),
            pl.BlockSpec((None, N_MOD, d), lambda i, j: (seg_of_tile(i), 0, 0)),
            pl.BlockSpec((d, nh), lambda i, j: (0, j)),
        ],
        out_specs=[out_spec, out_spec],
        out_shape=[sds, sds],
        scratch_shapes=[pltpu.VMEM((tm, d), BF16), pltpu.VMEM((nh // LANE, tm, LANE), F32)],
        compiler_params=_cparams(("parallel", "arbitrary")),
    )(x, g.reshape(1, d), mods_l, w_bf)


def _mlp_kernel(x_ref, g_ref, mod_ref, w1_ref, w2_ref, fn_ref, o_ref, h_ref, acc_ref, *, final_norm):
    j = pl.program_id(1)

    @pl.when(j == 0)
    def _():
        h = _norm_mod(x_ref[...], g_ref[...], mod_ref[3:4, :], mod_ref[4:5, :])
        h_ref[...] = _bf(h)
        acc_ref[...] = jnp.zeros_like(acc_ref)

    a = _dot(h_ref[...], w1_ref[...])
    a = jnp.square(jnp.maximum(a, 0.0))
    acc_ref[...] += _dot(_bf(a), w2_ref[...])

    @pl.when(j == pl.num_programs(1) - 1)
    def _():
        xo = x_ref[...] + mod_ref[5:6, :] * acc_ref[...]
        if final_norm:
            ms = jnp.mean(xo * xo, axis=-1, keepdims=True)
            xo = xo * lax.rsqrt(ms + NORM_EPS) * fn_ref[...]
        o_ref[...] = xo


def _mlp(x, rows_out, g, mods_l, w1_bf, w2_bf, fn, seg_of_tile, *, final_norm):
    d = x.shape[1]
    dff = w1_bf.shape[1]
    tm, tf = ROW_TILE, 512
    return pl.pallas_call(
        functools.partial(_mlp_kernel, final_norm=final_norm),
        grid=(rows_out // tm, dff // tf),
        in_specs=[
            pl.BlockSpec((tm, d), lambda i, j: (i, 0)),
            pl.BlockSpec((1, d), lambda i, j: (0, 0)),
            pl.BlockSpec((None, N_MOD, d), lambda i, j: (seg_of_tile(i), 0, 0)),
            pl.BlockSpec((d, tf), lambda i, j: (0, j)),
            pl.BlockSpec((tf, d), lambda i, j: (j, 0)),
            pl.BlockSpec((1, d), lambda i, j: (0, 0)),
        ],
        out_specs=pl.BlockSpec((tm, d), lambda i, j: (i, 0)),
        out_shape=jax.ShapeDtypeStruct((rows_out, d), F32),
        scratch_shapes=[pltpu.VMEM((tm, d), BF16), pltpu.VMEM((tm, d), F32)],
        compiler_params=_cparams(("parallel", "arbitrary")),
    )(x, g.reshape(1, d), mods_l, w1_bf, w2_bf, fn.reshape(1, d))


def _proj_res_kernel(y_ref, x_ref, mod_ref, w_ref, o_ref, yt_ref):
    k = w_ref.shape[0]
    gr = y_ref.shape[0]
    nq = k // LANE
    for c in range(GRID_W):
        for q in range(nq):
            yt_ref[q, pl.ds(c, gr, stride=GRID_W), :] = (
                y_ref[:, c * k + q * LANE:c * k + (q + 1) * LANE])
    yt = jnp.concatenate([_bf(yt_ref[q]) for q in range(nq)], axis=1)
    o_ref[...] = x_ref[...] + mod_ref[2:3, :] * _dot(yt, w_ref[...])


def _proj_res(y_grid, x, rows_out, mods_l, w_bf, seg_of_tile):
    k, d = w_bf.shape
    tm = ROW_TILE
    return pl.pallas_call(
        _proj_res_kernel,
        grid=(rows_out // tm,),
        scratch_shapes=[pltpu.VMEM((k // LANE, tm, LANE), F32)],
        in_specs=[
            pl.BlockSpec((tm // GRID_W, GRID_W * k), lambda i: (i, 0)),
            pl.BlockSpec((tm, d), lambda i: (i, 0)),
            pl.BlockSpec((None, N_MOD, d), lambda i: (seg_of_tile(i), 0, 0)),
            pl.BlockSpec((k, d), lambda i: (0, 0)),
        ],
        out_specs=pl.BlockSpec((tm, d), lambda i: (i, 0)),
        out_shape=jax.ShapeDtypeStruct((rows_out, d), F32),
        compiler_params=_cparams(("parallel",)),
    )(y_grid, x, mods_l, w_bf)


def _head_sum(z, e_bf):
    parts = []
    for gi in range(z.shape[1] // GROUP_W):
        zs = z[:, gi * GROUP_W:(gi + 1) * GROUP_W]
        parts.append(_dot_exact_rhs(zs, e_bf))
    return jnp.concatenate(parts, axis=1)


def _rwkv_prep_kernel(p_ref, pp_ref, pn_ref, mu_ref, w0_ref, a0_ref, w2_ref, a2_ref, g2_ref,
                      kk_ref, ka_ref, rk_ref, e_ref,
                      r_o, v_o, kk_o, lw0_o, lw1_o, kd0_o, kd1_o, b0_o, b1_o, g_o, bonus_o,
                      *, n_lat_tiles, lat_tiles_per_seq, ctx_tiles_per_seq, aw):
    i = pl.program_id(0)
    tp = p_ref.shape[0]
    in_ctx = i >= n_lat_tiles
    pos = jnp.where(in_ctx, (i - n_lat_tiles) % ctx_tiles_per_seq, i % lat_tiles_per_seq)
    last = jnp.where(in_ctx, ctx_tiles_per_seq - 1, lat_tiles_per_seq - 1)
    is_first = pos == 0
    is_last = pos == last
    rowid = lax.broadcasted_iota(jnp.int32, (tp, 1), 0)

    def shifted(c0, c1):
        x = p_ref[:, c0:c1]
        prow = jnp.where(is_first, 0.0, pp_ref[SUBLANE - 1:SUBLANE, c0:c1])
        nrow = jnp.where(is_last, 0.0, pn_ref[0:1, c0:c1])
        xp = jnp.where(rowid == 0, prow, pltpu.roll(x, 1, 0))
        xn = jnp.where(rowid == tp - 1, nrow, pltpu.roll(x, tp - 1, 0))
        return x + mu_ref[0:1, c0:c1] * (xp - x) + mu_ref[1:2, c0:c1] * (xn - x)

    e_bf = e_ref[...]
    r = shifted(0, aw)
    k = shifted(aw, 2 * aw)
    v = shifted(2 * aw, 3 * aw)
    gd = shifted(3 * aw, 3 * aw + A_LORA_G)
    lo0 = 3 * aw + A_LORA_G
    xl = shifted(lo0, lo0 + 2 * A_LORA_W + 2 * A_LORA_A)

    r_o[...] = r
    v_o[...] = v
    g_o[...] = _dot(_bf(_sigmoid(gd)), g2_ref[...])

    kx = k * kk_ref[...]
    kn = jnp.sqrt(_head_sum(kx * kx, e_bf))
    kk = kx / jnp.maximum(kn, 1e-12)
    kk_o[...] = kk

    xl_t = _bf(jnp.tanh(xl))
    xl_b = _bf(xl)
    kd_sum = None
    for d, (lw_o, kd_o, b_o) in enumerate(((lw0_o, kd0_o, b0_o), (lw1_o, kd1_o, b1_o))):
        w_log = -_softplus(-(w0_ref[d:d + 1, :] + _dot(xl_t, w2_ref[d]))) - 0.5
        lw_o[...] = -jnp.exp(w_log)
        asig = _sigmoid(a0_ref[d:d + 1, :] + _dot(xl_b, a2_ref[d]))
        kd = k * (1.0 + (asig - 1.0) * ka_ref[...])
        kd_o[...] = kd
        b_o[...] = kk * asig
        kd_sum = kd if kd_sum is None else kd_sum + kd
    bonus_o[...] = _head_sum(r * rk_ref[...] * kd_sum, e_bf) * v


def _rwkv_prep(pa, mu_p, w0, a0, w2p, a2p, g2, k_k, k_a, r_k, e_bf, *, n_lat, seq, ctx_len):
    r = pa.shape[0]
    ncol = mu_p.shape[1]
    aw = w0.shape[1]
    tp = PREP_TILE
    hb = tp // SUBLANE
    nblk8 = r // SUBLANE
    lw = 2 * A_LORA_W + 2 * A_LORA_A
    row = lambda a: a.reshape(1, aw)
    full = lambda shape: pl.BlockSpec(shape, lambda i: (0,) * len(shape))
    out_spec = pl.BlockSpec((tp, aw), lambda i: (i, 0))
    out_sds = jax.ShapeDtypeStruct((r, aw), F32)
    return pl.pallas_call(
        functools.partial(_rwkv_prep_kernel, n_lat_tiles=n_lat // tp, lat_tiles_per_seq=seq // tp,
                          ctx_tiles_per_seq=ctx_len // tp, aw=aw),
        grid=(r // tp,),
        in_specs=[
            pl.BlockSpec((tp, ncol), lambda i: (i, 0)),
            pl.BlockSpec((SUBLANE, ncol), lambda i: (jnp.maximum(i * hb - 1, 0), 0)),
            pl.BlockSpec((SUBLANE, ncol), lambda i: (jnp.minimum((i + 1) * hb, nblk8 - 1), 0)),
            full((2, ncol)), full((2, aw)), full((2, aw)),
            full((2, lw, aw)), full((2, lw, aw)), full((A_LORA_G, aw)),
            full((1, aw)), full((1, aw)), full((1, aw)), full((GROUP_W, GROUP_W)),
        ],
        out_specs=[out_spec] * 11,
        out_shape=[out_sds] * 11,
        compiler_params=_cparams(("parallel",)),
    )(pa, pa, pa, mu_p, w0, a0, w2p, a2p, g2, row(k_k), row(k_a), row(r_k), e_bf)


def _rwkv_masks(reverse):
    L = CHUNK
    gw = GROUP_W
    row = lax.broadcasted_iota(jnp.int32, (gw, gw), 0)
    col = lax.broadcasted_iota(jnp.int32, (gw, gw), 1)
    same_head = (row >> 6) == (col >> 6)
    tr = row & (L - 1)
    tc = col & (L - 1)
    strict = same_head & ((tr < tc) if reverse else (tr > tc))
    incl = same_head & ((tr <= tc) if reverse else (tr >= tc))
    t_r = lax.broadcasted_iota(jnp.int32, (L, L), 0)
    t_c = lax.broadcasted_iota(jnp.int32, (L, L), 1)
    tri = _bf(jnp.where((t_c >= t_r) if reverse else (t_c <= t_r), 1.0, 0.0))
    return dict(same_head=same_head, strict=strict, incl=incl, eye=row == col, tri=tri)


def _rwkv_chunks(probs):
    L = CHUNK
    gw = GROUP_W
    masks = {rv: _rwkv_masks(rv) for rv in sorted({p[7] for p in probs})}
    mk = [masks[p[7]] for p in probs]
    each = lambda f, *cols: [f(*args) for args in zip(*cols)]
    rep = lambda x: jnp.concatenate([x] * HEADS_PER_GROUP, axis=0)
    nat = lambda x: x[0:L] + x[L:2 * L] + x[2 * L:3 * L] + x[3 * L:4 * L]

    c = [_dot_exact_lhs(m["tri"], p[3]) for m, p in zip(mk, probs)]
    c_l = [ci[(0 if p[7] else L - 1):(1 if p[7] else L), :] for ci, p in zip(c, probs)]

    at_bd, rt_bd, v_bf, bend_bd, kend_bd, lhs, rhs, e_l = [], [], [], [], [], [], [], []
    for (r, v, kk, lw, kd, b, _, _), m, ci, cl in zip(probs, mk, c, c_l):
        bd = lambda x, sh=m["same_head"]: jnp.where(sh, rep(x), 0.0)
        e_nc = jnp.exp(-ci)
        e_end = jnp.exp(cl - ci)
        a_ = bd(-kk * jnp.exp(ci - lw))
        r_ = bd(r * jnp.exp(ci))
        at_bd.append(a_)
        rt_bd.append(r_)
        v_bf.append(_bf(bd(v)))
        bend_bd.append(_bf(bd(b * e_end)))
        kend_bd.append(_bf(bd(kd * e_end)))
        lhs.append(_bf(jnp.concatenate([a_, r_], axis=0)))
        rhs.append(_bf(jnp.concatenate([rep(b * e_nc), rep(kd * e_nc)], axis=0)))
        e_l.append(jnp.exp(cl))

    s = each(_dot_nt, lhs, rhs)
    a_ab = [jnp.where(m["strict"], si[:gw, :gw], 0.0) for m, si in zip(mk, s)]
    a_ak = [_bf(jnp.where(m["strict"], si[:gw, gw:], 0.0)) for m, si in zip(mk, s)]
    r_b = [_bf(jnp.where(m["incl"], si[gw:, :gw], 0.0)) for m, si in zip(mk, s)]
    r_k = [_bf(jnp.where(m["incl"], si[gw:, gw:], 0.0)) for m, si in zip(mk, s)]

    akv = each(_dot, a_ak, v_bf)
    t = [jnp.where(m["eye"], 1.0, 0.0) + a for m, a in zip(mk, a_ab)]
    a_pow = a_ab
    for _ in range(5):
        a_pow_bf = [_bf(a) for a in a_pow]
        a_pow = each(_dot, a_pow_bf, a_pow_bf)
        t = [ti + _dot(_bf(ti), _bf(ap)) for ti, ap in zip(t, a_pow)]
    pq = [_dot(_bf(ti), _bf(jnp.concatenate([a_, ak], axis=1)))
          for ti, a_, ak in zip(t, at_bd, akv)]
    p_bf = [_bf(x[:, :gw]) for x in pq]
    q_bf = [_bf(x[:, gw:]) for x in pq]

    m_ = [jnp.where(m["eye"], el, 0.0) + _dot_tn(be, p)
          for m, el, be, p in zip(mk, e_l, bend_bd, p_bf)]
    n_ = [_dot_tn(be, q) + _dot_tn(ke, v) for be, q, ke, v in zip(bend_bd, q_bf, kend_bd, v_bf)]
    g_ = [r_ + _dot(rb, p) for r_, rb, p in zip(rt_bd, r_b, p_bf)]
    y0 = [_dot(rb, q) + _dot(rk, v) for rb, q, rk, v in zip(r_b, q_bf, r_k, v_bf)]

    h_bf = [_bf(p[6]) for p in probs]
    y = [_dot(_bf(nat(g)), h) + nat(y0i) for g, h, y0i in zip(g_, h_bf, y0)]
    h_new = [_dot(_bf(mi), h) + ni for mi, h, ni in zip(m_, h_bf, n_)]
    return list(zip(y, h_new))


def _rwkv_scan_kernel(rf, vf, kkf, lwf, kdf, bf_, rb, vb, kkb, lwb, kdb, bb,
                      yf_ref, yb_ref, hf_ref, hb_ref):
    @pl.when(pl.program_id(1) == 0)
    def _():
        hf_ref[...] = jnp.zeros_like(hf_ref)
        hb_ref[...] = jnp.zeros_like(hb_ref)

    gs = hf_ref.shape[0]
    probs = []
    for gi in range(gs):
        sl = slice(gi * GROUP_W, (gi + 1) * GROUP_W)
        probs.append((rf[:, sl], vf[:, sl], kkf[:, sl], lwf[:, sl], kdf[:, sl], bf_[:, sl],
                      hf_ref[gi], False))
        probs.append((rb[:, sl], vb[:, sl], kkb[:, sl], lwb[:, sl], kdb[:, sl], bb[:, sl],
                      hb_ref[gi], True))
    res = _rwkv_chunks(probs)
    for gi in range(gs):
        sl = slice(gi * GROUP_W, (gi + 1) * GROUP_W)
        yf_ref[:, sl], hf_ref[gi] = res[2 * gi]
        yb_ref[:, sl], hb_ref[gi] = res[2 * gi + 1]


def _chunk_row_block(b, i, *, reverse, n_lat, seq, ctx_len):
    nc = ctx_len // CHUNK
    nl = seq // CHUNK
    ci = (nc - 1 - i) if reverse else i
    li = (nl - 1 - (i - nc)) if reverse else (i - nc)
    return jnp.where(i < nc, (n_lat + b * ctx_len) // CHUNK + ci, b * nl + li)


def _rwkv_scan(r, v, kk, lw0, lw1, kd0, kd1, b0, b1, *, batch, n_lat, seq, ctx_len):
    rows, aw = r.shape
    gs = RWKV_GROUPS_PER_STEP
    ng = aw // (GROUP_W * gs)
    geo = dict(n_lat=n_lat, seq=seq, ctx_len=ctx_len)

    def spec(reverse):
        return pl.BlockSpec(
            (CHUNK, GROUP_W * gs),
            lambda pg, i: (_chunk_row_block(pg // ng, i, reverse=reverse, **geo), pg % ng))

    sf, sb = spec(False), spec(True)
    sds = jax.ShapeDtypeStruct((rows, aw), F32)
    return pl.pallas_call(
        _rwkv_scan_kernel,
        grid=(batch * ng, (seq + ctx_len) // CHUNK),
        in_specs=[sf] * 6 + [sb] * 6,
        out_specs=[sf, sb],
        out_shape=[sds, sds],
        scratch_shapes=[pltpu.VMEM((gs, GROUP_W, GROUP_W), F32),
                        pltpu.VMEM((gs, GROUP_W, GROUP_W), F32)],
        compiler_params=_cparams(("parallel", "arbitrary")),
    )(r, v, kk, lw0, kd0, b0, r, v, kk, lw1, kd1, b1)


def _gla_scan_kernel(qkf, vf, gdf, qkb, vb, gdb, gw2_ref, gb_ref, of_ref, ob_ref, sf_ref, sb_ref,
                     *, dk, dv):
    @pl.when(pl.program_id(1) == 0)
    def _():
        sf_ref[...] = jnp.zeros_like(sf_ref)
        sb_ref[...] = jnp.zeros_like(sb_ref)

    L = CHUNK
    wk = B_HEADS * dk
    t_r = lax.broadcasted_iota(jnp.int32, (L, L), 0)
    t_c = lax.broadcasted_iota(jnp.int32, (L, L), 1)

    qh, kh, keh, vh, dech, keeph, sth = [], [], [], [], [], [], []
    for d, (qk_ref, v_ref, gd_ref, st_ref) in enumerate(((qkf, vf, gdf, sf_ref),
                                                         (qkb, vb, gdb, sb_ref))):
        reverse = d == 1
        keep = (t_c >= t_r) if reverse else (t_c <= t_r)
        tri = _bf(jnp.where(keep, 1.0, 0.0))
        z = _dot(_bf(gd_ref[...]), gw2_ref[d]) + gb_ref[d:d + 1, :]
        la = -_softplus(-z) * (1.0 / B_GATE_TAU)
        cum = _dot_exact_lhs(tri, la)
        last = 0 if reverse else L - 1
        cl = cum[last:last + 1, :]
        q_in = _bf(qk_ref[:, :wk] * (dk ** -0.5) * jnp.exp(cum))
        k_in = _bf(qk_ref[:, wk:] * jnp.exp(-cum))
        k_end = _bf(qk_ref[:, wk:] * jnp.exp(cl - cum))
        dec = jnp.exp(cl)
        for h in range(B_HEADS):
            ks = slice(h * dk, (h + 1) * dk)
            qh.append(q_in[:, ks])
            kh.append(k_in[:, ks])
            keh.append(k_end[:, ks])
            vh.append(_bf(v_ref[:, h * dv:(h + 1) * dv]))
            dech.append(dec[:, ks])
            keeph.append(keep)
            sth.append(st_ref[h])

    sc = [_bf(jnp.where(kp, _dot_nt(q, k), 0.0)) for kp, q, k in zip(keeph, qh, kh)]
    inter = [_dot_nt(q, _bf(st)) for q, st in zip(qh, sth)]
    kvt = [_dot_tn(v, ke) for v, ke in zip(vh, keh)]
    out = [_dot(s_, v) + it for s_, v, it in zip(sc, vh, inter)]
    for d, (o_ref, st_ref) in enumerate(((of_ref, sf_ref), (ob_ref, sb_ref))):
        for h in range(B_HEADS):
            i = d * B_HEADS + h
            o_ref[:, h * dv:(h + 1) * dv] = out[i]
            st_ref[h] = sth[i] * dech[i] + kvt[i]


def _gla_scan(pb, gw2p, gb, *, batch, n_lat, seq, ctx_len, wk, wv, col0):
    rows = pb.shape[0]
    dk, dv = wk // B_HEADS, wv // B_HEADS
    geo = dict(n_lat=n_lat, seq=seq, ctx_len=ctx_len)
    gd_blk = (col0 + 2 * wk + 2 * wv) // LANE

    def specs(reverse):
        rb = lambda b, i: _chunk_row_block(b, i, reverse=reverse, **geo)
        return [
            pl.BlockSpec((CHUNK, 2 * wk), lambda b, i: (rb(b, i), col0 // (2 * wk))),
            pl.BlockSpec((CHUNK, wv), lambda b, i: (rb(b, i), (col0 + 2 * wk) // wv)),
            pl.BlockSpec((CHUNK, LANE), lambda b, i: (rb(b, i), gd_blk)),
        ]

    def out_spec(reverse):
        return pl.BlockSpec((CHUNK, wv),
                            lambda b, i: (_chunk_row_block(b, i, reverse=reverse, **geo), 0))

    sds = jax.ShapeDtypeStruct((rows, wv), F32)
    return pl.pallas_call(
        functools.partial(_gla_scan_kernel, dk=dk, dv=dv),
        grid=(batch, (seq + ctx_len) // CHUNK),
        in_specs=specs(False) + specs(True) + [
            pl.BlockSpec((2, LANE, wk), lambda b, i: (0, 0, 0)),
            pl.BlockSpec((2, wk), lambda b, i: (0, 0)),
        ],
        out_specs=[out_spec(False), out_spec(True)],
        out_shape=[sds, sds],
        scratch_shapes=[pltpu.VMEM((B_HEADS, dv, dk), F32), pltpu.VMEM((B_HEADS, dv, dk), F32)],
        compiler_params=_cparams(("parallel", "arbitrary")),
    )(pb, pb, pb, pb, pb, pb, gw2p, gb)


def _mixer_out_kernel(yf, yb, bonus, g, of, ob, gate, x_ref, mod_ref, lnw, lnb, gn, e_ref, w_ref,
                      o_ref, h_ref, *, aw, dv):
    e_bf = e_ref[...]
    y = yf[...] + yb[...]
    inv = 1.0 / A_HEAD_DIM
    mean = _head_sum(y, e_bf) * inv
    dlt = y - mean
    var = _head_sum(dlt * dlt, e_bf) * inv
    yn = dlt * lax.rsqrt(var + A_LN_EPS) * lnw[...] + lnb[...]
    h_ref[:, :aw] = _bf((yn + bonus[...]) * g[...])

    o = of[...] + ob[...]
    gt = _silu(gate[...])
    for h in range(B_HEADS):
        sl = slice(h * dv, (h + 1) * dv)
        oh = o[:, sl]
        ms = jnp.mean(oh * oh, axis=-1, keepdims=True)
        h_ref[:, aw + h * dv:aw + (h + 1) * dv] = _bf(oh * lax.rsqrt(ms + NORM_EPS) * gn[...] * gt[:, sl])

    o_ref[...] = x_ref[...] + mod_ref[2:3, :] * _dot(h_ref[...], w_ref[...])


def _mixer_out(yf, yb, bonus, g, of, ob, pb, x, mods_l, ln_w, ln_b, gla_norm, e_bf, w_bf,
               seg_of_tile, *, wk, col0):
    rows, d = x.shape
    aw = yf.shape[1]
    wv = of.shape[1]
    dv = wv // B_HEADS
    tm = 256
    tiles_per_row_tile = ROW_TILE // tm
    rs = lambda w: pl.BlockSpec((tm, w), lambda i: (i, 0))
    full = lambda shape: pl.BlockSpec(shape, lambda i: (0,) * len(shape))
    return pl.pallas_call(
        functools.partial(_mixer_out_kernel, aw=aw, dv=dv),
        grid=(rows // tm,),
        in_specs=[
            rs(aw), rs(aw), rs(aw), rs(aw), rs(wv), rs(wv),
            pl.BlockSpec((tm, wv), lambda i: (i, (col0 + 2 * wk + wv) // wv)),
            rs(d),
            pl.BlockSpec((None, N_MOD, d), lambda i: (seg_of_tile(i // tiles_per_row_tile), 0, 0)),
            full((1, aw)), full((1, aw)), full((1, dv)), full((GROUP_W, GROUP_W)),
            full((aw + wv, d)),
        ],
        out_specs=rs(d),
        out_shape=jax.ShapeDtypeStruct((rows, d), F32),
        scratch_shapes=[pltpu.VMEM((tm, aw + wv), BF16)],
        compiler_params=_cparams(("parallel",)),
    )(yf, yb, bonus, g, of, ob, pb, x, mods_l, ln_w.reshape(1, aw), ln_b.reshape(1, aw),
      gla_norm.reshape(1, dv), e_bf, w_bf)


def _lru_kernel(*refs, reverse, n_blk, combine):
    if combine:
        (x_ref, xp_ref, xn_ref, h0_ref, cw_ref, cb_ref, wc_ref, ba_ref, bx_ref, lam_ref,
         hf_ref, gate_ref, out_ref, hl_ref, a_sc, u_sc, hs_sc, carry_sc) = refs
    else:
        (x_ref, xp_ref, xn_ref, h0_ref, cw_ref, cb_ref, wc_ref, ba_ref, bx_ref, lam_ref,
         out_ref, hl_ref, a_sc, u_sc, hs_sc, carry_sc) = refs
    j = pl.program_id(2)
    blk = (n_blk - 1 - j) if reverse else j
    lb, ct = x_ref.shape
    cblk = wc_ref.shape[1]
    is_first = blk == 0
    is_last = blk == n_blk - 1
    rowid = lax.broadcasted_iota(jnp.int32, (lb, 1), 0)

    x = x_ref[...]
    p6 = jnp.where(is_first, 0.0, xp_ref[SUBLANE - 2:SUBLANE - 1, :])
    p7 = jnp.where(is_first, 0.0, xp_ref[SUBLANE - 1:SUBLANE, :])
    n0 = jnp.where(is_last, 0.0, xn_ref[0:1, :])
    xm1 = jnp.where(rowid == 0, p7, pltpu.roll(x, 1, 0))
    xm2 = jnp.where(rowid == 0, p6, jnp.where(rowid == 1, p7, pltpu.roll(x, 2, 0)))
    xp1 = jnp.where(rowid == lb - 1, n0, pltpu.roll(x, lb - 1, 0))
    xs = (cb_ref[...] + cw_ref[0:1, :] * xm2 + cw_ref[1:2, :] * xm1 + cw_ref[2:3, :] * x
          + cw_ref[3:4, :] * xp1)

    for n in range(ct // cblk):
        sl = slice(n * cblk, (n + 1) * cblk)
        xb = xs[:, sl]
        ri = _dot(_bf(xb), wc_ref[n])
        rg = _sigmoid(ri[:, :cblk] + ba_ref[:, sl])
        ig = _sigmoid(ri[:, cblk:] + bx_ref[:, sl])
        log_a = -C_CONST * rg * _softplus(-lam_ref[:, sl])
        a = jnp.exp(log_a)
        a_sc[:, sl] = a
        u_sc[:, sl] = jnp.sqrt(1.0 - a * a) * (ig * xb)

    @pl.when(j == 0)
    def _():
        carry_sc[...] = h0_ref[0:1, :]

    def step(t, h):
        tt = (lb - 1 - t) if reverse else t
        h = a_sc[pl.ds(tt, 1), :] * h + u_sc[pl.ds(tt, 1), :]
        hs_sc[pl.ds(tt, 1), :] = h
        return h

    h = lax.fori_loop(0, lb, step, carry_sc[...], unroll=8)
    carry_sc[...] = h
    hl_ref[...] = jnp.broadcast_to(h, hl_ref.shape)
    if combine:
        out_ref[...] = (hs_sc[...] + hf_ref[...]) * jax.nn.gelu(gate_ref[...])
    else:
        out_ref[...] = hs_sc[...]


def _lru_pass(x_arr, x_specs, h0, conv_w, conv_b, wcat, ba, bx, lam, out_shape, out_spec, extra,
              *, batch, n_blk, lb, reverse):
    c = conv_w.shape[1]
    ct = 1024
    nct = c // ct
    cblk = wcat.shape[1]
    chan = lambda rows_: pl.BlockSpec((rows_, ct), lambda b, k, j: (0, k))
    in_specs = list(x_specs) + [
        pl.BlockSpec((SUBLANE, ct), lambda b, k, j: (b, k)),
        chan(C_CONV), chan(1),
        pl.BlockSpec((ct // cblk, cblk, 2 * cblk), lambda b, k, j: (k, 0, 0)),
        chan(1), chan(1), chan(1),
    ] + [s for _, s in extra]
    args = [x_arr, x_arr, x_arr, h0, conv_w, conv_b.reshape(1, c), wcat, ba.reshape(1, c),
            bx.reshape(1, c), lam.reshape(1, c)] + [a for a, _ in extra]
    return pl.pallas_call(
        functools.partial(_lru_kernel, reverse=reverse, n_blk=n_blk, combine=bool(extra)),
        grid=(batch, nct, n_blk),
        in_specs=in_specs,
        out_specs=[out_spec, pl.BlockSpec((SUBLANE, ct), lambda b, k, j: (b, k))],
        out_shape=[out_shape, jax.ShapeDtypeStruct((batch * SUBLANE, c), F32)],
        scratch_shapes=[pltpu.VMEM((lb, ct), F32), pltpu.VMEM((lb, ct), F32),
                        pltpu.VMEM((lb, ct), F32), pltpu.VMEM((1, ct), F32)],
        compiler_params=_cparams(("parallel", "parallel", "arbitrary")),
    )(*args)


def _rglru(hw_ctx, gate_grid, x_grid, conv_w, conv_b, wa, ba, wx, bx, lam, *, batch, n_lat, seq,
           ctx_len):
    c = conv_w.shape[1]
    ct = 1024
    rows = seq // GRID_W
    lbc = 128
    ncb = ctx_len // lbc
    xcol = c // ct
    per_col = c // ct

    wcat = [_bf(jnp.concatenate([wa[d], wx[d]], axis=-1)) for d in range(2)]
    zeros_state = jnp.zeros((batch * SUBLANE, c), F32)

    def ctx_specs(reverse):
        blk = (lambda j: ncb - 1 - j) if reverse else (lambda j: j)
        per8 = lbc // SUBLANE
        cur = pl.BlockSpec((lbc, ct), lambda b, k, j: (b * ncb + blk(j), xcol + k))
        prv = pl.BlockSpec((SUBLANE, ct), lambda b, k, j: (
            b * ncb * per8 + jnp.maximum(blk(j) * per8 - 1, 0), xcol + k))
        nxt = pl.BlockSpec((SUBLANE, ct), lambda b, k, j: (
            b * ncb * per8 + jnp.minimum((blk(j) + 1) * per8, ncb * per8 - 1), xcol + k))
        return cur, prv, nxt

    def lat_specs(reverse):
        blk = (lambda j: GRID_W - 1 - j) if reverse else (lambda j: j)
        r8 = rows // SUBLANE
        cur = pl.BlockSpec((rows, ct), lambda b, k, j: (b, blk(j) * per_col + k))
        prv = pl.BlockSpec((SUBLANE, ct), lambda b, k, j: (
            b * r8 + r8 - 1, jnp.maximum(blk(j) - 1, 0) * per_col + k))
        nxt = pl.BlockSpec((SUBLANE, ct), lambda b, k, j: (
            b * r8, jnp.minimum(blk(j) + 1, GRID_W - 1) * per_col + k))
        return cur, prv, nxt

    common = dict(batch=batch)
    ctx_scratch_sds = jax.ShapeDtypeStruct((batch * ctx_len, c), F32)
    ctx_out = lambda reverse: pl.BlockSpec(
        (lbc, ct), lambda b, k, j: (b * ncb + ((ncb - 1 - j) if reverse else j), k))
    _, st = _lru_pass(hw_ctx, ctx_specs(False), zeros_state, conv_w, conv_b, wcat[0], ba[0], bx[0],
                      lam[0], ctx_scratch_sds, ctx_out(False), [], n_blk=ncb, lb=lbc,
                      reverse=False, **common)
    hf_sds = jax.ShapeDtypeStruct((n_lat, c), F32)
    hf_spec = lambda reverse: pl.BlockSpec(
        (rows, ct), lambda b, k, j: (b * GRID_W + ((GRID_W - 1 - j) if reverse else j), k))
    hf, _ = _lru_pass(x_grid, lat_specs(False), st, conv_w, conv_b, wcat[0], ba[0], bx[0], lam[0],
                      hf_sds, hf_spec(False), [], n_blk=GRID_W, lb=rows, reverse=False, **common)
    _, st = _lru_pass(hw_ctx, ctx_specs(True), zeros_state, conv_w, conv_b, wcat[1], ba[1], bx[1],
                      lam[1], ctx_scratch_sds, ctx_out(True), [], n_blk=ncb, lb=lbc,
                      reverse=True, **common)
    grid_spec = pl.BlockSpec((rows, ct), lambda b, k, j: (b, (GRID_W - 1 - j) * per_col + k))
    y_sds = jax.ShapeDtypeStruct((n_lat // GRID_W, GRID_W * c), F32)
    y, _ = _lru_pass(x_grid, lat_specs(True), st, conv_w, conv_b, wcat[1], ba[1], bx[1], lam[1],
                     y_sds, grid_spec, [(hf, hf_spec(True)), (gate_grid, grid_spec)],
                     n_blk=GRID_W, lb=rows, reverse=True, **common)
    return y


def _pad_cols(w, n):
    return jnp.pad(w, ((0, 0),) * (w.ndim - 1) + ((0, n - w.shape[-1]),))


def kernel(x, c, ctx, c_ctx, mod_w, mod_b, norm1, norm2, mlp_w1, mlp_w2, ab_w_in, ab_w_out, rw_mu,
           rw_w0, rw_w2, rw_a0, rw_a2, rw_g2, rw_kk, rw_ka, rw_rk, rw_ln_w, rw_ln_b, gla_gw2,
           gla_gb, gla_norm, lru_w_in, lru_w_out, lru_conv_w, lru_conv_b, lru_wa, lru_ba, lru_wx,
           lru_bx, lru_lam, final_norm):
    batch, seq, d = x.shape
    ctx_len = ctx.shape[1]
    depth = mod_w.shape[0]
    n_lat = batch * seq
    aw = rw_w0.shape[-1]
    wk = gla_gb.shape[-1]
    wv = ab_w_out.shape[1] - aw
    a_cols = rw_mu.shape[-1]
    b_cols = ab_w_in.shape[-1] - a_cols
    assert depth == 2 and batch + 1 <= SUBLANE
    assert seq % ROW_TILE == 0 and (batch * ctx_len) % ROW_TILE == 0
    assert ctx_len % PREP_TILE == 0 and ctx_len % 128 == 0 and seq % (GRID_W * SUBLANE) == 0
    assert aw % GROUP_W == 0 and wv == aw and 2 * wk == wv

    tiles_per_seq = seq // ROW_TILE
    seg_of_tile = lambda i: jnp.minimum(i // tiles_per_seq, batch)

    xs = jnp.concatenate([x.reshape(n_lat, d), ctx.reshape(batch * ctx_len, d)], axis=0)
    cvec = jnp.concatenate(
        [c, c_ctx[None, :], jnp.zeros((SUBLANE - batch - 1, d), F32)], axis=0)
    mods = _mods(cvec, mod_w, mod_b)

    hid = jnp.arange(GROUP_W) // A_HEAD_DIM
    e_bf = _bf(hid[:, None] == hid[None, :])

    a_pad = -(-a_cols // LANE) * LANE
    col0 = -(-a_cols // (2 * wk)) * (2 * wk)
    n_tiles = 3
    tn = -(-(col0 + b_cols) // (n_tiles * LANE)) * LANE
    w_ab = _bf(jnp.concatenate(
        [_pad_cols(ab_w_in[0][:, :a_cols], col0),
         _pad_cols(ab_w_in[0][:, a_cols:], n_tiles * tn - col0)], axis=1))
    pa = pb = _norm_matmul(xs, norm1[0], mods[0], w_ab, seg_of_tile, tn=tn)

    lora_w = 2 * A_LORA_W + 2 * A_LORA_A
    w2p = jnp.zeros((2, lora_w, aw), F32)
    a2p = jnp.zeros((2, lora_w, aw), F32)
    for dd in range(2):
        w2p = w2p.at[dd, dd * A_LORA_W:(dd + 1) * A_LORA_W].set(rw_w2[0, dd])
        a2p = a2p.at[dd, 2 * A_LORA_W + dd * A_LORA_A:2 * A_LORA_W + (dd + 1) * A_LORA_A].set(
            rw_a2[0, dd])
    (r_, v_, kk_, lw0, lw1, kd0, kd1, b0, b1, g_, bonus) = _rwkv_prep(
        pa, _pad_cols(rw_mu[0], a_pad), rw_w0[0], rw_a0[0], _bf(w2p), _bf(a2p), _bf(rw_g2[0]),
        rw_kk[0], rw_ka[0], rw_rk[0].reshape(-1), e_bf, n_lat=n_lat, seq=seq, ctx_len=ctx_len)
    yf, yb = _rwkv_scan(r_, v_, kk_, lw0, lw1, kd0, kd1, b0, b1, batch=batch, n_lat=n_lat, seq=seq,
                        ctx_len=ctx_len)

    gw2p = jnp.zeros((2, LANE, wk), F32)
    for dd in range(2):
        gw2p = gw2p.at[dd, dd * B_LORA:(dd + 1) * B_LORA].set(gla_gw2[0, dd])
    of, ob = _gla_scan(pb, _bf(gw2p), gla_gb[0], batch=batch, n_lat=n_lat, seq=seq,
                       ctx_len=ctx_len, wk=wk, wv=wv, col0=col0)

    xs = _mixer_out(yf, yb, bonus, g_, of, ob, pb, xs, mods[0], rw_ln_w[0], rw_ln_b[0],
                    gla_norm[0], e_bf, _bf(ab_w_out[0]), seg_of_tile, wk=wk, col0=col0)
    xs = _mlp(xs, xs.shape[0], norm2[0], mods[0], _bf(mlp_w1[0]), _bf(mlp_w2[0]), final_norm,
              seg_of_tile, final_norm=False)

    w_lru = _bf(lru_w_in[0])
    hw_ctx = _norm_matmul(xs, norm1[1], mods[1], w_lru, seg_of_tile, tn=2048,
                          tile0=n_lat // ROW_TILE, n_tiles=(batch * ctx_len) // ROW_TILE)
    gate_grid, x_grid = _norm_matmul_grid(xs, n_lat, norm1[1], mods[1], w_lru, seg_of_tile)
    y = _rglru(hw_ctx, gate_grid, x_grid, lru_conv_w[0], lru_conv_b[0], lru_wa[0], lru_ba[0],
               lru_wx[0], lru_bx[0], lru_lam[0], batch=batch, n_lat=n_lat, seq=seq,
               ctx_len=ctx_len)
    xl = _proj_res(y, xs, n_lat, mods[1], _bf(lru_w_out[0]), seg_of_tile)
    out = _mlp(xl, n_lat, norm2[1], mods[1], _bf(mlp_w1[1]), _bf(mlp_w2[1]), final_norm,
               seg_of_tile, final_norm=True)
    return out.reshape(batch, seq, d)
```

```python
import functools

import jax
import jax.numpy as jnp
from jax import lax
from jax.experimental import pallas as pl
from jax.experimental.pallas import tpu as pltpu

F32 = jnp.float32
BF16 = jnp.bfloat16

NORM_EPS = 1e-6
GRID_W = 64
N_MOD = 6

A_HEAD_DIM = 64
A_LORA_W = 96
A_LORA_A = 96
A_LORA_G = 256
A_LN_EPS = 64e-5
HEADS_PER_GROUP = 4
GROUP_W = HEADS_PER_GROUP * A_HEAD_DIM
CHUNK = 64
RWKV_GROUPS_PER_STEP = 4

B_HEADS = 4
B_LORA = 16
B_GATE_TAU = 16.0

C_BLOCKS = 8
C_CONV = 4
C_CONST = 8.0

LANE = 128
SUBLANE = 8
ROW_TILE = 512
MLP_TILE = 1024
PREP_TILE = 128
VMEM_LIMIT = 56 * 1024 * 1024


def _cparams(sem):
    return pltpu.CompilerParams(dimension_semantics=sem, vmem_limit_bytes=VMEM_LIMIT)


def _bf(x):
    return x.astype(BF16)


def _dot(a, b):
    return jnp.dot(a, b, preferred_element_type=F32)


def _dot_nt(a, b):
    return lax.dot_general(a, b, (((1,), (1,)), ((), ())), preferred_element_type=F32)


def _dot_tn(a, b):
    return lax.dot_general(a, b, (((0,), (0,)), ((), ())), preferred_element_type=F32)


def _split3(x):
    hi = _bf(x)
    r1 = x - hi.astype(F32)
    mid = _bf(r1)
    lo = _bf(r1 - mid.astype(F32))
    return hi, mid, lo


def _dot_exact_lhs(m_bf, x):
    hi, mid, lo = _split3(x)
    return _dot(m_bf, hi) + _dot(m_bf, mid) + _dot(m_bf, lo)


def _dot_exact_rhs(x, m_bf):
    hi, mid, lo = _split3(x)
    return _dot(hi, m_bf) + _dot(mid, m_bf) + _dot(lo, m_bf)


def _softplus(x):
    return jnp.maximum(x, 0.0) + jnp.log1p(jnp.exp(-jnp.abs(x)))


def _sigmoid(x):
    return jax.nn.sigmoid(x)


def _silu(x):
    return x * _sigmoid(x)


def _norm_mod(x, g, shift, scale):
    ms = jnp.mean(x * x, axis=-1, keepdims=True)
    return (x * lax.rsqrt(ms + NORM_EPS) * g) * (1.0 + scale) + shift


def _mods_kernel(c_ref, w_ref, b_ref, o_ref):
    s = _silu(c_ref[...])
    o_ref[...] = _dot(_bf(s), _bf(w_ref[...])) + b_ref[...]


def _mods(cvec, mod_w, mod_b):
    depth, d, n = mod_w.shape
    tn = 1024
    out = pl.pallas_call(
        _mods_kernel,
        grid=(depth, n // tn),
        in_specs=[
            pl.BlockSpec((SUBLANE, d), lambda l, j: (0, 0)),
            pl.BlockSpec((None, d, tn), lambda l, j: (l, 0, j)),
            pl.BlockSpec((None, 1, tn), lambda l, j: (l, 0, j)),
        ],
        out_specs=pl.BlockSpec((None, SUBLANE, tn), lambda l, j: (l, 0, j)),
        out_shape=jax.ShapeDtypeStruct((depth, SUBLANE, n), F32),
        compiler_params=_cparams(("parallel", "parallel")),
    )(cvec, mod_w, mod_b.reshape(depth, 1, n))
    return out.reshape(depth, SUBLANE, N_MOD, d)


def _norm_matmul_kernel(x_ref, g_ref, mod_ref, w_ref, o_ref, h_ref, *, shift_row, scale_row):
    @pl.when(pl.program_id(1) == 0)
    def _():
        h = _norm_mod(x_ref[...], g_ref[...], mod_ref[shift_row:shift_row + 1, :],
                      mod_ref[scale_row:scale_row + 1, :])
        h_ref[...] = _bf(h)

    o_ref[...] = _dot(h_ref[...], w_ref[...])


def _norm_matmul(x, g, mods_l, w_bf, seg_of_tile, *, tn, tile0=0, n_tiles=None, shift_row=0,
                 scale_row=1):
    r, d = x.shape
    n = w_bf.shape[1]
    tm = ROW_TILE
    n_tiles = r // tm if n_tiles is None else n_tiles
    return pl.pallas_call(
        functools.partial(_norm_matmul_kernel, shift_row=shift_row, scale_row=scale_row),
        grid=(n_tiles, n // tn),
        in_specs=[
            pl.BlockSpec((tm, d), lambda i, j: (i + tile0, 0)),
            pl.BlockSpec((1, d), lambda i, j: (0, 0)),
            pl.BlockSpec((None, N_MOD, d), lambda i, j: (seg_of_tile(i + tile0), 0, 0)),
            pl.BlockSpec((d, tn), lambda i, j: (0, j)),
        ],
        out_specs=pl.BlockSpec((tm, tn), lambda i, j: (i, j)),
        out_shape=jax.ShapeDtypeStruct((n_tiles * tm, n), F32),
        scratch_shapes=[pltpu.VMEM((tm, d), BF16)],
        compiler_params=_cparams(("parallel", "arbitrary")),
    )(x, g.reshape(1, d), mods_l, w_bf)


def _grid_tile(rows):
    gr = min(rows, 64)
    gc = ROW_TILE // gr
    assert gc % SUBLANE == 0 and GRID_W % gc == 0 and rows % gr == 0
    return gr, gc


def _norm_matmul_grid_kernel(x_ref, g_ref, mod_ref, w_ref, o0_ref, o1_ref, h_ref, res_ref):
    j = pl.program_id(1)
    gr, gc, d = x_ref.shape

    @pl.when(j == 0)
    def _():
        h = _norm_mod(x_ref[...].reshape(gr * gc, d), g_ref[...], mod_ref[0:1, :], mod_ref[1:2, :])
        h_ref[...] = _bf(h)

    res = _dot(h_ref[...], w_ref[...])
    nh = w_ref.shape[1]
    nq = nh // LANE
    for q in range(nq):
        res_ref[q] = res[:, q * LANE:(q + 1) * LANE]

    def scatter(o_ref):
        for c in range(gc):
            for q in range(nq):
                o_ref[:, c * nh + q * LANE:c * nh + (q + 1) * LANE] = (
                    res_ref[q, pl.ds(c, gr, stride=gc), :])

    @pl.when(j == 0)
    def _():
        scatter(o0_ref)

    @pl.when(j == 1)
    def _():
        scatter(o1_ref)


def _norm_matmul_grid(x, n_lat, rows, g, mods_l, w_bf):
    d = x.shape[1]
    nh = w_bf.shape[1] // 2
    gr, gc = _grid_tile(rows)
    ncg = GRID_W // gc
    x3 = x.reshape(x.shape[0] // GRID_W, GRID_W, d)
    out_spec = pl.BlockSpec((gr, gc * nh), lambda i, j: (i // ncg, i % ncg))
    sds = jax.ShapeDtypeStruct((n_lat // GRID_W, GRID_W * nh), F32)
    return pl.pallas_call(
        _norm_matmul_grid_kernel,
        grid=(n_lat // ROW_TILE, 2),
        in_specs=[
            pl.BlockSpec((gr, gc, d), lambda i, j: (i // ncg, i % ncg, 0)),
            pl.BlockSpec((1, d), lambda i, j: (0, 0)),
            pl.BlockSpec((None, N_MOD, d), lambda i, j: ((i // ncg) * gr // rows, 0, 0)),
            pl.BlockSpec((d, nh), lambda i, j: (0, j)),
        ],
        out_specs=[out_spec, out_spec],
        out_shape=[sds, sds],
        scratch_shapes=[pltpu.VMEM((ROW_TILE, d), BF16),
                        pltpu.VMEM((nh // LANE, ROW_TILE, LANE), F32)],
        compiler_params=_cparams(("parallel", "arbitrary")),
    )(x3, g.reshape(1, d), mods_l, w_bf)


def _mlp_kernel(x_ref, g_ref, mod_ref, w1_ref, w2_ref, fn_ref, o_ref, h_ref, *, final_norm):
    j = pl.program_id(1)
    slab = 256
    n_slab = x_ref.shape[0] // slab

    @pl.when(j == 0)
    def _():
        for s in range(n_slab):
            rs = slice(s * slab, (s + 1) * slab)
            h_ref[rs, :] = _bf(_norm_mod(x_ref[rs, :], g_ref[...], mod_ref[3:4, :],
                                         mod_ref[4:5, :]))

    a = _dot(h_ref[...], w1_ref[...])
    a = _bf(jnp.square(jnp.maximum(a, 0.0)))
    nsplit = 2
    wn = o_ref.shape[1] // nsplit
    for s in range(nsplit):
        part = _dot(a, w2_ref[:, s * wn:(s + 1) * wn])

        @pl.when(j == 0)
        def _():
            o_ref[:, s * wn:(s + 1) * wn] = part

        @pl.when(j > 0)
        def _():
            o_ref[:, s * wn:(s + 1) * wn] += part

    @pl.when(j == pl.num_programs(1) - 1)
    def _():
        for s in range(n_slab):
            rs = slice(s * slab, (s + 1) * slab)
            xo = x_ref[rs, :] + mod_ref[5:6, :] * o_ref[rs, :]
            if final_norm:
                ms = jnp.mean(xo * xo, axis=-1, keepdims=True)
                xo = xo * lax.rsqrt(ms + NORM_EPS) * fn_ref[...]
            o_ref[rs, :] = xo


def _mlp(x, g, mods_l, w1_bf, w2_bf, fn, seg_of_tile, *, tm, tile0, n_tiles, final_norm):
    d = x.shape[1]
    dff = w1_bf.shape[1]
    tf = 512
    return pl.pallas_call(
        functools.partial(_mlp_kernel, final_norm=final_norm),
        grid=(n_tiles, dff // tf),
        in_specs=[
            pl.BlockSpec((tm, d), lambda i, j: (i + tile0, 0), pipeline_mode=pl.Buffered(1)),
            pl.BlockSpec((1, d), lambda i, j: (0, 0)),
            pl.BlockSpec((None, N_MOD, d), lambda i, j: (seg_of_tile(i + tile0), 0, 0)),
            pl.BlockSpec((d, tf), lambda i, j: (0, j)),
            pl.BlockSpec((tf, d), lambda i, j: (j, 0)),
            pl.BlockSpec((1, d), lambda i, j: (0, 0)),
        ],
        out_specs=pl.BlockSpec((tm, d), lambda i, j: (i, 0)),
        out_shape=jax.ShapeDtypeStruct((n_tiles * tm, d), F32),
        scratch_shapes=[pltpu.VMEM((tm, d), BF16)],
        compiler_params=_cparams(("parallel", "arbitrary")),
    )(x, g.reshape(1, d), mods_l, w1_bf, w2_bf, fn.reshape(1, d))


def _proj_res_kernel(y_ref, x_ref, mod_ref, w_ref, o_ref, yt_ref):
    k = w_ref.shape[0]
    gr, gc, d = x_ref.shape
    nq = k // LANE
    for c in range(gc):
        for q in range(nq):
            yt_ref[q, pl.ds(c, gr, stride=gc), :] = (
                y_ref[:, c * k + q * LANE:c * k + (q + 1) * LANE])
    yt = jnp.concatenate([_bf(yt_ref[q]) for q in range(nq)], axis=1)
    out = x_ref[...].reshape(gr * gc, d) + mod_ref[2:3, :] * _dot(yt, w_ref[...])
    o_ref[...] = out.reshape(gr, gc, d)


def _proj_res(y_grid, x, n_lat, rows, mods_l, w_bf):
    k, d = w_bf.shape
    gr, gc = _grid_tile(rows)
    ncg = GRID_W // gc
    x3 = x.reshape(x.shape[0] // GRID_W, GRID_W, d)
    tok_spec = pl.BlockSpec((gr, gc, d), lambda i: (i // ncg, i % ncg, 0))
    out = pl.pallas_call(
        _proj_res_kernel,
        grid=(n_lat // ROW_TILE,),
        in_specs=[
            pl.BlockSpec((gr, gc * k), lambda i: (i // ncg, i % ncg)),
            tok_spec,
            pl.BlockSpec((None, N_MOD, d), lambda i: ((i // ncg) * gr // rows, 0, 0)),
            pl.BlockSpec((k, d), lambda i: (0, 0)),
        ],
        out_specs=tok_spec,
        out_shape=jax.ShapeDtypeStruct((n_lat // GRID_W, GRID_W, d), F32),
        scratch_shapes=[pltpu.VMEM((k // LANE, ROW_TILE, LANE), F32)],
        compiler_params=_cparams(("parallel",)),
    )(y_grid, x3, mods_l, w_bf)
    return out.reshape(n_lat, d)


def _head_sum(z, e_bf):
    parts = []
    for gi in range(z.shape[1] // GROUP_W):
        zs = z[:, gi * GROUP_W:(gi + 1) * GROUP_W]
        parts.append(_dot_exact_rhs(zs, e_bf))
    return jnp.concatenate(parts, axis=1)


def _rwkv_prep_kernel(p_ref, pp_ref, pn_ref, mu_ref, w0_ref, a0_ref, w2_ref, a2_ref, g2_ref,
                      kk_ref, ka_ref, rk_ref, e_ref,
                      r_o, v_o, kk_o, lw0_o, lw1_o, kd0_o, kd1_o, b0_o, b1_o, g_o, bonus_o,
                      *, n_lat_tiles, lat_tiles_per_seq, ctx_tiles_per_seq, aw):
    i = pl.program_id(0)
    tp = p_ref.shape[0]
    in_ctx = i >= n_lat_tiles
    pos = jnp.where(in_ctx, (i - n_lat_tiles) % ctx_tiles_per_seq, i % lat_tiles_per_seq)
    last = jnp.where(in_ctx, ctx_tiles_per_seq - 1, lat_tiles_per_seq - 1)
    is_first = pos == 0
    is_last = pos == last
    rowid = lax.broadcasted_iota(jnp.int32, (tp, 1), 0)

    def shifted(c0, c1):
        x = p_ref[:, c0:c1]
        prow = jnp.where(is_first, 0.0, pp_ref[SUBLANE - 1:SUBLANE, c0:c1])
        nrow = jnp.where(is_last, 0.0, pn_ref[0:1, c0:c1])
        xp = jnp.where(rowid == 0, prow, pltpu.roll(x, 1, 0))
        xn = jnp.where(rowid == tp - 1, nrow, pltpu.roll(x, tp - 1, 0))
        return x + mu_ref[0:1, c0:c1] * (xp - x) + mu_ref[1:2, c0:c1] * (xn - x)

    e_bf = e_ref[...]
    r = shifted(0, aw)
    k = shifted(aw, 2 * aw)
    v = shifted(2 * aw, 3 * aw)
    gd = shifted(3 * aw, 3 * aw + A_LORA_G)
    lo0 = 3 * aw + A_LORA_G
    xl = shifted(lo0, lo0 + 2 * A_LORA_W + 2 * A_LORA_A)

    r_o[...] = r
    v_o[...] = v
    g_o[...] = _dot(_bf(_sigmoid(gd)), g2_ref[...])

    kx = k * kk_ref[...]
    kn = jnp.sqrt(_head_sum(kx * kx, e_bf))
    kk = kx / jnp.maximum(kn, 1e-12)
    kk_o[...] = kk

    xl_t = _bf(jnp.tanh(xl))
    xl_b = _bf(xl)
    kd_sum = None
    for d, (lw_o, kd_o, b_o) in enumerate(((lw0_o, kd0_o, b0_o), (lw1_o, kd1_o, b1_o))):
        w_log = -_softplus(-(w0_ref[d:d + 1, :] + _dot(xl_t, w2_ref[d]))) - 0.5
        lw_o[...] = -jnp.exp(w_log)
        asig = _sigmoid(a0_ref[d:d + 1, :] + _dot(xl_b, a2_ref[d]))
        kd = k * (1.0 + (asig - 1.0) * ka_ref[...])
        kd_o[...] = kd
        b_o[...] = kk * asig
        kd_sum = kd if kd_sum is None else kd_sum + kd
    bonus_o[...] = _head_sum(r * rk_ref[...] * kd_sum, e_bf) * v


def _rwkv_prep(pa, mu_p, w0, a0, w2p, a2p, g2, k_k, k_a, r_k, e_bf, *, n_lat, seq, ctx_len):
    r = pa.shape[0]
    ncol = mu_p.shape[1]
    aw = w0.shape[1]
    tp = PREP_TILE
    hb = tp // SUBLANE
    nblk8 = r // SUBLANE
    lw = 2 * A_LORA_W + 2 * A_LORA_A
    row = lambda a: a.reshape(1, aw)
    full = lambda shape: pl.BlockSpec(shape, lambda i: (0,) * len(shape))
    out_spec = pl.BlockSpec((tp, aw), lambda i: (i, 0))
    out_sds = jax.ShapeDtypeStruct((r, aw), F32)
    return pl.pallas_call(
        functools.partial(_rwkv_prep_kernel, n_lat_tiles=n_lat // tp, lat_tiles_per_seq=seq // tp,
                          ctx_tiles_per_seq=ctx_len // tp, aw=aw),
        grid=(r // tp,),
        in_specs=[
            pl.BlockSpec((tp, ncol), lambda i: (i, 0)),
            pl.BlockSpec((SUBLANE, ncol), lambda i: (jnp.maximum(i * hb - 1, 0), 0)),
            pl.BlockSpec((SUBLANE, ncol), lambda i: (jnp.minimum((i + 1) * hb, nblk8 - 1), 0)),
            full((2, ncol)), full((2, aw)), full((2, aw)),
            full((2, lw, aw)), full((2, lw, aw)), full((A_LORA_G, aw)),
            full((1, aw)), full((1, aw)), full((1, aw)), full((GROUP_W, GROUP_W)),
        ],
        out_specs=[out_spec] * 11,
        out_shape=[out_sds] * 11,
        compiler_params=_cparams(("parallel",)),
    )(pa, pa, pa, mu_p, w0, a0, w2p, a2p, g2, row(k_k), row(k_a), row(r_k), e_bf)


def _rwkv_masks(reverse):
    L = CHUNK
    gw = GROUP_W
    row = lax.broadcasted_iota(jnp.int32, (gw, gw), 0)
    col = lax.broadcasted_iota(jnp.int32, (gw, gw), 1)
    same_head = (row >> 6) == (col >> 6)
    tr = row & (L - 1)
    tc = col & (L - 1)
    strict = same_head & ((tr < tc) if reverse else (tr > tc))
    incl = same_head & ((tr <= tc) if reverse else (tr >= tc))
    t_r = lax.broadcasted_iota(jnp.int32, (L, L), 0)
    t_c = lax.broadcasted_iota(jnp.int32, (L, L), 1)
    tri = _bf(jnp.where((t_c >= t_r) if reverse else (t_c <= t_r), 1.0, 0.0))
    return dict(same_head=same_head, strict=strict, incl=incl, eye=row == col, tri=tri)


def _rwkv_chunks(probs):
    L = CHUNK
    gw = GROUP_W
    masks = {rv: _rwkv_masks(rv) for rv in sorted({p[7] for p in probs})}
    mk = [masks[p[7]] for p in probs]
    each = lambda f, *cols: [f(*args) for args in zip(*cols)]
    rep = lambda x: jnp.concatenate([x] * HEADS_PER_GROUP, axis=0)
    nat = lambda x: x[0:L] + x[L:2 * L] + x[2 * L:3 * L] + x[3 * L:4 * L]

    c = [_dot_exact_lhs(m["tri"], p[3]) for m, p in zip(mk, probs)]
    c_l = [ci[(0 if p[7] else L - 1):(1 if p[7] else L), :] for ci, p in zip(c, probs)]

    at_bd, rt_bd, v_bf, bend_bd, kend_bd, lhs, rhs, e_l = [], [], [], [], [], [], [], []
    for (r, v, kk, lw, kd, b, _, _), m, ci, cl in zip(probs, mk, c, c_l):
        bd = lambda x, sh=m["same_head"]: jnp.where(sh, rep(x), 0.0)
        e_nc = jnp.exp(-ci)
        e_end = jnp.exp(cl - ci)
        a_ = bd(-kk * jnp.exp(ci - lw))
        r_ = bd(r * jnp.exp(ci))
        at_bd.append(a_)
        rt_bd.append(r_)
        v_bf.append(_bf(bd(v)))
        bend_bd.append(_bf(bd(b * e_end)))
        kend_bd.append(_bf(bd(kd * e_end)))
        lhs.append(_bf(jnp.concatenate([a_, r_], axis=0)))
        rhs.append(_bf(jnp.concatenate([rep(b * e_nc), rep(kd * e_nc)], axis=0)))
        e_l.append(jnp.exp(cl))

    s = each(_dot_nt, lhs, rhs)
    a_ab = [jnp.where(m["strict"], si[:gw, :gw], 0.0) for m, si in zip(mk, s)]
    a_ak = [_bf(jnp.where(m["strict"], si[:gw, gw:], 0.0)) for m, si in zip(mk, s)]
    r_b = [_bf(jnp.where(m["incl"], si[gw:, :gw], 0.0)) for m, si in zip(mk, s)]
    r_k = [_bf(jnp.where(m["incl"], si[gw:, gw:], 0.0)) for m, si in zip(mk, s)]

    akv = each(_dot, a_ak, v_bf)
    t = [jnp.where(m["eye"], 1.0, 0.0) + a for m, a in zip(mk, a_ab)]
    a_pow = a_ab
    for _ in range(5):
        a_pow_bf = [_bf(a) for a in a_pow]
        a_pow = each(_dot, a_pow_bf, a_pow_bf)
        t = [ti + _dot(_bf(ti), _bf(ap)) for ti, ap in zip(t, a_pow)]
    pq = [_dot(_bf(ti), _bf(jnp.concatenate([a_, ak], axis=1)))
          for ti, a_, ak in zip(t, at_bd, akv)]
    p_bf = [_bf(x[:, :gw]) for x in pq]
    q_bf = [_bf(x[:, gw:]) for x in pq]

    m_ = [jnp.where(m["eye"], el, 0.0) + _dot_tn(be, p)
          for m, el, be, p in zip(mk, e_l, bend_bd, p_bf)]
    n_ = [_dot_tn(be, q) + _dot_tn(ke, v) for be, q, ke, v in zip(bend_bd, q_bf, kend_bd, v_bf)]
    g_ = [r_ + _dot(rb, p) for r_, rb, p in zip(rt_bd, r_b, p_bf)]
    y0 = [_dot(rb, q) + _dot(rk, v) for rb, q, rk, v in zip(r_b, q_bf, r_k, v_bf)]

    h_bf = [_bf(p[6]) for p in probs]
    y = [_dot(_bf(nat(g)), h) + nat(y0i) for g, h, y0i in zip(g_, h_bf, y0)]
    h_new = [_dot(_bf(mi), h) + ni for mi, h, ni in zip(m_, h_bf, n_)]
    return list(zip(y, h_new))


def _rwkv_scan_kernel(rf, vf, kkf, lwf, kdf, bf_, rb, vb, kkb, lwb, kdb, bb,
                      yf_ref, yb_ref, hf_ref, hb_ref):
    @pl.when(pl.program_id(1) == 0)
    def _():
        hf_ref[...] = jnp.zeros_like(hf_ref)
        hb_ref[...] = jnp.zeros_like(hb_ref)

    gs = hf_ref.shape[0]
    probs = []
    for gi in range(gs):
        sl = slice(gi * GROUP_W, (gi + 1) * GROUP_W)
        probs.append((rf[:, sl], vf[:, sl], kkf[:, sl], lwf[:, sl], kdf[:, sl], bf_[:, sl],
                      hf_ref[gi], False))
        probs.append((rb[:, sl], vb[:, sl], kkb[:, sl], lwb[:, sl], kdb[:, sl], bb[:, sl],
                      hb_ref[gi], True))
    res = _rwkv_chunks(probs)
    for gi in range(gs):
        sl = slice(gi * GROUP_W, (gi + 1) * GROUP_W)
        yf_ref[:, sl], hf_ref[gi] = res[2 * gi]
        yb_ref[:, sl], hb_ref[gi] = res[2 * gi + 1]


def _chunk_row_block(b, i, *, reverse, n_lat, seq, ctx_len):
    nc = ctx_len // CHUNK
    nl = seq // CHUNK
    ci = (nc - 1 - i) if reverse else i
    li = (nl - 1 - (i - nc)) if reverse else (i - nc)
    return jnp.where(i < nc, (n_lat + b * ctx_len) // CHUNK + ci, b * nl + li)


def _rwkv_scan(r, v, kk, lw0, lw1, kd0, kd1, b0, b1, *, batch, n_lat, seq, ctx_len):
    rows, aw = r.shape
    gs = RWKV_GROUPS_PER_STEP
    ng = aw // (GROUP_W * gs)
    geo = dict(n_lat=n_lat, seq=seq, ctx_len=ctx_len)

    def spec(reverse):
        return pl.BlockSpec(
            (CHUNK, GROUP_W * gs),
            lambda pg, i: (_chunk_row_block(pg // ng, i, reverse=reverse, **geo), pg % ng))

    sf, sb = spec(False), spec(True)
    sds = jax.ShapeDtypeStruct((rows, aw), F32)
    return pl.pallas_call(
        _rwkv_scan_kernel,
        grid=(batch * ng, (seq + ctx_len) // CHUNK),
        in_specs=[sf] * 6 + [sb] * 6,
        out_specs=[sf, sb],
        out_shape=[sds, sds],
        scratch_shapes=[pltpu.VMEM((gs, GROUP_W, GROUP_W), F32),
                        pltpu.VMEM((gs, GROUP_W, GROUP_W), F32)],
        compiler_params=_cparams(("parallel", "arbitrary")),
    )(r, v, kk, lw0, kd0, b0, r, v, kk, lw1, kd1, b1)


def _gla_scan_kernel(qkf, vf, gdf, qkb, vb, gdb, gw2_ref, gb_ref, of_ref, ob_ref, sf_ref, sb_ref,
                     *, dk, dv):
    @pl.when(pl.program_id(1) == 0)
    def _():
        sf_ref[...] = jnp.zeros_like(sf_ref)
        sb_ref[...] = jnp.zeros_like(sb_ref)

    L = CHUNK
    wk = B_HEADS * dk
    t_r = lax.broadcasted_iota(jnp.int32, (L, L), 0)
    t_c = lax.broadcasted_iota(jnp.int32, (L, L), 1)

    qh, kh, keh, vh, dech, keeph, sth = [], [], [], [], [], [], []
    for d, (qk_ref, v_ref, gd_ref, st_ref) in enumerate(((qkf, vf, gdf, sf_ref),
                                                         (qkb, vb, gdb, sb_ref))):
        reverse = d == 1
        keep = (t_c >= t_r) if reverse else (t_c <= t_r)
        tri = _bf(jnp.where(keep, 1.0, 0.0))
        z = _dot(_bf(gd_ref[...]), gw2_ref[d]) + gb_ref[d:d + 1, :]
        la = -_softplus(-z) * (1.0 / B_GATE_TAU)
        cum = _dot_exact_lhs(tri, la)
        last = 0 if reverse else L - 1
        cl = cum[last:last + 1, :]
        q_in = _bf(qk_ref[:, :wk] * (dk ** -0.5) * jnp.exp(cum))
        k_in = _bf(qk_ref[:, wk:] * jnp.exp(-cum))
        k_end = _bf(qk_ref[:, wk:] * jnp.exp(cl - cum))
        dec = jnp.exp(cl)
        for h in range(B_HEADS):
            ks = slice(h * dk, (h + 1) * dk)
            qh.append(q_in[:, ks])
            kh.append(k_in[:, ks])
            keh.append(k_end[:, ks])
            vh.append(_bf(v_ref[:, h * dv:(h + 1) * dv]))
            dech.append(dec[:, ks])
            keeph.append(keep)
            sth.append(st_ref[h])

    sc = [_bf(jnp.where(kp, _dot_nt(q, k), 0.0)) for kp, q, k in zip(keeph, qh, kh)]
    inter = [_dot_nt(q, _bf(st)) for q, st in zip(qh, sth)]
    kvt = [_dot_tn(v, ke) for v, ke in zip(vh, keh)]
    out = [_dot(s_, v) + it for s_, v, it in zip(sc, vh, inter)]
    for d, (o_ref, st_ref) in enumerate(((of_ref, sf_ref), (ob_ref, sb_ref))):
        for h in range(B_HEADS):
            i = d * B_HEADS + h
            o_ref[:, h * dv:(h + 1) * dv] = out[i]
            st_ref[h] = sth[i] * dech[i] + kvt[i]


def _gla_scan(pb, gw2p, gb, *, batch, n_lat, seq, ctx_len, wk, wv, col0):
    rows = pb.shape[0]
    dk, dv = wk // B_HEADS, wv // B_HEADS
    geo = dict(n_lat=n_lat, seq=seq, ctx_len=ctx_len)
    gd_blk = (col0 + 2 * wk + 2 * wv) // LANE

    def specs(reverse):
        rb = lambda b, i: _chunk_row_block(b, i, reverse=reverse, **geo)
        return [
            pl.BlockSpec((CHUNK, 2 * wk), lambda b, i: (rb(b, i), col0 // (2 * wk))),
            pl.BlockSpec((CHUNK, wv), lambda b, i: (rb(b, i), (col0 + 2 * wk) // wv)),
            pl.BlockSpec((CHUNK, LANE), lambda b, i: (rb(b, i), gd_blk)),
        ]

    def out_spec(reverse):
        return pl.BlockSpec((CHUNK, wv),
                            lambda b, i: (_chunk_row_block(b, i, reverse=reverse, **geo), 0))

    sds = jax.ShapeDtypeStruct((rows, wv), F32)
    return pl.pallas_call(
        functools.partial(_gla_scan_kernel, dk=dk, dv=dv),
        grid=(batch, (seq + ctx_len) // CHUNK),
        in_specs=specs(False) + specs(True) + [
            pl.BlockSpec((2, LANE, wk), lambda b, i: (0, 0, 0)),
            pl.BlockSpec((2, wk), lambda b, i: (0, 0)),
        ],
        out_specs=[out_spec(False), out_spec(True)],
        out_shape=[sds, sds],
        scratch_shapes=[pltpu.VMEM((B_HEADS, dv, dk), F32), pltpu.VMEM((B_HEADS, dv, dk), F32)],
        compiler_params=_cparams(("parallel", "arbitrary")),
    )(pb, pb, pb, pb, pb, pb, gw2p, gb)


def _mixer_out_kernel(yf, yb, bonus, g, of, ob, gate, x_ref, mod_ref, lnw, lnb, gn, e_ref, w_ref,
                      o_ref, h_ref, *, aw, dv):
    e_bf = e_ref[...]
    y = yf[...] + yb[...]
    inv = 1.0 / A_HEAD_DIM
    mean = _head_sum(y, e_bf) * inv
    dlt = y - mean
    var = _head_sum(dlt * dlt, e_bf) * inv
    yn = dlt * lax.rsqrt(var + A_LN_EPS) * lnw[...] + lnb[...]
    h_ref[:, :aw] = _bf((yn + bonus[...]) * g[...])

    o = of[...] + ob[...]
    gt = _silu(gate[...])
    for h in range(B_HEADS):
        sl = slice(h * dv, (h + 1) * dv)
        oh = o[:, sl]
        ms = jnp.mean(oh * oh, axis=-1, keepdims=True)
        h_ref[:, aw + h * dv:aw + (h + 1) * dv] = _bf(oh * lax.rsqrt(ms + NORM_EPS) * gn[...] * gt[:, sl])

    o_ref[...] = x_ref[...] + mod_ref[2:3, :] * _dot(h_ref[...], w_ref[...])


def _mixer_out(yf, yb, bonus, g, of, ob, pb, x, mods_l, ln_w, ln_b, gla_norm, e_bf, w_bf,
               seg_of_tile, *, wk, col0):
    rows, d = x.shape
    aw = yf.shape[1]
    wv = of.shape[1]
    dv = wv // B_HEADS
    tm = 256
    tiles_per_row_tile = ROW_TILE // tm
    rs = lambda w: pl.BlockSpec((tm, w), lambda i: (i, 0))
    full = lambda shape: pl.BlockSpec(shape, lambda i: (0,) * len(shape))
    return pl.pallas_call(
        functools.partial(_mixer_out_kernel, aw=aw, dv=dv),
        grid=(rows // tm,),
        in_specs=[
            rs(aw), rs(aw), rs(aw), rs(aw), rs(wv), rs(wv),
            pl.BlockSpec((tm, wv), lambda i: (i, (col0 + 2 * wk + wv) // wv)),
            rs(d),
            pl.BlockSpec((None, N_MOD, d), lambda i: (seg_of_tile(i // tiles_per_row_tile), 0, 0)),
            full((1, aw)), full((1, aw)), full((1, dv)), full((GROUP_W, GROUP_W)),
            full((aw + wv, d)),
        ],
        out_specs=rs(d),
        out_shape=jax.ShapeDtypeStruct((rows, d), F32),
        scratch_shapes=[pltpu.VMEM((tm, aw + wv), BF16)],
        compiler_params=_cparams(("parallel",)),
    )(yf, yb, bonus, g, of, ob, pb, x, mods_l, ln_w.reshape(1, aw), ln_b.reshape(1, aw),
      gla_norm.reshape(1, dv), e_bf, w_bf)


def _lru_kernel(*refs, reverse, n_blk, combine):
    if combine:
        (x_ref, xp_ref, xn_ref, h0_ref, cw_ref, cb_ref, wc_ref, ba_ref, bx_ref, lam_ref,
         hf_ref, gate_ref, out_ref, hl_ref, a_sc, u_sc, hs_sc, carry_sc) = refs
    else:
        (x_ref, xp_ref, xn_ref, h0_ref, cw_ref, cb_ref, wc_ref, ba_ref, bx_ref, lam_ref,
         out_ref, hl_ref, a_sc, u_sc, hs_sc, carry_sc) = refs
    j = pl.program_id(2)
    blk = (n_blk - 1 - j) if reverse else j
    lb, ct = x_ref.shape
    cblk = wc_ref.shape[1]
    is_first = blk == 0
    is_last = blk == n_blk - 1
    rowid = lax.broadcasted_iota(jnp.int32, (lb, 1), 0)

    x = x_ref[...]
    p6 = jnp.where(is_first, 0.0, xp_ref[SUBLANE - 2:SUBLANE - 1, :])
    p7 = jnp.where(is_first, 0.0, xp_ref[SUBLANE - 1:SUBLANE, :])
    n0 = jnp.where(is_last, 0.0, xn_ref[0:1, :])
    xm1 = jnp.where(rowid == 0, p7, pltpu.roll(x, 1, 0))
    xm2 = jnp.where(rowid == 0, p6, jnp.where(rowid == 1, p7, pltpu.roll(x, 2, 0)))
    xp1 = jnp.where(rowid == lb - 1, n0, pltpu.roll(x, lb - 1, 0))
    xs = (cb_ref[...] + cw_ref[0:1, :] * xm2 + cw_ref[1:2, :] * xm1 + cw_ref[2:3, :] * x
          + cw_ref[3:4, :] * xp1)

    for n in range(ct // cblk):
        sl = slice(n * cblk, (n + 1) * cblk)
        xb = xs[:, sl]
        ri = _dot(_bf(xb), wc_ref[n])
        rg = _sigmoid(ri[:, :cblk] + ba_ref[:, sl])
        ig = _sigmoid(ri[:, cblk:] + bx_ref[:, sl])
        log_a = -C_CONST * rg * _softplus(-lam_ref[:, sl])
        a = jnp.exp(log_a)
        a_sc[:, sl] = a
        u_sc[:, sl] = jnp.sqrt(1.0 - a * a) * (ig * xb)

    @pl.when(j == 0)
    def _():
        carry_sc[...] = h0_ref[0:1, :]

    def step(t, h):
        tt = (lb - 1 - t) if reverse else t
        h = a_sc[pl.ds(tt, 1), :] * h + u_sc[pl.ds(tt, 1), :]
        hs_sc[pl.ds(tt, 1), :] = h
        return h

    h = lax.fori_loop(0, lb, step, carry_sc[...], unroll=8)
    carry_sc[...] = h
    hl_ref[...] = jnp.broadcast_to(h, hl_ref.shape)
    if combine:
        out_ref[...] = (hs_sc[...] + hf_ref[...]) * jax.nn.gelu(gate_ref[...])
    else:
        out_ref[...] = hs_sc[...]


def _lru_pass(x_arr, x_specs, h0, conv_w, conv_b, wcat, ba, bx, lam, out_shape, out_spec, extra,
              *, batch, n_blk, lb, reverse):
    c = conv_w.shape[1]
    ct = 1024
    nct = c // ct
    cblk = wcat.shape[1]
    chan = lambda rows_: pl.BlockSpec((rows_, ct), lambda b, k, j: (0, k))
    in_specs = list(x_specs) + [
        pl.BlockSpec((SUBLANE, ct), lambda b, k, j: (b, k)),
        chan(C_CONV), chan(1),
        pl.BlockSpec((ct // cblk, cblk, 2 * cblk), lambda b, k, j: (k, 0, 0)),
        chan(1), chan(1), chan(1),
    ] + [s for _, s in extra]
    args = [x_arr, x_arr, x_arr, h0, conv_w, conv_b.reshape(1, c), wcat, ba.reshape(1, c),
            bx.reshape(1, c), lam.reshape(1, c)] + [a for a, _ in extra]
    return pl.pallas_call(
        functools.partial(_lru_kernel, reverse=reverse, n_blk=n_blk, combine=bool(extra)),
        grid=(batch, nct, n_blk),
        in_specs=in_specs,
        out_specs=[out_spec, pl.BlockSpec((SUBLANE, ct), lambda b, k, j: (b, k))],
        out_shape=[out_shape, jax.ShapeDtypeStruct((batch * SUBLANE, c), F32)],
        scratch_shapes=[pltpu.VMEM((lb, ct), F32), pltpu.VMEM((lb, ct), F32),
                        pltpu.VMEM((lb, ct), F32), pltpu.VMEM((1, ct), F32)],
        compiler_params=_cparams(("parallel", "parallel", "arbitrary")),
    )(*args)


def _rglru(hw_ctx, gate_grid, x_grid, conv_w, conv_b, wa, ba, wx, bx, lam, *, batch, n_lat, seq,
           ctx_len):
    c = conv_w.shape[1]
    ct = 1024
    rows = seq // GRID_W
    lbc = 128
    ncb = ctx_len // lbc
    xcol = c // ct
    per_col = c // ct

    wcat = [_bf(jnp.concatenate([wa[d], wx[d]], axis=-1)) for d in range(2)]
    zeros_state = jnp.zeros((batch * SUBLANE, c), F32)

    def ctx_specs(reverse):
        blk = (lambda j: ncb - 1 - j) if reverse else (lambda j: j)
        per8 = lbc // SUBLANE
        cur = pl.BlockSpec((lbc, ct), lambda b, k, j: (b * ncb + blk(j), xcol + k))
        prv = pl.BlockSpec((SUBLANE, ct), lambda b, k, j: (
            b * ncb * per8 + jnp.maximum(blk(j) * per8 - 1, 0), xcol + k))
        nxt = pl.BlockSpec((SUBLANE, ct), lambda b, k, j: (
            b * ncb * per8 + jnp.minimum((blk(j) + 1) * per8, ncb * per8 - 1), xcol + k))
        return cur, prv, nxt

    def lat_specs(reverse):
        blk = (lambda j: GRID_W - 1 - j) if reverse else (lambda j: j)
        r8 = rows // SUBLANE
        cur = pl.BlockSpec((rows, ct), lambda b, k, j: (b, blk(j) * per_col + k))
        prv = pl.BlockSpec((SUBLANE, ct), lambda b, k, j: (
            b * r8 + r8 - 1, jnp.maximum(blk(j) - 1, 0) * per_col + k))
        nxt = pl.BlockSpec((SUBLANE, ct), lambda b, k, j: (
            b * r8, jnp.minimum(blk(j) + 1, GRID_W - 1) * per_col + k))
        return cur, prv, nxt

    common = dict(batch=batch)
    ctx_scratch_sds = jax.ShapeDtypeStruct((batch * ctx_len, c), F32)
    ctx_out = lambda reverse: pl.BlockSpec(
        (lbc, ct), lambda b, k, j: (b * ncb + ((ncb - 1 - j) if reverse else j), k))
    _, st = _lru_pass(hw_ctx, ctx_specs(False), zeros_state, conv_w, conv_b, wcat[0], ba[0], bx[0],
                      lam[0], ctx_scratch_sds, ctx_out(False), [], n_blk=ncb, lb=lbc,
                      reverse=False, **common)
    hf_sds = jax.ShapeDtypeStruct((n_lat, c), F32)
    hf_spec = lambda reverse: pl.BlockSpec(
        (rows, ct), lambda b, k, j: (b * GRID_W + ((GRID_W - 1 - j) if reverse else j), k))
    hf, _ = _lru_pass(x_grid, lat_specs(False), st, conv_w, conv_b, wcat[0], ba[0], bx[0], lam[0],
                      hf_sds, hf_spec(False), [], n_blk=GRID_W, lb=rows, reverse=False, **common)
    _, st = _lru_pass(hw_ctx, ctx_specs(True), zeros_state, conv_w, conv_b, wcat[1], ba[1], bx[1],
                      lam[1], ctx_scratch_sds, ctx_out(True), [], n_blk=ncb, lb=lbc,
                      reverse=True, **common)
    grid_spec = pl.BlockSpec((rows, ct), lambda b, k, j: (b, (GRID_W - 1 - j) * per_col + k))
    y_sds = jax.ShapeDtypeStruct((n_lat // GRID_W, GRID_W * c), F32)
    y, _ = _lru_pass(x_grid, lat_specs(True), st, conv_w, conv_b, wcat[1], ba[1], bx[1], lam[1],
                     y_sds, grid_spec, [(hf, hf_spec(True)), (gate_grid, grid_spec)],
                     n_blk=GRID_W, lb=rows, reverse=True, **common)
    return y


def _pad_cols(w, n):
    return jnp.pad(w, ((0, 0),) * (w.ndim - 1) + ((0, n - w.shape[-1]),))


def kernel(x, c, ctx, c_ctx, mod_w, mod_b, norm1, norm2, mlp_w1, mlp_w2, ab_w_in, ab_w_out, rw_mu,
           rw_w0, rw_w2, rw_a0, rw_a2, rw_g2, rw_kk, rw_ka, rw_rk, rw_ln_w, rw_ln_b, gla_gw2,
           gla_gb, gla_norm, lru_w_in, lru_w_out, lru_conv_w, lru_conv_b, lru_wa, lru_ba, lru_wx,
           lru_bx, lru_lam, final_norm):
    batch, seq, d = x.shape
    ctx_len = ctx.shape[1]
    depth = mod_w.shape[0]
    n_lat = batch * seq
    aw = rw_w0.shape[-1]
    wk = gla_gb.shape[-1]
    wv = ab_w_out.shape[1] - aw
    a_cols = rw_mu.shape[-1]
    b_cols = ab_w_in.shape[-1] - a_cols
    assert depth == 2 and batch + 1 <= SUBLANE
    assert seq % ROW_TILE == 0 and (batch * ctx_len) % ROW_TILE == 0
    assert ctx_len % PREP_TILE == 0 and ctx_len % 128 == 0 and seq % (GRID_W * SUBLANE) == 0
    assert aw % GROUP_W == 0 and wv == aw and 2 * wk == wv

    tiles_per_seq = seq // ROW_TILE
    seg_of_tile = lambda i: jnp.minimum(i // tiles_per_seq, batch)

    xs = jnp.concatenate([x.reshape(n_lat, d), ctx.reshape(batch * ctx_len, d)], axis=0)
    cvec = jnp.concatenate(
        [c, c_ctx[None, :], jnp.zeros((SUBLANE - batch - 1, d), F32)], axis=0)
    mods = _mods(cvec, mod_w, mod_b)

    hid = jnp.arange(GROUP_W) // A_HEAD_DIM
    e_bf = _bf(hid[:, None] == hid[None, :])

    a_pad = -(-a_cols // LANE) * LANE
    col0 = -(-a_cols // (2 * wk)) * (2 * wk)
    n_tiles = 3
    tn = -(-(col0 + b_cols) // (n_tiles * LANE)) * LANE
    w_ab = _bf(jnp.concatenate(
        [_pad_cols(ab_w_in[0][:, :a_cols], col0),
         _pad_cols(ab_w_in[0][:, a_cols:], n_tiles * tn - col0)], axis=1))
    pa = pb = _norm_matmul(xs, norm1[0], mods[0], w_ab, seg_of_tile, tn=tn)

    lora_w = 2 * A_LORA_W + 2 * A_LORA_A
    w2p = jnp.zeros((2, lora_w, aw), F32)
    a2p = jnp.zeros((2, lora_w, aw), F32)
    for dd in range(2):
        w2p = w2p.at[dd, dd * A_LORA_W:(dd + 1) * A_LORA_W].set(rw_w2[0, dd])
        a2p = a2p.at[dd, 2 * A_LORA_W + dd * A_LORA_A:2 * A_LORA_W + (dd + 1) * A_LORA_A].set(
            rw_a2[0, dd])
    (r_, v_, kk_, lw0, lw1, kd0, kd1, b0, b1, g_, bonus) = _rwkv_prep(
        pa, _pad_cols(rw_mu[0], a_pad), rw_w0[0], rw_a0[0], _bf(w2p), _bf(a2p), _bf(rw_g2[0]),
        rw_kk[0], rw_ka[0], rw_rk[0].reshape(-1), e_bf, n_lat=n_lat, seq=seq, ctx_len=ctx_len)
    yf, yb = _rwkv_scan(r_, v_, kk_, lw0, lw1, kd0, kd1, b0, b1, batch=batch, n_lat=n_lat, seq=seq,
                        ctx_len=ctx_len)

    gw2p = jnp.zeros((2, LANE, wk), F32)
    for dd in range(2):
        gw2p = gw2p.at[dd, dd * B_LORA:(dd + 1) * B_LORA].set(gla_gw2[0, dd])
    of, ob = _gla_scan(pb, _bf(gw2p), gla_gb[0], batch=batch, n_lat=n_lat, seq=seq,
                       ctx_len=ctx_len, wk=wk, wv=wv, col0=col0)

    xs = _mixer_out(yf, yb, bonus, g_, of, ob, pb, xs, mods[0], rw_ln_w[0], rw_ln_b[0],
                    gla_norm[0], e_bf, _bf(ab_w_out[0]), seg_of_tile, wk=wk, col0=col0)
    seg_of_mlp_tile = lambda i: jnp.minimum(i // (seq // MLP_TILE), batch)
    ctx_tiles = (batch * ctx_len) // ROW_TILE
    w1_bf, w2_bf = _bf(mlp_w1[0]), _bf(mlp_w2[0])
    x_lat = _mlp(xs, norm2[0], mods[0], w1_bf, w2_bf, final_norm, seg_of_mlp_tile, tm=MLP_TILE,
                 tile0=0, n_tiles=n_lat // MLP_TILE, final_norm=False)
    x_ctx = _mlp(xs, norm2[0], mods[0], w1_bf, w2_bf, final_norm, seg_of_tile, tm=ROW_TILE,
                 tile0=n_lat // ROW_TILE, n_tiles=ctx_tiles, final_norm=False)

    w_lru = _bf(lru_w_in[0])
    hw_ctx = _norm_matmul(x_ctx, norm1[1], mods[1], w_lru, lambda i: batch, tn=2048)
    gate_grid, x_grid = _norm_matmul_grid(x_lat, n_lat, seq // GRID_W, norm1[1], mods[1], w_lru)
    y = _rglru(hw_ctx, gate_grid, x_grid, lru_conv_w[0], lru_conv_b[0], lru_wa[0], lru_ba[0],
               lru_wx[0], lru_bx[0], lru_lam[0], batch=batch, n_lat=n_lat, seq=seq,
               ctx_len=ctx_len)
    xl = _proj_res(y, x_lat, n_lat, seq // GRID_W, mods[1], _bf(lru_w_out[0]))
    out = _mlp(xl, norm2[1], mods[1], _bf(mlp_w1[1]), _bf(mlp_w2[1]), final_norm, seg_of_mlp_tile,
               tm=MLP_TILE, tile0=0, n_tiles=n_lat // MLP_TILE, final_norm=True)
    return out.reshape(batch, seq, d)
```

```python
import functools

import jax
import jax.numpy as jnp
from jax import lax
from jax.experimental import pallas as pl
from jax.experimental.pallas import tpu as pltpu

F32 = jnp.float32
BF16 = jnp.bfloat16

NORM_EPS = 1e-6
GRID_W = 64
N_MOD = 6

A_HEAD_DIM = 64
A_LORA_W = 96
A_LORA_A = 96
A_LORA_G = 256
A_LN_EPS = 64e-5
HEADS_PER_GROUP = 4
GROUP_W = HEADS_PER_GROUP * A_HEAD_DIM
CHUNK = 64
RWKV_GROUPS_PER_STEP = 4

B_HEADS = 4
B_LORA = 16
B_GATE_TAU = 16.0

C_BLOCKS = 8
C_CONV = 4
C_CONST = 8.0

LANE = 128
SUBLANE = 8
ROW_TILE = 512
MLP_TILE = 1024
PREP_TILE = 128
VMEM_LIMIT = 56 * 1024 * 1024


def _cparams(sem):
    return pltpu.CompilerParams(dimension_semantics=sem, vmem_limit_bytes=VMEM_LIMIT)


def _bf(x):
    return x.astype(BF16)


def _dot(a, b):
    return jnp.dot(a, b, preferred_element_type=F32)


def _dot_nt(a, b):
    return lax.dot_general(a, b, (((1,), (1,)), ((), ())), preferred_element_type=F32)


def _dot_tn(a, b):
    return lax.dot_general(a, b, (((0,), (0,)), ((), ())), preferred_element_type=F32)


def _split3(x):
    hi = _bf(x)
    r1 = x - hi.astype(F32)
    mid = _bf(r1)
    lo = _bf(r1 - mid.astype(F32))
    return hi, mid, lo


def _dot_exact_lhs(m_bf, x):
    hi, mid, lo = _split3(x)
    return _dot(m_bf, hi) + _dot(m_bf, mid) + _dot(m_bf, lo)


def _dot_exact_rhs(x, m_bf):
    hi, mid, lo = _split3(x)
    return _dot(hi, m_bf) + _dot(mid, m_bf) + _dot(lo, m_bf)


def _softplus(x):
    return jnp.maximum(x, 0.0) + jnp.log1p(jnp.exp(-jnp.abs(x)))


def _sigmoid(x):
    return jax.nn.sigmoid(x)


def _silu(x):
    return x * _sigmoid(x)


def _norm_mod(x, g, shift, scale):
    ms = jnp.mean(x * x, axis=-1, keepdims=True)
    return (x * lax.rsqrt(ms + NORM_EPS) * g) * (1.0 + scale) + shift


def _mods_kernel(c_ref, w_ref, b_ref, o_ref):
    s = _silu(c_ref[...])
    o_ref[...] = _dot(_bf(s), _bf(w_ref[...])) + b_ref[...]


def _mods(cvec, mod_w, mod_b):
    depth, d, n = mod_w.shape
    tn = 1024
    out = pl.pallas_call(
        _mods_kernel,
        grid=(depth, n // tn),
        in_specs=[
            pl.BlockSpec((SUBLANE, d), lambda l, j: (0, 0)),
            pl.BlockSpec((None, d, tn), lambda l, j: (l, 0, j)),
            pl.BlockSpec((None, 1, tn), lambda l, j: (l, 0, j)),
        ],
        out_specs=pl.BlockSpec((None, SUBLANE, tn), lambda l, j: (l, 0, j)),
        out_shape=jax.ShapeDtypeStruct((depth, SUBLANE, n), F32),
        compiler_params=_cparams(("parallel", "parallel")),
    )(cvec, mod_w, mod_b.reshape(depth, 1, n))
    return out.reshape(depth, SUBLANE, N_MOD, d)


def _norm_matmul_kernel(x_ref, g_ref, mod_ref, w_ref, o_ref, h_ref, *, shift_row, scale_row):
    @pl.when(pl.program_id(1) == 0)
    def _():
        h = _norm_mod(x_ref[...], g_ref[...], mod_ref[shift_row:shift_row + 1, :],
                      mod_ref[scale_row:scale_row + 1, :])
        h_ref[...] = _bf(h)

    o_ref[...] = _dot(h_ref[...], w_ref[...])


def _norm_matmul(x, g, mods_l, w_bf, seg_of_tile, *, tn, tile0=0, n_tiles=None, shift_row=0,
                 scale_row=1):
    r, d = x.shape
    n = w_bf.shape[1]
    tm = ROW_TILE
    n_tiles = r // tm if n_tiles is None else n_tiles
    return pl.pallas_call(
        functools.partial(_norm_matmul_kernel, shift_row=shift_row, scale_row=scale_row),
        grid=(n_tiles, n // tn),
        in_specs=[
            pl.BlockSpec((tm, d), lambda i, j: (i + tile0, 0)),
            pl.BlockSpec((1, d), lambda i, j: (0, 0)),
            pl.BlockSpec((None, N_MOD, d), lambda i, j: (seg_of_tile(i + tile0), 0, 0)),
            pl.BlockSpec((d, tn), lambda i, j: (0, j)),
        ],
        out_specs=pl.BlockSpec((tm, tn), lambda i, j: (i, j)),
        out_shape=jax.ShapeDtypeStruct((n_tiles * tm, n), F32),
        scratch_shapes=[pltpu.VMEM((tm, d), BF16)],
        compiler_params=_cparams(("parallel", "arbitrary")),
    )(x, g.reshape(1, d), mods_l, w_bf)


def _grid_tile(rows):
    gr = min(rows, 64)
    gc = ROW_TILE // gr
    assert gc % SUBLANE == 0 and GRID_W % gc == 0 and rows % gr == 0
    return gr, gc


def _norm_matmul_grid_kernel(x_ref, g_ref, mod_ref, w_ref, o0_ref, o1_ref, h_ref, res_ref):
    j = pl.program_id(1)
    gr, gc, d = x_ref.shape

    @pl.when(j == 0)
    def _():
        h = _norm_mod(x_ref[...].reshape(gr * gc, d), g_ref[...], mod_ref[0:1, :], mod_ref[1:2, :])
        h_ref[...] = _bf(h)

    res = _dot(h_ref[...], w_ref[...])
    nh = w_ref.shape[1]
    nq = nh // LANE
    for q in range(nq):
        res_ref[q] = res[:, q * LANE:(q + 1) * LANE]

    def scatter(o_ref):
        for c in range(gc):
            for q in range(nq):
                o_ref[:, c * nh + q * LANE:c * nh + (q + 1) * LANE] = (
                    res_ref[q, pl.ds(c, gr, stride=gc), :])

    @pl.when(j == 0)
    def _():
        scatter(o0_ref)

    @pl.when(j == 1)
    def _():
        scatter(o1_ref)


def _norm_matmul_grid(x, n_lat, rows, g, mods_l, w_bf):
    d = x.shape[1]
    nh = w_bf.shape[1] // 2
    gr, gc = _grid_tile(rows)
    ncg = GRID_W // gc
    x3 = x.reshape(x.shape[0] // GRID_W, GRID_W, d)
    out_spec = pl.BlockSpec((gr, gc * nh), lambda i, j: (i // ncg, i % ncg))
    sds = jax.ShapeDtypeStruct((n_lat // GRID_W, GRID_W * nh), F32)
    return pl.pallas_call(
        _norm_matmul_grid_kernel,
        grid=(n_lat // ROW_TILE, 2),
        in_specs=[
            pl.BlockSpec((gr, gc, d), lambda i, j: (i // ncg, i % ncg, 0)),
            pl.BlockSpec((1, d), lambda i, j: (0, 0)),
            pl.BlockSpec((None, N_MOD, d), lambda i, j: ((i // ncg) * gr // rows, 0, 0)),
            pl.BlockSpec((d, nh), lambda i, j: (0, j)),
        ],
        out_specs=[out_spec, out_spec],
        out_shape=[sds, sds],
        scratch_shapes=[pltpu.VMEM((ROW_TILE, d), BF16),
                        pltpu.VMEM((nh // LANE, ROW_TILE, LANE), F32)],
        compiler_params=_cparams(("parallel", "arbitrary")),
    )(x3, g.reshape(1, d), mods_l, w_bf)


def _mlp_kernel(x_ref, g_ref, mod_ref, w1_ref, w2_ref, fn_ref, o_ref, h_ref, *, final_norm):
    j = pl.program_id(1)
    slab = 256
    n_slab = x_ref.shape[0] // slab

    @pl.when(j == 0)
    def _():
        for s in range(n_slab):
            rs = slice(s * slab, (s + 1) * slab)
            h_ref[rs, :] = _bf(_norm_mod(x_ref[rs, :], g_ref[...], mod_ref[3:4, :],
                                         mod_ref[4:5, :]))
        o_ref[...] = jnp.zeros_like(o_ref)

    a = _dot(h_ref[...], w1_ref[...])
    a = _bf(jnp.square(jnp.maximum(a, 0.0)))
    nsplit = 2
    wn = o_ref.shape[1] // nsplit
    for s in range(nsplit):
        o_ref[:, s * wn:(s + 1) * wn] += _dot(a, w2_ref[:, s * wn:(s + 1) * wn])

    @pl.when(j == pl.num_programs(1) - 1)
    def _():
        for s in range(n_slab):
            rs = slice(s * slab, (s + 1) * slab)
            xo = x_ref[rs, :] + mod_ref[5:6, :] * o_ref[rs, :]
            if final_norm:
                ms = jnp.mean(xo * xo, axis=-1, keepdims=True)
                xo = xo * lax.rsqrt(ms + NORM_EPS) * fn_ref[...]
            o_ref[rs, :] = xo


def _mlp(x, g, mods_l, w1_bf, w2_bf, fn, seg_of_tile, *, tm, tile0, n_tiles, final_norm):
    d = x.shape[1]
    dff = w1_bf.shape[1]
    tf = 512
    return pl.pallas_call(
        functools.partial(_mlp_kernel, final_norm=final_norm),
        grid=(n_tiles, dff // tf),
        in_specs=[
            pl.BlockSpec((tm, d), lambda i, j: (i + tile0, 0), pipeline_mode=pl.Buffered(1)),
            pl.BlockSpec((1, d), lambda i, j: (0, 0)),
            pl.BlockSpec((None, N_MOD, d), lambda i, j: (seg_of_tile(i + tile0), 0, 0)),
            pl.BlockSpec((d, tf), lambda i, j: (0, j)),
            pl.BlockSpec((tf, d), lambda i, j: (j, 0)),
            pl.BlockSpec((1, d), lambda i, j: (0, 0)),
        ],
        out_specs=pl.BlockSpec((tm, d), lambda i, j: (i, 0)),
        out_shape=jax.ShapeDtypeStruct((n_tiles * tm, d), F32),
        scratch_shapes=[pltpu.VMEM((tm, d), BF16)],
        compiler_params=_cparams(("parallel", "arbitrary")),
    )(x, g.reshape(1, d), mods_l, w1_bf, w2_bf, fn.reshape(1, d))


def _proj_res_kernel(y_ref, x_ref, mod_ref, w_ref, o_ref, yt_ref):
    k = w_ref.shape[0]
    gr, gc, d = x_ref.shape
    nq = k // LANE
    for c in range(gc):
        for q in range(nq):
            yt_ref[q, pl.ds(c, gr, stride=gc), :] = (
                y_ref[:, c * k + q * LANE:c * k + (q + 1) * LANE])
    yt = jnp.concatenate([_bf(yt_ref[q]) for q in range(nq)], axis=1)
    out = x_ref[...].reshape(gr * gc, d) + mod_ref[2:3, :] * _dot(yt, w_ref[...])
    o_ref[...] = out.reshape(gr, gc, d)


def _proj_res(y_grid, x, n_lat, rows, mods_l, w_bf):
    k, d = w_bf.shape
    gr, gc = _grid_tile(rows)
    ncg = GRID_W // gc
    x3 = x.reshape(x.shape[0] // GRID_W, GRID_W, d)
    tok_spec = pl.BlockSpec((gr, gc, d), lambda i: (i // ncg, i % ncg, 0))
    out = pl.pallas_call(
        _proj_res_kernel,
        grid=(n_lat // ROW_TILE,),
        in_specs=[
            pl.BlockSpec((gr, gc * k), lambda i: (i // ncg, i % ncg)),
            tok_spec,
            pl.BlockSpec((None, N_MOD, d), lambda i: ((i // ncg) * gr // rows, 0, 0)),
            pl.BlockSpec((k, d), lambda i: (0, 0)),
        ],
        out_specs=tok_spec,
        out_shape=jax.ShapeDtypeStruct((n_lat // GRID_W, GRID_W, d), F32),
        scratch_shapes=[pltpu.VMEM((k // LANE, ROW_TILE, LANE), F32)],
        compiler_params=_cparams(("parallel",)),
    )(y_grid, x3, mods_l, w_bf)
    return out.reshape(n_lat, d)


def _head_sum(z, e_bf):
    parts = []
    for gi in range(z.shape[1] // GROUP_W):
        zs = z[:, gi * GROUP_W:(gi + 1) * GROUP_W]
        parts.append(_dot_exact_rhs(zs, e_bf))
    return jnp.concatenate(parts, axis=1)


def _rwkv_prep_kernel(p_ref, pp_ref, pn_ref, mu_ref, w0_ref, a0_ref, w2_ref, a2_ref, g2_ref,
                      kk_ref, ka_ref, rk_ref, e_ref,
                      r_o, v_o, kk_o, lw0_o, lw1_o, kd0_o, kd1_o, b0_o, b1_o, g_o, bonus_o,
                      *, n_lat_tiles, lat_tiles_per_seq, ctx_tiles_per_seq, aw):
    i = pl.program_id(0)
    tp = p_ref.shape[0]
    in_ctx = i >= n_lat_tiles
    pos = jnp.where(in_ctx, (i - n_lat_tiles) % ctx_tiles_per_seq, i % lat_tiles_per_seq)
    last = jnp.where(in_ctx, ctx_tiles_per_seq - 1, lat_tiles_per_seq - 1)
    is_first = pos == 0
    is_last = pos == last
    rowid = lax.broadcasted_iota(jnp.int32, (tp, 1), 0)

    def shifted(c0, c1):
        x = p_ref[:, c0:c1]
        prow = jnp.where(is_first, 0.0, pp_ref[SUBLANE - 1:SUBLANE, c0:c1])
        nrow = jnp.where(is_last, 0.0, pn_ref[0:1, c0:c1])
        xp = jnp.where(rowid == 0, prow, pltpu.roll(x, 1, 0))
        xn = jnp.where(rowid == tp - 1, nrow, pltpu.roll(x, tp - 1, 0))
        return x + mu_ref[0:1, c0:c1] * (xp - x) + mu_ref[1:2, c0:c1] * (xn - x)

    e_bf = e_ref[...]
    r = shifted(0, aw)
    k = shifted(aw, 2 * aw)
    v = shifted(2 * aw, 3 * aw)
    gd = shifted(3 * aw, 3 * aw + A_LORA_G)
    lo0 = 3 * aw + A_LORA_G
    xl = shifted(lo0, lo0 + 2 * A_LORA_W + 2 * A_LORA_A)

    r_o[...] = r
    v_o[...] = v
    g_o[...] = _dot(_bf(_sigmoid(gd)), g2_ref[...])

    kx = k * kk_ref[...]
    kn = jnp.sqrt(_head_sum(kx * kx, e_bf))
    kk = kx / jnp.maximum(kn, 1e-12)
    kk_o[...] = kk

    xl_t = _bf(jnp.tanh(xl))
    xl_b = _bf(xl)
    kd_sum = None
    for d, (lw_o, kd_o, b_o) in enumerate(((lw0_o, kd0_o, b0_o), (lw1_o, kd1_o, b1_o))):
        w_log = -_softplus(-(w0_ref[d:d + 1, :] + _dot(xl_t, w2_ref[d]))) - 0.5
        lw_o[...] = -jnp.exp(w_log)
        asig = _sigmoid(a0_ref[d:d + 1, :] + _dot(xl_b, a2_ref[d]))
        kd = k * (1.0 + (asig - 1.0) * ka_ref[...])
        kd_o[...] = kd
        b_o[...] = kk * asig
        kd_sum = kd if kd_sum is None else kd_sum + kd
    bonus_o[...] = _head_sum(r * rk_ref[...] * kd_sum, e_bf) * v


def _rwkv_prep(pa, mu_p, w0, a0, w2p, a2p, g2, k_k, k_a, r_k, e_bf, *, n_lat, seq, ctx_len):
    r = pa.shape[0]
    ncol = mu_p.shape[1]
    aw = w0.shape[1]
    tp = PREP_TILE
    hb = tp // SUBLANE
    nblk8 = r // SUBLANE
    lw = 2 * A_LORA_W + 2 * A_LORA_A
    row = lambda a: a.reshape(1, aw)
    full = lambda shape: pl.BlockSpec(shape, lambda i: (0,) * len(shape))
    out_spec = pl.BlockSpec((tp, aw), lambda i: (i, 0))
    out_sds = jax.ShapeDtypeStruct((r, aw), F32)
    return pl.pallas_call(
        functools.partial(_rwkv_prep_kernel, n_lat_tiles=n_lat // tp, lat_tiles_per_seq=seq // tp,
                          ctx_tiles_per_seq=ctx_len // tp, aw=aw),
        grid=(r // tp,),
        in_specs=[
            pl.BlockSpec((tp, ncol), lambda i: (i, 0)),
            pl.BlockSpec((SUBLANE, ncol), lambda i: (jnp.maximum(i * hb - 1, 0), 0)),
            pl.BlockSpec((SUBLANE, ncol), lambda i: (jnp.minimum((i + 1) * hb, nblk8 - 1), 0)),
            full((2, ncol)), full((2, aw)), full((2, aw)),
            full((2, lw, aw)), full((2, lw, aw)), full((A_LORA_G, aw)),
            full((1, aw)), full((1, aw)), full((1, aw)), full((GROUP_W, GROUP_W)),
        ],
        out_specs=[out_spec] * 11,
        out_shape=[out_sds] * 11,
        compiler_params=_cparams(("parallel",)),
    )(pa, pa, pa, mu_p, w0, a0, w2p, a2p, g2, row(k_k), row(k_a), row(r_k), e_bf)


def _rwkv_masks(reverse):
    L = CHUNK
    gw = GROUP_W
    row = lax.broadcasted_iota(jnp.int32, (gw, gw), 0)
    col = lax.broadcasted_iota(jnp.int32, (gw, gw), 1)
    same_head = (row >> 6) == (col >> 6)
    tr = row & (L - 1)
    tc = col & (L - 1)
    strict = same_head & ((tr < tc) if reverse else (tr > tc))
    incl = same_head & ((tr <= tc) if reverse else (tr >= tc))
    t_r = lax.broadcasted_iota(jnp.int32, (L, L), 0)
    t_c = lax.broadcasted_iota(jnp.int32, (L, L), 1)
    tri = _bf(jnp.where((t_c >= t_r) if reverse else (t_c <= t_r), 1.0, 0.0))
    return dict(same_head=same_head, strict=strict, incl=incl, eye=row == col, tri=tri)


def _rwkv_chunks(probs):
    L = CHUNK
    gw = GROUP_W
    masks = {rv: _rwkv_masks(rv) for rv in sorted({p[7] for p in probs})}
    mk = [masks[p[7]] for p in probs]
    each = lambda f, *cols: [f(*args) for args in zip(*cols)]
    rep = lambda x: jnp.concatenate([x] * HEADS_PER_GROUP, axis=0)
    nat = lambda x: x[0:L] + x[L:2 * L] + x[2 * L:3 * L] + x[3 * L:4 * L]

    c = [_dot_exact_lhs(m["tri"], p[3]) for m, p in zip(mk, probs)]
    c_l = [ci[(0 if p[7] else L - 1):(1 if p[7] else L), :] for ci, p in zip(c, probs)]

    at_bd, rt_bd, v_bf, bend_bd, kend_bd, lhs, rhs, e_l = [], [], [], [], [], [], [], []
    for (r, v, kk, lw, kd, b, _, _), m, ci, cl in zip(probs, mk, c, c_l):
        bd = lambda x, sh=m["same_head"]: jnp.where(sh, rep(x), 0.0)
        e_nc = jnp.exp(-ci)
        e_end = jnp.exp(cl - ci)
        a_ = bd(-kk * jnp.exp(ci - lw))
        r_ = bd(r * jnp.exp(ci))
        at_bd.append(a_)
        rt_bd.append(r_)
        v_bf.append(_bf(bd(v)))
        bend_bd.append(_bf(bd(b * e_end)))
        kend_bd.append(_bf(bd(kd * e_end)))
        lhs.append(_bf(jnp.concatenate([a_, r_], axis=0)))
        rhs.append(_bf(jnp.concatenate([rep(b * e_nc), rep(kd * e_nc)], axis=0)))
        e_l.append(jnp.exp(cl))

    s = each(_dot_nt, lhs, rhs)
    a_ab = [jnp.where(m["strict"], si[:gw, :gw], 0.0) for m, si in zip(mk, s)]
    a_ak = [_bf(jnp.where(m["strict"], si[:gw, gw:], 0.0)) for m, si in zip(mk, s)]
    r_b = [_bf(jnp.where(m["incl"], si[gw:, :gw], 0.0)) for m, si in zip(mk, s)]
    r_k = [_bf(jnp.where(m["incl"], si[gw:, gw:], 0.0)) for m, si in zip(mk, s)]

    akv = each(_dot, a_ak, v_bf)
    t = [jnp.where(m["eye"], 1.0, 0.0) + a for m, a in zip(mk, a_ab)]
    a_pow = a_ab
    for _ in range(5):
        a_pow_bf = [_bf(a) for a in a_pow]
        a_pow = each(_dot, a_pow_bf, a_pow_bf)
        t = [ti + _dot(_bf(ti), _bf(ap)) for ti, ap in zip(t, a_pow)]
    pq = [_dot(_bf(ti), _bf(jnp.concatenate([a_, ak], axis=1)))
          for ti, a_, ak in zip(t, at_bd, akv)]
    p_bf = [_bf(x[:, :gw]) for x in pq]
    q_bf = [_bf(x[:, gw:]) for x in pq]

    m_ = [jnp.where(m["eye"], el, 0.0) + _dot_tn(be, p)
          for m, el, be, p in zip(mk, e_l, bend_bd, p_bf)]
    n_ = [_dot_tn(be, q) + _dot_tn(ke, v) for be, q, ke, v in zip(bend_bd, q_bf, kend_bd, v_bf)]
    g_ = [r_ + _dot(rb, p) for r_, rb, p in zip(rt_bd, r_b, p_bf)]
    y0 = [_dot(rb, q) + _dot(rk, v) for rb, q, rk, v in zip(r_b, q_bf, r_k, v_bf)]

    h_bf = [_bf(p[6]) for p in probs]
    y = [_dot(_bf(nat(g)), h) + nat(y0i) for g, h, y0i in zip(g_, h_bf, y0)]
    h_new = [_dot(_bf(mi), h) + ni for mi, h, ni in zip(m_, h_bf, n_)]
    return list(zip(y, h_new))


def _rwkv_scan_kernel(rf, vf, kkf, lwf, kdf, bf_, rb, vb, kkb, lwb, kdb, bb,
                      yf_ref, yb_ref, hf_ref, hb_ref):
    @pl.when(pl.program_id(1) == 0)
    def _():
        hf_ref[...] = jnp.zeros_like(hf_ref)
        hb_ref[...] = jnp.zeros_like(hb_ref)

    gs = hf_ref.shape[0]
    probs = []
    for gi in range(gs):
        sl = slice(gi * GROUP_W, (gi + 1) * GROUP_W)
        probs.append((rf[:, sl], vf[:, sl], kkf[:, sl], lwf[:, sl], kdf[:, sl], bf_[:, sl],
                      hf_ref[gi], False))
        probs.append((rb[:, sl], vb[:, sl], kkb[:, sl], lwb[:, sl], kdb[:, sl], bb[:, sl],
                      hb_ref[gi], True))
    res = _rwkv_chunks(probs)
    for gi in range(gs):
        sl = slice(gi * GROUP_W, (gi + 1) * GROUP_W)
        yf_ref[:, sl], hf_ref[gi] = res[2 * gi]
        yb_ref[:, sl], hb_ref[gi] = res[2 * gi + 1]


def _chunk_row_block(b, i, *, reverse, n_lat, seq, ctx_len):
    nc = ctx_len // CHUNK
    nl = seq // CHUNK
    ci = (nc - 1 - i) if reverse else i
    li = (nl - 1 - (i - nc)) if reverse else (i - nc)
    return jnp.where(i < nc, (n_lat + b * ctx_len) // CHUNK + ci, b * nl + li)


def _rwkv_scan(r, v, kk, lw0, lw1, kd0, kd1, b0, b1, *, batch, n_lat, seq, ctx_len):
    rows, aw = r.shape
    gs = RWKV_GROUPS_PER_STEP
    ng = aw // (GROUP_W * gs)
    geo = dict(n_lat=n_lat, seq=seq, ctx_len=ctx_len)

    def spec(reverse):
        return pl.BlockSpec(
            (CHUNK, GROUP_W * gs),
            lambda pg, i: (_chunk_row_block(pg // ng, i, reverse=reverse, **geo), pg % ng))

    sf, sb = spec(False), spec(True)
    sds = jax.ShapeDtypeStruct((rows, aw), F32)
    return pl.pallas_call(
        _rwkv_scan_kernel,
        grid=(batch * ng, (seq + ctx_len) // CHUNK),
        in_specs=[sf] * 6 + [sb] * 6,
        out_specs=[sf, sb],
        out_shape=[sds, sds],
        scratch_shapes=[pltpu.VMEM((gs, GROUP_W, GROUP_W), F32),
                        pltpu.VMEM((gs, GROUP_W, GROUP_W), F32)],
        compiler_params=_cparams(("parallel", "arbitrary")),
    )(r, v, kk, lw0, kd0, b0, r, v, kk, lw1, kd1, b1)


def _gla_scan_kernel(qkf, vf, gdf, qkb, vb, gdb, gw2_ref, gb_ref, of_ref, ob_ref, sf_ref, sb_ref,
                     *, dk, dv):
    @pl.when(pl.program_id(1) == 0)
    def _():
        sf_ref[...] = jnp.zeros_like(sf_ref)
        sb_ref[...] = jnp.zeros_like(sb_ref)

    L = CHUNK
    wk = B_HEADS * dk
    t_r = lax.broadcasted_iota(jnp.int32, (L, L), 0)
    t_c = lax.broadcasted_iota(jnp.int32, (L, L), 1)

    qh, kh, keh, vh, dech, keeph, sth = [], [], [], [], [], [], []
    for d, (qk_ref, v_ref, gd_ref, st_ref) in enumerate(((qkf, vf, gdf, sf_ref),
                                                         (qkb, vb, gdb, sb_ref))):
        reverse = d == 1
        keep = (t_c >= t_r) if reverse else (t_c <= t_r)
        tri = _bf(jnp.where(keep, 1.0, 0.0))
        z = _dot(_bf(gd_ref[...]), gw2_ref[d]) + gb_ref[d:d + 1, :]
        la = -_softplus(-z) * (1.0 / B_GATE_TAU)
        cum = _dot_exact_lhs(tri, la)
        last = 0 if reverse else L - 1
        cl = cum[last:last + 1, :]
        q_in = _bf(qk_ref[:, :wk] * (dk ** -0.5) * jnp.exp(cum))
        k_in = _bf(qk_ref[:, wk:] * jnp.exp(-cum))
        k_end = _bf(qk_ref[:, wk:] * jnp.exp(cl - cum))
        dec = jnp.exp(cl)
        for h in range(B_HEADS):
            ks = slice(h * dk, (h + 1) * dk)
            qh.append(q_in[:, ks])
            kh.append(k_in[:, ks])
            keh.append(k_end[:, ks])
            vh.append(_bf(v_ref[:, h * dv:(h + 1) * dv]))
            dech.append(dec[:, ks])
            keeph.append(keep)
            sth.append(st_ref[h])

    sc = [_bf(jnp.where(kp, _dot_nt(q, k), 0.0)) for kp, q, k in zip(keeph, qh, kh)]
    inter = [_dot_nt(q, _bf(st)) for q, st in zip(qh, sth)]
    kvt = [_dot_tn(v, ke) for v, ke in zip(vh, keh)]
    out = [_dot(s_, v) + it for s_, v, it in zip(sc, vh, inter)]
    for d, (o_ref, st_ref) in enumerate(((of_ref, sf_ref), (ob_ref, sb_ref))):
        for h in range(B_HEADS):
            i = d * B_HEADS + h
            o_ref[:, h * dv:(h + 1) * dv] = out[i]
            st_ref[h] = sth[i] * dech[i] + kvt[i]


def _gla_scan(pb, gw2p, gb, *, batch, n_lat, seq, ctx_len, wk, wv, col0):
    rows = pb.shape[0]
    dk, dv = wk // B_HEADS, wv // B_HEADS
    geo = dict(n_lat=n_lat, seq=seq, ctx_len=ctx_len)
    gd_blk = (col0 + 2 * wk + 2 * wv) // LANE

    def specs(reverse):
        rb = lambda b, i: _chunk_row_block(b, i, reverse=reverse, **geo)
        return [
            pl.BlockSpec((CHUNK, 2 * wk), lambda b, i: (rb(b, i), col0 // (2 * wk))),
            pl.BlockSpec((CHUNK, wv), lambda b, i: (rb(b, i), (col0 + 2 * wk) // wv)),
            pl.BlockSpec((CHUNK, LANE), lambda b, i: (rb(b, i), gd_blk)),
        ]

    def out_spec(reverse):
        return pl.BlockSpec((CHUNK, wv),
                            lambda b, i: (_chunk_row_block(b, i, reverse=reverse, **geo), 0))

    sds = jax.ShapeDtypeStruct((rows, wv), F32)
    return pl.pallas_call(
        functools.partial(_gla_scan_kernel, dk=dk, dv=dv),
        grid=(batch, (seq + ctx_len) // CHUNK),
        in_specs=specs(False) + specs(True) + [
            pl.BlockSpec((2, LANE, wk), lambda b, i: (0, 0, 0)),
            pl.BlockSpec((2, wk), lambda b, i: (0, 0)),
        ],
        out_specs=[out_spec(False), out_spec(True)],
        out_shape=[sds, sds],
        scratch_shapes=[pltpu.VMEM((B_HEADS, dv, dk), F32), pltpu.VMEM((B_HEADS, dv, dk), F32)],
        compiler_params=_cparams(("parallel", "arbitrary")),
    )(pb, pb, pb, pb, pb, pb, gw2p, gb)


def _mixer_out_kernel(yf, yb, bonus, g, of, ob, gate, x_ref, mod_ref, lnw, lnb, gn, e_ref, w_ref,
                      o_ref, h_ref, *, aw, dv):
    e_bf = e_ref[...]
    y = yf[...] + yb[...]
    inv = 1.0 / A_HEAD_DIM
    mean = _head_sum(y, e_bf) * inv
    dlt = y - mean
    var = _head_sum(dlt * dlt, e_bf) * inv
    yn = dlt * lax.rsqrt(var + A_LN_EPS) * lnw[...] + lnb[...]
    h_ref[:, :aw] = _bf((yn + bonus[...]) * g[...])

    o = of[...] + ob[...]
    gt = _silu(gate[...])
    for h in range(B_HEADS):
        sl = slice(h * dv, (h + 1) * dv)
        oh = o[:, sl]
        ms = jnp.mean(oh * oh, axis=-1, keepdims=True)
        h_ref[:, aw + h * dv:aw + (h + 1) * dv] = _bf(oh * lax.rsqrt(ms + NORM_EPS) * gn[...] * gt[:, sl])

    o_ref[...] = x_ref[...] + mod_ref[2:3, :] * _dot(h_ref[...], w_ref[...])


def _mixer_out(yf, yb, bonus, g, of, ob, pb, x, mods_l, ln_w, ln_b, gla_norm, e_bf, w_bf,
               seg_of_tile, *, wk, col0):
    rows, d = x.shape
    aw = yf.shape[1]
    wv = of.shape[1]
    dv = wv // B_HEADS
    tm = 256
    tiles_per_row_tile = ROW_TILE // tm
    rs = lambda w: pl.BlockSpec((tm, w), lambda i: (i, 0))
    full = lambda shape: pl.BlockSpec(shape, lambda i: (0,) * len(shape))
    return pl.pallas_call(
        functools.partial(_mixer_out_kernel, aw=aw, dv=dv),
        grid=(rows // tm,),
        in_specs=[
            rs(aw), rs(aw), rs(aw), rs(aw), rs(wv), rs(wv),
            pl.BlockSpec((tm, wv), lambda i: (i, (col0 + 2 * wk + wv) // wv)),
            rs(d),
            pl.BlockSpec((None, N_MOD, d), lambda i: (seg_of_tile(i // tiles_per_row_tile), 0, 0)),
            full((1, aw)), full((1, aw)), full((1, dv)), full((GROUP_W, GROUP_W)),
            full((aw + wv, d)),
        ],
        out_specs=rs(d),
        out_shape=jax.ShapeDtypeStruct((rows, d), F32),
        scratch_shapes=[pltpu.VMEM((tm, aw + wv), BF16)],
        compiler_params=_cparams(("parallel",)),
    )(yf, yb, bonus, g, of, ob, pb, x, mods_l, ln_w.reshape(1, aw), ln_b.reshape(1, aw),
      gla_norm.reshape(1, dv), e_bf, w_bf)


def _lru_kernel(*refs, reverse, n_blk, combine):
    if combine:
        (x_ref, xp_ref, xn_ref, h0_ref, cw_ref, cb_ref, wc_ref, ba_ref, bx_ref, lam_ref,
         hf_ref, gate_ref, out_ref, hl_ref, a_sc, u_sc, hs_sc, carry_sc) = refs
    else:
        (x_ref, xp_ref, xn_ref, h0_ref, cw_ref, cb_ref, wc_ref, ba_ref, bx_ref, lam_ref,
         out_ref, hl_ref, a_sc, u_sc, hs_sc, carry_sc) = refs
    j = pl.program_id(2)
    blk = (n_blk - 1 - j) if reverse else j
    lb, ct = x_ref.shape
    cblk = wc_ref.shape[1]
    is_first = blk == 0
    is_last = blk == n_blk - 1
    rowid = lax.broadcasted_iota(jnp.int32, (lb, 1), 0)

    x = x_ref[...]
    p6 = jnp.where(is_first, 0.0, xp_ref[SUBLANE - 2:SUBLANE - 1, :])
    p7 = jnp.where(is_first, 0.0, xp_ref[SUBLANE - 1:SUBLANE, :])
    n0 = jnp.where(is_last, 0.0, xn_ref[0:1, :])
    xm1 = jnp.where(rowid == 0, p7, pltpu.roll(x, 1, 0))
    xm2 = jnp.where(rowid == 0, p6, jnp.where(rowid == 1, p7, pltpu.roll(x, 2, 0)))
    xp1 = jnp.where(rowid == lb - 1, n0, pltpu.roll(x, lb - 1, 0))
    xs = (cb_ref[...] + cw_ref[0:1, :] * xm2 + cw_ref[1:2, :] * xm1 + cw_ref[2:3, :] * x
          + cw_ref[3:4, :] * xp1)

    for n in range(ct // cblk):
        sl = slice(n * cblk, (n + 1) * cblk)
        xb = xs[:, sl]
        ri = _dot(_bf(xb), wc_ref[n])
        rg = _sigmoid(ri[:, :cblk] + ba_ref[:, sl])
        ig = _sigmoid(ri[:, cblk:] + bx_ref[:, sl])
        log_a = -C_CONST * rg * _softplus(-lam_ref[:, sl])
        a = jnp.exp(log_a)
        a_sc[:, sl] = a
        u_sc[:, sl] = jnp.sqrt(1.0 - a * a) * (ig * xb)

    @pl.when(j == 0)
    def _():
        carry_sc[...] = h0_ref[0:1, :]

    def step(t, h):
        tt = (lb - 1 - t) if reverse else t
        h = a_sc[pl.ds(tt, 1), :] * h + u_sc[pl.ds(tt, 1), :]
        hs_sc[pl.ds(tt, 1), :] = h
        return h

    h = lax.fori_loop(0, lb, step, carry_sc[...], unroll=8)
    carry_sc[...] = h
    hl_ref[...] = jnp.broadcast_to(h, hl_ref.shape)
    if combine:
        out_ref[...] = (hs_sc[...] + hf_ref[...]) * jax.nn.gelu(gate_ref[...])
    else:
        out_ref[...] = hs_sc[...]


def _lru_pass(x_arr, x_specs, h0, conv_w, conv_b, wcat, ba, bx, lam, out_shape, out_spec, extra,
              *, batch, n_blk, lb, reverse):
    c = conv_w.shape[1]
    ct = 1024
    nct = c // ct
    cblk = wcat.shape[1]
    chan = lambda rows_: pl.BlockSpec((rows_, ct), lambda b, k, j: (0, k))
    in_specs = list(x_specs) + [
        pl.BlockSpec((SUBLANE, ct), lambda b, k, j: (b, k)),
        chan(C_CONV), chan(1),
        pl.BlockSpec((ct // cblk, cblk, 2 * cblk), lambda b, k, j: (k, 0, 0)),
        chan(1), chan(1), chan(1),
    ] + [s for _, s in extra]
    args = [x_arr, x_arr, x_arr, h0, conv_w, conv_b.reshape(1, c), wcat, ba.reshape(1, c),
            bx.reshape(1, c), lam.reshape(1, c)] + [a for a, _ in extra]
    return pl.pallas_call(
        functools.partial(_lru_kernel, reverse=reverse, n_blk=n_blk, combine=bool(extra)),
        grid=(batch, nct, n_blk),
        in_specs=in_specs,
        out_specs=[out_spec, pl.BlockSpec((SUBLANE, ct), lambda b, k, j: (b, k))],
        out_shape=[out_shape, jax.ShapeDtypeStruct((batch * SUBLANE, c), F32)],
        scratch_shapes=[pltpu.VMEM((lb, ct), F32), pltpu.VMEM((lb, ct), F32),
                        pltpu.VMEM((lb, ct), F32), pltpu.VMEM((1, ct), F32)],
        compiler_params=_cparams(("parallel", "parallel", "arbitrary")),
    )(*args)


def _rglru(hw_ctx, gate_grid, x_grid, conv_w, conv_b, wa, ba, wx, bx, lam, *, batch, n_lat, seq,
           ctx_len):
    c = conv_w.shape[1]
    ct = 1024
    rows = seq // GRID_W
    lbc = 128
    ncb = ctx_len // lbc
    xcol = c // ct
    per_col = c // ct

    wcat = [_bf(jnp.concatenate([wa[d], wx[d]], axis=-1)) for d in range(2)]
    zeros_state = jnp.zeros((batch * SUBLANE, c), F32)

    def ctx_specs(reverse):
        blk = (lambda j: ncb - 1 - j) if reverse else (lambda j: j)
        per8 = lbc // SUBLANE
        cur = pl.BlockSpec((lbc, ct), lambda b, k, j: (b * ncb + blk(j), xcol + k))
        prv = pl.BlockSpec((SUBLANE, ct), lambda b, k, j: (
            b * ncb * per8 + jnp.maximum(blk(j) * per8 - 1, 0), xcol + k))
        nxt = pl.BlockSpec((SUBLANE, ct), lambda b, k, j: (
            b * ncb * per8 + jnp.minimum((blk(j) + 1) * per8, ncb * per8 - 1), xcol + k))
        return cur, prv, nxt

    def lat_specs(reverse):
        blk = (lambda j: GRID_W - 1 - j) if reverse else (lambda j: j)
        r8 = rows // SUBLANE
        cur = pl.BlockSpec((rows, ct), lambda b, k, j: (b, blk(j) * per_col + k))
        prv = pl.BlockSpec((SUBLANE, ct), lambda b, k, j: (
            b * r8 + r8 - 1, jnp.maximum(blk(j) - 1, 0) * per_col + k))
        nxt = pl.BlockSpec((SUBLANE, ct), lambda b, k, j: (
            b * r8, jnp.minimum(blk(j) + 1, GRID_W - 1) * per_col + k))
        return cur, prv, nxt

    common = dict(batch=batch)
    ctx_scratch_sds = jax.ShapeDtypeStruct((batch * ctx_len, c), F32)
    ctx_out = lambda reverse: pl.BlockSpec(
        (lbc, ct), lambda b, k, j: (b * ncb + ((ncb - 1 - j) if reverse else j), k))
    _, st = _lru_pass(hw_ctx, ctx_specs(False), zeros_state, conv_w, conv_b, wcat[0], ba[0], bx[0],
                      lam[0], ctx_scratch_sds, ctx_out(False), [], n_blk=ncb, lb=lbc,
                      reverse=False, **common)
    hf_sds = jax.ShapeDtypeStruct((n_lat, c), F32)
    hf_spec = lambda reverse: pl.BlockSpec(
        (rows, ct), lambda b, k, j: (b * GRID_W + ((GRID_W - 1 - j) if reverse else j), k))
    hf, _ = _lru_pass(x_grid, lat_specs(False), st, conv_w, conv_b, wcat[0], ba[0], bx[0], lam[0],
                      hf_sds, hf_spec(False), [], n_blk=GRID_W, lb=rows, reverse=False, **common)
    _, st = _lru_pass(hw_ctx, ctx_specs(True), zeros_state, conv_w, conv_b, wcat[1], ba[1], bx[1],
                      lam[1], ctx_scratch_sds, ctx_out(True), [], n_blk=ncb, lb=lbc,
                      reverse=True, **common)
    grid_spec = pl.BlockSpec((rows, ct), lambda b, k, j: (b, (GRID_W - 1 - j) * per_col + k))
    y_sds = jax.ShapeDtypeStruct((n_lat // GRID_W, GRID_W * c), F32)
    y, _ = _lru_pass(x_grid, lat_specs(True), st, conv_w, conv_b, wcat[1], ba[1], bx[1], lam[1],
                     y_sds, grid_spec, [(hf, hf_spec(True)), (gate_grid, grid_spec)],
                     n_blk=GRID_W, lb=rows, reverse=True, **common)
    return y


def _pad_cols(w, n):
    return jnp.pad(w, ((0, 0),) * (w.ndim - 1) + ((0, n - w.shape[-1]),))


def kernel(x, c, ctx, c_ctx, mod_w, mod_b, norm1, norm2, mlp_w1, mlp_w2, ab_w_in, ab_w_out, rw_mu,
           rw_w0, rw_w2, rw_a0, rw_a2, rw_g2, rw_kk, rw_ka, rw_rk, rw_ln_w, rw_ln_b, gla_gw2,
           gla_gb, gla_norm, lru_w_in, lru_w_out, lru_conv_w, lru_conv_b, lru_wa, lru_ba, lru_wx,
           lru_bx, lru_lam, final_norm):
    batch, seq, d = x.shape
    ctx_len = ctx.shape[1]
    depth = mod_w.shape[0]
    n_lat = batch * seq
    aw = rw_w0.shape[-1]
    wk = gla_gb.shape[-1]
    wv = ab_w_out.shape[1] - aw
    a_cols = rw_mu.shape[-1]
    b_cols = ab_w_in.shape[-1] - a_cols
    assert depth == 2 and batch + 1 <= SUBLANE
    assert seq % ROW_TILE == 0 and (batch * ctx_len) % ROW_TILE == 0
    assert ctx_len % PREP_TILE == 0 and ctx_len % 128 == 0 and seq % (GRID_W * SUBLANE) == 0
    assert aw % GROUP_W == 0 and wv == aw and 2 * wk == wv

    tiles_per_seq = seq // ROW_TILE
    seg_of_tile = lambda i: jnp.minimum(i // tiles_per_seq, batch)

    xs = jnp.concatenate([x.reshape(n_lat, d), ctx.reshape(batch * ctx_len, d)], axis=0)
    cvec = jnp.concatenate(
        [c, c_ctx[None, :], jnp.zeros((SUBLANE - batch - 1, d), F32)], axis=0)
    mods = _mods(cvec, mod_w, mod_b)

    hid = jnp.arange(GROUP_W) // A_HEAD_DIM
    e_bf = _bf(hid[:, None] == hid[None, :])

    a_pad = -(-a_cols // LANE) * LANE
    col0 = -(-a_cols // (2 * wk)) * (2 * wk)
    n_tiles = 3
    tn = -(-(col0 + b_cols) // (n_tiles * LANE)) * LANE
    w_ab = _bf(jnp.concatenate(
        [_pad_cols(ab_w_in[0][:, :a_cols], col0),
         _pad_cols(ab_w_in[0][:, a_cols:], n_tiles * tn - col0)], axis=1))
    pa = pb = _norm_matmul(xs, norm1[0], mods[0], w_ab, seg_of_tile, tn=tn)

    lora_w = 2 * A_LORA_W + 2 * A_LORA_A
    w2p = jnp.zeros((2, lora_w, aw), F32)
    a2p = jnp.zeros((2, lora_w, aw), F32)
    for dd in range(2):
        w2p = w2p.at[dd, dd * A_LORA_W:(dd + 1) * A_LORA_W].set(rw_w2[0, dd])
        a2p = a2p.at[dd, 2 * A_LORA_W + dd * A_LORA_A:2 * A_LORA_W + (dd + 1) * A_LORA_A].set(
            rw_a2[0, dd])
    (r_, v_, kk_, lw0, lw1, kd0, kd1, b0, b1, g_, bonus) = _rwkv_prep(
        pa, _pad_cols(rw_mu[0], a_pad), rw_w0[0], rw_a0[0], _bf(w2p), _bf(a2p), _bf(rw_g2[0]),
        rw_kk[0], rw_ka[0], rw_rk[0].reshape(-1), e_bf, n_lat=n_lat, seq=seq, ctx_len=ctx_len)
    yf, yb = _rwkv_scan(r_, v_, kk_, lw0, lw1, kd0, kd1, b0, b1, batch=batch, n_lat=n_lat, seq=seq,
                        ctx_len=ctx_len)

    gw2p = jnp.zeros((2, LANE, wk), F32)
    for dd in range(2):
        gw2p = gw2p.at[dd, dd * B_LORA:(dd + 1) * B_LORA].set(gla_gw2[0, dd])
    of, ob = _gla_scan(pb, _bf(gw2p), gla_gb[0], batch=batch, n_lat=n_lat, seq=seq,
                       ctx_len=ctx_len, wk=wk, wv=wv, col0=col0)

    xs = _mixer_out(yf, yb, bonus, g_, of, ob, pb, xs, mods[0], rw_ln_w[0], rw_ln_b[0],
                    gla_norm[0], e_bf, _bf(ab_w_out[0]), seg_of_tile, wk=wk, col0=col0)
    seg_of_mlp_tile = lambda i: jnp.minimum(i // (seq // MLP_TILE), batch)
    ctx_tiles = (batch * ctx_len) // ROW_TILE
    w1_bf, w2_bf = _bf(mlp_w1[0]), _bf(mlp_w2[0])
    x_lat = _mlp(xs, norm2[0], mods[0], w1_bf, w2_bf, final_norm, seg_of_mlp_tile, tm=MLP_TILE,
                 tile0=0, n_tiles=n_lat // MLP_TILE, final_norm=False)
    x_ctx = _mlp(xs, norm2[0], mods[0], w1_bf, w2_bf, final_norm, seg_of_tile, tm=ROW_TILE,
                 tile0=n_lat // ROW_TILE, n_tiles=ctx_tiles, final_norm=False)

    w_lru = _bf(lru_w_in[0])
    hw_ctx = _norm_matmul(x_ctx, norm1[1], mods[1], w_lru, lambda i: batch, tn=2048)
    gate_grid, x_grid = _norm_matmul_grid(x_lat, n_lat, seq // GRID_W, norm1[1], mods[1], w_lru)
    y = _rglru(hw_ctx, gate_grid, x_grid, lru_conv_w[0], lru_conv_b[0], lru_wa[0], lru_ba[0],
               lru_wx[0], lru_bx[0], lru_lam[0], batch=batch, n_lat=n_lat, seq=seq,
               ctx_len=ctx_len)
    xl = _proj_res(y, x_lat, n_lat, seq // GRID_W, mods[1], _bf(lru_w_out[0]))
    out = _mlp(xl, norm2[1], mods[1], _bf(mlp_w1[1]), _bf(mlp_w2[1]), final_norm, seg_of_mlp_tile,
               tm=MLP_TILE, tile0=0, n_tiles=n_lat // MLP_TILE, final_norm=True)
    return out.reshape(batch, seq, d)
```

```python
import functools

import jax
import jax.numpy as jnp
from jax import lax
from jax.experimental import pallas as pl
from jax.experimental.pallas import tpu as pltpu

F32 = jnp.float32
BF16 = jnp.bfloat16

NORM_EPS = 1e-6
GRID_W = 64
N_MOD = 6

A_HEAD_DIM = 64
A_LORA_W = 96
A_LORA_A = 96
A_LORA_G = 256
A_LN_EPS = 64e-5
HEADS_PER_GROUP = 4
GROUP_W = HEADS_PER_GROUP * A_HEAD_DIM
CHUNK = 64
RWKV_GROUPS_PER_STEP = 4

B_HEADS = 4
B_LORA = 16
B_GATE_TAU = 16.0

C_BLOCKS = 8
C_CONV = 4
C_CONST = 8.0

LANE = 128
SUBLANE = 8
ROW_TILE = 512
MLP_TILE = 1024
PREP_TILE = 128
VMEM_LIMIT = 56 * 1024 * 1024


def _cparams(sem):
    return pltpu.CompilerParams(dimension_semantics=sem, vmem_limit_bytes=VMEM_LIMIT)


def _bf(x):
    return x.astype(BF16)


def _dot(a, b):
    return jnp.dot(a, b, preferred_element_type=F32)


def _dot_nt(a, b):
    return lax.dot_general(a, b, (((1,), (1,)), ((), ())), preferred_element_type=F32)


def _dot_tn(a, b):
    return lax.dot_general(a, b, (((0,), (0,)), ((), ())), preferred_element_type=F32)


def _split3(x):
    hi = _bf(x)
    r1 = x - hi.astype(F32)
    mid = _bf(r1)
    lo = _bf(r1 - mid.astype(F32))
    return hi, mid, lo


def _dot_exact_lhs(m_bf, x):
    hi, mid, lo = _split3(x)
    return _dot(m_bf, hi) + _dot(m_bf, mid) + _dot(m_bf, lo)


def _dot_exact_rhs(x, m_bf):
    hi, mid, lo = _split3(x)
    return _dot(hi, m_bf) + _dot(mid, m_bf) + _dot(lo, m_bf)


def _softplus(x):
    return jnp.maximum(x, 0.0) + jnp.log(1.0 + jnp.exp(-jnp.abs(x)))


def _sigmoid(x):
    return 0.5 * jnp.tanh(0.5 * x) + 0.5


def _silu(x):
    return x * _sigmoid(x)


def _norm_mod(x, g, shift, scale):
    ms = jnp.mean(x * x, axis=-1, keepdims=True)
    return (x * lax.rsqrt(ms + NORM_EPS) * g) * (1.0 + scale) + shift


def _mods_kernel(c_ref, w_ref, b_ref, o_ref):
    s = _silu(c_ref[...])
    o_ref[...] = _dot(_bf(s), _bf(w_ref[...])) + b_ref[...]


def _mods(cvec, mod_w, mod_b):
    depth, d, n = mod_w.shape
    tn = 1024
    out = pl.pallas_call(
        _mods_kernel,
        grid=(depth, n // tn),
        in_specs=[
            pl.BlockSpec((SUBLANE, d), lambda l, j: (0, 0)),
            pl.BlockSpec((None, d, tn), lambda l, j: (l, 0, j)),
            pl.BlockSpec((None, 1, tn), lambda l, j: (l, 0, j)),
        ],
        out_specs=pl.BlockSpec((None, SUBLANE, tn), lambda l, j: (l, 0, j)),
        out_shape=jax.ShapeDtypeStruct((depth, SUBLANE, n), F32),
        compiler_params=_cparams(("parallel", "parallel")),
    )(cvec, mod_w, mod_b.reshape(depth, 1, n))
    return out.reshape(depth, SUBLANE, N_MOD, d)


def _norm_matmul_kernel(x_ref, g_ref, mod_ref, w_ref, o_ref, h_ref, *, shift_row, scale_row):
    @pl.when(pl.program_id(1) == 0)
    def _():
        h = _norm_mod(x_ref[...], g_ref[...], mod_ref[shift_row:shift_row + 1, :],
                      mod_ref[scale_row:scale_row + 1, :])
        h_ref[...] = _bf(h)

    o_ref[...] = _dot(h_ref[...], w_ref[...])


def _norm_matmul(x, g, mods_l, w_bf, seg_of_tile, *, tn, tile0=0, n_tiles=None, shift_row=0,
                 scale_row=1):
    r, d = x.shape
    n = w_bf.shape[1]
    tm = ROW_TILE
    n_tiles = r // tm if n_tiles is None else n_tiles
    return pl.pallas_call(
        functools.partial(_norm_matmul_kernel, shift_row=shift_row, scale_row=scale_row),
        grid=(n_tiles, n // tn),
        in_specs=[
            pl.BlockSpec((tm, d), lambda i, j: (i + tile0, 0)),
            pl.BlockSpec((1, d), lambda i, j: (0, 0)),
            pl.BlockSpec((None, N_MOD, d), lambda i, j: (seg_of_tile(i + tile0), 0, 0)),
            pl.BlockSpec((d, tn), lambda i, j: (0, j)),
        ],
        out_specs=pl.BlockSpec((tm, tn), lambda i, j: (i, j)),
        out_shape=jax.ShapeDtypeStruct((n_tiles * tm, n), F32),
        scratch_shapes=[pltpu.VMEM((tm, d), BF16)],
        compiler_params=_cparams(("parallel", "arbitrary")),
    )(x, g.reshape(1, d), mods_l, w_bf)


def _grid_tile(rows):
    gr = min(rows, 64)
    gc = ROW_TILE // gr
    assert gc % SUBLANE == 0 and GRID_W % gc == 0 and rows % gr == 0
    return gr, gc


def _norm_matmul_grid_kernel(x_ref, g_ref, mod_ref, w_ref, o0_ref, o1_ref, h_ref, res_ref):
    j = pl.program_id(1)
    gr, gc, d = x_ref.shape

    @pl.when(j == 0)
    def _():
        h = _norm_mod(x_ref[...].reshape(gr * gc, d), g_ref[...], mod_ref[0:1, :], mod_ref[1:2, :])
        h_ref[...] = _bf(h)

    res = _dot(h_ref[...], w_ref[...])
    nh = w_ref.shape[1]
    nq = nh // LANE
    for q in range(nq):
        res_ref[q] = res[:, q * LANE:(q + 1) * LANE]

    def scatter(o_ref):
        for c in range(gc):
            for q in range(nq):
                o_ref[:, c * nh + q * LANE:c * nh + (q + 1) * LANE] = (
                    res_ref[q, pl.ds(c, gr, stride=gc), :])

    @pl.when(j == 0)
    def _():
        scatter(o0_ref)

    @pl.when(j == 1)
    def _():
        scatter(o1_ref)


def _norm_matmul_grid(x, n_lat, rows, g, mods_l, w_bf):
    d = x.shape[1]
    nh = w_bf.shape[1] // 2
    gr, gc = _grid_tile(rows)
    ncg = GRID_W // gc
    x3 = x.reshape(x.shape[0] // GRID_W, GRID_W, d)
    out_spec = pl.BlockSpec((gr, gc * nh), lambda i, j: (i // ncg, i % ncg))
    sds = jax.ShapeDtypeStruct((n_lat // GRID_W, GRID_W * nh), F32)
    return pl.pallas_call(
        _norm_matmul_grid_kernel,
        grid=(n_lat // ROW_TILE, 2),
        in_specs=[
            pl.BlockSpec((gr, gc, d), lambda i, j: (i // ncg, i % ncg, 0)),
            pl.BlockSpec((1, d), lambda i, j: (0, 0)),
            pl.BlockSpec((None, N_MOD, d), lambda i, j: ((i // ncg) * gr // rows, 0, 0)),
            pl.BlockSpec((d, nh), lambda i, j: (0, j)),
        ],
        out_specs=[out_spec, out_spec],
        out_shape=[sds, sds],
        scratch_shapes=[pltpu.VMEM((ROW_TILE, d), BF16),
                        pltpu.VMEM((nh // LANE, ROW_TILE, LANE), F32)],
        compiler_params=_cparams(("parallel", "arbitrary")),
    )(x3, g.reshape(1, d), mods_l, w_bf)


def _mlp_kernel(x_ref, g_ref, mod_ref, w1_ref, w2_ref, fn_ref, o_ref, h_ref, *, final_norm):
    j = pl.program_id(1)
    slab = 256
    n_slab = x_ref.shape[0] // slab

    @pl.when(j == 0)
    def _():
        for s in range(n_slab):
            rs = slice(s * slab, (s + 1) * slab)
            h_ref[rs, :] = _bf(_norm_mod(x_ref[rs, :], g_ref[...], mod_ref[3:4, :],
                                         mod_ref[4:5, :]))
        o_ref[...] = jnp.zeros_like(o_ref)

    a = _dot(h_ref[...], w1_ref[...])
    a = _bf(jnp.square(jnp.maximum(a, 0.0)))
    nsplit = 2
    wn = o_ref.shape[1] // nsplit
    for s in range(nsplit):
        o_ref[:, s * wn:(s + 1) * wn] += _dot(a, w2_ref[:, s * wn:(s + 1) * wn])

    @pl.when(j == pl.num_programs(1) - 1)
    def _():
        for s in range(n_slab):
            rs = slice(s * slab, (s + 1) * slab)
            xo = x_ref[rs, :] + mod_ref[5:6, :] * o_ref[rs, :]
            if final_norm:
                ms = jnp.mean(xo * xo, axis=-1, keepdims=True)
                xo = xo * lax.rsqrt(ms + NORM_EPS) * fn_ref[...]
            o_ref[rs, :] = xo


def _mlp(x, g, mods_l, w1_bf, w2_bf, fn, seg_of_tile, *, tm, tile0, n_tiles, final_norm):
    d = x.shape[1]
    dff = w1_bf.shape[1]
    tf = 512
    return pl.pallas_call(
        functools.partial(_mlp_kernel, final_norm=final_norm),
        grid=(n_tiles, dff // tf),
        in_specs=[
            pl.BlockSpec((tm, d), lambda i, j: (i + tile0, 0), pipeline_mode=pl.Buffered(1)),
            pl.BlockSpec((1, d), lambda i, j: (0, 0)),
            pl.BlockSpec((None, N_MOD, d), lambda i, j: (seg_of_tile(i + tile0), 0, 0)),
            pl.BlockSpec((d, tf), lambda i, j: (0, j)),
            pl.BlockSpec((tf, d), lambda i, j: (j, 0)),
            pl.BlockSpec((1, d), lambda i, j: (0, 0)),
        ],
        out_specs=pl.BlockSpec((tm, d), lambda i, j: (i, 0)),
        out_shape=jax.ShapeDtypeStruct((n_tiles * tm, d), F32),
        scratch_shapes=[pltpu.VMEM((tm, d), BF16)],
        compiler_params=_cparams(("parallel", "arbitrary")),
    )(x, g.reshape(1, d), mods_l, w1_bf, w2_bf, fn.reshape(1, d))


def _proj_res_kernel(y_ref, x_ref, mod_ref, w_ref, o_ref, yt_ref):
    k = w_ref.shape[0]
    gr, gc, d = x_ref.shape
    nq = k // LANE
    for c in range(gc):
        for q in range(nq):
            yt_ref[q, pl.ds(c, gr, stride=gc), :] = (
                y_ref[:, c * k + q * LANE:c * k + (q + 1) * LANE])
    yt = jnp.concatenate([_bf(yt_ref[q]) for q in range(nq)], axis=1)
    out = x_ref[...].reshape(gr * gc, d) + mod_ref[2:3, :] * _dot(yt, w_ref[...])
    o_ref[...] = out.reshape(gr, gc, d)


def _proj_res(y_grid, x, n_lat, rows, mods_l, w_bf):
    k, d = w_bf.shape
    gr, gc = _grid_tile(rows)
    ncg = GRID_W // gc
    x3 = x.reshape(x.shape[0] // GRID_W, GRID_W, d)
    tok_spec = pl.BlockSpec((gr, gc, d), lambda i: (i // ncg, i % ncg, 0))
    out = pl.pallas_call(
        _proj_res_kernel,
        grid=(n_lat // ROW_TILE,),
        in_specs=[
            pl.BlockSpec((gr, gc * k), lambda i: (i // ncg, i % ncg)),
            tok_spec,
            pl.BlockSpec((None, N_MOD, d), lambda i: ((i // ncg) * gr // rows, 0, 0)),
            pl.BlockSpec((k, d), lambda i: (0, 0)),
        ],
        out_specs=tok_spec,
        out_shape=jax.ShapeDtypeStruct((n_lat // GRID_W, GRID_W, d), F32),
        scratch_shapes=[pltpu.VMEM((k // LANE, ROW_TILE, LANE), F32)],
        compiler_params=_cparams(("parallel",)),
    )(y_grid, x3, mods_l, w_bf)
    return out.reshape(n_lat, d)


def _head_sum(z, e_bf):
    parts = []
    for gi in range(z.shape[1] // GROUP_W):
        zs = z[:, gi * GROUP_W:(gi + 1) * GROUP_W]
        parts.append(_dot_exact_rhs(zs, e_bf))
    return jnp.concatenate(parts, axis=1)


def _rwkv_prep_kernel(p_ref, pp_ref, pn_ref, mu_ref, w0_ref, a0_ref, w2_ref, a2_ref, g2_ref,
                      kk_ref, ka_ref, rk_ref, e_ref,
                      r_o, v_o, kk_o, lw0_o, lw1_o, kd0_o, kd1_o, b0_o, b1_o, g_o, bonus_o,
                      *, n_lat_tiles, lat_tiles_per_seq, ctx_tiles_per_seq, aw):
    i = pl.program_id(0)
    tp = p_ref.shape[0]
    in_ctx = i >= n_lat_tiles
    pos = jnp.where(in_ctx, (i - n_lat_tiles) % ctx_tiles_per_seq, i % lat_tiles_per_seq)
    last = jnp.where(in_ctx, ctx_tiles_per_seq - 1, lat_tiles_per_seq - 1)
    is_first = pos == 0
    is_last = pos == last
    rowid = lax.broadcasted_iota(jnp.int32, (tp, 1), 0)

    def shifted(c0, c1):
        x = p_ref[:, c0:c1]
        prow = jnp.where(is_first, 0.0, pp_ref[SUBLANE - 1:SUBLANE, c0:c1])
        nrow = jnp.where(is_last, 0.0, pn_ref[0:1, c0:c1])
        xp = jnp.where(rowid == 0, prow, pltpu.roll(x, 1, 0))
        xn = jnp.where(rowid == tp - 1, nrow, pltpu.roll(x, tp - 1, 0))
        return x + mu_ref[0:1, c0:c1] * (xp - x) + mu_ref[1:2, c0:c1] * (xn - x)

    e_bf = e_ref[...]
    r = shifted(0, aw)
    k = shifted(aw, 2 * aw)
    v = shifted(2 * aw, 3 * aw)
    gd = shifted(3 * aw, 3 * aw + A_LORA_G)
    lo0 = 3 * aw + A_LORA_G
    xl = shifted(lo0, lo0 + 2 * A_LORA_W + 2 * A_LORA_A)

    r_o[...] = r
    v_o[...] = v
    g_o[...] = _dot(_bf(_sigmoid(gd)), g2_ref[...])

    kx = k * kk_ref[...]
    kn = jnp.sqrt(_head_sum(kx * kx, e_bf))
    kk = kx / jnp.maximum(kn, 1e-12)
    kk_o[...] = kk

    xl_t = _bf(jnp.tanh(xl))
    xl_b = _bf(xl)
    kd_sum = None
    for d, (lw_o, kd_o, b_o) in enumerate(((lw0_o, kd0_o, b0_o), (lw1_o, kd1_o, b1_o))):
        w_log = -_softplus(-(w0_ref[d:d + 1, :] + _dot(xl_t, w2_ref[d]))) - 0.5
        lw_o[...] = -jnp.exp(w_log)
        asig = _sigmoid(a0_ref[d:d + 1, :] + _dot(xl_b, a2_ref[d]))
        kd = k * (1.0 + (asig - 1.0) * ka_ref[...])
        kd_o[...] = kd
        b_o[...] = kk * asig
        kd_sum = kd if kd_sum is None else kd_sum + kd
    bonus_o[...] = _head_sum(r * rk_ref[...] * kd_sum, e_bf) * v


def _rwkv_prep(pa, mu_p, w0, a0, w2p, a2p, g2, k_k, k_a, r_k, e_bf, *, n_lat, seq, ctx_len):
    r = pa.shape[0]
    ncol = mu_p.shape[1]
    aw = w0.shape[1]
    tp = PREP_TILE
    hb = tp // SUBLANE
    nblk8 = r // SUBLANE
    lw = 2 * A_LORA_W + 2 * A_LORA_A
    row = lambda a: a.reshape(1, aw)
    full = lambda shape: pl.BlockSpec(shape, lambda i: (0,) * len(shape))
    out_spec = pl.BlockSpec((tp, aw), lambda i: (i, 0))
    out_sds = jax.ShapeDtypeStruct((r, aw), F32)
    return pl.pallas_call(
        functools.partial(_rwkv_prep_kernel, n_lat_tiles=n_lat // tp, lat_tiles_per_seq=seq // tp,
                          ctx_tiles_per_seq=ctx_len // tp, aw=aw),
        grid=(r // tp,),
        in_specs=[
            pl.BlockSpec((tp, ncol), lambda i: (i, 0)),
            pl.BlockSpec((SUBLANE, ncol), lambda i: (jnp.maximum(i * hb - 1, 0), 0)),
            pl.BlockSpec((SUBLANE, ncol), lambda i: (jnp.minimum((i + 1) * hb, nblk8 - 1), 0)),
            full((2, ncol)), full((2, aw)), full((2, aw)),
            full((2, lw, aw)), full((2, lw, aw)), full((A_LORA_G, aw)),
            full((1, aw)), full((1, aw)), full((1, aw)), full((GROUP_W, GROUP_W)),
        ],
        out_specs=[out_spec] * 11,
        out_shape=[out_sds] * 11,
        compiler_params=_cparams(("parallel",)),
    )(pa, pa, pa, mu_p, w0, a0, w2p, a2p, g2, row(k_k), row(k_a), row(r_k), e_bf)


def _rwkv_masks(reverse):
    L = CHUNK
    gw = GROUP_W
    row = lax.broadcasted_iota(jnp.int32, (gw, gw), 0)
    col = lax.broadcasted_iota(jnp.int32, (gw, gw), 1)
    same_head = (row >> 6) == (col >> 6)
    tr = row & (L - 1)
    tc = col & (L - 1)
    strict = same_head & ((tr < tc) if reverse else (tr > tc))
    incl = same_head & ((tr <= tc) if reverse else (tr >= tc))
    t_r = lax.broadcasted_iota(jnp.int32, (L, L), 0)
    t_c = lax.broadcasted_iota(jnp.int32, (L, L), 1)
    tri = _bf(jnp.where((t_c >= t_r) if reverse else (t_c <= t_r), 1.0, 0.0))
    return dict(same_head=same_head, strict=strict, incl=incl, eye=row == col, tri=tri)


def _rwkv_chunks(probs):
    L = CHUNK
    gw = GROUP_W
    masks = {rv: _rwkv_masks(rv) for rv in sorted({p[7] for p in probs})}
    mk = [masks[p[7]] for p in probs]
    each = lambda f, *cols: [f(*args) for args in zip(*cols)]
    rep = lambda x: jnp.concatenate([x] * HEADS_PER_GROUP, axis=0)
    nat = lambda x: x[0:L] + x[L:2 * L] + x[2 * L:3 * L] + x[3 * L:4 * L]

    c = [_dot_exact_lhs(m["tri"], p[3]) for m, p in zip(mk, probs)]
    c_l = [ci[(0 if p[7] else L - 1):(1 if p[7] else L), :] for ci, p in zip(c, probs)]

    at_bd, rt_bd, v_bf, bend_bd, kend_bd, lhs, rhs, e_l = [], [], [], [], [], [], [], []
    for (r, v, kk, lw, kd, b, _, _), m, ci, cl in zip(probs, mk, c, c_l):
        bd = lambda x, sh=m["same_head"]: jnp.where(sh, rep(x), 0.0)
        e_nc = jnp.exp(-ci)
        e_end = jnp.exp(cl - ci)
        a_ = bd(-kk * jnp.exp(ci - lw))
        r_ = bd(r * jnp.exp(ci))
        at_bd.append(a_)
        rt_bd.append(r_)
        v_bf.append(_bf(bd(v)))
        bend_bd.append(_bf(bd(b * e_end)))
        kend_bd.append(_bf(bd(kd * e_end)))
        lhs.append(_bf(jnp.concatenate([a_, r_], axis=0)))
        rhs.append(_bf(jnp.concatenate([rep(b * e_nc), rep(kd * e_nc)], axis=0)))
        e_l.append(jnp.exp(cl))

    s = each(_dot_nt, lhs, rhs)
    a_ab = [jnp.where(m["strict"], si[:gw, :gw], 0.0) for m, si in zip(mk, s)]
    a_ak = [_bf(jnp.where(m["strict"], si[:gw, gw:], 0.0)) for m, si in zip(mk, s)]
    r_b = [_bf(jnp.where(m["incl"], si[gw:, :gw], 0.0)) for m, si in zip(mk, s)]
    r_k = [_bf(jnp.where(m["incl"], si[gw:, gw:], 0.0)) for m, si in zip(mk, s)]

    akv = each(_dot, a_ak, v_bf)
    t = [jnp.where(m["eye"], 1.0, 0.0) + a for m, a in zip(mk, a_ab)]
    a_pow = a_ab
    for _ in range(5):
        a_pow_bf = [_bf(a) for a in a_pow]
        a_pow = each(_dot, a_pow_bf, a_pow_bf)
        t = [ti + _dot(_bf(ti), _bf(ap)) for ti, ap in zip(t, a_pow)]
    h_bf = [_bf(p[6]) for p in probs]
    x = [_dot(_bf(a_), h) + ak for a_, h, ak in zip(at_bd, h_bf, akv)]
    u = [_dot(_bf(ti), _bf(xi)) for ti, xi in zip(t, x)]
    uv = [jnp.concatenate([_bf(ui), v], axis=0) for ui, v in zip(u, v_bf)]
    y = [_dot(_bf(nat(r_)), h) + nat(_dot(jnp.concatenate([rb, rk], axis=1), uvi))
         for r_, h, rb, rk, uvi in zip(rt_bd, h_bf, r_b, r_k, uv)]
    e_col = [jnp.broadcast_to(el, (gw, gw)).T for el in e_l]
    h_new = [ec * p[6] + _dot_tn(jnp.concatenate([be, ke], axis=0), uvi)
             for ec, p, be, ke, uvi in zip(e_col, probs, bend_bd, kend_bd, uv)]
    return list(zip(y, h_new))


def _rwkv_scan_kernel(rf, vf, kkf, lwf, kdf, bf_, rb, vb, kkb, lwb, kdb, bb,
                      yf_ref, yb_ref, hf_ref, hb_ref):
    @pl.when(pl.program_id(1) == 0)
    def _():
        hf_ref[...] = jnp.zeros_like(hf_ref)
        hb_ref[...] = jnp.zeros_like(hb_ref)

    gs = hf_ref.shape[0]
    probs = []
    for gi in range(gs):
        sl = slice(gi * GROUP_W, (gi + 1) * GROUP_W)
        probs.append((rf[:, sl], vf[:, sl], kkf[:, sl], lwf[:, sl], kdf[:, sl], bf_[:, sl],
                      hf_ref[gi], False))
        probs.append((rb[:, sl], vb[:, sl], kkb[:, sl], lwb[:, sl], kdb[:, sl], bb[:, sl],
                      hb_ref[gi], True))
    res = _rwkv_chunks(probs)
    for gi in range(gs):
        sl = slice(gi * GROUP_W, (gi + 1) * GROUP_W)
        yf_ref[:, sl], hf_ref[gi] = res[2 * gi]
        yb_ref[:, sl], hb_ref[gi] = res[2 * gi + 1]


def _chunk_row_block(b, i, *, reverse, n_lat, seq, ctx_len):
    nc = ctx_len // CHUNK
    nl = seq // CHUNK
    ci = (nc - 1 - i) if reverse else i
    li = (nl - 1 - (i - nc)) if reverse else (i - nc)
    return jnp.where(i < nc, (n_lat + b * ctx_len) // CHUNK + ci, b * nl + li)


def _rwkv_scan(r, v, kk, lw0, lw1, kd0, kd1, b0, b1, *, batch, n_lat, seq, ctx_len):
    rows, aw = r.shape
    gs = RWKV_GROUPS_PER_STEP
    ng = aw // (GROUP_W * gs)
    geo = dict(n_lat=n_lat, seq=seq, ctx_len=ctx_len)

    def spec(reverse):
        return pl.BlockSpec(
            (CHUNK, GROUP_W * gs),
            lambda pg, i: (_chunk_row_block(pg // ng, i, reverse=reverse, **geo), pg % ng))

    sf, sb = spec(False), spec(True)
    sds = jax.ShapeDtypeStruct((rows, aw), F32)
    return pl.pallas_call(
        _rwkv_scan_kernel,
        grid=(batch * ng, (seq + ctx_len) // CHUNK),
        in_specs=[sf] * 6 + [sb] * 6,
        out_specs=[sf, sb],
        out_shape=[sds, sds],
        scratch_shapes=[pltpu.VMEM((gs, GROUP_W, GROUP_W), F32),
                        pltpu.VMEM((gs, GROUP_W, GROUP_W), F32)],
        compiler_params=_cparams(("parallel", "arbitrary")),
    )(r, v, kk, lw0, kd0, b0, r, v, kk, lw1, kd1, b1)


def _gla_scan_kernel(qkf, vf, gdf, qkb, vb, gdb, gw2_ref, gb_ref, of_ref, ob_ref, sf_ref, sb_ref,
                     *, dk, dv):
    @pl.when(pl.program_id(1) == 0)
    def _():
        sf_ref[...] = jnp.zeros_like(sf_ref)
        sb_ref[...] = jnp.zeros_like(sb_ref)

    L = CHUNK
    wk = B_HEADS * dk
    t_r = lax.broadcasted_iota(jnp.int32, (L, L), 0)
    t_c = lax.broadcasted_iota(jnp.int32, (L, L), 1)

    qh, kh, keh, vh, dech, keeph, sth = [], [], [], [], [], [], []
    for d, (qk_ref, v_ref, gd_ref, st_ref) in enumerate(((qkf, vf, gdf, sf_ref),
                                                         (qkb, vb, gdb, sb_ref))):
        reverse = d == 1
        keep = (t_c >= t_r) if reverse else (t_c <= t_r)
        tri = _bf(jnp.where(keep, 1.0, 0.0))
        z = _dot(_bf(gd_ref[...]), gw2_ref[d]) + gb_ref[d:d + 1, :]
        la = -_softplus(-z) * (1.0 / B_GATE_TAU)
        cum = _dot_exact_lhs(tri, la)
        last = 0 if reverse else L - 1
        cl = cum[last:last + 1, :]
        q_in = _bf(qk_ref[:, :wk] * (dk ** -0.5) * jnp.exp(cum))
        k_in = _bf(qk_ref[:, wk:] * jnp.exp(-cum))
        k_end = _bf(qk_ref[:, wk:] * jnp.exp(cl - cum))
        dec = jnp.exp(cl)
        for h in range(B_HEADS):
            ks = slice(h * dk, (h + 1) * dk)
            qh.append(q_in[:, ks])
            kh.append(k_in[:, ks])
            keh.append(k_end[:, ks])
            vh.append(_bf(v_ref[:, h * dv:(h + 1) * dv]))
            dech.append(dec[:, ks])
            keeph.append(keep)
            sth.append(st_ref[h])

    sc = [_bf(jnp.where(kp, _dot_nt(q, k), 0.0)) for kp, q, k in zip(keeph, qh, kh)]
    inter = [_dot_nt(q, _bf(st)) for q, st in zip(qh, sth)]
    kvt = [_dot_tn(v, ke) for v, ke in zip(vh, keh)]
    out = [_dot(s_, v) + it for s_, v, it in zip(sc, vh, inter)]
    for d, (o_ref, st_ref) in enumerate(((of_ref, sf_ref), (ob_ref, sb_ref))):
        for h in range(B_HEADS):
            i = d * B_HEADS + h
            o_ref[:, h * dv:(h + 1) * dv] = out[i]
            st_ref[h] = sth[i] * dech[i] + kvt[i]


def _gla_scan(pb, gw2p, gb, *, batch, n_lat, seq, ctx_len, wk, wv, col0):
    rows = pb.shape[0]
    dk, dv = wk // B_HEADS, wv // B_HEADS
    geo = dict(n_lat=n_lat, seq=seq, ctx_len=ctx_len)
    gd_blk = (col0 + 2 * wk + 2 * wv) // LANE

    def specs(reverse):
        rb = lambda b, i: _chunk_row_block(b, i, reverse=reverse, **geo)
        return [
            pl.BlockSpec((CHUNK, 2 * wk), lambda b, i: (rb(b, i), col0 // (2 * wk))),
            pl.BlockSpec((CHUNK, wv), lambda b, i: (rb(b, i), (col0 + 2 * wk) // wv)),
            pl.BlockSpec((CHUNK, LANE), lambda b, i: (rb(b, i), gd_blk)),
        ]

    def out_spec(reverse):
        return pl.BlockSpec((CHUNK, wv),
                            lambda b, i: (_chunk_row_block(b, i, reverse=reverse, **geo), 0))

    sds = jax.ShapeDtypeStruct((rows, wv), F32)
    return pl.pallas_call(
        functools.partial(_gla_scan_kernel, dk=dk, dv=dv),
        grid=(batch, (seq + ctx_len) // CHUNK),
        in_specs=specs(False) + specs(True) + [
            pl.BlockSpec((2, LANE, wk), lambda b, i: (0, 0, 0)),
            pl.BlockSpec((2, wk), lambda b, i: (0, 0)),
        ],
        out_specs=[out_spec(False), out_spec(True)],
        out_shape=[sds, sds],
        scratch_shapes=[pltpu.VMEM((B_HEADS, dv, dk), F32), pltpu.VMEM((B_HEADS, dv, dk), F32)],
        compiler_params=_cparams(("parallel", "arbitrary")),
    )(pb, pb, pb, pb, pb, pb, gw2p, gb)


def _mixer_out_kernel(yf, yb, bonus, g, of, ob, gate, x_ref, mod_ref, lnw, lnb, gn, e_ref, w_ref,
                      o_ref, h_ref, *, aw, dv):
    e_bf = e_ref[...]
    y = yf[...] + yb[...]
    inv = 1.0 / A_HEAD_DIM
    mean = _head_sum(y, e_bf) * inv
    dlt = y - mean
    var = _head_sum(dlt * dlt, e_bf) * inv
    yn = dlt * lax.rsqrt(var + A_LN_EPS) * lnw[...] + lnb[...]
    h_ref[:, :aw] = _bf((yn + bonus[...]) * g[...])

    o = of[...] + ob[...]
    gt = _silu(gate[...])
    for h in range(B_HEADS):
        sl = slice(h * dv, (h + 1) * dv)
        oh = o[:, sl]
        ms = jnp.mean(oh * oh, axis=-1, keepdims=True)
        h_ref[:, aw + h * dv:aw + (h + 1) * dv] = _bf(oh * lax.rsqrt(ms + NORM_EPS) * gn[...] * gt[:, sl])

    o_ref[...] = x_ref[...] + mod_ref[2:3, :] * _dot(h_ref[...], w_ref[...])


def _mixer_out(yf, yb, bonus, g, of, ob, pb, x, mods_l, ln_w, ln_b, gla_norm, e_bf, w_bf,
               seg_of_tile, *, wk, col0):
    rows, d = x.shape
    aw = yf.shape[1]
    wv = of.shape[1]
    dv = wv // B_HEADS
    tm = 256
    tiles_per_row_tile = ROW_TILE // tm
    rs = lambda w: pl.BlockSpec((tm, w), lambda i: (i, 0))
    full = lambda shape: pl.BlockSpec(shape, lambda i: (0,) * len(shape))
    return pl.pallas_call(
        functools.partial(_mixer_out_kernel, aw=aw, dv=dv),
        grid=(rows // tm,),
        in_specs=[
            rs(aw), rs(aw), rs(aw), rs(aw), rs(wv), rs(wv),
            pl.BlockSpec((tm, wv), lambda i: (i, (col0 + 2 * wk + wv) // wv)),
            rs(d),
            pl.BlockSpec((None, N_MOD, d), lambda i: (seg_of_tile(i // tiles_per_row_tile), 0, 0)),
            full((1, aw)), full((1, aw)), full((1, dv)), full((GROUP_W, GROUP_W)),
            full((aw + wv, d)),
        ],
        out_specs=rs(d),
        out_shape=jax.ShapeDtypeStruct((rows, d), F32),
        scratch_shapes=[pltpu.VMEM((tm, aw + wv), BF16)],
        compiler_params=_cparams(("parallel",)),
    )(yf, yb, bonus, g, of, ob, pb, x, mods_l, ln_w.reshape(1, aw), ln_b.reshape(1, aw),
      gla_norm.reshape(1, dv), e_bf, w_bf)


def _lru_kernel(*refs, reverse, n_blk, combine):
    if combine:
        (x_ref, xp_ref, xn_ref, h0_ref, cw_ref, cb_ref, wc_ref, ba_ref, bx_ref, lam_ref,
         hf_ref, gate_ref, out_ref, hl_ref, a_sc, u_sc, hs_sc, carry_sc) = refs
    else:
        (x_ref, xp_ref, xn_ref, h0_ref, cw_ref, cb_ref, wc_ref, ba_ref, bx_ref, lam_ref,
         out_ref, hl_ref, a_sc, u_sc, hs_sc, carry_sc) = refs
    j = pl.program_id(2)
    blk = (n_blk - 1 - j) if reverse else j
    lb, ct = x_ref.shape
    cblk = wc_ref.shape[1]
    is_first = blk == 0
    is_last = blk == n_blk - 1
    rowid = lax.broadcasted_iota(jnp.int32, (lb, 1), 0)

    x = x_ref[...]
    p6 = jnp.where(is_first, 0.0, xp_ref[SUBLANE - 2:SUBLANE - 1, :])
    p7 = jnp.where(is_first, 0.0, xp_ref[SUBLANE - 1:SUBLANE, :])
    n0 = jnp.where(is_last, 0.0, xn_ref[0:1, :])
    xm1 = jnp.where(rowid == 0, p7, pltpu.roll(x, 1, 0))
    xm2 = jnp.where(rowid == 0, p6, jnp.where(rowid == 1, p7, pltpu.roll(x, 2, 0)))
    xp1 = jnp.where(rowid == lb - 1, n0, pltpu.roll(x, lb - 1, 0))
    xs = (cb_ref[...] + cw_ref[0:1, :] * xm2 + cw_ref[1:2, :] * xm1 + cw_ref[2:3, :] * x
          + cw_ref[3:4, :] * xp1)

    for n in range(ct // cblk):
        sl = slice(n * cblk, (n + 1) * cblk)
        xb = xs[:, sl]
        ri = _dot(_bf(xb), wc_ref[n])
        rg = _sigmoid(ri[:, :cblk] + ba_ref[:, sl])
        ig = _sigmoid(ri[:, cblk:] + bx_ref[:, sl])
        log_a = -C_CONST * rg * _softplus(-lam_ref[:, sl])
        a = jnp.exp(log_a)
        a_sc[:, sl] = a
        u_sc[:, sl] = jnp.sqrt(1.0 - a * a) * (ig * xb)

    @pl.when(j == 0)
    def _():
        carry_sc[...] = h0_ref[0:1, :]

    def step(t, h):
        tt = (lb - 1 - t) if reverse else t
        h = a_sc[pl.ds(tt, 1), :] * h + u_sc[pl.ds(tt, 1), :]
        hs_sc[pl.ds(tt, 1), :] = h
        return h

    h = lax.fori_loop(0, lb, step, carry_sc[...], unroll=8)
    carry_sc[...] = h
    hl_ref[...] = jnp.broadcast_to(h, hl_ref.shape)
    if combine:
        out_ref[...] = (hs_sc[...] + hf_ref[...]) * jax.nn.gelu(gate_ref[...])
    else:
        out_ref[...] = hs_sc[...]


def _lru_pass(x_arr, x_specs, h0, conv_w, conv_b, wcat, ba, bx, lam, out_shape, out_spec, extra,
              *, batch, n_blk, lb, reverse):
    c = conv_w.shape[1]
    ct = 1024
    nct = c // ct
    cblk = wcat.shape[1]
    chan = lambda rows_: pl.BlockSpec((rows_, ct), lambda b, k, j: (0, k))
    in_specs = list(x_specs) + [
        pl.BlockSpec((SUBLANE, ct), lambda b, k, j: (b, k)),
        chan(C_CONV), chan(1),
        pl.BlockSpec((ct // cblk, cblk, 2 * cblk), lambda b, k, j: (k, 0, 0)),
        chan(1), chan(1), chan(1),
    ] + [s for _, s in extra]
    args = [x_arr, x_arr, x_arr, h0, conv_w, conv_b.reshape(1, c), wcat, ba.reshape(1, c),
            bx.reshape(1, c), lam.reshape(1, c)] + [a for a, _ in extra]
    return pl.pallas_call(
        functools.partial(_lru_kernel, reverse=reverse, n_blk=n_blk, combine=bool(extra)),
        grid=(batch, nct, n_blk),
        in_specs=in_specs,
        out_specs=[out_spec, pl.BlockSpec((SUBLANE, ct), lambda b, k, j: (b, k))],
        out_shape=[out_shape, jax.ShapeDtypeStruct((batch * SUBLANE, c), F32)],
        scratch_shapes=[pltpu.VMEM((lb, ct), F32), pltpu.VMEM((lb, ct), F32),
                        pltpu.VMEM((lb, ct), F32), pltpu.VMEM((1, ct), F32)],
        compiler_params=_cparams(("parallel", "parallel", "arbitrary")),
    )(*args)


def _rglru(hw_ctx, gate_grid, x_grid, conv_w, conv_b, wa, ba, wx, bx, lam, *, batch, n_lat, seq,
           ctx_len):
    c = conv_w.shape[1]
    ct = 1024
    rows = seq // GRID_W
    lbc = 128
    ncb = ctx_len // lbc
    xcol = c // ct
    per_col = c // ct

    wcat = [_bf(jnp.concatenate([wa[d], wx[d]], axis=-1)) for d in range(2)]
    zeros_state = jnp.zeros((batch * SUBLANE, c), F32)

    def ctx_specs(reverse):
        blk = (lambda j: ncb - 1 - j) if reverse else (lambda j: j)
        per8 = lbc // SUBLANE
        cur = pl.BlockSpec((lbc, ct), lambda b, k, j: (b * ncb + blk(j), xcol + k))
        prv = pl.BlockSpec((SUBLANE, ct), lambda b, k, j: (
            b * ncb * per8 + jnp.maximum(blk(j) * per8 - 1, 0), xcol + k))
        nxt = pl.BlockSpec((SUBLANE, ct), lambda b, k, j: (
            b * ncb * per8 + jnp.minimum((blk(j) + 1) * per8, ncb * per8 - 1), xcol + k))
        return cur, prv, nxt

    def lat_specs(reverse):
        blk = (lambda j: GRID_W - 1 - j) if reverse else (lambda j: j)
        r8 = rows // SUBLANE
        cur = pl.BlockSpec((rows, ct), lambda b, k, j: (b, blk(j) * per_col + k))
        prv = pl.BlockSpec((SUBLANE, ct), lambda b, k, j: (
            b * r8 + r8 - 1, jnp.maximum(blk(j) - 1, 0) * per_col + k))
        nxt = pl.BlockSpec((SUBLANE, ct), lambda b, k, j: (
            b * r8, jnp.minimum(blk(j) + 1, GRID_W - 1) * per_col + k))
        return cur, prv, nxt

    common = dict(batch=batch)
    ctx_scratch_sds = jax.ShapeDtypeStruct((batch * ctx_len, c), F32)
    ctx_out = lambda reverse: pl.BlockSpec(
        (lbc, ct), lambda b, k, j: (b * ncb + ((ncb - 1 - j) if reverse else j), k))
    _, st = _lru_pass(hw_ctx, ctx_specs(False), zeros_state, conv_w, conv_b, wcat[0], ba[0], bx[0],
                      lam[0], ctx_scratch_sds, ctx_out(False), [], n_blk=ncb, lb=lbc,
                      reverse=False, **common)
    hf_sds = jax.ShapeDtypeStruct((n_lat, c), F32)
    hf_spec = lambda reverse: pl.BlockSpec(
        (rows, ct), lambda b, k, j: (b * GRID_W + ((GRID_W - 1 - j) if reverse else j), k))
    hf, _ = _lru_pass(x_grid, lat_specs(False), st, conv_w, conv_b, wcat[0], ba[0], bx[0], lam[0],
                      hf_sds, hf_spec(False), [], n_blk=GRID_W, lb=rows, reverse=False, **common)
    _, st = _lru_pass(hw_ctx, ctx_specs(True), zeros_state, conv_w, conv_b, wcat[1], ba[1], bx[1],
                      lam[1], ctx_scratch_sds, ctx_out(True), [], n_blk=ncb, lb=lbc,
                      reverse=True, **common)
    grid_spec = pl.BlockSpec((rows, ct), lambda b, k, j: (b, (GRID_W - 1 - j) * per_col + k))
    y_sds = jax.ShapeDtypeStruct((n_lat // GRID_W, GRID_W * c), F32)
    y, _ = _lru_pass(x_grid, lat_specs(True), st, conv_w, conv_b, wcat[1], ba[1], bx[1], lam[1],
                     y_sds, grid_spec, [(hf, hf_spec(True)), (gate_grid, grid_spec)],
                     n_blk=GRID_W, lb=rows, reverse=True, **common)
    return y


def _pad_cols(w, n):
    return jnp.pad(w, ((0, 0),) * (w.ndim - 1) + ((0, n - w.shape[-1]),))


def kernel(x, c, ctx, c_ctx, mod_w, mod_b, norm1, norm2, mlp_w1, mlp_w2, ab_w_in, ab_w_out, rw_mu,
           rw_w0, rw_w2, rw_a0, rw_a2, rw_g2, rw_kk, rw_ka, rw_rk, rw_ln_w, rw_ln_b, gla_gw2,
           gla_gb, gla_norm, lru_w_in, lru_w_out, lru_conv_w, lru_conv_b, lru_wa, lru_ba, lru_wx,
           lru_bx, lru_lam, final_norm):
    batch, seq, d = x.shape
    ctx_len = ctx.shape[1]
    depth = mod_w.shape[0]
    n_lat = batch * seq
    aw = rw_w0.shape[-1]
    wk = gla_gb.shape[-1]
    wv = ab_w_out.shape[1] - aw
    a_cols = rw_mu.shape[-1]
    b_cols = ab_w_in.shape[-1] - a_cols
    assert depth == 2 and batch + 1 <= SUBLANE
    assert seq % ROW_TILE == 0 and (batch * ctx_len) % ROW_TILE == 0
    assert ctx_len % PREP_TILE == 0 and ctx_len % 128 == 0 and seq % (GRID_W * SUBLANE) == 0
    assert aw % GROUP_W == 0 and wv == aw and 2 * wk == wv

    tiles_per_seq = seq // ROW_TILE
    seg_of_tile = lambda i: jnp.minimum(i // tiles_per_seq, batch)

    xs = jnp.concatenate([x.reshape(n_lat, d), ctx.reshape(batch * ctx_len, d)], axis=0)
    cvec = jnp.concatenate(
        [c, c_ctx[None, :], jnp.zeros((SUBLANE - batch - 1, d), F32)], axis=0)
    mods = _mods(cvec, mod_w, mod_b)

    hid = jnp.arange(GROUP_W) // A_HEAD_DIM
    e_bf = _bf(hid[:, None] == hid[None, :])

    a_pad = -(-a_cols // LANE) * LANE
    col0 = -(-a_cols // (2 * wk)) * (2 * wk)
    n_tiles = 3
    tn = -(-(col0 + b_cols) // (n_tiles * LANE)) * LANE
    w_ab = _bf(jnp.concatenate(
        [_pad_cols(ab_w_in[0][:, :a_cols], col0),
         _pad_cols(ab_w_in[0][:, a_cols:], n_tiles * tn - col0)], axis=1))
    pa = pb = _norm_matmul(xs, norm1[0], mods[0], w_ab, seg_of_tile, tn=tn)

    lora_w = 2 * A_LORA_W + 2 * A_LORA_A
    w2p = jnp.zeros((2, lora_w, aw), F32)
    a2p = jnp.zeros((2, lora_w, aw), F32)
    for dd in range(2):
        w2p = w2p.at[dd, dd * A_LORA_W:(dd + 1) * A_LORA_W].set(rw_w2[0, dd])
        a2p = a2p.at[dd, 2 * A_LORA_W + dd * A_LORA_A:2 * A_LORA_W + (dd + 1) * A_LORA_A].set(
            rw_a2[0, dd])
    (r_, v_, kk_, lw0, lw1, kd0, kd1, b0, b1, g_, bonus) = _rwkv_prep(
        pa, _pad_cols(rw_mu[0], a_pad), rw_w0[0], rw_a0[0], _bf(w2p), _bf(a2p), _bf(rw_g2[0]),
        rw_kk[0], rw_ka[0], rw_rk[0].reshape(-1), e_bf, n_lat=n_lat, seq=seq, ctx_len=ctx_len)
    yf, yb = _rwkv_scan(r_, v_, kk_, lw0, lw1, kd0, kd1, b0, b1, batch=batch, n_lat=n_lat, seq=seq,
                        ctx_len=ctx_len)

    gw2p = jnp.zeros((2, LANE, wk), F32)
    for dd in range(2):
        gw2p = gw2p.at[dd, dd * B_LORA:(dd + 1) * B_LORA].set(gla_gw2[0, dd])
    of, ob = _gla_scan(pb, _bf(gw2p), gla_gb[0], batch=batch, n_lat=n_lat, seq=seq,
                       ctx_len=ctx_len, wk=wk, wv=wv, col0=col0)

    xs = _mixer_out(yf, yb, bonus, g_, of, ob, pb, xs, mods[0], rw_ln_w[0], rw_ln_b[0],
                    gla_norm[0], e_bf, _bf(ab_w_out[0]), seg_of_tile, wk=wk, col0=col0)
    seg_of_mlp_tile = lambda i: jnp.minimum(i // (seq // MLP_TILE), batch)
    ctx_tiles = (batch * ctx_len) // ROW_TILE
    w1_bf, w2_bf = _bf(mlp_w1[0]), _bf(mlp_w2[0])
    x_lat = _mlp(xs, norm2[0], mods[0], w1_bf, w2_bf, final_norm, seg_of_mlp_tile, tm=MLP_TILE,
                 tile0=0, n_tiles=n_lat // MLP_TILE, final_norm=False)
    x_ctx = _mlp(xs, norm2[0], mods[0], w1_bf, w2_bf, final_norm, seg_of_tile, tm=ROW_TILE,
                 tile0=n_lat // ROW_TILE, n_tiles=ctx_tiles, final_norm=False)

    w_lru = _bf(lru_w_in[0])
    hw_ctx = _norm_matmul(x_ctx, norm1[1], mods[1], w_lru, lambda i: batch, tn=2048)
    gate_grid, x_grid = _norm_matmul_grid(x_lat, n_lat, seq // GRID_W, norm1[1], mods[1], w_lru)
    y = _rglru(hw_ctx, gate_grid, x_grid, lru_conv_w[0], lru_conv_b[0], lru_wa[0], lru_ba[0],
               lru_wx[0], lru_bx[0], lru_lam[0], batch=batch, n_lat=n_lat, seq=seq,
               ctx_len=ctx_len)
    xl = _proj_res(y, x_lat, n_lat, seq // GRID_W, mods[1], _bf(lru_w_out[0]))
    out = _mlp(xl, norm2[1], mods[1], _bf(mlp_w1[1]), _bf(mlp_w2[1]), final_norm, seg_of_mlp_tile,
               tm=MLP_TILE, tile0=0, n_tiles=n_lat // MLP_TILE, final_norm=True)
    return out.reshape(batch, seq, d)
```

```python
import functools

import jax
import jax.numpy as jnp
from jax import lax
from jax.experimental import pallas as pl
from jax.experimental.pallas import tpu as pltpu

F32 = jnp.float32
BF16 = jnp.bfloat16

NORM_EPS = 1e-6
GRID_W = 64
N_MOD = 6

A_HEAD_DIM = 64
A_LORA_W = 96
A_LORA_A = 96
A_LORA_G = 256
A_LN_EPS = 64e-5
HEADS_PER_GROUP = 4
GROUP_W = HEADS_PER_GROUP * A_HEAD_DIM
CHUNK = 64
RWKV_GROUPS_PER_STEP = 4

B_HEADS = 4
B_LORA = 16
B_GATE_TAU = 16.0

C_BLOCKS = 8
C_CONV = 4
C_CONST = 8.0

LANE = 128
SUBLANE = 8
ROW_TILE = 512
MLP_TILE = 512
MLP_STEP_ELEMS = 1024 * 512
PREP_TILE = 128
VMEM_LIMIT = 56 * 1024 * 1024


def _cparams(sem):
    return pltpu.CompilerParams(dimension_semantics=sem, vmem_limit_bytes=VMEM_LIMIT)


def _bf(x):
    return x.astype(BF16)


def _dot(a, b):
    return jnp.dot(a, b, preferred_element_type=F32)


def _dot_nt(a, b):
    return lax.dot_general(a, b, (((1,), (1,)), ((), ())), preferred_element_type=F32)


def _dot_tn(a, b):
    return lax.dot_general(a, b, (((0,), (0,)), ((), ())), preferred_element_type=F32)


def _split3(x):
    hi = _bf(x)
    r1 = x - hi.astype(F32)
    mid = _bf(r1)
    lo = _bf(r1 - mid.astype(F32))
    return hi, mid, lo


def _dot_exact_lhs(m_bf, x):
    hi, mid, lo = _split3(x)
    return _dot(m_bf, hi) + _dot(m_bf, mid) + _dot(m_bf, lo)


def _dot_exact_rhs(x, m_bf):
    hi, mid, lo = _split3(x)
    return _dot(hi, m_bf) + _dot(mid, m_bf) + _dot(lo, m_bf)


def _softplus(x):
    return jnp.maximum(x, 0.0) + jnp.log(1.0 + jnp.exp(-jnp.abs(x)))


def _sigmoid(x):
    return 0.5 * jnp.tanh(0.5 * x) + 0.5


def _silu(x):
    return x * _sigmoid(x)


def _norm_mod(x, g, shift, scale):
    ms = jnp.mean(x * x, axis=-1, keepdims=True)
    return (x * lax.rsqrt(ms + NORM_EPS) * g) * (1.0 + scale) + shift


def _mods_kernel(c_ref, w_ref, b_ref, o_ref):
    s = _silu(c_ref[...])
    o_ref[...] = _dot(_bf(s), _bf(w_ref[...])) + b_ref[...]


def _mods(cvec, mod_w, mod_b):
    depth, d, n = mod_w.shape
    tn = 1024
    out = pl.pallas_call(
        _mods_kernel,
        grid=(depth, n // tn),
        in_specs=[
            pl.BlockSpec((SUBLANE, d), lambda l, j: (0, 0)),
            pl.BlockSpec((None, d, tn), lambda l, j: (l, 0, j)),
            pl.BlockSpec((None, 1, tn), lambda l, j: (l, 0, j)),
        ],
        out_specs=pl.BlockSpec((None, SUBLANE, tn), lambda l, j: (l, 0, j)),
        out_shape=jax.ShapeDtypeStruct((depth, SUBLANE, n), F32),
        compiler_params=_cparams(("parallel", "parallel")),
    )(cvec, mod_w, mod_b.reshape(depth, 1, n))
    return out.reshape(depth, SUBLANE, N_MOD, d)


def _norm_matmul_kernel(x_ref, g_ref, mod_ref, w_ref, o_ref, h_ref, *, shift_row, scale_row):
    @pl.when(pl.program_id(1) == 0)
    def _():
        h = _norm_mod(x_ref[...], g_ref[...], mod_ref[shift_row:shift_row + 1, :],
                      mod_ref[scale_row:scale_row + 1, :])
        h_ref[...] = _bf(h)

    o_ref[...] = _dot(h_ref[...], w_ref[...])


def _norm_matmul(x, g, mods_l, w_bf, seg_of_tile, *, tn, tile0=0, n_tiles=None, shift_row=0,
                 scale_row=1):
    r, d = x.shape
    n = w_bf.shape[1]
    tm = ROW_TILE
    n_tiles = r // tm if n_tiles is None else n_tiles
    return pl.pallas_call(
        functools.partial(_norm_matmul_kernel, shift_row=shift_row, scale_row=scale_row),
        grid=(n_tiles, n // tn),
        in_specs=[
            pl.BlockSpec((tm, d), lambda i, j: (i + tile0, 0)),
            pl.BlockSpec((1, d), lambda i, j: (0, 0)),
            pl.BlockSpec((None, N_MOD, d), lambda i, j: (seg_of_tile(i + tile0), 0, 0)),
            pl.BlockSpec((d, tn), lambda i, j: (0, j)),
        ],
        out_specs=pl.BlockSpec((tm, tn), lambda i, j: (i, j)),
        out_shape=jax.ShapeDtypeStruct((n_tiles * tm, n), F32),
        scratch_shapes=[pltpu.VMEM((tm, d), BF16)],
        compiler_params=_cparams(("parallel", "arbitrary")),
    )(x, g.reshape(1, d), mods_l, w_bf)


def _grid_tile(rows):
    gr = min(rows, 64)
    gc = ROW_TILE // gr
    assert gc % SUBLANE == 0 and GRID_W % gc == 0 and rows % gr == 0
    return gr, gc


def _norm_matmul_grid_kernel(x_ref, g_ref, mod_ref, w_ref, o0_ref, o1_ref, h_ref, res_ref):
    j = pl.program_id(1)
    gr, gc, d = x_ref.shape

    @pl.when(j == 0)
    def _():
        h = _norm_mod(x_ref[...].reshape(gr * gc, d), g_ref[...], mod_ref[0:1, :], mod_ref[1:2, :])
        h_ref[...] = _bf(h)

    res = _dot(h_ref[...], w_ref[...])
    nh = w_ref.shape[1]
    nq = nh // LANE
    for q in range(nq):
        res_ref[q] = res[:, q * LANE:(q + 1) * LANE]

    def scatter(o_ref):
        for c in range(gc):
            for q in range(nq):
                o_ref[:, c * nh + q * LANE:c * nh + (q + 1) * LANE] = (
                    res_ref[q, pl.ds(c, gr, stride=gc), :])

    @pl.when(j == 0)
    def _():
        scatter(o0_ref)

    @pl.when(j == 1)
    def _():
        scatter(o1_ref)


def _norm_matmul_grid(x, n_lat, rows, g, mods_l, w_bf):
    d = x.shape[1]
    nh = w_bf.shape[1] // 2
    gr, gc = _grid_tile(rows)
    ncg = GRID_W // gc
    x3 = x.reshape(x.shape[0] // GRID_W, GRID_W, d)
    out_spec = pl.BlockSpec((gr, gc * nh), lambda i, j: (i // ncg, i % ncg))
    sds = jax.ShapeDtypeStruct((n_lat // GRID_W, GRID_W * nh), F32)
    return pl.pallas_call(
        _norm_matmul_grid_kernel,
        grid=(n_lat // ROW_TILE, 2),
        in_specs=[
            pl.BlockSpec((gr, gc, d), lambda i, j: (i // ncg, i % ncg, 0)),
            pl.BlockSpec((1, d), lambda i, j: (0, 0)),
            pl.BlockSpec((None, N_MOD, d), lambda i, j: ((i // ncg) * gr // rows, 0, 0)),
            pl.BlockSpec((d, nh), lambda i, j: (0, j)),
        ],
        out_specs=[out_spec, out_spec],
        out_shape=[sds, sds],
        scratch_shapes=[pltpu.VMEM((ROW_TILE, d), BF16),
                        pltpu.VMEM((nh // LANE, ROW_TILE, LANE), F32)],
        compiler_params=_cparams(("parallel", "arbitrary")),
    )(x3, g.reshape(1, d), mods_l, w_bf)


def _mlp_kernel(x_ref, g_ref, mod_ref, w1_ref, w2_ref, fn_ref, o_ref, h_ref, *, final_norm):
    j = pl.program_id(1)
    slab = 256
    n_slab = x_ref.shape[0] // slab

    @pl.when(j == 0)
    def _():
        for s in range(n_slab):
            rs = slice(s * slab, (s + 1) * slab)
            h_ref[rs, :] = _bf(_norm_mod(x_ref[rs, :], g_ref[...], mod_ref[3:4, :],
                                         mod_ref[4:5, :]))
        o_ref[...] = jnp.zeros_like(o_ref)

    a = _dot(h_ref[...], w1_ref[...])
    a = _bf(jnp.square(jnp.maximum(a, 0.0)))
    nsplit = 2
    wn = o_ref.shape[1] // nsplit
    for s in range(nsplit):
        o_ref[:, s * wn:(s + 1) * wn] += _dot(a, w2_ref[:, s * wn:(s + 1) * wn])

    @pl.when(j == pl.num_programs(1) - 1)
    def _():
        for s in range(n_slab):
            rs = slice(s * slab, (s + 1) * slab)
            xo = x_ref[rs, :] + mod_ref[5:6, :] * o_ref[rs, :]
            if final_norm:
                ms = jnp.mean(xo * xo, axis=-1, keepdims=True)
                xo = xo * lax.rsqrt(ms + NORM_EPS) * fn_ref[...]
            o_ref[rs, :] = xo


def _mlp(x, g, mods_l, w1_bf, w2_bf, fn, seg_of_tile, *, tm, tile0, n_tiles, final_norm):
    d = x.shape[1]
    dff = w1_bf.shape[1]
    tf = MLP_STEP_ELEMS // tm
    return pl.pallas_call(
        functools.partial(_mlp_kernel, final_norm=final_norm),
        grid=(n_tiles, dff // tf),
        in_specs=[
            pl.BlockSpec((tm, d), lambda i, j: (i + tile0, 0), pipeline_mode=pl.Buffered(1)),
            pl.BlockSpec((1, d), lambda i, j: (0, 0)),
            pl.BlockSpec((None, N_MOD, d), lambda i, j: (seg_of_tile(i + tile0), 0, 0)),
            pl.BlockSpec((d, tf), lambda i, j: (0, j)),
            pl.BlockSpec((tf, d), lambda i, j: (j, 0)),
            pl.BlockSpec((1, d), lambda i, j: (0, 0)),
        ],
        out_specs=pl.BlockSpec((tm, d), lambda i, j: (i, 0)),
        out_shape=jax.ShapeDtypeStruct((n_tiles * tm, d), F32),
        scratch_shapes=[pltpu.VMEM((tm, d), BF16)],
        compiler_params=_cparams(("parallel", "arbitrary")),
    )(x, g.reshape(1, d), mods_l, w1_bf, w2_bf, fn.reshape(1, d))


def _proj_res_kernel(y_ref, x_ref, mod_ref, w_ref, o_ref, yt_ref):
    k = w_ref.shape[0]
    gr, gc, d = x_ref.shape
    nq = k // LANE
    for c in range(gc):
        for q in range(nq):
            yt_ref[q, pl.ds(c, gr, stride=gc), :] = (
                y_ref[:, c * k + q * LANE:c * k + (q + 1) * LANE])
    yt = jnp.concatenate([_bf(yt_ref[q]) for q in range(nq)], axis=1)
    out = x_ref[...].reshape(gr * gc, d) + mod_ref[2:3, :] * _dot(yt, w_ref[...])
    o_ref[...] = out.reshape(gr, gc, d)


def _proj_res(y_grid, x, n_lat, rows, mods_l, w_bf):
    k, d = w_bf.shape
    gr, gc = _grid_tile(rows)
    ncg = GRID_W // gc
    x3 = x.reshape(x.shape[0] // GRID_W, GRID_W, d)
    tok_spec = pl.BlockSpec((gr, gc, d), lambda i: (i // ncg, i % ncg, 0))
    out = pl.pallas_call(
        _proj_res_kernel,
        grid=(n_lat // ROW_TILE,),
        in_specs=[
            pl.BlockSpec((gr, gc * k), lambda i: (i // ncg, i % ncg)),
            tok_spec,
            pl.BlockSpec((None, N_MOD, d), lambda i: ((i // ncg) * gr // rows, 0, 0)),
            pl.BlockSpec((k, d), lambda i: (0, 0)),
        ],
        out_specs=tok_spec,
        out_shape=jax.ShapeDtypeStruct((n_lat // GRID_W, GRID_W, d), F32),
        scratch_shapes=[pltpu.VMEM((k // LANE, ROW_TILE, LANE), F32)],
        compiler_params=_cparams(("parallel",)),
    )(y_grid, x3, mods_l, w_bf)
    return out.reshape(n_lat, d)


def _head_sum(z, e_bf):
    parts = []
    for gi in range(z.shape[1] // GROUP_W):
        zs = z[:, gi * GROUP_W:(gi + 1) * GROUP_W]
        parts.append(_dot_exact_rhs(zs, e_bf))
    return jnp.concatenate(parts, axis=1)


def _rwkv_prep_kernel(p_ref, pp_ref, pn_ref, mu_ref, w0_ref, a0_ref, w2_ref, a2_ref, g2_ref,
                      kk_ref, ka_ref, rk_ref, e_ref,
                      r_o, v_o, kk_o, lw0_o, lw1_o, kd0_o, kd1_o, b0_o, b1_o, g_o, bonus_o,
                      *, n_lat_tiles, lat_tiles_per_seq, ctx_tiles_per_seq, aw):
    i = pl.program_id(0)
    tp = p_ref.shape[0]
    in_ctx = i >= n_lat_tiles
    pos = jnp.where(in_ctx, (i - n_lat_tiles) % ctx_tiles_per_seq, i % lat_tiles_per_seq)
    last = jnp.where(in_ctx, ctx_tiles_per_seq - 1, lat_tiles_per_seq - 1)
    is_first = pos == 0
    is_last = pos == last
    rowid = lax.broadcasted_iota(jnp.int32, (tp, 1), 0)

    def shifted(c0, c1):
        x = p_ref[:, c0:c1]
        prow = jnp.where(is_first, 0.0, pp_ref[SUBLANE - 1:SUBLANE, c0:c1])
        nrow = jnp.where(is_last, 0.0, pn_ref[0:1, c0:c1])
        xp = jnp.where(rowid == 0, prow, pltpu.roll(x, 1, 0))
        xn = jnp.where(rowid == tp - 1, nrow, pltpu.roll(x, tp - 1, 0))
        return x + mu_ref[0:1, c0:c1] * (xp - x) + mu_ref[1:2, c0:c1] * (xn - x)

    e_bf = e_ref[...]
    r = shifted(0, aw)
    k = shifted(aw, 2 * aw)
    v = shifted(2 * aw, 3 * aw)
    gd = shifted(3 * aw, 3 * aw + A_LORA_G)
    lo0 = 3 * aw + A_LORA_G
    xl = shifted(lo0, lo0 + 2 * A_LORA_W + 2 * A_LORA_A)

    r_o[...] = r
    v_o[...] = v
    g_o[...] = _dot(_bf(_sigmoid(gd)), g2_ref[...])

    kx = k * kk_ref[...]
    kn = jnp.sqrt(_head_sum(kx * kx, e_bf))
    kk = kx / jnp.maximum(kn, 1e-12)
    kk_o[...] = kk

    xl_t = _bf(jnp.tanh(xl))
    xl_b = _bf(xl)
    kd_sum = None
    for d, (lw_o, kd_o, b_o) in enumerate(((lw0_o, kd0_o, b0_o), (lw1_o, kd1_o, b1_o))):
        w_log = -_softplus(-(w0_ref[d:d + 1, :] + _dot(xl_t, w2_ref[d]))) - 0.5
        lw_o[...] = -jnp.exp(w_log)
        asig = _sigmoid(a0_ref[d:d + 1, :] + _dot(xl_b, a2_ref[d]))
        kd = k * (1.0 + (asig - 1.0) * ka_ref[...])
        kd_o[...] = kd
        b_o[...] = kk * asig
        kd_sum = kd if kd_sum is None else kd_sum + kd
    bonus_o[...] = _head_sum(r * rk_ref[...] * kd_sum, e_bf) * v


def _rwkv_prep(pa, mu_p, w0, a0, w2p, a2p, g2, k_k, k_a, r_k, e_bf, *, n_lat, seq, ctx_len):
    r = pa.shape[0]
    ncol = mu_p.shape[1]
    aw = w0.shape[1]
    tp = PREP_TILE
    hb = tp // SUBLANE
    nblk8 = r // SUBLANE
    lw = 2 * A_LORA_W + 2 * A_LORA_A
    row = lambda a: a.reshape(1, aw)
    full = lambda shape: pl.BlockSpec(shape, lambda i: (0,) * len(shape))
    out_spec = pl.BlockSpec((tp, aw), lambda i: (i, 0))
    out_sds = jax.ShapeDtypeStruct((r, aw), F32)
    return pl.pallas_call(
        functools.partial(_rwkv_prep_kernel, n_lat_tiles=n_lat // tp, lat_tiles_per_seq=seq // tp,
                          ctx_tiles_per_seq=ctx_len // tp, aw=aw),
        grid=(r // tp,),
        in_specs=[
            pl.BlockSpec((tp, ncol), lambda i: (i, 0)),
            pl.BlockSpec((SUBLANE, ncol), lambda i: (jnp.maximum(i * hb - 1, 0), 0)),
            pl.BlockSpec((SUBLANE, ncol), lambda i: (jnp.minimum((i + 1) * hb, nblk8 - 1), 0)),
            full((2, ncol)), full((2, aw)), full((2, aw)),
            full((2, lw, aw)), full((2, lw, aw)), full((A_LORA_G, aw)),
            full((1, aw)), full((1, aw)), full((1, aw)), full((GROUP_W, GROUP_W)),
        ],
        out_specs=[out_spec] * 11,
        out_shape=[out_sds] * 11,
        compiler_params=_cparams(("parallel",)),
    )(pa, pa, pa, mu_p, w0, a0, w2p, a2p, g2, row(k_k), row(k_a), row(r_k), e_bf)


def _rwkv_masks(reverse):
    L = CHUNK
    gw = GROUP_W
    row = lax.broadcasted_iota(jnp.int32, (gw, gw), 0)
    col = lax.broadcasted_iota(jnp.int32, (gw, gw), 1)
    same_head = (row >> 6) == (col >> 6)
    tr = row & (L - 1)
    tc = col & (L - 1)
    strict = same_head & ((tr < tc) if reverse else (tr > tc))
    incl = same_head & ((tr <= tc) if reverse else (tr >= tc))
    t_r = lax.broadcasted_iota(jnp.int32, (L, L), 0)
    t_c = lax.broadcasted_iota(jnp.int32, (L, L), 1)
    tri = _bf(jnp.where((t_c >= t_r) if reverse else (t_c <= t_r), 1.0, 0.0))
    return dict(same_head=same_head, strict=strict, incl=incl, eye=row == col, tri=tri)


def _rwkv_chunks(probs):
    L = CHUNK
    gw = GROUP_W
    masks = {rv: _rwkv_masks(rv) for rv in sorted({p[7] for p in probs})}
    mk = [masks[p[7]] for p in probs]
    each = lambda f, *cols: [f(*args) for args in zip(*cols)]
    rep = lambda x: jnp.concatenate([x] * HEADS_PER_GROUP, axis=0)
    nat = lambda x: x[0:L] + x[L:2 * L] + x[2 * L:3 * L] + x[3 * L:4 * L]

    c = [_dot_exact_lhs(m["tri"], p[3]) for m, p in zip(mk, probs)]
    c_l = [ci[(0 if p[7] else L - 1):(1 if p[7] else L), :] for ci, p in zip(c, probs)]

    at_bd, rt_bd, v_bf, bend_bd, kend_bd, lhs, rhs, e_l = [], [], [], [], [], [], [], []
    for (r, v, kk, lw, kd, b, _, _), m, ci, cl in zip(probs, mk, c, c_l):
        bd = lambda x, sh=m["same_head"]: jnp.where(sh, rep(x), 0.0)
        e_nc = jnp.exp(-ci)
        e_end = jnp.exp(cl - ci)
        a_ = bd(-kk * jnp.exp(ci - lw))
        r_ = bd(r * jnp.exp(ci))
        at_bd.append(a_)
        rt_bd.append(r_)
        v_bf.append(_bf(bd(v)))
        bend_bd.append(_bf(bd(b * e_end)))
        kend_bd.append(_bf(bd(kd * e_end)))
        lhs.append(_bf(jnp.concatenate([a_, r_], axis=0)))
        rhs.append(_bf(jnp.concatenate([rep(b * e_nc), rep(kd * e_nc)], axis=0)))
        e_l.append(jnp.exp(cl))

    s = each(_dot_nt, lhs, rhs)
    a_ab = [jnp.where(m["strict"], si[:gw, :gw], 0.0) for m, si in zip(mk, s)]
    a_ak = [_bf(jnp.where(m["strict"], si[:gw, gw:], 0.0)) for m, si in zip(mk, s)]
    r_b = [_bf(jnp.where(m["incl"], si[gw:, :gw], 0.0)) for m, si in zip(mk, s)]
    r_k = [_bf(jnp.where(m["incl"], si[gw:, gw:], 0.0)) for m, si in zip(mk, s)]

    akv = each(_dot, a_ak, v_bf)
    t = [jnp.where(m["eye"], 1.0, 0.0) + a for m, a in zip(mk, a_ab)]
    a_pow = a_ab
    for _ in range(5):
        a_pow_bf = [_bf(a) for a in a_pow]
        a_pow = each(_dot, a_pow_bf, a_pow_bf)
        t = [ti + _dot(_bf(ti), _bf(ap)) for ti, ap in zip(t, a_pow)]
    h_bf = [_bf(p[6]) for p in probs]
    x = [_dot(_bf(a_), h) + ak for a_, h, ak in zip(at_bd, h_bf, akv)]
    u = [_dot(_bf(ti), _bf(xi)) for ti, xi in zip(t, x)]
    uv = [jnp.concatenate([_bf(ui), v], axis=0) for ui, v in zip(u, v_bf)]
    y = [_dot(_bf(nat(r_)), h) + nat(_dot(jnp.concatenate([rb, rk], axis=1), uvi))
         for r_, h, rb, rk, uvi in zip(rt_bd, h_bf, r_b, r_k, uv)]
    e_col = [jnp.broadcast_to(el, (gw, gw)).T for el in e_l]
    h_new = [ec * p[6] + _dot_tn(jnp.concatenate([be, ke], axis=0), uvi)
             for ec, p, be, ke, uvi in zip(e_col, probs, bend_bd, kend_bd, uv)]
    return list(zip(y, h_new))


def _rwkv_scan_kernel(rf, vf, kkf, lwf, kdf, bf_, rb, vb, kkb, lwb, kdb, bb,
                      yf_ref, yb_ref, hf_ref, hb_ref):
    @pl.when(pl.program_id(1) == 0)
    def _():
        hf_ref[...] = jnp.zeros_like(hf_ref)
        hb_ref[...] = jnp.zeros_like(hb_ref)

    gs = hf_ref.shape[0]
    probs = []
    for gi in range(gs):
        sl = slice(gi * GROUP_W, (gi + 1) * GROUP_W)
        probs.append((rf[:, sl], vf[:, sl], kkf[:, sl], lwf[:, sl], kdf[:, sl], bf_[:, sl],
                      hf_ref[gi], False))
        probs.append((rb[:, sl], vb[:, sl], kkb[:, sl], lwb[:, sl], kdb[:, sl], bb[:, sl],
                      hb_ref[gi], True))
    res = _rwkv_chunks(probs)
    for gi in range(gs):
        sl = slice(gi * GROUP_W, (gi + 1) * GROUP_W)
        yf_ref[:, sl], hf_ref[gi] = res[2 * gi]
        yb_ref[:, sl], hb_ref[gi] = res[2 * gi + 1]


def _chunk_row_block(b, i, *, reverse, n_lat, seq, ctx_len):
    nc = ctx_len // CHUNK
    nl = seq // CHUNK
    ci = (nc - 1 - i) if reverse else i
    li = (nl - 1 - (i - nc)) if reverse else (i - nc)
    return jnp.where(i < nc, (n_lat + b * ctx_len) // CHUNK + ci, b * nl + li)


def _rwkv_scan(r, v, kk, lw0, lw1, kd0, kd1, b0, b1, *, batch, n_lat, seq, ctx_len):
    rows, aw = r.shape
    gs = RWKV_GROUPS_PER_STEP
    ng = aw // (GROUP_W * gs)
    geo = dict(n_lat=n_lat, seq=seq, ctx_len=ctx_len)

    def spec(reverse):
        return pl.BlockSpec(
            (CHUNK, GROUP_W * gs),
            lambda pg, i: (_chunk_row_block(pg // ng, i, reverse=reverse, **geo), pg % ng))

    sf, sb = spec(False), spec(True)
    sds = jax.ShapeDtypeStruct((rows, aw), F32)
    return pl.pallas_call(
        _rwkv_scan_kernel,
        grid=(batch * ng, (seq + ctx_len) // CHUNK),
        in_specs=[sf] * 6 + [sb] * 6,
        out_specs=[sf, sb],
        out_shape=[sds, sds],
        scratch_shapes=[pltpu.VMEM((gs, GROUP_W, GROUP_W), F32),
                        pltpu.VMEM((gs, GROUP_W, GROUP_W), F32)],
        compiler_params=_cparams(("parallel", "arbitrary")),
    )(r, v, kk, lw0, kd0, b0, r, v, kk, lw1, kd1, b1)


def _gla_scan_kernel(qkf, vf, gdf, qkb, vb, gdb, gw2_ref, gb_ref, of_ref, ob_ref, sf_ref, sb_ref,
                     *, dk, dv):
    @pl.when(pl.program_id(1) == 0)
    def _():
        sf_ref[...] = jnp.zeros_like(sf_ref)
        sb_ref[...] = jnp.zeros_like(sb_ref)

    L = CHUNK
    wk = B_HEADS * dk
    t_r = lax.broadcasted_iota(jnp.int32, (L, L), 0)
    t_c = lax.broadcasted_iota(jnp.int32, (L, L), 1)

    qh, kh, keh, vh, dech, keeph, sth = [], [], [], [], [], [], []
    for d, (qk_ref, v_ref, gd_ref, st_ref) in enumerate(((qkf, vf, gdf, sf_ref),
                                                         (qkb, vb, gdb, sb_ref))):
        reverse = d == 1
        keep = (t_c >= t_r) if reverse else (t_c <= t_r)
        tri = _bf(jnp.where(keep, 1.0, 0.0))
        z = _dot(_bf(gd_ref[...]), gw2_ref[d]) + gb_ref[d:d + 1, :]
        la = -_softplus(-z) * (1.0 / B_GATE_TAU)
        cum = _dot_exact_lhs(tri, la)
        last = 0 if reverse else L - 1
        cl = cum[last:last + 1, :]
        q_in = _bf(qk_ref[:, :wk] * (dk ** -0.5) * jnp.exp(cum))
        k_in = _bf(qk_ref[:, wk:] * jnp.exp(-cum))
        k_end = _bf(qk_ref[:, wk:] * jnp.exp(cl - cum))
        dec = jnp.exp(cl)
        for h in range(B_HEADS):
            ks = slice(h * dk, (h + 1) * dk)
            qh.append(q_in[:, ks])
            kh.append(k_in[:, ks])
            keh.append(k_end[:, ks])
            vh.append(_bf(v_ref[:, h * dv:(h + 1) * dv]))
            dech.append(dec[:, ks])
            keeph.append(keep)
            sth.append(st_ref[h])

    sc = [_bf(jnp.where(kp, _dot_nt(q, k), 0.0)) for kp, q, k in zip(keeph, qh, kh)]
    inter = [_dot_nt(q, _bf(st)) for q, st in zip(qh, sth)]
    kvt = [_dot_tn(v, ke) for v, ke in zip(vh, keh)]
    out = [_dot(s_, v) + it for s_, v, it in zip(sc, vh, inter)]
    for d, (o_ref, st_ref) in enumerate(((of_ref, sf_ref), (ob_ref, sb_ref))):
        for h in range(B_HEADS):
            i = d * B_HEADS + h
            o_ref[:, h * dv:(h + 1) * dv] = out[i]
            st_ref[h] = sth[i] * dech[i] + kvt[i]


def _gla_scan(pb, gw2p, gb, *, batch, n_lat, seq, ctx_len, wk, wv, col0):
    rows = pb.shape[0]
    dk, dv = wk // B_HEADS, wv // B_HEADS
    geo = dict(n_lat=n_lat, seq=seq, ctx_len=ctx_len)
    gd_blk = (col0 + 2 * wk + 2 * wv) // LANE

    def specs(reverse):
        rb = lambda b, i: _chunk_row_block(b, i, reverse=reverse, **geo)
        return [
            pl.BlockSpec((CHUNK, 2 * wk), lambda b, i: (rb(b, i), col0 // (2 * wk))),
            pl.BlockSpec((CHUNK, wv), lambda b, i: (rb(b, i), (col0 + 2 * wk) // wv)),
            pl.BlockSpec((CHUNK, LANE), lambda b, i: (rb(b, i), gd_blk)),
        ]

    def out_spec(reverse):
        return pl.BlockSpec((CHUNK, wv),
                            lambda b, i: (_chunk_row_block(b, i, reverse=reverse, **geo), 0))

    sds = jax.ShapeDtypeStruct((rows, wv), F32)
    return pl.pallas_call(
        functools.partial(_gla_scan_kernel, dk=dk, dv=dv),
        grid=(batch, (seq + ctx_len) // CHUNK),
        in_specs=specs(False) + specs(True) + [
            pl.BlockSpec((2, LANE, wk), lambda b, i: (0, 0, 0)),
            pl.BlockSpec((2, wk), lambda b, i: (0, 0)),
        ],
        out_specs=[out_spec(False), out_spec(True)],
        out_shape=[sds, sds],
        scratch_shapes=[pltpu.VMEM((B_HEADS, dv, dk), F32), pltpu.VMEM((B_HEADS, dv, dk), F32)],
        compiler_params=_cparams(("parallel", "arbitrary")),
    )(pb, pb, pb, pb, pb, pb, gw2p, gb)


def _mixer_out_kernel(yf, yb, bonus, g, of, ob, gate, x_ref, mod_ref, lnw, lnb, gn, e_ref, w_ref,
                      o_ref, h_ref, *, aw, dv):
    e_bf = e_ref[...]
    y = yf[...] + yb[...]
    inv = 1.0 / A_HEAD_DIM
    mean = _head_sum(y, e_bf) * inv
    dlt = y - mean
    var = _head_sum(dlt * dlt, e_bf) * inv
    yn = dlt * lax.rsqrt(var + A_LN_EPS) * lnw[...] + lnb[...]
    h_ref[:, :aw] = _bf((yn + bonus[...]) * g[...])

    o = of[...] + ob[...]
    gt = _silu(gate[...])
    for h in range(B_HEADS):
        sl = slice(h * dv, (h + 1) * dv)
        oh = o[:, sl]
        ms = jnp.mean(oh * oh, axis=-1, keepdims=True)
        h_ref[:, aw + h * dv:aw + (h + 1) * dv] = _bf(oh * lax.rsqrt(ms + NORM_EPS) * gn[...] * gt[:, sl])

    o_ref[...] = x_ref[...] + mod_ref[2:3, :] * _dot(h_ref[...], w_ref[...])


def _mixer_out(yf, yb, bonus, g, of, ob, pb, x, mods_l, ln_w, ln_b, gla_norm, e_bf, w_bf,
               seg_of_tile, *, wk, col0):
    rows, d = x.shape
    aw = yf.shape[1]
    wv = of.shape[1]
    dv = wv // B_HEADS
    tm = 256
    tiles_per_row_tile = ROW_TILE // tm
    rs = lambda w: pl.BlockSpec((tm, w), lambda i: (i, 0))
    full = lambda shape: pl.BlockSpec(shape, lambda i: (0,) * len(shape))
    return pl.pallas_call(
        functools.partial(_mixer_out_kernel, aw=aw, dv=dv),
        grid=(rows // tm,),
        in_specs=[
            rs(aw), rs(aw), rs(aw), rs(aw), rs(wv), rs(wv),
            pl.BlockSpec((tm, wv), lambda i: (i, (col0 + 2 * wk + wv) // wv)),
            rs(d),
            pl.BlockSpec((None, N_MOD, d), lambda i: (seg_of_tile(i // tiles_per_row_tile), 0, 0)),
            full((1, aw)), full((1, aw)), full((1, dv)), full((GROUP_W, GROUP_W)),
            full((aw + wv, d)),
        ],
        out_specs=rs(d),
        out_shape=jax.ShapeDtypeStruct((rows, d), F32),
        scratch_shapes=[pltpu.VMEM((tm, aw + wv), BF16)],
        compiler_params=_cparams(("parallel",)),
    )(yf, yb, bonus, g, of, ob, pb, x, mods_l, ln_w.reshape(1, aw), ln_b.reshape(1, aw),
      gla_norm.reshape(1, dv), e_bf, w_bf)


def _lru_kernel(*refs, reverse, n_blk, combine):
    if combine:
        (x_ref, xp_ref, xn_ref, h0_ref, cw_ref, cb_ref, wc_ref, ba_ref, bx_ref, lam_ref,
         hf_ref, gate_ref, out_ref, hl_ref, a_sc, u_sc, hs_sc, carry_sc) = refs
    else:
        (x_ref, xp_ref, xn_ref, h0_ref, cw_ref, cb_ref, wc_ref, ba_ref, bx_ref, lam_ref,
         out_ref, hl_ref, a_sc, u_sc, hs_sc, carry_sc) = refs
    j = pl.program_id(2)
    blk = (n_blk - 1 - j) if reverse else j
    lb, ct = x_ref.shape
    cblk = wc_ref.shape[1]
    is_first = blk == 0
    is_last = blk == n_blk - 1
    rowid = lax.broadcasted_iota(jnp.int32, (lb, 1), 0)

    x = x_ref[...]
    p6 = jnp.where(is_first, 0.0, xp_ref[SUBLANE - 2:SUBLANE - 1, :])
    p7 = jnp.where(is_first, 0.0, xp_ref[SUBLANE - 1:SUBLANE, :])
    n0 = jnp.where(is_last, 0.0, xn_ref[0:1, :])
    xm1 = jnp.where(rowid == 0, p7, pltpu.roll(x, 1, 0))
    xm2 = jnp.where(rowid == 0, p6, jnp.where(rowid == 1, p7, pltpu.roll(x, 2, 0)))
    xp1 = jnp.where(rowid == lb - 1, n0, pltpu.roll(x, lb - 1, 0))
    xs = (cb_ref[...] + cw_ref[0:1, :] * xm2 + cw_ref[1:2, :] * xm1 + cw_ref[2:3, :] * x
          + cw_ref[3:4, :] * xp1)

    for n in range(ct // cblk):
        sl = slice(n * cblk, (n + 1) * cblk)
        xb = xs[:, sl]
        ri = _dot(_bf(xb), wc_ref[n])
        rg = _sigmoid(ri[:, :cblk] + ba_ref[:, sl])
        ig = _sigmoid(ri[:, cblk:] + bx_ref[:, sl])
        log_a = -C_CONST * rg * _softplus(-lam_ref[:, sl])
        a = jnp.exp(log_a)
        a_sc[:, sl] = a
        u_sc[:, sl] = jnp.sqrt(1.0 - a * a) * (ig * xb)

    @pl.when(j == 0)
    def _():
        carry_sc[...] = h0_ref[0:1, :]

    def step(t, h):
        tt = (lb - 1 - t) if reverse else t
        h = a_sc[pl.ds(tt, 1), :] * h + u_sc[pl.ds(tt, 1), :]
        hs_sc[pl.ds(tt, 1), :] = h
        return h

    h = lax.fori_loop(0, lb, step, carry_sc[...], unroll=8)
    carry_sc[...] = h
    hl_ref[...] = jnp.broadcast_to(h, hl_ref.shape)
    if combine:
        out_ref[...] = (hs_sc[...] + hf_ref[...]) * jax.nn.gelu(gate_ref[...])
    else:
        out_ref[...] = hs_sc[...]


def _lru_pass(x_arr, x_specs, h0, conv_w, conv_b, wcat, ba, bx, lam, out_shape, out_spec, extra,
              *, batch, n_blk, lb, reverse):
    c = conv_w.shape[1]
    ct = 1024
    nct = c // ct
    cblk = wcat.shape[1]
    chan = lambda rows_: pl.BlockSpec((rows_, ct), lambda b, k, j: (0, k))
    in_specs = list(x_specs) + [
        pl.BlockSpec((SUBLANE, ct), lambda b, k, j: (b, k)),
        chan(C_CONV), chan(1),
        pl.BlockSpec((ct // cblk, cblk, 2 * cblk), lambda b, k, j: (k, 0, 0)),
        chan(1), chan(1), chan(1),
    ] + [s for _, s in extra]
    args = [x_arr, x_arr, x_arr, h0, conv_w, conv_b.reshape(1, c), wcat, ba.reshape(1, c),
            bx.reshape(1, c), lam.reshape(1, c)] + [a for a, _ in extra]
    return pl.pallas_call(
        functools.partial(_lru_kernel, reverse=reverse, n_blk=n_blk, combine=bool(extra)),
        grid=(batch, nct, n_blk),
        in_specs=in_specs,
        out_specs=[out_spec, pl.BlockSpec((SUBLANE, ct), lambda b, k, j: (b, k))],
        out_shape=[out_shape, jax.ShapeDtypeStruct((batch * SUBLANE, c), F32)],
        scratch_shapes=[pltpu.VMEM((lb, ct), F32), pltpu.VMEM((lb, ct), F32),
                        pltpu.VMEM((lb, ct), F32), pltpu.VMEM((1, ct), F32)],
        compiler_params=_cparams(("parallel", "parallel", "arbitrary")),
    )(*args)


def _rglru(hw_ctx, gate_grid, x_grid, conv_w, conv_b, wa, ba, wx, bx, lam, *, batch, n_lat, seq,
           ctx_len):
    c = conv_w.shape[1]
    ct = 1024
    rows = seq // GRID_W
    lbc = 128
    ncb = ctx_len // lbc
    xcol = c // ct
    per_col = c // ct

    wcat = [_bf(jnp.concatenate([wa[d], wx[d]], axis=-1)) for d in range(2)]
    zeros_state = jnp.zeros((batch * SUBLANE, c), F32)

    def ctx_specs(reverse):
        blk = (lambda j: ncb - 1 - j) if reverse else (lambda j: j)
        per8 = lbc // SUBLANE
        cur = pl.BlockSpec((lbc, ct), lambda b, k, j: (b * ncb + blk(j), xcol + k))
        prv = pl.BlockSpec((SUBLANE, ct), lambda b, k, j: (
            b * ncb * per8 + jnp.maximum(blk(j) * per8 - 1, 0), xcol + k))
        nxt = pl.BlockSpec((SUBLANE, ct), lambda b, k, j: (
            b * ncb * per8 + jnp.minimum((blk(j) + 1) * per8, ncb * per8 - 1), xcol + k))
        return cur, prv, nxt

    def lat_specs(reverse):
        blk = (lambda j: GRID_W - 1 - j) if reverse else (lambda j: j)
        r8 = rows // SUBLANE
        cur = pl.BlockSpec((rows, ct), lambda b, k, j: (b, blk(j) * per_col + k))
        prv = pl.BlockSpec((SUBLANE, ct), lambda b, k, j: (
            b * r8 + r8 - 1, jnp.maximum(blk(j) - 1, 0) * per_col + k))
        nxt = pl.BlockSpec((SUBLANE, ct), lambda b, k, j: (
            b * r8, jnp.minimum(blk(j) + 1, GRID_W - 1) * per_col + k))
        return cur, prv, nxt

    common = dict(batch=batch)
    ctx_scratch_sds = jax.ShapeDtypeStruct((batch * ctx_len, c), F32)
    ctx_out = lambda reverse: pl.BlockSpec(
        (lbc, ct), lambda b, k, j: (b * ncb + ((ncb - 1 - j) if reverse else j), k))
    _, st = _lru_pass(hw_ctx, ctx_specs(False), zeros_state, conv_w, conv_b, wcat[0], ba[0], bx[0],
                      lam[0], ctx_scratch_sds, ctx_out(False), [], n_blk=ncb, lb=lbc,
                      reverse=False, **common)
    hf_sds = jax.ShapeDtypeStruct((n_lat, c), F32)
    hf_spec = lambda reverse: pl.BlockSpec(
        (rows, ct), lambda b, k, j: (b * GRID_W + ((GRID_W - 1 - j) if reverse else j), k))
    hf, _ = _lru_pass(x_grid, lat_specs(False), st, conv_w, conv_b, wcat[0], ba[0], bx[0], lam[0],
                      hf_sds, hf_spec(False), [], n_blk=GRID_W, lb=rows, reverse=False, **common)
    _, st = _lru_pass(hw_ctx, ctx_specs(True), zeros_state, conv_w, conv_b, wcat[1], ba[1], bx[1],
                      lam[1], ctx_scratch_sds, ctx_out(True), [], n_blk=ncb, lb=lbc,
                      reverse=True, **common)
    grid_spec = pl.BlockSpec((rows, ct), lambda b, k, j: (b, (GRID_W - 1 - j) * per_col + k))
    y_sds = jax.ShapeDtypeStruct((n_lat // GRID_W, GRID_W * c), F32)
    y, _ = _lru_pass(x_grid, lat_specs(True), st, conv_w, conv_b, wcat[1], ba[1], bx[1], lam[1],
                     y_sds, grid_spec, [(hf, hf_spec(True)), (gate_grid, grid_spec)],
                     n_blk=GRID_W, lb=rows, reverse=True, **common)
    return y


def _pad_cols(w, n):
    return jnp.pad(w, ((0, 0),) * (w.ndim - 1) + ((0, n - w.shape[-1]),))


def kernel(x, c, ctx, c_ctx, mod_w, mod_b, norm1, norm2, mlp_w1, mlp_w2, ab_w_in, ab_w_out, rw_mu,
           rw_w0, rw_w2, rw_a0, rw_a2, rw_g2, rw_kk, rw_ka, rw_rk, rw_ln_w, rw_ln_b, gla_gw2,
           gla_gb, gla_norm, lru_w_in, lru_w_out, lru_conv_w, lru_conv_b, lru_wa, lru_ba, lru_wx,
           lru_bx, lru_lam, final_norm):
    batch, seq, d = x.shape
    ctx_len = ctx.shape[1]
    depth = mod_w.shape[0]
    n_lat = batch * seq
    aw = rw_w0.shape[-1]
    wk = gla_gb.shape[-1]
    wv = ab_w_out.shape[1] - aw
    a_cols = rw_mu.shape[-1]
    b_cols = ab_w_in.shape[-1] - a_cols
    assert depth == 2 and batch + 1 <= SUBLANE
    assert seq % ROW_TILE == 0 and (batch * ctx_len) % ROW_TILE == 0
    assert ctx_len % PREP_TILE == 0 and ctx_len % 128 == 0 and seq % (GRID_W * SUBLANE) == 0
    assert aw % GROUP_W == 0 and wv == aw and 2 * wk == wv

    tiles_per_seq = seq // ROW_TILE
    seg_of_tile = lambda i: jnp.minimum(i // tiles_per_seq, batch)

    xs = jnp.concatenate([x.reshape(n_lat, d), ctx.reshape(batch * ctx_len, d)], axis=0)
    cvec = jnp.concatenate(
        [c, c_ctx[None, :], jnp.zeros((SUBLANE - batch - 1, d), F32)], axis=0)
    mods = _mods(cvec, mod_w, mod_b)

    hid = jnp.arange(GROUP_W) // A_HEAD_DIM
    e_bf = _bf(hid[:, None] == hid[None, :])

    a_pad = -(-a_cols // LANE) * LANE
    col0 = -(-a_cols // (2 * wk)) * (2 * wk)
    n_tiles = 3
    tn = -(-(col0 + b_cols) // (n_tiles * LANE)) * LANE
    w_ab = _bf(jnp.concatenate(
        [_pad_cols(ab_w_in[0][:, :a_cols], col0),
         _pad_cols(ab_w_in[0][:, a_cols:], n_tiles * tn - col0)], axis=1))
    pa = pb = _norm_matmul(xs, norm1[0], mods[0], w_ab, seg_of_tile, tn=tn)

    lora_w = 2 * A_LORA_W + 2 * A_LORA_A
    w2p = jnp.zeros((2, lora_w, aw), F32)
    a2p = jnp.zeros((2, lora_w, aw), F32)
    for dd in range(2):
        w2p = w2p.at[dd, dd * A_LORA_W:(dd + 1) * A_LORA_W].set(rw_w2[0, dd])
        a2p = a2p.at[dd, 2 * A_LORA_W + dd * A_LORA_A:2 * A_LORA_W + (dd + 1) * A_LORA_A].set(
            rw_a2[0, dd])
    (r_, v_, kk_, lw0, lw1, kd0, kd1, b0, b1, g_, bonus) = _rwkv_prep(
        pa, _pad_cols(rw_mu[0], a_pad), rw_w0[0], rw_a0[0], _bf(w2p), _bf(a2p), _bf(rw_g2[0]),
        rw_kk[0], rw_ka[0], rw_rk[0].reshape(-1), e_bf, n_lat=n_lat, seq=seq, ctx_len=ctx_len)
    yf, yb = _rwkv_scan(r_, v_, kk_, lw0, lw1, kd0, kd1, b0, b1, batch=batch, n_lat=n_lat, seq=seq,
                        ctx_len=ctx_len)

    gw2p = jnp.zeros((2, LANE, wk), F32)
    for dd in range(2):
        gw2p = gw2p.at[dd, dd * B_LORA:(dd + 1) * B_LORA].set(gla_gw2[0, dd])
    of, ob = _gla_scan(pb, _bf(gw2p), gla_gb[0], batch=batch, n_lat=n_lat, seq=seq,
                       ctx_len=ctx_len, wk=wk, wv=wv, col0=col0)

    xs = _mixer_out(yf, yb, bonus, g_, of, ob, pb, xs, mods[0], rw_ln_w[0], rw_ln_b[0],
                    gla_norm[0], e_bf, _bf(ab_w_out[0]), seg_of_tile, wk=wk, col0=col0)
    seg_of_mlp_tile = lambda i: jnp.minimum(i // (seq // MLP_TILE), batch)
    ctx_tiles = (batch * ctx_len) // ROW_TILE
    w1_bf, w2_bf = _bf(mlp_w1[0]), _bf(mlp_w2[0])
    x_lat = _mlp(xs, norm2[0], mods[0], w1_bf, w2_bf, final_norm, seg_of_mlp_tile, tm=MLP_TILE,
                 tile0=0, n_tiles=n_lat // MLP_TILE, final_norm=False)
    x_ctx = _mlp(xs, norm2[0], mods[0], w1_bf, w2_bf, final_norm, seg_of_tile, tm=ROW_TILE,
                 tile0=n_lat // ROW_TILE, n_tiles=ctx_tiles, final_norm=False)

    w_lru = _bf(lru_w_in[0])
    hw_ctx = _norm_matmul(x_ctx, norm1[1], mods[1], w_lru, lambda i: batch, tn=2048)
    gate_grid, x_grid = _norm_matmul_grid(x_lat, n_lat, seq // GRID_W, norm1[1], mods[1], w_lru)
    y = _rglru(hw_ctx, gate_grid, x_grid, lru_conv_w[0], lru_conv_b[0], lru_wa[0], lru_ba[0],
               lru_wx[0], lru_bx[0], lru_lam[0], batch=batch, n_lat=n_lat, seq=seq,
               ctx_len=ctx_len)
    xl = _proj_res(y, x_lat, n_lat, seq // GRID_W, mods[1], _bf(lru_w_out[0]))
    out = _mlp(xl, norm2[1], mods[1], _bf(mlp_w1[1]), _bf(mlp_w2[1]), final_norm, seg_of_mlp_tile,
               tm=MLP_TILE, tile0=0, n_tiles=n_lat // MLP_TILE, final_norm=True)
    return out.reshape(batch, seq, d)
```

```python
import functools

import jax
import jax.numpy as jnp
from jax import lax
from jax.experimental import pallas as pl
from jax.experimental.pallas import tpu as pltpu

F32 = jnp.float32
BF16 = jnp.bfloat16

NORM_EPS = 1e-6
GRID_W = 64
N_MOD = 6

A_HEAD_DIM = 64
A_LORA_W = 96
A_LORA_A = 96
A_LORA_G = 256
A_LN_EPS = 64e-5
HEADS_PER_GROUP = 4
GROUP_W = HEADS_PER_GROUP * A_HEAD_DIM
CHUNK = 64
RWKV_GROUPS_PER_STEP = 4

B_HEADS = 4
B_LORA = 16
B_GATE_TAU = 16.0

C_BLOCKS = 8
C_CONV = 4
C_CONST = 8.0

LANE = 128
SUBLANE = 8
ROW_TILE = 512
MLP_TILE = 1024
MLP_STEP_ELEMS = 1024 * 512
PREP_TILE = 128
VMEM_LIMIT = 56 * 1024 * 1024


def _cparams(sem):
    return pltpu.CompilerParams(dimension_semantics=sem, vmem_limit_bytes=VMEM_LIMIT)


def _bf(x):
    return x.astype(BF16)


def _dot(a, b):
    return jnp.dot(a, b, preferred_element_type=F32)


def _dot_nt(a, b):
    return lax.dot_general(a, b, (((1,), (1,)), ((), ())), preferred_element_type=F32)


def _dot_tn(a, b):
    return lax.dot_general(a, b, (((0,), (0,)), ((), ())), preferred_element_type=F32)


def _split3(x):
    hi = _bf(x)
    r1 = x - hi.astype(F32)
    mid = _bf(r1)
    lo = _bf(r1 - mid.astype(F32))
    return hi, mid, lo


def _dot_exact_lhs(m_bf, x):
    hi, mid, lo = _split3(x)
    return _dot(m_bf, hi) + _dot(m_bf, mid) + _dot(m_bf, lo)


def _dot_exact_rhs(x, m_bf):
    hi, mid, lo = _split3(x)
    return _dot(hi, m_bf) + _dot(mid, m_bf) + _dot(lo, m_bf)


def _softplus(x):
    return jnp.maximum(x, 0.0) + jnp.log(1.0 + jnp.exp(-jnp.abs(x)))


def _sigmoid(x):
    return 0.5 * jnp.tanh(0.5 * x) + 0.5


def _silu(x):
    return x * _sigmoid(x)


def _norm_mod(x, g, shift, scale):
    ms = jnp.mean(x * x, axis=-1, keepdims=True)
    return (x * lax.rsqrt(ms + NORM_EPS) * g) * (1.0 + scale) + shift


def _mods_kernel(c_ref, w_ref, b_ref, o_ref):
    s = _silu(c_ref[...])
    o_ref[...] = _dot(_bf(s), _bf(w_ref[...])) + b_ref[...]


def _mods(cvec, mod_w, mod_b):
    depth, d, n = mod_w.shape
    tn = 1024
    out = pl.pallas_call(
        _mods_kernel,
        grid=(depth, n // tn),
        in_specs=[
            pl.BlockSpec((SUBLANE, d), lambda l, j: (0, 0)),
            pl.BlockSpec((None, d, tn), lambda l, j: (l, 0, j)),
            pl.BlockSpec((None, 1, tn), lambda l, j: (l, 0, j)),
        ],
        out_specs=pl.BlockSpec((None, SUBLANE, tn), lambda l, j: (l, 0, j)),
        out_shape=jax.ShapeDtypeStruct((depth, SUBLANE, n), F32),
        compiler_params=_cparams(("parallel", "parallel")),
    )(cvec, mod_w, mod_b.reshape(depth, 1, n))
    return out.reshape(depth, SUBLANE, N_MOD, d)


def _norm_matmul_kernel(xa_ref, xb_ref, g_ref, mod_ref, w_ref, o_ref, h_ref, *, n_a_tiles):
    i = pl.program_id(0)

    @pl.when(pl.program_id(1) == 0)
    def _():
        x = jnp.where(i < n_a_tiles, xa_ref[...], xb_ref[...])
        h_ref[...] = _bf(_norm_mod(x, g_ref[...], mod_ref[0:1, :], mod_ref[1:2, :]))

    o_ref[...] = _dot(h_ref[...], w_ref[...])


def _two_source_specs(tm, d, n_a_tiles):
    return [pl.BlockSpec((tm, d), lambda i, *_: (jnp.minimum(i, n_a_tiles - 1), 0)),
            pl.BlockSpec((tm, d), lambda i, *_: (jnp.maximum(i - n_a_tiles, 0), 0))]


def _norm_matmul(xa, xb, g, mods_l, w_bf, seg_of_tile, *, tn):
    d = xa.shape[1]
    n = w_bf.shape[1]
    tm = ROW_TILE
    n_a_tiles = xa.shape[0] // tm
    n_tiles = n_a_tiles + (0 if xb is None else xb.shape[0] // tm)
    return pl.pallas_call(
        functools.partial(_norm_matmul_kernel, n_a_tiles=n_a_tiles),
        grid=(n_tiles, n // tn),
        in_specs=_two_source_specs(tm, d, n_a_tiles) + [
            pl.BlockSpec((1, d), lambda i, j: (0, 0)),
            pl.BlockSpec((None, N_MOD, d), lambda i, j: (seg_of_tile(i), 0, 0)),
            pl.BlockSpec((d, tn), lambda i, j: (0, j)),
        ],
        out_specs=pl.BlockSpec((tm, tn), lambda i, j: (i, j)),
        out_shape=jax.ShapeDtypeStruct((n_tiles * tm, n), F32),
        scratch_shapes=[pltpu.VMEM((tm, d), BF16)],
        compiler_params=_cparams(("parallel", "arbitrary")),
    )(xa, xa if xb is None else xb, g.reshape(1, d), mods_l, w_bf)


def _grid_tile(rows):
    gr = min(rows, 64)
    gc = ROW_TILE // gr
    assert gc % SUBLANE == 0 and GRID_W % gc == 0 and rows % gr == 0
    return gr, gc


def _norm_matmul_grid_kernel(x_ref, g_ref, mod_ref, w_ref, o0_ref, o1_ref, h_ref, res_ref):
    j = pl.program_id(1)
    gr, gc, d = x_ref.shape

    @pl.when(j == 0)
    def _():
        h = _norm_mod(x_ref[...].reshape(gr * gc, d), g_ref[...], mod_ref[0:1, :], mod_ref[1:2, :])
        h_ref[...] = _bf(h)

    res = _dot(h_ref[...], w_ref[...])
    nh = w_ref.shape[1]
    nq = nh // LANE
    for q in range(nq):
        res_ref[q] = res[:, q * LANE:(q + 1) * LANE]

    def scatter(o_ref):
        for c in range(gc):
            for q in range(nq):
                o_ref[:, c * nh + q * LANE:c * nh + (q + 1) * LANE] = (
                    res_ref[q, pl.ds(c, gr, stride=gc), :])

    @pl.when(j == 0)
    def _():
        scatter(o0_ref)

    @pl.when(j == 1)
    def _():
        scatter(o1_ref)


def _norm_matmul_grid(x, n_lat, rows, g, mods_l, w_bf):
    d = x.shape[1]
    nh = w_bf.shape[1] // 2
    gr, gc = _grid_tile(rows)
    ncg = GRID_W // gc
    x3 = x.reshape(x.shape[0] // GRID_W, GRID_W, d)
    out_spec = pl.BlockSpec((gr, gc * nh), lambda i, j: (i // ncg, i % ncg))
    sds = jax.ShapeDtypeStruct((n_lat // GRID_W, GRID_W * nh), F32)
    return pl.pallas_call(
        _norm_matmul_grid_kernel,
        grid=(n_lat // ROW_TILE, 2),
        in_specs=[
            pl.BlockSpec((gr, gc, d), lambda i, j: (i // ncg, i % ncg, 0)),
            pl.BlockSpec((1, d), lambda i, j: (0, 0)),
            pl.BlockSpec((None, N_MOD, d), lambda i, j: ((i // ncg) * gr // rows, 0, 0)),
            pl.BlockSpec((d, nh), lambda i, j: (0, j)),
        ],
        out_specs=[out_spec, out_spec],
        out_shape=[sds, sds],
        scratch_shapes=[pltpu.VMEM((ROW_TILE, d), BF16),
                        pltpu.VMEM((nh // LANE, ROW_TILE, LANE), F32)],
        compiler_params=_cparams(("parallel", "arbitrary")),
    )(x3, g.reshape(1, d), mods_l, w_bf)


def _mlp_kernel(x_ref, g_ref, mod_ref, w1_ref, w2_ref, fn_ref, o_ref, h_ref, *, final_norm):
    j = pl.program_id(1)
    slab = 256
    n_slab = x_ref.shape[0] // slab

    @pl.when(j == 0)
    def _():
        for s in range(n_slab):
            rs = slice(s * slab, (s + 1) * slab)
            h_ref[rs, :] = _bf(_norm_mod(x_ref[rs, :], g_ref[...], mod_ref[3:4, :],
                                         mod_ref[4:5, :]))
        o_ref[...] = jnp.zeros_like(o_ref)

    a = _dot(h_ref[...], w1_ref[...])
    a = _bf(jnp.square(jnp.maximum(a, 0.0)))
    nsplit = 2
    wn = o_ref.shape[1] // nsplit
    for s in range(nsplit):
        o_ref[:, s * wn:(s + 1) * wn] += _dot(a, w2_ref[:, s * wn:(s + 1) * wn])

    @pl.when(j == pl.num_programs(1) - 1)
    def _():
        for s in range(n_slab):
            rs = slice(s * slab, (s + 1) * slab)
            xo = x_ref[rs, :] + mod_ref[5:6, :] * o_ref[rs, :]
            if final_norm:
                ms = jnp.mean(xo * xo, axis=-1, keepdims=True)
                xo = xo * lax.rsqrt(ms + NORM_EPS) * fn_ref[...]
            o_ref[rs, :] = xo


def _mlp(x, g, mods_l, w1_bf, w2_bf, fn, seg_of_tile, *, tm, tile0, n_tiles, final_norm):
    d = x.shape[1]
    dff = w1_bf.shape[1]
    tf = MLP_STEP_ELEMS // tm
    return pl.pallas_call(
        functools.partial(_mlp_kernel, final_norm=final_norm),
        grid=(n_tiles, dff // tf),
        in_specs=[
            pl.BlockSpec((tm, d), lambda i, j: (i + tile0, 0), pipeline_mode=pl.Buffered(1)),
            pl.BlockSpec((1, d), lambda i, j: (0, 0)),
            pl.BlockSpec((None, N_MOD, d), lambda i, j: (seg_of_tile(i + tile0), 0, 0)),
            pl.BlockSpec((d, tf), lambda i, j: (0, j)),
            pl.BlockSpec((tf, d), lambda i, j: (j, 0)),
            pl.BlockSpec((1, d), lambda i, j: (0, 0)),
        ],
        out_specs=pl.BlockSpec((tm, d), lambda i, j: (i, 0)),
        out_shape=jax.ShapeDtypeStruct((n_tiles * tm, d), F32),
        scratch_shapes=[pltpu.VMEM((tm, d), BF16)],
        compiler_params=_cparams(("parallel", "arbitrary")),
    )(x, g.reshape(1, d), mods_l, w1_bf, w2_bf, fn.reshape(1, d))


def _proj_res_kernel(y_ref, x_ref, mod_ref, w_ref, o_ref, yt_ref):
    k = w_ref.shape[0]
    gr, gc, d = x_ref.shape
    nq = k // LANE
    for c in range(gc):
        for q in range(nq):
            yt_ref[q, pl.ds(c, gr, stride=gc), :] = (
                y_ref[:, c * k + q * LANE:c * k + (q + 1) * LANE])
    yt = jnp.concatenate([_bf(yt_ref[q]) for q in range(nq)], axis=1)
    out = x_ref[...].reshape(gr * gc, d) + mod_ref[2:3, :] * _dot(yt, w_ref[...])
    o_ref[...] = out.reshape(gr, gc, d)


def _proj_res(y_grid, x, n_lat, rows, mods_l, w_bf):
    k, d = w_bf.shape
    gr, gc = _grid_tile(rows)
    ncg = GRID_W // gc
    x3 = x.reshape(x.shape[0] // GRID_W, GRID_W, d)
    tok_spec = pl.BlockSpec((gr, gc, d), lambda i: (i // ncg, i % ncg, 0))
    out = pl.pallas_call(
        _proj_res_kernel,
        grid=(n_lat // ROW_TILE,),
        in_specs=[
            pl.BlockSpec((gr, gc * k), lambda i: (i // ncg, i % ncg)),
            tok_spec,
            pl.BlockSpec((None, N_MOD, d), lambda i: ((i // ncg) * gr // rows, 0, 0)),
            pl.BlockSpec((k, d), lambda i: (0, 0)),
        ],
        out_specs=tok_spec,
        out_shape=jax.ShapeDtypeStruct((n_lat // GRID_W, GRID_W, d), F32),
        scratch_shapes=[pltpu.VMEM((k // LANE, ROW_TILE, LANE), F32)],
        compiler_params=_cparams(("parallel",)),
    )(y_grid, x3, mods_l, w_bf)
    return out.reshape(n_lat, d)


def _head_sum(z, e_bf):
    parts = []
    for gi in range(z.shape[1] // GROUP_W):
        zs = z[:, gi * GROUP_W:(gi + 1) * GROUP_W]
        parts.append(_dot_exact_rhs(zs, e_bf))
    return jnp.concatenate(parts, axis=1)


def _rwkv_prep_kernel(p_ref, pp_ref, pn_ref, mu_ref, w0_ref, a0_ref, w2_ref, a2_ref, g2_ref,
                      kk_ref, ka_ref, rk_ref, e_ref,
                      r_o, v_o, kk_o, lw0_o, lw1_o, kd0_o, kd1_o, b0_o, b1_o, g_o, bonus_o,
                      *, n_lat_tiles, lat_tiles_per_seq, ctx_tiles_per_seq, aw):
    i = pl.program_id(0)
    tp = p_ref.shape[0]
    in_ctx = i >= n_lat_tiles
    pos = jnp.where(in_ctx, (i - n_lat_tiles) % ctx_tiles_per_seq, i % lat_tiles_per_seq)
    last = jnp.where(in_ctx, ctx_tiles_per_seq - 1, lat_tiles_per_seq - 1)
    is_first = pos == 0
    is_last = pos == last
    rowid = lax.broadcasted_iota(jnp.int32, (tp, 1), 0)

    def shifted(c0, c1):
        x = p_ref[:, c0:c1]
        prow = jnp.where(is_first, 0.0, pp_ref[SUBLANE - 1:SUBLANE, c0:c1])
        nrow = jnp.where(is_last, 0.0, pn_ref[0:1, c0:c1])
        xp = jnp.where(rowid == 0, prow, pltpu.roll(x, 1, 0))
        xn = jnp.where(rowid == tp - 1, nrow, pltpu.roll(x, tp - 1, 0))
        return x + mu_ref[0:1, c0:c1] * (xp - x) + mu_ref[1:2, c0:c1] * (xn - x)

    e_bf = e_ref[...]
    r = shifted(0, aw)
    k = shifted(aw, 2 * aw)
    v = shifted(2 * aw, 3 * aw)
    gd = shifted(3 * aw, 3 * aw + A_LORA_G)
    lo0 = 3 * aw + A_LORA_G
    xl = shifted(lo0, lo0 + 2 * A_LORA_W + 2 * A_LORA_A)

    r_o[...] = r
    v_o[...] = v
    g_o[...] = _dot(_bf(_sigmoid(gd)), g2_ref[...])

    kx = k * kk_ref[...]
    kn = jnp.sqrt(_head_sum(kx * kx, e_bf))
    kk = kx / jnp.maximum(kn, 1e-12)
    kk_o[...] = kk

    xl_t = _bf(jnp.tanh(xl))
    xl_b = _bf(xl)
    kd_sum = None
    for d, (lw_o, kd_o, b_o) in enumerate(((lw0_o, kd0_o, b0_o), (lw1_o, kd1_o, b1_o))):
        w_log = -_softplus(-(w0_ref[d:d + 1, :] + _dot(xl_t, w2_ref[d]))) - 0.5
        lw_o[...] = -jnp.exp(w_log)
        asig = _sigmoid(a0_ref[d:d + 1, :] + _dot(xl_b, a2_ref[d]))
        kd = k * (1.0 + (asig - 1.0) * ka_ref[...])
        kd_o[...] = kd
        b_o[...] = kk * asig
        kd_sum = kd if kd_sum is None else kd_sum + kd
    bonus_o[...] = _head_sum(r * rk_ref[...] * kd_sum, e_bf) * v


def _rwkv_prep(pa, mu_p, w0, a0, w2p, a2p, g2, k_k, k_a, r_k, e_bf, *, n_lat, seq, ctx_len):
    r = pa.shape[0]
    ncol = mu_p.shape[1]
    aw = w0.shape[1]
    tp = PREP_TILE
    hb = tp // SUBLANE
    nblk8 = r // SUBLANE
    lw = 2 * A_LORA_W + 2 * A_LORA_A
    row = lambda a: a.reshape(1, aw)
    full = lambda shape: pl.BlockSpec(shape, lambda i: (0,) * len(shape))
    out_spec = pl.BlockSpec((tp, aw), lambda i: (i, 0))
    out_sds = jax.ShapeDtypeStruct((r, aw), F32)
    return pl.pallas_call(
        functools.partial(_rwkv_prep_kernel, n_lat_tiles=n_lat // tp, lat_tiles_per_seq=seq // tp,
                          ctx_tiles_per_seq=ctx_len // tp, aw=aw),
        grid=(r // tp,),
        in_specs=[
            pl.BlockSpec((tp, ncol), lambda i: (i, 0)),
            pl.BlockSpec((SUBLANE, ncol), lambda i: (jnp.maximum(i * hb - 1, 0), 0)),
            pl.BlockSpec((SUBLANE, ncol), lambda i: (jnp.minimum((i + 1) * hb, nblk8 - 1), 0)),
            full((2, ncol)), full((2, aw)), full((2, aw)),
            full((2, lw, aw)), full((2, lw, aw)), full((A_LORA_G, aw)),
            full((1, aw)), full((1, aw)), full((1, aw)), full((GROUP_W, GROUP_W)),
        ],
        out_specs=[out_spec] * 11,
        out_shape=[out_sds] * 11,
        compiler_params=_cparams(("parallel",)),
    )(pa, pa, pa, mu_p, w0, a0, w2p, a2p, g2, row(k_k), row(k_a), row(r_k), e_bf)


def _rwkv_masks(reverse):
    L = CHUNK
    gw = GROUP_W
    row = lax.broadcasted_iota(jnp.int32, (gw, gw), 0)
    col = lax.broadcasted_iota(jnp.int32, (gw, gw), 1)
    same_head = (row >> 6) == (col >> 6)
    tr = row & (L - 1)
    tc = col & (L - 1)
    strict = same_head & ((tr < tc) if reverse else (tr > tc))
    incl = same_head & ((tr <= tc) if reverse else (tr >= tc))
    t_r = lax.broadcasted_iota(jnp.int32, (L, L), 0)
    t_c = lax.broadcasted_iota(jnp.int32, (L, L), 1)
    tri = _bf(jnp.where((t_c >= t_r) if reverse else (t_c <= t_r), 1.0, 0.0))
    return dict(same_head=same_head, strict=strict, incl=incl, eye=row == col, tri=tri)


def _rwkv_chunks(probs, side=None):
    L = CHUNK
    gw = GROUP_W
    tick = (lambda: None) if side is None else (lambda: next(side, None))
    masks = {rv: _rwkv_masks(rv) for rv in sorted({p[7] for p in probs})}
    mk = [masks[p[7]] for p in probs]
    each = lambda f, *cols: [f(*args) for args in zip(*cols)]
    rep = lambda x: jnp.concatenate([x] * HEADS_PER_GROUP, axis=0)
    nat = lambda x: x[0:L] + x[L:2 * L] + x[2 * L:3 * L] + x[3 * L:4 * L]

    c = [_dot_exact_lhs(m["tri"], p[3]) for m, p in zip(mk, probs)]
    c_l = [ci[(0 if p[7] else L - 1):(1 if p[7] else L), :] for ci, p in zip(c, probs)]

    at_bd, rt_bd, v_bf, bend_bd, kend_bd, lhs, rhs, e_l = [], [], [], [], [], [], [], []
    for (r, v, kk, lw, kd, b, _, _), m, ci, cl in zip(probs, mk, c, c_l):
        bd = lambda x, sh=m["same_head"]: jnp.where(sh, rep(x), 0.0)
        e_nc = jnp.exp(-ci)
        e_end = jnp.exp(cl - ci)
        a_ = bd(-kk * jnp.exp(ci - lw))
        r_ = bd(r * jnp.exp(ci))
        at_bd.append(a_)
        rt_bd.append(r_)
        v_bf.append(_bf(bd(v)))
        bend_bd.append(_bf(bd(b * e_end)))
        kend_bd.append(_bf(bd(kd * e_end)))
        lhs.append(_bf(jnp.concatenate([a_, r_], axis=0)))
        rhs.append(_bf(jnp.concatenate([rep(b * e_nc), rep(kd * e_nc)], axis=0)))
        e_l.append(jnp.exp(cl))

    s = each(_dot_nt, lhs, rhs)
    a_ab = [jnp.where(m["strict"], si[:gw, :gw], 0.0) for m, si in zip(mk, s)]
    a_ak = [_bf(jnp.where(m["strict"], si[:gw, gw:], 0.0)) for m, si in zip(mk, s)]
    r_b = [_bf(jnp.where(m["incl"], si[gw:, :gw], 0.0)) for m, si in zip(mk, s)]
    r_k = [_bf(jnp.where(m["incl"], si[gw:, gw:], 0.0)) for m, si in zip(mk, s)]

    tick()
    akv = each(_dot, a_ak, v_bf)
    t = [jnp.where(m["eye"], 1.0, 0.0) + a for m, a in zip(mk, a_ab)]
    a_pow = a_ab
    for _ in range(5):
        a_pow_bf = [_bf(a) for a in a_pow]
        a_pow = each(_dot, a_pow_bf, a_pow_bf)
        tick()
        t = [ti + _dot(_bf(ti), _bf(ap)) for ti, ap in zip(t, a_pow)]
    h_bf = [_bf(p[6]) for p in probs]
    x = [_dot(_bf(a_), h) + ak for a_, h, ak in zip(at_bd, h_bf, akv)]
    tick()
    u = [_dot(_bf(ti), _bf(xi)) for ti, xi in zip(t, x)]
    tick()
    uv = [jnp.concatenate([_bf(ui), v], axis=0) for ui, v in zip(u, v_bf)]
    y = [_dot(_bf(nat(r_)), h) + nat(_dot(jnp.concatenate([rb, rk], axis=1), uvi))
         for r_, h, rb, rk, uvi in zip(rt_bd, h_bf, r_b, r_k, uv)]
    e_col = [jnp.broadcast_to(el, (gw, gw)).T for el in e_l]
    h_new = [ec * p[6] + _dot_tn(jnp.concatenate([be, ke], axis=0), uvi)
             for ec, p, be, ke, uvi in zip(e_col, probs, bend_bd, kend_bd, uv)]
    if side is not None:
        for _ in side:
            pass
    return list(zip(y, h_new))


def _chunk_row_block(b, i, *, reverse, n_lat, seq, ctx_len):
    nc = ctx_len // CHUNK
    nl = seq // CHUNK
    ci = (nc - 1 - i) if reverse else i
    li = (nl - 1 - (i - nc)) if reverse else (i - nc)
    return jnp.where(i < nc, (n_lat + b * ctx_len) // CHUNK + ci, b * nl + li)


def _gla_stages(qkf, vf, gdf, qkb, vb, gdb, gw2_ref, gb_ref, of_ref, ob_ref, sf_ref, sb_ref):
    dv, dk = sf_ref.shape[1:]
    L = CHUNK
    wk = B_HEADS * dk
    t_r = lax.broadcasted_iota(jnp.int32, (L, L), 0)
    t_c = lax.broadcasted_iota(jnp.int32, (L, L), 1)

    qh, kh, keh, vh, dech, keeph, sth = [], [], [], [], [], [], []
    for d, (qk_ref, v_ref, gd_ref, st_ref) in enumerate(((qkf, vf, gdf, sf_ref),
                                                         (qkb, vb, gdb, sb_ref))):
        reverse = d == 1
        keep = (t_c >= t_r) if reverse else (t_c <= t_r)
        tri = _bf(jnp.where(keep, 1.0, 0.0))
        z = _dot(_bf(gd_ref[...]), gw2_ref[d]) + gb_ref[d:d + 1, :]
        la = -_softplus(-z) * (1.0 / B_GATE_TAU)
        cum = _dot_exact_lhs(tri, la)
        last = 0 if reverse else L - 1
        cl = cum[last:last + 1, :]
        q_in = _bf(qk_ref[:, :wk] * (dk ** -0.5) * jnp.exp(cum))
        k_in = _bf(qk_ref[:, wk:] * jnp.exp(-cum))
        k_end = _bf(qk_ref[:, wk:] * jnp.exp(cl - cum))
        dec = jnp.exp(cl)
        for h in range(B_HEADS):
            ks = slice(h * dk, (h + 1) * dk)
            qh.append(q_in[:, ks])
            kh.append(k_in[:, ks])
            keh.append(k_end[:, ks])
            vh.append(_bf(v_ref[:, h * dv:(h + 1) * dv]))
            dech.append(dec[:, ks])
            keeph.append(keep)
            sth.append(st_ref[h])
        yield

    sc = [_bf(jnp.where(kp, _dot_nt(q, k), 0.0)) for kp, q, k in zip(keeph, qh, kh)]
    yield
    inter = [_dot_nt(q, _bf(st)) for q, st in zip(qh, sth)]
    yield
    kvt = [_dot_tn(v, ke) for v, ke in zip(vh, keh)]
    yield
    out = [_dot(s_, v) + it for s_, v, it in zip(sc, vh, inter)]
    yield
    for d, (o_ref, st_ref) in enumerate(((of_ref, sf_ref), (ob_ref, sb_ref))):
        for h in range(B_HEADS):
            i = d * B_HEADS + h
            o_ref[:, h * dv:(h + 1) * dv] = out[i]
            st_ref[h] = sth[i] * dech[i] + kvt[i]


def _mix_scan_kernel(rf, vf, kkf, lwf, kdf, bf_, rb, vb, kkb, lwb, kdb, bb,
                     qkf, gvf, gdf, qkb, gvb, gdb, gw2_ref, gb_ref,
                     yf_ref, yb_ref, of_ref, ob_ref, hf_ref, hb_ref, sf_ref, sb_ref):
    @pl.when(pl.program_id(1) == 0)
    def _():
        for ref in (hf_ref, hb_ref, sf_ref, sb_ref):
            ref[...] = jnp.zeros_like(ref)

    gs = hf_ref.shape[0]
    probs = []
    for gi in range(gs):
        sl = slice(gi * GROUP_W, (gi + 1) * GROUP_W)
        probs.append((rf[:, sl], vf[:, sl], kkf[:, sl], lwf[:, sl], kdf[:, sl], bf_[:, sl],
                      hf_ref[gi], False))
        probs.append((rb[:, sl], vb[:, sl], kkb[:, sl], lwb[:, sl], kdb[:, sl], bb[:, sl],
                      hb_ref[gi], True))
    gla = _gla_stages(qkf, gvf, gdf, qkb, gvb, gdb, gw2_ref, gb_ref, of_ref, ob_ref, sf_ref,
                      sb_ref)
    res = _rwkv_chunks(probs, side=gla)
    for gi in range(gs):
        sl = slice(gi * GROUP_W, (gi + 1) * GROUP_W)
        yf_ref[:, sl], hf_ref[gi] = res[2 * gi]
        yb_ref[:, sl], hb_ref[gi] = res[2 * gi + 1]


def _mix_scan(r, v, kk, lw0, lw1, kd0, kd1, b0, b1, pb, gw2p, gb, *, batch, n_lat, seq, ctx_len,
              wk, wv, col0):
    rows, aw = r.shape
    assert aw == GROUP_W * RWKV_GROUPS_PER_STEP
    dk, dv = wk // B_HEADS, wv // B_HEADS
    geo = dict(n_lat=n_lat, seq=seq, ctx_len=ctx_len)
    gd_blk = (col0 + 2 * wk + 2 * wv) // LANE

    def specs(reverse):
        rb = lambda b, i: _chunk_row_block(b, i, reverse=reverse, **geo)
        rw = pl.BlockSpec((CHUNK, aw), lambda b, i: (rb(b, i), 0))
        gla = [
            pl.BlockSpec((CHUNK, 2 * wk), lambda b, i: (rb(b, i), col0 // (2 * wk))),
            pl.BlockSpec((CHUNK, wv), lambda b, i: (rb(b, i), (col0 + 2 * wk) // wv)),
            pl.BlockSpec((CHUNK, LANE), lambda b, i: (rb(b, i), gd_blk)),
        ]
        out = pl.BlockSpec((CHUNK, wv), lambda b, i: (rb(b, i), 0))
        return rw, gla, out

    rwf, glaf, outf = specs(False)
    rwb, glab, outb = specs(True)
    sds_a = jax.ShapeDtypeStruct((rows, aw), F32)
    sds_b = jax.ShapeDtypeStruct((rows, wv), F32)
    gs = RWKV_GROUPS_PER_STEP
    return pl.pallas_call(
        _mix_scan_kernel,
        grid=(batch, (seq + ctx_len) // CHUNK),
        in_specs=[rwf] * 6 + [rwb] * 6 + glaf + glab + [
            pl.BlockSpec((2, LANE, wk), lambda b, i: (0, 0, 0)),
            pl.BlockSpec((2, wk), lambda b, i: (0, 0)),
        ],
        out_specs=[rwf, rwb, outf, outb],
        out_shape=[sds_a, sds_a, sds_b, sds_b],
        scratch_shapes=[pltpu.VMEM((gs, GROUP_W, GROUP_W), F32),
                        pltpu.VMEM((gs, GROUP_W, GROUP_W), F32),
                        pltpu.VMEM((B_HEADS, dv, dk), F32), pltpu.VMEM((B_HEADS, dv, dk), F32)],
        compiler_params=_cparams(("parallel", "arbitrary")),
    )(r, v, kk, lw0, kd0, b0, r, v, kk, lw1, kd1, b1, pb, pb, pb, pb, pb, pb, gw2p, gb)


def _mixer_out_kernel(yf, yb, bonus, g, of, ob, gate, xa_ref, xb_ref, mod_ref, lnw, lnb, gn, e_ref,
                      w_ref, o_ref, h_ref, *, aw, dv, n_a_tiles):
    e_bf = e_ref[...]
    y = yf[...] + yb[...]
    inv = 1.0 / A_HEAD_DIM
    mean = _head_sum(y, e_bf) * inv
    dlt = y - mean
    var = _head_sum(dlt * dlt, e_bf) * inv
    yn = dlt * lax.rsqrt(var + A_LN_EPS) * lnw[...] + lnb[...]
    h_ref[:, :aw] = _bf((yn + bonus[...]) * g[...])

    o = of[...] + ob[...]
    gt = _silu(gate[...])
    for h in range(B_HEADS):
        sl = slice(h * dv, (h + 1) * dv)
        oh = o[:, sl]
        ms = jnp.mean(oh * oh, axis=-1, keepdims=True)
        h_ref[:, aw + h * dv:aw + (h + 1) * dv] = _bf(oh * lax.rsqrt(ms + NORM_EPS) * gn[...] * gt[:, sl])

    x = jnp.where(pl.program_id(0) < n_a_tiles, xa_ref[...], xb_ref[...])
    o_ref[...] = x + mod_ref[2:3, :] * _dot(h_ref[...], w_ref[...])


def _mixer_out(yf, yb, bonus, g, of, ob, pb, xa, xb, mods_l, ln_w, ln_b, gla_norm, e_bf, w_bf,
               seg_of_tile, *, wk, col0):
    d = xa.shape[1]
    rows = xa.shape[0] + xb.shape[0]
    aw = yf.shape[1]
    wv = of.shape[1]
    dv = wv // B_HEADS
    tm = 256
    tiles_per_row_tile = ROW_TILE // tm
    n_a_tiles = xa.shape[0] // tm
    rs = lambda w: pl.BlockSpec((tm, w), lambda i: (i, 0))
    full = lambda shape: pl.BlockSpec(shape, lambda i: (0,) * len(shape))
    return pl.pallas_call(
        functools.partial(_mixer_out_kernel, aw=aw, dv=dv, n_a_tiles=n_a_tiles),
        grid=(rows // tm,),
        in_specs=[
            rs(aw), rs(aw), rs(aw), rs(aw), rs(wv), rs(wv),
            pl.BlockSpec((tm, wv), lambda i: (i, (col0 + 2 * wk + wv) // wv)),
        ] + _two_source_specs(tm, d, n_a_tiles) + [
            pl.BlockSpec((None, N_MOD, d), lambda i: (seg_of_tile(i // tiles_per_row_tile), 0, 0)),
            full((1, aw)), full((1, aw)), full((1, dv)), full((GROUP_W, GROUP_W)),
            full((aw + wv, d)),
        ],
        out_specs=rs(d),
        out_shape=jax.ShapeDtypeStruct((rows, d), F32),
        scratch_shapes=[pltpu.VMEM((tm, aw + wv), BF16)],
        compiler_params=_cparams(("parallel",)),
    )(yf, yb, bonus, g, of, ob, pb, xa, xb, mods_l, ln_w.reshape(1, aw), ln_b.reshape(1, aw),
      gla_norm.reshape(1, dv), e_bf, w_bf)


def _lru_kernel(*refs, reverse, n_blk, combine):
    if combine:
        (x_ref, xp_ref, xn_ref, h0_ref, cw_ref, cb_ref, wc_ref, ba_ref, bx_ref, lam_ref,
         hf_ref, gate_ref, out_ref, hl_ref, a_sc, u_sc, hs_sc, carry_sc) = refs
    else:
        (x_ref, xp_ref, xn_ref, h0_ref, cw_ref, cb_ref, wc_ref, ba_ref, bx_ref, lam_ref,
         out_ref, hl_ref, a_sc, u_sc, hs_sc, carry_sc) = refs
    j = pl.program_id(2)
    blk = (n_blk - 1 - j) if reverse else j
    lb, ct = x_ref.shape
    cblk = wc_ref.shape[1]
    is_first = blk == 0
    is_last = blk == n_blk - 1
    rowid = lax.broadcasted_iota(jnp.int32, (lb, 1), 0)

    x = x_ref[...]
    p6 = jnp.where(is_first, 0.0, xp_ref[SUBLANE - 2:SUBLANE - 1, :])
    p7 = jnp.where(is_first, 0.0, xp_ref[SUBLANE - 1:SUBLANE, :])
    n0 = jnp.where(is_last, 0.0, xn_ref[0:1, :])
    xm1 = jnp.where(rowid == 0, p7, pltpu.roll(x, 1, 0))
    xm2 = jnp.where(rowid == 0, p6, jnp.where(rowid == 1, p7, pltpu.roll(x, 2, 0)))
    xp1 = jnp.where(rowid == lb - 1, n0, pltpu.roll(x, lb - 1, 0))
    xs = (cb_ref[...] + cw_ref[0:1, :] * xm2 + cw_ref[1:2, :] * xm1 + cw_ref[2:3, :] * x
          + cw_ref[3:4, :] * xp1)

    for n in range(ct // cblk):
        sl = slice(n * cblk, (n + 1) * cblk)
        xb = xs[:, sl]
        ri = _dot(_bf(xb), wc_ref[n])
        rg = _sigmoid(ri[:, :cblk] + ba_ref[:, sl])
        ig = _sigmoid(ri[:, cblk:] + bx_ref[:, sl])
        log_a = -C_CONST * rg * _softplus(-lam_ref[:, sl])
        a = jnp.exp(log_a)
        a_sc[:, sl] = a
        u_sc[:, sl] = jnp.sqrt(1.0 - a * a) * (ig * xb)

    @pl.when(j == 0)
    def _():
        carry_sc[...] = h0_ref[0:1, :]

    def step(t, h):
        tt = (lb - 1 - t) if reverse else t
        h = a_sc[pl.ds(tt, 1), :] * h + u_sc[pl.ds(tt, 1), :]
        hs_sc[pl.ds(tt, 1), :] = h
        return h

    h = lax.fori_loop(0, lb, step, carry_sc[...], unroll=8)
    carry_sc[...] = h
    hl_ref[...] = jnp.broadcast_to(h, hl_ref.shape)
    if combine:
        out_ref[...] = (hs_sc[...] + hf_ref[...]) * jax.nn.gelu(gate_ref[...])
    else:
        out_ref[...] = hs_sc[...]


def _lru_pass(x_arr, x_specs, h0, conv_w, conv_b, wcat, ba, bx, lam, out_shape, out_spec, extra,
              *, batch, n_blk, lb, reverse):
    c = conv_w.shape[1]
    ct = 1024
    nct = c // ct
    cblk = wcat.shape[1]
    chan = lambda rows_: pl.BlockSpec((rows_, ct), lambda b, k, j: (0, k))
    in_specs = list(x_specs) + [
        pl.BlockSpec((SUBLANE, ct), lambda b, k, j: (b, k)),
        chan(C_CONV), chan(1),
        pl.BlockSpec((ct // cblk, cblk, 2 * cblk), lambda b, k, j: (k, 0, 0)),
        chan(1), chan(1), chan(1),
    ] + [s for _, s in extra]
    args = [x_arr, x_arr, x_arr, h0, conv_w, conv_b.reshape(1, c), wcat, ba.reshape(1, c),
            bx.reshape(1, c), lam.reshape(1, c)] + [a for a, _ in extra]
    return pl.pallas_call(
        functools.partial(_lru_kernel, reverse=reverse, n_blk=n_blk, combine=bool(extra)),
        grid=(batch, nct, n_blk),
        in_specs=in_specs,
        out_specs=[out_spec, pl.BlockSpec((SUBLANE, ct), lambda b, k, j: (b, k))],
        out_shape=[out_shape, jax.ShapeDtypeStruct((batch * SUBLANE, c), F32)],
        scratch_shapes=[pltpu.VMEM((lb, ct), F32), pltpu.VMEM((lb, ct), F32),
                        pltpu.VMEM((lb, ct), F32), pltpu.VMEM((1, ct), F32)],
        compiler_params=_cparams(("parallel", "parallel", "arbitrary")),
    )(*args)


def _rglru(hw_ctx, gate_grid, x_grid, conv_w, conv_b, wa, ba, wx, bx, lam, *, batch, n_lat, seq,
           ctx_len):
    c = conv_w.shape[1]
    ct = 1024
    rows = seq // GRID_W
    lbc = 128
    ncb = ctx_len // lbc
    xcol = c // ct
    per_col = c // ct

    wcat = [_bf(jnp.concatenate([wa[d], wx[d]], axis=-1)) for d in range(2)]
    zeros_state = jnp.zeros((batch * SUBLANE, c), F32)

    def ctx_specs(reverse):
        blk = (lambda j: ncb - 1 - j) if reverse else (lambda j: j)
        per8 = lbc // SUBLANE
        cur = pl.BlockSpec((lbc, ct), lambda b, k, j: (b * ncb + blk(j), xcol + k))
        prv = pl.BlockSpec((SUBLANE, ct), lambda b, k, j: (
            b * ncb * per8 + jnp.maximum(blk(j) * per8 - 1, 0), xcol + k))
        nxt = pl.BlockSpec((SUBLANE, ct), lambda b, k, j: (
            b * ncb * per8 + jnp.minimum((blk(j) + 1) * per8, ncb * per8 - 1), xcol + k))
        return cur, prv, nxt

    def lat_specs(reverse):
        blk = (lambda j: GRID_W - 1 - j) if reverse else (lambda j: j)
        r8 = rows // SUBLANE
        cur = pl.BlockSpec((rows, ct), lambda b, k, j: (b, blk(j) * per_col + k))
        prv = pl.BlockSpec((SUBLANE, ct), lambda b, k, j: (
            b * r8 + r8 - 1, jnp.maximum(blk(j) - 1, 0) * per_col + k))
        nxt = pl.BlockSpec((SUBLANE, ct), lambda b, k, j: (
            b * r8, jnp.minimum(blk(j) + 1, GRID_W - 1) * per_col + k))
        return cur, prv, nxt

    common = dict(batch=batch)
    ctx_scratch_sds = jax.ShapeDtypeStruct((batch * ctx_len, c), F32)
    ctx_out = lambda reverse: pl.BlockSpec(
        (lbc, ct), lambda b, k, j: (b * ncb + ((ncb - 1 - j) if reverse else j), k))
    _, st = _lru_pass(hw_ctx, ctx_specs(False), zeros_state, conv_w, conv_b, wcat[0], ba[0], bx[0],
                      lam[0], ctx_scratch_sds, ctx_out(False), [], n_blk=ncb, lb=lbc,
                      reverse=False, **common)
    hf_sds = jax.ShapeDtypeStruct((n_lat, c), F32)
    hf_spec = lambda reverse: pl.BlockSpec(
        (rows, ct), lambda b, k, j: (b * GRID_W + ((GRID_W - 1 - j) if reverse else j), k))
    hf, _ = _lru_pass(x_grid, lat_specs(False), st, conv_w, conv_b, wcat[0], ba[0], bx[0], lam[0],
                      hf_sds, hf_spec(False), [], n_blk=GRID_W, lb=rows, reverse=False, **common)
    _, st = _lru_pass(hw_ctx, ctx_specs(True), zeros_state, conv_w, conv_b, wcat[1], ba[1], bx[1],
                      lam[1], ctx_scratch_sds, ctx_out(True), [], n_blk=ncb, lb=lbc,
                      reverse=True, **common)
    grid_spec = pl.BlockSpec((rows, ct), lambda b, k, j: (b, (GRID_W - 1 - j) * per_col + k))
    y_sds = jax.ShapeDtypeStruct((n_lat // GRID_W, GRID_W * c), F32)
    y, _ = _lru_pass(x_grid, lat_specs(True), st, conv_w, conv_b, wcat[1], ba[1], bx[1], lam[1],
                     y_sds, grid_spec, [(hf, hf_spec(True)), (gate_grid, grid_spec)],
                     n_blk=GRID_W, lb=rows, reverse=True, **common)
    return y


def _pad_cols(w, n):
    return jnp.pad(w, ((0, 0),) * (w.ndim - 1) + ((0, n - w.shape[-1]),))


def kernel(x, c, ctx, c_ctx, mod_w, mod_b, norm1, norm2, mlp_w1, mlp_w2, ab_w_in, ab_w_out, rw_mu,
           rw_w0, rw_w2, rw_a0, rw_a2, rw_g2, rw_kk, rw_ka, rw_rk, rw_ln_w, rw_ln_b, gla_gw2,
           gla_gb, gla_norm, lru_w_in, lru_w_out, lru_conv_w, lru_conv_b, lru_wa, lru_ba, lru_wx,
           lru_bx, lru_lam, final_norm):
    batch, seq, d = x.shape
    ctx_len = ctx.shape[1]
    depth = mod_w.shape[0]
    n_lat = batch * seq
    aw = rw_w0.shape[-1]
    wk = gla_gb.shape[-1]
    wv = ab_w_out.shape[1] - aw
    a_cols = rw_mu.shape[-1]
    b_cols = ab_w_in.shape[-1] - a_cols
    assert depth == 2 and batch + 1 <= SUBLANE
    assert seq % ROW_TILE == 0 and (batch * ctx_len) % ROW_TILE == 0
    assert ctx_len % PREP_TILE == 0 and ctx_len % 128 == 0 and seq % (GRID_W * SUBLANE) == 0
    assert aw % GROUP_W == 0 and wv == aw and 2 * wk == wv

    tiles_per_seq = seq // ROW_TILE
    seg_of_tile = lambda i: jnp.minimum(i // tiles_per_seq, batch)

    x2 = x.reshape(n_lat, d)
    ctx2 = ctx.reshape(batch * ctx_len, d)
    cvec = jnp.concatenate(
        [c, c_ctx[None, :], jnp.zeros((SUBLANE - batch - 1, d), F32)], axis=0)
    mods = _mods(cvec, mod_w, mod_b)

    hid = jnp.arange(GROUP_W) // A_HEAD_DIM
    e_bf = _bf(hid[:, None] == hid[None, :])

    a_pad = -(-a_cols // LANE) * LANE
    col0 = -(-a_cols // (2 * wk)) * (2 * wk)
    n_tiles = 3
    tn = -(-(col0 + b_cols) // (n_tiles * LANE)) * LANE
    w_ab = _bf(jnp.concatenate(
        [_pad_cols(ab_w_in[0][:, :a_cols], col0),
         _pad_cols(ab_w_in[0][:, a_cols:], n_tiles * tn - col0)], axis=1))
    pa = pb = _norm_matmul(x2, ctx2, norm1[0], mods[0], w_ab, seg_of_tile, tn=tn)

    lora_w = 2 * A_LORA_W + 2 * A_LORA_A
    w2p = jnp.zeros((2, lora_w, aw), F32)
    a2p = jnp.zeros((2, lora_w, aw), F32)
    for dd in range(2):
        w2p = w2p.at[dd, dd * A_LORA_W:(dd + 1) * A_LORA_W].set(rw_w2[0, dd])
        a2p = a2p.at[dd, 2 * A_LORA_W + dd * A_LORA_A:2 * A_LORA_W + (dd + 1) * A_LORA_A].set(
            rw_a2[0, dd])
    (r_, v_, kk_, lw0, lw1, kd0, kd1, b0, b1, g_, bonus) = _rwkv_prep(
        pa, _pad_cols(rw_mu[0], a_pad), rw_w0[0], rw_a0[0], _bf(w2p), _bf(a2p), _bf(rw_g2[0]),
        rw_kk[0], rw_ka[0], rw_rk[0].reshape(-1), e_bf, n_lat=n_lat, seq=seq, ctx_len=ctx_len)
    gw2p = jnp.zeros((2, LANE, wk), F32)
    for dd in range(2):
        gw2p = gw2p.at[dd, dd * B_LORA:(dd + 1) * B_LORA].set(gla_gw2[0, dd])
    yf, yb, of, ob = _mix_scan(r_, v_, kk_, lw0, lw1, kd0, kd1, b0, b1, pb, _bf(gw2p), gla_gb[0],
                               batch=batch, n_lat=n_lat, seq=seq, ctx_len=ctx_len, wk=wk, wv=wv,
                               col0=col0)

    xs = _mixer_out(yf, yb, bonus, g_, of, ob, pb, x2, ctx2, mods[0], rw_ln_w[0], rw_ln_b[0],
                    gla_norm[0], e_bf, _bf(ab_w_out[0]), seg_of_tile, wk=wk, col0=col0)
    seg_of_mlp_tile = lambda i: jnp.minimum(i // (seq // MLP_TILE), batch)
    ctx_tiles = (batch * ctx_len) // ROW_TILE
    w1_bf, w2_bf = _bf(mlp_w1[0]), _bf(mlp_w2[0])
    x_lat = _mlp(xs, norm2[0], mods[0], w1_bf, w2_bf, final_norm, seg_of_mlp_tile, tm=MLP_TILE,
                 tile0=0, n_tiles=n_lat // MLP_TILE, final_norm=False)
    x_ctx = _mlp(xs, norm2[0], mods[0], w1_bf, w2_bf, final_norm, seg_of_tile, tm=ROW_TILE,
                 tile0=n_lat // ROW_TILE, n_tiles=ctx_tiles, final_norm=False)

    w_lru = _bf(lru_w_in[0])
    hw_ctx = _norm_matmul(x_ctx, None, norm1[1], mods[1], w_lru, lambda i: batch, tn=2048)
    gate_grid, x_grid = _norm_matmul_grid(x_lat, n_lat, seq // GRID_W, norm1[1], mods[1], w_lru)
    y = _rglru(hw_ctx, gate_grid, x_grid, lru_conv_w[0], lru_conv_b[0], lru_wa[0], lru_ba[0],
               lru_wx[0], lru_bx[0], lru_lam[0], batch=batch, n_lat=n_lat, seq=seq,
               ctx_len=ctx_len)
    xl = _proj_res(y, x_lat, n_lat, seq // GRID_W, mods[1], _bf(lru_w_out[0]))
    out = _mlp(xl, norm2[1], mods[1], _bf(mlp_w1[1]), _bf(mlp_w2[1]), final_norm, seg_of_mlp_tile,
               tm=MLP_TILE, tile0=0, n_tiles=n_lat // MLP_TILE, final_norm=True)
    return out.reshape(batch, seq, d)
```

```python
import functools

import jax
import jax.numpy as jnp
from jax import lax
from jax.experimental import pallas as pl
from jax.experimental.pallas import tpu as pltpu

F32 = jnp.float32
BF16 = jnp.bfloat16

NORM_EPS = 1e-6
GRID_W = 64
N_MOD = 6

A_HEAD_DIM = 64
A_LORA_W = 96
A_LORA_A = 96
A_LORA_G = 256
A_LN_EPS = 64e-5
HEADS_PER_GROUP = 4
GROUP_W = HEADS_PER_GROUP * A_HEAD_DIM
CHUNK = 64
RWKV_GROUPS_PER_STEP = 4

B_HEADS = 4
B_LORA = 16
B_GATE_TAU = 16.0

C_BLOCKS = 8
C_CONV = 4
C_CONST = 8.0

LANE = 128
SUBLANE = 8
ROW_TILE = 512
MLP_TILE = 1024
MLP_STEP_ELEMS = 1024 * 256
PREP_TILE = 128
VMEM_LIMIT = 56 * 1024 * 1024


def _cparams(sem):
    return pltpu.CompilerParams(dimension_semantics=sem, vmem_limit_bytes=VMEM_LIMIT)


def _bf(x):
    return x.astype(BF16)


def _dot(a, b):
    return jnp.dot(a, b, preferred_element_type=F32)


def _dot_nt(a, b):
    return lax.dot_general(a, b, (((1,), (1,)), ((), ())), preferred_element_type=F32)


def _dot_tn(a, b):
    return lax.dot_general(a, b, (((0,), (0,)), ((), ())), preferred_element_type=F32)


def _split3(x):
    hi = _bf(x)
    r1 = x - hi.astype(F32)
    mid = _bf(r1)
    lo = _bf(r1 - mid.astype(F32))
    return hi, mid, lo


def _dot_exact_lhs(m_bf, x):
    hi, mid, lo = _split3(x)
    return _dot(m_bf, hi) + _dot(m_bf, mid) + _dot(m_bf, lo)


def _dot_exact_rhs(x, m_bf):
    hi, mid, lo = _split3(x)
    return _dot(hi, m_bf) + _dot(mid, m_bf) + _dot(lo, m_bf)


def _softplus(x):
    return jnp.maximum(x, 0.0) + jnp.log(1.0 + jnp.exp(-jnp.abs(x)))


def _sigmoid(x):
    return 0.5 * jnp.tanh(0.5 * x) + 0.5


def _silu(x):
    return x * _sigmoid(x)


def _norm_mod(x, g, shift, scale):
    ms = jnp.mean(x * x, axis=-1, keepdims=True)
    return (x * lax.rsqrt(ms + NORM_EPS) * g) * (1.0 + scale) + shift


def _mods_kernel(c_ref, w_ref, b_ref, o_ref):
    s = _silu(c_ref[...])
    o_ref[...] = _dot(_bf(s), _bf(w_ref[...])) + b_ref[...]


def _mods(cvec, mod_w, mod_b):
    depth, d, n = mod_w.shape
    tn = 1024
    out = pl.pallas_call(
        _mods_kernel,
        grid=(depth, n // tn),
        in_specs=[
            pl.BlockSpec((SUBLANE, d), lambda l, j: (0, 0)),
            pl.BlockSpec((None, d, tn), lambda l, j: (l, 0, j)),
            pl.BlockSpec((None, 1, tn), lambda l, j: (l, 0, j)),
        ],
        out_specs=pl.BlockSpec((None, SUBLANE, tn), lambda l, j: (l, 0, j)),
        out_shape=jax.ShapeDtypeStruct((depth, SUBLANE, n), F32),
        compiler_params=_cparams(("parallel", "parallel")),
    )(cvec, mod_w, mod_b.reshape(depth, 1, n))
    return out.reshape(depth, SUBLANE, N_MOD, d)


def _norm_matmul_kernel(xa_ref, xb_ref, g_ref, mod_ref, w_ref, o_ref, h_ref, *, n_a_tiles):
    i = pl.program_id(0)

    @pl.when(pl.program_id(1) == 0)
    def _():
        x = jnp.where(i < n_a_tiles, xa_ref[...], xb_ref[...])
        h_ref[...] = _bf(_norm_mod(x, g_ref[...], mod_ref[0:1, :], mod_ref[1:2, :]))

    o_ref[...] = _dot(h_ref[...], w_ref[...])


def _two_source_specs(tm, d, n_a_tiles):
    return [pl.BlockSpec((tm, d), lambda i, *_: (jnp.minimum(i, n_a_tiles - 1), 0)),
            pl.BlockSpec((tm, d), lambda i, *_: (jnp.maximum(i - n_a_tiles, 0), 0))]


def _norm_matmul(xa, xb, g, mods_l, w_bf, seg_of_tile, *, tn):
    d = xa.shape[1]
    n = w_bf.shape[1]
    tm = ROW_TILE
    n_a_tiles = xa.shape[0] // tm
    n_tiles = n_a_tiles + (0 if xb is None else xb.shape[0] // tm)
    return pl.pallas_call(
        functools.partial(_norm_matmul_kernel, n_a_tiles=n_a_tiles),
        grid=(n_tiles, n // tn),
        in_specs=_two_source_specs(tm, d, n_a_tiles) + [
            pl.BlockSpec((1, d), lambda i, j: (0, 0)),
            pl.BlockSpec((None, N_MOD, d), lambda i, j: (seg_of_tile(i), 0, 0)),
            pl.BlockSpec((d, tn), lambda i, j: (0, j)),
        ],
        out_specs=pl.BlockSpec((tm, tn), lambda i, j: (i, j)),
        out_shape=jax.ShapeDtypeStruct((n_tiles * tm, n), F32),
        scratch_shapes=[pltpu.VMEM((tm, d), BF16)],
        compiler_params=_cparams(("parallel", "arbitrary")),
    )(xa, xa if xb is None else xb, g.reshape(1, d), mods_l, w_bf)


def _grid_tile(rows):
    gr = min(rows, 64)
    gc = ROW_TILE // gr
    assert gc % SUBLANE == 0 and GRID_W % gc == 0 and rows % gr == 0
    return gr, gc


def _norm_matmul_grid_kernel(x_ref, g_ref, mod_ref, w_ref, o0_ref, o1_ref, h_ref, res_ref):
    j = pl.program_id(1)
    gr, gc, d = x_ref.shape

    @pl.when(j == 0)
    def _():
        h = _norm_mod(x_ref[...].reshape(gr * gc, d), g_ref[...], mod_ref[0:1, :], mod_ref[1:2, :])
        h_ref[...] = _bf(h)

    res = _dot(h_ref[...], w_ref[...])
    nh = w_ref.shape[1]
    nq = nh // LANE
    for q in range(nq):
        res_ref[q] = res[:, q * LANE:(q + 1) * LANE]

    def scatter(o_ref):
        for c in range(gc):
            for q in range(nq):
                o_ref[:, c * nh + q * LANE:c * nh + (q + 1) * LANE] = (
                    res_ref[q, pl.ds(c, gr, stride=gc), :])

    @pl.when(j == 0)
    def _():
        scatter(o0_ref)

    @pl.when(j == 1)
    def _():
        scatter(o1_ref)


def _norm_matmul_grid(x, n_lat, rows, g, mods_l, w_bf):
    d = x.shape[1]
    nh = w_bf.shape[1] // 2
    gr, gc = _grid_tile(rows)
    ncg = GRID_W // gc
    x3 = x.reshape(x.shape[0] // GRID_W, GRID_W, d)
    out_spec = pl.BlockSpec((gr, gc * nh), lambda i, j: (i // ncg, i % ncg))
    sds = jax.ShapeDtypeStruct((n_lat // GRID_W, GRID_W * nh), F32)
    return pl.pallas_call(
        _norm_matmul_grid_kernel,
        grid=(n_lat // ROW_TILE, 2),
        in_specs=[
            pl.BlockSpec((gr, gc, d), lambda i, j: (i // ncg, i % ncg, 0)),
            pl.BlockSpec((1, d), lambda i, j: (0, 0)),
            pl.BlockSpec((None, N_MOD, d), lambda i, j: ((i // ncg) * gr // rows, 0, 0)),
            pl.BlockSpec((d, nh), lambda i, j: (0, j)),
        ],
        out_specs=[out_spec, out_spec],
        out_shape=[sds, sds],
        scratch_shapes=[pltpu.VMEM((ROW_TILE, d), BF16),
                        pltpu.VMEM((nh // LANE, ROW_TILE, LANE), F32)],
        compiler_params=_cparams(("parallel", "arbitrary")),
    )(x3, g.reshape(1, d), mods_l, w_bf)


def _mlp_kernel(x_ref, g_ref, mod_ref, w1_ref, w2_ref, fn_ref, o_ref, h_ref, *, final_norm):
    j = pl.program_id(1)
    slab = 256
    n_slab = x_ref.shape[0] // slab

    @pl.when(j == 0)
    def _():
        for s in range(n_slab):
            rs = slice(s * slab, (s + 1) * slab)
            h_ref[rs, :] = _bf(_norm_mod(x_ref[rs, :], g_ref[...], mod_ref[3:4, :],
                                         mod_ref[4:5, :]))
        o_ref[...] = jnp.zeros_like(o_ref)

    a = _dot(h_ref[...], _bf(w1_ref[...]))
    a = _bf(jnp.square(jnp.maximum(a, 0.0)))
    nsplit = 2
    wn = o_ref.shape[1] // nsplit
    for s in range(nsplit):
        o_ref[:, s * wn:(s + 1) * wn] += _dot(a, _bf(w2_ref[:, s * wn:(s + 1) * wn]))

    @pl.when(j == pl.num_programs(1) - 1)
    def _():
        for s in range(n_slab):
            rs = slice(s * slab, (s + 1) * slab)
            xo = x_ref[rs, :] + mod_ref[5:6, :] * o_ref[rs, :]
            if final_norm:
                ms = jnp.mean(xo * xo, axis=-1, keepdims=True)
                xo = xo * lax.rsqrt(ms + NORM_EPS) * fn_ref[...]
            o_ref[rs, :] = xo


def _mlp(x, g, mods_l, w1, w2, layer, fn, seg_of_tile, *, tm, tile0, n_tiles, final_norm):
    d = x.shape[1]
    dff = w1.shape[2]
    tf = MLP_STEP_ELEMS // tm
    return pl.pallas_call(
        functools.partial(_mlp_kernel, final_norm=final_norm),
        grid=(n_tiles, dff // tf),
        in_specs=[
            pl.BlockSpec((tm, d), lambda i, j: (i + tile0, 0), pipeline_mode=pl.Buffered(1)),
            pl.BlockSpec((1, d), lambda i, j: (0, 0)),
            pl.BlockSpec((None, N_MOD, d), lambda i, j: (seg_of_tile(i + tile0), 0, 0)),
            pl.BlockSpec((None, d, tf), lambda i, j: (layer, 0, j)),
            pl.BlockSpec((None, tf, d), lambda i, j: (layer, j, 0)),
            pl.BlockSpec((1, d), lambda i, j: (0, 0)),
        ],
        out_specs=pl.BlockSpec((tm, d), lambda i, j: (i, 0)),
        out_shape=jax.ShapeDtypeStruct((n_tiles * tm, d), F32),
        scratch_shapes=[pltpu.VMEM((tm, d), BF16)],
        compiler_params=_cparams(("parallel", "arbitrary")),
    )(x, g.reshape(1, d), mods_l, w1, w2, fn.reshape(1, d))


def _proj_res_kernel(y_ref, x_ref, mod_ref, w_ref, o_ref, yt_ref):
    k = w_ref.shape[0]
    gr, gc, d = x_ref.shape
    nq = k // LANE
    for c in range(gc):
        for q in range(nq):
            yt_ref[q, pl.ds(c, gr, stride=gc), :] = (
                y_ref[:, c * k + q * LANE:c * k + (q + 1) * LANE])
    yt = jnp.concatenate([_bf(yt_ref[q]) for q in range(nq)], axis=1)
    out = x_ref[...].reshape(gr * gc, d) + mod_ref[2:3, :] * _dot(yt, w_ref[...])
    o_ref[...] = out.reshape(gr, gc, d)


def _proj_res(y_grid, x, n_lat, rows, mods_l, w_bf):
    k, d = w_bf.shape
    gr, gc = _grid_tile(rows)
    ncg = GRID_W // gc
    x3 = x.reshape(x.shape[0] // GRID_W, GRID_W, d)
    tok_spec = pl.BlockSpec((gr, gc, d), lambda i: (i // ncg, i % ncg, 0))
    out = pl.pallas_call(
        _proj_res_kernel,
        grid=(n_lat // ROW_TILE,),
        in_specs=[
            pl.BlockSpec((gr, gc * k), lambda i: (i // ncg, i % ncg)),
            tok_spec,
            pl.BlockSpec((None, N_MOD, d), lambda i: ((i // ncg) * gr // rows, 0, 0)),
            pl.BlockSpec((k, d), lambda i: (0, 0)),
        ],
        out_specs=tok_spec,
        out_shape=jax.ShapeDtypeStruct((n_lat // GRID_W, GRID_W, d), F32),
        scratch_shapes=[pltpu.VMEM((k // LANE, ROW_TILE, LANE), F32)],
        compiler_params=_cparams(("parallel",)),
    )(y_grid, x3, mods_l, w_bf)
    return out.reshape(n_lat, d)


def _head_sum(z, e_bf):
    parts = []
    for gi in range(z.shape[1] // GROUP_W):
        zs = z[:, gi * GROUP_W:(gi + 1) * GROUP_W]
        parts.append(_dot_exact_rhs(zs, e_bf))
    return jnp.concatenate(parts, axis=1)


def _rwkv_prep_kernel(p_ref, pp_ref, pn_ref, mu_ref, w0_ref, a0_ref, w2_ref, a2_ref, g2_ref,
                      kk_ref, ka_ref, rk_ref, e_ref,
                      r_o, v_o, kk_o, lw0_o, lw1_o, kd0_o, kd1_o, b0_o, b1_o, g_o, bonus_o,
                      *, n_lat_tiles, lat_tiles_per_seq, ctx_tiles_per_seq, aw):
    i = pl.program_id(0)
    tp = p_ref.shape[0]
    in_ctx = i >= n_lat_tiles
    pos = jnp.where(in_ctx, (i - n_lat_tiles) % ctx_tiles_per_seq, i % lat_tiles_per_seq)
    last = jnp.where(in_ctx, ctx_tiles_per_seq - 1, lat_tiles_per_seq - 1)
    is_first = pos == 0
    is_last = pos == last
    rowid = lax.broadcasted_iota(jnp.int32, (tp, 1), 0)

    def shifted(c0, c1):
        x = p_ref[:, c0:c1]
        prow = jnp.where(is_first, 0.0, pp_ref[SUBLANE - 1:SUBLANE, c0:c1])
        nrow = jnp.where(is_last, 0.0, pn_ref[0:1, c0:c1])
        xp = jnp.where(rowid == 0, prow, pltpu.roll(x, 1, 0))
        xn = jnp.where(rowid == tp - 1, nrow, pltpu.roll(x, tp - 1, 0))
        return x + mu_ref[0:1, c0:c1] * (xp - x) + mu_ref[1:2, c0:c1] * (xn - x)

    e_bf = e_ref[...]
    r = shifted(0, aw)
    k = shifted(aw, 2 * aw)
    v = shifted(2 * aw, 3 * aw)
    gd = shifted(3 * aw, 3 * aw + A_LORA_G)
    lo0 = 3 * aw + A_LORA_G
    xl = shifted(lo0, lo0 + 2 * A_LORA_W + 2 * A_LORA_A)

    r_o[...] = r
    v_o[...] = v
    g_o[...] = _dot(_bf(_sigmoid(gd)), g2_ref[...])

    kx = k * kk_ref[...]
    kn = jnp.sqrt(_head_sum(kx * kx, e_bf))
    kk = kx / jnp.maximum(kn, 1e-12)
    kk_o[...] = kk

    xl_t = _bf(jnp.tanh(xl))
    xl_b = _bf(xl)
    kd_sum = None
    for d, (lw_o, kd_o, b_o) in enumerate(((lw0_o, kd0_o, b0_o), (lw1_o, kd1_o, b1_o))):
        w_log = -_softplus(-(w0_ref[d:d + 1, :] + _dot(xl_t, w2_ref[d]))) - 0.5
        lw_o[...] = -jnp.exp(w_log)
        asig = _sigmoid(a0_ref[d:d + 1, :] + _dot(xl_b, a2_ref[d]))
        kd = k * (1.0 + (asig - 1.0) * ka_ref[...])
        kd_o[...] = kd
        b_o[...] = kk * asig
        kd_sum = kd if kd_sum is None else kd_sum + kd
    bonus_o[...] = _head_sum(r * rk_ref[...] * kd_sum, e_bf) * v


def _rwkv_prep(pa, mu_p, w0, a0, w2p, a2p, g2, k_k, k_a, r_k, e_bf, *, n_lat, seq, ctx_len):
    r = pa.shape[0]
    ncol = mu_p.shape[1]
    aw = w0.shape[1]
    tp = PREP_TILE
    hb = tp // SUBLANE
    nblk8 = r // SUBLANE
    lw = 2 * A_LORA_W + 2 * A_LORA_A
    row = lambda a: a.reshape(1, aw)
    full = lambda shape: pl.BlockSpec(shape, lambda i: (0,) * len(shape))
    out_spec = pl.BlockSpec((tp, aw), lambda i: (i, 0))
    out_sds = jax.ShapeDtypeStruct((r, aw), F32)
    return pl.pallas_call(
        functools.partial(_rwkv_prep_kernel, n_lat_tiles=n_lat // tp, lat_tiles_per_seq=seq // tp,
                          ctx_tiles_per_seq=ctx_len // tp, aw=aw),
        grid=(r // tp,),
        in_specs=[
            pl.BlockSpec((tp, ncol), lambda i: (i, 0)),
            pl.BlockSpec((SUBLANE, ncol), lambda i: (jnp.maximum(i * hb - 1, 0), 0)),
            pl.BlockSpec((SUBLANE, ncol), lambda i: (jnp.minimum((i + 1) * hb, nblk8 - 1), 0)),
            full((2, ncol)), full((2, aw)), full((2, aw)),
            full((2, lw, aw)), full((2, lw, aw)), full((A_LORA_G, aw)),
            full((1, aw)), full((1, aw)), full((1, aw)), full((GROUP_W, GROUP_W)),
        ],
        out_specs=[out_spec] * 11,
        out_shape=[out_sds] * 11,
        compiler_params=_cparams(("parallel",)),
    )(pa, pa, pa, mu_p, w0, a0, w2p, a2p, g2, row(k_k), row(k_a), row(r_k), e_bf)


def _rwkv_masks(reverse):
    L = CHUNK
    gw = GROUP_W
    row = lax.broadcasted_iota(jnp.int32, (gw, gw), 0)
    col = lax.broadcasted_iota(jnp.int32, (gw, gw), 1)
    same_head = (row >> 6) == (col >> 6)
    tr = row & (L - 1)
    tc = col & (L - 1)
    strict = same_head & ((tr < tc) if reverse else (tr > tc))
    incl = same_head & ((tr <= tc) if reverse else (tr >= tc))
    t_r = lax.broadcasted_iota(jnp.int32, (L, L), 0)
    t_c = lax.broadcasted_iota(jnp.int32, (L, L), 1)
    tri = _bf(jnp.where((t_c >= t_r) if reverse else (t_c <= t_r), 1.0, 0.0))
    return dict(same_head=same_head, strict=strict, incl=incl, eye=row == col, tri=tri)


def _rwkv_chunks(probs, side=None):
    L = CHUNK
    gw = GROUP_W
    tick = (lambda: None) if side is None else (lambda: next(side, None))
    masks = {rv: _rwkv_masks(rv) for rv in sorted({p[7] for p in probs})}
    mk = [masks[p[7]] for p in probs]
    each = lambda f, *cols: [f(*args) for args in zip(*cols)]
    rep = lambda x: jnp.concatenate([x] * HEADS_PER_GROUP, axis=0)
    nat = lambda x: x[0:L] + x[L:2 * L] + x[2 * L:3 * L] + x[3 * L:4 * L]

    c = [_dot_exact_lhs(m["tri"], p[3]) for m, p in zip(mk, probs)]
    c_l = [ci[(0 if p[7] else L - 1):(1 if p[7] else L), :] for ci, p in zip(c, probs)]

    at_bd, rt_bd, v_bf, bend_bd, kend_bd, lhs, rhs, e_l = [], [], [], [], [], [], [], []
    for (r, v, kk, lw, kd, b, _, _), m, ci, cl in zip(probs, mk, c, c_l):
        bd = lambda x, sh=m["same_head"]: jnp.where(sh, rep(x), 0.0)
        e_nc = jnp.exp(-ci)
        e_end = jnp.exp(cl - ci)
        a_ = bd(-kk * jnp.exp(ci - lw))
        r_ = bd(r * jnp.exp(ci))
        at_bd.append(a_)
        rt_bd.append(r_)
        v_bf.append(_bf(bd(v)))
        bend_bd.append(_bf(bd(b * e_end)))
        kend_bd.append(_bf(bd(kd * e_end)))
        lhs.append(_bf(jnp.concatenate([a_, r_], axis=0)))
        rhs.append(_bf(jnp.concatenate([rep(b * e_nc), rep(kd * e_nc)], axis=0)))
        e_l.append(jnp.exp(cl))

    s = each(_dot_nt, lhs, rhs)
    a_ab = [jnp.where(m["strict"], si[:gw, :gw], 0.0) for m, si in zip(mk, s)]
    a_ak = [_bf(jnp.where(m["strict"], si[:gw, gw:], 0.0)) for m, si in zip(mk, s)]
    r_b = [_bf(jnp.where(m["incl"], si[gw:, :gw], 0.0)) for m, si in zip(mk, s)]
    r_k = [_bf(jnp.where(m["incl"], si[gw:, gw:], 0.0)) for m, si in zip(mk, s)]

    tick()
    akv = each(_dot, a_ak, v_bf)
    t = [jnp.where(m["eye"], 1.0, 0.0) + a for m, a in zip(mk, a_ab)]
    a_pow = a_ab
    for _ in range(5):
        a_pow_bf = [_bf(a) for a in a_pow]
        a_pow = each(_dot, a_pow_bf, a_pow_bf)
        tick()
        t = [ti + _dot(_bf(ti), _bf(ap)) for ti, ap in zip(t, a_pow)]
    h_bf = [_bf(p[6]) for p in probs]
    x = [_dot(_bf(a_), h) + ak for a_, h, ak in zip(at_bd, h_bf, akv)]
    tick()
    u = [_dot(_bf(ti), _bf(xi)) for ti, xi in zip(t, x)]
    tick()
    uv = [jnp.concatenate([_bf(ui), v], axis=0) for ui, v in zip(u, v_bf)]
    y = [_dot(_bf(nat(r_)), h) + nat(_dot(jnp.concatenate([rb, rk], axis=1), uvi))
         for r_, h, rb, rk, uvi in zip(rt_bd, h_bf, r_b, r_k, uv)]
    e_col = [jnp.broadcast_to(el, (gw, gw)).T for el in e_l]
    h_new = [ec * p[6] + _dot_tn(jnp.concatenate([be, ke], axis=0), uvi)
             for ec, p, be, ke, uvi in zip(e_col, probs, bend_bd, kend_bd, uv)]
    if side is not None:
        for _ in side:
            pass
    return list(zip(y, h_new))


def _chunk_row_block(b, i, *, reverse, n_lat, seq, ctx_len):
    nc = ctx_len // CHUNK
    nl = seq // CHUNK
    ci = (nc - 1 - i) if reverse else i
    li = (nl - 1 - (i - nc)) if reverse else (i - nc)
    return jnp.where(i < nc, (n_lat + b * ctx_len) // CHUNK + ci, b * nl + li)


def _gla_stages(qkf, vf, gdf, qkb, vb, gdb, gw2_ref, gb_ref, of_ref, ob_ref, sf_ref, sb_ref):
    dv, dk = sf_ref.shape[1:]
    L = CHUNK
    wk = B_HEADS * dk
    t_r = lax.broadcasted_iota(jnp.int32, (L, L), 0)
    t_c = lax.broadcasted_iota(jnp.int32, (L, L), 1)

    qh, kh, keh, vh, dech, keeph, sth = [], [], [], [], [], [], []
    for d, (qk_ref, v_ref, gd_ref, st_ref) in enumerate(((qkf, vf, gdf, sf_ref),
                                                         (qkb, vb, gdb, sb_ref))):
        reverse = d == 1
        keep = (t_c >= t_r) if reverse else (t_c <= t_r)
        tri = _bf(jnp.where(keep, 1.0, 0.0))
        z = _dot(_bf(gd_ref[...]), gw2_ref[d]) + gb_ref[d:d + 1, :]
        la = -_softplus(-z) * (1.0 / B_GATE_TAU)
        cum = _dot_exact_lhs(tri, la)
        last = 0 if reverse else L - 1
        cl = cum[last:last + 1, :]
        q_in = _bf(qk_ref[:, :wk] * (dk ** -0.5) * jnp.exp(cum))
        k_in = _bf(qk_ref[:, wk:] * jnp.exp(-cum))
        k_end = _bf(qk_ref[:, wk:] * jnp.exp(cl - cum))
        dec = jnp.exp(cl)
        for h in range(B_HEADS):
            ks = slice(h * dk, (h + 1) * dk)
            qh.append(q_in[:, ks])
            kh.append(k_in[:, ks])
            keh.append(k_end[:, ks])
            vh.append(_bf(v_ref[:, h * dv:(h + 1) * dv]))
            dech.append(dec[:, ks])
            keeph.append(keep)
            sth.append(st_ref[h])
        yield

    sc = [_bf(jnp.where(kp, _dot_nt(q, k), 0.0)) for kp, q, k in zip(keeph, qh, kh)]
    yield
    inter = [_dot_nt(q, _bf(st)) for q, st in zip(qh, sth)]
    yield
    kvt = [_dot_tn(v, ke) for v, ke in zip(vh, keh)]
    yield
    out = [_dot(s_, v) + it for s_, v, it in zip(sc, vh, inter)]
    yield
    for d, (o_ref, st_ref) in enumerate(((of_ref, sf_ref), (ob_ref, sb_ref))):
        for h in range(B_HEADS):
            i = d * B_HEADS + h
            o_ref[:, h * dv:(h + 1) * dv] = out[i]
            st_ref[h] = sth[i] * dech[i] + kvt[i]


def _mix_scan_kernel(rf, vf, kkf, lwf, kdf, bf_, rb, vb, kkb, lwb, kdb, bb,
                     qkf, gvf, gdf, qkb, gvb, gdb, gw2_ref, gb_ref,
                     yf_ref, yb_ref, of_ref, ob_ref, hf_ref, hb_ref, sf_ref, sb_ref):
    @pl.when(pl.program_id(1) == 0)
    def _():
        for ref in (hf_ref, hb_ref, sf_ref, sb_ref):
            ref[...] = jnp.zeros_like(ref)

    gs = hf_ref.shape[0]
    probs = []
    for gi in range(gs):
        sl = slice(gi * GROUP_W, (gi + 1) * GROUP_W)
        probs.append((rf[:, sl], vf[:, sl], kkf[:, sl], lwf[:, sl], kdf[:, sl], bf_[:, sl],
                      hf_ref[gi], False))
        probs.append((rb[:, sl], vb[:, sl], kkb[:, sl], lwb[:, sl], kdb[:, sl], bb[:, sl],
                      hb_ref[gi], True))
    gla = _gla_stages(qkf, gvf, gdf, qkb, gvb, gdb, gw2_ref, gb_ref, of_ref, ob_ref, sf_ref,
                      sb_ref)
    res = _rwkv_chunks(probs, side=gla)
    for gi in range(gs):
        sl = slice(gi * GROUP_W, (gi + 1) * GROUP_W)
        yf_ref[:, sl], hf_ref[gi] = res[2 * gi]
        yb_ref[:, sl], hb_ref[gi] = res[2 * gi + 1]


def _mix_scan(r, v, kk, lw0, lw1, kd0, kd1, b0, b1, pb, gw2p, gb, *, batch, n_lat, seq, ctx_len,
              wk, wv, col0):
    rows, aw = r.shape
    assert aw == GROUP_W * RWKV_GROUPS_PER_STEP
    dk, dv = wk // B_HEADS, wv // B_HEADS
    geo = dict(n_lat=n_lat, seq=seq, ctx_len=ctx_len)
    gd_blk = (col0 + 2 * wk + 2 * wv) // LANE

    def specs(reverse):
        rb = lambda b, i: _chunk_row_block(b, i, reverse=reverse, **geo)
        rw = pl.BlockSpec((CHUNK, aw), lambda b, i: (rb(b, i), 0))
        gla = [
            pl.BlockSpec((CHUNK, 2 * wk), lambda b, i: (rb(b, i), col0 // (2 * wk))),
            pl.BlockSpec((CHUNK, wv), lambda b, i: (rb(b, i), (col0 + 2 * wk) // wv)),
            pl.BlockSpec((CHUNK, LANE), lambda b, i: (rb(b, i), gd_blk)),
        ]
        out = pl.BlockSpec((CHUNK, wv), lambda b, i: (rb(b, i), 0))
        return rw, gla, out

    rwf, glaf, outf = specs(False)
    rwb, glab, outb = specs(True)
    sds_a = jax.ShapeDtypeStruct((rows, aw), F32)
    sds_b = jax.ShapeDtypeStruct((rows, wv), F32)
    gs = RWKV_GROUPS_PER_STEP
    return pl.pallas_call(
        _mix_scan_kernel,
        grid=(batch, (seq + ctx_len) // CHUNK),
        in_specs=[rwf] * 6 + [rwb] * 6 + glaf + glab + [
            pl.BlockSpec((2, LANE, wk), lambda b, i: (0, 0, 0)),
            pl.BlockSpec((2, wk), lambda b, i: (0, 0)),
        ],
        out_specs=[rwf, rwb, outf, outb],
        out_shape=[sds_a, sds_a, sds_b, sds_b],
        scratch_shapes=[pltpu.VMEM((gs, GROUP_W, GROUP_W), F32),
                        pltpu.VMEM((gs, GROUP_W, GROUP_W), F32),
                        pltpu.VMEM((B_HEADS, dv, dk), F32), pltpu.VMEM((B_HEADS, dv, dk), F32)],
        compiler_params=_cparams(("parallel", "arbitrary")),
    )(r, v, kk, lw0, kd0, b0, r, v, kk, lw1, kd1, b1, pb, pb, pb, pb, pb, pb, gw2p, gb)


def _mixer_out_kernel(yf, yb, bonus, g, of, ob, gate, xa_ref, xb_ref, mod_ref, lnw, lnb, gn, e_ref,
                      w_ref, o_ref, h_ref, *, aw, dv, n_a_tiles):
    e_bf = e_ref[...]
    y = yf[...] + yb[...]
    inv = 1.0 / A_HEAD_DIM
    mean = _head_sum(y, e_bf) * inv
    dlt = y - mean
    var = _head_sum(dlt * dlt, e_bf) * inv
    yn = dlt * lax.rsqrt(var + A_LN_EPS) * lnw[...] + lnb[...]
    h_ref[:, :aw] = _bf((yn + bonus[...]) * g[...])

    o = of[...] + ob[...]
    gt = _silu(gate[...])
    for h in range(B_HEADS):
        sl = slice(h * dv, (h + 1) * dv)
        oh = o[:, sl]
        ms = jnp.mean(oh * oh, axis=-1, keepdims=True)
        h_ref[:, aw + h * dv:aw + (h + 1) * dv] = _bf(oh * lax.rsqrt(ms + NORM_EPS) * gn[...] * gt[:, sl])

    x = jnp.where(pl.program_id(0) < n_a_tiles, xa_ref[...], xb_ref[...])
    o_ref[...] = x + mod_ref[2:3, :] * _dot(h_ref[...], w_ref[...])


def _mixer_out(yf, yb, bonus, g, of, ob, pb, xa, xb, mods_l, ln_w, ln_b, gla_norm, e_bf, w_bf,
               seg_of_tile, *, wk, col0):
    d = xa.shape[1]
    rows = xa.shape[0] + xb.shape[0]
    aw = yf.shape[1]
    wv = of.shape[1]
    dv = wv // B_HEADS
    tm = 256
    tiles_per_row_tile = ROW_TILE // tm
    n_a_tiles = xa.shape[0] // tm
    rs = lambda w: pl.BlockSpec((tm, w), lambda i: (i, 0))
    full = lambda shape: pl.BlockSpec(shape, lambda i: (0,) * len(shape))
    return pl.pallas_call(
        functools.partial(_mixer_out_kernel, aw=aw, dv=dv, n_a_tiles=n_a_tiles),
        grid=(rows // tm,),
        in_specs=[
            rs(aw), rs(aw), rs(aw), rs(aw), rs(wv), rs(wv),
            pl.BlockSpec((tm, wv), lambda i: (i, (col0 + 2 * wk + wv) // wv)),
        ] + _two_source_specs(tm, d, n_a_tiles) + [
            pl.BlockSpec((None, N_MOD, d), lambda i: (seg_of_tile(i // tiles_per_row_tile), 0, 0)),
            full((1, aw)), full((1, aw)), full((1, dv)), full((GROUP_W, GROUP_W)),
            full((aw + wv, d)),
        ],
        out_specs=rs(d),
        out_shape=jax.ShapeDtypeStruct((rows, d), F32),
        scratch_shapes=[pltpu.VMEM((tm, aw + wv), BF16)],
        compiler_params=_cparams(("parallel",)),
    )(yf, yb, bonus, g, of, ob, pb, xa, xb, mods_l, ln_w.reshape(1, aw), ln_b.reshape(1, aw),
      gla_norm.reshape(1, dv), e_bf, w_bf)


def _lru_kernel(*refs, reverse, n_blk, combine):
    if combine:
        (x_ref, xp_ref, xn_ref, h0_ref, cw_ref, cb_ref, wc_ref, ba_ref, bx_ref, lam_ref,
         hf_ref, gate_ref, out_ref, hl_ref, a_sc, u_sc, hs_sc, carry_sc) = refs
    else:
        (x_ref, xp_ref, xn_ref, h0_ref, cw_ref, cb_ref, wc_ref, ba_ref, bx_ref, lam_ref,
         out_ref, hl_ref, a_sc, u_sc, hs_sc, carry_sc) = refs
    j = pl.program_id(2)
    blk = (n_blk - 1 - j) if reverse else j
    lb, ct = x_ref.shape
    cblk = wc_ref.shape[1]
    is_first = blk == 0
    is_last = blk == n_blk - 1
    rowid = lax.broadcasted_iota(jnp.int32, (lb, 1), 0)

    x = x_ref[...]
    p6 = jnp.where(is_first, 0.0, xp_ref[SUBLANE - 2:SUBLANE - 1, :])
    p7 = jnp.where(is_first, 0.0, xp_ref[SUBLANE - 1:SUBLANE, :])
    n0 = jnp.where(is_last, 0.0, xn_ref[0:1, :])
    xm1 = jnp.where(rowid == 0, p7, pltpu.roll(x, 1, 0))
    xm2 = jnp.where(rowid == 0, p6, jnp.where(rowid == 1, p7, pltpu.roll(x, 2, 0)))
    xp1 = jnp.where(rowid == lb - 1, n0, pltpu.roll(x, lb - 1, 0))
    xs = (cb_ref[...] + cw_ref[0:1, :] * xm2 + cw_ref[1:2, :] * xm1 + cw_ref[2:3, :] * x
          + cw_ref[3:4, :] * xp1)

    for n in range(ct // cblk):
        sl = slice(n * cblk, (n + 1) * cblk)
        xb = xs[:, sl]
        ri = _dot(_bf(xb), wc_ref[n])
        rg = _sigmoid(ri[:, :cblk] + ba_ref[:, sl])
        ig = _sigmoid(ri[:, cblk:] + bx_ref[:, sl])
        log_a = -C_CONST * rg * _softplus(-lam_ref[:, sl])
        a = jnp.exp(log_a)
        a_sc[:, sl] = a
        u_sc[:, sl] = jnp.sqrt(1.0 - a * a) * (ig * xb)

    @pl.when(j == 0)
    def _():
        carry_sc[...] = h0_ref[0:1, :]

    def step(t, h):
        tt = (lb - 1 - t) if reverse else t
        h = a_sc[pl.ds(tt, 1), :] * h + u_sc[pl.ds(tt, 1), :]
        hs_sc[pl.ds(tt, 1), :] = h
        return h

    h = lax.fori_loop(0, lb, step, carry_sc[...], unroll=8)
    carry_sc[...] = h
    hl_ref[...] = jnp.broadcast_to(h, hl_ref.shape)
    if combine:
        out_ref[...] = (hs_sc[...] + hf_ref[...]) * jax.nn.gelu(gate_ref[...])
    else:
        out_ref[...] = hs_sc[...]


def _lru_pass(x_arr, x_specs, h0, conv_w, conv_b, wcat, ba, bx, lam, out_shape, out_spec, extra,
              *, batch, n_blk, lb, reverse):
    c = conv_w.shape[1]
    ct = 1024
    nct = c // ct
    cblk = wcat.shape[1]
    chan = lambda rows_: pl.BlockSpec((rows_, ct), lambda b, k, j: (0, k))
    in_specs = list(x_specs) + [
        pl.BlockSpec((SUBLANE, ct), lambda b, k, j: (b, k)),
        chan(C_CONV), chan(1),
        pl.BlockSpec((ct // cblk, cblk, 2 * cblk), lambda b, k, j: (k, 0, 0)),
        chan(1), chan(1), chan(1),
    ] + [s for _, s in extra]
    args = [x_arr, x_arr, x_arr, h0, conv_w, conv_b.reshape(1, c), wcat, ba.reshape(1, c),
            bx.reshape(1, c), lam.reshape(1, c)] + [a for a, _ in extra]
    return pl.pallas_call(
        functools.partial(_lru_kernel, reverse=reverse, n_blk=n_blk, combine=bool(extra)),
        grid=(batch, nct, n_blk),
        in_specs=in_specs,
        out_specs=[out_spec, pl.BlockSpec((SUBLANE, ct), lambda b, k, j: (b, k))],
        out_shape=[out_shape, jax.ShapeDtypeStruct((batch * SUBLANE, c), F32)],
        scratch_shapes=[pltpu.VMEM((lb, ct), F32), pltpu.VMEM((lb, ct), F32),
                        pltpu.VMEM((lb, ct), F32), pltpu.VMEM((1, ct), F32)],
        compiler_params=_cparams(("parallel", "parallel", "arbitrary")),
    )(*args)


def _rglru(hw_ctx, gate_grid, x_grid, conv_w, conv_b, wa, ba, wx, bx, lam, *, batch, n_lat, seq,
           ctx_len):
    c = conv_w.shape[1]
    ct = 1024
    rows = seq // GRID_W
    lbc = 128
    ncb = ctx_len // lbc
    xcol = c // ct
    per_col = c // ct

    wcat = [_bf(jnp.concatenate([wa[d], wx[d]], axis=-1)) for d in range(2)]
    zeros_state = jnp.zeros((batch * SUBLANE, c), F32)

    def ctx_specs(reverse):
        blk = (lambda j: ncb - 1 - j) if reverse else (lambda j: j)
        per8 = lbc // SUBLANE
        cur = pl.BlockSpec((lbc, ct), lambda b, k, j: (b * ncb + blk(j), xcol + k))
        prv = pl.BlockSpec((SUBLANE, ct), lambda b, k, j: (
            b * ncb * per8 + jnp.maximum(blk(j) * per8 - 1, 0), xcol + k))
        nxt = pl.BlockSpec((SUBLANE, ct), lambda b, k, j: (
            b * ncb * per8 + jnp.minimum((blk(j) + 1) * per8, ncb * per8 - 1), xcol + k))
        return cur, prv, nxt

    def lat_specs(reverse):
        blk = (lambda j: GRID_W - 1 - j) if reverse else (lambda j: j)
        r8 = rows // SUBLANE
        cur = pl.BlockSpec((rows, ct), lambda b, k, j: (b, blk(j) * per_col + k))
        prv = pl.BlockSpec((SUBLANE, ct), lambda b, k, j: (
            b * r8 + r8 - 1, jnp.maximum(blk(j) - 1, 0) * per_col + k))
        nxt = pl.BlockSpec((SUBLANE, ct), lambda b, k, j: (
            b * r8, jnp.minimum(blk(j) + 1, GRID_W - 1) * per_col + k))
        return cur, prv, nxt

    common = dict(batch=batch)
    ctx_scratch_sds = jax.ShapeDtypeStruct((batch * ctx_len, c), F32)
    ctx_out = lambda reverse: pl.BlockSpec(
        (lbc, ct), lambda b, k, j: (b * ncb + ((ncb - 1 - j) if reverse else j), k))
    _, st = _lru_pass(hw_ctx, ctx_specs(False), zeros_state, conv_w, conv_b, wcat[0], ba[0], bx[0],
                      lam[0], ctx_scratch_sds, ctx_out(False), [], n_blk=ncb, lb=lbc,
                      reverse=False, **common)
    hf_sds = jax.ShapeDtypeStruct((n_lat, c), F32)
    hf_spec = lambda reverse: pl.BlockSpec(
        (rows, ct), lambda b, k, j: (b * GRID_W + ((GRID_W - 1 - j) if reverse else j), k))
    hf, _ = _lru_pass(x_grid, lat_specs(False), st, conv_w, conv_b, wcat[0], ba[0], bx[0], lam[0],
                      hf_sds, hf_spec(False), [], n_blk=GRID_W, lb=rows, reverse=False, **common)
    _, st = _lru_pass(hw_ctx, ctx_specs(True), zeros_state, conv_w, conv_b, wcat[1], ba[1], bx[1],
                      lam[1], ctx_scratch_sds, ctx_out(True), [], n_blk=ncb, lb=lbc,
                      reverse=True, **common)
    grid_spec = pl.BlockSpec((rows, ct), lambda b, k, j: (b, (GRID_W - 1 - j) * per_col + k))
    y_sds = jax.ShapeDtypeStruct((n_lat // GRID_W, GRID_W * c), F32)
    y, _ = _lru_pass(x_grid, lat_specs(True), st, conv_w, conv_b, wcat[1], ba[1], bx[1], lam[1],
                     y_sds, grid_spec, [(hf, hf_spec(True)), (gate_grid, grid_spec)],
                     n_blk=GRID_W, lb=rows, reverse=True, **common)
    return y


def _pad_cols(w, n):
    return jnp.pad(w, ((0, 0),) * (w.ndim - 1) + ((0, n - w.shape[-1]),))


def kernel(x, c, ctx, c_ctx, mod_w, mod_b, norm1, norm2, mlp_w1, mlp_w2, ab_w_in, ab_w_out, rw_mu,
           rw_w0, rw_w2, rw_a0, rw_a2, rw_g2, rw_kk, rw_ka, rw_rk, rw_ln_w, rw_ln_b, gla_gw2,
           gla_gb, gla_norm, lru_w_in, lru_w_out, lru_conv_w, lru_conv_b, lru_wa, lru_ba, lru_wx,
           lru_bx, lru_lam, final_norm):
    batch, seq, d = x.shape
    ctx_len = ctx.shape[1]
    depth = mod_w.shape[0]
    n_lat = batch * seq
    aw = rw_w0.shape[-1]
    wk = gla_gb.shape[-1]
    wv = ab_w_out.shape[1] - aw
    a_cols = rw_mu.shape[-1]
    b_cols = ab_w_in.shape[-1] - a_cols
    assert depth == 2 and batch + 1 <= SUBLANE
    assert seq % ROW_TILE == 0 and (batch * ctx_len) % ROW_TILE == 0
    assert ctx_len % PREP_TILE == 0 and ctx_len % 128 == 0 and seq % (GRID_W * SUBLANE) == 0
    assert aw % GROUP_W == 0 and wv == aw and 2 * wk == wv

    tiles_per_seq = seq // ROW_TILE
    seg_of_tile = lambda i: jnp.minimum(i // tiles_per_seq, batch)

    x2 = x.reshape(n_lat, d)
    ctx2 = ctx.reshape(batch * ctx_len, d)
    cvec = jnp.concatenate(
        [c, c_ctx[None, :], jnp.zeros((SUBLANE - batch - 1, d), F32)], axis=0)
    mods = _mods(cvec, mod_w, mod_b)

    hid = jnp.arange(GROUP_W) // A_HEAD_DIM
    e_bf = _bf(hid[:, None] == hid[None, :])

    a_pad = -(-a_cols // LANE) * LANE
    col0 = -(-a_cols // (2 * wk)) * (2 * wk)
    n_tiles = 3
    tn = -(-(col0 + b_cols) // (n_tiles * LANE)) * LANE
    w_ab = _bf(jnp.concatenate(
        [_pad_cols(ab_w_in[0][:, :a_cols], col0),
         _pad_cols(ab_w_in[0][:, a_cols:], n_tiles * tn - col0)], axis=1))
    pa = pb = _norm_matmul(x2, ctx2, norm1[0], mods[0], w_ab, seg_of_tile, tn=tn)

    lora_w = 2 * A_LORA_W + 2 * A_LORA_A
    w2p = jnp.zeros((2, lora_w, aw), F32)
    a2p = jnp.zeros((2, lora_w, aw), F32)
    for dd in range(2):
        w2p = w2p.at[dd, dd * A_LORA_W:(dd + 1) * A_LORA_W].set(rw_w2[0, dd])
        a2p = a2p.at[dd, 2 * A_LORA_W + dd * A_LORA_A:2 * A_LORA_W + (dd + 1) * A_LORA_A].set(
            rw_a2[0, dd])
    (r_, v_, kk_, lw0, lw1, kd0, kd1, b0, b1, g_, bonus) = _rwkv_prep(
        pa, _pad_cols(rw_mu[0], a_pad), rw_w0[0], rw_a0[0], _bf(w2p), _bf(a2p), _bf(rw_g2[0]),
        rw_kk[0], rw_ka[0], rw_rk[0].reshape(-1), e_bf, n_lat=n_lat, seq=seq, ctx_len=ctx_len)
    gw2p = jnp.zeros((2, LANE, wk), F32)
    for dd in range(2):
        gw2p = gw2p.at[dd, dd * B_LORA:(dd + 1) * B_LORA].set(gla_gw2[0, dd])
    yf, yb, of, ob = _mix_scan(r_, v_, kk_, lw0, lw1, kd0, kd1, b0, b1, pb, _bf(gw2p), gla_gb[0],
                               batch=batch, n_lat=n_lat, seq=seq, ctx_len=ctx_len, wk=wk, wv=wv,
                               col0=col0)

    xs = _mixer_out(yf, yb, bonus, g_, of, ob, pb, x2, ctx2, mods[0], rw_ln_w[0], rw_ln_b[0],
                    gla_norm[0], e_bf, _bf(ab_w_out[0]), seg_of_tile, wk=wk, col0=col0)
    seg_of_mlp_tile = lambda i: jnp.minimum(i // (seq // MLP_TILE), batch)
    ctx_tiles = (batch * ctx_len) // ROW_TILE
    x_lat = _mlp(xs, norm2[0], mods[0], mlp_w1, mlp_w2, 0, final_norm, seg_of_mlp_tile, tm=MLP_TILE,
                 tile0=0, n_tiles=n_lat // MLP_TILE, final_norm=False)
    x_ctx = _mlp(xs, norm2[0], mods[0], mlp_w1, mlp_w2, 0, final_norm, seg_of_tile, tm=ROW_TILE,
                 tile0=n_lat // ROW_TILE, n_tiles=ctx_tiles, final_norm=False)

    w_lru = _bf(lru_w_in[0])
    hw_ctx = _norm_matmul(x_ctx, None, norm1[1], mods[1], w_lru, lambda i: batch, tn=2048)
    gate_grid, x_grid = _norm_matmul_grid(x_lat, n_lat, seq // GRID_W, norm1[1], mods[1], w_lru)
    y = _rglru(hw_ctx, gate_grid, x_grid, lru_conv_w[0], lru_conv_b[0], lru_wa[0], lru_ba[0],
               lru_wx[0], lru_bx[0], lru_lam[0], batch=batch, n_lat=n_lat, seq=seq,
               ctx_len=ctx_len)
    xl = _proj_res(y, x_lat, n_lat, seq // GRID_W, mods[1], _bf(lru_w_out[0]))
    out = _mlp(xl, norm2[1], mods[1], mlp_w1, mlp_w2, 1, final_norm, seg_of_mlp_tile,
               tm=MLP_TILE, tile0=0, n_tiles=n_lat // MLP_TILE, final_norm=True)
    return out.reshape(batch, seq, d)
```

```python
import functools

import jax
import jax.numpy as jnp
from jax import lax
from jax.experimental import pallas as pl
from jax.experimental.pallas import tpu as pltpu

F32 = jnp.float32
BF16 = jnp.bfloat16

NORM_EPS = 1e-6
GRID_W = 64
N_MOD = 6

A_HEAD_DIM = 64
A_LORA_W = 96
A_LORA_A = 96
A_LORA_G = 256
A_LN_EPS = 64e-5
HEADS_PER_GROUP = 4
GROUP_W = HEADS_PER_GROUP * A_HEAD_DIM
CHUNK = 64
RWKV_GROUPS_PER_STEP = 4

B_HEADS = 4
B_LORA = 16
B_GATE_TAU = 16.0

C_BLOCKS = 8
C_CONV = 4
C_CONST = 8.0

LANE = 128
SUBLANE = 8
ROW_TILE = 512
MLP_TILE = 1024
MLP_STEP_ELEMS = 1024 * 512
LRU_CHANNEL_TILE = 2048
PREP_TILE = 128
VMEM_LIMIT = 56 * 1024 * 1024


def _cparams(sem):
    return pltpu.CompilerParams(dimension_semantics=sem, vmem_limit_bytes=VMEM_LIMIT)


def _bf(x):
    return x.astype(BF16)


def _dot(a, b):
    return jnp.dot(a, b, preferred_element_type=F32)


def _dot_nt(a, b):
    return lax.dot_general(a, b, (((1,), (1,)), ((), ())), preferred_element_type=F32)


def _dot_tn(a, b):
    return lax.dot_general(a, b, (((0,), (0,)), ((), ())), preferred_element_type=F32)


def _split3(x):
    hi = _bf(x)
    r1 = x - hi.astype(F32)
    mid = _bf(r1)
    lo = _bf(r1 - mid.astype(F32))
    return hi, mid, lo


def _dot_exact_lhs(m_bf, x):
    hi, mid, lo = _split3(x)
    return _dot(m_bf, hi) + _dot(m_bf, mid) + _dot(m_bf, lo)


def _dot_exact_rhs(x, m_bf):
    hi, mid, lo = _split3(x)
    return _dot(hi, m_bf) + _dot(mid, m_bf) + _dot(lo, m_bf)


def _softplus(x):
    return jnp.maximum(x, 0.0) + jnp.log(1.0 + jnp.exp(-jnp.abs(x)))


def _sigmoid(x):
    return 0.5 * jnp.tanh(0.5 * x) + 0.5


def _silu(x):
    return x * _sigmoid(x)


def _norm_mod(x, g, shift, scale):
    ms = jnp.mean(x * x, axis=-1, keepdims=True)
    return (x * lax.rsqrt(ms + NORM_EPS) * g) * (1.0 + scale) + shift


def _mods_kernel(c_ref, w_ref, b_ref, o_ref):
    s = _silu(c_ref[...])
    o_ref[...] = _dot(_bf(s), _bf(w_ref[...])) + b_ref[...]


def _mods(cvec, mod_w, mod_b):
    depth, d, n = mod_w.shape
    tn = 1024
    out = pl.pallas_call(
        _mods_kernel,
        grid=(depth, n // tn),
        in_specs=[
            pl.BlockSpec((SUBLANE, d), lambda l, j: (0, 0)),
            pl.BlockSpec((None, d, tn), lambda l, j: (l, 0, j)),
            pl.BlockSpec((None, 1, tn), lambda l, j: (l, 0, j)),
        ],
        out_specs=pl.BlockSpec((None, SUBLANE, tn), lambda l, j: (l, 0, j)),
        out_shape=jax.ShapeDtypeStruct((depth, SUBLANE, n), F32),
        compiler_params=_cparams(("parallel", "parallel")),
    )(cvec, mod_w, mod_b.reshape(depth, 1, n))
    return out.reshape(depth, SUBLANE, N_MOD, d)


def _norm_matmul_kernel(xa_ref, xb_ref, g_ref, mod_ref, w_ref, o_ref, h_ref, *, n_a_tiles):
    i = pl.program_id(0)

    @pl.when(pl.program_id(1) == 0)
    def _():
        x = jnp.where(i < n_a_tiles, xa_ref[...], xb_ref[...])
        h_ref[...] = _bf(_norm_mod(x, g_ref[...], mod_ref[0:1, :], mod_ref[1:2, :]))

    o_ref[...] = _dot(h_ref[...], w_ref[...])


def _two_source_specs(tm, d, n_a_tiles):
    return [pl.BlockSpec((tm, d), lambda i, *_: (jnp.minimum(i, n_a_tiles - 1), 0)),
            pl.BlockSpec((tm, d), lambda i, *_: (jnp.maximum(i - n_a_tiles, 0), 0))]


def _norm_matmul(xa, xb, g, mods_l, w_bf, seg_of_tile, *, tn):
    d = xa.shape[1]
    n = w_bf.shape[1]
    tm = ROW_TILE
    n_a_tiles = xa.shape[0] // tm
    n_tiles = n_a_tiles + (0 if xb is None else xb.shape[0] // tm)
    return pl.pallas_call(
        functools.partial(_norm_matmul_kernel, n_a_tiles=n_a_tiles),
        grid=(n_tiles, n // tn),
        in_specs=_two_source_specs(tm, d, n_a_tiles) + [
            pl.BlockSpec((1, d), lambda i, j: (0, 0)),
            pl.BlockSpec((None, N_MOD, d), lambda i, j: (seg_of_tile(i), 0, 0)),
            pl.BlockSpec((d, tn), lambda i, j: (0, j)),
        ],
        out_specs=pl.BlockSpec((tm, tn), lambda i, j: (i, j)),
        out_shape=jax.ShapeDtypeStruct((n_tiles * tm, n), F32),
        scratch_shapes=[pltpu.VMEM((tm, d), BF16)],
        compiler_params=_cparams(("parallel", "arbitrary")),
    )(xa, xa if xb is None else xb, g.reshape(1, d), mods_l, w_bf)


def _grid_tile(rows):
    gr = min(rows, 64)
    gc = ROW_TILE // gr
    assert gc % SUBLANE == 0 and GRID_W % gc == 0 and rows % gr == 0
    return gr, gc


def _norm_matmul_grid_kernel(x_ref, g_ref, mod_ref, w_ref, o0_ref, o1_ref, h_ref, res_ref):
    j = pl.program_id(1)
    gr, gc, d = x_ref.shape

    @pl.when(j == 0)
    def _():
        h = _norm_mod(x_ref[...].reshape(gr * gc, d), g_ref[...], mod_ref[0:1, :], mod_ref[1:2, :])
        h_ref[...] = _bf(h)

    res = _dot(h_ref[...], w_ref[...])
    nh = w_ref.shape[1]
    nq = nh // LANE
    for q in range(nq):
        res_ref[q] = res[:, q * LANE:(q + 1) * LANE]

    def scatter(o_ref):
        for c in range(gc):
            for q in range(nq):
                o_ref[:, c * nh + q * LANE:c * nh + (q + 1) * LANE] = (
                    res_ref[q, pl.ds(c, gr, stride=gc), :])

    @pl.when(j == 0)
    def _():
        scatter(o0_ref)

    @pl.when(j == 1)
    def _():
        scatter(o1_ref)


def _norm_matmul_grid(x, n_lat, rows, g, mods_l, w_bf):
    d = x.shape[1]
    nh = w_bf.shape[1] // 2
    gr, gc = _grid_tile(rows)
    ncg = GRID_W // gc
    x3 = x.reshape(x.shape[0] // GRID_W, GRID_W, d)
    out_spec = pl.BlockSpec((gr, gc * nh), lambda i, j: (i // ncg, i % ncg))
    sds = jax.ShapeDtypeStruct((n_lat // GRID_W, GRID_W * nh), F32)
    return pl.pallas_call(
        _norm_matmul_grid_kernel,
        grid=(n_lat // ROW_TILE, 2),
        in_specs=[
            pl.BlockSpec((gr, gc, d), lambda i, j: (i // ncg, i % ncg, 0)),
            pl.BlockSpec((1, d), lambda i, j: (0, 0)),
            pl.BlockSpec((None, N_MOD, d), lambda i, j: ((i // ncg) * gr // rows, 0, 0)),
            pl.BlockSpec((d, nh), lambda i, j: (0, j)),
        ],
        out_specs=[out_spec, out_spec],
        out_shape=[sds, sds],
        scratch_shapes=[pltpu.VMEM((ROW_TILE, d), BF16),
                        pltpu.VMEM((nh // LANE, ROW_TILE, LANE), F32)],
        compiler_params=_cparams(("parallel", "arbitrary")),
    )(x3, g.reshape(1, d), mods_l, w_bf)


def _mlp_kernel(x_ref, g_ref, mod_ref, w1_ref, w2_ref, fn_ref, o_ref, h_ref, *, final_norm):
    j = pl.program_id(1)
    slab = 256
    n_slab = x_ref.shape[0] // slab

    @pl.when(j == 0)
    def _():
        for s in range(n_slab):
            rs = slice(s * slab, (s + 1) * slab)
            h_ref[rs, :] = _bf(_norm_mod(x_ref[rs, :], g_ref[...], mod_ref[3:4, :],
                                         mod_ref[4:5, :]))
        o_ref[...] = jnp.zeros_like(o_ref)

    a = _dot(h_ref[...], w1_ref[...])
    a = _bf(jnp.square(jnp.maximum(a, 0.0)))
    nsplit = 2
    wn = o_ref.shape[1] // nsplit
    for s in range(nsplit):
        o_ref[:, s * wn:(s + 1) * wn] += _dot(a, w2_ref[:, s * wn:(s + 1) * wn])

    @pl.when(j == pl.num_programs(1) - 1)
    def _():
        for s in range(n_slab):
            rs = slice(s * slab, (s + 1) * slab)
            xo = x_ref[rs, :] + mod_ref[5:6, :] * o_ref[rs, :]
            if final_norm:
                ms = jnp.mean(xo * xo, axis=-1, keepdims=True)
                xo = xo * lax.rsqrt(ms + NORM_EPS) * fn_ref[...]
            o_ref[rs, :] = xo


def _mlp(x, g, mods_l, w1_bf, w2_bf, fn, seg_of_tile, *, tm, tile0, n_tiles, final_norm):
    d = x.shape[1]
    dff = w1_bf.shape[1]
    tf = MLP_STEP_ELEMS // tm
    return pl.pallas_call(
        functools.partial(_mlp_kernel, final_norm=final_norm),
        grid=(n_tiles, dff // tf),
        in_specs=[
            pl.BlockSpec((tm, d), lambda i, j: (i + tile0, 0), pipeline_mode=pl.Buffered(1)),
            pl.BlockSpec((1, d), lambda i, j: (0, 0)),
            pl.BlockSpec((None, N_MOD, d), lambda i, j: (seg_of_tile(i + tile0), 0, 0)),
            pl.BlockSpec((d, tf), lambda i, j: (0, j)),
            pl.BlockSpec((tf, d), lambda i, j: (j, 0)),
            pl.BlockSpec((1, d), lambda i, j: (0, 0)),
        ],
        out_specs=pl.BlockSpec((tm, d), lambda i, j: (i, 0)),
        out_shape=jax.ShapeDtypeStruct((n_tiles * tm, d), F32),
        scratch_shapes=[pltpu.VMEM((tm, d), BF16)],
        compiler_params=_cparams(("parallel", "arbitrary")),
    )(x, g.reshape(1, d), mods_l, w1_bf, w2_bf, fn.reshape(1, d))


def _proj_res_kernel(y_ref, x_ref, mod_ref, w_ref, o_ref, yt_ref):
    k = w_ref.shape[0]
    gr, gc, d = x_ref.shape
    nq = k // LANE
    for c in range(gc):
        for q in range(nq):
            yt_ref[q, pl.ds(c, gr, stride=gc), :] = (
                y_ref[:, c * k + q * LANE:c * k + (q + 1) * LANE])
    yt = jnp.concatenate([_bf(yt_ref[q]) for q in range(nq)], axis=1)
    out = x_ref[...].reshape(gr * gc, d) + mod_ref[2:3, :] * _dot(yt, w_ref[...])
    o_ref[...] = out.reshape(gr, gc, d)


def _proj_res(y_grid, x, n_lat, rows, mods_l, w_bf):
    k, d = w_bf.shape
    gr, gc = _grid_tile(rows)
    ncg = GRID_W // gc
    x3 = x.reshape(x.shape[0] // GRID_W, GRID_W, d)
    tok_spec = pl.BlockSpec((gr, gc, d), lambda i: (i // ncg, i % ncg, 0))
    out = pl.pallas_call(
        _proj_res_kernel,
        grid=(n_lat // ROW_TILE,),
        in_specs=[
            pl.BlockSpec((gr, gc * k), lambda i: (i // ncg, i % ncg)),
            tok_spec,
            pl.BlockSpec((None, N_MOD, d), lambda i: ((i // ncg) * gr // rows, 0, 0)),
            pl.BlockSpec((k, d), lambda i: (0, 0)),
        ],
        out_specs=tok_spec,
        out_shape=jax.ShapeDtypeStruct((n_lat // GRID_W, GRID_W, d), F32),
        scratch_shapes=[pltpu.VMEM((k // LANE, ROW_TILE, LANE), F32)],
        compiler_params=_cparams(("parallel",)),
    )(y_grid, x3, mods_l, w_bf)
    return out.reshape(n_lat, d)


def _head_sum(z, e_bf):
    parts = []
    for gi in range(z.shape[1] // GROUP_W):
        zs = z[:, gi * GROUP_W:(gi + 1) * GROUP_W]
        parts.append(_dot_exact_rhs(zs, e_bf))
    return jnp.concatenate(parts, axis=1)


def _rwkv_prep_kernel(p_ref, pp_ref, pn_ref, mu_ref, w0_ref, a0_ref, w2_ref, a2_ref, g2_ref,
                      kk_ref, ka_ref, rk_ref, e_ref,
                      r_o, v_o, kk_o, lw0_o, lw1_o, kd0_o, kd1_o, b0_o, b1_o, g_o, bonus_o,
                      *, n_lat_tiles, lat_tiles_per_seq, ctx_tiles_per_seq, aw):
    i = pl.program_id(0)
    tp = p_ref.shape[0]
    in_ctx = i >= n_lat_tiles
    pos = jnp.where(in_ctx, (i - n_lat_tiles) % ctx_tiles_per_seq, i % lat_tiles_per_seq)
    last = jnp.where(in_ctx, ctx_tiles_per_seq - 1, lat_tiles_per_seq - 1)
    is_first = pos == 0
    is_last = pos == last
    rowid = lax.broadcasted_iota(jnp.int32, (tp, 1), 0)

    def shifted(c0, c1):
        x = p_ref[:, c0:c1]
        prow = jnp.where(is_first, 0.0, pp_ref[SUBLANE - 1:SUBLANE, c0:c1])
        nrow = jnp.where(is_last, 0.0, pn_ref[0:1, c0:c1])
        xp = jnp.where(rowid == 0, prow, pltpu.roll(x, 1, 0))
        xn = jnp.where(rowid == tp - 1, nrow, pltpu.roll(x, tp - 1, 0))
        return x + mu_ref[0:1, c0:c1] * (xp - x) + mu_ref[1:2, c0:c1] * (xn - x)

    e_bf = e_ref[...]
    r = shifted(0, aw)
    k = shifted(aw, 2 * aw)
    v = shifted(2 * aw, 3 * aw)
    gd = shifted(3 * aw, 3 * aw + A_LORA_G)
    lo0 = 3 * aw + A_LORA_G
    xl = shifted(lo0, lo0 + 2 * A_LORA_W + 2 * A_LORA_A)

    r_o[...] = r.astype(r_o.dtype)
    v_o[...] = v.astype(v_o.dtype)
    g_o[...] = _dot(_bf(_sigmoid(gd)), g2_ref[...])

    kx = k * kk_ref[...]
    kn = jnp.sqrt(_head_sum(kx * kx, e_bf))
    kk = kx / jnp.maximum(kn, 1e-12)
    kk_o[...] = kk.astype(kk_o.dtype)

    xl_t = _bf(jnp.tanh(xl))
    xl_b = _bf(xl)
    kd_sum = None
    for d, (lw_o, kd_o, b_o) in enumerate(((lw0_o, kd0_o, b0_o), (lw1_o, kd1_o, b1_o))):
        w_log = -_softplus(-(w0_ref[d:d + 1, :] + _dot(xl_t, w2_ref[d]))) - 0.5
        lw_o[...] = -jnp.exp(w_log)
        asig = _sigmoid(a0_ref[d:d + 1, :] + _dot(xl_b, a2_ref[d]))
        kd = k * (1.0 + (asig - 1.0) * ka_ref[...])
        kd_o[...] = kd.astype(kd_o.dtype)
        b_o[...] = (kk * asig).astype(b_o.dtype)
        kd_sum = kd if kd_sum is None else kd_sum + kd
    bonus_o[...] = _head_sum(r * rk_ref[...] * kd_sum, e_bf) * v


def _rwkv_prep(pa, mu_p, w0, a0, w2p, a2p, g2, k_k, k_a, r_k, e_bf, *, n_lat, seq, ctx_len):
    r = pa.shape[0]
    ncol = mu_p.shape[1]
    aw = w0.shape[1]
    tp = PREP_TILE
    hb = tp // SUBLANE
    nblk8 = r // SUBLANE
    lw = 2 * A_LORA_W + 2 * A_LORA_A
    row = lambda a: a.reshape(1, aw)
    full = lambda shape: pl.BlockSpec(shape, lambda i: (0,) * len(shape))
    out_spec = pl.BlockSpec((tp, aw), lambda i: (i, 0))
    out_dtypes = [BF16, BF16, BF16, F32, F32, BF16, BF16, BF16, BF16, F32, F32]
    return pl.pallas_call(
        functools.partial(_rwkv_prep_kernel, n_lat_tiles=n_lat // tp, lat_tiles_per_seq=seq // tp,
                          ctx_tiles_per_seq=ctx_len // tp, aw=aw),
        grid=(r // tp,),
        in_specs=[
            pl.BlockSpec((tp, ncol), lambda i: (i, 0)),
            pl.BlockSpec((SUBLANE, ncol), lambda i: (jnp.maximum(i * hb - 1, 0), 0)),
            pl.BlockSpec((SUBLANE, ncol), lambda i: (jnp.minimum((i + 1) * hb, nblk8 - 1), 0)),
            full((2, ncol)), full((2, aw)), full((2, aw)),
            full((2, lw, aw)), full((2, lw, aw)), full((A_LORA_G, aw)),
            full((1, aw)), full((1, aw)), full((1, aw)), full((GROUP_W, GROUP_W)),
        ],
        out_specs=[out_spec] * 11,
        out_shape=[jax.ShapeDtypeStruct((r, aw), dt) for dt in out_dtypes],
        compiler_params=_cparams(("parallel",)),
    )(pa, pa, pa, mu_p, w0, a0, w2p, a2p, g2, row(k_k), row(k_a), row(r_k), e_bf)


def _rwkv_masks(reverse):
    L = CHUNK
    gw = GROUP_W
    row = lax.broadcasted_iota(jnp.int32, (gw, gw), 0)
    col = lax.broadcasted_iota(jnp.int32, (gw, gw), 1)
    same_head = (row >> 6) == (col >> 6)
    tr = row & (L - 1)
    tc = col & (L - 1)
    strict = same_head & ((tr < tc) if reverse else (tr > tc))
    incl = same_head & ((tr <= tc) if reverse else (tr >= tc))
    t_r = lax.broadcasted_iota(jnp.int32, (L, L), 0)
    t_c = lax.broadcasted_iota(jnp.int32, (L, L), 1)
    tri = _bf(jnp.where((t_c >= t_r) if reverse else (t_c <= t_r), 1.0, 0.0))
    return dict(same_head=same_head, strict=strict, incl2=jnp.concatenate([incl, incl], axis=1),
                eye=row == col, tri=tri)


def _rwkv_chunks(probs, side=None):
    L = CHUNK
    gw = GROUP_W
    tick = (lambda: None) if side is None else (lambda: next(side, None))
    masks = {rv: _rwkv_masks(rv) for rv in sorted({p[7] for p in probs})}
    mk = [masks[p[7]] for p in probs]
    each = lambda f, *cols: [f(*args) for args in zip(*cols)]
    rep = lambda x: jnp.concatenate([x] * HEADS_PER_GROUP, axis=0)
    nat = lambda x: x[0:L] + x[L:2 * L] + x[2 * L:3 * L] + x[3 * L:4 * L]

    c = [_dot_exact_lhs(m["tri"], p[3]) for m, p in zip(mk, probs)]
    c_l = [ci[(0 if p[7] else L - 1):(1 if p[7] else L), :] for ci, p in zip(c, probs)]

    at_bf, r_nat, v_bf, bk_end, lhs, rhs, e_l = [], [], [], [], [], [], []
    for (r, v, kk, lw, kd, b, _, _), m, ci, cl in zip(probs, mk, c, c_l):
        bd = lambda x, sh=m["same_head"]: _bf(jnp.where(sh, rep(x), 0.0))
        e_nc = jnp.exp(-ci)
        e_end = jnp.exp(cl - ci)
        a_ = bd(-kk * jnp.exp(ci - lw))
        r_s = r * jnp.exp(ci)
        at_bf.append(a_)
        r_nat.append(_bf(r_s))
        v_bf.append(bd(v))
        bk_end.append(jnp.concatenate([bd(b * e_end), bd(kd * e_end)], axis=0))
        lhs.append(jnp.concatenate([a_, bd(r_s)], axis=0))
        rhs.append(jnp.concatenate([rep(_bf(b * e_nc)), rep(_bf(kd * e_nc))], axis=0))
        e_l.append(jnp.exp(cl))

    t, apb, a_ak, rbk = [], [], [], []
    for m, l_, r_ in zip(mk, lhs, rhs):
        s = _dot_nt(l_, r_)
        a_ab = jnp.where(m["strict"], s[:gw, :gw], 0.0)
        t.append(jnp.where(m["eye"], 1.0, a_ab))
        apb.append(_bf(a_ab))
        a_ak.append(_bf(jnp.where(m["strict"], s[:gw, gw:], 0.0)))
        rbk.append(_bf(jnp.where(m["incl2"], s[gw:, :], 0.0)))
    tick()
    akv = each(_dot, a_ak, v_bf)
    tb = [_bf(ti) for ti in t]
    for _ in range(5):
        apb = [_bf(_dot(a, a)) for a in apb]
        tick()
        t = [ti + _dot(tbi, a) for ti, tbi, a in zip(t, tb, apb)]
        tb = [_bf(ti) for ti in t]
    h_bf = [_bf(p[6]) for p in probs]
    xb = [_bf(_dot(a_, h) + ak) for a_, h, ak in zip(at_bf, h_bf, akv)]
    tick()
    uv = [jnp.concatenate([_bf(_dot(tbi, xi)), v], axis=0)
          for tbi, xi, v in zip(tb, xb, v_bf)]
    tick()
    y = [_dot(rn, h) + nat(_dot(rbk_i, uvi)) for rn, h, rbk_i, uvi in zip(r_nat, h_bf, rbk, uv)]
    e_col = [jnp.broadcast_to(el, (gw, gw)).T for el in e_l]
    h_new = [ec * p[6] + _dot_tn(bke, uvi) for ec, p, bke, uvi in zip(e_col, probs, bk_end, uv)]
    if side is not None:
        for _ in side:
            pass
    return list(zip(y, h_new))


def _chunk_row_block(b, i, *, reverse, n_lat, seq, ctx_len):
    nc = ctx_len // CHUNK
    nl = seq // CHUNK
    ci = (nc - 1 - i) if reverse else i
    li = (nl - 1 - (i - nc)) if reverse else (i - nc)
    return jnp.where(i < nc, (n_lat + b * ctx_len) // CHUNK + ci, b * nl + li)


def _gla_stages(qkf, vf, gdf, qkb, vb, gdb, gw2_ref, gb_ref, of_ref, ob_ref, sf_ref, sb_ref):
    dv, dk = sf_ref.shape[1:]
    L = CHUNK
    wk = B_HEADS * dk
    t_r = lax.broadcasted_iota(jnp.int32, (L, L), 0)
    t_c = lax.broadcasted_iota(jnp.int32, (L, L), 1)

    qh, kh, keh, vh, dech, keeph, sth = [], [], [], [], [], [], []
    for d, (qk_ref, v_ref, gd_ref, st_ref) in enumerate(((qkf, vf, gdf, sf_ref),
                                                         (qkb, vb, gdb, sb_ref))):
        reverse = d == 1
        keep = (t_c >= t_r) if reverse else (t_c <= t_r)
        tri = _bf(jnp.where(keep, 1.0, 0.0))
        z = _dot(_bf(gd_ref[...]), gw2_ref[d]) + gb_ref[d:d + 1, :]
        la = -_softplus(-z) * (1.0 / B_GATE_TAU)
        cum = _dot_exact_lhs(tri, la)
        last = 0 if reverse else L - 1
        cl = cum[last:last + 1, :]
        q_in = _bf(qk_ref[:, :wk] * (dk ** -0.5) * jnp.exp(cum))
        k_in = _bf(qk_ref[:, wk:] * jnp.exp(-cum))
        k_end = _bf(qk_ref[:, wk:] * jnp.exp(cl - cum))
        dec = jnp.exp(cl)
        for h in range(B_HEADS):
            ks = slice(h * dk, (h + 1) * dk)
            qh.append(q_in[:, ks])
            kh.append(k_in[:, ks])
            keh.append(k_end[:, ks])
            vh.append(_bf(v_ref[:, h * dv:(h + 1) * dv]))
            dech.append(dec[:, ks])
            keeph.append(keep)
            sth.append(st_ref[h])
        yield

    sc = [_bf(jnp.where(kp, _dot_nt(q, k), 0.0)) for kp, q, k in zip(keeph, qh, kh)]
    yield
    inter = [_dot_nt(q, _bf(st)) for q, st in zip(qh, sth)]
    yield
    kvt = [_dot_tn(v, ke) for v, ke in zip(vh, keh)]
    yield
    out = [_dot(s_, v) + it for s_, v, it in zip(sc, vh, inter)]
    yield
    for d, (o_ref, st_ref) in enumerate(((of_ref, sf_ref), (ob_ref, sb_ref))):
        for h in range(B_HEADS):
            i = d * B_HEADS + h
            o_ref[:, h * dv:(h + 1) * dv] = out[i]
            st_ref[h] = sth[i] * dech[i] + kvt[i]


def _mix_scan_kernel(rf, vf, kkf, lwf, kdf, bf_, rb, vb, kkb, lwb, kdb, bb,
                     qkf, gvf, gdf, qkb, gvb, gdb, gw2_ref, gb_ref,
                     yf_ref, yb_ref, of_ref, ob_ref, hf_ref, hb_ref, sf_ref, sb_ref):
    @pl.when(pl.program_id(1) == 0)
    def _():
        for ref in (hf_ref, hb_ref, sf_ref, sb_ref):
            ref[...] = jnp.zeros_like(ref)

    gs = hf_ref.shape[0]
    probs = []
    for gi in range(gs):
        sl = slice(gi * GROUP_W, (gi + 1) * GROUP_W)
        f32 = lambda ref: ref[:, sl].astype(F32)
        probs.append((f32(rf), f32(vf), f32(kkf), lwf[:, sl], f32(kdf), f32(bf_), hf_ref[gi], False))
        probs.append((f32(rb), f32(vb), f32(kkb), lwb[:, sl], f32(kdb), f32(bb), hb_ref[gi], True))
    gla = _gla_stages(qkf, gvf, gdf, qkb, gvb, gdb, gw2_ref, gb_ref, of_ref, ob_ref, sf_ref,
                      sb_ref)
    res = _rwkv_chunks(probs, side=gla)
    for gi in range(gs):
        sl = slice(gi * GROUP_W, (gi + 1) * GROUP_W)
        yf_ref[:, sl], hf_ref[gi] = res[2 * gi]
        yb_ref[:, sl], hb_ref[gi] = res[2 * gi + 1]


def _mix_scan(r, v, kk, lw0, lw1, kd0, kd1, b0, b1, pb, gw2p, gb, *, batch, n_lat, seq, ctx_len,
              wk, wv, col0):
    rows, aw = r.shape
    assert aw == GROUP_W * RWKV_GROUPS_PER_STEP
    dk, dv = wk // B_HEADS, wv // B_HEADS
    geo = dict(n_lat=n_lat, seq=seq, ctx_len=ctx_len)
    gd_blk = (col0 + 2 * wk + 2 * wv) // LANE

    def specs(reverse):
        rb = lambda b, i: _chunk_row_block(b, i, reverse=reverse, **geo)
        rw = pl.BlockSpec((CHUNK, aw), lambda b, i: (rb(b, i), 0))
        gla = [
            pl.BlockSpec((CHUNK, 2 * wk), lambda b, i: (rb(b, i), col0 // (2 * wk))),
            pl.BlockSpec((CHUNK, wv), lambda b, i: (rb(b, i), (col0 + 2 * wk) // wv)),
            pl.BlockSpec((CHUNK, LANE), lambda b, i: (rb(b, i), gd_blk)),
        ]
        out = pl.BlockSpec((CHUNK, wv), lambda b, i: (rb(b, i), 0))
        return rw, gla, out

    rwf, glaf, outf = specs(False)
    rwb, glab, outb = specs(True)
    sds_a = jax.ShapeDtypeStruct((rows, aw), F32)
    sds_b = jax.ShapeDtypeStruct((rows, wv), F32)
    gs = RWKV_GROUPS_PER_STEP
    return pl.pallas_call(
        _mix_scan_kernel,
        grid=(batch, (seq + ctx_len) // CHUNK),
        in_specs=[rwf] * 6 + [rwb] * 6 + glaf + glab + [
            pl.BlockSpec((2, LANE, wk), lambda b, i: (0, 0, 0)),
            pl.BlockSpec((2, wk), lambda b, i: (0, 0)),
        ],
        out_specs=[rwf, rwb, outf, outb],
        out_shape=[sds_a, sds_a, sds_b, sds_b],
        scratch_shapes=[pltpu.VMEM((gs, GROUP_W, GROUP_W), F32),
                        pltpu.VMEM((gs, GROUP_W, GROUP_W), F32),
                        pltpu.VMEM((B_HEADS, dv, dk), F32), pltpu.VMEM((B_HEADS, dv, dk), F32)],
        compiler_params=_cparams(("parallel", "arbitrary")),
    )(r, v, kk, lw0, kd0, b0, r, v, kk, lw1, kd1, b1, pb, pb, pb, pb, pb, pb, gw2p, gb)


def _mixer_out_kernel(yf, yb, bonus, g, of, ob, gate, xa_ref, xb_ref, mod_ref, lnw, lnb, gn, e_ref,
                      w_ref, o_ref, h_ref, *, aw, dv, n_a_tiles):
    e_bf = e_ref[...]
    y = yf[...] + yb[...]
    inv = 1.0 / A_HEAD_DIM
    mean = _head_sum(y, e_bf) * inv
    dlt = y - mean
    var = _head_sum(dlt * dlt, e_bf) * inv
    yn = dlt * lax.rsqrt(var + A_LN_EPS) * lnw[...] + lnb[...]
    h_ref[:, :aw] = _bf((yn + bonus[...]) * g[...])

    o = of[...] + ob[...]
    gt = _silu(gate[...])
    for h in range(B_HEADS):
        sl = slice(h * dv, (h + 1) * dv)
        oh = o[:, sl]
        ms = jnp.mean(oh * oh, axis=-1, keepdims=True)
        h_ref[:, aw + h * dv:aw + (h + 1) * dv] = _bf(oh * lax.rsqrt(ms + NORM_EPS) * gn[...] * gt[:, sl])

    x = jnp.where(pl.program_id(0) < n_a_tiles, xa_ref[...], xb_ref[...])
    o_ref[...] = x + mod_ref[2:3, :] * _dot(h_ref[...], w_ref[...])


def _mixer_out(yf, yb, bonus, g, of, ob, pb, xa, xb, mods_l, ln_w, ln_b, gla_norm, e_bf, w_bf,
               seg_of_tile, *, wk, col0):
    d = xa.shape[1]
    rows = xa.shape[0] + xb.shape[0]
    aw = yf.shape[1]
    wv = of.shape[1]
    dv = wv // B_HEADS
    tm = 256
    tiles_per_row_tile = ROW_TILE // tm
    n_a_tiles = xa.shape[0] // tm
    rs = lambda w: pl.BlockSpec((tm, w), lambda i: (i, 0))
    full = lambda shape: pl.BlockSpec(shape, lambda i: (0,) * len(shape))
    return pl.pallas_call(
        functools.partial(_mixer_out_kernel, aw=aw, dv=dv, n_a_tiles=n_a_tiles),
        grid=(rows // tm,),
        in_specs=[
            rs(aw), rs(aw), rs(aw), rs(aw), rs(wv), rs(wv),
            pl.BlockSpec((tm, wv), lambda i: (i, (col0 + 2 * wk + wv) // wv)),
        ] + _two_source_specs(tm, d, n_a_tiles) + [
            pl.BlockSpec((None, N_MOD, d), lambda i: (seg_of_tile(i // tiles_per_row_tile), 0, 0)),
            full((1, aw)), full((1, aw)), full((1, dv)), full((GROUP_W, GROUP_W)),
            full((aw + wv, d)),
        ],
        out_specs=rs(d),
        out_shape=jax.ShapeDtypeStruct((rows, d), F32),
        scratch_shapes=[pltpu.VMEM((tm, aw + wv), BF16)],
        compiler_params=_cparams(("parallel",)),
    )(yf, yb, bonus, g, of, ob, pb, xa, xb, mods_l, ln_w.reshape(1, aw), ln_b.reshape(1, aw),
      gla_norm.reshape(1, dv), e_bf, w_bf)


def _lru_kernel(*refs, reverse, n_blk, combine):
    if combine:
        (x_ref, xp_ref, xn_ref, h0_ref, cw_ref, cb_ref, wc_ref, ba_ref, bx_ref, lam_ref,
         hf_ref, gate_ref, out_ref, hl_ref, a_sc, u_sc, hs_sc, carry_sc) = refs
    else:
        (x_ref, xp_ref, xn_ref, h0_ref, cw_ref, cb_ref, wc_ref, ba_ref, bx_ref, lam_ref,
         out_ref, hl_ref, a_sc, u_sc, hs_sc, carry_sc) = refs
    j = pl.program_id(2)
    blk = (n_blk - 1 - j) if reverse else j
    lb, ct = x_ref.shape
    cblk = wc_ref.shape[1]
    is_first = blk == 0
    is_last = blk == n_blk - 1
    rowid = lax.broadcasted_iota(jnp.int32, (lb, 1), 0)

    x = x_ref[...]
    p6 = jnp.where(is_first, 0.0, xp_ref[SUBLANE - 2:SUBLANE - 1, :])
    p7 = jnp.where(is_first, 0.0, xp_ref[SUBLANE - 1:SUBLANE, :])
    n0 = jnp.where(is_last, 0.0, xn_ref[0:1, :])
    xm1 = jnp.where(rowid == 0, p7, pltpu.roll(x, 1, 0))
    xm2 = jnp.where(rowid == 0, p6, jnp.where(rowid == 1, p7, pltpu.roll(x, 2, 0)))
    xp1 = jnp.where(rowid == lb - 1, n0, pltpu.roll(x, lb - 1, 0))
    xs = (cb_ref[...] + cw_ref[0:1, :] * xm2 + cw_ref[1:2, :] * xm1 + cw_ref[2:3, :] * x
          + cw_ref[3:4, :] * xp1)

    for n in range(ct // cblk):
        sl = slice(n * cblk, (n + 1) * cblk)
        xb = xs[:, sl]
        ri = _dot(_bf(xb), wc_ref[n])
        rg = _sigmoid(ri[:, :cblk] + ba_ref[:, sl])
        ig = _sigmoid(ri[:, cblk:] + bx_ref[:, sl])
        log_a = -C_CONST * rg * _softplus(-lam_ref[:, sl])
        a = jnp.exp(log_a)
        a_sc[:, sl] = a
        u_sc[:, sl] = jnp.sqrt(1.0 - a * a) * (ig * xb)

    @pl.when(j == 0)
    def _():
        carry_sc[...] = h0_ref[0:1, :]

    def step(t, h):
        tt = (lb - 1 - t) if reverse else t
        h = a_sc[pl.ds(tt, 1), :] * h + u_sc[pl.ds(tt, 1), :]
        hs_sc[pl.ds(tt, 1), :] = h
        return h

    h = lax.fori_loop(0, lb, step, carry_sc[...], unroll=8)
    carry_sc[...] = h
    hl_ref[...] = jnp.broadcast_to(h, hl_ref.shape)
    if combine:
        out_ref[...] = (hs_sc[...] + hf_ref[...]) * jax.nn.gelu(gate_ref[...])
    else:
        out_ref[...] = hs_sc[...]


def _lru_pass(x_arr, x_specs, h0, conv_w, conv_b, wcat, ba, bx, lam, out_shape, out_spec, extra,
              *, batch, n_blk, lb, reverse):
    c = conv_w.shape[1]
    ct = LRU_CHANNEL_TILE
    nct = c // ct
    cblk = wcat.shape[1]
    chan = lambda rows_: pl.BlockSpec((rows_, ct), lambda b, k, j: (0, k))
    in_specs = list(x_specs) + [
        pl.BlockSpec((SUBLANE, ct), lambda b, k, j: (b, k)),
        chan(C_CONV), chan(1),
        pl.BlockSpec((ct // cblk, cblk, 2 * cblk), lambda b, k, j: (k, 0, 0)),
        chan(1), chan(1), chan(1),
    ] + [s for _, s in extra]
    args = [x_arr, x_arr, x_arr, h0, conv_w, conv_b.reshape(1, c), wcat, ba.reshape(1, c),
            bx.reshape(1, c), lam.reshape(1, c)] + [a for a, _ in extra]
    return pl.pallas_call(
        functools.partial(_lru_kernel, reverse=reverse, n_blk=n_blk, combine=bool(extra)),
        grid=(batch, nct, n_blk),
        in_specs=in_specs,
        out_specs=[out_spec, pl.BlockSpec((SUBLANE, ct), lambda b, k, j: (b, k))],
        out_shape=[out_shape, jax.ShapeDtypeStruct((batch * SUBLANE, c), F32)],
        scratch_shapes=[pltpu.VMEM((lb, ct), F32), pltpu.VMEM((lb, ct), F32),
                        pltpu.VMEM((lb, ct), F32), pltpu.VMEM((1, ct), F32)],
        compiler_params=_cparams(("parallel", "parallel", "arbitrary")),
    )(*args)


def _rglru(hw_ctx, gate_grid, x_grid, conv_w, conv_b, wa, ba, wx, bx, lam, *, batch, n_lat, seq,
           ctx_len):
    c = conv_w.shape[1]
    ct = LRU_CHANNEL_TILE
    rows = seq // GRID_W
    lbc = 128
    ncb = ctx_len // lbc
    xcol = c // ct
    per_col = c // ct

    wcat = [_bf(jnp.concatenate([wa[d], wx[d]], axis=-1)) for d in range(2)]
    zeros_state = jnp.zeros((batch * SUBLANE, c), F32)

    def ctx_specs(reverse):
        blk = (lambda j: ncb - 1 - j) if reverse else (lambda j: j)
        per8 = lbc // SUBLANE
        cur = pl.BlockSpec((lbc, ct), lambda b, k, j: (b * ncb + blk(j), xcol + k))
        prv = pl.BlockSpec((SUBLANE, ct), lambda b, k, j: (
            b * ncb * per8 + jnp.maximum(blk(j) * per8 - 1, 0), xcol + k))
        nxt = pl.BlockSpec((SUBLANE, ct), lambda b, k, j: (
            b * ncb * per8 + jnp.minimum((blk(j) + 1) * per8, ncb * per8 - 1), xcol + k))
        return cur, prv, nxt

    def lat_specs(reverse):
        blk = (lambda j: GRID_W - 1 - j) if reverse else (lambda j: j)
        r8 = rows // SUBLANE
        cur = pl.BlockSpec((rows, ct), lambda b, k, j: (b, blk(j) * per_col + k))
        prv = pl.BlockSpec((SUBLANE, ct), lambda b, k, j: (
            b * r8 + r8 - 1, jnp.maximum(blk(j) - 1, 0) * per_col + k))
        nxt = pl.BlockSpec((SUBLANE, ct), lambda b, k, j: (
            b * r8, jnp.minimum(blk(j) + 1, GRID_W - 1) * per_col + k))
        return cur, prv, nxt

    common = dict(batch=batch)
    ctx_scratch_sds = jax.ShapeDtypeStruct((batch * ctx_len, c), F32)
    ctx_out = lambda reverse: pl.BlockSpec(
        (lbc, ct), lambda b, k, j: (b * ncb + ((ncb - 1 - j) if reverse else j), k))
    _, st = _lru_pass(hw_ctx, ctx_specs(False), zeros_state, conv_w, conv_b, wcat[0], ba[0], bx[0],
                      lam[0], ctx_scratch_sds, ctx_out(False), [], n_blk=ncb, lb=lbc,
                      reverse=False, **common)
    hf_sds = jax.ShapeDtypeStruct((n_lat, c), F32)
    hf_spec = lambda reverse: pl.BlockSpec(
        (rows, ct), lambda b, k, j: (b * GRID_W + ((GRID_W - 1 - j) if reverse else j), k))
    hf, _ = _lru_pass(x_grid, lat_specs(False), st, conv_w, conv_b, wcat[0], ba[0], bx[0], lam[0],
                      hf_sds, hf_spec(False), [], n_blk=GRID_W, lb=rows, reverse=False, **common)
    _, st = _lru_pass(hw_ctx, ctx_specs(True), zeros_state, conv_w, conv_b, wcat[1], ba[1], bx[1],
                      lam[1], ctx_scratch_sds, ctx_out(True), [], n_blk=ncb, lb=lbc,
                      reverse=True, **common)
    grid_spec = pl.BlockSpec((rows, ct), lambda b, k, j: (b, (GRID_W - 1 - j) * per_col + k))
    y_sds = jax.ShapeDtypeStruct((n_lat // GRID_W, GRID_W * c), F32)
    y, _ = _lru_pass(x_grid, lat_specs(True), st, conv_w, conv_b, wcat[1], ba[1], bx[1], lam[1],
                     y_sds, grid_spec, [(hf, hf_spec(True)), (gate_grid, grid_spec)],
                     n_blk=GRID_W, lb=rows, reverse=True, **common)
    return y


def _pad_cols(w, n):
    return jnp.pad(w, ((0, 0),) * (w.ndim - 1) + ((0, n - w.shape[-1]),))


def kernel(x, c, ctx, c_ctx, mod_w, mod_b, norm1, norm2, mlp_w1, mlp_w2, ab_w_in, ab_w_out, rw_mu,
           rw_w0, rw_w2, rw_a0, rw_a2, rw_g2, rw_kk, rw_ka, rw_rk, rw_ln_w, rw_ln_b, gla_gw2,
           gla_gb, gla_norm, lru_w_in, lru_w_out, lru_conv_w, lru_conv_b, lru_wa, lru_ba, lru_wx,
           lru_bx, lru_lam, final_norm):
    batch, seq, d = x.shape
    ctx_len = ctx.shape[1]
    depth = mod_w.shape[0]
    n_lat = batch * seq
    aw = rw_w0.shape[-1]
    wk = gla_gb.shape[-1]
    wv = ab_w_out.shape[1] - aw
    a_cols = rw_mu.shape[-1]
    b_cols = ab_w_in.shape[-1] - a_cols
    assert depth == 2 and batch + 1 <= SUBLANE
    assert seq % ROW_TILE == 0 and (batch * ctx_len) % ROW_TILE == 0
    assert ctx_len % PREP_TILE == 0 and ctx_len % 128 == 0 and seq % (GRID_W * SUBLANE) == 0
    assert aw % GROUP_W == 0 and wv == aw and 2 * wk == wv

    tiles_per_seq = seq // ROW_TILE
    seg_of_tile = lambda i: jnp.minimum(i // tiles_per_seq, batch)

    x2 = x.reshape(n_lat, d)
    ctx2 = ctx.reshape(batch * ctx_len, d)
    cvec = jnp.concatenate(
        [c, c_ctx[None, :], jnp.zeros((SUBLANE - batch - 1, d), F32)], axis=0)
    mods = _mods(cvec, mod_w, mod_b)

    hid = jnp.arange(GROUP_W) // A_HEAD_DIM
    e_bf = _bf(hid[:, None] == hid[None, :])

    a_pad = -(-a_cols // LANE) * LANE
    col0 = -(-a_cols // (2 * wk)) * (2 * wk)
    n_tiles = 3
    tn = -(-(col0 + b_cols) // (n_tiles * LANE)) * LANE
    w_ab = _bf(jnp.concatenate(
        [_pad_cols(ab_w_in[0][:, :a_cols], col0),
         _pad_cols(ab_w_in[0][:, a_cols:], n_tiles * tn - col0)], axis=1))
    pa = pb = _norm_matmul(x2, ctx2, norm1[0], mods[0], w_ab, seg_of_tile, tn=tn)

    lora_w = 2 * A_LORA_W + 2 * A_LORA_A
    w2p = jnp.zeros((2, lora_w, aw), F32)
    a2p = jnp.zeros((2, lora_w, aw), F32)
    for dd in range(2):
        w2p = w2p.at[dd, dd * A_LORA_W:(dd + 1) * A_LORA_W].set(rw_w2[0, dd])
        a2p = a2p.at[dd, 2 * A_LORA_W + dd * A_LORA_A:2 * A_LORA_W + (dd + 1) * A_LORA_A].set(
            rw_a2[0, dd])
    (r_, v_, kk_, lw0, lw1, kd0, kd1, b0, b1, g_, bonus) = _rwkv_prep(
        pa, _pad_cols(rw_mu[0], a_pad), rw_w0[0], rw_a0[0], _bf(w2p), _bf(a2p), _bf(rw_g2[0]),
        rw_kk[0], rw_ka[0], rw_rk[0].reshape(-1), e_bf, n_lat=n_lat, seq=seq, ctx_len=ctx_len)
    gw2p = jnp.zeros((2, LANE, wk), F32)
    for dd in range(2):
        gw2p = gw2p.at[dd, dd * B_LORA:(dd + 1) * B_LORA].set(gla_gw2[0, dd])
    yf, yb, of, ob = _mix_scan(r_, v_, kk_, lw0, lw1, kd0, kd1, b0, b1, pb, _bf(gw2p), gla_gb[0],
                               batch=batch, n_lat=n_lat, seq=seq, ctx_len=ctx_len, wk=wk, wv=wv,
                               col0=col0)

    xs = _mixer_out(yf, yb, bonus, g_, of, ob, pb, x2, ctx2, mods[0], rw_ln_w[0], rw_ln_b[0],
                    gla_norm[0], e_bf, _bf(ab_w_out[0]), seg_of_tile, wk=wk, col0=col0)
    seg_of_mlp_tile = lambda i: jnp.minimum(i // (seq // MLP_TILE), batch)
    ctx_tiles = (batch * ctx_len) // ROW_TILE
    w1_bf, w2_bf = _bf(mlp_w1[0]), _bf(mlp_w2[0])
    x_lat = _mlp(xs, norm2[0], mods[0], w1_bf, w2_bf, final_norm, seg_of_mlp_tile, tm=MLP_TILE,
                 tile0=0, n_tiles=n_lat // MLP_TILE, final_norm=False)
    x_ctx = _mlp(xs, norm2[0], mods[0], w1_bf, w2_bf, final_norm, seg_of_tile, tm=ROW_TILE,
                 tile0=n_lat // ROW_TILE, n_tiles=ctx_tiles, final_norm=False)

    w_lru = _bf(lru_w_in[0])
    hw_ctx = _norm_matmul(x_ctx, None, norm1[1], mods[1], w_lru, lambda i: batch, tn=2048)
    gate_grid, x_grid = _norm_matmul_grid(x_lat, n_lat, seq // GRID_W, norm1[1], mods[1], w_lru)
    y = _rglru(hw_ctx, gate_grid, x_grid, lru_conv_w[0], lru_conv_b[0], lru_wa[0], lru_ba[0],
               lru_wx[0], lru_bx[0], lru_lam[0], batch=batch, n_lat=n_lat, seq=seq,
               ctx_len=ctx_len)
    xl = _proj_res(y, x_lat, n_lat, seq // GRID_W, mods[1], _bf(lru_w_out[0]))
    out = _mlp(xl, norm2[1], mods[1], _bf(mlp_w1[1]), _bf(mlp_w2[1]), final_norm, seg_of_mlp_tile,
               tm=MLP_TILE, tile0=0, n_tiles=n_lat // MLP_TILE, final_norm=True)
    return out.reshape(batch, seq, d)
```

```python
import functools

import jax
import jax.numpy as jnp
from jax import lax
from jax.experimental import pallas as pl
from jax.experimental.pallas import tpu as pltpu

F32 = jnp.float32
BF16 = jnp.bfloat16

NORM_EPS = 1e-6
GRID_W = 64
N_MOD = 6

A_HEAD_DIM = 64
A_LORA_W = 96
A_LORA_A = 96
A_LORA_G = 256
A_LN_EPS = 64e-5
HEADS_PER_GROUP = 4
GROUP_W = HEADS_PER_GROUP * A_HEAD_DIM
CHUNK = 64
RWKV_GROUPS_PER_STEP = 4

B_HEADS = 4
B_LORA = 16
B_GATE_TAU = 16.0

C_BLOCKS = 8
C_CONV = 4
C_CONST = 8.0

LANE = 128
SUBLANE = 8
ROW_TILE = 512
MLP_TILE = 1024
MLP_STEP_ELEMS = 1024 * 512
LRU_CHANNEL_TILE = 2048
LRU_COLS_PER_STEP = 4
PREP_TILE = 128
VMEM_LIMIT = 56 * 1024 * 1024


def _cparams(sem):
    return pltpu.CompilerParams(dimension_semantics=sem, vmem_limit_bytes=VMEM_LIMIT)


def _bf(x):
    return x.astype(BF16)


def _dot(a, b):
    return jnp.dot(a, b, preferred_element_type=F32)


def _dot_nt(a, b):
    return lax.dot_general(a, b, (((1,), (1,)), ((), ())), preferred_element_type=F32)


def _dot_tn(a, b):
    return lax.dot_general(a, b, (((0,), (0,)), ((), ())), preferred_element_type=F32)


def _split3(x):
    hi = _bf(x)
    r1 = x - hi.astype(F32)
    mid = _bf(r1)
    lo = _bf(r1 - mid.astype(F32))
    return hi, mid, lo


def _dot_exact_lhs(m_bf, x):
    hi, mid, lo = _split3(x)
    return _dot(m_bf, hi) + _dot(m_bf, mid) + _dot(m_bf, lo)


def _dot_exact_rhs(x, m_bf):
    hi, mid, lo = _split3(x)
    return _dot(hi, m_bf) + _dot(mid, m_bf) + _dot(lo, m_bf)


def _softplus(x):
    return jnp.maximum(x, 0.0) + jnp.log(1.0 + jnp.exp(-jnp.abs(x)))


def _sigmoid(x):
    return 0.5 * jnp.tanh(0.5 * x) + 0.5


def _silu(x):
    return x * _sigmoid(x)


def _norm_mod(x, g, shift, scale):
    ms = jnp.mean(x * x, axis=-1, keepdims=True)
    return (x * lax.rsqrt(ms + NORM_EPS) * g) * (1.0 + scale) + shift


def _mods_kernel(c_ref, w_ref, b_ref, o_ref):
    s = _silu(c_ref[...])
    o_ref[...] = _dot(_bf(s), _bf(w_ref[...])) + b_ref[...]


def _mods(cvec, mod_w, mod_b):
    depth, d, n = mod_w.shape
    tn = 1024
    out = pl.pallas_call(
        _mods_kernel,
        grid=(depth, n // tn),
        in_specs=[
            pl.BlockSpec((SUBLANE, d), lambda l, j: (0, 0)),
            pl.BlockSpec((None, d, tn), lambda l, j: (l, 0, j)),
            pl.BlockSpec((None, 1, tn), lambda l, j: (l, 0, j)),
        ],
        out_specs=pl.BlockSpec((None, SUBLANE, tn), lambda l, j: (l, 0, j)),
        out_shape=jax.ShapeDtypeStruct((depth, SUBLANE, n), F32),
        compiler_params=_cparams(("parallel", "parallel")),
    )(cvec, mod_w, mod_b.reshape(depth, 1, n))
    return out.reshape(depth, SUBLANE, N_MOD, d)


def _norm_matmul_kernel(xa_ref, xb_ref, g_ref, mod_ref, w_ref, o_ref, h_ref, *, n_a_tiles):
    i = pl.program_id(0)

    @pl.when(pl.program_id(1) == 0)
    def _():
        x = jnp.where(i < n_a_tiles, xa_ref[...], xb_ref[...])
        h_ref[...] = _bf(_norm_mod(x, g_ref[...], mod_ref[0:1, :], mod_ref[1:2, :]))

    o_ref[...] = _dot(h_ref[...], w_ref[...])


def _two_source_specs(tm, d, n_a_tiles):
    return [pl.BlockSpec((tm, d), lambda i, *_: (jnp.minimum(i, n_a_tiles - 1), 0)),
            pl.BlockSpec((tm, d), lambda i, *_: (jnp.maximum(i - n_a_tiles, 0), 0))]


def _norm_matmul(xa, xb, g, mods_l, w_bf, seg_of_tile, *, tn):
    d = xa.shape[1]
    n = w_bf.shape[1]
    tm = ROW_TILE
    n_a_tiles = xa.shape[0] // tm
    n_tiles = n_a_tiles + (0 if xb is None else xb.shape[0] // tm)
    return pl.pallas_call(
        functools.partial(_norm_matmul_kernel, n_a_tiles=n_a_tiles),
        grid=(n_tiles, n // tn),
        in_specs=_two_source_specs(tm, d, n_a_tiles) + [
            pl.BlockSpec((1, d), lambda i, j: (0, 0)),
            pl.BlockSpec((None, N_MOD, d), lambda i, j: (seg_of_tile(i), 0, 0)),
            pl.BlockSpec((d, tn), lambda i, j: (0, j)),
        ],
        out_specs=pl.BlockSpec((tm, tn), lambda i, j: (i, j)),
        out_shape=jax.ShapeDtypeStruct((n_tiles * tm, n), F32),
        scratch_shapes=[pltpu.VMEM((tm, d), BF16)],
        compiler_params=_cparams(("parallel", "arbitrary")),
    )(xa, xa if xb is None else xb, g.reshape(1, d), mods_l, w_bf)


def _grid_tile(rows):
    gr = min(rows, 64)
    gc = ROW_TILE // gr
    assert gc % SUBLANE == 0 and GRID_W % gc == 0 and rows % gr == 0
    return gr, gc


def _norm_matmul_grid_kernel(x_ref, g_ref, mod_ref, w_ref, o0_ref, o1_ref, h_ref, res_ref):
    j = pl.program_id(1)
    gr, gc, d = x_ref.shape

    @pl.when(j == 0)
    def _():
        h = _norm_mod(x_ref[...].reshape(gr * gc, d), g_ref[...], mod_ref[0:1, :], mod_ref[1:2, :])
        h_ref[...] = _bf(h)

    res = _dot(h_ref[...], w_ref[...])
    nh = w_ref.shape[1]
    nq = nh // LANE
    for q in range(nq):
        res_ref[q] = res[:, q * LANE:(q + 1) * LANE]

    def scatter(o_ref):
        for c in range(gc):
            for q in range(nq):
                o_ref[:, c * nh + q * LANE:c * nh + (q + 1) * LANE] = (
                    res_ref[q, pl.ds(c, gr, stride=gc), :])

    @pl.when(j == 0)
    def _():
        scatter(o0_ref)

    @pl.when(j == 1)
    def _():
        scatter(o1_ref)


def _norm_matmul_grid(x, n_lat, rows, g, mods_l, w_bf):
    d = x.shape[1]
    nh = w_bf.shape[1] // 2
    gr, gc = _grid_tile(rows)
    ncg = GRID_W // gc
    x3 = x.reshape(x.shape[0] // GRID_W, GRID_W, d)
    out_spec = pl.BlockSpec((gr, gc * nh), lambda i, j: (i // ncg, i % ncg))
    sds = jax.ShapeDtypeStruct((n_lat // GRID_W, GRID_W * nh), F32)
    return pl.pallas_call(
        _norm_matmul_grid_kernel,
        grid=(n_lat // ROW_TILE, 2),
        in_specs=[
            pl.BlockSpec((gr, gc, d), lambda i, j: (i // ncg, i % ncg, 0)),
            pl.BlockSpec((1, d), lambda i, j: (0, 0)),
            pl.BlockSpec((None, N_MOD, d), lambda i, j: ((i // ncg) * gr // rows, 0, 0)),
            pl.BlockSpec((d, nh), lambda i, j: (0, j)),
        ],
        out_specs=[out_spec, out_spec],
        out_shape=[sds, sds],
        scratch_shapes=[pltpu.VMEM((ROW_TILE, d), BF16),
                        pltpu.VMEM((nh // LANE, ROW_TILE, LANE), F32)],
        compiler_params=_cparams(("parallel", "arbitrary")),
    )(x3, g.reshape(1, d), mods_l, w_bf)


def _mlp_kernel(x_ref, g_ref, mod_ref, w1_ref, w2_ref, fn_ref, o_ref, h_ref, *, final_norm):
    j = pl.program_id(1)
    slab = 256
    n_slab = x_ref.shape[0] // slab

    @pl.when(j == 0)
    def _():
        for s in range(n_slab):
            rs = slice(s * slab, (s + 1) * slab)
            h_ref[rs, :] = _bf(_norm_mod(x_ref[rs, :], g_ref[...], mod_ref[3:4, :],
                                         mod_ref[4:5, :]))
        o_ref[...] = jnp.zeros_like(o_ref)

    a = _dot(h_ref[...], w1_ref[...])
    a = _bf(jnp.square(jnp.maximum(a, 0.0)))
    nsplit = 2
    wn = o_ref.shape[1] // nsplit
    for s in range(nsplit):
        o_ref[:, s * wn:(s + 1) * wn] += _dot(a, w2_ref[:, s * wn:(s + 1) * wn])

    @pl.when(j == pl.num_programs(1) - 1)
    def _():
        for s in range(n_slab):
            rs = slice(s * slab, (s + 1) * slab)
            xo = x_ref[rs, :] + mod_ref[5:6, :] * o_ref[rs, :]
            if final_norm:
                ms = jnp.mean(xo * xo, axis=-1, keepdims=True)
                xo = xo * lax.rsqrt(ms + NORM_EPS) * fn_ref[...]
            o_ref[rs, :] = xo


def _mlp(x, g, mods_l, w1_bf, w2_bf, fn, seg_of_tile, *, tm, tile0, n_tiles, final_norm):
    d = x.shape[1]
    dff = w1_bf.shape[1]
    tf = MLP_STEP_ELEMS // tm
    return pl.pallas_call(
        functools.partial(_mlp_kernel, final_norm=final_norm),
        grid=(n_tiles, dff // tf),
        in_specs=[
            pl.BlockSpec((tm, d), lambda i, j: (i + tile0, 0), pipeline_mode=pl.Buffered(1)),
            pl.BlockSpec((1, d), lambda i, j: (0, 0)),
            pl.BlockSpec((None, N_MOD, d), lambda i, j: (seg_of_tile(i + tile0), 0, 0)),
            pl.BlockSpec((d, tf), lambda i, j: (0, j)),
            pl.BlockSpec((tf, d), lambda i, j: (j, 0)),
            pl.BlockSpec((1, d), lambda i, j: (0, 0)),
        ],
        out_specs=pl.BlockSpec((tm, d), lambda i, j: (i, 0)),
        out_shape=jax.ShapeDtypeStruct((n_tiles * tm, d), F32),
        scratch_shapes=[pltpu.VMEM((tm, d), BF16)],
        compiler_params=_cparams(("parallel", "arbitrary")),
    )(x, g.reshape(1, d), mods_l, w1_bf, w2_bf, fn.reshape(1, d))


def _proj_res_kernel(y_ref, x_ref, mod_ref, w_ref, o_ref, yt_ref):
    k = w_ref.shape[0]
    gr, gc, d = x_ref.shape
    nq = k // LANE
    for c in range(gc):
        for q in range(nq):
            yt_ref[q, pl.ds(c, gr, stride=gc), :] = (
                y_ref[:, c * k + q * LANE:c * k + (q + 1) * LANE])
    yt = jnp.concatenate([_bf(yt_ref[q]) for q in range(nq)], axis=1)
    out = x_ref[...].reshape(gr * gc, d) + mod_ref[2:3, :] * _dot(yt, w_ref[...])
    o_ref[...] = out.reshape(gr, gc, d)


def _proj_res(y_grid, x, n_lat, rows, mods_l, w_bf):
    k, d = w_bf.shape
    gr, gc = _grid_tile(rows)
    ncg = GRID_W // gc
    x3 = x.reshape(x.shape[0] // GRID_W, GRID_W, d)
    tok_spec = pl.BlockSpec((gr, gc, d), lambda i: (i // ncg, i % ncg, 0))
    out = pl.pallas_call(
        _proj_res_kernel,
        grid=(n_lat // ROW_TILE,),
        in_specs=[
            pl.BlockSpec((gr, gc * k), lambda i: (i // ncg, i % ncg)),
            tok_spec,
            pl.BlockSpec((None, N_MOD, d), lambda i: ((i // ncg) * gr // rows, 0, 0)),
            pl.BlockSpec((k, d), lambda i: (0, 0)),
        ],
        out_specs=tok_spec,
        out_shape=jax.ShapeDtypeStruct((n_lat // GRID_W, GRID_W, d), F32),
        scratch_shapes=[pltpu.VMEM((k // LANE, ROW_TILE, LANE), F32)],
        compiler_params=_cparams(("parallel",)),
    )(y_grid, x3, mods_l, w_bf)
    return out.reshape(n_lat, d)


def _head_sum(z, e_bf):
    parts = []
    for gi in range(z.shape[1] // GROUP_W):
        zs = z[:, gi * GROUP_W:(gi + 1) * GROUP_W]
        parts.append(_dot_exact_rhs(zs, e_bf))
    return jnp.concatenate(parts, axis=1)


def _rwkv_prep_kernel(p_ref, pp_ref, pn_ref, mu_ref, w0_ref, a0_ref, w2_ref, a2_ref, g2_ref,
                      kk_ref, ka_ref, rk_ref, e_ref,
                      r_o, v_o, kk_o, lw0_o, lw1_o, kd0_o, kd1_o, b0_o, b1_o, g_o, bonus_o,
                      *, n_lat_tiles, lat_tiles_per_seq, ctx_tiles_per_seq, aw):
    i = pl.program_id(0)
    tp = p_ref.shape[0]
    in_ctx = i >= n_lat_tiles
    pos = jnp.where(in_ctx, (i - n_lat_tiles) % ctx_tiles_per_seq, i % lat_tiles_per_seq)
    last = jnp.where(in_ctx, ctx_tiles_per_seq - 1, lat_tiles_per_seq - 1)
    is_first = pos == 0
    is_last = pos == last
    rowid = lax.broadcasted_iota(jnp.int32, (tp, 1), 0)

    def shifted(c0, c1):
        x = p_ref[:, c0:c1]
        prow = jnp.where(is_first, 0.0, pp_ref[SUBLANE - 1:SUBLANE, c0:c1])
        nrow = jnp.where(is_last, 0.0, pn_ref[0:1, c0:c1])
        xp = jnp.where(rowid == 0, prow, pltpu.roll(x, 1, 0))
        xn = jnp.where(rowid == tp - 1, nrow, pltpu.roll(x, tp - 1, 0))
        return x + mu_ref[0:1, c0:c1] * (xp - x) + mu_ref[1:2, c0:c1] * (xn - x)

    e_bf = e_ref[...]
    r = shifted(0, aw)
    k = shifted(aw, 2 * aw)
    v = shifted(2 * aw, 3 * aw)
    gd = shifted(3 * aw, 3 * aw + A_LORA_G)
    lo0 = 3 * aw + A_LORA_G
    xl = shifted(lo0, lo0 + 2 * A_LORA_W + 2 * A_LORA_A)

    r_o[...] = r.astype(r_o.dtype)
    v_o[...] = v.astype(v_o.dtype)
    g_o[...] = _dot(_bf(_sigmoid(gd)), g2_ref[...])

    kx = k * kk_ref[...]
    kn = jnp.sqrt(_head_sum(kx * kx, e_bf))
    kk = kx / jnp.maximum(kn, 1e-12)
    kk_o[...] = kk.astype(kk_o.dtype)

    xl_t = _bf(jnp.tanh(xl))
    xl_b = _bf(xl)
    kd_sum = None
    for d, (lw_o, kd_o, b_o) in enumerate(((lw0_o, kd0_o, b0_o), (lw1_o, kd1_o, b1_o))):
        w_log = -_softplus(-(w0_ref[d:d + 1, :] + _dot(xl_t, w2_ref[d]))) - 0.5
        lw_o[...] = -jnp.exp(w_log)
        asig = _sigmoid(a0_ref[d:d + 1, :] + _dot(xl_b, a2_ref[d]))
        kd = k * (1.0 + (asig - 1.0) * ka_ref[...])
        kd_o[...] = kd.astype(kd_o.dtype)
        b_o[...] = (kk * asig).astype(b_o.dtype)
        kd_sum = kd if kd_sum is None else kd_sum + kd
    bonus_o[...] = _head_sum(r * rk_ref[...] * kd_sum, e_bf) * v


def _rwkv_prep(pa, mu_p, w0, a0, w2p, a2p, g2, k_k, k_a, r_k, e_bf, *, n_lat, seq, ctx_len):
    r = pa.shape[0]
    ncol = mu_p.shape[1]
    aw = w0.shape[1]
    tp = PREP_TILE
    hb = tp // SUBLANE
    nblk8 = r // SUBLANE
    lw = 2 * A_LORA_W + 2 * A_LORA_A
    row = lambda a: a.reshape(1, aw)
    full = lambda shape: pl.BlockSpec(shape, lambda i: (0,) * len(shape))
    out_spec = pl.BlockSpec((tp, aw), lambda i: (i, 0))
    out_dtypes = [BF16, BF16, BF16, F32, F32, BF16, BF16, BF16, BF16, F32, F32]
    return pl.pallas_call(
        functools.partial(_rwkv_prep_kernel, n_lat_tiles=n_lat // tp, lat_tiles_per_seq=seq // tp,
                          ctx_tiles_per_seq=ctx_len // tp, aw=aw),
        grid=(r // tp,),
        in_specs=[
            pl.BlockSpec((tp, ncol), lambda i: (i, 0)),
            pl.BlockSpec((SUBLANE, ncol), lambda i: (jnp.maximum(i * hb - 1, 0), 0)),
            pl.BlockSpec((SUBLANE, ncol), lambda i: (jnp.minimum((i + 1) * hb, nblk8 - 1), 0)),
            full((2, ncol)), full((2, aw)), full((2, aw)),
            full((2, lw, aw)), full((2, lw, aw)), full((A_LORA_G, aw)),
            full((1, aw)), full((1, aw)), full((1, aw)), full((GROUP_W, GROUP_W)),
        ],
        out_specs=[out_spec] * 11,
        out_shape=[jax.ShapeDtypeStruct((r, aw), dt) for dt in out_dtypes],
        compiler_params=_cparams(("parallel",)),
    )(pa, pa, pa, mu_p, w0, a0, w2p, a2p, g2, row(k_k), row(k_a), row(r_k), e_bf)


def _rwkv_masks(reverse):
    L = CHUNK
    gw = GROUP_W
    row = lax.broadcasted_iota(jnp.int32, (gw, gw), 0)
    col = lax.broadcasted_iota(jnp.int32, (gw, gw), 1)
    same_head = (row >> 6) == (col >> 6)
    tr = row & (L - 1)
    tc = col & (L - 1)
    strict = same_head & ((tr < tc) if reverse else (tr > tc))
    incl = same_head & ((tr <= tc) if reverse else (tr >= tc))
    t_r = lax.broadcasted_iota(jnp.int32, (L, L), 0)
    t_c = lax.broadcasted_iota(jnp.int32, (L, L), 1)
    tri = _bf(jnp.where((t_c >= t_r) if reverse else (t_c <= t_r), 1.0, 0.0))
    return dict(same_head=same_head, strict=strict, incl2=jnp.concatenate([incl, incl], axis=1),
                eye=row == col, tri=tri)


def _rwkv_chunks(probs, side=None):
    L = CHUNK
    gw = GROUP_W
    tick = (lambda: None) if side is None else (lambda: next(side, None))
    masks = {rv: _rwkv_masks(rv) for rv in sorted({p[7] for p in probs})}
    mk = [masks[p[7]] for p in probs]
    each = lambda f, *cols: [f(*args) for args in zip(*cols)]
    rep = lambda x: jnp.concatenate([x] * HEADS_PER_GROUP, axis=0)
    nat = lambda x: x[0:L] + x[L:2 * L] + x[2 * L:3 * L] + x[3 * L:4 * L]

    c = [_dot_exact_lhs(m["tri"], p[3]) for m, p in zip(mk, probs)]
    c_l = [ci[(0 if p[7] else L - 1):(1 if p[7] else L), :] for ci, p in zip(c, probs)]

    at_bf, r_nat, v_bf, bk_end, lhs, rhs, e_l = [], [], [], [], [], [], []
    for (r, v, kk, lw, kd, b, _, _), m, ci, cl in zip(probs, mk, c, c_l):
        bd = lambda x, sh=m["same_head"]: _bf(jnp.where(sh, rep(x), 0.0))
        e_nc = jnp.exp(-ci)
        e_end = jnp.exp(cl - ci)
        a_ = bd(-kk * jnp.exp(ci - lw))
        r_s = r * jnp.exp(ci)
        at_bf.append(a_)
        r_nat.append(_bf(r_s))
        v_bf.append(bd(v))
        bk_end.append(jnp.concatenate([bd(b * e_end), bd(kd * e_end)], axis=0))
        lhs.append(jnp.concatenate([a_, bd(r_s)], axis=0))
        rhs.append(jnp.concatenate([rep(_bf(b * e_nc)), rep(_bf(kd * e_nc))], axis=0))
        e_l.append(jnp.exp(cl))

    t, apb, a_ak, rbk = [], [], [], []
    for m, l_, r_ in zip(mk, lhs, rhs):
        s = _dot_nt(l_, r_)
        a_ab = jnp.where(m["strict"], s[:gw, :gw], 0.0)
        t.append(jnp.where(m["eye"], 1.0, a_ab))
        apb.append(_bf(a_ab))
        a_ak.append(_bf(jnp.where(m["strict"], s[:gw, gw:], 0.0)))
        rbk.append(_bf(jnp.where(m["incl2"], s[gw:, :], 0.0)))
    tick()
    akv = each(_dot, a_ak, v_bf)
    tb = [_bf(ti) for ti in t]
    for _ in range(5):
        apb = [_bf(_dot(a, a)) for a in apb]
        tick()
        t = [ti + _dot(tbi, a) for ti, tbi, a in zip(t, tb, apb)]
        tb = [_bf(ti) for ti in t]
    h_bf = [_bf(p[6]) for p in probs]
    xb = [_bf(_dot(a_, h) + ak) for a_, h, ak in zip(at_bf, h_bf, akv)]
    tick()
    uv = [jnp.concatenate([_bf(_dot(tbi, xi)), v], axis=0)
          for tbi, xi, v in zip(tb, xb, v_bf)]
    tick()
    y = [_dot(rn, h) + nat(_dot(rbk_i, uvi)) for rn, h, rbk_i, uvi in zip(r_nat, h_bf, rbk, uv)]
    e_col = [jnp.broadcast_to(el, (gw, gw)).T for el in e_l]
    h_new = [ec * p[6] + _dot_tn(bke, uvi) for ec, p, bke, uvi in zip(e_col, probs, bk_end, uv)]
    if side is not None:
        for _ in side:
            pass
    return list(zip(y, h_new))


def _chunk_row_block(b, i, *, reverse, n_lat, seq, ctx_len):
    nc = ctx_len // CHUNK
    nl = seq // CHUNK
    ci = (nc - 1 - i) if reverse else i
    li = (nl - 1 - (i - nc)) if reverse else (i - nc)
    return jnp.where(i < nc, (n_lat + b * ctx_len) // CHUNK + ci, b * nl + li)


def _gla_stages(qkf, vf, gdf, qkb, vb, gdb, gw2_ref, gb_ref, of_ref, ob_ref, sf_ref, sb_ref):
    dv, dk = sf_ref.shape[1:]
    L = CHUNK
    wk = B_HEADS * dk
    t_r = lax.broadcasted_iota(jnp.int32, (L, L), 0)
    t_c = lax.broadcasted_iota(jnp.int32, (L, L), 1)

    qh, kh, keh, vh, dech, keeph, sth = [], [], [], [], [], [], []
    for d, (qk_ref, v_ref, gd_ref, st_ref) in enumerate(((qkf, vf, gdf, sf_ref),
                                                         (qkb, vb, gdb, sb_ref))):
        reverse = d == 1
        keep = (t_c >= t_r) if reverse else (t_c <= t_r)
        tri = _bf(jnp.where(keep, 1.0, 0.0))
        z = _dot(_bf(gd_ref[...]), gw2_ref[d]) + gb_ref[d:d + 1, :]
        la = -_softplus(-z) * (1.0 / B_GATE_TAU)
        cum = _dot_exact_lhs(tri, la)
        last = 0 if reverse else L - 1
        cl = cum[last:last + 1, :]
        q_in = _bf(qk_ref[:, :wk] * (dk ** -0.5) * jnp.exp(cum))
        k_in = _bf(qk_ref[:, wk:] * jnp.exp(-cum))
        k_end = _bf(qk_ref[:, wk:] * jnp.exp(cl - cum))
        dec = jnp.exp(cl)
        for h in range(B_HEADS):
            ks = slice(h * dk, (h + 1) * dk)
            qh.append(q_in[:, ks])
            kh.append(k_in[:, ks])
            keh.append(k_end[:, ks])
            vh.append(_bf(v_ref[:, h * dv:(h + 1) * dv]))
            dech.append(dec[:, ks])
            keeph.append(keep)
            sth.append(st_ref[h])
        yield

    sc = [_bf(jnp.where(kp, _dot_nt(q, k), 0.0)) for kp, q, k in zip(keeph, qh, kh)]
    yield
    inter = [_dot_nt(q, _bf(st)) for q, st in zip(qh, sth)]
    yield
    kvt = [_dot_tn(v, ke) for v, ke in zip(vh, keh)]
    yield
    out = [_dot(s_, v) + it for s_, v, it in zip(sc, vh, inter)]
    yield
    for d, (o_ref, st_ref) in enumerate(((of_ref, sf_ref), (ob_ref, sb_ref))):
        for h in range(B_HEADS):
            i = d * B_HEADS + h
            o_ref[:, h * dv:(h + 1) * dv] = out[i]
            st_ref[h] = sth[i] * dech[i] + kvt[i]


def _mix_scan_kernel(rf, vf, kkf, lwf, kdf, bf_, rb, vb, kkb, lwb, kdb, bb,
                     qkf, gvf, gdf, qkb, gvb, gdb, gw2_ref, gb_ref,
                     yf_ref, yb_ref, of_ref, ob_ref, hf_ref, hb_ref, sf_ref, sb_ref):
    @pl.when(pl.program_id(1) == 0)
    def _():
        for ref in (hf_ref, hb_ref, sf_ref, sb_ref):
            ref[...] = jnp.zeros_like(ref)

    gs = hf_ref.shape[0]
    probs = []
    for gi in range(gs):
        sl = slice(gi * GROUP_W, (gi + 1) * GROUP_W)
        f32 = lambda ref: ref[:, sl].astype(F32)
        probs.append((f32(rf), f32(vf), f32(kkf), lwf[:, sl], f32(kdf), f32(bf_), hf_ref[gi], False))
        probs.append((f32(rb), f32(vb), f32(kkb), lwb[:, sl], f32(kdb), f32(bb), hb_ref[gi], True))
    gla = _gla_stages(qkf, gvf, gdf, qkb, gvb, gdb, gw2_ref, gb_ref, of_ref, ob_ref, sf_ref,
                      sb_ref)
    res = _rwkv_chunks(probs, side=gla)
    for gi in range(gs):
        sl = slice(gi * GROUP_W, (gi + 1) * GROUP_W)
        yf_ref[:, sl], hf_ref[gi] = res[2 * gi]
        yb_ref[:, sl], hb_ref[gi] = res[2 * gi + 1]


def _mix_scan(r, v, kk, lw0, lw1, kd0, kd1, b0, b1, pb, gw2p, gb, *, batch, n_lat, seq, ctx_len,
              wk, wv, col0):
    rows, aw = r.shape
    assert aw == GROUP_W * RWKV_GROUPS_PER_STEP
    dk, dv = wk // B_HEADS, wv // B_HEADS
    geo = dict(n_lat=n_lat, seq=seq, ctx_len=ctx_len)
    gd_blk = (col0 + 2 * wk + 2 * wv) // LANE

    def specs(reverse):
        rb = lambda b, i: _chunk_row_block(b, i, reverse=reverse, **geo)
        rw = pl.BlockSpec((CHUNK, aw), lambda b, i: (rb(b, i), 0))
        gla = [
            pl.BlockSpec((CHUNK, 2 * wk), lambda b, i: (rb(b, i), col0 // (2 * wk))),
            pl.BlockSpec((CHUNK, wv), lambda b, i: (rb(b, i), (col0 + 2 * wk) // wv)),
            pl.BlockSpec((CHUNK, LANE), lambda b, i: (rb(b, i), gd_blk)),
        ]
        out = pl.BlockSpec((CHUNK, wv), lambda b, i: (rb(b, i), 0))
        return rw, gla, out

    rwf, glaf, outf = specs(False)
    rwb, glab, outb = specs(True)
    sds_a = jax.ShapeDtypeStruct((rows, aw), F32)
    sds_b = jax.ShapeDtypeStruct((rows, wv), F32)
    gs = RWKV_GROUPS_PER_STEP
    return pl.pallas_call(
        _mix_scan_kernel,
        grid=(batch, (seq + ctx_len) // CHUNK),
        in_specs=[rwf] * 6 + [rwb] * 6 + glaf + glab + [
            pl.BlockSpec((2, LANE, wk), lambda b, i: (0, 0, 0)),
            pl.BlockSpec((2, wk), lambda b, i: (0, 0)),
        ],
        out_specs=[rwf, rwb, outf, outb],
        out_shape=[sds_a, sds_a, sds_b, sds_b],
        scratch_shapes=[pltpu.VMEM((gs, GROUP_W, GROUP_W), F32),
                        pltpu.VMEM((gs, GROUP_W, GROUP_W), F32),
                        pltpu.VMEM((B_HEADS, dv, dk), F32), pltpu.VMEM((B_HEADS, dv, dk), F32)],
        compiler_params=_cparams(("parallel", "arbitrary")),
    )(r, v, kk, lw0, kd0, b0, r, v, kk, lw1, kd1, b1, pb, pb, pb, pb, pb, pb, gw2p, gb)


def _mixer_out_kernel(yf, yb, bonus, g, of, ob, gate, xa_ref, xb_ref, mod_ref, lnw, lnb, gn, e_ref,
                      w_ref, o_ref, h_ref, *, aw, dv, n_a_tiles):
    e_bf = e_ref[...]
    y = yf[...] + yb[...]
    inv = 1.0 / A_HEAD_DIM
    mean = _head_sum(y, e_bf) * inv
    dlt = y - mean
    var = _head_sum(dlt * dlt, e_bf) * inv
    yn = dlt * lax.rsqrt(var + A_LN_EPS) * lnw[...] + lnb[...]
    h_ref[:, :aw] = _bf((yn + bonus[...]) * g[...])

    o = of[...] + ob[...]
    gt = _silu(gate[...])
    for h in range(B_HEADS):
        sl = slice(h * dv, (h + 1) * dv)
        oh = o[:, sl]
        ms = jnp.mean(oh * oh, axis=-1, keepdims=True)
        h_ref[:, aw + h * dv:aw + (h + 1) * dv] = _bf(oh * lax.rsqrt(ms + NORM_EPS) * gn[...] * gt[:, sl])

    x = jnp.where(pl.program_id(0) < n_a_tiles, xa_ref[...], xb_ref[...])
    o_ref[...] = x + mod_ref[2:3, :] * _dot(h_ref[...], w_ref[...])


def _mixer_out(yf, yb, bonus, g, of, ob, pb, xa, xb, mods_l, ln_w, ln_b, gla_norm, e_bf, w_bf,
               seg_of_tile, *, wk, col0):
    d = xa.shape[1]
    rows = xa.shape[0] + xb.shape[0]
    aw = yf.shape[1]
    wv = of.shape[1]
    dv = wv // B_HEADS
    tm = 256
    tiles_per_row_tile = ROW_TILE // tm
    n_a_tiles = xa.shape[0] // tm
    rs = lambda w: pl.BlockSpec((tm, w), lambda i: (i, 0))
    full = lambda shape: pl.BlockSpec(shape, lambda i: (0,) * len(shape))
    return pl.pallas_call(
        functools.partial(_mixer_out_kernel, aw=aw, dv=dv, n_a_tiles=n_a_tiles),
        grid=(rows // tm,),
        in_specs=[
            rs(aw), rs(aw), rs(aw), rs(aw), rs(wv), rs(wv),
            pl.BlockSpec((tm, wv), lambda i: (i, (col0 + 2 * wk + wv) // wv)),
        ] + _two_source_specs(tm, d, n_a_tiles) + [
            pl.BlockSpec((None, N_MOD, d), lambda i: (seg_of_tile(i // tiles_per_row_tile), 0, 0)),
            full((1, aw)), full((1, aw)), full((1, dv)), full((GROUP_W, GROUP_W)),
            full((aw + wv, d)),
        ],
        out_specs=rs(d),
        out_shape=jax.ShapeDtypeStruct((rows, d), F32),
        scratch_shapes=[pltpu.VMEM((tm, aw + wv), BF16)],
        compiler_params=_cparams(("parallel",)),
    )(yf, yb, bonus, g, of, ob, pb, xa, xb, mods_l, ln_w.reshape(1, aw), ln_b.reshape(1, aw),
      gla_norm.reshape(1, dv), e_bf, w_bf)


def _lru_kernel(*refs, reverse, n_blk, combine, cols):
    if combine:
        (x_ref, xp_ref, xn_ref, h0_ref, cw_ref, cb_ref, wc_ref, ba_ref, bx_ref, lam_ref,
         hf_ref, gate_ref, out_ref, hl_ref, a_sc, u_sc, hs_sc, carry_sc) = refs
    else:
        (x_ref, xp_ref, xn_ref, h0_ref, cw_ref, cb_ref, wc_ref, ba_ref, bx_ref, lam_ref,
         out_ref, hl_ref, a_sc, u_sc, hs_sc, carry_sc) = refs
    j = pl.program_id(2)
    blk = (n_blk - 1 - j) if reverse else j
    lb = x_ref.shape[0]
    ct = cw_ref.shape[1]
    cblk = wc_ref.shape[1]
    rowid = lax.broadcasted_iota(jnp.int32, (lb, 1), 0)
    decay_rate = -C_CONST * _softplus(-lam_ref[...])

    @pl.when(j == 0)
    def _():
        carry_sc[...] = h0_ref[0:1, :]

    for q in (range(cols - 1, -1, -1) if reverse else range(cols)):
        lanes = slice(q * ct, (q + 1) * ct)
        x = x_ref[:, lanes]
        if q > 0:
            prev = slice((q - 1) * ct, q * ct)
            p6, p7 = x_ref[lb - 2:lb - 1, prev], x_ref[lb - 1:lb, prev]
        else:
            p6 = jnp.where(blk == 0, 0.0, xp_ref[SUBLANE - 2:SUBLANE - 1, :])
            p7 = jnp.where(blk == 0, 0.0, xp_ref[SUBLANE - 1:SUBLANE, :])
        if q < cols - 1:
            n0 = x_ref[0:1, (q + 1) * ct:(q + 2) * ct]
        else:
            n0 = jnp.where(blk == n_blk - 1, 0.0, xn_ref[0:1, :])
        xm1 = jnp.where(rowid == 0, p7, pltpu.roll(x, 1, 0))
        xm2 = jnp.where(rowid == 0, p6, jnp.where(rowid == 1, p7, pltpu.roll(x, 2, 0)))
        xp1 = jnp.where(rowid == lb - 1, n0, pltpu.roll(x, lb - 1, 0))
        xs = (cb_ref[...] + cw_ref[0:1, :] * xm2 + cw_ref[1:2, :] * xm1 + cw_ref[2:3, :] * x
              + cw_ref[3:4, :] * xp1)

        for n in range(ct // cblk):
            sl = slice(n * cblk, (n + 1) * cblk)
            xb = xs[:, sl]
            ri = _dot(_bf(xb), wc_ref[n])
            rg = _sigmoid(ri[:, :cblk] + ba_ref[:, sl])
            ig = _sigmoid(ri[:, cblk:] + bx_ref[:, sl])
            a = jnp.exp(decay_rate[:, sl] * rg)
            a_sc[q, :, sl] = a
            u_sc[q, :, sl] = jnp.sqrt(1.0 - a * a) * (ig * xb)

        def step(t, h, q=q):
            tt = (lb - 1 - t) if reverse else t
            h = a_sc[q, pl.ds(tt, 1), :] * h + u_sc[q, pl.ds(tt, 1), :]
            hs_sc[q, pl.ds(tt, 1), :] = h
            return h

        carry_sc[...] = lax.fori_loop(0, lb, step, carry_sc[...], unroll=8)
        if combine:
            out_ref[:, lanes] = ((hs_sc[q] + hf_ref[q * lb:(q + 1) * lb, :])
                                 * jax.nn.gelu(gate_ref[:, lanes]))
        else:
            out_ref[q * lb:(q + 1) * lb, :] = hs_sc[q]
    hl_ref[...] = jnp.broadcast_to(carry_sc[...], hl_ref.shape)


def _lru_pass(x_arr, x_specs, h0, conv_w, conv_b, wcat, ba, bx, lam, out_shape, out_spec, extra,
              *, batch, n_blk, lb, cols, reverse):
    c = conv_w.shape[1]
    ct = LRU_CHANNEL_TILE
    nct = c // ct
    cblk = wcat.shape[1]
    chan = lambda rows_: pl.BlockSpec((rows_, ct), lambda b, k, j: (0, k))
    in_specs = list(x_specs) + [
        pl.BlockSpec((SUBLANE, ct), lambda b, k, j: (b, k)),
        chan(C_CONV), chan(1),
        pl.BlockSpec((ct // cblk, cblk, 2 * cblk), lambda b, k, j: (k, 0, 0)),
        chan(1), chan(1), chan(1),
    ] + [s for _, s in extra]
    args = [x_arr, x_arr, x_arr, h0, conv_w, conv_b.reshape(1, c), wcat, ba.reshape(1, c),
            bx.reshape(1, c), lam.reshape(1, c)] + [a for a, _ in extra]
    return pl.pallas_call(
        functools.partial(_lru_kernel, reverse=reverse, n_blk=n_blk, combine=bool(extra),
                          cols=cols),
        grid=(batch, nct, n_blk),
        in_specs=in_specs,
        out_specs=[out_spec, pl.BlockSpec((SUBLANE, ct), lambda b, k, j: (b, k))],
        out_shape=[out_shape, jax.ShapeDtypeStruct((batch * SUBLANE, c), F32)],
        scratch_shapes=[pltpu.VMEM((cols, lb, ct), F32), pltpu.VMEM((cols, lb, ct), F32),
                        pltpu.VMEM((cols, lb, ct), F32), pltpu.VMEM((1, ct), F32)],
        compiler_params=_cparams(("parallel", "parallel", "arbitrary")),
    )(*args)


def _rglru(hw_ctx, gate_grid, x_grid, conv_w, conv_b, wa, ba, wx, bx, lam, *, batch, n_lat, seq,
           ctx_len):
    c = conv_w.shape[1]
    ct = LRU_CHANNEL_TILE
    rows = seq // GRID_W
    lbc = 128
    ncb = ctx_len // lbc
    xcol = c // ct

    wcat = [_bf(jnp.concatenate([wa[d], wx[d]], axis=-1)) for d in range(2)]
    zeros_state = jnp.zeros((batch * SUBLANE, c), F32)

    def ctx_specs(reverse):
        blk = (lambda j: ncb - 1 - j) if reverse else (lambda j: j)
        per8 = lbc // SUBLANE
        cur = pl.BlockSpec((lbc, ct), lambda b, k, j: (b * ncb + blk(j), xcol + k))
        prv = pl.BlockSpec((SUBLANE, ct), lambda b, k, j: (
            b * ncb * per8 + jnp.maximum(blk(j) * per8 - 1, 0), xcol + k))
        nxt = pl.BlockSpec((SUBLANE, ct), lambda b, k, j: (
            b * ncb * per8 + jnp.minimum((blk(j) + 1) * per8, ncb * per8 - 1), xcol + k))
        return cur, prv, nxt

    assert ct == c
    kc = LRU_COLS_PER_STEP
    ncg = GRID_W // kc

    def lat_specs(reverse):
        grp = (lambda j: ncg - 1 - j) if reverse else (lambda j: j)
        r8 = rows // SUBLANE
        cur = pl.BlockSpec((rows, kc * ct), lambda b, k, j: (b, grp(j)))
        prv = pl.BlockSpec((SUBLANE, ct), lambda b, k, j: (
            b * r8 + r8 - 1, jnp.maximum(grp(j) * kc - 1, 0)))
        nxt = pl.BlockSpec((SUBLANE, ct), lambda b, k, j: (
            b * r8, jnp.minimum(grp(j) * kc + kc, GRID_W - 1)))
        hf = pl.BlockSpec((kc * rows, ct), lambda b, k, j: (b * ncg + grp(j), 0))
        return (cur, prv, nxt), hf, cur

    common = dict(batch=batch)
    ctx_scratch_sds = jax.ShapeDtypeStruct((batch * ctx_len, c), F32)
    ctx_out = lambda reverse: pl.BlockSpec(
        (lbc, ct), lambda b, k, j: (b * ncb + ((ncb - 1 - j) if reverse else j), k))
    _, st = _lru_pass(hw_ctx, ctx_specs(False), zeros_state, conv_w, conv_b, wcat[0], ba[0], bx[0],
                      lam[0], ctx_scratch_sds, ctx_out(False), [], n_blk=ncb, lb=lbc, cols=1,
                      reverse=False, **common)
    hf_sds = jax.ShapeDtypeStruct((n_lat, c), F32)
    x_specs, hf_spec, _ = lat_specs(False)
    hf, _ = _lru_pass(x_grid, x_specs, st, conv_w, conv_b, wcat[0], ba[0], bx[0], lam[0],
                      hf_sds, hf_spec, [], n_blk=ncg, lb=rows, cols=kc, reverse=False, **common)
    _, st = _lru_pass(hw_ctx, ctx_specs(True), zeros_state, conv_w, conv_b, wcat[1], ba[1], bx[1],
                      lam[1], ctx_scratch_sds, ctx_out(True), [], n_blk=ncb, lb=lbc, cols=1,
                      reverse=True, **common)
    x_specs, hf_spec, grid_spec = lat_specs(True)
    y_sds = jax.ShapeDtypeStruct((n_lat // GRID_W, GRID_W * c), F32)
    y, _ = _lru_pass(x_grid, x_specs, st, conv_w, conv_b, wcat[1], ba[1], bx[1], lam[1],
                     y_sds, grid_spec, [(hf, hf_spec), (gate_grid, grid_spec)],
                     n_blk=ncg, lb=rows, cols=kc, reverse=True, **common)
    return y


def _pad_cols(w, n):
    return jnp.pad(w, ((0, 0),) * (w.ndim - 1) + ((0, n - w.shape[-1]),))


def kernel(x, c, ctx, c_ctx, mod_w, mod_b, norm1, norm2, mlp_w1, mlp_w2, ab_w_in, ab_w_out, rw_mu,
           rw_w0, rw_w2, rw_a0, rw_a2, rw_g2, rw_kk, rw_ka, rw_rk, rw_ln_w, rw_ln_b, gla_gw2,
           gla_gb, gla_norm, lru_w_in, lru_w_out, lru_conv_w, lru_conv_b, lru_wa, lru_ba, lru_wx,
           lru_bx, lru_lam, final_norm):
    batch, seq, d = x.shape
    ctx_len = ctx.shape[1]
    depth = mod_w.shape[0]
    n_lat = batch * seq
    aw = rw_w0.shape[-1]
    wk = gla_gb.shape[-1]
    wv = ab_w_out.shape[1] - aw
    a_cols = rw_mu.shape[-1]
    b_cols = ab_w_in.shape[-1] - a_cols
    assert depth == 2 and batch + 1 <= SUBLANE
    assert seq % ROW_TILE == 0 and (batch * ctx_len) % ROW_TILE == 0
    assert ctx_len % PREP_TILE == 0 and ctx_len % 128 == 0 and seq % (GRID_W * SUBLANE) == 0
    assert aw % GROUP_W == 0 and wv == aw and 2 * wk == wv

    tiles_per_seq = seq // ROW_TILE
    seg_of_tile = lambda i: jnp.minimum(i // tiles_per_seq, batch)

    x2 = x.reshape(n_lat, d)
    ctx2 = ctx.reshape(batch * ctx_len, d)
    cvec = jnp.concatenate(
        [c, c_ctx[None, :], jnp.zeros((SUBLANE - batch - 1, d), F32)], axis=0)
    mods = _mods(cvec, mod_w, mod_b)

    hid = jnp.arange(GROUP_W) // A_HEAD_DIM
    e_bf = _bf(hid[:, None] == hid[None, :])

    a_pad = -(-a_cols // LANE) * LANE
    col0 = -(-a_cols // (2 * wk)) * (2 * wk)
    n_tiles = 3
    tn = -(-(col0 + b_cols) // (n_tiles * LANE)) * LANE
    w_ab = _bf(jnp.concatenate(
        [_pad_cols(ab_w_in[0][:, :a_cols], col0),
         _pad_cols(ab_w_in[0][:, a_cols:], n_tiles * tn - col0)], axis=1))
    pa = pb = _norm_matmul(x2, ctx2, norm1[0], mods[0], w_ab, seg_of_tile, tn=tn)

    lora_w = 2 * A_LORA_W + 2 * A_LORA_A
    w2p = jnp.zeros((2, lora_w, aw), F32)
    a2p = jnp.zeros((2, lora_w, aw), F32)
    for dd in range(2):
        w2p = w2p.at[dd, dd * A_LORA_W:(dd + 1) * A_LORA_W].set(rw_w2[0, dd])
        a2p = a2p.at[dd, 2 * A_LORA_W + dd * A_LORA_A:2 * A_LORA_W + (dd + 1) * A_LORA_A].set(
            rw_a2[0, dd])
    (r_, v_, kk_, lw0, lw1, kd0, kd1, b0, b1, g_, bonus) = _rwkv_prep(
        pa, _pad_cols(rw_mu[0], a_pad), rw_w0[0], rw_a0[0], _bf(w2p), _bf(a2p), _bf(rw_g2[0]),
        rw_kk[0], rw_ka[0], rw_rk[0].reshape(-1), e_bf, n_lat=n_lat, seq=seq, ctx_len=ctx_len)
    gw2p = jnp.zeros((2, LANE, wk), F32)
    for dd in range(2):
        gw2p = gw2p.at[dd, dd * B_LORA:(dd + 1) * B_LORA].set(gla_gw2[0, dd])
    yf, yb, of, ob = _mix_scan(r_, v_, kk_, lw0, lw1, kd0, kd1, b0, b1, pb, _bf(gw2p), gla_gb[0],
                               batch=batch, n_lat=n_lat, seq=seq, ctx_len=ctx_len, wk=wk, wv=wv,
                               col0=col0)

    xs = _mixer_out(yf, yb, bonus, g_, of, ob, pb, x2, ctx2, mods[0], rw_ln_w[0], rw_ln_b[0],
                    gla_norm[0], e_bf, _bf(ab_w_out[0]), seg_of_tile, wk=wk, col0=col0)
    seg_of_mlp_tile = lambda i: jnp.minimum(i // (seq // MLP_TILE), batch)
    ctx_tiles = (batch * ctx_len) // ROW_TILE
    w1_bf, w2_bf = _bf(mlp_w1[0]), _bf(mlp_w2[0])
    x_lat = _mlp(xs, norm2[0], mods[0], w1_bf, w2_bf, final_norm, seg_of_mlp_tile, tm=MLP_TILE,
                 tile0=0, n_tiles=n_lat // MLP_TILE, final_norm=False)
    x_ctx = _mlp(xs, norm2[0], mods[0], w1_bf, w2_bf, final_norm, seg_of_tile, tm=ROW_TILE,
                 tile0=n_lat // ROW_TILE, n_tiles=ctx_tiles, final_norm=False)

    w_lru = _bf(lru_w_in[0])
    hw_ctx = _norm_matmul(x_ctx, None, norm1[1], mods[1], w_lru, lambda i: batch, tn=2048)
    gate_grid, x_grid = _norm_matmul_grid(x_lat, n_lat, seq // GRID_W, norm1[1], mods[1], w_lru)
    y = _rglru(hw_ctx, gate_grid, x_grid, lru_conv_w[0], lru_conv_b[0], lru_wa[0], lru_ba[0],
               lru_wx[0], lru_bx[0], lru_lam[0], batch=batch, n_lat=n_lat, seq=seq,
               ctx_len=ctx_len)
    xl = _proj_res(y, x_lat, n_lat, seq // GRID_W, mods[1], _bf(lru_w_out[0]))
    out = _mlp(xl, norm2[1], mods[1], _bf(mlp_w1[1]), _bf(mlp_w2[1]), final_norm, seg_of_mlp_tile,
               tm=MLP_TILE, tile0=0, n_tiles=n_lat // MLP_TILE, final_norm=True)
    return out.reshape(batch, seq, d)
```

```python
import functools

import jax
import jax.numpy as jnp
from jax import lax
from jax.experimental import pallas as pl
from jax.experimental.pallas import tpu as pltpu

F32 = jnp.float32
BF16 = jnp.bfloat16

NORM_EPS = 1e-6
GRID_W = 64
N_MOD = 6

A_HEAD_DIM = 64
A_LORA_W = 96
A_LORA_A = 96
A_LORA_G = 256
A_LN_EPS = 64e-5
HEADS_PER_GROUP = 4
GROUP_W = HEADS_PER_GROUP * A_HEAD_DIM
CHUNK = 64
RWKV_GROUPS_PER_STEP = 4

B_HEADS = 4
B_LORA = 16
B_GATE_TAU = 16.0

C_BLOCKS = 8
C_CONV = 4
C_CONST = 8.0

LANE = 128
SUBLANE = 8
ROW_TILE = 512
MLP_TILE = 1024
MLP_STEP_ELEMS = 1024 * 512
LRU_CHANNEL_TILE = 2048
LRU_COLS_PER_STEP = 4
PREP_TILE = 128
VMEM_LIMIT = 56 * 1024 * 1024


def _cparams(sem):
    return pltpu.CompilerParams(dimension_semantics=sem, vmem_limit_bytes=VMEM_LIMIT)


def _bf(x):
    return x.astype(BF16)


def _dot(a, b):
    return jnp.dot(a, b, preferred_element_type=F32)


def _dot_nt(a, b):
    return lax.dot_general(a, b, (((1,), (1,)), ((), ())), preferred_element_type=F32)


def _dot_tn(a, b):
    return lax.dot_general(a, b, (((0,), (0,)), ((), ())), preferred_element_type=F32)


def _split3(x):
    hi = _bf(x)
    r1 = x - hi.astype(F32)
    mid = _bf(r1)
    lo = _bf(r1 - mid.astype(F32))
    return hi, mid, lo


def _dot_exact_lhs(m_bf, x):
    hi, mid, lo = _split3(x)
    return _dot(m_bf, hi) + _dot(m_bf, mid) + _dot(m_bf, lo)


def _dot_exact_rhs(x, m_bf):
    hi, mid, lo = _split3(x)
    return _dot(hi, m_bf) + _dot(mid, m_bf) + _dot(lo, m_bf)


def _softplus(x):
    return jnp.maximum(x, 0.0) + jnp.log(1.0 + jnp.exp(-jnp.abs(x)))


def _sigmoid(x):
    return 0.5 * jnp.tanh(0.5 * x) + 0.5


def _silu(x):
    return x * _sigmoid(x)


def _norm_mod(x, g, shift, scale):
    ms = jnp.mean(x * x, axis=-1, keepdims=True)
    return (x * lax.rsqrt(ms + NORM_EPS) * g) * (1.0 + scale) + shift


def _mods_kernel(c_ref, w_ref, b_ref, o_ref):
    s = _silu(c_ref[...])
    o_ref[...] = _dot(_bf(s), _bf(w_ref[...])) + b_ref[...]


def _mods(cvec, mod_w, mod_b):
    depth, d, n = mod_w.shape
    tn = 1024
    out = pl.pallas_call(
        _mods_kernel,
        grid=(depth, n // tn),
        in_specs=[
            pl.BlockSpec((SUBLANE, d), lambda l, j: (0, 0)),
            pl.BlockSpec((None, d, tn), lambda l, j: (l, 0, j)),
            pl.BlockSpec((None, 1, tn), lambda l, j: (l, 0, j)),
        ],
        out_specs=pl.BlockSpec((None, SUBLANE, tn), lambda l, j: (l, 0, j)),
        out_shape=jax.ShapeDtypeStruct((depth, SUBLANE, n), F32),
        compiler_params=_cparams(("parallel", "parallel")),
    )(cvec, mod_w, mod_b.reshape(depth, 1, n))
    return out.reshape(depth, SUBLANE, N_MOD, d)


def _norm_matmul_kernel(xa_ref, xb_ref, g_ref, mod_ref, w_ref, o_ref, h_ref, *, n_a_tiles):
    i = pl.program_id(0)

    @pl.when(pl.program_id(1) == 0)
    def _():
        x = jnp.where(i < n_a_tiles, xa_ref[...], xb_ref[...])
        h_ref[...] = _bf(_norm_mod(x, g_ref[...], mod_ref[0:1, :], mod_ref[1:2, :]))

    o_ref[...] = _dot(h_ref[...], w_ref[...])


def _two_source_specs(tm, d, n_a_tiles):
    return [pl.BlockSpec((tm, d), lambda i, *_: (jnp.minimum(i, n_a_tiles - 1), 0)),
            pl.BlockSpec((tm, d), lambda i, *_: (jnp.maximum(i - n_a_tiles, 0), 0))]


def _norm_matmul(xa, xb, g, mods_l, w_bf, seg_of_tile, *, tn):
    d = xa.shape[1]
    n = w_bf.shape[1]
    tm = ROW_TILE
    n_a_tiles = xa.shape[0] // tm
    n_tiles = n_a_tiles + (0 if xb is None else xb.shape[0] // tm)
    return pl.pallas_call(
        functools.partial(_norm_matmul_kernel, n_a_tiles=n_a_tiles),
        grid=(n_tiles, n // tn),
        in_specs=_two_source_specs(tm, d, n_a_tiles) + [
            pl.BlockSpec((1, d), lambda i, j: (0, 0)),
            pl.BlockSpec((None, N_MOD, d), lambda i, j: (seg_of_tile(i), 0, 0)),
            pl.BlockSpec((d, tn), lambda i, j: (0, j)),
        ],
        out_specs=pl.BlockSpec((tm, tn), lambda i, j: (i, j)),
        out_shape=jax.ShapeDtypeStruct((n_tiles * tm, n), F32),
        scratch_shapes=[pltpu.VMEM((tm, d), BF16)],
        compiler_params=_cparams(("parallel", "arbitrary")),
    )(xa, xa if xb is None else xb, g.reshape(1, d), mods_l, w_bf)


def _grid_tile(rows):
    gr = min(rows, 64)
    gc = ROW_TILE // gr
    assert gc % SUBLANE == 0 and GRID_W % gc == 0 and rows % gr == 0
    return gr, gc


def _norm_matmul_grid_kernel(x_ref, g_ref, mod_ref, w_ref, o0_ref, o1_ref, h_ref, res_ref):
    j = pl.program_id(1)
    gr, gc, d = x_ref.shape

    @pl.when(j == 0)
    def _():
        h = _norm_mod(x_ref[...].reshape(gr * gc, d), g_ref[...], mod_ref[0:1, :], mod_ref[1:2, :])
        h_ref[...] = _bf(h)

    res = _dot(h_ref[...], w_ref[...])
    nh = w_ref.shape[1]
    nq = nh // LANE
    for q in range(nq):
        res_ref[q] = res[:, q * LANE:(q + 1) * LANE]

    def scatter(o_ref):
        for c in range(gc):
            for q in range(nq):
                o_ref[:, c * nh + q * LANE:c * nh + (q + 1) * LANE] = (
                    res_ref[q, pl.ds(c, gr, stride=gc), :])

    @pl.when(j == 0)
    def _():
        scatter(o0_ref)

    @pl.when(j == 1)
    def _():
        scatter(o1_ref)


def _norm_matmul_grid(x, n_lat, rows, g, mods_l, w_bf):
    d = x.shape[1]
    nh = w_bf.shape[1] // 2
    gr, gc = _grid_tile(rows)
    ncg = GRID_W // gc
    x3 = x.reshape(x.shape[0] // GRID_W, GRID_W, d)
    out_spec = pl.BlockSpec((gr, gc * nh), lambda i, j: (i // ncg, i % ncg))
    sds = jax.ShapeDtypeStruct((n_lat // GRID_W, GRID_W * nh), F32)
    return pl.pallas_call(
        _norm_matmul_grid_kernel,
        grid=(n_lat // ROW_TILE, 2),
        in_specs=[
            pl.BlockSpec((gr, gc, d), lambda i, j: (i // ncg, i % ncg, 0)),
            pl.BlockSpec((1, d), lambda i, j: (0, 0)),
            pl.BlockSpec((None, N_MOD, d), lambda i, j: ((i // ncg) * gr // rows, 0, 0)),
            pl.BlockSpec((d, nh), lambda i, j: (0, j)),
        ],
        out_specs=[out_spec, out_spec],
        out_shape=[sds, sds],
        scratch_shapes=[pltpu.VMEM((ROW_TILE, d), BF16),
                        pltpu.VMEM((nh // LANE, ROW_TILE, LANE), F32)],
        compiler_params=_cparams(("parallel", "arbitrary")),
    )(x3, g.reshape(1, d), mods_l, w_bf)


def _mlp_kernel(x_ref, g_ref, mod_ref, w1_ref, w2_ref, fn_ref, o_ref, h_ref, *, final_norm):
    j = pl.program_id(1)
    slab = 256
    n_slab = x_ref.shape[0] // slab

    @pl.when(j == 0)
    def _():
        for s in range(n_slab):
            rs = slice(s * slab, (s + 1) * slab)
            h_ref[rs, :] = _bf(_norm_mod(x_ref[rs, :], g_ref[...], mod_ref[3:4, :],
                                         mod_ref[4:5, :]))
        o_ref[...] = jnp.zeros_like(o_ref)

    a = _dot(h_ref[...], w1_ref[...])
    a = _bf(jnp.square(jnp.maximum(a, 0.0)))
    nsplit = 2
    wn = o_ref.shape[1] // nsplit
    for s in range(nsplit):
        o_ref[:, s * wn:(s + 1) * wn] += _dot(a, w2_ref[:, s * wn:(s + 1) * wn])

    @pl.when(j == pl.num_programs(1) - 1)
    def _():
        for s in range(n_slab):
            rs = slice(s * slab, (s + 1) * slab)
            xo = x_ref[rs, :] + mod_ref[5:6, :] * o_ref[rs, :]
            if final_norm:
                ms = jnp.mean(xo * xo, axis=-1, keepdims=True)
                xo = xo * lax.rsqrt(ms + NORM_EPS) * fn_ref[...]
            o_ref[rs, :] = xo


def _mlp(x, g, mods_l, w1_bf, w2_bf, fn, seg_of_tile, *, tm, tile0, n_tiles, final_norm):
    d = x.shape[1]
    dff = w1_bf.shape[1]
    tf = MLP_STEP_ELEMS // tm
    return pl.pallas_call(
        functools.partial(_mlp_kernel, final_norm=final_norm),
        grid=(n_tiles, dff // tf),
        in_specs=[
            pl.BlockSpec((tm, d), lambda i, j: (i + tile0, 0), pipeline_mode=pl.Buffered(1)),
            pl.BlockSpec((1, d), lambda i, j: (0, 0)),
            pl.BlockSpec((None, N_MOD, d), lambda i, j: (seg_of_tile(i + tile0), 0, 0)),
            pl.BlockSpec((d, tf), lambda i, j: (0, j)),
            pl.BlockSpec((tf, d), lambda i, j: (j, 0)),
            pl.BlockSpec((1, d), lambda i, j: (0, 0)),
        ],
        out_specs=pl.BlockSpec((tm, d), lambda i, j: (i, 0)),
        out_shape=jax.ShapeDtypeStruct((n_tiles * tm, d), F32),
        scratch_shapes=[pltpu.VMEM((tm, d), BF16)],
        compiler_params=_cparams(("parallel", "arbitrary")),
    )(x, g.reshape(1, d), mods_l, w1_bf, w2_bf, fn.reshape(1, d))


def _proj_res_kernel(y_ref, x_ref, mod_ref, w_ref, o_ref, yt_ref):
    k = w_ref.shape[0]
    gr, gc, d = x_ref.shape
    nq = k // LANE
    for c in range(gc):
        for q in range(nq):
            yt_ref[q, pl.ds(c, gr, stride=gc), :] = (
                y_ref[:, c * k + q * LANE:c * k + (q + 1) * LANE])
    yt = jnp.concatenate([_bf(yt_ref[q]) for q in range(nq)], axis=1)
    out = x_ref[...].reshape(gr * gc, d) + mod_ref[2:3, :] * _dot(yt, w_ref[...])
    o_ref[...] = out.reshape(gr, gc, d)


def _proj_res(y_grid, x, n_lat, rows, mods_l, w_bf):
    k, d = w_bf.shape
    gr, gc = _grid_tile(rows)
    ncg = GRID_W // gc
    x3 = x.reshape(x.shape[0] // GRID_W, GRID_W, d)
    tok_spec = pl.BlockSpec((gr, gc, d), lambda i: (i // ncg, i % ncg, 0))
    out = pl.pallas_call(
        _proj_res_kernel,
        grid=(n_lat // ROW_TILE,),
        in_specs=[
            pl.BlockSpec((gr, gc * k), lambda i: (i // ncg, i % ncg)),
            tok_spec,
            pl.BlockSpec((None, N_MOD, d), lambda i: ((i // ncg) * gr // rows, 0, 0)),
            pl.BlockSpec((k, d), lambda i: (0, 0)),
        ],
        out_specs=tok_spec,
        out_shape=jax.ShapeDtypeStruct((n_lat // GRID_W, GRID_W, d), F32),
        scratch_shapes=[pltpu.VMEM((k // LANE, ROW_TILE, LANE), F32)],
        compiler_params=_cparams(("parallel",)),
    )(y_grid, x3, mods_l, w_bf)
    return out.reshape(n_lat, d)


def _head_sum(z, e_bf):
    parts = []
    for gi in range(z.shape[1] // GROUP_W):
        zs = z[:, gi * GROUP_W:(gi + 1) * GROUP_W]
        parts.append(_dot_exact_rhs(zs, e_bf))
    return jnp.concatenate(parts, axis=1)


def _rwkv_prep_kernel(p_ref, pp_ref, pn_ref, mu_ref, w0_ref, a0_ref, w2_ref, a2_ref, g2_ref,
                      kk_ref, ka_ref, rk_ref, e_ref,
                      r_o, v_o, kk_o, lw0_o, lw1_o, kd0_o, kd1_o, b0_o, b1_o, g_o, bonus_o,
                      *, n_lat_tiles, lat_tiles_per_seq, ctx_tiles_per_seq, aw):
    i = pl.program_id(0)
    tp = p_ref.shape[0]
    in_ctx = i >= n_lat_tiles
    pos = jnp.where(in_ctx, (i - n_lat_tiles) % ctx_tiles_per_seq, i % lat_tiles_per_seq)
    last = jnp.where(in_ctx, ctx_tiles_per_seq - 1, lat_tiles_per_seq - 1)
    is_first = pos == 0
    is_last = pos == last
    rowid = lax.broadcasted_iota(jnp.int32, (tp, 1), 0)

    def shifted(c0, c1):
        x = p_ref[:, c0:c1]
        prow = jnp.where(is_first, 0.0, pp_ref[SUBLANE - 1:SUBLANE, c0:c1])
        nrow = jnp.where(is_last, 0.0, pn_ref[0:1, c0:c1])
        xp = jnp.where(rowid == 0, prow, pltpu.roll(x, 1, 0))
        xn = jnp.where(rowid == tp - 1, nrow, pltpu.roll(x, tp - 1, 0))
        return x + mu_ref[0:1, c0:c1] * (xp - x) + mu_ref[1:2, c0:c1] * (xn - x)

    e_bf = e_ref[...]
    r = shifted(0, aw)
    k = shifted(aw, 2 * aw)
    v = shifted(2 * aw, 3 * aw)
    gd = shifted(3 * aw, 3 * aw + A_LORA_G)
    lo0 = 3 * aw + A_LORA_G
    xl = shifted(lo0, lo0 + 2 * A_LORA_W + 2 * A_LORA_A)

    r_o[...] = r.astype(r_o.dtype)
    v_o[...] = v.astype(v_o.dtype)
    g_o[...] = _dot(_bf(_sigmoid(gd)), g2_ref[...])

    kx = k * kk_ref[...]
    kn = jnp.sqrt(_head_sum(kx * kx, e_bf))
    kk = kx / jnp.maximum(kn, 1e-12)
    kk_o[...] = kk.astype(kk_o.dtype)

    xl_t = _bf(jnp.tanh(xl))
    xl_b = _bf(xl)
    kd_sum = None
    for d, (lw_o, kd_o, b_o) in enumerate(((lw0_o, kd0_o, b0_o), (lw1_o, kd1_o, b1_o))):
        w_log = -_softplus(-(w0_ref[d:d + 1, :] + _dot(xl_t, w2_ref[d]))) - 0.5
        lw_o[...] = -jnp.exp(w_log)
        asig = _sigmoid(a0_ref[d:d + 1, :] + _dot(xl_b, a2_ref[d]))
        kd = k * (1.0 + (asig - 1.0) * ka_ref[...])
        kd_o[...] = kd.astype(kd_o.dtype)
        b_o[...] = (kk * asig).astype(b_o.dtype)
        kd_sum = kd if kd_sum is None else kd_sum + kd
    bonus_o[...] = _head_sum(r * rk_ref[...] * kd_sum, e_bf) * v


def _rwkv_prep(pa, mu_p, w0, a0, w2p, a2p, g2, k_k, k_a, r_k, e_bf, *, n_lat, seq, ctx_len):
    r = pa.shape[0]
    ncol = mu_p.shape[1]
    aw = w0.shape[1]
    tp = PREP_TILE
    hb = tp // SUBLANE
    nblk8 = r // SUBLANE
    lw = 2 * A_LORA_W + 2 * A_LORA_A
    row = lambda a: a.reshape(1, aw)
    full = lambda shape: pl.BlockSpec(shape, lambda i: (0,) * len(shape))
    out_spec = pl.BlockSpec((tp, aw), lambda i: (i, 0))
    out_dtypes = [BF16, BF16, BF16, F32, F32, BF16, BF16, BF16, BF16, F32, F32]
    return pl.pallas_call(
        functools.partial(_rwkv_prep_kernel, n_lat_tiles=n_lat // tp, lat_tiles_per_seq=seq // tp,
                          ctx_tiles_per_seq=ctx_len // tp, aw=aw),
        grid=(r // tp,),
        in_specs=[
            pl.BlockSpec((tp, ncol), lambda i: (i, 0)),
            pl.BlockSpec((SUBLANE, ncol), lambda i: (jnp.maximum(i * hb - 1, 0), 0)),
            pl.BlockSpec((SUBLANE, ncol), lambda i: (jnp.minimum((i + 1) * hb, nblk8 - 1), 0)),
            full((2, ncol)), full((2, aw)), full((2, aw)),
            full((2, lw, aw)), full((2, lw, aw)), full((A_LORA_G, aw)),
            full((1, aw)), full((1, aw)), full((1, aw)), full((GROUP_W, GROUP_W)),
        ],
        out_specs=[out_spec] * 11,
        out_shape=[jax.ShapeDtypeStruct((r, aw), dt) for dt in out_dtypes],
        compiler_params=_cparams(("parallel",)),
    )(pa, pa, pa, mu_p, w0, a0, w2p, a2p, g2, row(k_k), row(k_a), row(r_k), e_bf)


def _rwkv_masks(reverse):
    L = CHUNK
    gw = GROUP_W
    row = lax.broadcasted_iota(jnp.int32, (gw, gw), 0)
    col = lax.broadcasted_iota(jnp.int32, (gw, gw), 1)
    same_head = (row >> 6) == (col >> 6)
    tr = row & (L - 1)
    tc = col & (L - 1)
    strict = same_head & ((tr < tc) if reverse else (tr > tc))
    incl = same_head & ((tr <= tc) if reverse else (tr >= tc))
    t_r = lax.broadcasted_iota(jnp.int32, (L, L), 0)
    t_c = lax.broadcasted_iota(jnp.int32, (L, L), 1)
    tri = _bf(jnp.where((t_c >= t_r) if reverse else (t_c <= t_r), 1.0, 0.0))
    return dict(same_head=same_head, strict=strict, incl2=jnp.concatenate([incl, incl], axis=1),
                eye=row == col, tri=tri)


def _rwkv_chunks(probs, side=None):
    L = CHUNK
    gw = GROUP_W
    tick = (lambda: None) if side is None else (lambda: next(side, None))
    masks = {rv: _rwkv_masks(rv) for rv in sorted({p[7] for p in probs})}
    mk = [masks[p[7]] for p in probs]
    each = lambda f, *cols: [f(*args) for args in zip(*cols)]
    rep = lambda x: jnp.concatenate([x] * HEADS_PER_GROUP, axis=0)
    nat = lambda x: x[0:L] + x[L:2 * L] + x[2 * L:3 * L] + x[3 * L:4 * L]

    c = [_dot_exact_lhs(m["tri"], p[3]) for m, p in zip(mk, probs)]
    c_l = [ci[(0 if p[7] else L - 1):(1 if p[7] else L), :] for ci, p in zip(c, probs)]

    at_bf, r_nat, v_bf, bk_end, lhs, rhs, e_l = [], [], [], [], [], [], []
    for (r, v, kk, lw, kd, b, _, _), m, ci, cl in zip(probs, mk, c, c_l):
        bd = lambda x, sh=m["same_head"]: _bf(jnp.where(sh, rep(x), 0.0))
        e_nc = jnp.exp(-ci)
        e_end = jnp.exp(cl - ci)
        a_ = bd(-kk * jnp.exp(ci - lw))
        r_s = r * jnp.exp(ci)
        at_bf.append(a_)
        r_nat.append(_bf(r_s))
        v_bf.append(bd(v))
        bk_end.append(jnp.concatenate([bd(b * e_end), bd(kd * e_end)], axis=0))
        lhs.append(jnp.concatenate([a_, bd(r_s)], axis=0))
        rhs.append(jnp.concatenate([_bf(b * e_nc), _bf(kd * e_nc)], axis=0))
        e_l.append(jnp.exp(cl))

    lo_half = lax.broadcasted_iota(jnp.int32, (2 * gw, 2 * L), 1) < L
    x4 = lambda x: jnp.concatenate([x, x], axis=1)
    t, apb, a_ak, rbk = [], [], [], []
    for m, l_, r_ in zip(mk, lhs, rhs):
        s = _dot_nt(l_, r_)
        s_sw = pltpu.roll(s, L, 1)
        sb = x4(jnp.where(lo_half, s, s_sw))
        sk = x4(jnp.where(lo_half, s_sw, s))
        a_ab = jnp.where(m["strict"], sb[:gw], 0.0)
        t.append(jnp.where(m["eye"], 1.0, a_ab))
        apb.append(_bf(a_ab))
        a_ak.append(_bf(jnp.where(m["strict"], sk[:gw], 0.0)))
        rbk.append(_bf(jnp.where(m["incl2"], jnp.concatenate([sb[gw:], sk[gw:]], axis=1), 0.0)))
    tick()
    akv = each(_dot, a_ak, v_bf)
    tb = [_bf(ti) for ti in t]
    for _ in range(5):
        apb = [_bf(_dot(a, a)) for a in apb]
        tick()
        t = [ti + _dot(tbi, a) for ti, tbi, a in zip(t, tb, apb)]
        tb = [_bf(ti) for ti in t]
    h_bf = [_bf(p[6]) for p in probs]
    xb = [_bf(_dot(a_, h) + ak) for a_, h, ak in zip(at_bf, h_bf, akv)]
    tick()
    uv = [jnp.concatenate([_bf(_dot(tbi, xi)), v], axis=0)
          for tbi, xi, v in zip(tb, xb, v_bf)]
    tick()
    y = [_dot(rn, h) + nat(_dot(rbk_i, uvi)) for rn, h, rbk_i, uvi in zip(r_nat, h_bf, rbk, uv)]
    e_col = [jnp.broadcast_to(el, (gw, gw)).T for el in e_l]
    h_new = [ec * p[6] + _dot_tn(bke, uvi) for ec, p, bke, uvi in zip(e_col, probs, bk_end, uv)]
    if side is not None:
        for _ in side:
            pass
    return list(zip(y, h_new))


def _chunk_row_block(b, i, *, reverse, n_lat, seq, ctx_len):
    nc = ctx_len // CHUNK
    nl = seq // CHUNK
    ci = (nc - 1 - i) if reverse else i
    li = (nl - 1 - (i - nc)) if reverse else (i - nc)
    return jnp.where(i < nc, (n_lat + b * ctx_len) // CHUNK + ci, b * nl + li)


def _gla_stages(qkf, vf, gdf, qkb, vb, gdb, gw2_ref, gb_ref, of_ref, ob_ref, sf_ref, sb_ref):
    dv, dk = sf_ref.shape[1:]
    L = CHUNK
    wk = B_HEADS * dk
    t_r = lax.broadcasted_iota(jnp.int32, (L, L), 0)
    t_c = lax.broadcasted_iota(jnp.int32, (L, L), 1)

    qh, kh, keh, vh, dech, keeph, sth = [], [], [], [], [], [], []
    for d, (qk_ref, v_ref, gd_ref, st_ref) in enumerate(((qkf, vf, gdf, sf_ref),
                                                         (qkb, vb, gdb, sb_ref))):
        reverse = d == 1
        keep = (t_c >= t_r) if reverse else (t_c <= t_r)
        tri = _bf(jnp.where(keep, 1.0, 0.0))
        z = _dot(_bf(gd_ref[...]), gw2_ref[d]) + gb_ref[d:d + 1, :]
        la = -_softplus(-z) * (1.0 / B_GATE_TAU)
        cum = _dot_exact_lhs(tri, la)
        last = 0 if reverse else L - 1
        cl = cum[last:last + 1, :]
        q_in = _bf(qk_ref[:, :wk] * (dk ** -0.5) * jnp.exp(cum))
        k_in = _bf(qk_ref[:, wk:] * jnp.exp(-cum))
        k_end = _bf(qk_ref[:, wk:] * jnp.exp(cl - cum))
        dec = jnp.exp(cl)
        for h in range(B_HEADS):
            ks = slice(h * dk, (h + 1) * dk)
            qh.append(q_in[:, ks])
            kh.append(k_in[:, ks])
            keh.append(k_end[:, ks])
            vh.append(_bf(v_ref[:, h * dv:(h + 1) * dv]))
            dech.append(dec[:, ks])
            keeph.append(keep)
            sth.append(st_ref[h])
        yield

    sc = [_bf(jnp.where(kp, _dot_nt(q, k), 0.0)) for kp, q, k in zip(keeph, qh, kh)]
    yield
    inter = [_dot_nt(q, _bf(st)) for q, st in zip(qh, sth)]
    yield
    kvt = [_dot_tn(v, ke) for v, ke in zip(vh, keh)]
    yield
    out = [_dot(s_, v) + it for s_, v, it in zip(sc, vh, inter)]
    yield
    for d, (o_ref, st_ref) in enumerate(((of_ref, sf_ref), (ob_ref, sb_ref))):
        for h in range(B_HEADS):
            i = d * B_HEADS + h
            o_ref[:, h * dv:(h + 1) * dv] = out[i]
            st_ref[h] = sth[i] * dech[i] + kvt[i]


def _mix_scan_kernel(rf, vf, kkf, lwf, kdf, bf_, rb, vb, kkb, lwb, kdb, bb,
                     qkf, gvf, gdf, qkb, gvb, gdb, gw2_ref, gb_ref,
                     yf_ref, yb_ref, of_ref, ob_ref, hf_ref, hb_ref, sf_ref, sb_ref):
    @pl.when(pl.program_id(1) == 0)
    def _():
        for ref in (hf_ref, hb_ref, sf_ref, sb_ref):
            ref[...] = jnp.zeros_like(ref)

    gs = hf_ref.shape[0]
    probs = []
    for gi in range(gs):
        sl = slice(gi * GROUP_W, (gi + 1) * GROUP_W)
        f32 = lambda ref: ref[:, sl].astype(F32)
        probs.append((f32(rf), f32(vf), f32(kkf), lwf[:, sl], f32(kdf), f32(bf_), hf_ref[gi], False))
        probs.append((f32(rb), f32(vb), f32(kkb), lwb[:, sl], f32(kdb), f32(bb), hb_ref[gi], True))
    gla = _gla_stages(qkf, gvf, gdf, qkb, gvb, gdb, gw2_ref, gb_ref, of_ref, ob_ref, sf_ref,
                      sb_ref)
    res = _rwkv_chunks(probs, side=gla)
    for gi in range(gs):
        sl = slice(gi * GROUP_W, (gi + 1) * GROUP_W)
        yf_ref[:, sl], hf_ref[gi] = res[2 * gi]
        yb_ref[:, sl], hb_ref[gi] = res[2 * gi + 1]


def _mix_scan(r, v, kk, lw0, lw1, kd0, kd1, b0, b1, pb, gw2p, gb, *, batch, n_lat, seq, ctx_len,
              wk, wv, col0):
    rows, aw = r.shape
    assert aw == GROUP_W * RWKV_GROUPS_PER_STEP
    dk, dv = wk // B_HEADS, wv // B_HEADS
    geo = dict(n_lat=n_lat, seq=seq, ctx_len=ctx_len)
    gd_blk = (col0 + 2 * wk + 2 * wv) // LANE

    def specs(reverse):
        rb = lambda b, i: _chunk_row_block(b, i, reverse=reverse, **geo)
        rw = pl.BlockSpec((CHUNK, aw), lambda b, i: (rb(b, i), 0))
        gla = [
            pl.BlockSpec((CHUNK, 2 * wk), lambda b, i: (rb(b, i), col0 // (2 * wk))),
            pl.BlockSpec((CHUNK, wv), lambda b, i: (rb(b, i), (col0 + 2 * wk) // wv)),
            pl.BlockSpec((CHUNK, LANE), lambda b, i: (rb(b, i), gd_blk)),
        ]
        out = pl.BlockSpec((CHUNK, wv), lambda b, i: (rb(b, i), 0))
        return rw, gla, out

    rwf, glaf, outf = specs(False)
    rwb, glab, outb = specs(True)
    sds_a = jax.ShapeDtypeStruct((rows, aw), F32)
    sds_b = jax.ShapeDtypeStruct((rows, wv), F32)
    gs = RWKV_GROUPS_PER_STEP
    return pl.pallas_call(
        _mix_scan_kernel,
        grid=(batch, (seq + ctx_len) // CHUNK),
        in_specs=[rwf] * 6 + [rwb] * 6 + glaf + glab + [
            pl.BlockSpec((2, LANE, wk), lambda b, i: (0, 0, 0)),
            pl.BlockSpec((2, wk), lambda b, i: (0, 0)),
        ],
        out_specs=[rwf, rwb, outf, outb],
        out_shape=[sds_a, sds_a, sds_b, sds_b],
        scratch_shapes=[pltpu.VMEM((gs, GROUP_W, GROUP_W), F32),
                        pltpu.VMEM((gs, GROUP_W, GROUP_W), F32),
                        pltpu.VMEM((B_HEADS, dv, dk), F32), pltpu.VMEM((B_HEADS, dv, dk), F32)],
        compiler_params=_cparams(("parallel", "arbitrary")),
    )(r, v, kk, lw0, kd0, b0, r, v, kk, lw1, kd1, b1, pb, pb, pb, pb, pb, pb, gw2p, gb)


def _mixer_out_kernel(yf, yb, bonus, g, of, ob, gate, xa_ref, xb_ref, mod_ref, lnw, lnb, gn, e_ref,
                      w_ref, o_ref, h_ref, *, aw, dv, n_a_tiles):
    e_bf = e_ref[...]
    y = yf[...] + yb[...]
    inv = 1.0 / A_HEAD_DIM
    mean = _head_sum(y, e_bf) * inv
    dlt = y - mean
    var = _head_sum(dlt * dlt, e_bf) * inv
    yn = dlt * lax.rsqrt(var + A_LN_EPS) * lnw[...] + lnb[...]
    h_ref[:, :aw] = _bf((yn + bonus[...]) * g[...])

    o = of[...] + ob[...]
    gt = _silu(gate[...])
    for h in range(B_HEADS):
        sl = slice(h * dv, (h + 1) * dv)
        oh = o[:, sl]
        ms = jnp.mean(oh * oh, axis=-1, keepdims=True)
        h_ref[:, aw + h * dv:aw + (h + 1) * dv] = _bf(oh * lax.rsqrt(ms + NORM_EPS) * gn[...] * gt[:, sl])

    x = jnp.where(pl.program_id(0) < n_a_tiles, xa_ref[...], xb_ref[...])
    o_ref[...] = x + mod_ref[2:3, :] * _dot(h_ref[...], w_ref[...])


def _mixer_out(yf, yb, bonus, g, of, ob, pb, xa, xb, mods_l, ln_w, ln_b, gla_norm, e_bf, w_bf,
               seg_of_tile, *, wk, col0):
    d = xa.shape[1]
    rows = xa.shape[0] + xb.shape[0]
    aw = yf.shape[1]
    wv = of.shape[1]
    dv = wv // B_HEADS
    tm = 256
    tiles_per_row_tile = ROW_TILE // tm
    n_a_tiles = xa.shape[0] // tm
    rs = lambda w: pl.BlockSpec((tm, w), lambda i: (i, 0))
    full = lambda shape: pl.BlockSpec(shape, lambda i: (0,) * len(shape))
    return pl.pallas_call(
        functools.partial(_mixer_out_kernel, aw=aw, dv=dv, n_a_tiles=n_a_tiles),
        grid=(rows // tm,),
        in_specs=[
            rs(aw), rs(aw), rs(aw), rs(aw), rs(wv), rs(wv),
            pl.BlockSpec((tm, wv), lambda i: (i, (col0 + 2 * wk + wv) // wv)),
        ] + _two_source_specs(tm, d, n_a_tiles) + [
            pl.BlockSpec((None, N_MOD, d), lambda i: (seg_of_tile(i // tiles_per_row_tile), 0, 0)),
            full((1, aw)), full((1, aw)), full((1, dv)), full((GROUP_W, GROUP_W)),
            full((aw + wv, d)),
        ],
        out_specs=rs(d),
        out_shape=jax.ShapeDtypeStruct((rows, d), F32),
        scratch_shapes=[pltpu.VMEM((tm, aw + wv), BF16)],
        compiler_params=_cparams(("parallel",)),
    )(yf, yb, bonus, g, of, ob, pb, xa, xb, mods_l, ln_w.reshape(1, aw), ln_b.reshape(1, aw),
      gla_norm.reshape(1, dv), e_bf, w_bf)


def _lru_kernel(*refs, reverse, n_blk, combine, cols):
    if combine:
        (x_ref, xp_ref, xn_ref, h0_ref, cw_ref, cb_ref, wc_ref, ba_ref, bx_ref, lam_ref,
         hf_ref, gate_ref, out_ref, hl_ref, a_sc, u_sc, hs_sc, carry_sc) = refs
    else:
        (x_ref, xp_ref, xn_ref, h0_ref, cw_ref, cb_ref, wc_ref, ba_ref, bx_ref, lam_ref,
         out_ref, hl_ref, a_sc, u_sc, hs_sc, carry_sc) = refs
    j = pl.program_id(2)
    blk = (n_blk - 1 - j) if reverse else j
    lb = x_ref.shape[0]
    ct = cw_ref.shape[1]
    cblk = wc_ref.shape[1]
    rowid = lax.broadcasted_iota(jnp.int32, (lb, 1), 0)
    decay_rate = -C_CONST * _softplus(-lam_ref[...])

    @pl.when(j == 0)
    def _():
        carry_sc[...] = h0_ref[0:1, :]

    for q in (range(cols - 1, -1, -1) if reverse else range(cols)):
        lanes = slice(q * ct, (q + 1) * ct)
        x = x_ref[:, lanes]
        if q > 0:
            prev = slice((q - 1) * ct, q * ct)
            p6, p7 = x_ref[lb - 2:lb - 1, prev], x_ref[lb - 1:lb, prev]
        else:
            p6 = jnp.where(blk == 0, 0.0, xp_ref[SUBLANE - 2:SUBLANE - 1, :])
            p7 = jnp.where(blk == 0, 0.0, xp_ref[SUBLANE - 1:SUBLANE, :])
        if q < cols - 1:
            n0 = x_ref[0:1, (q + 1) * ct:(q + 2) * ct]
        else:
            n0 = jnp.where(blk == n_blk - 1, 0.0, xn_ref[0:1, :])
        xm1 = jnp.where(rowid == 0, p7, pltpu.roll(x, 1, 0))
        xm2 = jnp.where(rowid == 0, p6, jnp.where(rowid == 1, p7, pltpu.roll(x, 2, 0)))
        xp1 = jnp.where(rowid == lb - 1, n0, pltpu.roll(x, lb - 1, 0))
        xs = (cb_ref[...] + cw_ref[0:1, :] * xm2 + cw_ref[1:2, :] * xm1 + cw_ref[2:3, :] * x
              + cw_ref[3:4, :] * xp1)

        for n in range(ct // cblk):
            sl = slice(n * cblk, (n + 1) * cblk)
            xb = xs[:, sl]
            ri = _dot(_bf(xb), wc_ref[n])
            rg = _sigmoid(ri[:, :cblk] + ba_ref[:, sl])
            ig = _sigmoid(ri[:, cblk:] + bx_ref[:, sl])
            a = jnp.exp(decay_rate[:, sl] * rg)
            a_sc[q, :, sl] = a
            u_sc[q, :, sl] = jnp.sqrt(1.0 - a * a) * (ig * xb)

        def step(t, h, q=q):
            tt = (lb - 1 - t) if reverse else t
            h = a_sc[q, pl.ds(tt, 1), :] * h + u_sc[q, pl.ds(tt, 1), :]
            hs_sc[q, pl.ds(tt, 1), :] = h
            return h

        carry_sc[...] = lax.fori_loop(0, lb, step, carry_sc[...], unroll=8)
        if combine:
            out_ref[:, lanes] = ((hs_sc[q] + hf_ref[q * lb:(q + 1) * lb, :])
                                 * jax.nn.gelu(gate_ref[:, lanes]))
        else:
            out_ref[q * lb:(q + 1) * lb, :] = hs_sc[q]
    hl_ref[...] = jnp.broadcast_to(carry_sc[...], hl_ref.shape)


def _lru_pass(x_arr, x_specs, h0, conv_w, conv_b, wcat, ba, bx, lam, out_shape, out_spec, extra,
              *, batch, n_blk, lb, cols, reverse):
    c = conv_w.shape[1]
    ct = LRU_CHANNEL_TILE
    nct = c // ct
    cblk = wcat.shape[1]
    chan = lambda rows_: pl.BlockSpec((rows_, ct), lambda b, k, j: (0, k))
    in_specs = list(x_specs) + [
        pl.BlockSpec((SUBLANE, ct), lambda b, k, j: (b, k)),
        chan(C_CONV), chan(1),
        pl.BlockSpec((ct // cblk, cblk, 2 * cblk), lambda b, k, j: (k, 0, 0)),
        chan(1), chan(1), chan(1),
    ] + [s for _, s in extra]
    args = [x_arr, x_arr, x_arr, h0, conv_w, conv_b.reshape(1, c), wcat, ba.reshape(1, c),
            bx.reshape(1, c), lam.reshape(1, c)] + [a for a, _ in extra]
    return pl.pallas_call(
        functools.partial(_lru_kernel, reverse=reverse, n_blk=n_blk, combine=bool(extra),
                          cols=cols),
        grid=(batch, nct, n_blk),
        in_specs=in_specs,
        out_specs=[out_spec, pl.BlockSpec((SUBLANE, ct), lambda b, k, j: (b, k))],
        out_shape=[out_shape, jax.ShapeDtypeStruct((batch * SUBLANE, c), F32)],
        scratch_shapes=[pltpu.VMEM((cols, lb, ct), F32), pltpu.VMEM((cols, lb, ct), F32),
                        pltpu.VMEM((cols, lb, ct), F32), pltpu.VMEM((1, ct), F32)],
        compiler_params=_cparams(("parallel", "parallel", "arbitrary")),
    )(*args)


def _rglru(hw_ctx, gate_grid, x_grid, conv_w, conv_b, wa, ba, wx, bx, lam, *, batch, n_lat, seq,
           ctx_len):
    c = conv_w.shape[1]
    ct = LRU_CHANNEL_TILE
    rows = seq // GRID_W
    lbc = 128
    ncb = ctx_len // lbc
    xcol = c // ct

    wcat = [_bf(jnp.concatenate([wa[d], wx[d]], axis=-1)) for d in range(2)]
    zeros_state = jnp.zeros((batch * SUBLANE, c), F32)

    def ctx_specs(reverse):
        blk = (lambda j: ncb - 1 - j) if reverse else (lambda j: j)
        per8 = lbc // SUBLANE
        cur = pl.BlockSpec((lbc, ct), lambda b, k, j: (b * ncb + blk(j), xcol + k))
        prv = pl.BlockSpec((SUBLANE, ct), lambda b, k, j: (
            b * ncb * per8 + jnp.maximum(blk(j) * per8 - 1, 0), xcol + k))
        nxt = pl.BlockSpec((SUBLANE, ct), lambda b, k, j: (
            b * ncb * per8 + jnp.minimum((blk(j) + 1) * per8, ncb * per8 - 1), xcol + k))
        return cur, prv, nxt

    assert ct == c
    kc = LRU_COLS_PER_STEP
    ncg = GRID_W // kc

    def lat_specs(reverse):
        grp = (lambda j: ncg - 1 - j) if reverse else (lambda j: j)
        r8 = rows // SUBLANE
        cur = pl.BlockSpec((rows, kc * ct), lambda b, k, j: (b, grp(j)))
        prv = pl.BlockSpec((SUBLANE, ct), lambda b, k, j: (
            b * r8 + r8 - 1, jnp.maximum(grp(j) * kc - 1, 0)))
        nxt = pl.BlockSpec((SUBLANE, ct), lambda b, k, j: (
            b * r8, jnp.minimum(grp(j) * kc + kc, GRID_W - 1)))
        hf = pl.BlockSpec((kc * rows, ct), lambda b, k, j: (b * ncg + grp(j), 0))
        return (cur, prv, nxt), hf, cur

    common = dict(batch=batch)
    ctx_scratch_sds = jax.ShapeDtypeStruct((batch * ctx_len, c), F32)
    ctx_out = lambda reverse: pl.BlockSpec(
        (lbc, ct), lambda b, k, j: (b * ncb + ((ncb - 1 - j) if reverse else j), k))
    _, st = _lru_pass(hw_ctx, ctx_specs(False), zeros_state, conv_w, conv_b, wcat[0], ba[0], bx[0],
                      lam[0], ctx_scratch_sds, ctx_out(False), [], n_blk=ncb, lb=lbc, cols=1,
                      reverse=False, **common)
    hf_sds = jax.ShapeDtypeStruct((n_lat, c), F32)
    x_specs, hf_spec, _ = lat_specs(False)
    hf, _ = _lru_pass(x_grid, x_specs, st, conv_w, conv_b, wcat[0], ba[0], bx[0], lam[0],
                      hf_sds, hf_spec, [], n_blk=ncg, lb=rows, cols=kc, reverse=False, **common)
    _, st = _lru_pass(hw_ctx, ctx_specs(True), zeros_state, conv_w, conv_b, wcat[1], ba[1], bx[1],
                      lam[1], ctx_scratch_sds, ctx_out(True), [], n_blk=ncb, lb=lbc, cols=1,
                      reverse=True, **common)
    x_specs, hf_spec, grid_spec = lat_specs(True)
    y_sds = jax.ShapeDtypeStruct((n_lat // GRID_W, GRID_W * c), F32)
    y, _ = _lru_pass(x_grid, x_specs, st, conv_w, conv_b, wcat[1], ba[1], bx[1], lam[1],
                     y_sds, grid_spec, [(hf, hf_spec), (gate_grid, grid_spec)],
                     n_blk=ncg, lb=rows, cols=kc, reverse=True, **common)
    return y


def _pad_cols(w, n):
    return jnp.pad(w, ((0, 0),) * (w.ndim - 1) + ((0, n - w.shape[-1]),))


def kernel(x, c, ctx, c_ctx, mod_w, mod_b, norm1, norm2, mlp_w1, mlp_w2, ab_w_in, ab_w_out, rw_mu,
           rw_w0, rw_w2, rw_a0, rw_a2, rw_g2, rw_kk, rw_ka, rw_rk, rw_ln_w, rw_ln_b, gla_gw2,
           gla_gb, gla_norm, lru_w_in, lru_w_out, lru_conv_w, lru_conv_b, lru_wa, lru_ba, lru_wx,
           lru_bx, lru_lam, final_norm):
    batch, seq, d = x.shape
    ctx_len = ctx.shape[1]
    depth = mod_w.shape[0]
    n_lat = batch * seq
    aw = rw_w0.shape[-1]
    wk = gla_gb.shape[-1]
    wv = ab_w_out.shape[1] - aw
    a_cols = rw_mu.shape[-1]
    b_cols = ab_w_in.shape[-1] - a_cols
    assert depth == 2 and batch + 1 <= SUBLANE
    assert seq % ROW_TILE == 0 and (batch * ctx_len) % ROW_TILE == 0
    assert ctx_len % PREP_TILE == 0 and ctx_len % 128 == 0 and seq % (GRID_W * SUBLANE) == 0
    assert aw % GROUP_W == 0 and wv == aw and 2 * wk == wv

    tiles_per_seq = seq // ROW_TILE
    seg_of_tile = lambda i: jnp.minimum(i // tiles_per_seq, batch)

    x2 = x.reshape(n_lat, d)
    ctx2 = ctx.reshape(batch * ctx_len, d)
    cvec = jnp.concatenate(
        [c, c_ctx[None, :], jnp.zeros((SUBLANE - batch - 1, d), F32)], axis=0)
    mods = _mods(cvec, mod_w, mod_b)

    hid = jnp.arange(GROUP_W) // A_HEAD_DIM
    e_bf = _bf(hid[:, None] == hid[None, :])

    a_pad = -(-a_cols // LANE) * LANE
    col0 = -(-a_cols // (2 * wk)) * (2 * wk)
    n_tiles = 3
    tn = -(-(col0 + b_cols) // (n_tiles * LANE)) * LANE
    w_ab = _bf(jnp.concatenate(
        [_pad_cols(ab_w_in[0][:, :a_cols], col0),
         _pad_cols(ab_w_in[0][:, a_cols:], n_tiles * tn - col0)], axis=1))
    pa = pb = _norm_matmul(x2, ctx2, norm1[0], mods[0], w_ab, seg_of_tile, tn=tn)

    lora_w = 2 * A_LORA_W + 2 * A_LORA_A
    w2p = jnp.zeros((2, lora_w, aw), F32)
    a2p = jnp.zeros((2, lora_w, aw), F32)
    for dd in range(2):
        w2p = w2p.at[dd, dd * A_LORA_W:(dd + 1) * A_LORA_W].set(rw_w2[0, dd])
        a2p = a2p.at[dd, 2 * A_LORA_W + dd * A_LORA_A:2 * A_LORA_W + (dd + 1) * A_LORA_A].set(
            rw_a2[0, dd])
    (r_, v_, kk_, lw0, lw1, kd0, kd1, b0, b1, g_, bonus) = _rwkv_prep(
        pa, _pad_cols(rw_mu[0], a_pad), rw_w0[0], rw_a0[0], _bf(w2p), _bf(a2p), _bf(rw_g2[0]),
        rw_kk[0], rw_ka[0], rw_rk[0].reshape(-1), e_bf, n_lat=n_lat, seq=seq, ctx_len=ctx_len)
    gw2p = jnp.zeros((2, LANE, wk), F32)
    for dd in range(2):
        gw2p = gw2p.at[dd, dd * B_LORA:(dd + 1) * B_LORA].set(gla_gw2[0, dd])
    yf, yb, of, ob = _mix_scan(r_, v_, kk_, lw0, lw1, kd0, kd1, b0, b1, pb, _bf(gw2p), gla_gb[0],
                               batch=batch, n_lat=n_lat, seq=seq, ctx_len=ctx_len, wk=wk, wv=wv,
                               col0=col0)

    xs = _mixer_out(yf, yb, bonus, g_, of, ob, pb, x2, ctx2, mods[0], rw_ln_w[0], rw_ln_b[0],
                    gla_norm[0], e_bf, _bf(ab_w_out[0]), seg_of_tile, wk=wk, col0=col0)
    seg_of_mlp_tile = lambda i: jnp.minimum(i // (seq // MLP_TILE), batch)
    ctx_tiles = (batch * ctx_len) // ROW_TILE
    w1_bf, w2_bf = _bf(mlp_w1[0]), _bf(mlp_w2[0])
    x_lat = _mlp(xs, norm2[0], mods[0], w1_bf, w2_bf, final_norm, seg_of_mlp_tile, tm=MLP_TILE,
                 tile0=0, n_tiles=n_lat // MLP_TILE, final_norm=False)
    x_ctx = _mlp(xs, norm2[0], mods[0], w1_bf, w2_bf, final_norm, seg_of_tile, tm=ROW_TILE,
                 tile0=n_lat // ROW_TILE, n_tiles=ctx_tiles, final_norm=False)

    w_lru = _bf(lru_w_in[0])
    hw_ctx = _norm_matmul(x_ctx, None, norm1[1], mods[1], w_lru, lambda i: batch, tn=2048)
    gate_grid, x_grid = _norm_matmul_grid(x_lat, n_lat, seq // GRID_W, norm1[1], mods[1], w_lru)
    y = _rglru(hw_ctx, gate_grid, x_grid, lru_conv_w[0], lru_conv_b[0], lru_wa[0], lru_ba[0],
               lru_wx[0], lru_bx[0], lru_lam[0], batch=batch, n_lat=n_lat, seq=seq,
               ctx_len=ctx_len)
    xl = _proj_res(y, x_lat, n_lat, seq // GRID_W, mods[1], _bf(lru_w_out[0]))
    out = _mlp(xl, norm2[1], mods[1], _bf(mlp_w1[1]), _bf(mlp_w2[1]), final_norm, seg_of_mlp_tile,
               tm=MLP_TILE, tile0=0, n_tiles=n_lat // MLP_TILE, final_norm=True)
    return out.reshape(batch, seq, d)
```

```python
import functools

import jax
import jax.numpy as jnp
from jax import lax
from jax.experimental import pallas as pl
from jax.experimental.pallas import tpu as pltpu

F32 = jnp.float32
BF16 = jnp.bfloat16

NORM_EPS = 1e-6
GRID_W = 64
N_MOD = 6

A_HEAD_DIM = 64
A_LORA_W = 96
A_LORA_A = 96
A_LORA_G = 256
A_LN_EPS = 64e-5
HEADS_PER_GROUP = 4
GROUP_W = HEADS_PER_GROUP * A_HEAD_DIM
CHUNK = 64
RWKV_GROUPS_PER_STEP = 4

B_HEADS = 4
B_LORA = 16
B_GATE_TAU = 16.0

C_BLOCKS = 8
C_CONV = 4
C_CONST = 8.0

LANE = 128
SUBLANE = 8
ROW_TILE = 512
MLP_TILE = 1024
MLP_STEP_ELEMS = 1024 * 512
LRU_CHANNEL_TILE = 2048
LRU_COLS_PER_STEP = 4
PREP_TILE = 128
VMEM_LIMIT = 56 * 1024 * 1024


def _cparams(sem):
    return pltpu.CompilerParams(dimension_semantics=sem, vmem_limit_bytes=VMEM_LIMIT)


def _bf(x):
    return x.astype(BF16)


def _dot(a, b):
    return jnp.dot(a, b, preferred_element_type=F32)


def _dot_nt(a, b):
    return lax.dot_general(a, b, (((1,), (1,)), ((), ())), preferred_element_type=F32)


def _dot_tn(a, b):
    return lax.dot_general(a, b, (((0,), (0,)), ((), ())), preferred_element_type=F32)


def _split3(x):
    hi = _bf(x)
    r1 = x - hi.astype(F32)
    mid = _bf(r1)
    lo = _bf(r1 - mid.astype(F32))
    return hi, mid, lo


def _dot_exact_lhs(m_bf, x):
    hi, mid, lo = _split3(x)
    return _dot(m_bf, hi) + _dot(m_bf, mid) + _dot(m_bf, lo)


def _dot_exact_rhs(x, m_bf):
    hi, mid, lo = _split3(x)
    return _dot(hi, m_bf) + _dot(mid, m_bf) + _dot(lo, m_bf)


def _softplus(x):
    return jnp.maximum(x, 0.0) + jnp.log(1.0 + jnp.exp(-jnp.abs(x)))


def _sigmoid(x):
    return 0.5 * jnp.tanh(0.5 * x) + 0.5


def _silu(x):
    return x * _sigmoid(x)


def _norm_mod(x, g, shift, scale):
    ms = jnp.mean(x * x, axis=-1, keepdims=True)
    return (x * lax.rsqrt(ms + NORM_EPS) * g) * (1.0 + scale) + shift


def _mods_kernel(c_ref, w_ref, b_ref, o_ref):
    s = _silu(c_ref[...])
    o_ref[...] = _dot(_bf(s), _bf(w_ref[...])) + b_ref[...]


def _mods(cvec, mod_w, mod_b):
    depth, d, n = mod_w.shape
    tn = 1024
    out = pl.pallas_call(
        _mods_kernel,
        grid=(depth, n // tn),
        in_specs=[
            pl.BlockSpec((SUBLANE, d), lambda l, j: (0, 0)),
            pl.BlockSpec((None, d, tn), lambda l, j: (l, 0, j)),
            pl.BlockSpec((None, 1, tn), lambda l, j: (l, 0, j)),
        ],
        out_specs=pl.BlockSpec((None, SUBLANE, tn), lambda l, j: (l, 0, j)),
        out_shape=jax.ShapeDtypeStruct((depth, SUBLANE, n), F32),
        compiler_params=_cparams(("parallel", "parallel")),
    )(cvec, mod_w, mod_b.reshape(depth, 1, n))
    return out.reshape(depth, SUBLANE, N_MOD, d)


def _norm_matmul_kernel(xa_ref, xb_ref, g_ref, mod_ref, w_ref, o_ref, h_ref, *, n_a_tiles):
    i = pl.program_id(0)

    @pl.when(pl.program_id(1) == 0)
    def _():
        x = jnp.where(i < n_a_tiles, xa_ref[...], xb_ref[...])
        h_ref[...] = _bf(_norm_mod(x, g_ref[...], mod_ref[0:1, :], mod_ref[1:2, :]))

    o_ref[...] = _dot(h_ref[...], w_ref[...])


def _two_source_specs(tm, d, n_a_tiles):
    return [pl.BlockSpec((tm, d), lambda i, *_: (jnp.minimum(i, n_a_tiles - 1), 0)),
            pl.BlockSpec((tm, d), lambda i, *_: (jnp.maximum(i - n_a_tiles, 0), 0))]


def _norm_matmul(xa, xb, g, mods_l, w_bf, seg_of_tile, *, tn):
    d = xa.shape[1]
    n = w_bf.shape[1]
    tm = ROW_TILE
    n_a_tiles = xa.shape[0] // tm
    n_tiles = n_a_tiles + (0 if xb is None else xb.shape[0] // tm)
    return pl.pallas_call(
        functools.partial(_norm_matmul_kernel, n_a_tiles=n_a_tiles),
        grid=(n_tiles, n // tn),
        in_specs=_two_source_specs(tm, d, n_a_tiles) + [
            pl.BlockSpec((1, d), lambda i, j: (0, 0)),
            pl.BlockSpec((None, N_MOD, d), lambda i, j: (seg_of_tile(i), 0, 0)),
            pl.BlockSpec((d, tn), lambda i, j: (0, j)),
        ],
        out_specs=pl.BlockSpec((tm, tn), lambda i, j: (i, j)),
        out_shape=jax.ShapeDtypeStruct((n_tiles * tm, n), F32),
        scratch_shapes=[pltpu.VMEM((tm, d), BF16)],
        compiler_params=_cparams(("parallel", "arbitrary")),
    )(xa, xa if xb is None else xb, g.reshape(1, d), mods_l, w_bf)


def _grid_tile(rows):
    gr = min(rows, 64)
    gc = ROW_TILE // gr
    assert gc % SUBLANE == 0 and GRID_W % gc == 0 and rows % gr == 0
    return gr, gc


def _norm_matmul_grid_kernel(x_ref, g_ref, mod_ref, w_ref, o0_ref, o1_ref, h_ref, res_ref):
    j = pl.program_id(1)
    gr, gc, d = x_ref.shape

    @pl.when(j == 0)
    def _():
        h = _norm_mod(x_ref[...].reshape(gr * gc, d), g_ref[...], mod_ref[0:1, :], mod_ref[1:2, :])
        h_ref[...] = _bf(h)

    res = _dot(h_ref[...], w_ref[...])
    nh = w_ref.shape[1]
    nq = nh // LANE
    for q in range(nq):
        res_ref[q] = res[:, q * LANE:(q + 1) * LANE]

    def scatter(o_ref):
        for c in range(gc):
            for q in range(nq):
                o_ref[:, c * nh + q * LANE:c * nh + (q + 1) * LANE] = (
                    res_ref[q, pl.ds(c, gr, stride=gc), :])

    @pl.when(j == 0)
    def _():
        scatter(o0_ref)

    @pl.when(j == 1)
    def _():
        scatter(o1_ref)


def _norm_matmul_grid(x, n_lat, rows, g, mods_l, w_bf):
    d = x.shape[1]
    nh = w_bf.shape[1] // 2
    gr, gc = _grid_tile(rows)
    ncg = GRID_W // gc
    x3 = x.reshape(x.shape[0] // GRID_W, GRID_W, d)
    out_spec = pl.BlockSpec((gr, gc * nh), lambda i, j: (i // ncg, i % ncg))
    sds = jax.ShapeDtypeStruct((n_lat // GRID_W, GRID_W * nh), F32)
    return pl.pallas_call(
        _norm_matmul_grid_kernel,
        grid=(n_lat // ROW_TILE, 2),
        in_specs=[
            pl.BlockSpec((gr, gc, d), lambda i, j: (i // ncg, i % ncg, 0)),
            pl.BlockSpec((1, d), lambda i, j: (0, 0)),
            pl.BlockSpec((None, N_MOD, d), lambda i, j: ((i // ncg) * gr // rows, 0, 0)),
            pl.BlockSpec((d, nh), lambda i, j: (0, j)),
        ],
        out_specs=[out_spec, out_spec],
        out_shape=[sds, sds],
        scratch_shapes=[pltpu.VMEM((ROW_TILE, d), BF16),
                        pltpu.VMEM((nh // LANE, ROW_TILE, LANE), F32)],
        compiler_params=_cparams(("parallel", "arbitrary")),
    )(x3, g.reshape(1, d), mods_l, w_bf)


def _mlp_kernel(x_ref, g_ref, mod_ref, w1_ref, w2_ref, fn_ref, o_ref, h_ref, *, final_norm):
    j = pl.program_id(1)
    slab = 256
    n_slab = x_ref.shape[0] // slab

    @pl.when(j == 0)
    def _():
        for s in range(n_slab):
            rs = slice(s * slab, (s + 1) * slab)
            h_ref[rs, :] = _bf(_norm_mod(x_ref[rs, :], g_ref[...], mod_ref[3:4, :],
                                         mod_ref[4:5, :]))
        o_ref[...] = jnp.zeros_like(o_ref)

    a = _dot(h_ref[...], w1_ref[...])
    a = _bf(jnp.square(jnp.maximum(a, 0.0)))
    nsplit = 2
    wn = o_ref.shape[1] // nsplit
    for s in range(nsplit):
        o_ref[:, s * wn:(s + 1) * wn] += _dot(a, w2_ref[:, s * wn:(s + 1) * wn])

    @pl.when(j == pl.num_programs(1) - 1)
    def _():
        for s in range(n_slab):
            rs = slice(s * slab, (s + 1) * slab)
            xo = x_ref[rs, :] + mod_ref[5:6, :] * o_ref[rs, :]
            if final_norm:
                ms = jnp.mean(xo * xo, axis=-1, keepdims=True)
                xo = xo * lax.rsqrt(ms + NORM_EPS) * fn_ref[...]
            o_ref[rs, :] = xo


def _mlp(x, g, mods_l, w1_bf, w2_bf, fn, seg_of_tile, *, tm, tile0, n_tiles, final_norm):
    d = x.shape[1]
    dff = w1_bf.shape[1]
    tf = MLP_STEP_ELEMS // tm
    return pl.pallas_call(
        functools.partial(_mlp_kernel, final_norm=final_norm),
        grid=(n_tiles, dff // tf),
        in_specs=[
            pl.BlockSpec((tm, d), lambda i, j: (i + tile0, 0), pipeline_mode=pl.Buffered(1)),
            pl.BlockSpec((1, d), lambda i, j: (0, 0)),
            pl.BlockSpec((None, N_MOD, d), lambda i, j: (seg_of_tile(i + tile0), 0, 0)),
            pl.BlockSpec((d, tf), lambda i, j: (0, j)),
            pl.BlockSpec((tf, d), lambda i, j: (j, 0)),
            pl.BlockSpec((1, d), lambda i, j: (0, 0)),
        ],
        out_specs=pl.BlockSpec((tm, d), lambda i, j: (i, 0)),
        out_shape=jax.ShapeDtypeStruct((n_tiles * tm, d), F32),
        scratch_shapes=[pltpu.VMEM((tm, d), BF16)],
        compiler_params=_cparams(("parallel", "arbitrary")),
    )(x, g.reshape(1, d), mods_l, w1_bf, w2_bf, fn.reshape(1, d))


def _proj_res_kernel(y_ref, x_ref, mod_ref, w_ref, o_ref, yt_ref):
    k = w_ref.shape[0]
    gr, gc, d = x_ref.shape
    nq = k // LANE
    for c in range(gc):
        for q in range(nq):
            yt_ref[q, pl.ds(c, gr, stride=gc), :] = (
                y_ref[:, c * k + q * LANE:c * k + (q + 1) * LANE])
    yt = jnp.concatenate([_bf(yt_ref[q]) for q in range(nq)], axis=1)
    out = x_ref[...].reshape(gr * gc, d) + mod_ref[2:3, :] * _dot(yt, w_ref[...])
    o_ref[...] = out.reshape(gr, gc, d)


def _proj_res(y_grid, x, n_lat, rows, mods_l, w_bf):
    k, d = w_bf.shape
    gr, gc = _grid_tile(rows)
    ncg = GRID_W // gc
    x3 = x.reshape(x.shape[0] // GRID_W, GRID_W, d)
    tok_spec = pl.BlockSpec((gr, gc, d), lambda i: (i // ncg, i % ncg, 0))
    out = pl.pallas_call(
        _proj_res_kernel,
        grid=(n_lat // ROW_TILE,),
        in_specs=[
            pl.BlockSpec((gr, gc * k), lambda i: (i // ncg, i % ncg)),
            tok_spec,
            pl.BlockSpec((None, N_MOD, d), lambda i: ((i // ncg) * gr // rows, 0, 0)),
            pl.BlockSpec((k, d), lambda i: (0, 0)),
        ],
        out_specs=tok_spec,
        out_shape=jax.ShapeDtypeStruct((n_lat // GRID_W, GRID_W, d), F32),
        scratch_shapes=[pltpu.VMEM((k // LANE, ROW_TILE, LANE), F32)],
        compiler_params=_cparams(("parallel",)),
    )(y_grid, x3, mods_l, w_bf)
    return out.reshape(n_lat, d)


def _head_sum(z, e_bf):
    parts = []
    for gi in range(z.shape[1] // GROUP_W):
        zs = z[:, gi * GROUP_W:(gi + 1) * GROUP_W]
        parts.append(_dot_exact_rhs(zs, e_bf))
    return jnp.concatenate(parts, axis=1)


def _rwkv_prep_kernel(p_ref, pp_ref, pn_ref, mu_ref, w0_ref, a0_ref, w2_ref, a2_ref, g2_ref,
                      kk_ref, ka_ref, rk_ref, e_ref,
                      r_o, v_o, kk_o, lw0_o, lw1_o, kd0_o, kd1_o, b0_o, b1_o, g_o, bonus_o,
                      *, n_lat_tiles, lat_tiles_per_seq, ctx_tiles_per_seq, aw):
    i = pl.program_id(0)
    tp = p_ref.shape[0]
    in_ctx = i >= n_lat_tiles
    pos = jnp.where(in_ctx, (i - n_lat_tiles) % ctx_tiles_per_seq, i % lat_tiles_per_seq)
    last = jnp.where(in_ctx, ctx_tiles_per_seq - 1, lat_tiles_per_seq - 1)
    is_first = pos == 0
    is_last = pos == last
    rowid = lax.broadcasted_iota(jnp.int32, (tp, 1), 0)

    def shifted(c0, c1):
        x = p_ref[:, c0:c1]
        prow = jnp.where(is_first, 0.0, pp_ref[SUBLANE - 1:SUBLANE, c0:c1])
        nrow = jnp.where(is_last, 0.0, pn_ref[0:1, c0:c1])
        xp = jnp.where(rowid == 0, prow, pltpu.roll(x, 1, 0))
        xn = jnp.where(rowid == tp - 1, nrow, pltpu.roll(x, tp - 1, 0))
        return x + mu_ref[0:1, c0:c1] * (xp - x) + mu_ref[1:2, c0:c1] * (xn - x)

    e_bf = e_ref[...]
    r = shifted(0, aw)
    k = shifted(aw, 2 * aw)
    v = shifted(2 * aw, 3 * aw)
    gd = shifted(3 * aw, 3 * aw + A_LORA_G)
    lo0 = 3 * aw + A_LORA_G
    xl = shifted(lo0, lo0 + 2 * A_LORA_W + 2 * A_LORA_A)

    r_o[...] = r.astype(r_o.dtype)
    v_o[...] = v.astype(v_o.dtype)
    g_o[...] = _dot(_bf(_sigmoid(gd)), g2_ref[...])

    kx = k * kk_ref[...]
    kn = jnp.sqrt(_head_sum(kx * kx, e_bf))
    kk = kx / jnp.maximum(kn, 1e-12)
    kk_o[...] = kk.astype(kk_o.dtype)

    xl_t = _bf(jnp.tanh(xl))
    xl_b = _bf(xl)
    kd_sum = None
    for d, (lw_o, kd_o, b_o) in enumerate(((lw0_o, kd0_o, b0_o), (lw1_o, kd1_o, b1_o))):
        w_log = -_softplus(-(w0_ref[d:d + 1, :] + _dot(xl_t, w2_ref[d]))) - 0.5
        lw_o[...] = -jnp.exp(w_log)
        asig = _sigmoid(a0_ref[d:d + 1, :] + _dot(xl_b, a2_ref[d]))
        kd = k * (1.0 + (asig - 1.0) * ka_ref[...])
        kd_o[...] = kd.astype(kd_o.dtype)
        b_o[...] = (kk * asig).astype(b_o.dtype)
        kd_sum = kd if kd_sum is None else kd_sum + kd
    bonus_o[...] = _head_sum(r * rk_ref[...] * kd_sum, e_bf) * v


def _rwkv_prep(pa, mu_p, w0, a0, w2p, a2p, g2, k_k, k_a, r_k, e_bf, *, n_lat, seq, ctx_len):
    r = pa.shape[0]
    ncol = mu_p.shape[1]
    aw = w0.shape[1]
    tp = PREP_TILE
    hb = tp // SUBLANE
    nblk8 = r // SUBLANE
    lw = 2 * A_LORA_W + 2 * A_LORA_A
    row = lambda a: a.reshape(1, aw)
    full = lambda shape: pl.BlockSpec(shape, lambda i: (0,) * len(shape))
    out_spec = pl.BlockSpec((tp, aw), lambda i: (i, 0))
    out_dtypes = [BF16, BF16, BF16, F32, F32, BF16, BF16, BF16, BF16, F32, F32]
    return pl.pallas_call(
        functools.partial(_rwkv_prep_kernel, n_lat_tiles=n_lat // tp, lat_tiles_per_seq=seq // tp,
                          ctx_tiles_per_seq=ctx_len // tp, aw=aw),
        grid=(r // tp,),
        in_specs=[
            pl.BlockSpec((tp, ncol), lambda i: (i, 0)),
            pl.BlockSpec((SUBLANE, ncol), lambda i: (jnp.maximum(i * hb - 1, 0), 0)),
            pl.BlockSpec((SUBLANE, ncol), lambda i: (jnp.minimum((i + 1) * hb, nblk8 - 1), 0)),
            full((2, ncol)), full((2, aw)), full((2, aw)),
            full((2, lw, aw)), full((2, lw, aw)), full((A_LORA_G, aw)),
            full((1, aw)), full((1, aw)), full((1, aw)), full((GROUP_W, GROUP_W)),
        ],
        out_specs=[out_spec] * 11,
        out_shape=[jax.ShapeDtypeStruct((r, aw), dt) for dt in out_dtypes],
        compiler_params=_cparams(("parallel",)),
    )(pa, pa, pa, mu_p, w0, a0, w2p, a2p, g2, row(k_k), row(k_a), row(r_k), e_bf)


def _rwkv_masks(reverse):
    L = CHUNK
    gw = GROUP_W
    row = lax.broadcasted_iota(jnp.int32, (gw, gw), 0)
    col = lax.broadcasted_iota(jnp.int32, (gw, gw), 1)
    same_head = (row >> 6) == (col >> 6)
    tr = row & (L - 1)
    tc = col & (L - 1)
    strict = same_head & ((tr < tc) if reverse else (tr > tc))
    t_n = lax.broadcasted_iota(jnp.int32, (L, gw), 0)
    s_n = lax.broadcasted_iota(jnp.int32, (L, gw), 1) & (L - 1)
    strict_n = (t_n < s_n) if reverse else (t_n > s_n)
    incl_n = (t_n <= s_n) if reverse else (t_n >= s_n)
    t_r = lax.broadcasted_iota(jnp.int32, (L, L), 0)
    t_c = lax.broadcasted_iota(jnp.int32, (L, L), 1)
    tri = _bf(jnp.where((t_c >= t_r) if reverse else (t_c <= t_r), 1.0, 0.0))
    return dict(same_head=same_head, strict=strict, strict_n=strict_n,
                incl_n2=jnp.concatenate([incl_n, incl_n], axis=1), eye=row == col, tri=tri)


def _rwkv_chunks(probs, side=None):
    L = CHUNK
    gw = GROUP_W
    tick = (lambda: None) if side is None else (lambda: next(side, None))
    masks = {rv: _rwkv_masks(rv) for rv in sorted({p[7] for p in probs})}
    mk = [masks[p[7]] for p in probs]
    each = lambda f, *cols: [f(*args) for args in zip(*cols)]
    rep = lambda x: jnp.concatenate([x] * HEADS_PER_GROUP, axis=0)
    nat = lambda x: x[0:L] + x[L:2 * L] + x[2 * L:3 * L] + x[3 * L:4 * L]

    c = [_dot_exact_lhs(m["tri"], p[3]) for m, p in zip(mk, probs)]
    c_l = [ci[(0 if p[7] else L - 1):(1 if p[7] else L), :] for ci, p in zip(c, probs)]

    an, rn, vn, v_bd, bk_end, at_bd, bt2, btk_bd, e_l = [], [], [], [], [], [], [], [], []
    for (r, v, kk, lw, kd, b, _, _), m, ci, cl in zip(probs, mk, c, c_l):
        bd = lambda x, sh=m["same_head"]: _bf(jnp.where(sh, rep(x), 0.0))
        e_nc = jnp.exp(-ci)
        e_end = jnp.exp(cl - ci)
        a_s = -kk * jnp.exp(ci - lw)
        b_s = b * e_nc
        an.append(_bf(a_s))
        rn.append(_bf(r * jnp.exp(ci)))
        at_bd.append(bd(a_s))
        bt2.append(jnp.concatenate([_bf(b_s)] * 2, axis=0))
        btk_bd.append(jnp.concatenate([bd(b_s), bd(kd * e_nc)], axis=0))
        vn.append(_bf(v))
        v_bd.append(bd(v))
        bk_end.append(jnp.concatenate([_bf(b * e_end), _bf(kd * e_end)], axis=0))
        e_l.append(jnp.exp(cl))

    t, apb, a_ak, rbk = [], [], [], []
    for m, ab, b2, a_n, r_n, bk in zip(mk, at_bd, bt2, an, rn, btk_bd):
        s = _dot_nt(ab, b2)
        a_ab = jnp.where(m["strict"], jnp.concatenate([s, s], axis=1), 0.0)
        t.append(jnp.where(m["eye"], 1.0, a_ab))
        apb.append(_bf(a_ab))
        sn = _dot_nt(jnp.concatenate([a_n, r_n], axis=0), bk)
        a_ak.append(_bf(jnp.where(m["strict_n"], sn[:L, gw:], 0.0)))
        rbk.append(_bf(jnp.where(m["incl_n2"], sn[L:, :], 0.0)))
    tick()
    akv = each(_dot, a_ak, v_bd)
    tb = [_bf(ti) for ti in t]
    for _ in range(5):
        apb = [_bf(_dot(a, a)) for a in apb]
        tick()
        t = [ti + _dot(tbi, a) for ti, tbi, a in zip(t, tb, apb)]
        tb = [_bf(ti) for ti in t]
    h_bf = [_bf(p[6]) for p in probs]
    x_bd = [_bf(jnp.where(m["same_head"], rep(_dot(a_n, h) + ak), 0.0))
            for m, a_n, h, ak in zip(mk, an, h_bf, akv)]
    tick()
    u = [_dot(tbi, xi) for tbi, xi in zip(tb, x_bd)]
    tick()
    y = [_dot(r_n, h) + _dot(rbk_i, jnp.concatenate([_bf(ui), v], axis=0))
         for r_n, h, rbk_i, ui, v in zip(rn, h_bf, rbk, u, v_bd)]
    e_col = [jnp.broadcast_to(el, (gw, gw)).T for el in e_l]
    h_new = []
    for m, ec, p, bke, ui, v_n in zip(mk, e_col, probs, bk_end, u, vn):
        upd = _dot_tn(bke, jnp.concatenate([_bf(nat(ui)), v_n], axis=0))
        h_new.append(ec * p[6] + jnp.where(m["same_head"], upd, 0.0))
    if side is not None:
        for _ in side:
            pass
    return list(zip(y, h_new))


def _chunk_row_block(b, i, *, reverse, n_lat, seq, ctx_len):
    nc = ctx_len // CHUNK
    nl = seq // CHUNK
    ci = (nc - 1 - i) if reverse else i
    li = (nl - 1 - (i - nc)) if reverse else (i - nc)
    return jnp.where(i < nc, (n_lat + b * ctx_len) // CHUNK + ci, b * nl + li)


def _gla_stages(qkf, vf, gdf, qkb, vb, gdb, gw2_ref, gb_ref, of_ref, ob_ref, sf_ref, sb_ref):
    dv, dk = sf_ref.shape[1:]
    L = CHUNK
    wk = B_HEADS * dk
    t_r = lax.broadcasted_iota(jnp.int32, (L, L), 0)
    t_c = lax.broadcasted_iota(jnp.int32, (L, L), 1)

    qh, kh, keh, vh, dech, keeph, sth = [], [], [], [], [], [], []
    for d, (qk_ref, v_ref, gd_ref, st_ref) in enumerate(((qkf, vf, gdf, sf_ref),
                                                         (qkb, vb, gdb, sb_ref))):
        reverse = d == 1
        keep = (t_c >= t_r) if reverse else (t_c <= t_r)
        tri = _bf(jnp.where(keep, 1.0, 0.0))
        z = _dot(_bf(gd_ref[...]), gw2_ref[d]) + gb_ref[d:d + 1, :]
        la = -_softplus(-z) * (1.0 / B_GATE_TAU)
        cum = _dot_exact_lhs(tri, la)
        last = 0 if reverse else L - 1
        cl = cum[last:last + 1, :]
        q_in = _bf(qk_ref[:, :wk] * (dk ** -0.5) * jnp.exp(cum))
        k_in = _bf(qk_ref[:, wk:] * jnp.exp(-cum))
        k_end = _bf(qk_ref[:, wk:] * jnp.exp(cl - cum))
        dec = jnp.exp(cl)
        for h in range(B_HEADS):
            ks = slice(h * dk, (h + 1) * dk)
            qh.append(q_in[:, ks])
            kh.append(k_in[:, ks])
            keh.append(k_end[:, ks])
            vh.append(_bf(v_ref[:, h * dv:(h + 1) * dv]))
            dech.append(dec[:, ks])
            keeph.append(keep)
            sth.append(st_ref[h])
        yield

    sc = [_bf(jnp.where(kp, _dot_nt(q, k), 0.0)) for kp, q, k in zip(keeph, qh, kh)]
    yield
    inter = [_dot_nt(q, _bf(st)) for q, st in zip(qh, sth)]
    yield
    kvt = [_dot_tn(v, ke) for v, ke in zip(vh, keh)]
    yield
    out = [_dot(s_, v) + it for s_, v, it in zip(sc, vh, inter)]
    yield
    for d, (o_ref, st_ref) in enumerate(((of_ref, sf_ref), (ob_ref, sb_ref))):
        for h in range(B_HEADS):
            i = d * B_HEADS + h
            o_ref[:, h * dv:(h + 1) * dv] = out[i]
            st_ref[h] = sth[i] * dech[i] + kvt[i]


def _mix_scan_kernel(rf, vf, kkf, lwf, kdf, bf_, rb, vb, kkb, lwb, kdb, bb,
                     qkf, gvf, gdf, qkb, gvb, gdb, gw2_ref, gb_ref,
                     yf_ref, yb_ref, of_ref, ob_ref, hf_ref, hb_ref, sf_ref, sb_ref):
    @pl.when(pl.program_id(1) == 0)
    def _():
        for ref in (hf_ref, hb_ref, sf_ref, sb_ref):
            ref[...] = jnp.zeros_like(ref)

    gs = hf_ref.shape[0]
    probs = []
    for gi in range(gs):
        sl = slice(gi * GROUP_W, (gi + 1) * GROUP_W)
        f32 = lambda ref: ref[:, sl].astype(F32)
        probs.append((f32(rf), f32(vf), f32(kkf), lwf[:, sl], f32(kdf), f32(bf_), hf_ref[gi], False))
        probs.append((f32(rb), f32(vb), f32(kkb), lwb[:, sl], f32(kdb), f32(bb), hb_ref[gi], True))
    gla = _gla_stages(qkf, gvf, gdf, qkb, gvb, gdb, gw2_ref, gb_ref, of_ref, ob_ref, sf_ref,
                      sb_ref)
    res = _rwkv_chunks(probs, side=gla)
    for gi in range(gs):
        sl = slice(gi * GROUP_W, (gi + 1) * GROUP_W)
        yf_ref[:, sl], hf_ref[gi] = res[2 * gi]
        yb_ref[:, sl], hb_ref[gi] = res[2 * gi + 1]


def _mix_scan(r, v, kk, lw0, lw1, kd0, kd1, b0, b1, pb, gw2p, gb, *, batch, n_lat, seq, ctx_len,
              wk, wv, col0):
    rows, aw = r.shape
    assert aw == GROUP_W * RWKV_GROUPS_PER_STEP
    dk, dv = wk // B_HEADS, wv // B_HEADS
    geo = dict(n_lat=n_lat, seq=seq, ctx_len=ctx_len)
    gd_blk = (col0 + 2 * wk + 2 * wv) // LANE

    def specs(reverse):
        rb = lambda b, i: _chunk_row_block(b, i, reverse=reverse, **geo)
        rw = pl.BlockSpec((CHUNK, aw), lambda b, i: (rb(b, i), 0))
        gla = [
            pl.BlockSpec((CHUNK, 2 * wk), lambda b, i: (rb(b, i), col0 // (2 * wk))),
            pl.BlockSpec((CHUNK, wv), lambda b, i: (rb(b, i), (col0 + 2 * wk) // wv)),
            pl.BlockSpec((CHUNK, LANE), lambda b, i: (rb(b, i), gd_blk)),
        ]
        out = pl.BlockSpec((CHUNK, wv), lambda b, i: (rb(b, i), 0))
        return rw, gla, out

    rwf, glaf, outf = specs(False)
    rwb, glab, outb = specs(True)
    sds_a = jax.ShapeDtypeStruct((rows, aw), F32)
    sds_b = jax.ShapeDtypeStruct((rows, wv), F32)
    gs = RWKV_GROUPS_PER_STEP
    return pl.pallas_call(
        _mix_scan_kernel,
        grid=(batch, (seq + ctx_len) // CHUNK),
        in_specs=[rwf] * 6 + [rwb] * 6 + glaf + glab + [
            pl.BlockSpec((2, LANE, wk), lambda b, i: (0, 0, 0)),
            pl.BlockSpec((2, wk), lambda b, i: (0, 0)),
        ],
        out_specs=[rwf, rwb, outf, outb],
        out_shape=[sds_a, sds_a, sds_b, sds_b],
        scratch_shapes=[pltpu.VMEM((gs, GROUP_W, GROUP_W), F32),
                        pltpu.VMEM((gs, GROUP_W, GROUP_W), F32),
                        pltpu.VMEM((B_HEADS, dv, dk), F32), pltpu.VMEM((B_HEADS, dv, dk), F32)],
        compiler_params=_cparams(("parallel", "arbitrary")),
    )(r, v, kk, lw0, kd0, b0, r, v, kk, lw1, kd1, b1, pb, pb, pb, pb, pb, pb, gw2p, gb)


def _mixer_out_kernel(yf, yb, bonus, g, of, ob, gate, xa_ref, xb_ref, mod_ref, lnw, lnb, gn, e_ref,
                      w_ref, o_ref, h_ref, *, aw, dv, n_a_tiles):
    e_bf = e_ref[...]
    y = yf[...] + yb[...]
    inv = 1.0 / A_HEAD_DIM
    mean = _head_sum(y, e_bf) * inv
    dlt = y - mean
    var = _head_sum(dlt * dlt, e_bf) * inv
    yn = dlt * lax.rsqrt(var + A_LN_EPS) * lnw[...] + lnb[...]
    h_ref[:, :aw] = _bf((yn + bonus[...]) * g[...])

    o = of[...] + ob[...]
    gt = _silu(gate[...])
    for h in range(B_HEADS):
        sl = slice(h * dv, (h + 1) * dv)
        oh = o[:, sl]
        ms = jnp.mean(oh * oh, axis=-1, keepdims=True)
        h_ref[:, aw + h * dv:aw + (h + 1) * dv] = _bf(oh * lax.rsqrt(ms + NORM_EPS) * gn[...] * gt[:, sl])

    x = jnp.where(pl.program_id(0) < n_a_tiles, xa_ref[...], xb_ref[...])
    o_ref[...] = x + mod_ref[2:3, :] * _dot(h_ref[...], w_ref[...])


def _mixer_out(yf, yb, bonus, g, of, ob, pb, xa, xb, mods_l, ln_w, ln_b, gla_norm, e_bf, w_bf,
               seg_of_tile, *, wk, col0):
    d = xa.shape[1]
    rows = xa.shape[0] + xb.shape[0]
    aw = yf.shape[1]
    wv = of.shape[1]
    dv = wv // B_HEADS
    tm = 256
    tiles_per_row_tile = ROW_TILE // tm
    n_a_tiles = xa.shape[0] // tm
    rs = lambda w: pl.BlockSpec((tm, w), lambda i: (i, 0))
    full = lambda shape: pl.BlockSpec(shape, lambda i: (0,) * len(shape))
    return pl.pallas_call(
        functools.partial(_mixer_out_kernel, aw=aw, dv=dv, n_a_tiles=n_a_tiles),
        grid=(rows // tm,),
        in_specs=[
            rs(aw), rs(aw), rs(aw), rs(aw), rs(wv), rs(wv),
            pl.BlockSpec((tm, wv), lambda i: (i, (col0 + 2 * wk + wv) // wv)),
        ] + _two_source_specs(tm, d, n_a_tiles) + [
            pl.BlockSpec((None, N_MOD, d), lambda i: (seg_of_tile(i // tiles_per_row_tile), 0, 0)),
            full((1, aw)), full((1, aw)), full((1, dv)), full((GROUP_W, GROUP_W)),
            full((aw + wv, d)),
        ],
        out_specs=rs(d),
        out_shape=jax.ShapeDtypeStruct((rows, d), F32),
        scratch_shapes=[pltpu.VMEM((tm, aw + wv), BF16)],
        compiler_params=_cparams(("parallel",)),
    )(yf, yb, bonus, g, of, ob, pb, xa, xb, mods_l, ln_w.reshape(1, aw), ln_b.reshape(1, aw),
      gla_norm.reshape(1, dv), e_bf, w_bf)


def _lru_kernel(*refs, reverse, n_blk, combine, cols):
    if combine:
        (x_ref, xp_ref, xn_ref, h0_ref, cw_ref, cb_ref, wc_ref, ba_ref, bx_ref, lam_ref,
         hf_ref, gate_ref, out_ref, hl_ref, a_sc, u_sc, hs_sc, carry_sc) = refs
    else:
        (x_ref, xp_ref, xn_ref, h0_ref, cw_ref, cb_ref, wc_ref, ba_ref, bx_ref, lam_ref,
         out_ref, hl_ref, a_sc, u_sc, hs_sc, carry_sc) = refs
    j = pl.program_id(2)
    blk = (n_blk - 1 - j) if reverse else j
    lb = x_ref.shape[0]
    ct = cw_ref.shape[1]
    cblk = wc_ref.shape[1]
    rowid = lax.broadcasted_iota(jnp.int32, (lb, 1), 0)
    decay_rate = -C_CONST * _softplus(-lam_ref[...])

    @pl.when(j == 0)
    def _():
        carry_sc[...] = h0_ref[0:1, :]

    for q in (range(cols - 1, -1, -1) if reverse else range(cols)):
        lanes = slice(q * ct, (q + 1) * ct)
        x = x_ref[:, lanes]
        if q > 0:
            prev = slice((q - 1) * ct, q * ct)
            p6, p7 = x_ref[lb - 2:lb - 1, prev], x_ref[lb - 1:lb, prev]
        else:
            p6 = jnp.where(blk == 0, 0.0, xp_ref[SUBLANE - 2:SUBLANE - 1, :])
            p7 = jnp.where(blk == 0, 0.0, xp_ref[SUBLANE - 1:SUBLANE, :])
        if q < cols - 1:
            n0 = x_ref[0:1, (q + 1) * ct:(q + 2) * ct]
        else:
            n0 = jnp.where(blk == n_blk - 1, 0.0, xn_ref[0:1, :])
        xm1 = jnp.where(rowid == 0, p7, pltpu.roll(x, 1, 0))
        xm2 = jnp.where(rowid == 0, p6, jnp.where(rowid == 1, p7, pltpu.roll(x, 2, 0)))
        xp1 = jnp.where(rowid == lb - 1, n0, pltpu.roll(x, lb - 1, 0))
        xs = (cb_ref[...] + cw_ref[0:1, :] * xm2 + cw_ref[1:2, :] * xm1 + cw_ref[2:3, :] * x
              + cw_ref[3:4, :] * xp1)

        for n in range(ct // cblk):
            sl = slice(n * cblk, (n + 1) * cblk)
            xb = xs[:, sl]
            ri = _dot(_bf(xb), wc_ref[n])
            rg = _sigmoid(ri[:, :cblk] + ba_ref[:, sl])
            ig = _sigmoid(ri[:, cblk:] + bx_ref[:, sl])
            a = jnp.exp(decay_rate[:, sl] * rg)
            a_sc[q, :, sl] = a
            u_sc[q, :, sl] = jnp.sqrt(1.0 - a * a) * (ig * xb)

        def step(t, h, q=q):
            tt = (lb - 1 - t) if reverse else t
            h = a_sc[q, pl.ds(tt, 1), :] * h + u_sc[q, pl.ds(tt, 1), :]
            hs_sc[q, pl.ds(tt, 1), :] = h
            return h

        carry_sc[...] = lax.fori_loop(0, lb, step, carry_sc[...], unroll=8)
        if combine:
            out_ref[:, lanes] = ((hs_sc[q] + hf_ref[q * lb:(q + 1) * lb, :])
                                 * jax.nn.gelu(gate_ref[:, lanes]))
        else:
            out_ref[q * lb:(q + 1) * lb, :] = hs_sc[q]
    hl_ref[...] = jnp.broadcast_to(carry_sc[...], hl_ref.shape)


def _lru_pass(x_arr, x_specs, h0, conv_w, conv_b, wcat, ba, bx, lam, out_shape, out_spec, extra,
              *, batch, n_blk, lb, cols, reverse):
    c = conv_w.shape[1]
    ct = LRU_CHANNEL_TILE
    nct = c // ct
    cblk = wcat.shape[1]
    chan = lambda rows_: pl.BlockSpec((rows_, ct), lambda b, k, j: (0, k))
    in_specs = list(x_specs) + [
        pl.BlockSpec((SUBLANE, ct), lambda b, k, j: (b, k)),
        chan(C_CONV), chan(1),
        pl.BlockSpec((ct // cblk, cblk, 2 * cblk), lambda b, k, j: (k, 0, 0)),
        chan(1), chan(1), chan(1),
    ] + [s for _, s in extra]
    args = [x_arr, x_arr, x_arr, h0, conv_w, conv_b.reshape(1, c), wcat, ba.reshape(1, c),
            bx.reshape(1, c), lam.reshape(1, c)] + [a for a, _ in extra]
    return pl.pallas_call(
        functools.partial(_lru_kernel, reverse=reverse, n_blk=n_blk, combine=bool(extra),
                          cols=cols),
        grid=(batch, nct, n_blk),
        in_specs=in_specs,
        out_specs=[out_spec, pl.BlockSpec((SUBLANE, ct), lambda b, k, j: (b, k))],
        out_shape=[out_shape, jax.ShapeDtypeStruct((batch * SUBLANE, c), F32)],
        scratch_shapes=[pltpu.VMEM((cols, lb, ct), F32), pltpu.VMEM((cols, lb, ct), F32),
                        pltpu.VMEM((cols, lb, ct), F32), pltpu.VMEM((1, ct), F32)],
        compiler_params=_cparams(("parallel", "parallel", "arbitrary")),
    )(*args)


def _rglru(hw_ctx, gate_grid, x_grid, conv_w, conv_b, wa, ba, wx, bx, lam, *, batch, n_lat, seq,
           ctx_len):
    c = conv_w.shape[1]
    ct = LRU_CHANNEL_TILE
    rows = seq // GRID_W
    lbc = 128
    ncb = ctx_len // lbc
    xcol = c // ct

    wcat = [_bf(jnp.concatenate([wa[d], wx[d]], axis=-1)) for d in range(2)]
    zeros_state = jnp.zeros((batch * SUBLANE, c), F32)

    def ctx_specs(reverse):
        blk = (lambda j: ncb - 1 - j) if reverse else (lambda j: j)
        per8 = lbc // SUBLANE
        cur = pl.BlockSpec((lbc, ct), lambda b, k, j: (b * ncb + blk(j), xcol + k))
        prv = pl.BlockSpec((SUBLANE, ct), lambda b, k, j: (
            b * ncb * per8 + jnp.maximum(blk(j) * per8 - 1, 0), xcol + k))
        nxt = pl.BlockSpec((SUBLANE, ct), lambda b, k, j: (
            b * ncb * per8 + jnp.minimum((blk(j) + 1) * per8, ncb * per8 - 1), xcol + k))
        return cur, prv, nxt

    assert ct == c
    kc = LRU_COLS_PER_STEP
    ncg = GRID_W // kc

    def lat_specs(reverse):
        grp = (lambda j: ncg - 1 - j) if reverse else (lambda j: j)
        r8 = rows // SUBLANE
        cur = pl.BlockSpec((rows, kc * ct), lambda b, k, j: (b, grp(j)))
        prv = pl.BlockSpec((SUBLANE, ct), lambda b, k, j: (
            b * r8 + r8 - 1, jnp.maximum(grp(j) * kc - 1, 0)))
        nxt = pl.BlockSpec((SUBLANE, ct), lambda b, k, j: (
            b * r8, jnp.minimum(grp(j) * kc + kc, GRID_W - 1)))
        hf = pl.BlockSpec((kc * rows, ct), lambda b, k, j: (b * ncg + grp(j), 0))
        return (cur, prv, nxt), hf, cur

    common = dict(batch=batch)
    ctx_scratch_sds = jax.ShapeDtypeStruct((batch * ctx_len, c), F32)
    ctx_out = lambda reverse: pl.BlockSpec(
        (lbc, ct), lambda b, k, j: (b * ncb + ((ncb - 1 - j) if reverse else j), k))
    _, st = _lru_pass(hw_ctx, ctx_specs(False), zeros_state, conv_w, conv_b, wcat[0], ba[0], bx[0],
                      lam[0], ctx_scratch_sds, ctx_out(False), [], n_blk=ncb, lb=lbc, cols=1,
                      reverse=False, **common)
    hf_sds = jax.ShapeDtypeStruct((n_lat, c), F32)
    x_specs, hf_spec, _ = lat_specs(False)
    hf, _ = _lru_pass(x_grid, x_specs, st, conv_w, conv_b, wcat[0], ba[0], bx[0], lam[0],
                      hf_sds, hf_spec, [], n_blk=ncg, lb=rows, cols=kc, reverse=False, **common)
    _, st = _lru_pass(hw_ctx, ctx_specs(True), zeros_state, conv_w, conv_b, wcat[1], ba[1], bx[1],
                      lam[1], ctx_scratch_sds, ctx_out(True), [], n_blk=ncb, lb=lbc, cols=1,
                      reverse=True, **common)
    x_specs, hf_spec, grid_spec = lat_specs(True)
    y_sds = jax.ShapeDtypeStruct((n_lat // GRID_W, GRID_W * c), F32)
    y, _ = _lru_pass(x_grid, x_specs, st, conv_w, conv_b, wcat[1], ba[1], bx[1], lam[1],
                     y_sds, grid_spec, [(hf, hf_spec), (gate_grid, grid_spec)],
                     n_blk=ncg, lb=rows, cols=kc, reverse=True, **common)
    return y


def _pad_cols(w, n):
    return jnp.pad(w, ((0, 0),) * (w.ndim - 1) + ((0, n - w.shape[-1]),))


def kernel(x, c, ctx, c_ctx, mod_w, mod_b, norm1, norm2, mlp_w1, mlp_w2, ab_w_in, ab_w_out, rw_mu,
           rw_w0, rw_w2, rw_a0, rw_a2, rw_g2, rw_kk, rw_ka, rw_rk, rw_ln_w, rw_ln_b, gla_gw2,
           gla_gb, gla_norm, lru_w_in, lru_w_out, lru_conv_w, lru_conv_b, lru_wa, lru_ba, lru_wx,
           lru_bx, lru_lam, final_norm):
    batch, seq, d = x.shape
    ctx_len = ctx.shape[1]
    depth = mod_w.shape[0]
    n_lat = batch * seq
    aw = rw_w0.shape[-1]
    wk = gla_gb.shape[-1]
    wv = ab_w_out.shape[1] - aw
    a_cols = rw_mu.shape[-1]
    b_cols = ab_w_in.shape[-1] - a_cols
    assert depth == 2 and batch + 1 <= SUBLANE
    assert seq % ROW_TILE == 0 and (batch * ctx_len) % ROW_TILE == 0
    assert ctx_len % PREP_TILE == 0 and ctx_len % 128 == 0 and seq % (GRID_W * SUBLANE) == 0
    assert aw % GROUP_W == 0 and wv == aw and 2 * wk == wv

    tiles_per_seq = seq // ROW_TILE
    seg_of_tile = lambda i: jnp.minimum(i // tiles_per_seq, batch)

    x2 = x.reshape(n_lat, d)
    ctx2 = ctx.reshape(batch * ctx_len, d)
    cvec = jnp.concatenate(
        [c, c_ctx[None, :], jnp.zeros((SUBLANE - batch - 1, d), F32)], axis=0)
    mods = _mods(cvec, mod_w, mod_b)

    hid = jnp.arange(GROUP_W) // A_HEAD_DIM
    e_bf = _bf(hid[:, None] == hid[None, :])

    a_pad = -(-a_cols // LANE) * LANE
    col0 = -(-a_cols // (2 * wk)) * (2 * wk)
    n_tiles = 3
    tn = -(-(col0 + b_cols) // (n_tiles * LANE)) * LANE
    w_ab = _bf(jnp.concatenate(
        [_pad_cols(ab_w_in[0][:, :a_cols], col0),
         _pad_cols(ab_w_in[0][:, a_cols:], n_tiles * tn - col0)], axis=1))
    pa = pb = _norm_matmul(x2, ctx2, norm1[0], mods[0], w_ab, seg_of_tile, tn=tn)

    lora_w = 2 * A_LORA_W + 2 * A_LORA_A
    w2p = jnp.zeros((2, lora_w, aw), F32)
    a2p = jnp.zeros((2, lora_w, aw), F32)
    for dd in range(2):
        w2p = w2p.at[dd, dd * A_LORA_W:(dd + 1) * A_LORA_W].set(rw_w2[0, dd])
        a2p = a2p.at[dd, 2 * A_LORA_W + dd * A_LORA_A:2 * A_LORA_W + (dd + 1) * A_LORA_A].set(
            rw_a2[0, dd])
    (r_, v_, kk_, lw0, lw1, kd0, kd1, b0, b1, g_, bonus) = _rwkv_prep(
        pa, _pad_cols(rw_mu[0], a_pad), rw_w0[0], rw_a0[0], _bf(w2p), _bf(a2p), _bf(rw_g2[0]),
        rw_kk[0], rw_ka[0], rw_rk[0].reshape(-1), e_bf, n_lat=n_lat, seq=seq, ctx_len=ctx_len)
    gw2p = jnp.zeros((2, LANE, wk), F32)
    for dd in range(2):
        gw2p = gw2p.at[dd, dd * B_LORA:(dd + 1) * B_LORA].set(gla_gw2[0, dd])
    yf, yb, of, ob = _mix_scan(r_, v_, kk_, lw0, lw1, kd0, kd1, b0, b1, pb, _bf(gw2p), gla_gb[0],
                               batch=batch, n_lat=n_lat, seq=seq, ctx_len=ctx_len, wk=wk, wv=wv,
                               col0=col0)

    xs = _mixer_out(yf, yb, bonus, g_, of, ob, pb, x2, ctx2, mods[0], rw_ln_w[0], rw_ln_b[0],
                    gla_norm[0], e_bf, _bf(ab_w_out[0]), seg_of_tile, wk=wk, col0=col0)
    seg_of_mlp_tile = lambda i: jnp.minimum(i // (seq // MLP_TILE), batch)
    ctx_tiles = (batch * ctx_len) // ROW_TILE
    w1_bf, w2_bf = _bf(mlp_w1[0]), _bf(mlp_w2[0])
    x_lat = _mlp(xs, norm2[0], mods[0], w1_bf, w2_bf, final_norm, seg_of_mlp_tile, tm=MLP_TILE,
                 tile0=0, n_tiles=n_lat // MLP_TILE, final_norm=False)
    x_ctx = _mlp(xs, norm2[0], mods[0], w1_bf, w2_bf, final_norm, seg_of_tile, tm=ROW_TILE,
                 tile0=n_lat // ROW_TILE, n_tiles=ctx_tiles, final_norm=False)

    w_lru = _bf(lru_w_in[0])
    hw_ctx = _norm_matmul(x_ctx, None, norm1[1], mods[1], w_lru, lambda i: batch, tn=2048)
    gate_grid, x_grid = _norm_matmul_grid(x_lat, n_lat, seq // GRID_W, norm1[1], mods[1], w_lru)
    y = _rglru(hw_ctx, gate_grid, x_grid, lru_conv_w[0], lru_conv_b[0], lru_wa[0], lru_ba[0],
               lru_wx[0], lru_bx[0], lru_lam[0], batch=batch, n_lat=n_lat, seq=seq,
               ctx_len=ctx_len)
    xl = _proj_res(y, x_lat, n_lat, seq // GRID_W, mods[1], _bf(lru_w_out[0]))
    out = _mlp(xl, norm2[1], mods[1], _bf(mlp_w1[1]), _bf(mlp_w2[1]), final_norm, seg_of_mlp_tile,
               tm=MLP_TILE, tile0=0, n_tiles=n_lat // MLP_TILE, final_norm=True)
    return out.reshape(batch, seq, d)
```

```python
import functools

import jax
import jax.numpy as jnp
from jax import lax
from jax.experimental import pallas as pl
from jax.experimental.pallas import tpu as pltpu

F32 = jnp.float32
BF16 = jnp.bfloat16

NORM_EPS = 1e-6
GRID_W = 64
N_MOD = 6

A_HEAD_DIM = 64
A_LORA_W = 96
A_LORA_A = 96
A_LORA_G = 256
A_LN_EPS = 64e-5
HEADS_PER_GROUP = 4
GROUP_W = HEADS_PER_GROUP * A_HEAD_DIM
CHUNK = 64
RWKV_GROUPS_PER_STEP = 4

B_HEADS = 4
B_LORA = 16
B_GATE_TAU = 16.0

C_BLOCKS = 8
C_CONV = 4
C_CONST = 8.0

LANE = 128
SUBLANE = 8
ROW_TILE = 512
MLP_TILE = 1024
MLP_STEP_ELEMS = 1024 * 512
LRU_CHANNEL_TILE = 2048
LRU_COLS_PER_STEP = 4
PREP_TILE = 128
VMEM_LIMIT = 56 * 1024 * 1024


def _cparams(sem):
    return pltpu.CompilerParams(dimension_semantics=sem, vmem_limit_bytes=VMEM_LIMIT)


def _bf(x):
    return x.astype(BF16)


def _dot(a, b):
    return jnp.dot(a, b, preferred_element_type=F32)


def _dot_nt(a, b):
    return lax.dot_general(a, b, (((1,), (1,)), ((), ())), preferred_element_type=F32)


def _dot_tn(a, b):
    return lax.dot_general(a, b, (((0,), (0,)), ((), ())), preferred_element_type=F32)


def _split3(x):
    hi = _bf(x)
    r1 = x - hi.astype(F32)
    mid = _bf(r1)
    lo = _bf(r1 - mid.astype(F32))
    return hi, mid, lo


def _dot_exact_lhs(m_bf, x):
    hi, mid, lo = _split3(x)
    return _dot(m_bf, hi) + _dot(m_bf, mid) + _dot(m_bf, lo)


def _dot_exact_rhs(x, m_bf):
    hi, mid, lo = _split3(x)
    return _dot(hi, m_bf) + _dot(mid, m_bf) + _dot(lo, m_bf)


def _softplus(x):
    return jnp.maximum(x, 0.0) + jnp.log(1.0 + jnp.exp(-jnp.abs(x)))


def _sigmoid(x):
    return 0.5 * jnp.tanh(0.5 * x) + 0.5


def _silu(x):
    return x * _sigmoid(x)


def _norm_mod(x, g, shift, scale):
    ms = jnp.mean(x * x, axis=-1, keepdims=True)
    return (x * lax.rsqrt(ms + NORM_EPS) * g) * (1.0 + scale) + shift


def _mods_kernel(c_ref, w_ref, b_ref, o_ref):
    s = _silu(c_ref[...])
    o_ref[...] = _dot(_bf(s), _bf(w_ref[...])) + b_ref[...]


def _mods(cvec, mod_w, mod_b):
    depth, d, n = mod_w.shape
    tn = 1024
    out = pl.pallas_call(
        _mods_kernel,
        grid=(depth, n // tn),
        in_specs=[
            pl.BlockSpec((SUBLANE, d), lambda l, j: (0, 0)),
            pl.BlockSpec((None, d, tn), lambda l, j: (l, 0, j)),
            pl.BlockSpec((None, 1, tn), lambda l, j: (l, 0, j)),
        ],
        out_specs=pl.BlockSpec((None, SUBLANE, tn), lambda l, j: (l, 0, j)),
        out_shape=jax.ShapeDtypeStruct((depth, SUBLANE, n), F32),
        compiler_params=_cparams(("parallel", "parallel")),
    )(cvec, mod_w, mod_b.reshape(depth, 1, n))
    return out.reshape(depth, SUBLANE, N_MOD, d)


def _norm_matmul_kernel(xa_ref, xb_ref, g_ref, mod_ref, w_ref, o_ref, h_ref, *, n_a_tiles):
    i = pl.program_id(0)

    @pl.when(pl.program_id(1) == 0)
    def _():
        x = jnp.where(i < n_a_tiles, xa_ref[...], xb_ref[...])
        h_ref[...] = _bf(_norm_mod(x, g_ref[...], mod_ref[0:1, :], mod_ref[1:2, :]))

    o_ref[...] = _dot(h_ref[...], w_ref[...])


def _two_source_specs(tm, d, n_a_tiles):
    return [pl.BlockSpec((tm, d), lambda i, *_: (jnp.minimum(i, n_a_tiles - 1), 0)),
            pl.BlockSpec((tm, d), lambda i, *_: (jnp.maximum(i - n_a_tiles, 0), 0))]


def _norm_matmul(xa, xb, g, mods_l, w_bf, seg_of_tile, *, tn):
    d = xa.shape[1]
    n = w_bf.shape[1]
    tm = ROW_TILE
    n_a_tiles = xa.shape[0] // tm
    n_tiles = n_a_tiles + (0 if xb is None else xb.shape[0] // tm)
    return pl.pallas_call(
        functools.partial(_norm_matmul_kernel, n_a_tiles=n_a_tiles),
        grid=(n_tiles, n // tn),
        in_specs=_two_source_specs(tm, d, n_a_tiles) + [
            pl.BlockSpec((1, d), lambda i, j: (0, 0)),
            pl.BlockSpec((None, N_MOD, d), lambda i, j: (seg_of_tile(i), 0, 0)),
            pl.BlockSpec((d, tn), lambda i, j: (0, j)),
        ],
        out_specs=pl.BlockSpec((tm, tn), lambda i, j: (i, j)),
        out_shape=jax.ShapeDtypeStruct((n_tiles * tm, n), F32),
        scratch_shapes=[pltpu.VMEM((tm, d), BF16)],
        compiler_params=_cparams(("parallel", "arbitrary")),
    )(xa, xa if xb is None else xb, g.reshape(1, d), mods_l, w_bf)


def _grid_tile(rows):
    gr = min(rows, 64)
    gc = ROW_TILE // gr
    assert gc % SUBLANE == 0 and GRID_W % gc == 0 and rows % gr == 0
    return gr, gc


def _norm_matmul_grid_kernel(x_ref, g_ref, mod_ref, w_ref, o0_ref, o1_ref, h_ref, res_ref):
    j = pl.program_id(1)
    gr, gc, d = x_ref.shape

    @pl.when(j == 0)
    def _():
        h = _norm_mod(x_ref[...].reshape(gr * gc, d), g_ref[...], mod_ref[0:1, :], mod_ref[1:2, :])
        h_ref[...] = _bf(h)

    res = _dot(h_ref[...], w_ref[...])
    nh = w_ref.shape[1]
    nq = nh // LANE
    for q in range(nq):
        res_ref[q] = res[:, q * LANE:(q + 1) * LANE]

    def scatter(o_ref):
        for c in range(gc):
            for q in range(nq):
                o_ref[:, c * nh + q * LANE:c * nh + (q + 1) * LANE] = (
                    res_ref[q, pl.ds(c, gr, stride=gc), :])

    @pl.when(j == 0)
    def _():
        scatter(o0_ref)

    @pl.when(j == 1)
    def _():
        scatter(o1_ref)


def _norm_matmul_grid(x, n_lat, rows, g, mods_l, w_bf):
    d = x.shape[1]
    nh = w_bf.shape[1] // 2
    gr, gc = _grid_tile(rows)
    ncg = GRID_W // gc
    x3 = x.reshape(x.shape[0] // GRID_W, GRID_W, d)
    out_spec = pl.BlockSpec((gr, gc * nh), lambda i, j: (i // ncg, i % ncg))
    sds = jax.ShapeDtypeStruct((n_lat // GRID_W, GRID_W * nh), F32)
    return pl.pallas_call(
        _norm_matmul_grid_kernel,
        grid=(n_lat // ROW_TILE, 2),
        in_specs=[
            pl.BlockSpec((gr, gc, d), lambda i, j: (i // ncg, i % ncg, 0)),
            pl.BlockSpec((1, d), lambda i, j: (0, 0)),
            pl.BlockSpec((None, N_MOD, d), lambda i, j: ((i // ncg) * gr // rows, 0, 0)),
            pl.BlockSpec((d, nh), lambda i, j: (0, j)),
        ],
        out_specs=[out_spec, out_spec],
        out_shape=[sds, sds],
        scratch_shapes=[pltpu.VMEM((ROW_TILE, d), BF16),
                        pltpu.VMEM((nh // LANE, ROW_TILE, LANE), F32)],
        compiler_params=_cparams(("parallel", "arbitrary")),
    )(x3, g.reshape(1, d), mods_l, w_bf)


def _mlp_kernel(x_ref, g_ref, mod_ref, w1_ref, w2_ref, fn_ref, o_ref, h_ref, *, final_norm):
    j = pl.program_id(1)
    slab = 256
    n_slab = x_ref.shape[0] // slab

    @pl.when(j == 0)
    def _():
        for s in range(n_slab):
            rs = slice(s * slab, (s + 1) * slab)
            h_ref[rs, :] = _bf(_norm_mod(x_ref[rs, :], g_ref[...], mod_ref[3:4, :],
                                         mod_ref[4:5, :]))
        o_ref[...] = jnp.zeros_like(o_ref)

    a = _dot(h_ref[...], w1_ref[...])
    a = _bf(jnp.square(jnp.maximum(a, 0.0)))
    nsplit = 2
    wn = o_ref.shape[1] // nsplit
    for s in range(nsplit):
        o_ref[:, s * wn:(s + 1) * wn] += _dot(a, w2_ref[:, s * wn:(s + 1) * wn])

    @pl.when(j == pl.num_programs(1) - 1)
    def _():
        for s in range(n_slab):
            rs = slice(s * slab, (s + 1) * slab)
            xo = x_ref[rs, :] + mod_ref[5:6, :] * o_ref[rs, :]
            if final_norm:
                ms = jnp.mean(xo * xo, axis=-1, keepdims=True)
                xo = xo * lax.rsqrt(ms + NORM_EPS) * fn_ref[...]
            o_ref[rs, :] = xo


def _mlp(x, g, mods_l, w1_bf, w2_bf, fn, seg_of_tile, *, tm, tile0, n_tiles, final_norm):
    d = x.shape[1]
    dff = w1_bf.shape[1]
    tf = MLP_STEP_ELEMS // tm
    return pl.pallas_call(
        functools.partial(_mlp_kernel, final_norm=final_norm),
        grid=(n_tiles, dff // tf),
        in_specs=[
            pl.BlockSpec((tm, d), lambda i, j: (i + tile0, 0), pipeline_mode=pl.Buffered(1)),
            pl.BlockSpec((1, d), lambda i, j: (0, 0)),
            pl.BlockSpec((None, N_MOD, d), lambda i, j: (seg_of_tile(i + tile0), 0, 0)),
            pl.BlockSpec((d, tf), lambda i, j: (0, j)),
            pl.BlockSpec((tf, d), lambda i, j: (j, 0)),
            pl.BlockSpec((1, d), lambda i, j: (0, 0)),
        ],
        out_specs=pl.BlockSpec((tm, d), lambda i, j: (i, 0)),
        out_shape=jax.ShapeDtypeStruct((n_tiles * tm, d), F32),
        scratch_shapes=[pltpu.VMEM((tm, d), BF16)],
        compiler_params=_cparams(("parallel", "arbitrary")),
    )(x, g.reshape(1, d), mods_l, w1_bf, w2_bf, fn.reshape(1, d))


def _proj_res_kernel(y_ref, x_ref, mod_ref, w_ref, o_ref, yt_ref):
    k = w_ref.shape[0]
    gr, gc, d = x_ref.shape
    nq = k // LANE
    for c in range(gc):
        for q in range(nq):
            yt_ref[q, pl.ds(c, gr, stride=gc), :] = (
                y_ref[:, c * k + q * LANE:c * k + (q + 1) * LANE])
    yt = jnp.concatenate([_bf(yt_ref[q]) for q in range(nq)], axis=1)
    out = x_ref[...].reshape(gr * gc, d) + mod_ref[2:3, :] * _dot(yt, w_ref[...])
    o_ref[...] = out.reshape(gr, gc, d)


def _proj_res(y_grid, x, n_lat, rows, mods_l, w_bf):
    k, d = w_bf.shape
    gr, gc = _grid_tile(rows)
    ncg = GRID_W // gc
    x3 = x.reshape(x.shape[0] // GRID_W, GRID_W, d)
    tok_spec = pl.BlockSpec((gr, gc, d), lambda i: (i // ncg, i % ncg, 0))
    out = pl.pallas_call(
        _proj_res_kernel,
        grid=(n_lat // ROW_TILE,),
        in_specs=[
            pl.BlockSpec((gr, gc * k), lambda i: (i // ncg, i % ncg)),
            tok_spec,
            pl.BlockSpec((None, N_MOD, d), lambda i: ((i // ncg) * gr // rows, 0, 0)),
            pl.BlockSpec((k, d), lambda i: (0, 0)),
        ],
        out_specs=tok_spec,
        out_shape=jax.ShapeDtypeStruct((n_lat // GRID_W, GRID_W, d), F32),
        scratch_shapes=[pltpu.VMEM((k // LANE, ROW_TILE, LANE), F32)],
        compiler_params=_cparams(("parallel",)),
    )(y_grid, x3, mods_l, w_bf)
    return out.reshape(n_lat, d)


def _head_sum(z, e_bf):
    parts = []
    for gi in range(z.shape[1] // GROUP_W):
        zs = z[:, gi * GROUP_W:(gi + 1) * GROUP_W]
        parts.append(_dot_exact_rhs(zs, e_bf))
    return jnp.concatenate(parts, axis=1)


def _rwkv_prep_kernel(p_ref, pp_ref, pn_ref, mu_ref, w0_ref, a0_ref, w2_ref, a2_ref, g2_ref,
                      kk_ref, ka_ref, rk_ref, e_ref,
                      r_o, v_o, kk_o, lw0_o, lw1_o, kd0_o, kd1_o, b0_o, b1_o, g_o, bonus_o,
                      *, n_lat_tiles, lat_tiles_per_seq, ctx_tiles_per_seq, aw):
    i = pl.program_id(0)
    tp = p_ref.shape[0]
    in_ctx = i >= n_lat_tiles
    pos = jnp.where(in_ctx, (i - n_lat_tiles) % ctx_tiles_per_seq, i % lat_tiles_per_seq)
    last = jnp.where(in_ctx, ctx_tiles_per_seq - 1, lat_tiles_per_seq - 1)
    is_first = pos == 0
    is_last = pos == last
    rowid = lax.broadcasted_iota(jnp.int32, (tp, 1), 0)

    def shifted(c0, c1):
        x = p_ref[:, c0:c1]
        prow = jnp.where(is_first, 0.0, pp_ref[SUBLANE - 1:SUBLANE, c0:c1])
        nrow = jnp.where(is_last, 0.0, pn_ref[0:1, c0:c1])
        xp = jnp.where(rowid == 0, prow, pltpu.roll(x, 1, 0))
        xn = jnp.where(rowid == tp - 1, nrow, pltpu.roll(x, tp - 1, 0))
        return x + mu_ref[0:1, c0:c1] * (xp - x) + mu_ref[1:2, c0:c1] * (xn - x)

    e_bf = e_ref[...]
    r = shifted(0, aw)
    k = shifted(aw, 2 * aw)
    v = shifted(2 * aw, 3 * aw)
    gd = shifted(3 * aw, 3 * aw + A_LORA_G)
    lo0 = 3 * aw + A_LORA_G
    xl = shifted(lo0, lo0 + 2 * A_LORA_W + 2 * A_LORA_A)

    r_o[...] = r.astype(r_o.dtype)
    v_o[...] = v.astype(v_o.dtype)
    g_o[...] = _dot(_bf(_sigmoid(gd)), g2_ref[...])

    kx = k * kk_ref[...]
    kn = jnp.sqrt(_head_sum(kx * kx, e_bf))
    kk = kx / jnp.maximum(kn, 1e-12)
    kk_o[...] = kk.astype(kk_o.dtype)

    xl_t = _bf(jnp.tanh(xl))
    xl_b = _bf(xl)
    kd_sum = None
    for d, (lw_o, kd_o, b_o) in enumerate(((lw0_o, kd0_o, b0_o), (lw1_o, kd1_o, b1_o))):
        w_log = -_softplus(-(w0_ref[d:d + 1, :] + _dot(xl_t, w2_ref[d]))) - 0.5
        lw_o[...] = -jnp.exp(w_log)
        asig = _sigmoid(a0_ref[d:d + 1, :] + _dot(xl_b, a2_ref[d]))
        kd = k * (1.0 + (asig - 1.0) * ka_ref[...])
        kd_o[...] = kd.astype(kd_o.dtype)
        b_o[...] = (kk * asig).astype(b_o.dtype)
        kd_sum = kd if kd_sum is None else kd_sum + kd
    bonus_o[...] = _head_sum(r * rk_ref[...] * kd_sum, e_bf) * v


def _rwkv_prep(pa, mu_p, w0, a0, w2p, a2p, g2, k_k, k_a, r_k, e_bf, *, n_lat, seq, ctx_len):
    r = pa.shape[0]
    ncol = mu_p.shape[1]
    aw = w0.shape[1]
    tp = PREP_TILE
    hb = tp // SUBLANE
    nblk8 = r // SUBLANE
    lw = 2 * A_LORA_W + 2 * A_LORA_A
    row = lambda a: a.reshape(1, aw)
    full = lambda shape: pl.BlockSpec(shape, lambda i: (0,) * len(shape))
    out_spec = pl.BlockSpec((tp, aw), lambda i: (i, 0))
    out_dtypes = [BF16, BF16, BF16, F32, F32, BF16, BF16, BF16, BF16, F32, F32]
    return pl.pallas_call(
        functools.partial(_rwkv_prep_kernel, n_lat_tiles=n_lat // tp, lat_tiles_per_seq=seq // tp,
                          ctx_tiles_per_seq=ctx_len // tp, aw=aw),
        grid=(r // tp,),
        in_specs=[
            pl.BlockSpec((tp, ncol), lambda i: (i, 0)),
            pl.BlockSpec((SUBLANE, ncol), lambda i: (jnp.maximum(i * hb - 1, 0), 0)),
            pl.BlockSpec((SUBLANE, ncol), lambda i: (jnp.minimum((i + 1) * hb, nblk8 - 1), 0)),
            full((2, ncol)), full((2, aw)), full((2, aw)),
            full((2, lw, aw)), full((2, lw, aw)), full((A_LORA_G, aw)),
            full((1, aw)), full((1, aw)), full((1, aw)), full((GROUP_W, GROUP_W)),
        ],
        out_specs=[out_spec] * 11,
        out_shape=[jax.ShapeDtypeStruct((r, aw), dt) for dt in out_dtypes],
        compiler_params=_cparams(("parallel",)),
    )(pa, pa, pa, mu_p, w0, a0, w2p, a2p, g2, row(k_k), row(k_a), row(r_k), e_bf)


def _rwkv_masks(reverse):
    L = CHUNK
    gw = GROUP_W
    row = lax.broadcasted_iota(jnp.int32, (gw, gw), 0)
    col = lax.broadcasted_iota(jnp.int32, (gw, gw), 1)
    same_head = (row >> 6) == (col >> 6)
    t_n = lax.broadcasted_iota(jnp.int32, (L, gw), 0)
    s_n = lax.broadcasted_iota(jnp.int32, (L, gw), 1) & (L - 1)
    strict_n = (t_n < s_n) if reverse else (t_n > s_n)
    incl_n = (t_n <= s_n) if reverse else (t_n >= s_n)
    t_r = lax.broadcasted_iota(jnp.int32, (L, L), 0)
    t_c = lax.broadcasted_iota(jnp.int32, (L, L), 1)
    tri = _bf(jnp.where((t_c >= t_r) if reverse else (t_c <= t_r), 1.0, 0.0))
    return dict(same_head=same_head, strict_n=strict_n, eye_n=t_n == s_n,
                incl_n2=jnp.concatenate([incl_n, incl_n], axis=1), tri=tri)


def _rwkv_chunks(probs, side=None):
    L = CHUNK
    gw = GROUP_W
    tick = (lambda: None) if side is None else (lambda: next(side, None))
    masks = {rv: _rwkv_masks(rv) for rv in sorted({p[7] for p in probs})}
    mk = [masks[p[7]] for p in probs]
    each = lambda f, *cols: [f(*args) for args in zip(*cols)]
    rep = lambda x: jnp.concatenate([x] * HEADS_PER_GROUP, axis=0)

    c = [_dot_exact_lhs(m["tri"], p[3]) for m, p in zip(mk, probs)]
    c_l = [ci[(0 if p[7] else L - 1):(1 if p[7] else L), :] for ci, p in zip(c, probs)]

    blocks = [lambda x, sh=m["same_head"]: _bf(jnp.where(sh, rep(x), 0.0)) for m in mk]
    an, rn, vn, v_bd, bk_end, btk_bd, e_l = [], [], [], [], [], [], []
    for (r, v, kk, lw, kd, b, _, _), bd, ci, cl in zip(probs, blocks, c, c_l):
        e_nc = jnp.exp(-ci)
        e_end = jnp.exp(cl - ci)
        an.append(_bf(-kk * jnp.exp(ci - lw)))
        rn.append(_bf(r * jnp.exp(ci)))
        btk_bd.append(jnp.concatenate([bd(b * e_nc), bd(kd * e_nc)], axis=0))
        vn.append(_bf(v))
        v_bd.append(bd(v))
        bk_end.append(jnp.concatenate([_bf(b * e_end), _bf(kd * e_end)], axis=0))
        e_l.append(jnp.exp(cl))

    a_pow, t, a_ak, rbk = [], [], [], []
    for m, a_n, r_n, bk in zip(mk, an, rn, btk_bd):
        sn = _dot_nt(jnp.concatenate([a_n, r_n], axis=0), bk)
        a_ab = jnp.where(m["strict_n"], sn[:L, :gw], 0.0)
        a_pow.append(a_ab)
        t.append(jnp.where(m["eye_n"], 1.0, a_ab))
        a_ak.append(_bf(jnp.where(m["strict_n"], sn[:L, gw:], 0.0)))
        rbk.append(_bf(jnp.where(m["incl_n2"], sn[L:, :], 0.0)))
    tick()
    akv = each(_dot, a_ak, v_bd)
    a_bd = [bd(a) for bd, a in zip(blocks, a_pow)]
    tb = [_bf(ti) for ti in t]
    for _ in range(5):
        a_pow = [_dot(_bf(a), ab) for a, ab in zip(a_pow, a_bd)]
        a_bd = [bd(a) for bd, a in zip(blocks, a_pow)]
        tick()
        t = [ti + _dot(tbi, ab) for ti, tbi, ab in zip(t, tb, a_bd)]
        tb = [_bf(ti) for ti in t]
    h_bf = [_bf(p[6]) for p in probs]
    x_bd = [bd(_dot(a_n, h) + ak) for bd, a_n, h, ak in zip(blocks, an, h_bf, akv)]
    tick()
    u = [_dot(tbi, xi) for tbi, xi in zip(tb, x_bd)]
    tick()
    y = [_dot(r_n, h) + _dot(rbk_i, jnp.concatenate([bd(ui), v], axis=0))
         for bd, r_n, h, rbk_i, ui, v in zip(blocks, rn, h_bf, rbk, u, v_bd)]
    e_col = [jnp.broadcast_to(el, (gw, gw)).T for el in e_l]
    h_new = []
    for m, ec, p, bke, ui, v_n in zip(mk, e_col, probs, bk_end, u, vn):
        upd = _dot_tn(bke, jnp.concatenate([_bf(ui), v_n], axis=0))
        h_new.append(ec * p[6] + jnp.where(m["same_head"], upd, 0.0))
    if side is not None:
        for _ in side:
            pass
    return list(zip(y, h_new))


def _chunk_row_block(b, i, *, reverse, n_lat, seq, ctx_len):
    nc = ctx_len // CHUNK
    nl = seq // CHUNK
    ci = (nc - 1 - i) if reverse else i
    li = (nl - 1 - (i - nc)) if reverse else (i - nc)
    return jnp.where(i < nc, (n_lat + b * ctx_len) // CHUNK + ci, b * nl + li)


def _gla_stages(qkf, vf, gdf, qkb, vb, gdb, gw2_ref, gb_ref, of_ref, ob_ref, sf_ref, sb_ref):
    dv, dk = sf_ref.shape[1:]
    L = CHUNK
    wk = B_HEADS * dk
    t_r = lax.broadcasted_iota(jnp.int32, (L, L), 0)
    t_c = lax.broadcasted_iota(jnp.int32, (L, L), 1)

    qh, kh, keh, vh, dech, keeph, sth = [], [], [], [], [], [], []
    for d, (qk_ref, v_ref, gd_ref, st_ref) in enumerate(((qkf, vf, gdf, sf_ref),
                                                         (qkb, vb, gdb, sb_ref))):
        reverse = d == 1
        keep = (t_c >= t_r) if reverse else (t_c <= t_r)
        tri = _bf(jnp.where(keep, 1.0, 0.0))
        z = _dot(_bf(gd_ref[...]), gw2_ref[d]) + gb_ref[d:d + 1, :]
        la = -_softplus(-z) * (1.0 / B_GATE_TAU)
        cum = _dot_exact_lhs(tri, la)
        last = 0 if reverse else L - 1
        cl = cum[last:last + 1, :]
        q_in = _bf(qk_ref[:, :wk] * (dk ** -0.5) * jnp.exp(cum))
        k_in = _bf(qk_ref[:, wk:] * jnp.exp(-cum))
        k_end = _bf(qk_ref[:, wk:] * jnp.exp(cl - cum))
        dec = jnp.exp(cl)
        for h in range(B_HEADS):
            ks = slice(h * dk, (h + 1) * dk)
            qh.append(q_in[:, ks])
            kh.append(k_in[:, ks])
            keh.append(k_end[:, ks])
            vh.append(_bf(v_ref[:, h * dv:(h + 1) * dv]))
            dech.append(dec[:, ks])
            keeph.append(keep)
            sth.append(st_ref[h])
        yield

    sc = [_bf(jnp.where(kp, _dot_nt(q, k), 0.0)) for kp, q, k in zip(keeph, qh, kh)]
    yield
    inter = [_dot_nt(q, _bf(st)) for q, st in zip(qh, sth)]
    yield
    kvt = [_dot_tn(v, ke) for v, ke in zip(vh, keh)]
    yield
    out = [_dot(s_, v) + it for s_, v, it in zip(sc, vh, inter)]
    yield
    for d, (o_ref, st_ref) in enumerate(((of_ref, sf_ref), (ob_ref, sb_ref))):
        for h in range(B_HEADS):
            i = d * B_HEADS + h
            o_ref[:, h * dv:(h + 1) * dv] = out[i]
            st_ref[h] = sth[i] * dech[i] + kvt[i]


def _mix_scan_kernel(rf, vf, kkf, lwf, kdf, bf_, rb, vb, kkb, lwb, kdb, bb,
                     qkf, gvf, gdf, qkb, gvb, gdb, gw2_ref, gb_ref,
                     yf_ref, yb_ref, of_ref, ob_ref, hf_ref, hb_ref, sf_ref, sb_ref):
    @pl.when(pl.program_id(1) == 0)
    def _():
        for ref in (hf_ref, hb_ref, sf_ref, sb_ref):
            ref[...] = jnp.zeros_like(ref)

    gs = hf_ref.shape[0]
    probs = []
    for gi in range(gs):
        sl = slice(gi * GROUP_W, (gi + 1) * GROUP_W)
        f32 = lambda ref: ref[:, sl].astype(F32)
        probs.append((f32(rf), f32(vf), f32(kkf), lwf[:, sl], f32(kdf), f32(bf_), hf_ref[gi], False))
        probs.append((f32(rb), f32(vb), f32(kkb), lwb[:, sl], f32(kdb), f32(bb), hb_ref[gi], True))
    gla = _gla_stages(qkf, gvf, gdf, qkb, gvb, gdb, gw2_ref, gb_ref, of_ref, ob_ref, sf_ref,
                      sb_ref)
    res = _rwkv_chunks(probs, side=gla)
    for gi in range(gs):
        sl = slice(gi * GROUP_W, (gi + 1) * GROUP_W)
        yf_ref[:, sl], hf_ref[gi] = res[2 * gi]
        yb_ref[:, sl], hb_ref[gi] = res[2 * gi + 1]


def _mix_scan(r, v, kk, lw0, lw1, kd0, kd1, b0, b1, pb, gw2p, gb, *, batch, n_lat, seq, ctx_len,
              wk, wv, col0):
    rows, aw = r.shape
    assert aw == GROUP_W * RWKV_GROUPS_PER_STEP
    dk, dv = wk // B_HEADS, wv // B_HEADS
    geo = dict(n_lat=n_lat, seq=seq, ctx_len=ctx_len)
    gd_blk = (col0 + 2 * wk + 2 * wv) // LANE

    def specs(reverse):
        rb = lambda b, i: _chunk_row_block(b, i, reverse=reverse, **geo)
        rw = pl.BlockSpec((CHUNK, aw), lambda b, i: (rb(b, i), 0))
        gla = [
            pl.BlockSpec((CHUNK, 2 * wk), lambda b, i: (rb(b, i), col0 // (2 * wk))),
            pl.BlockSpec((CHUNK, wv), lambda b, i: (rb(b, i), (col0 + 2 * wk) // wv)),
            pl.BlockSpec((CHUNK, LANE), lambda b, i: (rb(b, i), gd_blk)),
        ]
        out = pl.BlockSpec((CHUNK, wv), lambda b, i: (rb(b, i), 0))
        return rw, gla, out

    rwf, glaf, outf = specs(False)
    rwb, glab, outb = specs(True)
    sds_a = jax.ShapeDtypeStruct((rows, aw), F32)
    sds_b = jax.ShapeDtypeStruct((rows, wv), F32)
    gs = RWKV_GROUPS_PER_STEP
    return pl.pallas_call(
        _mix_scan_kernel,
        grid=(batch, (seq + ctx_len) // CHUNK),
        in_specs=[rwf] * 6 + [rwb] * 6 + glaf + glab + [
            pl.BlockSpec((2, LANE, wk), lambda b, i: (0, 0, 0)),
            pl.BlockSpec((2, wk), lambda b, i: (0, 0)),
        ],
        out_specs=[rwf, rwb, outf, outb],
        out_shape=[sds_a, sds_a, sds_b, sds_b],
        scratch_shapes=[pltpu.VMEM((gs, GROUP_W, GROUP_W), F32),
                        pltpu.VMEM((gs, GROUP_W, GROUP_W), F32),
                        pltpu.VMEM((B_HEADS, dv, dk), F32), pltpu.VMEM((B_HEADS, dv, dk), F32)],
        compiler_params=_cparams(("parallel", "arbitrary")),
    )(r, v, kk, lw0, kd0, b0, r, v, kk, lw1, kd1, b1, pb, pb, pb, pb, pb, pb, gw2p, gb)


def _mixer_out_kernel(yf, yb, bonus, g, of, ob, gate, xa_ref, xb_ref, mod_ref, lnw, lnb, gn, e_ref,
                      w_ref, o_ref, h_ref, *, aw, dv, n_a_tiles):
    e_bf = e_ref[...]
    y = yf[...] + yb[...]
    inv = 1.0 / A_HEAD_DIM
    mean = _head_sum(y, e_bf) * inv
    dlt = y - mean
    var = _head_sum(dlt * dlt, e_bf) * inv
    yn = dlt * lax.rsqrt(var + A_LN_EPS) * lnw[...] + lnb[...]
    h_ref[:, :aw] = _bf((yn + bonus[...]) * g[...])

    o = of[...] + ob[...]
    gt = _silu(gate[...])
    for h in range(B_HEADS):
        sl = slice(h * dv, (h + 1) * dv)
        oh = o[:, sl]
        ms = jnp.mean(oh * oh, axis=-1, keepdims=True)
        h_ref[:, aw + h * dv:aw + (h + 1) * dv] = _bf(oh * lax.rsqrt(ms + NORM_EPS) * gn[...] * gt[:, sl])

    x = jnp.where(pl.program_id(0) < n_a_tiles, xa_ref[...], xb_ref[...])
    o_ref[...] = x + mod_ref[2:3, :] * _dot(h_ref[...], w_ref[...])


def _mixer_out(yf, yb, bonus, g, of, ob, pb, xa, xb, mods_l, ln_w, ln_b, gla_norm, e_bf, w_bf,
               seg_of_tile, *, wk, col0):
    d = xa.shape[1]
    rows = xa.shape[0] + xb.shape[0]
    aw = yf.shape[1]
    wv = of.shape[1]
    dv = wv // B_HEADS
    tm = 256
    tiles_per_row_tile = ROW_TILE // tm
    n_a_tiles = xa.shape[0] // tm
    rs = lambda w: pl.BlockSpec((tm, w), lambda i: (i, 0))
    full = lambda shape: pl.BlockSpec(shape, lambda i: (0,) * len(shape))
    return pl.pallas_call(
        functools.partial(_mixer_out_kernel, aw=aw, dv=dv, n_a_tiles=n_a_tiles),
        grid=(rows // tm,),
        in_specs=[
            rs(aw), rs(aw), rs(aw), rs(aw), rs(wv), rs(wv),
            pl.BlockSpec((tm, wv), lambda i: (i, (col0 + 2 * wk + wv) // wv)),
        ] + _two_source_specs(tm, d, n_a_tiles) + [
            pl.BlockSpec((None, N_MOD, d), lambda i: (seg_of_tile(i // tiles_per_row_tile), 0, 0)),
            full((1, aw)), full((1, aw)), full((1, dv)), full((GROUP_W, GROUP_W)),
            full((aw + wv, d)),
        ],
        out_specs=rs(d),
        out_shape=jax.ShapeDtypeStruct((rows, d), F32),
        scratch_shapes=[pltpu.VMEM((tm, aw + wv), BF16)],
        compiler_params=_cparams(("parallel",)),
    )(yf, yb, bonus, g, of, ob, pb, xa, xb, mods_l, ln_w.reshape(1, aw), ln_b.reshape(1, aw),
      gla_norm.reshape(1, dv), e_bf, w_bf)


def _lru_kernel(*refs, reverse, n_blk, combine, cols):
    if combine:
        (x_ref, xp_ref, xn_ref, h0_ref, cw_ref, cb_ref, wc_ref, ba_ref, bx_ref, lam_ref,
         hf_ref, gate_ref, out_ref, hl_ref, a_sc, u_sc, hs_sc, carry_sc) = refs
    else:
        (x_ref, xp_ref, xn_ref, h0_ref, cw_ref, cb_ref, wc_ref, ba_ref, bx_ref, lam_ref,
         out_ref, hl_ref, a_sc, u_sc, hs_sc, carry_sc) = refs
    j = pl.program_id(2)
    blk = (n_blk - 1 - j) if reverse else j
    lb = x_ref.shape[0]
    ct = cw_ref.shape[1]
    cblk = wc_ref.shape[1]
    rowid = lax.broadcasted_iota(jnp.int32, (lb, 1), 0)
    decay_rate = -C_CONST * _softplus(-lam_ref[...])

    @pl.when(j == 0)
    def _():
        carry_sc[...] = h0_ref[0:1, :]

    for q in (range(cols - 1, -1, -1) if reverse else range(cols)):
        lanes = slice(q * ct, (q + 1) * ct)
        x = x_ref[:, lanes]
        if q > 0:
            prev = slice((q - 1) * ct, q * ct)
            p6, p7 = x_ref[lb - 2:lb - 1, prev], x_ref[lb - 1:lb, prev]
        else:
            p6 = jnp.where(blk == 0, 0.0, xp_ref[SUBLANE - 2:SUBLANE - 1, :])
            p7 = jnp.where(blk == 0, 0.0, xp_ref[SUBLANE - 1:SUBLANE, :])
        if q < cols - 1:
            n0 = x_ref[0:1, (q + 1) * ct:(q + 2) * ct]
        else:
            n0 = jnp.where(blk == n_blk - 1, 0.0, xn_ref[0:1, :])
        xm1 = jnp.where(rowid == 0, p7, pltpu.roll(x, 1, 0))
        xm2 = jnp.where(rowid == 0, p6, jnp.where(rowid == 1, p7, pltpu.roll(x, 2, 0)))
        xp1 = jnp.where(rowid == lb - 1, n0, pltpu.roll(x, lb - 1, 0))
        xs = (cb_ref[...] + cw_ref[0:1, :] * xm2 + cw_ref[1:2, :] * xm1 + cw_ref[2:3, :] * x
              + cw_ref[3:4, :] * xp1)

        for n in range(ct // cblk):
            sl = slice(n * cblk, (n + 1) * cblk)
            xb = xs[:, sl]
            ri = _dot(_bf(xb), wc_ref[n])
            rg = _sigmoid(ri[:, :cblk] + ba_ref[:, sl])
            ig = _sigmoid(ri[:, cblk:] + bx_ref[:, sl])
            a = jnp.exp(decay_rate[:, sl] * rg)
            a_sc[q, :, sl] = a
            u_sc[q, :, sl] = jnp.sqrt(1.0 - a * a) * (ig * xb)

        def step(t, h, q=q):
            tt = (lb - 1 - t) if reverse else t
            h = a_sc[q, pl.ds(tt, 1), :] * h + u_sc[q, pl.ds(tt, 1), :]
            hs_sc[q, pl.ds(tt, 1), :] = h
            return h

        carry_sc[...] = lax.fori_loop(0, lb, step, carry_sc[...], unroll=8)
        if combine:
            out_ref[:, lanes] = ((hs_sc[q] + hf_ref[q * lb:(q + 1) * lb, :])
                                 * jax.nn.gelu(gate_ref[:, lanes]))
        else:
            out_ref[q * lb:(q + 1) * lb, :] = hs_sc[q]
    hl_ref[...] = jnp.broadcast_to(carry_sc[...], hl_ref.shape)


def _lru_pass(x_arr, x_specs, h0, conv_w, conv_b, wcat, ba, bx, lam, out_shape, out_spec, extra,
              *, batch, n_blk, lb, cols, reverse):
    c = conv_w.shape[1]
    ct = LRU_CHANNEL_TILE
    nct = c // ct
    cblk = wcat.shape[1]
    chan = lambda rows_: pl.BlockSpec((rows_, ct), lambda b, k, j: (0, k))
    in_specs = list(x_specs) + [
        pl.BlockSpec((SUBLANE, ct), lambda b, k, j: (b, k)),
        chan(C_CONV), chan(1),
        pl.BlockSpec((ct // cblk, cblk, 2 * cblk), lambda b, k, j: (k, 0, 0)),
        chan(1), chan(1), chan(1),
    ] + [s for _, s in extra]
    args = [x_arr, x_arr, x_arr, h0, conv_w, conv_b.reshape(1, c), wcat, ba.reshape(1, c),
            bx.reshape(1, c), lam.reshape(1, c)] + [a for a, _ in extra]
    return pl.pallas_call(
        functools.partial(_lru_kernel, reverse=reverse, n_blk=n_blk, combine=bool(extra),
                          cols=cols),
        grid=(batch, nct, n_blk),
        in_specs=in_specs,
        out_specs=[out_spec, pl.BlockSpec((SUBLANE, ct), lambda b, k, j: (b, k))],
        out_shape=[out_shape, jax.ShapeDtypeStruct((batch * SUBLANE, c), F32)],
        scratch_shapes=[pltpu.VMEM((cols, lb, ct), F32), pltpu.VMEM((cols, lb, ct), F32),
                        pltpu.VMEM((cols, lb, ct), F32), pltpu.VMEM((1, ct), F32)],
        compiler_params=_cparams(("parallel", "parallel", "arbitrary")),
    )(*args)


def _rglru(hw_ctx, gate_grid, x_grid, conv_w, conv_b, wa, ba, wx, bx, lam, *, batch, n_lat, seq,
           ctx_len):
    c = conv_w.shape[1]
    ct = LRU_CHANNEL_TILE
    rows = seq // GRID_W
    lbc = 128
    ncb = ctx_len // lbc
    xcol = c // ct

    wcat = [_bf(jnp.concatenate([wa[d], wx[d]], axis=-1)) for d in range(2)]
    zeros_state = jnp.zeros((batch * SUBLANE, c), F32)

    def ctx_specs(reverse):
        blk = (lambda j: ncb - 1 - j) if reverse else (lambda j: j)
        per8 = lbc // SUBLANE
        cur = pl.BlockSpec((lbc, ct), lambda b, k, j: (b * ncb + blk(j), xcol + k))
        prv = pl.BlockSpec((SUBLANE, ct), lambda b, k, j: (
            b * ncb * per8 + jnp.maximum(blk(j) * per8 - 1, 0), xcol + k))
        nxt = pl.BlockSpec((SUBLANE, ct), lambda b, k, j: (
            b * ncb * per8 + jnp.minimum((blk(j) + 1) * per8, ncb * per8 - 1), xcol + k))
        return cur, prv, nxt

    assert ct == c
    kc = LRU_COLS_PER_STEP
    ncg = GRID_W // kc

    def lat_specs(reverse):
        grp = (lambda j: ncg - 1 - j) if reverse else (lambda j: j)
        r8 = rows // SUBLANE
        cur = pl.BlockSpec((rows, kc * ct), lambda b, k, j: (b, grp(j)))
        prv = pl.BlockSpec((SUBLANE, ct), lambda b, k, j: (
            b * r8 + r8 - 1, jnp.maximum(grp(j) * kc - 1, 0)))
        nxt = pl.BlockSpec((SUBLANE, ct), lambda b, k, j: (
            b * r8, jnp.minimum(grp(j) * kc + kc, GRID_W - 1)))
        hf = pl.BlockSpec((kc * rows, ct), lambda b, k, j: (b * ncg + grp(j), 0))
        return (cur, prv, nxt), hf, cur

    common = dict(batch=batch)
    ctx_scratch_sds = jax.ShapeDtypeStruct((batch * ctx_len, c), F32)
    ctx_out = lambda reverse: pl.BlockSpec(
        (lbc, ct), lambda b, k, j: (b * ncb + ((ncb - 1 - j) if reverse else j), k))
    _, st = _lru_pass(hw_ctx, ctx_specs(False), zeros_state, conv_w, conv_b, wcat[0], ba[0], bx[0],
                      lam[0], ctx_scratch_sds, ctx_out(False), [], n_blk=ncb, lb=lbc, cols=1,
                      reverse=False, **common)
    hf_sds = jax.ShapeDtypeStruct((n_lat, c), F32)
    x_specs, hf_spec, _ = lat_specs(False)
    hf, _ = _lru_pass(x_grid, x_specs, st, conv_w, conv_b, wcat[0], ba[0], bx[0], lam[0],
                      hf_sds, hf_spec, [], n_blk=ncg, lb=rows, cols=kc, reverse=False, **common)
    _, st = _lru_pass(hw_ctx, ctx_specs(True), zeros_state, conv_w, conv_b, wcat[1], ba[1], bx[1],
                      lam[1], ctx_scratch_sds, ctx_out(True), [], n_blk=ncb, lb=lbc, cols=1,
                      reverse=True, **common)
    x_specs, hf_spec, grid_spec = lat_specs(True)
    y_sds = jax.ShapeDtypeStruct((n_lat // GRID_W, GRID_W * c), F32)
    y, _ = _lru_pass(x_grid, x_specs, st, conv_w, conv_b, wcat[1], ba[1], bx[1], lam[1],
                     y_sds, grid_spec, [(hf, hf_spec), (gate_grid, grid_spec)],
                     n_blk=ncg, lb=rows, cols=kc, reverse=True, **common)
    return y


def _pad_cols(w, n):
    return jnp.pad(w, ((0, 0),) * (w.ndim - 1) + ((0, n - w.shape[-1]),))


def kernel(x, c, ctx, c_ctx, mod_w, mod_b, norm1, norm2, mlp_w1, mlp_w2, ab_w_in, ab_w_out, rw_mu,
           rw_w0, rw_w2, rw_a0, rw_a2, rw_g2, rw_kk, rw_ka, rw_rk, rw_ln_w, rw_ln_b, gla_gw2,
           gla_gb, gla_norm, lru_w_in, lru_w_out, lru_conv_w, lru_conv_b, lru_wa, lru_ba, lru_wx,
           lru_bx, lru_lam, final_norm):
    batch, seq, d = x.shape
    ctx_len = ctx.shape[1]
    depth = mod_w.shape[0]
    n_lat = batch * seq
    aw = rw_w0.shape[-1]
    wk = gla_gb.shape[-1]
    wv = ab_w_out.shape[1] - aw
    a_cols = rw_mu.shape[-1]
    b_cols = ab_w_in.shape[-1] - a_cols
    assert depth == 2 and batch + 1 <= SUBLANE
    assert seq % ROW_TILE == 0 and (batch * ctx_len) % ROW_TILE == 0
    assert ctx_len % PREP_TILE == 0 and ctx_len % 128 == 0 and seq % (GRID_W * SUBLANE) == 0
    assert aw % GROUP_W == 0 and wv == aw and 2 * wk == wv

    tiles_per_seq = seq // ROW_TILE
    seg_of_tile = lambda i: jnp.minimum(i // tiles_per_seq, batch)

    x2 = x.reshape(n_lat, d)
    ctx2 = ctx.reshape(batch * ctx_len, d)
    cvec = jnp.concatenate(
        [c, c_ctx[None, :], jnp.zeros((SUBLANE - batch - 1, d), F32)], axis=0)
    mods = _mods(cvec, mod_w, mod_b)

    hid = jnp.arange(GROUP_W) // A_HEAD_DIM
    e_bf = _bf(hid[:, None] == hid[None, :])

    a_pad = -(-a_cols // LANE) * LANE
    col0 = -(-a_cols // (2 * wk)) * (2 * wk)
    n_tiles = 3
    tn = -(-(col0 + b_cols) // (n_tiles * LANE)) * LANE
    w_ab = _bf(jnp.concatenate(
        [_pad_cols(ab_w_in[0][:, :a_cols], col0),
         _pad_cols(ab_w_in[0][:, a_cols:], n_tiles * tn - col0)], axis=1))
    pa = pb = _norm_matmul(x2, ctx2, norm1[0], mods[0], w_ab, seg_of_tile, tn=tn)

    lora_w = 2 * A_LORA_W + 2 * A_LORA_A
    w2p = jnp.zeros((2, lora_w, aw), F32)
    a2p = jnp.zeros((2, lora_w, aw), F32)
    for dd in range(2):
        w2p = w2p.at[dd, dd * A_LORA_W:(dd + 1) * A_LORA_W].set(rw_w2[0, dd])
        a2p = a2p.at[dd, 2 * A_LORA_W + dd * A_LORA_A:2 * A_LORA_W + (dd + 1) * A_LORA_A].set(
            rw_a2[0, dd])
    (r_, v_, kk_, lw0, lw1, kd0, kd1, b0, b1, g_, bonus) = _rwkv_prep(
        pa, _pad_cols(rw_mu[0], a_pad), rw_w0[0], rw_a0[0], _bf(w2p), _bf(a2p), _bf(rw_g2[0]),
        rw_kk[0], rw_ka[0], rw_rk[0].reshape(-1), e_bf, n_lat=n_lat, seq=seq, ctx_len=ctx_len)
    gw2p = jnp.zeros((2, LANE, wk), F32)
    for dd in range(2):
        gw2p = gw2p.at[dd, dd * B_LORA:(dd + 1) * B_LORA].set(gla_gw2[0, dd])
    yf, yb, of, ob = _mix_scan(r_, v_, kk_, lw0, lw1, kd0, kd1, b0, b1, pb, _bf(gw2p), gla_gb[0],
                               batch=batch, n_lat=n_lat, seq=seq, ctx_len=ctx_len, wk=wk, wv=wv,
                               col0=col0)

    xs = _mixer_out(yf, yb, bonus, g_, of, ob, pb, x2, ctx2, mods[0], rw_ln_w[0], rw_ln_b[0],
                    gla_norm[0], e_bf, _bf(ab_w_out[0]), seg_of_tile, wk=wk, col0=col0)
    seg_of_mlp_tile = lambda i: jnp.minimum(i // (seq // MLP_TILE), batch)
    ctx_tiles = (batch * ctx_len) // ROW_TILE
    w1_bf, w2_bf = _bf(mlp_w1[0]), _bf(mlp_w2[0])
    x_lat = _mlp(xs, norm2[0], mods[0], w1_bf, w2_bf, final_norm, seg_of_mlp_tile, tm=MLP_TILE,
                 tile0=0, n_tiles=n_lat // MLP_TILE, final_norm=False)
    x_ctx = _mlp(xs, norm2[0], mods[0], w1_bf, w2_bf, final_norm, seg_of_tile, tm=ROW_TILE,
                 tile0=n_lat // ROW_TILE, n_tiles=ctx_tiles, final_norm=False)

    w_lru = _bf(lru_w_in[0])
    hw_ctx = _norm_matmul(x_ctx, None, norm1[1], mods[1], w_lru, lambda i: batch, tn=2048)
    gate_grid, x_grid = _norm_matmul_grid(x_lat, n_lat, seq // GRID_W, norm1[1], mods[1], w_lru)
    y = _rglru(hw_ctx, gate_grid, x_grid, lru_conv_w[0], lru_conv_b[0], lru_wa[0], lru_ba[0],
               lru_wx[0], lru_bx[0], lru_lam[0], batch=batch, n_lat=n_lat, seq=seq,
               ctx_len=ctx_len)
    xl = _proj_res(y, x_lat, n_lat, seq // GRID_W, mods[1], _bf(lru_w_out[0]))
    out = _mlp(xl, norm2[1], mods[1], _bf(mlp_w1[1]), _bf(mlp_w2[1]), final_norm, seg_of_mlp_tile,
               tm=MLP_TILE, tile0=0, n_tiles=n_lat // MLP_TILE, final_norm=True)
    return out.reshape(batch, seq, d)
```

```python
import functools

import jax
import jax.numpy as jnp
from jax import lax
from jax.experimental import pallas as pl
from jax.experimental.pallas import tpu as pltpu

F32 = jnp.float32
BF16 = jnp.bfloat16

NORM_EPS = 1e-6
GRID_W = 64
N_MOD = 6

A_HEAD_DIM = 64
A_LORA_W = 96
A_LORA_A = 96
A_LORA_G = 256
A_LN_EPS = 64e-5
HEADS_PER_GROUP = 4
GROUP_W = HEADS_PER_GROUP * A_HEAD_DIM
CHUNK = 64
RWKV_GROUPS_PER_STEP = 4

B_HEADS = 4
B_LORA = 16
B_GATE_TAU = 16.0

C_BLOCKS = 8
C_CONV = 4
C_CONST = 8.0

LANE = 128
SUBLANE = 8
ROW_TILE = 512
MLP_TILE = 1024
MLP_STEP_ELEMS = 1024 * 512
LRU_CHANNEL_TILE = 2048
LRU_COLS_PER_STEP = 4
PREP_TILE = 128
VMEM_LIMIT = 56 * 1024 * 1024


def _cparams(sem):
    return pltpu.CompilerParams(dimension_semantics=sem, vmem_limit_bytes=VMEM_LIMIT)


def _bf(x):
    return x.astype(BF16)


def _dot(a, b):
    return jnp.dot(a, b, preferred_element_type=F32)


def _dot_nt(a, b):
    return lax.dot_general(a, b, (((1,), (1,)), ((), ())), preferred_element_type=F32)


def _dot_tn(a, b):
    return lax.dot_general(a, b, (((0,), (0,)), ((), ())), preferred_element_type=F32)


def _split3(x):
    hi = _bf(x)
    r1 = x - hi.astype(F32)
    mid = _bf(r1)
    lo = _bf(r1 - mid.astype(F32))
    return hi, mid, lo


def _dot_exact_lhs(m_bf, x):
    hi, mid, lo = _split3(x)
    return _dot(m_bf, hi) + _dot(m_bf, mid) + _dot(m_bf, lo)


def _dot_exact_rhs(x, m_bf):
    hi, mid, lo = _split3(x)
    return _dot(hi, m_bf) + _dot(mid, m_bf) + _dot(lo, m_bf)


def _softplus(x):
    return jnp.maximum(x, 0.0) + jnp.log(1.0 + jnp.exp(-jnp.abs(x)))


def _sigmoid(x):
    return 0.5 * jnp.tanh(0.5 * x) + 0.5


def _silu(x):
    return x * _sigmoid(x)


def _norm_mod(x, g, shift, scale):
    ms = jnp.mean(x * x, axis=-1, keepdims=True)
    return (x * lax.rsqrt(ms + NORM_EPS) * g) * (1.0 + scale) + shift


def _mods_kernel(c_ref, w_ref, b_ref, o_ref):
    s = _silu(c_ref[...])
    o_ref[...] = _dot(_bf(s), _bf(w_ref[...])) + b_ref[...]


def _mods(cvec, mod_w, mod_b):
    depth, d, n = mod_w.shape
    tn = 1024
    out = pl.pallas_call(
        _mods_kernel,
        grid=(depth, n // tn),
        in_specs=[
            pl.BlockSpec((SUBLANE, d), lambda l, j: (0, 0)),
            pl.BlockSpec((None, d, tn), lambda l, j: (l, 0, j)),
            pl.BlockSpec((None, 1, tn), lambda l, j: (l, 0, j)),
        ],
        out_specs=pl.BlockSpec((None, SUBLANE, tn), lambda l, j: (l, 0, j)),
        out_shape=jax.ShapeDtypeStruct((depth, SUBLANE, n), F32),
        compiler_params=_cparams(("parallel", "parallel")),
    )(cvec, mod_w, mod_b.reshape(depth, 1, n))
    return out.reshape(depth, SUBLANE, N_MOD, d)


def _norm_matmul_kernel(xa_ref, xb_ref, g_ref, mod_ref, w_ref, o_ref, h_ref, *, n_a_tiles):
    i = pl.program_id(0)

    @pl.when(pl.program_id(1) == 0)
    def _():
        x = jnp.where(i < n_a_tiles, xa_ref[...], xb_ref[...])
        h_ref[...] = _bf(_norm_mod(x, g_ref[...], mod_ref[0:1, :], mod_ref[1:2, :]))

    o_ref[...] = _dot(h_ref[...], w_ref[...])


def _two_source_specs(tm, d, n_a_tiles):
    return [pl.BlockSpec((tm, d), lambda i, *_: (jnp.minimum(i, n_a_tiles - 1), 0)),
            pl.BlockSpec((tm, d), lambda i, *_: (jnp.maximum(i - n_a_tiles, 0), 0))]


def _norm_matmul(xa, xb, g, mods_l, w_bf, seg_of_tile, *, tn):
    d = xa.shape[1]
    n = w_bf.shape[1]
    tm = ROW_TILE
    n_a_tiles = xa.shape[0] // tm
    n_tiles = n_a_tiles + (0 if xb is None else xb.shape[0] // tm)
    return pl.pallas_call(
        functools.partial(_norm_matmul_kernel, n_a_tiles=n_a_tiles),
        grid=(n_tiles, n // tn),
        in_specs=_two_source_specs(tm, d, n_a_tiles) + [
            pl.BlockSpec((1, d), lambda i, j: (0, 0)),
            pl.BlockSpec((None, N_MOD, d), lambda i, j: (seg_of_tile(i), 0, 0)),
            pl.BlockSpec((d, tn), lambda i, j: (0, j)),
        ],
        out_specs=pl.BlockSpec((tm, tn), lambda i, j: (i, j)),
        out_shape=jax.ShapeDtypeStruct((n_tiles * tm, n), F32),
        scratch_shapes=[pltpu.VMEM((tm, d), BF16)],
        compiler_params=_cparams(("parallel", "arbitrary")),
    )(xa, xa if xb is None else xb, g.reshape(1, d), mods_l, w_bf)


def _grid_tile(rows):
    gr = min(rows, 64)
    gc = ROW_TILE // gr
    assert gc % SUBLANE == 0 and GRID_W % gc == 0 and rows % gr == 0
    return gr, gc


def _norm_matmul_grid_kernel(x_ref, g_ref, mod_ref, w_ref, o0_ref, o1_ref, h_ref, res_ref):
    j = pl.program_id(1)
    gr, gc, d = x_ref.shape

    @pl.when(j == 0)
    def _():
        h = _norm_mod(x_ref[...].reshape(gr * gc, d), g_ref[...], mod_ref[0:1, :], mod_ref[1:2, :])
        h_ref[...] = _bf(h)

    res = _dot(h_ref[...], w_ref[...])
    nh = w_ref.shape[1]
    nq = nh // LANE
    for q in range(nq):
        res_ref[q] = res[:, q * LANE:(q + 1) * LANE]

    def scatter(o_ref):
        for c in range(gc):
            for q in range(nq):
                o_ref[:, c * nh + q * LANE:c * nh + (q + 1) * LANE] = (
                    res_ref[q, pl.ds(c, gr, stride=gc), :])

    @pl.when(j == 0)
    def _():
        scatter(o0_ref)

    @pl.when(j == 1)
    def _():
        scatter(o1_ref)


def _norm_matmul_grid(x, n_lat, rows, g, mods_l, w_bf):
    d = x.shape[1]
    nh = w_bf.shape[1] // 2
    gr, gc = _grid_tile(rows)
    ncg = GRID_W // gc
    x3 = x.reshape(x.shape[0] // GRID_W, GRID_W, d)
    out_spec = pl.BlockSpec((gr, gc * nh), lambda i, j: (i // ncg, i % ncg))
    sds = jax.ShapeDtypeStruct((n_lat // GRID_W, GRID_W * nh), F32)
    return pl.pallas_call(
        _norm_matmul_grid_kernel,
        grid=(n_lat // ROW_TILE, 2),
        in_specs=[
            pl.BlockSpec((gr, gc, d), lambda i, j: (i // ncg, i % ncg, 0)),
            pl.BlockSpec((1, d), lambda i, j: (0, 0)),
            pl.BlockSpec((None, N_MOD, d), lambda i, j: ((i // ncg) * gr // rows, 0, 0)),
            pl.BlockSpec((d, nh), lambda i, j: (0, j)),
        ],
        out_specs=[out_spec, out_spec],
        out_shape=[sds, sds],
        scratch_shapes=[pltpu.VMEM((ROW_TILE, d), BF16),
                        pltpu.VMEM((nh // LANE, ROW_TILE, LANE), F32)],
        compiler_params=_cparams(("parallel", "arbitrary")),
    )(x3, g.reshape(1, d), mods_l, w_bf)


def _mlp_kernel(x_ref, g_ref, mod_ref, w1_ref, w2_ref, fn_ref, o_ref, h_ref, *, final_norm):
    j = pl.program_id(1)
    slab = 256
    n_slab = x_ref.shape[0] // slab

    @pl.when(j == 0)
    def _():
        for s in range(n_slab):
            rs = slice(s * slab, (s + 1) * slab)
            h_ref[rs, :] = _bf(_norm_mod(x_ref[rs, :], g_ref[...], mod_ref[3:4, :],
                                         mod_ref[4:5, :]))
        o_ref[...] = jnp.zeros_like(o_ref)

    a = _dot(h_ref[...], w1_ref[...])
    a = _bf(jnp.square(jnp.maximum(a, 0.0)))
    nsplit = 2
    wn = o_ref.shape[1] // nsplit
    for s in range(nsplit):
        o_ref[:, s * wn:(s + 1) * wn] += _dot(a, w2_ref[:, s * wn:(s + 1) * wn])

    @pl.when(j == pl.num_programs(1) - 1)
    def _():
        for s in range(n_slab):
            rs = slice(s * slab, (s + 1) * slab)
            xo = x_ref[rs, :] + mod_ref[5:6, :] * o_ref[rs, :]
            if final_norm:
                ms = jnp.mean(xo * xo, axis=-1, keepdims=True)
                xo = xo * lax.rsqrt(ms + NORM_EPS) * fn_ref[...]
            o_ref[rs, :] = xo


def _mlp(x, g, mods_l, w1_bf, w2_bf, fn, seg_of_tile, *, tm, tile0, n_tiles, final_norm):
    d = x.shape[1]
    dff = w1_bf.shape[1]
    tf = MLP_STEP_ELEMS // tm
    return pl.pallas_call(
        functools.partial(_mlp_kernel, final_norm=final_norm),
        grid=(n_tiles, dff // tf),
        in_specs=[
            pl.BlockSpec((tm, d), lambda i, j: (i + tile0, 0), pipeline_mode=pl.Buffered(1)),
            pl.BlockSpec((1, d), lambda i, j: (0, 0)),
            pl.BlockSpec((None, N_MOD, d), lambda i, j: (seg_of_tile(i + tile0), 0, 0)),
            pl.BlockSpec((d, tf), lambda i, j: (0, j)),
            pl.BlockSpec((tf, d), lambda i, j: (j, 0)),
            pl.BlockSpec((1, d), lambda i, j: (0, 0)),
        ],
        out_specs=pl.BlockSpec((tm, d), lambda i, j: (i, 0)),
        out_shape=jax.ShapeDtypeStruct((n_tiles * tm, d), F32),
        scratch_shapes=[pltpu.VMEM((tm, d), BF16)],
        compiler_params=_cparams(("parallel", "arbitrary")),
    )(x, g.reshape(1, d), mods_l, w1_bf, w2_bf, fn.reshape(1, d))


def _proj_res_kernel(y_ref, x_ref, mod_ref, w_ref, o_ref, yt_ref):
    k = w_ref.shape[0]
    gr, gc, d = x_ref.shape
    nq = k // LANE
    for c in range(gc):
        for q in range(nq):
            yt_ref[q, pl.ds(c, gr, stride=gc), :] = (
                y_ref[:, c * k + q * LANE:c * k + (q + 1) * LANE])
    yt = jnp.concatenate([_bf(yt_ref[q]) for q in range(nq)], axis=1)
    out = x_ref[...].reshape(gr * gc, d) + mod_ref[2:3, :] * _dot(yt, w_ref[...])
    o_ref[...] = out.reshape(gr, gc, d)


def _proj_res(y_grid, x, n_lat, rows, mods_l, w_bf):
    k, d = w_bf.shape
    gr, gc = _grid_tile(rows)
    ncg = GRID_W // gc
    x3 = x.reshape(x.shape[0] // GRID_W, GRID_W, d)
    tok_spec = pl.BlockSpec((gr, gc, d), lambda i: (i // ncg, i % ncg, 0))
    out = pl.pallas_call(
        _proj_res_kernel,
        grid=(n_lat // ROW_TILE,),
        in_specs=[
            pl.BlockSpec((gr, gc * k), lambda i: (i // ncg, i % ncg)),
            tok_spec,
            pl.BlockSpec((None, N_MOD, d), lambda i: ((i // ncg) * gr // rows, 0, 0)),
            pl.BlockSpec((k, d), lambda i: (0, 0)),
        ],
        out_specs=tok_spec,
        out_shape=jax.ShapeDtypeStruct((n_lat // GRID_W, GRID_W, d), F32),
        scratch_shapes=[pltpu.VMEM((k // LANE, ROW_TILE, LANE), F32)],
        compiler_params=_cparams(("parallel",)),
    )(y_grid, x3, mods_l, w_bf)
    return out.reshape(n_lat, d)


def _head_sum(z, e_bf):
    parts = []
    for gi in range(z.shape[1] // GROUP_W):
        zs = z[:, gi * GROUP_W:(gi + 1) * GROUP_W]
        parts.append(_dot_exact_rhs(zs, e_bf))
    return jnp.concatenate(parts, axis=1)


def _rwkv_prep_kernel(p_ref, pp_ref, pn_ref, mu_ref, w0_ref, a0_ref, w2_ref, a2_ref, g2_ref,
                      kk_ref, ka_ref, rk_ref, e_ref,
                      rvk_o, d0_o, d1_o, lw0_o, lw1_o, g_o, bonus_o,
                      *, n_lat_tiles, lat_tiles_per_seq, ctx_tiles_per_seq, aw):
    i = pl.program_id(0)
    tp = p_ref.shape[0]
    in_ctx = i >= n_lat_tiles
    pos = jnp.where(in_ctx, (i - n_lat_tiles) % ctx_tiles_per_seq, i % lat_tiles_per_seq)
    last = jnp.where(in_ctx, ctx_tiles_per_seq - 1, lat_tiles_per_seq - 1)
    is_first = pos == 0
    is_last = pos == last
    rowid = lax.broadcasted_iota(jnp.int32, (tp, 1), 0)

    def shifted(c0, c1):
        x = p_ref[:, c0:c1]
        prow = jnp.where(is_first, 0.0, pp_ref[SUBLANE - 1:SUBLANE, c0:c1])
        nrow = jnp.where(is_last, 0.0, pn_ref[0:1, c0:c1])
        xp = jnp.where(rowid == 0, prow, pltpu.roll(x, 1, 0))
        xn = jnp.where(rowid == tp - 1, nrow, pltpu.roll(x, tp - 1, 0))
        return x + mu_ref[0:1, c0:c1] * (xp - x) + mu_ref[1:2, c0:c1] * (xn - x)

    e_bf = e_ref[...]
    r = shifted(0, aw)
    k = shifted(aw, 2 * aw)
    v = shifted(2 * aw, 3 * aw)
    gd = shifted(3 * aw, 3 * aw + A_LORA_G)
    lo0 = 3 * aw + A_LORA_G
    xl = shifted(lo0, lo0 + 2 * A_LORA_W + 2 * A_LORA_A)

    rvk_o[:, :aw] = r.astype(rvk_o.dtype)
    rvk_o[:, aw:2 * aw] = v.astype(rvk_o.dtype)
    g_o[...] = _dot(_bf(_sigmoid(gd)), g2_ref[...])

    kx = k * kk_ref[...]
    kn = jnp.sqrt(_head_sum(kx * kx, e_bf))
    kk = kx / jnp.maximum(kn, 1e-12)
    rvk_o[:, 2 * aw:] = kk.astype(rvk_o.dtype)

    xl_t = _bf(jnp.tanh(xl))
    xl_b = _bf(xl)
    kd_sum = None
    for d, (lw_o, d_o) in enumerate(((lw0_o, d0_o), (lw1_o, d1_o))):
        w_log = -_softplus(-(w0_ref[d:d + 1, :] + _dot(xl_t, w2_ref[d]))) - 0.5
        lw_o[...] = -jnp.exp(w_log)
        asig = _sigmoid(a0_ref[d:d + 1, :] + _dot(xl_b, a2_ref[d]))
        kd = k * (1.0 + (asig - 1.0) * ka_ref[...])
        d_o[:, :aw] = kd.astype(d_o.dtype)
        d_o[:, aw:] = (kk * asig).astype(d_o.dtype)
        kd_sum = kd if kd_sum is None else kd_sum + kd
    bonus_o[...] = _head_sum(r * rk_ref[...] * kd_sum, e_bf) * v


def _rwkv_prep(pa, mu_p, w0, a0, w2p, a2p, g2, k_k, k_a, r_k, e_bf, *, n_lat, seq, ctx_len):
    r = pa.shape[0]
    ncol = mu_p.shape[1]
    aw = w0.shape[1]
    tp = PREP_TILE
    hb = tp // SUBLANE
    nblk8 = r // SUBLANE
    lw = 2 * A_LORA_W + 2 * A_LORA_A
    row = lambda a: a.reshape(1, aw)
    full = lambda shape: pl.BlockSpec(shape, lambda i: (0,) * len(shape))
    outs = [(3 * aw, BF16), (2 * aw, BF16), (2 * aw, BF16), (aw, F32), (aw, F32), (aw, F32),
            (aw, F32)]
    return pl.pallas_call(
        functools.partial(_rwkv_prep_kernel, n_lat_tiles=n_lat // tp, lat_tiles_per_seq=seq // tp,
                          ctx_tiles_per_seq=ctx_len // tp, aw=aw),
        grid=(r // tp,),
        in_specs=[
            pl.BlockSpec((tp, ncol), lambda i: (i, 0)),
            pl.BlockSpec((SUBLANE, ncol), lambda i: (jnp.maximum(i * hb - 1, 0), 0)),
            pl.BlockSpec((SUBLANE, ncol), lambda i: (jnp.minimum((i + 1) * hb, nblk8 - 1), 0)),
            full((2, ncol)), full((2, aw)), full((2, aw)),
            full((2, lw, aw)), full((2, lw, aw)), full((A_LORA_G, aw)),
            full((1, aw)), full((1, aw)), full((1, aw)), full((GROUP_W, GROUP_W)),
        ],
        out_specs=[pl.BlockSpec((tp, w), lambda i: (i, 0)) for w, _ in outs],
        out_shape=[jax.ShapeDtypeStruct((r, w), dt) for w, dt in outs],
        compiler_params=_cparams(("parallel",)),
    )(pa, pa, pa, mu_p, w0, a0, w2p, a2p, g2, row(k_k), row(k_a), row(r_k), e_bf)


def _rwkv_masks(reverse):
    L = CHUNK
    gw = GROUP_W
    row = lax.broadcasted_iota(jnp.int32, (gw, gw), 0)
    col = lax.broadcasted_iota(jnp.int32, (gw, gw), 1)
    same_head = (row >> 6) == (col >> 6)
    t_n = lax.broadcasted_iota(jnp.int32, (L, gw), 0)
    s_n = lax.broadcasted_iota(jnp.int32, (L, gw), 1) & (L - 1)
    strict_n = (t_n < s_n) if reverse else (t_n > s_n)
    incl_n = (t_n <= s_n) if reverse else (t_n >= s_n)
    t_r = lax.broadcasted_iota(jnp.int32, (L, L), 0)
    t_c = lax.broadcasted_iota(jnp.int32, (L, L), 1)
    tri = _bf(jnp.where((t_c >= t_r) if reverse else (t_c <= t_r), 1.0, 0.0))
    return dict(same_head=same_head, strict_n=strict_n, eye_n=t_n == s_n,
                incl_n2=jnp.concatenate([incl_n, incl_n], axis=1), tri=tri)


def _rwkv_chunks(probs, side=None):
    L = CHUNK
    gw = GROUP_W
    tick = (lambda: None) if side is None else (lambda: next(side, None))
    masks = {rv: _rwkv_masks(rv) for rv in sorted({p[7] for p in probs})}
    mk = [masks[p[7]] for p in probs]
    each = lambda f, *cols: [f(*args) for args in zip(*cols)]
    rep = lambda x: jnp.concatenate([x] * HEADS_PER_GROUP, axis=0)

    c = [_dot_exact_lhs(m["tri"], p[3]) for m, p in zip(mk, probs)]
    c_l = [ci[(0 if p[7] else L - 1):(1 if p[7] else L), :] for ci, p in zip(c, probs)]

    blocks = [lambda x, sh=m["same_head"]: _bf(jnp.where(sh, rep(x), 0.0)) for m in mk]
    an, rn, vn, v_bd, bk_end, btk_bd, e_l = [], [], [], [], [], [], []
    for (r, v, kk, lw, kd, b, _, _), bd, ci, cl in zip(probs, blocks, c, c_l):
        e_nc = jnp.exp(-ci)
        e_end = jnp.exp(cl - ci)
        an.append(_bf(-kk * jnp.exp(ci - lw)))
        rn.append(_bf(r * jnp.exp(ci)))
        btk_bd.append(jnp.concatenate([bd(b * e_nc), bd(kd * e_nc)], axis=0))
        vn.append(_bf(v))
        v_bd.append(bd(v))
        bk_end.append(jnp.concatenate([_bf(b * e_end), _bf(kd * e_end)], axis=0))
        e_l.append(jnp.exp(cl))

    a_pow, t, a_ak, rbk = [], [], [], []
    for m, a_n, r_n, bk in zip(mk, an, rn, btk_bd):
        sn = _dot_nt(jnp.concatenate([a_n, r_n], axis=0), bk)
        a_ab = jnp.where(m["strict_n"], sn[:L, :gw], 0.0)
        a_pow.append(a_ab)
        t.append(jnp.where(m["eye_n"], 1.0, a_ab))
        a_ak.append(_bf(jnp.where(m["strict_n"], sn[:L, gw:], 0.0)))
        rbk.append(_bf(jnp.where(m["incl_n2"], sn[L:, :], 0.0)))
    tick()
    akv = each(_dot, a_ak, v_bd)
    a_bd = [bd(a) for bd, a in zip(blocks, a_pow)]
    tb = [_bf(ti) for ti in t]
    for _ in range(5):
        a_pow = [_dot(_bf(a), ab) for a, ab in zip(a_pow, a_bd)]
        a_bd = [bd(a) for bd, a in zip(blocks, a_pow)]
        tick()
        t = [ti + _dot(tbi, ab) for ti, tbi, ab in zip(t, tb, a_bd)]
        tb = [_bf(ti) for ti in t]
    h_bf = [_bf(p[6]) for p in probs]
    x_bd = [bd(_dot(a_n, h) + ak) for bd, a_n, h, ak in zip(blocks, an, h_bf, akv)]
    tick()
    u = [_dot(tbi, xi) for tbi, xi in zip(tb, x_bd)]
    tick()
    y = [_dot(r_n, h) + _dot(rbk_i, jnp.concatenate([bd(ui), v], axis=0))
         for bd, r_n, h, rbk_i, ui, v in zip(blocks, rn, h_bf, rbk, u, v_bd)]
    e_col = [jnp.broadcast_to(el, (gw, gw)).T for el in e_l]
    h_new = []
    for m, ec, p, bke, ui, v_n in zip(mk, e_col, probs, bk_end, u, vn):
        upd = _dot_tn(bke, jnp.concatenate([_bf(ui), v_n], axis=0))
        h_new.append(ec * p[6] + jnp.where(m["same_head"], upd, 0.0))
    if side is not None:
        for _ in side:
            pass
    return list(zip(y, h_new))


def _chunk_row_block(b, i, *, reverse, n_lat, seq, ctx_len):
    nc = ctx_len // CHUNK
    nl = seq // CHUNK
    ci = (nc - 1 - i) if reverse else i
    li = (nl - 1 - (i - nc)) if reverse else (i - nc)
    return jnp.where(i < nc, (n_lat + b * ctx_len) // CHUNK + ci, b * nl + li)


def _gla_stages(qkvf, gdf, qkvb, gdb, gw2_ref, gb_ref, of_ref, ob_ref, sf_ref, sb_ref):
    dv, dk = sf_ref.shape[1:]
    L = CHUNK
    wk = B_HEADS * dk
    t_r = lax.broadcasted_iota(jnp.int32, (L, L), 0)
    t_c = lax.broadcasted_iota(jnp.int32, (L, L), 1)

    qh, kh, keh, vh, dech, keeph, sth = [], [], [], [], [], [], []
    for d, (qkv_ref, gd_ref, st_ref) in enumerate(((qkvf, gdf, sf_ref), (qkvb, gdb, sb_ref))):
        reverse = d == 1
        keep = (t_c >= t_r) if reverse else (t_c <= t_r)
        tri = _bf(jnp.where(keep, 1.0, 0.0))
        z = _dot(_bf(gd_ref[...]), gw2_ref[d]) + gb_ref[d:d + 1, :]
        la = -_softplus(-z) * (1.0 / B_GATE_TAU)
        cum = _dot_exact_lhs(tri, la)
        last = 0 if reverse else L - 1
        cl = cum[last:last + 1, :]
        q_in = _bf(qkv_ref[:, :wk] * (dk ** -0.5) * jnp.exp(cum))
        k_in = _bf(qkv_ref[:, wk:2 * wk] * jnp.exp(-cum))
        k_end = _bf(qkv_ref[:, wk:2 * wk] * jnp.exp(cl - cum))
        dec = jnp.exp(cl)
        for h in range(B_HEADS):
            ks = slice(h * dk, (h + 1) * dk)
            qh.append(q_in[:, ks])
            kh.append(k_in[:, ks])
            keh.append(k_end[:, ks])
            vh.append(_bf(qkv_ref[:, 2 * wk + h * dv:2 * wk + (h + 1) * dv]))
            dech.append(dec[:, ks])
            keeph.append(keep)
            sth.append(st_ref[h])
        yield

    sc = [_bf(jnp.where(kp, _dot_nt(q, k), 0.0)) for kp, q, k in zip(keeph, qh, kh)]
    yield
    inter = [_dot_nt(q, _bf(st)) for q, st in zip(qh, sth)]
    yield
    kvt = [_dot_tn(v, ke) for v, ke in zip(vh, keh)]
    yield
    out = [_dot(s_, v) + it for s_, v, it in zip(sc, vh, inter)]
    yield
    for d, (o_ref, st_ref) in enumerate(((of_ref, sf_ref), (ob_ref, sb_ref))):
        for h in range(B_HEADS):
            i = d * B_HEADS + h
            o_ref[:, h * dv:(h + 1) * dv] = out[i]
            st_ref[h] = sth[i] * dech[i] + kvt[i]


def _mix_scan_kernel(rvkf, lwf, df, rvkb, lwb, db, qkvf, gdf, qkvb, gdb, gw2_ref, gb_ref,
                     yf_ref, yb_ref, of_ref, ob_ref, hf_ref, hb_ref, sf_ref, sb_ref):
    @pl.when(pl.program_id(1) == 0)
    def _():
        for ref in (hf_ref, hb_ref, sf_ref, sb_ref):
            ref[...] = jnp.zeros_like(ref)

    gs = hf_ref.shape[0]
    aw = lwf.shape[1]
    probs = []
    for gi in range(gs):
        sl = slice(gi * GROUP_W, (gi + 1) * GROUP_W)
        f32 = lambda ref, j: ref[:, j * aw + gi * GROUP_W:j * aw + (gi + 1) * GROUP_W].astype(F32)
        for rvk, lw, dd, h_ref, rev in ((rvkf, lwf, df, hf_ref, False), (rvkb, lwb, db, hb_ref, True)):
            probs.append((f32(rvk, 0), f32(rvk, 1), f32(rvk, 2), lw[:, sl], f32(dd, 0), f32(dd, 1),
                          h_ref[gi], rev))
    gla = _gla_stages(qkvf, gdf, qkvb, gdb, gw2_ref, gb_ref, of_ref, ob_ref, sf_ref, sb_ref)
    res = _rwkv_chunks(probs, side=gla)
    for gi in range(gs):
        sl = slice(gi * GROUP_W, (gi + 1) * GROUP_W)
        yf_ref[:, sl], hf_ref[gi] = res[2 * gi]
        yb_ref[:, sl], hb_ref[gi] = res[2 * gi + 1]


def _mix_scan(rvk, d0, d1, lw0, lw1, pb, gw2p, gb, *, batch, n_lat, seq, ctx_len, wk, wv, col0):
    rows, aw = lw0.shape
    assert aw == GROUP_W * RWKV_GROUPS_PER_STEP
    dk, dv = wk // B_HEADS, wv // B_HEADS
    qkv_w = 2 * wk + wv
    assert col0 % qkv_w == 0
    geo = dict(n_lat=n_lat, seq=seq, ctx_len=ctx_len)
    gd_blk = (col0 + 2 * wk + 2 * wv) // LANE

    def specs(reverse):
        rb = lambda b, i: _chunk_row_block(b, i, reverse=reverse, **geo)
        row_blk = lambda w, cb=0: pl.BlockSpec((CHUNK, w), lambda b, i: (rb(b, i), cb))
        rw = [row_blk(3 * aw), row_blk(aw), row_blk(2 * aw)]
        gla = [row_blk(qkv_w, col0 // qkv_w), row_blk(LANE, gd_blk)]
        return rw, gla, row_blk(aw), row_blk(wv)

    rwf, glaf, yf, of = specs(False)
    rwb, glab, yb, ob = specs(True)
    sds_a = jax.ShapeDtypeStruct((rows, aw), F32)
    sds_b = jax.ShapeDtypeStruct((rows, wv), F32)
    gs = RWKV_GROUPS_PER_STEP
    return pl.pallas_call(
        _mix_scan_kernel,
        grid=(batch, (seq + ctx_len) // CHUNK),
        in_specs=rwf + rwb + glaf + glab + [
            pl.BlockSpec((2, LANE, wk), lambda b, i: (0, 0, 0)),
            pl.BlockSpec((2, wk), lambda b, i: (0, 0)),
        ],
        out_specs=[yf, yb, of, ob],
        out_shape=[sds_a, sds_a, sds_b, sds_b],
        scratch_shapes=[pltpu.VMEM((gs, GROUP_W, GROUP_W), F32),
                        pltpu.VMEM((gs, GROUP_W, GROUP_W), F32),
                        pltpu.VMEM((B_HEADS, dv, dk), F32), pltpu.VMEM((B_HEADS, dv, dk), F32)],
        compiler_params=_cparams(("parallel", "arbitrary")),
    )(rvk, lw0, d0, rvk, lw1, d1, pb, pb, pb, pb, gw2p, gb)


def _mixer_out_kernel(yf, yb, bonus, g, of, ob, gate, xa_ref, xb_ref, mod_ref, lnw, lnb, gn, e_ref,
                      w_ref, o_ref, h_ref, *, aw, dv, n_a_tiles):
    e_bf = e_ref[...]
    y = yf[...] + yb[...]
    inv = 1.0 / A_HEAD_DIM
    mean = _head_sum(y, e_bf) * inv
    dlt = y - mean
    var = _head_sum(dlt * dlt, e_bf) * inv
    yn = dlt * lax.rsqrt(var + A_LN_EPS) * lnw[...] + lnb[...]
    h_ref[:, :aw] = _bf((yn + bonus[...]) * g[...])

    o = of[...] + ob[...]
    gt = _silu(gate[...])
    for h in range(B_HEADS):
        sl = slice(h * dv, (h + 1) * dv)
        oh = o[:, sl]
        ms = jnp.mean(oh * oh, axis=-1, keepdims=True)
        h_ref[:, aw + h * dv:aw + (h + 1) * dv] = _bf(oh * lax.rsqrt(ms + NORM_EPS) * gn[...] * gt[:, sl])

    x = jnp.where(pl.program_id(0) < n_a_tiles, xa_ref[...], xb_ref[...])
    o_ref[...] = x + mod_ref[2:3, :] * _dot(h_ref[...], w_ref[...])


def _mixer_out(yf, yb, bonus, g, of, ob, pb, xa, xb, mods_l, ln_w, ln_b, gla_norm, e_bf, w_bf,
               seg_of_tile, *, wk, col0):
    d = xa.shape[1]
    rows = xa.shape[0] + xb.shape[0]
    aw = yf.shape[1]
    wv = of.shape[1]
    dv = wv // B_HEADS
    tm = 256
    tiles_per_row_tile = ROW_TILE // tm
    n_a_tiles = xa.shape[0] // tm
    rs = lambda w: pl.BlockSpec((tm, w), lambda i: (i, 0))
    full = lambda shape: pl.BlockSpec(shape, lambda i: (0,) * len(shape))
    return pl.pallas_call(
        functools.partial(_mixer_out_kernel, aw=aw, dv=dv, n_a_tiles=n_a_tiles),
        grid=(rows // tm,),
        in_specs=[
            rs(aw), rs(aw), rs(aw), rs(aw), rs(wv), rs(wv),
            pl.BlockSpec((tm, wv), lambda i: (i, (col0 + 2 * wk + wv) // wv)),
        ] + _two_source_specs(tm, d, n_a_tiles) + [
            pl.BlockSpec((None, N_MOD, d), lambda i: (seg_of_tile(i // tiles_per_row_tile), 0, 0)),
            full((1, aw)), full((1, aw)), full((1, dv)), full((GROUP_W, GROUP_W)),
            full((aw + wv, d)),
        ],
        out_specs=rs(d),
        out_shape=jax.ShapeDtypeStruct((rows, d), F32),
        scratch_shapes=[pltpu.VMEM((tm, aw + wv), BF16)],
        compiler_params=_cparams(("parallel",)),
    )(yf, yb, bonus, g, of, ob, pb, xa, xb, mods_l, ln_w.reshape(1, aw), ln_b.reshape(1, aw),
      gla_norm.reshape(1, dv), e_bf, w_bf)


def _lru_kernel(*refs, reverse, n_blk, combine, cols):
    if combine:
        (x_ref, xp_ref, xn_ref, h0_ref, cw_ref, cb_ref, wc_ref, ba_ref, bx_ref, lam_ref,
         hf_ref, gate_ref, out_ref, hl_ref, a_sc, u_sc, hs_sc, carry_sc) = refs
    else:
        (x_ref, xp_ref, xn_ref, h0_ref, cw_ref, cb_ref, wc_ref, ba_ref, bx_ref, lam_ref,
         out_ref, hl_ref, a_sc, u_sc, hs_sc, carry_sc) = refs
    j = pl.program_id(2)
    blk = (n_blk - 1 - j) if reverse else j
    lb = x_ref.shape[0]
    ct = cw_ref.shape[1]
    cblk = wc_ref.shape[1]
    rowid = lax.broadcasted_iota(jnp.int32, (lb, 1), 0)
    decay_rate = -C_CONST * _softplus(-lam_ref[...])

    @pl.when(j == 0)
    def _():
        carry_sc[...] = h0_ref[0:1, :]

    for q in (range(cols - 1, -1, -1) if reverse else range(cols)):
        lanes = slice(q * ct, (q + 1) * ct)
        x = x_ref[:, lanes]
        if q > 0:
            prev = slice((q - 1) * ct, q * ct)
            p6, p7 = x_ref[lb - 2:lb - 1, prev], x_ref[lb - 1:lb, prev]
        else:
            p6 = jnp.where(blk == 0, 0.0, xp_ref[SUBLANE - 2:SUBLANE - 1, :])
            p7 = jnp.where(blk == 0, 0.0, xp_ref[SUBLANE - 1:SUBLANE, :])
        if q < cols - 1:
            n0 = x_ref[0:1, (q + 1) * ct:(q + 2) * ct]
        else:
            n0 = jnp.where(blk == n_blk - 1, 0.0, xn_ref[0:1, :])
        xm1 = jnp.where(rowid == 0, p7, pltpu.roll(x, 1, 0))
        xm2 = jnp.where(rowid == 0, p6, jnp.where(rowid == 1, p7, pltpu.roll(x, 2, 0)))
        xp1 = jnp.where(rowid == lb - 1, n0, pltpu.roll(x, lb - 1, 0))
        xs = (cb_ref[...] + cw_ref[0:1, :] * xm2 + cw_ref[1:2, :] * xm1 + cw_ref[2:3, :] * x
              + cw_ref[3:4, :] * xp1)

        for n in range(ct // cblk):
            sl = slice(n * cblk, (n + 1) * cblk)
            xb = xs[:, sl]
            ri = _dot(_bf(xb), wc_ref[n])
            rg = _sigmoid(ri[:, :cblk] + ba_ref[:, sl])
            ig = _sigmoid(ri[:, cblk:] + bx_ref[:, sl])
            a = jnp.exp(decay_rate[:, sl] * rg)
            a_sc[q, :, sl] = a
            u_sc[q, :, sl] = jnp.sqrt(1.0 - a * a) * (ig * xb)

        def step(t, h, q=q):
            tt = (lb - 1 - t) if reverse else t
            h = a_sc[q, pl.ds(tt, 1), :] * h + u_sc[q, pl.ds(tt, 1), :]
            hs_sc[q, pl.ds(tt, 1), :] = h
            return h

        carry_sc[...] = lax.fori_loop(0, lb, step, carry_sc[...], unroll=8)
        if combine:
            out_ref[:, lanes] = ((hs_sc[q] + hf_ref[q * lb:(q + 1) * lb, :])
                                 * jax.nn.gelu(gate_ref[:, lanes]))
        else:
            out_ref[q * lb:(q + 1) * lb, :] = hs_sc[q]
    hl_ref[...] = jnp.broadcast_to(carry_sc[...], hl_ref.shape)


def _lru_pass(x_arr, x_specs, h0, conv_w, conv_b, wcat, ba, bx, lam, out_shape, out_spec, extra,
              *, batch, n_blk, lb, cols, reverse):
    c = conv_w.shape[1]
    ct = LRU_CHANNEL_TILE
    nct = c // ct
    cblk = wcat.shape[1]
    chan = lambda rows_: pl.BlockSpec((rows_, ct), lambda b, k, j: (0, k))
    in_specs = list(x_specs) + [
        pl.BlockSpec((SUBLANE, ct), lambda b, k, j: (b, k)),
        chan(C_CONV), chan(1),
        pl.BlockSpec((ct // cblk, cblk, 2 * cblk), lambda b, k, j: (k, 0, 0)),
        chan(1), chan(1), chan(1),
    ] + [s for _, s in extra]
    args = [x_arr, x_arr, x_arr, h0, conv_w, conv_b.reshape(1, c), wcat, ba.reshape(1, c),
            bx.reshape(1, c), lam.reshape(1, c)] + [a for a, _ in extra]
    return pl.pallas_call(
        functools.partial(_lru_kernel, reverse=reverse, n_blk=n_blk, combine=bool(extra),
                          cols=cols),
        grid=(batch, nct, n_blk),
        in_specs=in_specs,
        out_specs=[out_spec, pl.BlockSpec((SUBLANE, ct), lambda b, k, j: (b, k))],
        out_shape=[out_shape, jax.ShapeDtypeStruct((batch * SUBLANE, c), F32)],
        scratch_shapes=[pltpu.VMEM((cols, lb, ct), F32), pltpu.VMEM((cols, lb, ct), F32),
                        pltpu.VMEM((cols, lb, ct), F32), pltpu.VMEM((1, ct), F32)],
        compiler_params=_cparams(("parallel", "parallel", "arbitrary")),
    )(*args)


def _rglru(hw_ctx, gate_grid, x_grid, conv_w, conv_b, wa, ba, wx, bx, lam, *, batch, n_lat, seq,
           ctx_len):
    c = conv_w.shape[1]
    ct = LRU_CHANNEL_TILE
    rows = seq // GRID_W
    lbc = 128
    ncb = ctx_len // lbc
    xcol = c // ct

    wcat = [_bf(jnp.concatenate([wa[d], wx[d]], axis=-1)) for d in range(2)]
    zeros_state = jnp.zeros((batch * SUBLANE, c), F32)

    def ctx_specs(reverse):
        blk = (lambda j: ncb - 1 - j) if reverse else (lambda j: j)
        per8 = lbc // SUBLANE
        cur = pl.BlockSpec((lbc, ct), lambda b, k, j: (b * ncb + blk(j), xcol + k))
        prv = pl.BlockSpec((SUBLANE, ct), lambda b, k, j: (
            b * ncb * per8 + jnp.maximum(blk(j) * per8 - 1, 0), xcol + k))
        nxt = pl.BlockSpec((SUBLANE, ct), lambda b, k, j: (
            b * ncb * per8 + jnp.minimum((blk(j) + 1) * per8, ncb * per8 - 1), xcol + k))
        return cur, prv, nxt

    assert ct == c
    kc = LRU_COLS_PER_STEP
    ncg = GRID_W // kc

    def lat_specs(reverse):
        grp = (lambda j: ncg - 1 - j) if reverse else (lambda j: j)
        r8 = rows // SUBLANE
        cur = pl.BlockSpec((rows, kc * ct), lambda b, k, j: (b, grp(j)))
        prv = pl.BlockSpec((SUBLANE, ct), lambda b, k, j: (
            b * r8 + r8 - 1, jnp.maximum(grp(j) * kc - 1, 0)))
        nxt = pl.BlockSpec((SUBLANE, ct), lambda b, k, j: (
            b * r8, jnp.minimum(grp(j) * kc + kc, GRID_W - 1)))
        hf = pl.BlockSpec((kc * rows, ct), lambda b, k, j: (b * ncg + grp(j), 0))
        return (cur, prv, nxt), hf, cur

    common = dict(batch=batch)
    ctx_scratch_sds = jax.ShapeDtypeStruct((batch * ctx_len, c), F32)
    ctx_out = lambda reverse: pl.BlockSpec(
        (lbc, ct), lambda b, k, j: (b * ncb + ((ncb - 1 - j) if reverse else j), k))
    _, st = _lru_pass(hw_ctx, ctx_specs(False), zeros_state, conv_w, conv_b, wcat[0], ba[0], bx[0],
                      lam[0], ctx_scratch_sds, ctx_out(False), [], n_blk=ncb, lb=lbc, cols=1,
                      reverse=False, **common)
    hf_sds = jax.ShapeDtypeStruct((n_lat, c), F32)
    x_specs, hf_spec, _ = lat_specs(False)
    hf, _ = _lru_pass(x_grid, x_specs, st, conv_w, conv_b, wcat[0], ba[0], bx[0], lam[0],
                      hf_sds, hf_spec, [], n_blk=ncg, lb=rows, cols=kc, reverse=False, **common)
    _, st = _lru_pass(hw_ctx, ctx_specs(True), zeros_state, conv_w, conv_b, wcat[1], ba[1], bx[1],
                      lam[1], ctx_scratch_sds, ctx_out(True), [], n_blk=ncb, lb=lbc, cols=1,
                      reverse=True, **common)
    x_specs, hf_spec, grid_spec = lat_specs(True)
    y_sds = jax.ShapeDtypeStruct((n_lat // GRID_W, GRID_W * c), F32)
    y, _ = _lru_pass(x_grid, x_specs, st, conv_w, conv_b, wcat[1], ba[1], bx[1], lam[1],
                     y_sds, grid_spec, [(hf, hf_spec), (gate_grid, grid_spec)],
                     n_blk=ncg, lb=rows, cols=kc, reverse=True, **common)
    return y


def _pad_cols(w, n):
    return jnp.pad(w, ((0, 0),) * (w.ndim - 1) + ((0, n - w.shape[-1]),))


def kernel(x, c, ctx, c_ctx, mod_w, mod_b, norm1, norm2, mlp_w1, mlp_w2, ab_w_in, ab_w_out, rw_mu,
           rw_w0, rw_w2, rw_a0, rw_a2, rw_g2, rw_kk, rw_ka, rw_rk, rw_ln_w, rw_ln_b, gla_gw2,
           gla_gb, gla_norm, lru_w_in, lru_w_out, lru_conv_w, lru_conv_b, lru_wa, lru_ba, lru_wx,
           lru_bx, lru_lam, final_norm):
    batch, seq, d = x.shape
    ctx_len = ctx.shape[1]
    depth = mod_w.shape[0]
    n_lat = batch * seq
    aw = rw_w0.shape[-1]
    wk = gla_gb.shape[-1]
    wv = ab_w_out.shape[1] - aw
    a_cols = rw_mu.shape[-1]
    b_cols = ab_w_in.shape[-1] - a_cols
    assert depth == 2 and batch + 1 <= SUBLANE
    assert seq % ROW_TILE == 0 and (batch * ctx_len) % ROW_TILE == 0
    assert ctx_len % PREP_TILE == 0 and ctx_len % 128 == 0 and seq % (GRID_W * SUBLANE) == 0
    assert aw % GROUP_W == 0 and wv == aw and 2 * wk == wv

    tiles_per_seq = seq // ROW_TILE
    seg_of_tile = lambda i: jnp.minimum(i // tiles_per_seq, batch)

    x2 = x.reshape(n_lat, d)
    ctx2 = ctx.reshape(batch * ctx_len, d)
    cvec = jnp.concatenate(
        [c, c_ctx[None, :], jnp.zeros((SUBLANE - batch - 1, d), F32)], axis=0)
    mods = _mods(cvec, mod_w, mod_b)

    hid = jnp.arange(GROUP_W) // A_HEAD_DIM
    e_bf = _bf(hid[:, None] == hid[None, :])

    a_pad = -(-a_cols // LANE) * LANE
    col0 = -(-a_cols // (2 * wk)) * (2 * wk)
    n_tiles = 3
    tn = -(-(col0 + b_cols) // (n_tiles * LANE)) * LANE
    w_ab = _bf(jnp.concatenate(
        [_pad_cols(ab_w_in[0][:, :a_cols], col0),
         _pad_cols(ab_w_in[0][:, a_cols:], n_tiles * tn - col0)], axis=1))
    pa = pb = _norm_matmul(x2, ctx2, norm1[0], mods[0], w_ab, seg_of_tile, tn=tn)

    lora_w = 2 * A_LORA_W + 2 * A_LORA_A
    w2p = jnp.zeros((2, lora_w, aw), F32)
    a2p = jnp.zeros((2, lora_w, aw), F32)
    for dd in range(2):
        w2p = w2p.at[dd, dd * A_LORA_W:(dd + 1) * A_LORA_W].set(rw_w2[0, dd])
        a2p = a2p.at[dd, 2 * A_LORA_W + dd * A_LORA_A:2 * A_LORA_W + (dd + 1) * A_LORA_A].set(
            rw_a2[0, dd])
    (rvk, d0, d1, lw0, lw1, g_, bonus) = _rwkv_prep(
        pa, _pad_cols(rw_mu[0], a_pad), rw_w0[0], rw_a0[0], _bf(w2p), _bf(a2p), _bf(rw_g2[0]),
        rw_kk[0], rw_ka[0], rw_rk[0].reshape(-1), e_bf, n_lat=n_lat, seq=seq, ctx_len=ctx_len)
    gw2p = jnp.zeros((2, LANE, wk), F32)
    for dd in range(2):
        gw2p = gw2p.at[dd, dd * B_LORA:(dd + 1) * B_LORA].set(gla_gw2[0, dd])
    yf, yb, of, ob = _mix_scan(rvk, d0, d1, lw0, lw1, pb, _bf(gw2p), gla_gb[0],
                               batch=batch, n_lat=n_lat, seq=seq, ctx_len=ctx_len, wk=wk, wv=wv,
                               col0=col0)

    xs = _mixer_out(yf, yb, bonus, g_, of, ob, pb, x2, ctx2, mods[0], rw_ln_w[0], rw_ln_b[0],
                    gla_norm[0], e_bf, _bf(ab_w_out[0]), seg_of_tile, wk=wk, col0=col0)
    seg_of_mlp_tile = lambda i: jnp.minimum(i // (seq // MLP_TILE), batch)
    ctx_tiles = (batch * ctx_len) // ROW_TILE
    w1_bf, w2_bf = _bf(mlp_w1[0]), _bf(mlp_w2[0])
    x_lat = _mlp(xs, norm2[0], mods[0], w1_bf, w2_bf, final_norm, seg_of_mlp_tile, tm=MLP_TILE,
                 tile0=0, n_tiles=n_lat // MLP_TILE, final_norm=False)
    x_ctx = _mlp(xs, norm2[0], mods[0], w1_bf, w2_bf, final_norm, seg_of_tile, tm=ROW_TILE,
                 tile0=n_lat // ROW_TILE, n_tiles=ctx_tiles, final_norm=False)

    w_lru = _bf(lru_w_in[0])
    hw_ctx = _norm_matmul(x_ctx, None, norm1[1], mods[1], w_lru, lambda i: batch, tn=2048)
    gate_grid, x_grid = _norm_matmul_grid(x_lat, n_lat, seq // GRID_W, norm1[1], mods[1], w_lru)
    y = _rglru(hw_ctx, gate_grid, x_grid, lru_conv_w[0], lru_conv_b[0], lru_wa[0], lru_ba[0],
               lru_wx[0], lru_bx[0], lru_lam[0], batch=batch, n_lat=n_lat, seq=seq,
               ctx_len=ctx_len)
    xl = _proj_res(y, x_lat, n_lat, seq // GRID_W, mods[1], _bf(lru_w_out[0]))
    out = _mlp(xl, norm2[1], mods[1], _bf(mlp_w1[1]), _bf(mlp_w2[1]), final_norm, seg_of_mlp_tile,
               tm=MLP_TILE, tile0=0, n_tiles=n_lat // MLP_TILE, final_norm=True)
    return out.reshape(batch, seq, d)
```

```python
import functools

import jax
import jax.numpy as jnp
from jax import lax
from jax.experimental import pallas as pl
from jax.experimental.pallas import tpu as pltpu

F32 = jnp.float32
BF16 = jnp.bfloat16

NORM_EPS = 1e-6
GRID_W = 64
N_MOD = 6

A_HEAD_DIM = 64
A_LORA_W = 96
A_LORA_A = 96
A_LORA_G = 256
A_LN_EPS = 64e-5
HEADS_PER_GROUP = 4
GROUP_W = HEADS_PER_GROUP * A_HEAD_DIM
CHUNK = 64
RWKV_GROUPS_PER_STEP = 4

B_HEADS = 4
B_LORA = 16
B_GATE_TAU = 16.0

C_BLOCKS = 8
C_CONV = 4
C_CONST = 8.0

LANE = 128
SUBLANE = 8
ROW_TILE = 512
MLP_TILE = 1024
MLP_STEP_ELEMS = 1024 * 512
LRU_CHANNEL_TILE = 2048
LRU_COLS_PER_STEP = 4
PREP_TILE = 128
VMEM_LIMIT = 56 * 1024 * 1024


def _cparams(sem):
    return pltpu.CompilerParams(dimension_semantics=sem, vmem_limit_bytes=VMEM_LIMIT)


def _bf(x):
    return x.astype(BF16)


def _dot(a, b):
    return jnp.dot(a, b, preferred_element_type=F32)


def _dot_nt(a, b):
    return lax.dot_general(a, b, (((1,), (1,)), ((), ())), preferred_element_type=F32)


def _dot_tn(a, b):
    return lax.dot_general(a, b, (((0,), (0,)), ((), ())), preferred_element_type=F32)


def _split3(x):
    hi = _bf(x)
    r1 = x - hi.astype(F32)
    mid = _bf(r1)
    lo = _bf(r1 - mid.astype(F32))
    return hi, mid, lo


def _dot_exact_lhs(m_bf, x):
    hi, mid, lo = _split3(x)
    return _dot(m_bf, hi) + _dot(m_bf, mid) + _dot(m_bf, lo)


def _dot_exact_rhs(x, m_bf):
    hi, mid, lo = _split3(x)
    return _dot(hi, m_bf) + _dot(mid, m_bf) + _dot(lo, m_bf)


def _softplus(x):
    return jnp.maximum(x, 0.0) + jnp.log(1.0 + jnp.exp(-jnp.abs(x)))


def _sigmoid(x):
    return 0.5 * jnp.tanh(0.5 * x) + 0.5


def _silu(x):
    return x * _sigmoid(x)


def _norm_mod(x, g, shift, scale):
    ms = jnp.mean(x * x, axis=-1, keepdims=True)
    return (x * lax.rsqrt(ms + NORM_EPS) * g) * (1.0 + scale) + shift


def _mods_kernel(c_ref, w_ref, b_ref, o_ref):
    s = _silu(c_ref[...])
    o_ref[...] = _dot(_bf(s), _bf(w_ref[...])) + b_ref[...]


def _mods(cvec, mod_w, mod_b):
    depth, d, n = mod_w.shape
    tn = 1024
    out = pl.pallas_call(
        _mods_kernel,
        grid=(depth, n // tn),
        in_specs=[
            pl.BlockSpec((SUBLANE, d), lambda l, j: (0, 0)),
            pl.BlockSpec((None, d, tn), lambda l, j: (l, 0, j)),
            pl.BlockSpec((None, 1, tn), lambda l, j: (l, 0, j)),
        ],
        out_specs=pl.BlockSpec((None, SUBLANE, tn), lambda l, j: (l, 0, j)),
        out_shape=jax.ShapeDtypeStruct((depth, SUBLANE, n), F32),
        compiler_params=_cparams(("parallel", "parallel")),
    )(cvec, mod_w, mod_b.reshape(depth, 1, n))
    return out.reshape(depth, SUBLANE, N_MOD, d)


def _norm_matmul_kernel(xa_ref, xb_ref, g_ref, mod_ref, w_ref, o_ref, h_ref, *, n_a_tiles):
    i = pl.program_id(0)

    @pl.when(pl.program_id(1) == 0)
    def _():
        x = jnp.where(i < n_a_tiles, xa_ref[...], xb_ref[...])
        h_ref[...] = _bf(_norm_mod(x, g_ref[...], mod_ref[0:1, :], mod_ref[1:2, :]))

    o_ref[...] = _dot(h_ref[...], w_ref[...])


def _two_source_specs(tm, d, n_a_tiles):
    return [pl.BlockSpec((tm, d), lambda i, *_: (jnp.minimum(i, n_a_tiles - 1), 0)),
            pl.BlockSpec((tm, d), lambda i, *_: (jnp.maximum(i - n_a_tiles, 0), 0))]


def _norm_matmul(xa, xb, g, mods_l, w_bf, seg_of_tile, *, tn):
    d = xa.shape[1]
    n = w_bf.shape[1]
    tm = ROW_TILE
    n_a_tiles = xa.shape[0] // tm
    n_tiles = n_a_tiles + (0 if xb is None else xb.shape[0] // tm)
    return pl.pallas_call(
        functools.partial(_norm_matmul_kernel, n_a_tiles=n_a_tiles),
        grid=(n_tiles, n // tn),
        in_specs=_two_source_specs(tm, d, n_a_tiles) + [
            pl.BlockSpec((1, d), lambda i, j: (0, 0)),
            pl.BlockSpec((None, N_MOD, d), lambda i, j: (seg_of_tile(i), 0, 0)),
            pl.BlockSpec((d, tn), lambda i, j: (0, j)),
        ],
        out_specs=pl.BlockSpec((tm, tn), lambda i, j: (i, j)),
        out_shape=jax.ShapeDtypeStruct((n_tiles * tm, n), F32),
        scratch_shapes=[pltpu.VMEM((tm, d), BF16)],
        compiler_params=_cparams(("parallel", "arbitrary")),
    )(xa, xa if xb is None else xb, g.reshape(1, d), mods_l, w_bf)


def _grid_tile(rows):
    gr = min(rows, 64)
    gc = ROW_TILE // gr
    assert gc % SUBLANE == 0 and GRID_W % gc == 0 and rows % gr == 0
    return gr, gc


def _norm_matmul_grid_kernel(x_ref, g_ref, mod_ref, w_ref, o0_ref, o1_ref, h_ref, res_ref):
    j = pl.program_id(1)
    gr, gc, d = x_ref.shape

    @pl.when(j == 0)
    def _():
        h = _norm_mod(x_ref[...].reshape(gr * gc, d), g_ref[...], mod_ref[0:1, :], mod_ref[1:2, :])
        h_ref[...] = _bf(h)

    res = _dot(h_ref[...], w_ref[...])
    nh = w_ref.shape[1]
    nq = nh // LANE
    for q in range(nq):
        res_ref[q] = res[:, q * LANE:(q + 1) * LANE]

    def scatter(o_ref):
        for c in range(gc):
            for q in range(nq):
                o_ref[:, c * nh + q * LANE:c * nh + (q + 1) * LANE] = (
                    res_ref[q, pl.ds(c, gr, stride=gc), :])

    @pl.when(j == 0)
    def _():
        scatter(o0_ref)

    @pl.when(j == 1)
    def _():
        scatter(o1_ref)


def _norm_matmul_grid(x, n_lat, rows, g, mods_l, w_bf):
    d = x.shape[1]
    nh = w_bf.shape[1] // 2
    gr, gc = _grid_tile(rows)
    ncg = GRID_W // gc
    x3 = x.reshape(x.shape[0] // GRID_W, GRID_W, d)
    out_spec = pl.BlockSpec((gr, gc * nh), lambda i, j: (i // ncg, i % ncg))
    sds = jax.ShapeDtypeStruct((n_lat // GRID_W, GRID_W * nh), F32)
    return pl.pallas_call(
        _norm_matmul_grid_kernel,
        grid=(n_lat // ROW_TILE, 2),
        in_specs=[
            pl.BlockSpec((gr, gc, d), lambda i, j: (i // ncg, i % ncg, 0)),
            pl.BlockSpec((1, d), lambda i, j: (0, 0)),
            pl.BlockSpec((None, N_MOD, d), lambda i, j: ((i // ncg) * gr // rows, 0, 0)),
            pl.BlockSpec((d, nh), lambda i, j: (0, j)),
        ],
        out_specs=[out_spec, out_spec],
        out_shape=[sds, sds],
        scratch_shapes=[pltpu.VMEM((ROW_TILE, d), BF16),
                        pltpu.VMEM((nh // LANE, ROW_TILE, LANE), F32)],
        compiler_params=_cparams(("parallel", "arbitrary")),
    )(x3, g.reshape(1, d), mods_l, w_bf)


def _mlp_kernel(x_ref, g_ref, mod_ref, w1_ref, w2_ref, fn_ref, o_ref, h_ref, *, final_norm):
    j = pl.program_id(1)
    slab = 256
    n_slab = x_ref.shape[0] // slab

    @pl.when(j == 0)
    def _():
        for s in range(n_slab):
            rs = slice(s * slab, (s + 1) * slab)
            h_ref[rs, :] = _bf(_norm_mod(x_ref[rs, :], g_ref[...], mod_ref[3:4, :],
                                         mod_ref[4:5, :]))
        o_ref[...] = jnp.zeros_like(o_ref)

    a = _dot(h_ref[...], w1_ref[...])
    a = _bf(jnp.square(jnp.maximum(a, 0.0)))
    nsplit = 2
    wn = o_ref.shape[1] // nsplit
    for s in range(nsplit):
        o_ref[:, s * wn:(s + 1) * wn] += _dot(a, w2_ref[:, s * wn:(s + 1) * wn])

    @pl.when(j == pl.num_programs(1) - 1)
    def _():
        for s in range(n_slab):
            rs = slice(s * slab, (s + 1) * slab)
            xo = x_ref[rs, :] + mod_ref[5:6, :] * o_ref[rs, :]
            if final_norm:
                ms = jnp.mean(xo * xo, axis=-1, keepdims=True)
                xo = xo * lax.rsqrt(ms + NORM_EPS) * fn_ref[...]
            o_ref[rs, :] = xo


def _mlp(x, g, mods_l, w1_bf, w2_bf, fn, seg_of_tile, *, tm, tile0, n_tiles, final_norm):
    d = x.shape[1]
    dff = w1_bf.shape[1]
    tf = MLP_STEP_ELEMS // tm
    return pl.pallas_call(
        functools.partial(_mlp_kernel, final_norm=final_norm),
        grid=(n_tiles, dff // tf),
        in_specs=[
            pl.BlockSpec((tm, d), lambda i, j: (i + tile0, 0), pipeline_mode=pl.Buffered(1)),
            pl.BlockSpec((1, d), lambda i, j: (0, 0)),
            pl.BlockSpec((None, N_MOD, d), lambda i, j: (seg_of_tile(i + tile0), 0, 0)),
            pl.BlockSpec((d, tf), lambda i, j: (0, j)),
            pl.BlockSpec((tf, d), lambda i, j: (j, 0)),
            pl.BlockSpec((1, d), lambda i, j: (0, 0)),
        ],
        out_specs=pl.BlockSpec((tm, d), lambda i, j: (i, 0)),
        out_shape=jax.ShapeDtypeStruct((n_tiles * tm, d), F32),
        scratch_shapes=[pltpu.VMEM((tm, d), BF16)],
        compiler_params=_cparams(("parallel", "arbitrary")),
    )(x, g.reshape(1, d), mods_l, w1_bf, w2_bf, fn.reshape(1, d))


def _proj_res_kernel(y_ref, x_ref, mod_ref, w_ref, o_ref, yt_ref):
    k = w_ref.shape[0]
    gr, gc, d = x_ref.shape
    nq = k // LANE
    for c in range(gc):
        for q in range(nq):
            yt_ref[q, pl.ds(c, gr, stride=gc), :] = (
                y_ref[:, c * k + q * LANE:c * k + (q + 1) * LANE])
    yt = jnp.concatenate([_bf(yt_ref[q]) for q in range(nq)], axis=1)
    out = x_ref[...].reshape(gr * gc, d) + mod_ref[2:3, :] * _dot(yt, w_ref[...])
    o_ref[...] = out.reshape(gr, gc, d)


def _proj_res(y_grid, x, n_lat, rows, mods_l, w_bf):
    k, d = w_bf.shape
    gr, gc = _grid_tile(rows)
    ncg = GRID_W // gc
    x3 = x.reshape(x.shape[0] // GRID_W, GRID_W, d)
    tok_spec = pl.BlockSpec((gr, gc, d), lambda i: (i // ncg, i % ncg, 0))
    out = pl.pallas_call(
        _proj_res_kernel,
        grid=(n_lat // ROW_TILE,),
        in_specs=[
            pl.BlockSpec((gr, gc * k), lambda i: (i // ncg, i % ncg)),
            tok_spec,
            pl.BlockSpec((None, N_MOD, d), lambda i: ((i // ncg) * gr // rows, 0, 0)),
            pl.BlockSpec((k, d), lambda i: (0, 0)),
        ],
        out_specs=tok_spec,
        out_shape=jax.ShapeDtypeStruct((n_lat // GRID_W, GRID_W, d), F32),
        scratch_shapes=[pltpu.VMEM((k // LANE, ROW_TILE, LANE), F32)],
        compiler_params=_cparams(("parallel",)),
    )(y_grid, x3, mods_l, w_bf)
    return out.reshape(n_lat, d)


def _head_sum(z, e_bf):
    parts = []
    for gi in range(z.shape[1] // GROUP_W):
        zs = z[:, gi * GROUP_W:(gi + 1) * GROUP_W]
        parts.append(_dot_exact_rhs(zs, e_bf))
    return jnp.concatenate(parts, axis=1)


def _rwkv_prep_kernel(p_ref, pp_ref, pn_ref, mu_ref, w0_ref, a0_ref, w2_ref, a2_ref, g2_ref,
                      kk_ref, ka_ref, rk_ref, e_ref,
                      rvk_o, d0_o, d1_o, lw0_o, lw1_o, g_o, bonus_o,
                      *, n_lat_tiles, lat_tiles_per_seq, ctx_tiles_per_seq, aw):
    i = pl.program_id(0)
    tp = p_ref.shape[0]
    in_ctx = i >= n_lat_tiles
    pos = jnp.where(in_ctx, (i - n_lat_tiles) % ctx_tiles_per_seq, i % lat_tiles_per_seq)
    last = jnp.where(in_ctx, ctx_tiles_per_seq - 1, lat_tiles_per_seq - 1)
    is_first = pos == 0
    is_last = pos == last
    rowid = lax.broadcasted_iota(jnp.int32, (tp, 1), 0)

    def shifted(c0, c1):
        x = p_ref[:, c0:c1]
        prow = jnp.where(is_first, 0.0, pp_ref[SUBLANE - 1:SUBLANE, c0:c1])
        nrow = jnp.where(is_last, 0.0, pn_ref[0:1, c0:c1])
        xp = jnp.where(rowid == 0, prow, pltpu.roll(x, 1, 0))
        xn = jnp.where(rowid == tp - 1, nrow, pltpu.roll(x, tp - 1, 0))
        return x + mu_ref[0:1, c0:c1] * (xp - x) + mu_ref[1:2, c0:c1] * (xn - x)

    e_bf = e_ref[...]
    r = shifted(0, aw)
    k = shifted(aw, 2 * aw)
    v = shifted(2 * aw, 3 * aw)
    gd = shifted(3 * aw, 3 * aw + A_LORA_G)
    lo0 = 3 * aw + A_LORA_G
    xl = shifted(lo0, lo0 + 2 * A_LORA_W + 2 * A_LORA_A)

    rvk_o[:, :aw] = r.astype(rvk_o.dtype)
    rvk_o[:, aw:2 * aw] = v.astype(rvk_o.dtype)
    g_o[...] = _dot(_bf(_sigmoid(gd)), g2_ref[...])

    kx = k * kk_ref[...]
    kn = jnp.sqrt(_head_sum(kx * kx, e_bf))
    kk = kx / jnp.maximum(kn, 1e-12)
    rvk_o[:, 2 * aw:] = kk.astype(rvk_o.dtype)

    xl_t = _bf(jnp.tanh(xl))
    xl_b = _bf(xl)
    kd_sum = None
    for d, (lw_o, d_o) in enumerate(((lw0_o, d0_o), (lw1_o, d1_o))):
        w_log = -_softplus(-(w0_ref[d:d + 1, :] + _dot(xl_t, w2_ref[d]))) - 0.5
        lw_o[...] = -jnp.exp(w_log)
        asig = _sigmoid(a0_ref[d:d + 1, :] + _dot(xl_b, a2_ref[d]))
        kd = k * (1.0 + (asig - 1.0) * ka_ref[...])
        d_o[:, :aw] = kd.astype(d_o.dtype)
        d_o[:, aw:] = (kk * asig).astype(d_o.dtype)
        kd_sum = kd if kd_sum is None else kd_sum + kd
    bonus_o[...] = _head_sum(r * rk_ref[...] * kd_sum, e_bf) * v


def _rwkv_prep(pa, mu_p, w0, a0, w2p, a2p, g2, k_k, k_a, r_k, e_bf, *, n_lat, seq, ctx_len):
    r = pa.shape[0]
    ncol = mu_p.shape[1]
    aw = w0.shape[1]
    tp = PREP_TILE
    hb = tp // SUBLANE
    nblk8 = r // SUBLANE
    lw = 2 * A_LORA_W + 2 * A_LORA_A
    row = lambda a: a.reshape(1, aw)
    full = lambda shape: pl.BlockSpec(shape, lambda i: (0,) * len(shape))
    outs = [(3 * aw, BF16), (2 * aw, BF16), (2 * aw, BF16), (aw, F32), (aw, F32), (aw, F32),
            (aw, F32)]
    return pl.pallas_call(
        functools.partial(_rwkv_prep_kernel, n_lat_tiles=n_lat // tp, lat_tiles_per_seq=seq // tp,
                          ctx_tiles_per_seq=ctx_len // tp, aw=aw),
        grid=(r // tp,),
        in_specs=[
            pl.BlockSpec((tp, ncol), lambda i: (i, 0)),
            pl.BlockSpec((SUBLANE, ncol), lambda i: (jnp.maximum(i * hb - 1, 0), 0)),
            pl.BlockSpec((SUBLANE, ncol), lambda i: (jnp.minimum((i + 1) * hb, nblk8 - 1), 0)),
            full((2, ncol)), full((2, aw)), full((2, aw)),
            full((2, lw, aw)), full((2, lw, aw)), full((A_LORA_G, aw)),
            full((1, aw)), full((1, aw)), full((1, aw)), full((GROUP_W, GROUP_W)),
        ],
        out_specs=[pl.BlockSpec((tp, w), lambda i: (i, 0)) for w, _ in outs],
        out_shape=[jax.ShapeDtypeStruct((r, w), dt) for w, dt in outs],
        compiler_params=_cparams(("parallel",)),
    )(pa, pa, pa, mu_p, w0, a0, w2p, a2p, g2, row(k_k), row(k_a), row(r_k), e_bf)


def _rwkv_masks(reverse):
    L = CHUNK
    gw = GROUP_W
    row = lax.broadcasted_iota(jnp.int32, (gw, gw), 0)
    col = lax.broadcasted_iota(jnp.int32, (gw, gw), 1)
    same_head = (row >> 6) == (col >> 6)
    t_n = lax.broadcasted_iota(jnp.int32, (L, gw), 0)
    s_n = lax.broadcasted_iota(jnp.int32, (L, gw), 1) & (L - 1)
    strict_n = (t_n < s_n) if reverse else (t_n > s_n)
    incl_n = (t_n <= s_n) if reverse else (t_n >= s_n)
    t_r = lax.broadcasted_iota(jnp.int32, (L, L), 0)
    t_c = lax.broadcasted_iota(jnp.int32, (L, L), 1)
    tri = _bf(jnp.where((t_c >= t_r) if reverse else (t_c <= t_r), 1.0, 0.0))
    return dict(same_head=same_head, strict_n=strict_n, incl_n=incl_n, eye_n=t_n == s_n, tri=tri)


def _rwkv_chunks(probs, side=None):
    L = CHUNK
    gw = GROUP_W
    tick = (lambda: None) if side is None else (lambda: next(side, None))
    masks = {rv: _rwkv_masks(rv) for rv in sorted({p[7] for p in probs})}
    mk = [masks[p[7]] for p in probs]
    each = lambda f, *cols: [f(*args) for args in zip(*cols)]
    rep = lambda x: jnp.concatenate([x] * HEADS_PER_GROUP, axis=0)

    c = [_dot_exact_lhs(m["tri"], p[3]) for m, p in zip(mk, probs)]
    c_l = [ci[(0 if p[7] else L - 1):(1 if p[7] else L), :] for ci, p in zip(c, probs)]

    blocks = [lambda x, sh=m["same_head"]: _bf(jnp.where(sh, rep(x), 0.0)) for m in mk]
    an, rn, vn, v_bd, bk_end, btk_bd, e_l = [], [], [], [], [], [], []
    for (r, v, kk, lw, kd, b, _, _), bd, ci, cl in zip(probs, blocks, c, c_l):
        e_nc = jnp.exp(-ci)
        e_end = jnp.exp(cl - ci)
        an.append(_bf(-kk * jnp.exp(ci - lw)))
        rn.append(_bf(r * jnp.exp(ci)))
        btk_bd.append(jnp.concatenate([bd(b * e_nc), bd(kd * e_nc)], axis=0))
        vn.append(_bf(v))
        v_bd.append(bd(v))
        bk_end.append(jnp.concatenate([_bf(b * e_end), _bf(kd * e_end)], axis=0))
        e_l.append(jnp.exp(cl))

    stack = lambda x, z: jnp.concatenate([x, z], axis=0)
    a_pow, t, akr, r_b = [], [], [], []
    for m, a_n, r_n, bk in zip(mk, an, rn, btk_bd):
        sn = _dot_nt(stack(a_n, r_n), bk)
        a_ab = jnp.where(m["strict_n"], sn[:L, :gw], 0.0)
        a_pow.append(a_ab)
        t.append(jnp.where(m["eye_n"], 1.0, a_ab))
        akr.append(stack(_bf(jnp.where(m["strict_n"], sn[:L, gw:], 0.0)),
                         _bf(jnp.where(m["incl_n"], sn[L:, gw:], 0.0))))
        r_b.append(_bf(jnp.where(m["incl_n"], sn[L:, :gw], 0.0)))
    tick()
    kv = each(_dot, akr, v_bd)
    a_bd = [bd(a) for bd, a in zip(blocks, a_pow)]
    tb = [_bf(ti) for ti in t]
    a_pow = [_dot(_bf(a), ab) for a, ab in zip(a_pow, a_bd)]
    for level in range(5):
        a_bd = [bd(a) for bd, a in zip(blocks, a_pow)]
        tick()
        if level < 4:
            res = [_dot(stack(tbi, _bf(a)), ab) for tbi, a, ab in zip(tb, a_pow, a_bd)]
            t = [ti + ri[:L] for ti, ri in zip(t, res)]
            a_pow = [ri[L:] for ri in res]
        else:
            t = [ti + _dot(tbi, ab) for ti, tbi, ab in zip(t, tb, a_bd)]
        tb = [_bf(ti) for ti in t]
    h_bf = [_bf(p[6]) for p in probs]
    hx = [_dot(stack(a_n, r_n), h) for a_n, r_n, h in zip(an, rn, h_bf)]
    x_bd = [bd(hxi[:L] + kvi[:L]) for bd, hxi, kvi in zip(blocks, hx, kv)]
    tick()
    u = [_dot(tbi, xi) for tbi, xi in zip(tb, x_bd)]
    tick()
    y = [hxi[L:] + _dot(rb, bd(ui)) + kvi[L:]
         for bd, hxi, rb, ui, kvi in zip(blocks, hx, r_b, u, kv)]
    e_col = [jnp.broadcast_to(el, (gw, gw)).T for el in e_l]
    h_new = []
    for m, ec, p, bke, ui, v_n in zip(mk, e_col, probs, bk_end, u, vn):
        upd = _dot_tn(bke, jnp.concatenate([_bf(ui), v_n], axis=0))
        h_new.append(ec * p[6] + jnp.where(m["same_head"], upd, 0.0))
    if side is not None:
        for _ in side:
            pass
    return list(zip(y, h_new))


def _chunk_row_block(b, i, *, reverse, n_lat, seq, ctx_len):
    nc = ctx_len // CHUNK
    nl = seq // CHUNK
    ci = (nc - 1 - i) if reverse else i
    li = (nl - 1 - (i - nc)) if reverse else (i - nc)
    return jnp.where(i < nc, (n_lat + b * ctx_len) // CHUNK + ci, b * nl + li)


def _gla_stages(qkvf, gdf, qkvb, gdb, gw2_ref, gb_ref, of_ref, ob_ref, sf_ref, sb_ref):
    dv, dk = sf_ref.shape[1:]
    L = CHUNK
    wk = B_HEADS * dk
    t_r = lax.broadcasted_iota(jnp.int32, (L, L), 0)
    t_c = lax.broadcasted_iota(jnp.int32, (L, L), 1)

    qh, kh, keh, vh, dech, keeph, sth = [], [], [], [], [], [], []
    for d, (qkv_ref, gd_ref, st_ref) in enumerate(((qkvf, gdf, sf_ref), (qkvb, gdb, sb_ref))):
        reverse = d == 1
        keep = (t_c >= t_r) if reverse else (t_c <= t_r)
        tri = _bf(jnp.where(keep, 1.0, 0.0))
        z = _dot(_bf(gd_ref[...]), gw2_ref[d]) + gb_ref[d:d + 1, :]
        la = -_softplus(-z) * (1.0 / B_GATE_TAU)
        cum = _dot_exact_lhs(tri, la)
        last = 0 if reverse else L - 1
        cl = cum[last:last + 1, :]
        q_in = _bf(qkv_ref[:, :wk] * (dk ** -0.5) * jnp.exp(cum))
        k_in = _bf(qkv_ref[:, wk:2 * wk] * jnp.exp(-cum))
        k_end = _bf(qkv_ref[:, wk:2 * wk] * jnp.exp(cl - cum))
        dec = jnp.exp(cl)
        for h in range(B_HEADS):
            ks = slice(h * dk, (h + 1) * dk)
            qh.append(q_in[:, ks])
            kh.append(k_in[:, ks])
            keh.append(k_end[:, ks])
            vh.append(_bf(qkv_ref[:, 2 * wk + h * dv:2 * wk + (h + 1) * dv]))
            dech.append(dec[:, ks])
            keeph.append(keep)
            sth.append(st_ref[h])
        yield

    sc = [_bf(jnp.where(kp, _dot_nt(q, k), 0.0)) for kp, q, k in zip(keeph, qh, kh)]
    yield
    inter = [_dot_nt(q, _bf(st)) for q, st in zip(qh, sth)]
    yield
    kvt = [_dot_tn(v, ke) for v, ke in zip(vh, keh)]
    yield
    out = [_dot(s_, v) + it for s_, v, it in zip(sc, vh, inter)]
    yield
    for d, (o_ref, st_ref) in enumerate(((of_ref, sf_ref), (ob_ref, sb_ref))):
        for h in range(B_HEADS):
            i = d * B_HEADS + h
            o_ref[:, h * dv:(h + 1) * dv] = out[i]
            st_ref[h] = sth[i] * dech[i] + kvt[i]


def _mix_scan_kernel(rvkf, lwf, df, rvkb, lwb, db, qkvf, gdf, qkvb, gdb, gw2_ref, gb_ref,
                     yf_ref, yb_ref, of_ref, ob_ref, hf_ref, hb_ref, sf_ref, sb_ref):
    @pl.when(pl.program_id(1) == 0)
    def _():
        for ref in (hf_ref, hb_ref, sf_ref, sb_ref):
            ref[...] = jnp.zeros_like(ref)

    gs = hf_ref.shape[0]
    aw = lwf.shape[1]
    probs = []
    for gi in range(gs):
        sl = slice(gi * GROUP_W, (gi + 1) * GROUP_W)
        f32 = lambda ref, j: ref[:, j * aw + gi * GROUP_W:j * aw + (gi + 1) * GROUP_W].astype(F32)
        for rvk, lw, dd, h_ref, rev in ((rvkf, lwf, df, hf_ref, False), (rvkb, lwb, db, hb_ref, True)):
            probs.append((f32(rvk, 0), f32(rvk, 1), f32(rvk, 2), lw[:, sl], f32(dd, 0), f32(dd, 1),
                          h_ref[gi], rev))
    gla = _gla_stages(qkvf, gdf, qkvb, gdb, gw2_ref, gb_ref, of_ref, ob_ref, sf_ref, sb_ref)
    res = _rwkv_chunks(probs, side=gla)
    for gi in range(gs):
        sl = slice(gi * GROUP_W, (gi + 1) * GROUP_W)
        yf_ref[:, sl], hf_ref[gi] = res[2 * gi]
        yb_ref[:, sl], hb_ref[gi] = res[2 * gi + 1]


def _mix_scan(rvk, d0, d1, lw0, lw1, pb, gw2p, gb, *, batch, n_lat, seq, ctx_len, wk, wv, col0):
    rows, aw = lw0.shape
    assert aw == GROUP_W * RWKV_GROUPS_PER_STEP
    dk, dv = wk // B_HEADS, wv // B_HEADS
    qkv_w = 2 * wk + wv
    assert col0 % qkv_w == 0
    geo = dict(n_lat=n_lat, seq=seq, ctx_len=ctx_len)
    gd_blk = (col0 + 2 * wk + 2 * wv) // LANE

    def specs(reverse):
        rb = lambda b, i: _chunk_row_block(b, i, reverse=reverse, **geo)
        row_blk = lambda w, cb=0: pl.BlockSpec((CHUNK, w), lambda b, i: (rb(b, i), cb))
        rw = [row_blk(3 * aw), row_blk(aw), row_blk(2 * aw)]
        gla = [row_blk(qkv_w, col0 // qkv_w), row_blk(LANE, gd_blk)]
        return rw, gla, row_blk(aw), row_blk(wv)

    rwf, glaf, yf, of = specs(False)
    rwb, glab, yb, ob = specs(True)
    sds_a = jax.ShapeDtypeStruct((rows, aw), F32)
    sds_b = jax.ShapeDtypeStruct((rows, wv), F32)
    gs = RWKV_GROUPS_PER_STEP
    return pl.pallas_call(
        _mix_scan_kernel,
        grid=(batch, (seq + ctx_len) // CHUNK),
        in_specs=rwf + rwb + glaf + glab + [
            pl.BlockSpec((2, LANE, wk), lambda b, i: (0, 0, 0)),
            pl.BlockSpec((2, wk), lambda b, i: (0, 0)),
        ],
        out_specs=[yf, yb, of, ob],
        out_shape=[sds_a, sds_a, sds_b, sds_b],
        scratch_shapes=[pltpu.VMEM((gs, GROUP_W, GROUP_W), F32),
                        pltpu.VMEM((gs, GROUP_W, GROUP_W), F32),
                        pltpu.VMEM((B_HEADS, dv, dk), F32), pltpu.VMEM((B_HEADS, dv, dk), F32)],
        compiler_params=_cparams(("parallel", "arbitrary")),
    )(rvk, lw0, d0, rvk, lw1, d1, pb, pb, pb, pb, gw2p, gb)


def _mixer_out_kernel(yf, yb, bonus, g, of, ob, gate, xa_ref, xb_ref, mod_ref, lnw, lnb, gn, e_ref,
                      w_ref, o_ref, h_ref, *, aw, dv, n_a_tiles):
    e_bf = e_ref[...]
    y = yf[...] + yb[...]
    inv = 1.0 / A_HEAD_DIM
    mean = _head_sum(y, e_bf) * inv
    dlt = y - mean
    var = _head_sum(dlt * dlt, e_bf) * inv
    yn = dlt * lax.rsqrt(var + A_LN_EPS) * lnw[...] + lnb[...]
    h_ref[:, :aw] = _bf((yn + bonus[...]) * g[...])

    o = of[...] + ob[...]
    gt = _silu(gate[...])
    for h in range(B_HEADS):
        sl = slice(h * dv, (h + 1) * dv)
        oh = o[:, sl]
        ms = jnp.mean(oh * oh, axis=-1, keepdims=True)
        h_ref[:, aw + h * dv:aw + (h + 1) * dv] = _bf(oh * lax.rsqrt(ms + NORM_EPS) * gn[...] * gt[:, sl])

    x = jnp.where(pl.program_id(0) < n_a_tiles, xa_ref[...], xb_ref[...])
    o_ref[...] = x + mod_ref[2:3, :] * _dot(h_ref[...], w_ref[...])


def _mixer_out(yf, yb, bonus, g, of, ob, pb, xa, xb, mods_l, ln_w, ln_b, gla_norm, e_bf, w_bf,
               seg_of_tile, *, wk, col0):
    d = xa.shape[1]
    rows = xa.shape[0] + xb.shape[0]
    aw = yf.shape[1]
    wv = of.shape[1]
    dv = wv // B_HEADS
    tm = 256
    tiles_per_row_tile = ROW_TILE // tm
    n_a_tiles = xa.shape[0] // tm
    rs = lambda w: pl.BlockSpec((tm, w), lambda i: (i, 0))
    full = lambda shape: pl.BlockSpec(shape, lambda i: (0,) * len(shape))
    return pl.pallas_call(
        functools.partial(_mixer_out_kernel, aw=aw, dv=dv, n_a_tiles=n_a_tiles),
        grid=(rows // tm,),
        in_specs=[
            rs(aw), rs(aw), rs(aw), rs(aw), rs(wv), rs(wv),
            pl.BlockSpec((tm, wv), lambda i: (i, (col0 + 2 * wk + wv) // wv)),
        ] + _two_source_specs(tm, d, n_a_tiles) + [
            pl.BlockSpec((None, N_MOD, d), lambda i: (seg_of_tile(i // tiles_per_row_tile), 0, 0)),
            full((1, aw)), full((1, aw)), full((1, dv)), full((GROUP_W, GROUP_W)),
            full((aw + wv, d)),
        ],
        out_specs=rs(d),
        out_shape=jax.ShapeDtypeStruct((rows, d), F32),
        scratch_shapes=[pltpu.VMEM((tm, aw + wv), BF16)],
        compiler_params=_cparams(("parallel",)),
    )(yf, yb, bonus, g, of, ob, pb, xa, xb, mods_l, ln_w.reshape(1, aw), ln_b.reshape(1, aw),
      gla_norm.reshape(1, dv), e_bf, w_bf)


def _lru_kernel(*refs, reverse, n_blk, combine, cols):
    if combine:
        (x_ref, xp_ref, xn_ref, h0_ref, cw_ref, cb_ref, wc_ref, ba_ref, bx_ref, lam_ref,
         hf_ref, gate_ref, out_ref, hl_ref, a_sc, u_sc, hs_sc, carry_sc) = refs
    else:
        (x_ref, xp_ref, xn_ref, h0_ref, cw_ref, cb_ref, wc_ref, ba_ref, bx_ref, lam_ref,
         out_ref, hl_ref, a_sc, u_sc, hs_sc, carry_sc) = refs
    j = pl.program_id(2)
    blk = (n_blk - 1 - j) if reverse else j
    lb = x_ref.shape[0]
    ct = cw_ref.shape[1]
    cblk = wc_ref.shape[1]
    rowid = lax.broadcasted_iota(jnp.int32, (lb, 1), 0)
    decay_rate = -C_CONST * _softplus(-lam_ref[...])

    @pl.when(j == 0)
    def _():
        carry_sc[...] = h0_ref[0:1, :]

    for q in (range(cols - 1, -1, -1) if reverse else range(cols)):
        lanes = slice(q * ct, (q + 1) * ct)
        x = x_ref[:, lanes]
        if q > 0:
            prev = slice((q - 1) * ct, q * ct)
            p6, p7 = x_ref[lb - 2:lb - 1, prev], x_ref[lb - 1:lb, prev]
        else:
            p6 = jnp.where(blk == 0, 0.0, xp_ref[SUBLANE - 2:SUBLANE - 1, :])
            p7 = jnp.where(blk == 0, 0.0, xp_ref[SUBLANE - 1:SUBLANE, :])
        if q < cols - 1:
            n0 = x_ref[0:1, (q + 1) * ct:(q + 2) * ct]
        else:
            n0 = jnp.where(blk == n_blk - 1, 0.0, xn_ref[0:1, :])
        xm1 = jnp.where(rowid == 0, p7, pltpu.roll(x, 1, 0))
        xm2 = jnp.where(rowid == 0, p6, jnp.where(rowid == 1, p7, pltpu.roll(x, 2, 0)))
        xp1 = jnp.where(rowid == lb - 1, n0, pltpu.roll(x, lb - 1, 0))
        xs = (cb_ref[...] + cw_ref[0:1, :] * xm2 + cw_ref[1:2, :] * xm1 + cw_ref[2:3, :] * x
              + cw_ref[3:4, :] * xp1)

        for n in range(ct // cblk):
            sl = slice(n * cblk, (n + 1) * cblk)
            xb = xs[:, sl]
            ri = _dot(_bf(xb), wc_ref[n])
            rg = _sigmoid(ri[:, :cblk] + ba_ref[:, sl])
            ig = _sigmoid(ri[:, cblk:] + bx_ref[:, sl])
            a = jnp.exp(decay_rate[:, sl] * rg)
            a_sc[q, :, sl] = a
            u_sc[q, :, sl] = jnp.sqrt(1.0 - a * a) * (ig * xb)

        def step(t, h, q=q):
            tt = (lb - 1 - t) if reverse else t
            h = a_sc[q, pl.ds(tt, 1), :] * h + u_sc[q, pl.ds(tt, 1), :]
            hs_sc[q, pl.ds(tt, 1), :] = h
            return h

        carry_sc[...] = lax.fori_loop(0, lb, step, carry_sc[...], unroll=8)
        if combine:
            out_ref[:, lanes] = ((hs_sc[q] + hf_ref[q * lb:(q + 1) * lb, :])
                                 * jax.nn.gelu(gate_ref[:, lanes]))
        else:
            out_ref[q * lb:(q + 1) * lb, :] = hs_sc[q]
    hl_ref[...] = jnp.broadcast_to(carry_sc[...], hl_ref.shape)


def _lru_pass(x_arr, x_specs, h0, conv_w, conv_b, wcat, ba, bx, lam, out_shape, out_spec, extra,
              *, batch, n_blk, lb, cols, reverse):
    c = conv_w.shape[1]
    ct = LRU_CHANNEL_TILE
    nct = c // ct
    cblk = wcat.shape[1]
    chan = lambda rows_: pl.BlockSpec((rows_, ct), lambda b, k, j: (0, k))
    in_specs = list(x_specs) + [
        pl.BlockSpec((SUBLANE, ct), lambda b, k, j: (b, k)),
        chan(C_CONV), chan(1),
        pl.BlockSpec((ct // cblk, cblk, 2 * cblk), lambda b, k, j: (k, 0, 0)),
        chan(1), chan(1), chan(1),
    ] + [s for _, s in extra]
    args = [x_arr, x_arr, x_arr, h0, conv_w, conv_b.reshape(1, c), wcat, ba.reshape(1, c),
            bx.reshape(1, c), lam.reshape(1, c)] + [a for a, _ in extra]
    return pl.pallas_call(
        functools.partial(_lru_kernel, reverse=reverse, n_blk=n_blk, combine=bool(extra),
                          cols=cols),
        grid=(batch, nct, n_blk),
        in_specs=in_specs,
        out_specs=[out_spec, pl.BlockSpec((SUBLANE, ct), lambda b, k, j: (b, k))],
        out_shape=[out_shape, jax.ShapeDtypeStruct((batch * SUBLANE, c), F32)],
        scratch_shapes=[pltpu.VMEM((cols, lb, ct), F32), pltpu.VMEM((cols, lb, ct), F32),
                        pltpu.VMEM((cols, lb, ct), F32), pltpu.VMEM((1, ct), F32)],
        compiler_params=_cparams(("parallel", "parallel", "arbitrary")),
    )(*args)


def _rglru(hw_ctx, gate_grid, x_grid, conv_w, conv_b, wa, ba, wx, bx, lam, *, batch, n_lat, seq,
           ctx_len):
    c = conv_w.shape[1]
    ct = LRU_CHANNEL_TILE
    rows = seq // GRID_W
    lbc = 128
    ncb = ctx_len // lbc
    xcol = c // ct

    wcat = [_bf(jnp.concatenate([wa[d], wx[d]], axis=-1)) for d in range(2)]
    zeros_state = jnp.zeros((batch * SUBLANE, c), F32)

    def ctx_specs(reverse):
        blk = (lambda j: ncb - 1 - j) if reverse else (lambda j: j)
        per8 = lbc // SUBLANE
        cur = pl.BlockSpec((lbc, ct), lambda b, k, j: (b * ncb + blk(j), xcol + k))
        prv = pl.BlockSpec((SUBLANE, ct), lambda b, k, j: (
            b * ncb * per8 + jnp.maximum(blk(j) * per8 - 1, 0), xcol + k))
        nxt = pl.BlockSpec((SUBLANE, ct), lambda b, k, j: (
            b * ncb * per8 + jnp.minimum((blk(j) + 1) * per8, ncb * per8 - 1), xcol + k))
        return cur, prv, nxt

    assert ct == c
    kc = LRU_COLS_PER_STEP
    ncg = GRID_W // kc

    def lat_specs(reverse):
        grp = (lambda j: ncg - 1 - j) if reverse else (lambda j: j)
        r8 = rows // SUBLANE
        cur = pl.BlockSpec((rows, kc * ct), lambda b, k, j: (b, grp(j)))
        prv = pl.BlockSpec((SUBLANE, ct), lambda b, k, j: (
            b * r8 + r8 - 1, jnp.maximum(grp(j) * kc - 1, 0)))
        nxt = pl.BlockSpec((SUBLANE, ct), lambda b, k, j: (
            b * r8, jnp.minimum(grp(j) * kc + kc, GRID_W - 1)))
        hf = pl.BlockSpec((kc * rows, ct), lambda b, k, j: (b * ncg + grp(j), 0))
        return (cur, prv, nxt), hf, cur

    common = dict(batch=batch)
    ctx_scratch_sds = jax.ShapeDtypeStruct((batch * ctx_len, c), F32)
    ctx_out = lambda reverse: pl.BlockSpec(
        (lbc, ct), lambda b, k, j: (b * ncb + ((ncb - 1 - j) if reverse else j), k))
    _, st = _lru_pass(hw_ctx, ctx_specs(False), zeros_state, conv_w, conv_b, wcat[0], ba[0], bx[0],
                      lam[0], ctx_scratch_sds, ctx_out(False), [], n_blk=ncb, lb=lbc, cols=1,
                      reverse=False, **common)
    hf_sds = jax.ShapeDtypeStruct((n_lat, c), F32)
    x_specs, hf_spec, _ = lat_specs(False)
    hf, _ = _lru_pass(x_grid, x_specs, st, conv_w, conv_b, wcat[0], ba[0], bx[0], lam[0],
                      hf_sds, hf_spec, [], n_blk=ncg, lb=rows, cols=kc, reverse=False, **common)
    _, st = _lru_pass(hw_ctx, ctx_specs(True), zeros_state, conv_w, conv_b, wcat[1], ba[1], bx[1],
                      lam[1], ctx_scratch_sds, ctx_out(True), [], n_blk=ncb, lb=lbc, cols=1,
                      reverse=True, **common)
    x_specs, hf_spec, grid_spec = lat_specs(True)
    y_sds = jax.ShapeDtypeStruct((n_lat // GRID_W, GRID_W * c), F32)
    y, _ = _lru_pass(x_grid, x_specs, st, conv_w, conv_b, wcat[1], ba[1], bx[1], lam[1],
                     y_sds, grid_spec, [(hf, hf_spec), (gate_grid, grid_spec)],
                     n_blk=ncg, lb=rows, cols=kc, reverse=True, **common)
    return y


def _pad_cols(w, n):
    return jnp.pad(w, ((0, 0),) * (w.ndim - 1) + ((0, n - w.shape[-1]),))


def kernel(x, c, ctx, c_ctx, mod_w, mod_b, norm1, norm2, mlp_w1, mlp_w2, ab_w_in, ab_w_out, rw_mu,
           rw_w0, rw_w2, rw_a0, rw_a2, rw_g2, rw_kk, rw_ka, rw_rk, rw_ln_w, rw_ln_b, gla_gw2,
           gla_gb, gla_norm, lru_w_in, lru_w_out, lru_conv_w, lru_conv_b, lru_wa, lru_ba, lru_wx,
           lru_bx, lru_lam, final_norm):
    batch, seq, d = x.shape
    ctx_len = ctx.shape[1]
    depth = mod_w.shape[0]
    n_lat = batch * seq
    aw = rw_w0.shape[-1]
    wk = gla_gb.shape[-1]
    wv = ab_w_out.shape[1] - aw
    a_cols = rw_mu.shape[-1]
    b_cols = ab_w_in.shape[-1] - a_cols
    assert depth == 2 and batch + 1 <= SUBLANE
    assert seq % ROW_TILE == 0 and (batch * ctx_len) % ROW_TILE == 0
    assert ctx_len % PREP_TILE == 0 and ctx_len % 128 == 0 and seq % (GRID_W * SUBLANE) == 0
    assert aw % GROUP_W == 0 and wv == aw and 2 * wk == wv

    tiles_per_seq = seq // ROW_TILE
    seg_of_tile = lambda i: jnp.minimum(i // tiles_per_seq, batch)

    x2 = x.reshape(n_lat, d)
    ctx2 = ctx.reshape(batch * ctx_len, d)
    cvec = jnp.concatenate(
        [c, c_ctx[None, :], jnp.zeros((SUBLANE - batch - 1, d), F32)], axis=0)
    mods = _mods(cvec, mod_w, mod_b)

    hid = jnp.arange(GROUP_W) // A_HEAD_DIM
    e_bf = _bf(hid[:, None] == hid[None, :])

    a_pad = -(-a_cols // LANE) * LANE
    col0 = -(-a_cols // (2 * wk)) * (2 * wk)
    n_tiles = 3
    tn = -(-(col0 + b_cols) // (n_tiles * LANE)) * LANE
    w_ab = _bf(jnp.concatenate(
        [_pad_cols(ab_w_in[0][:, :a_cols], col0),
         _pad_cols(ab_w_in[0][:, a_cols:], n_tiles * tn - col0)], axis=1))
    pa = pb = _norm_matmul(x2, ctx2, norm1[0], mods[0], w_ab, seg_of_tile, tn=tn)

    lora_w = 2 * A_LORA_W + 2 * A_LORA_A
    w2p = jnp.zeros((2, lora_w, aw), F32)
    a2p = jnp.zeros((2, lora_w, aw), F32)
    for dd in range(2):
        w2p = w2p.at[dd, dd * A_LORA_W:(dd + 1) * A_LORA_W].set(rw_w2[0, dd])
        a2p = a2p.at[dd, 2 * A_LORA_W + dd * A_LORA_A:2 * A_LORA_W + (dd + 1) * A_LORA_A].set(
            rw_a2[0, dd])
    (rvk, d0, d1, lw0, lw1, g_, bonus) = _rwkv_prep(
        pa, _pad_cols(rw_mu[0], a_pad), rw_w0[0], rw_a0[0], _bf(w2p), _bf(a2p), _bf(rw_g2[0]),
        rw_kk[0], rw_ka[0], rw_rk[0].reshape(-1), e_bf, n_lat=n_lat, seq=seq, ctx_len=ctx_len)
    gw2p = jnp.zeros((2, LANE, wk), F32)
    for dd in range(2):
        gw2p = gw2p.at[dd, dd * B_LORA:(dd + 1) * B_LORA].set(gla_gw2[0, dd])
    yf, yb, of, ob = _mix_scan(rvk, d0, d1, lw0, lw1, pb, _bf(gw2p), gla_gb[0],
                               batch=batch, n_lat=n_lat, seq=seq, ctx_len=ctx_len, wk=wk, wv=wv,
                               col0=col0)

    xs = _mixer_out(yf, yb, bonus, g_, of, ob, pb, x2, ctx2, mods[0], rw_ln_w[0], rw_ln_b[0],
                    gla_norm[0], e_bf, _bf(ab_w_out[0]), seg_of_tile, wk=wk, col0=col0)
    seg_of_mlp_tile = lambda i: jnp.minimum(i // (seq // MLP_TILE), batch)
    ctx_tiles = (batch * ctx_len) // ROW_TILE
    w1_bf, w2_bf = _bf(mlp_w1[0]), _bf(mlp_w2[0])
    x_lat = _mlp(xs, norm2[0], mods[0], w1_bf, w2_bf, final_norm, seg_of_mlp_tile, tm=MLP_TILE,
                 tile0=0, n_tiles=n_lat // MLP_TILE, final_norm=False)
    x_ctx = _mlp(xs, norm2[0], mods[0], w1_bf, w2_bf, final_norm, seg_of_tile, tm=ROW_TILE,
                 tile0=n_lat // ROW_TILE, n_tiles=ctx_tiles, final_norm=False)

    w_lru = _bf(lru_w_in[0])
    hw_ctx = _norm_matmul(x_ctx, None, norm1[1], mods[1], w_lru, lambda i: batch, tn=2048)
    gate_grid, x_grid = _norm_matmul_grid(x_lat, n_lat, seq // GRID_W, norm1[1], mods[1], w_lru)
    y = _rglru(hw_ctx, gate_grid, x_grid, lru_conv_w[0], lru_conv_b[0], lru_wa[0], lru_ba[0],
               lru_wx[0], lru_bx[0], lru_lam[0], batch=batch, n_lat=n_lat, seq=seq,
               ctx_len=ctx_len)
    xl = _proj_res(y, x_lat, n_lat, seq // GRID_W, mods[1], _bf(lru_w_out[0]))
    out = _mlp(xl, norm2[1], mods[1], _bf(mlp_w1[1]), _bf(mlp_w2[1]), final_norm, seg_of_mlp_tile,
               tm=MLP_TILE, tile0=0, n_tiles=n_lat // MLP_TILE, final_norm=True)
    return out.reshape(batch, seq, d)
```

```python
import functools

import jax
import jax.numpy as jnp
from jax import lax
from jax.experimental import pallas as pl
from jax.experimental.pallas import tpu as pltpu

F32 = jnp.float32
BF16 = jnp.bfloat16

NORM_EPS = 1e-6
GRID_W = 64
N_MOD = 6

A_HEAD_DIM = 64
A_LORA_W = 96
A_LORA_A = 96
A_LORA_G = 256
A_LN_EPS = 64e-5
HEADS_PER_GROUP = 4
GROUP_W = HEADS_PER_GROUP * A_HEAD_DIM
CHUNK = 64
RWKV_GROUPS_PER_STEP = 4

B_HEADS = 4
B_LORA = 16
B_GATE_TAU = 16.0

C_BLOCKS = 8
C_CONV = 4
C_CONST = 8.0

LANE = 128
SUBLANE = 8
ROW_TILE = 512
MLP_TILE = 1024
MLP_STEP_ELEMS = 1024 * 512
LRU_CHANNEL_TILE = 2048
LRU_COLS_FWD = 4
LRU_COLS_BWD = 1
PREP_TILE = 256
VMEM_LIMIT = 56 * 1024 * 1024


def _cparams(sem):
    return pltpu.CompilerParams(dimension_semantics=sem, vmem_limit_bytes=VMEM_LIMIT)


def _bf(x):
    return x.astype(BF16)


def _dot(a, b):
    return jnp.dot(a, b, preferred_element_type=F32)


def _dot_nt(a, b):
    return lax.dot_general(a, b, (((1,), (1,)), ((), ())), preferred_element_type=F32)


def _dot_tn(a, b):
    return lax.dot_general(a, b, (((0,), (0,)), ((), ())), preferred_element_type=F32)


def _split3(x):
    hi = _bf(x)
    r1 = x - hi.astype(F32)
    mid = _bf(r1)
    lo = _bf(r1 - mid.astype(F32))
    return hi, mid, lo


def _dot_exact_lhs(m_bf, x):
    hi, mid, lo = _split3(x)
    return _dot(m_bf, hi) + _dot(m_bf, mid) + _dot(m_bf, lo)


def _dot_exact_rhs(x, m_bf):
    hi, mid, lo = _split3(x)
    return _dot(hi, m_bf) + _dot(mid, m_bf) + _dot(lo, m_bf)


def _softplus(x):
    return jnp.maximum(x, 0.0) + jnp.log(1.0 + jnp.exp(-jnp.abs(x)))


def _sigmoid(x):
    return 0.5 * jnp.tanh(0.5 * x) + 0.5


def _silu(x):
    return x * _sigmoid(x)


def _norm_mod(x, g, shift, scale):
    ms = jnp.mean(x * x, axis=-1, keepdims=True)
    return (x * lax.rsqrt(ms + NORM_EPS) * g) * (1.0 + scale) + shift


def _mods_kernel(c_ref, w_ref, b_ref, o_ref):
    s = _silu(c_ref[...])
    o_ref[...] = _dot(_bf(s), _bf(w_ref[...])) + b_ref[...]


def _mods(cvec, mod_w, mod_b):
    depth, d, n = mod_w.shape
    tn = 1024
    out = pl.pallas_call(
        _mods_kernel,
        grid=(depth, n // tn),
        in_specs=[
            pl.BlockSpec((SUBLANE, d), lambda l, j: (0, 0)),
            pl.BlockSpec((None, d, tn), lambda l, j: (l, 0, j)),
            pl.BlockSpec((None, 1, tn), lambda l, j: (l, 0, j)),
        ],
        out_specs=pl.BlockSpec((None, SUBLANE, tn), lambda l, j: (l, 0, j)),
        out_shape=jax.ShapeDtypeStruct((depth, SUBLANE, n), F32),
        compiler_params=_cparams(("parallel", "parallel")),
    )(cvec, mod_w, mod_b.reshape(depth, 1, n))
    return out.reshape(depth, SUBLANE, N_MOD, d)


def _norm_matmul_kernel(xa_ref, xb_ref, g_ref, mod_ref, w_ref, o_ref, h_ref, *, n_a_tiles):
    i = pl.program_id(0)

    @pl.when(pl.program_id(1) == 0)
    def _():
        x = jnp.where(i < n_a_tiles, xa_ref[...], xb_ref[...])
        h_ref[...] = _bf(_norm_mod(x, g_ref[...], mod_ref[0:1, :], mod_ref[1:2, :]))

    o_ref[...] = _dot(h_ref[...], w_ref[...])


def _two_source_specs(tm, d, n_a_tiles):
    return [pl.BlockSpec((tm, d), lambda i, *_: (jnp.minimum(i, n_a_tiles - 1), 0)),
            pl.BlockSpec((tm, d), lambda i, *_: (jnp.maximum(i - n_a_tiles, 0), 0))]


def _norm_matmul(xa, xb, g, mods_l, w_bf, seg_of_tile, *, tn):
    d = xa.shape[1]
    n = w_bf.shape[1]
    tm = ROW_TILE
    n_a_tiles = xa.shape[0] // tm
    n_tiles = n_a_tiles + (0 if xb is None else xb.shape[0] // tm)
    return pl.pallas_call(
        functools.partial(_norm_matmul_kernel, n_a_tiles=n_a_tiles),
        grid=(n_tiles, n // tn),
        in_specs=_two_source_specs(tm, d, n_a_tiles) + [
            pl.BlockSpec((1, d), lambda i, j: (0, 0)),
            pl.BlockSpec((None, N_MOD, d), lambda i, j: (seg_of_tile(i), 0, 0)),
            pl.BlockSpec((d, tn), lambda i, j: (0, j)),
        ],
        out_specs=pl.BlockSpec((tm, tn), lambda i, j: (i, j)),
        out_shape=jax.ShapeDtypeStruct((n_tiles * tm, n), F32),
        scratch_shapes=[pltpu.VMEM((tm, d), BF16)],
        compiler_params=_cparams(("parallel", "arbitrary")),
    )(xa, xa if xb is None else xb, g.reshape(1, d), mods_l, w_bf)


def _grid_tile(rows):
    gr = min(rows, 64)
    gc = ROW_TILE // gr
    assert gc % SUBLANE == 0 and GRID_W % gc == 0 and rows % gr == 0
    return gr, gc


def _norm_matmul_grid_kernel(x_ref, g_ref, mod_ref, w_ref, o0_ref, o1_ref, h_ref, res_ref):
    j = pl.program_id(1)
    gr, gc, d = x_ref.shape

    @pl.when(j == 0)
    def _():
        h = _norm_mod(x_ref[...].reshape(gr * gc, d), g_ref[...], mod_ref[0:1, :], mod_ref[1:2, :])
        h_ref[...] = _bf(h)

    res = _dot(h_ref[...], w_ref[...])
    nh = w_ref.shape[1]
    nq = nh // LANE
    for q in range(nq):
        res_ref[q] = res[:, q * LANE:(q + 1) * LANE]

    def scatter(o_ref):
        for c in range(gc):
            for q in range(nq):
                o_ref[:, c * nh + q * LANE:c * nh + (q + 1) * LANE] = (
                    res_ref[q, pl.ds(c, gr, stride=gc), :])

    @pl.when(j == 0)
    def _():
        scatter(o0_ref)

    @pl.when(j == 1)
    def _():
        scatter(o1_ref)


def _norm_matmul_grid(x, n_lat, rows, g, mods_l, w_bf):
    d = x.shape[1]
    nh = w_bf.shape[1] // 2
    gr, gc = _grid_tile(rows)
    ncg = GRID_W // gc
    x3 = x.reshape(x.shape[0] // GRID_W, GRID_W, d)
    out_spec = pl.BlockSpec((gr, gc * nh), lambda i, j: (i // ncg, i % ncg))
    sds = jax.ShapeDtypeStruct((n_lat // GRID_W, GRID_W * nh), F32)
    return pl.pallas_call(
        _norm_matmul_grid_kernel,
        grid=(n_lat // ROW_TILE, 2),
        in_specs=[
            pl.BlockSpec((gr, gc, d), lambda i, j: (i // ncg, i % ncg, 0)),
            pl.BlockSpec((1, d), lambda i, j: (0, 0)),
            pl.BlockSpec((None, N_MOD, d), lambda i, j: ((i // ncg) * gr // rows, 0, 0)),
            pl.BlockSpec((d, nh), lambda i, j: (0, j)),
        ],
        out_specs=[out_spec, out_spec],
        out_shape=[sds, sds],
        scratch_shapes=[pltpu.VMEM((ROW_TILE, d), BF16),
                        pltpu.VMEM((nh // LANE, ROW_TILE, LANE), F32)],
        compiler_params=_cparams(("parallel", "arbitrary")),
    )(x3, g.reshape(1, d), mods_l, w_bf)


def _mlp_kernel(x_ref, g_ref, mod_ref, w1_ref, w2_ref, fn_ref, o_ref, h_ref, *, final_norm):
    j = pl.program_id(1)
    slab = 256
    n_slab = x_ref.shape[0] // slab

    @pl.when(j == 0)
    def _():
        for s in range(n_slab):
            rs = slice(s * slab, (s + 1) * slab)
            h_ref[rs, :] = _bf(_norm_mod(x_ref[rs, :], g_ref[...], mod_ref[3:4, :],
                                         mod_ref[4:5, :]))
        o_ref[...] = jnp.zeros_like(o_ref)

    a = _dot(h_ref[...], w1_ref[...])
    a = _bf(jnp.square(jnp.maximum(a, 0.0)))
    nsplit = 2
    wn = o_ref.shape[1] // nsplit
    for s in range(nsplit):
        o_ref[:, s * wn:(s + 1) * wn] += _dot(a, w2_ref[:, s * wn:(s + 1) * wn])

    @pl.when(j == pl.num_programs(1) - 1)
    def _():
        for s in range(n_slab):
            rs = slice(s * slab, (s + 1) * slab)
            xo = x_ref[rs, :] + mod_ref[5:6, :] * o_ref[rs, :]
            if final_norm:
                ms = jnp.mean(xo * xo, axis=-1, keepdims=True)
                xo = xo * lax.rsqrt(ms + NORM_EPS) * fn_ref[...]
            o_ref[rs, :] = xo


def _mlp(x, g, mods_l, w1_bf, w2_bf, fn, seg_of_tile, *, tm, tile0, n_tiles, final_norm):
    d = x.shape[1]
    dff = w1_bf.shape[1]
    tf = MLP_STEP_ELEMS // tm
    return pl.pallas_call(
        functools.partial(_mlp_kernel, final_norm=final_norm),
        grid=(n_tiles, dff // tf),
        in_specs=[
            pl.BlockSpec((tm, d), lambda i, j: (i + tile0, 0), pipeline_mode=pl.Buffered(1)),
            pl.BlockSpec((1, d), lambda i, j: (0, 0)),
            pl.BlockSpec((None, N_MOD, d), lambda i, j: (seg_of_tile(i + tile0), 0, 0)),
            pl.BlockSpec((d, tf), lambda i, j: (0, j)),
            pl.BlockSpec((tf, d), lambda i, j: (j, 0)),
            pl.BlockSpec((1, d), lambda i, j: (0, 0)),
        ],
        out_specs=pl.BlockSpec((tm, d), lambda i, j: (i, 0)),
        out_shape=jax.ShapeDtypeStruct((n_tiles * tm, d), F32),
        scratch_shapes=[pltpu.VMEM((tm, d), BF16)],
        compiler_params=_cparams(("parallel", "arbitrary")),
    )(x, g.reshape(1, d), mods_l, w1_bf, w2_bf, fn.reshape(1, d))


def _proj_res_kernel(y_ref, x_ref, mod_ref, w_ref, o_ref, yt_ref):
    k = w_ref.shape[0]
    gr, gc, d = x_ref.shape
    nq = k // LANE
    for c in range(gc):
        for q in range(nq):
            yt_ref[q, pl.ds(c, gr, stride=gc), :] = (
                y_ref[:, c * k + q * LANE:c * k + (q + 1) * LANE])
    yt = jnp.concatenate([_bf(yt_ref[q]) for q in range(nq)], axis=1)
    out = x_ref[...].reshape(gr * gc, d) + mod_ref[2:3, :] * _dot(yt, w_ref[...])
    o_ref[...] = out.reshape(gr, gc, d)


def _proj_res(y_grid, x, n_lat, rows, mods_l, w_bf):
    k, d = w_bf.shape
    gr, gc = _grid_tile(rows)
    ncg = GRID_W // gc
    x3 = x.reshape(x.shape[0] // GRID_W, GRID_W, d)
    tok_spec = pl.BlockSpec((gr, gc, d), lambda i: (i // ncg, i % ncg, 0))
    out = pl.pallas_call(
        _proj_res_kernel,
        grid=(n_lat // ROW_TILE,),
        in_specs=[
            pl.BlockSpec((gr, gc * k), lambda i: (i // ncg, i % ncg)),
            tok_spec,
            pl.BlockSpec((None, N_MOD, d), lambda i: ((i // ncg) * gr // rows, 0, 0)),
            pl.BlockSpec((k, d), lambda i: (0, 0)),
        ],
        out_specs=tok_spec,
        out_shape=jax.ShapeDtypeStruct((n_lat // GRID_W, GRID_W, d), F32),
        scratch_shapes=[pltpu.VMEM((k // LANE, ROW_TILE, LANE), F32)],
        compiler_params=_cparams(("parallel",)),
    )(y_grid, x3, mods_l, w_bf)
    return out.reshape(n_lat, d)


def _head_sum(z, e_bf):
    parts = []
    for gi in range(z.shape[1] // GROUP_W):
        zs = z[:, gi * GROUP_W:(gi + 1) * GROUP_W]
        parts.append(_dot_exact_rhs(zs, e_bf))
    return jnp.concatenate(parts, axis=1)


def _rwkv_prep_kernel(p_ref, pp_ref, pn_ref, mu_ref, w0_ref, a0_ref, w2_ref, a2_ref, g2_ref,
                      kk_ref, ka_ref, rk_ref, e_ref,
                      rvk_o, d0_o, d1_o, lw0_o, lw1_o, g_o, bonus_o,
                      *, n_lat_tiles, lat_tiles_per_seq, ctx_tiles_per_seq, aw):
    i = pl.program_id(0)
    tp = p_ref.shape[0]
    in_ctx = i >= n_lat_tiles
    pos = jnp.where(in_ctx, (i - n_lat_tiles) % ctx_tiles_per_seq, i % lat_tiles_per_seq)
    last = jnp.where(in_ctx, ctx_tiles_per_seq - 1, lat_tiles_per_seq - 1)
    is_first = pos == 0
    is_last = pos == last
    rowid = lax.broadcasted_iota(jnp.int32, (tp, 1), 0)

    def shifted(c0, c1):
        x = p_ref[:, c0:c1]
        prow = jnp.where(is_first, 0.0, pp_ref[SUBLANE - 1:SUBLANE, c0:c1])
        nrow = jnp.where(is_last, 0.0, pn_ref[0:1, c0:c1])
        xp = jnp.where(rowid == 0, prow, pltpu.roll(x, 1, 0))
        xn = jnp.where(rowid == tp - 1, nrow, pltpu.roll(x, tp - 1, 0))
        return x + mu_ref[0:1, c0:c1] * (xp - x) + mu_ref[1:2, c0:c1] * (xn - x)

    e_bf = e_ref[...]
    r = shifted(0, aw)
    k = shifted(aw, 2 * aw)
    v = shifted(2 * aw, 3 * aw)
    gd = shifted(3 * aw, 3 * aw + A_LORA_G)
    lo0 = 3 * aw + A_LORA_G
    xl = shifted(lo0, lo0 + 2 * A_LORA_W + 2 * A_LORA_A)

    rvk_o[:, :aw] = r.astype(rvk_o.dtype)
    rvk_o[:, aw:2 * aw] = v.astype(rvk_o.dtype)
    g_o[...] = _dot(_bf(_sigmoid(gd)), g2_ref[...])

    kx = k * kk_ref[...]
    kn = jnp.sqrt(_head_sum(kx * kx, e_bf))
    kk = kx / jnp.maximum(kn, 1e-12)
    rvk_o[:, 2 * aw:] = kk.astype(rvk_o.dtype)

    xl_t = _bf(jnp.tanh(xl))
    xl_b = _bf(xl)
    kd_sum = None
    for d, (lw_o, d_o) in enumerate(((lw0_o, d0_o), (lw1_o, d1_o))):
        w_log = -_softplus(-(w0_ref[d:d + 1, :] + _dot(xl_t, w2_ref[d]))) - 0.5
        lw_o[...] = -jnp.exp(w_log)
        asig = _sigmoid(a0_ref[d:d + 1, :] + _dot(xl_b, a2_ref[d]))
        kd = k * (1.0 + (asig - 1.0) * ka_ref[...])
        d_o[:, :aw] = kd.astype(d_o.dtype)
        d_o[:, aw:] = (kk * asig).astype(d_o.dtype)
        kd_sum = kd if kd_sum is None else kd_sum + kd
    bonus_o[...] = _head_sum(r * rk_ref[...] * kd_sum, e_bf) * v


def _rwkv_prep(pa, mu_p, w0, a0, w2p, a2p, g2, k_k, k_a, r_k, e_bf, *, n_lat, seq, ctx_len):
    r = pa.shape[0]
    ncol = mu_p.shape[1]
    aw = w0.shape[1]
    tp = PREP_TILE
    hb = tp // SUBLANE
    nblk8 = r // SUBLANE
    lw = 2 * A_LORA_W + 2 * A_LORA_A
    row = lambda a: a.reshape(1, aw)
    full = lambda shape: pl.BlockSpec(shape, lambda i: (0,) * len(shape))
    outs = [(3 * aw, BF16), (2 * aw, BF16), (2 * aw, BF16), (aw, F32), (aw, F32), (aw, F32),
            (aw, F32)]
    return pl.pallas_call(
        functools.partial(_rwkv_prep_kernel, n_lat_tiles=n_lat // tp, lat_tiles_per_seq=seq // tp,
                          ctx_tiles_per_seq=ctx_len // tp, aw=aw),
        grid=(r // tp,),
        in_specs=[
            pl.BlockSpec((tp, ncol), lambda i: (i, 0)),
            pl.BlockSpec((SUBLANE, ncol), lambda i: (jnp.maximum(i * hb - 1, 0), 0)),
            pl.BlockSpec((SUBLANE, ncol), lambda i: (jnp.minimum((i + 1) * hb, nblk8 - 1), 0)),
            full((2, ncol)), full((2, aw)), full((2, aw)),
            full((2, lw, aw)), full((2, lw, aw)), full((A_LORA_G, aw)),
            full((1, aw)), full((1, aw)), full((1, aw)), full((GROUP_W, GROUP_W)),
        ],
        out_specs=[pl.BlockSpec((tp, w), lambda i: (i, 0)) for w, _ in outs],
        out_shape=[jax.ShapeDtypeStruct((r, w), dt) for w, dt in outs],
        compiler_params=_cparams(("parallel",)),
    )(pa, pa, pa, mu_p, w0, a0, w2p, a2p, g2, row(k_k), row(k_a), row(r_k), e_bf)


def _rwkv_masks(reverse):
    L = CHUNK
    gw = GROUP_W
    row = lax.broadcasted_iota(jnp.int32, (gw, gw), 0)
    col = lax.broadcasted_iota(jnp.int32, (gw, gw), 1)
    same_head = (row >> 6) == (col >> 6)
    t_n = lax.broadcasted_iota(jnp.int32, (L, gw), 0)
    s_n = lax.broadcasted_iota(jnp.int32, (L, gw), 1) & (L - 1)
    strict_n = (t_n < s_n) if reverse else (t_n > s_n)
    incl_n = (t_n <= s_n) if reverse else (t_n >= s_n)
    t_r = lax.broadcasted_iota(jnp.int32, (L, L), 0)
    t_c = lax.broadcasted_iota(jnp.int32, (L, L), 1)
    tri = _bf(jnp.where((t_c >= t_r) if reverse else (t_c <= t_r), 1.0, 0.0))
    return dict(same_head=same_head, strict_n=strict_n, incl_n=incl_n, eye_n=t_n == s_n, tri=tri)


def _rwkv_chunks(probs, side=None):
    L = CHUNK
    gw = GROUP_W
    tick = (lambda: None) if side is None else (lambda: next(side, None))
    masks = {rv: _rwkv_masks(rv) for rv in sorted({p[7] for p in probs})}
    mk = [masks[p[7]] for p in probs]
    each = lambda f, *cols: [f(*args) for args in zip(*cols)]
    rep = lambda x: jnp.concatenate([x] * HEADS_PER_GROUP, axis=0)

    c = [_dot_exact_lhs(m["tri"], p[3]) for m, p in zip(mk, probs)]
    c_l = [ci[(0 if p[7] else L - 1):(1 if p[7] else L), :] for ci, p in zip(c, probs)]

    blocks = [lambda x, sh=m["same_head"]: _bf(jnp.where(sh, rep(x), 0.0)) for m in mk]
    an, rn, vn, v_bd, bk_end, btk_bd, e_l = [], [], [], [], [], [], []
    for (r, v, kk, lw, kd, b, _, _), bd, ci, cl in zip(probs, blocks, c, c_l):
        e_nc = jnp.exp(-ci)
        e_end = jnp.exp(cl - ci)
        an.append(_bf(-kk * jnp.exp(ci - lw)))
        rn.append(_bf(r * jnp.exp(ci)))
        btk_bd.append(jnp.concatenate([bd(b * e_nc), bd(kd * e_nc)], axis=0))
        vn.append(_bf(v))
        v_bd.append(bd(v))
        bk_end.append(jnp.concatenate([_bf(b * e_end), _bf(kd * e_end)], axis=0))
        e_l.append(jnp.exp(cl))

    stack = lambda x, z: jnp.concatenate([x, z], axis=0)
    a_pow, t, akr, r_b = [], [], [], []
    for m, a_n, r_n, bk in zip(mk, an, rn, btk_bd):
        sn = _dot_nt(stack(a_n, r_n), bk)
        a_ab = jnp.where(m["strict_n"], sn[:L, :gw], 0.0)
        a_pow.append(a_ab)
        t.append(jnp.where(m["eye_n"], 1.0, a_ab))
        akr.append(stack(_bf(jnp.where(m["strict_n"], sn[:L, gw:], 0.0)),
                         _bf(jnp.where(m["incl_n"], sn[L:, gw:], 0.0))))
        r_b.append(_bf(jnp.where(m["incl_n"], sn[L:, :gw], 0.0)))
    tick()
    kv = each(_dot, akr, v_bd)
    a_bd = [bd(a) for bd, a in zip(blocks, a_pow)]
    tb = [_bf(ti) for ti in t]
    a_pow = [_dot(_bf(a), ab) for a, ab in zip(a_pow, a_bd)]
    for level in range(5):
        a_bd = [bd(a) for bd, a in zip(blocks, a_pow)]
        tick()
        if level < 4:
            res = [_dot(stack(tbi, _bf(a)), ab) for tbi, a, ab in zip(tb, a_pow, a_bd)]
            t = [ti + ri[:L] for ti, ri in zip(t, res)]
            a_pow = [ri[L:] for ri in res]
        else:
            t = [ti + _dot(tbi, ab) for ti, tbi, ab in zip(t, tb, a_bd)]
        tb = [_bf(ti) for ti in t]
    h_bf = [_bf(p[6]) for p in probs]
    hx = [_dot(stack(a_n, r_n), h) for a_n, r_n, h in zip(an, rn, h_bf)]
    x_bd = [bd(hxi[:L] + kvi[:L]) for bd, hxi, kvi in zip(blocks, hx, kv)]
    tick()
    u = [_dot(tbi, xi) for tbi, xi in zip(tb, x_bd)]
    tick()
    y = [hxi[L:] + _dot(rb, bd(ui)) + kvi[L:]
         for bd, hxi, rb, ui, kvi in zip(blocks, hx, r_b, u, kv)]
    e_col = [jnp.broadcast_to(el, (gw, gw)).T for el in e_l]
    h_new = []
    for m, ec, p, bke, ui, v_n in zip(mk, e_col, probs, bk_end, u, vn):
        upd = _dot_tn(bke, jnp.concatenate([_bf(ui), v_n], axis=0))
        h_new.append(ec * p[6] + jnp.where(m["same_head"], upd, 0.0))
    if side is not None:
        for _ in side:
            pass
    return list(zip(y, h_new))


def _chunk_row_block(b, i, *, reverse, n_lat, seq, ctx_len):
    nc = ctx_len // CHUNK
    nl = seq // CHUNK
    ci = (nc - 1 - i) if reverse else i
    li = (nl - 1 - (i - nc)) if reverse else (i - nc)
    return jnp.where(i < nc, (n_lat + b * ctx_len) // CHUNK + ci, b * nl + li)


def _gla_stages(qkvf, gdf, qkvb, gdb, gw2_ref, gb_ref, of_ref, ob_ref, sf_ref, sb_ref):
    dv, dk = sf_ref.shape[1:]
    L = CHUNK
    wk = B_HEADS * dk
    t_r = lax.broadcasted_iota(jnp.int32, (L, L), 0)
    t_c = lax.broadcasted_iota(jnp.int32, (L, L), 1)

    qh, kh, keh, vh, dech, keeph, sth = [], [], [], [], [], [], []
    for d, (qkv_ref, gd_ref, st_ref) in enumerate(((qkvf, gdf, sf_ref), (qkvb, gdb, sb_ref))):
        reverse = d == 1
        keep = (t_c >= t_r) if reverse else (t_c <= t_r)
        tri = _bf(jnp.where(keep, 1.0, 0.0))
        z = _dot(_bf(gd_ref[...]), gw2_ref[d]) + gb_ref[d:d + 1, :]
        la = -_softplus(-z) * (1.0 / B_GATE_TAU)
        cum = _dot_exact_lhs(tri, la)
        last = 0 if reverse else L - 1
        cl = cum[last:last + 1, :]
        q_in = _bf(qkv_ref[:, :wk] * (dk ** -0.5) * jnp.exp(cum))
        k_in = _bf(qkv_ref[:, wk:2 * wk] * jnp.exp(-cum))
        k_end = _bf(qkv_ref[:, wk:2 * wk] * jnp.exp(cl - cum))
        dec = jnp.exp(cl)
        for h in range(B_HEADS):
            ks = slice(h * dk, (h + 1) * dk)
            qh.append(q_in[:, ks])
            kh.append(k_in[:, ks])
            keh.append(k_end[:, ks])
            vh.append(_bf(qkv_ref[:, 2 * wk + h * dv:2 * wk + (h + 1) * dv]))
            dech.append(dec[:, ks])
            keeph.append(keep)
            sth.append(st_ref[h])
        yield

    sc = [_bf(jnp.where(kp, _dot_nt(q, k), 0.0)) for kp, q, k in zip(keeph, qh, kh)]
    yield
    inter = [_dot_nt(q, _bf(st)) for q, st in zip(qh, sth)]
    yield
    kvt = [_dot_tn(v, ke) for v, ke in zip(vh, keh)]
    yield
    out = [_dot(s_, v) + it for s_, v, it in zip(sc, vh, inter)]
    yield
    for d, (o_ref, st_ref) in enumerate(((of_ref, sf_ref), (ob_ref, sb_ref))):
        for h in range(B_HEADS):
            i = d * B_HEADS + h
            o_ref[:, h * dv:(h + 1) * dv] = out[i]
            st_ref[h] = sth[i] * dech[i] + kvt[i]


def _mix_scan_kernel(rvkf, lwf, df, rvkb, lwb, db, qkvf, gdf, qkvb, gdb, gw2_ref, gb_ref,
                     yf_ref, yb_ref, of_ref, ob_ref, hf_ref, hb_ref, sf_ref, sb_ref):
    @pl.when(pl.program_id(1) == 0)
    def _():
        for ref in (hf_ref, hb_ref, sf_ref, sb_ref):
            ref[...] = jnp.zeros_like(ref)

    gs = hf_ref.shape[0]
    aw = lwf.shape[1]
    probs = []
    for gi in range(gs):
        sl = slice(gi * GROUP_W, (gi + 1) * GROUP_W)
        f32 = lambda ref, j: ref[:, j * aw + gi * GROUP_W:j * aw + (gi + 1) * GROUP_W].astype(F32)
        for rvk, lw, dd, h_ref, rev in ((rvkf, lwf, df, hf_ref, False), (rvkb, lwb, db, hb_ref, True)):
            probs.append((f32(rvk, 0), f32(rvk, 1), f32(rvk, 2), lw[:, sl], f32(dd, 0), f32(dd, 1),
                          h_ref[gi], rev))
    gla = _gla_stages(qkvf, gdf, qkvb, gdb, gw2_ref, gb_ref, of_ref, ob_ref, sf_ref, sb_ref)
    res = _rwkv_chunks(probs, side=gla)
    for gi in range(gs):
        sl = slice(gi * GROUP_W, (gi + 1) * GROUP_W)
        yf_ref[:, sl], hf_ref[gi] = res[2 * gi]
        yb_ref[:, sl], hb_ref[gi] = res[2 * gi + 1]


def _mix_scan(rvk, d0, d1, lw0, lw1, pb, gw2p, gb, *, batch, n_lat, seq, ctx_len, wk, wv, col0):
    rows, aw = lw0.shape
    assert aw == GROUP_W * RWKV_GROUPS_PER_STEP
    dk, dv = wk // B_HEADS, wv // B_HEADS
    qkv_w = 2 * wk + wv
    assert col0 % qkv_w == 0
    geo = dict(n_lat=n_lat, seq=seq, ctx_len=ctx_len)
    gd_blk = (col0 + 2 * wk + 2 * wv) // LANE

    def specs(reverse):
        rb = lambda b, i: _chunk_row_block(b, i, reverse=reverse, **geo)
        row_blk = lambda w, cb=0: pl.BlockSpec((CHUNK, w), lambda b, i: (rb(b, i), cb))
        rw = [row_blk(3 * aw), row_blk(aw), row_blk(2 * aw)]
        gla = [row_blk(qkv_w, col0 // qkv_w), row_blk(LANE, gd_blk)]
        return rw, gla, row_blk(aw), row_blk(wv)

    rwf, glaf, yf, of = specs(False)
    rwb, glab, yb, ob = specs(True)
    sds_a = jax.ShapeDtypeStruct((rows, aw), F32)
    sds_b = jax.ShapeDtypeStruct((rows, wv), F32)
    gs = RWKV_GROUPS_PER_STEP
    return pl.pallas_call(
        _mix_scan_kernel,
        grid=(batch, (seq + ctx_len) // CHUNK),
        in_specs=rwf + rwb + glaf + glab + [
            pl.BlockSpec((2, LANE, wk), lambda b, i: (0, 0, 0)),
            pl.BlockSpec((2, wk), lambda b, i: (0, 0)),
        ],
        out_specs=[yf, yb, of, ob],
        out_shape=[sds_a, sds_a, sds_b, sds_b],
        scratch_shapes=[pltpu.VMEM((gs, GROUP_W, GROUP_W), F32),
                        pltpu.VMEM((gs, GROUP_W, GROUP_W), F32),
                        pltpu.VMEM((B_HEADS, dv, dk), F32), pltpu.VMEM((B_HEADS, dv, dk), F32)],
        compiler_params=_cparams(("parallel", "arbitrary")),
    )(rvk, lw0, d0, rvk, lw1, d1, pb, pb, pb, pb, gw2p, gb)


def _mixer_out_kernel(yf, yb, bonus, g, of, ob, gate, xa_ref, xb_ref, mod_ref, lnw, lnb, gn, e_ref,
                      w_ref, o_ref, h_ref, *, aw, dv, n_a_tiles):
    e_bf = e_ref[...]
    y = yf[...] + yb[...]
    inv = 1.0 / A_HEAD_DIM
    mean = _head_sum(y, e_bf) * inv
    dlt = y - mean
    var = _head_sum(dlt * dlt, e_bf) * inv
    yn = dlt * lax.rsqrt(var + A_LN_EPS) * lnw[...] + lnb[...]
    h_ref[:, :aw] = _bf((yn + bonus[...]) * g[...])

    o = of[...] + ob[...]
    gt = _silu(gate[...])
    for h in range(B_HEADS):
        sl = slice(h * dv, (h + 1) * dv)
        oh = o[:, sl]
        ms = jnp.mean(oh * oh, axis=-1, keepdims=True)
        h_ref[:, aw + h * dv:aw + (h + 1) * dv] = _bf(oh * lax.rsqrt(ms + NORM_EPS) * gn[...] * gt[:, sl])

    x = jnp.where(pl.program_id(0) < n_a_tiles, xa_ref[...], xb_ref[...])
    o_ref[...] = x + mod_ref[2:3, :] * _dot(h_ref[...], w_ref[...])


def _mixer_out(yf, yb, bonus, g, of, ob, pb, xa, xb, mods_l, ln_w, ln_b, gla_norm, e_bf, w_bf,
               seg_of_tile, *, wk, col0):
    d = xa.shape[1]
    rows = xa.shape[0] + xb.shape[0]
    aw = yf.shape[1]
    wv = of.shape[1]
    dv = wv // B_HEADS
    tm = 256
    tiles_per_row_tile = ROW_TILE // tm
    n_a_tiles = xa.shape[0] // tm
    rs = lambda w: pl.BlockSpec((tm, w), lambda i: (i, 0))
    full = lambda shape: pl.BlockSpec(shape, lambda i: (0,) * len(shape))
    return pl.pallas_call(
        functools.partial(_mixer_out_kernel, aw=aw, dv=dv, n_a_tiles=n_a_tiles),
        grid=(rows // tm,),
        in_specs=[
            rs(aw), rs(aw), rs(aw), rs(aw), rs(wv), rs(wv),
            pl.BlockSpec((tm, wv), lambda i: (i, (col0 + 2 * wk + wv) // wv)),
        ] + _two_source_specs(tm, d, n_a_tiles) + [
            pl.BlockSpec((None, N_MOD, d), lambda i: (seg_of_tile(i // tiles_per_row_tile), 0, 0)),
            full((1, aw)), full((1, aw)), full((1, dv)), full((GROUP_W, GROUP_W)),
            full((aw + wv, d)),
        ],
        out_specs=rs(d),
        out_shape=jax.ShapeDtypeStruct((rows, d), F32),
        scratch_shapes=[pltpu.VMEM((tm, aw + wv), BF16)],
        compiler_params=_cparams(("parallel",)),
    )(yf, yb, bonus, g, of, ob, pb, xa, xb, mods_l, ln_w.reshape(1, aw), ln_b.reshape(1, aw),
      gla_norm.reshape(1, dv), e_bf, w_bf)


def _lru_kernel(*refs, reverse, n_blk, combine, cols):
    if combine:
        (x_ref, xp_ref, xn_ref, h0_ref, cw_ref, cb_ref, wc_ref, ba_ref, bx_ref, lam_ref,
         hf_ref, gate_ref, out_ref, hl_ref, a_sc, u_sc, hs_sc, carry_sc) = refs
    else:
        (x_ref, xp_ref, xn_ref, h0_ref, cw_ref, cb_ref, wc_ref, ba_ref, bx_ref, lam_ref,
         out_ref, hl_ref, a_sc, u_sc, hs_sc, carry_sc) = refs
    j = pl.program_id(2)
    blk = (n_blk - 1 - j) if reverse else j
    lb = x_ref.shape[0]
    ct = cw_ref.shape[1]
    cblk = wc_ref.shape[1]
    rowid = lax.broadcasted_iota(jnp.int32, (lb, 1), 0)
    decay_rate = -C_CONST * _softplus(-lam_ref[...])

    @pl.when(j == 0)
    def _():
        carry_sc[...] = h0_ref[0:1, :]

    for q in (range(cols - 1, -1, -1) if reverse else range(cols)):
        lanes = slice(q * ct, (q + 1) * ct)
        x = x_ref[:, lanes]
        if q > 0:
            prev = slice((q - 1) * ct, q * ct)
            p6, p7 = x_ref[lb - 2:lb - 1, prev], x_ref[lb - 1:lb, prev]
        else:
            p6 = jnp.where(blk == 0, 0.0, xp_ref[SUBLANE - 2:SUBLANE - 1, :])
            p7 = jnp.where(blk == 0, 0.0, xp_ref[SUBLANE - 1:SUBLANE, :])
        if q < cols - 1:
            n0 = x_ref[0:1, (q + 1) * ct:(q + 2) * ct]
        else:
            n0 = jnp.where(blk == n_blk - 1, 0.0, xn_ref[0:1, :])
        xm1 = jnp.where(rowid == 0, p7, pltpu.roll(x, 1, 0))
        xm2 = jnp.where(rowid == 0, p6, jnp.where(rowid == 1, p7, pltpu.roll(x, 2, 0)))
        xp1 = jnp.where(rowid == lb - 1, n0, pltpu.roll(x, lb - 1, 0))
        xs = (cb_ref[...] + cw_ref[0:1, :] * xm2 + cw_ref[1:2, :] * xm1 + cw_ref[2:3, :] * x
              + cw_ref[3:4, :] * xp1)

        for n in range(ct // cblk):
            sl = slice(n * cblk, (n + 1) * cblk)
            xb = xs[:, sl]
            ri = _dot(_bf(xb), wc_ref[n])
            rg = _sigmoid(ri[:, :cblk] + ba_ref[:, sl])
            ig = _sigmoid(ri[:, cblk:] + bx_ref[:, sl])
            a = jnp.exp(decay_rate[:, sl] * rg)
            a_sc[q, :, sl] = a
            u_sc[q, :, sl] = jnp.sqrt(1.0 - a * a) * (ig * xb)

        def step(t, h, q=q):
            tt = (lb - 1 - t) if reverse else t
            h = a_sc[q, pl.ds(tt, 1), :] * h + u_sc[q, pl.ds(tt, 1), :]
            hs_sc[q, pl.ds(tt, 1), :] = h
            return h

        carry_sc[...] = lax.fori_loop(0, lb, step, carry_sc[...], unroll=8)
        if combine:
            out_ref[:, lanes] = ((hs_sc[q] + hf_ref[q * lb:(q + 1) * lb, :])
                                 * jax.nn.gelu(gate_ref[:, lanes]))
        else:
            out_ref[q * lb:(q + 1) * lb, :] = hs_sc[q]
    hl_ref[...] = jnp.broadcast_to(carry_sc[...], hl_ref.shape)


def _lru_pass(x_arr, x_specs, h0, conv_w, conv_b, wcat, ba, bx, lam, out_shape, out_spec, extra,
              *, batch, n_blk, lb, cols, reverse):
    c = conv_w.shape[1]
    ct = LRU_CHANNEL_TILE
    nct = c // ct
    cblk = wcat.shape[1]
    chan = lambda rows_: pl.BlockSpec((rows_, ct), lambda b, k, j: (0, k))
    in_specs = list(x_specs) + [
        pl.BlockSpec((SUBLANE, ct), lambda b, k, j: (b, k)),
        chan(C_CONV), chan(1),
        pl.BlockSpec((ct // cblk, cblk, 2 * cblk), lambda b, k, j: (k, 0, 0)),
        chan(1), chan(1), chan(1),
    ] + [s for _, s in extra]
    args = [x_arr, x_arr, x_arr, h0, conv_w, conv_b.reshape(1, c), wcat, ba.reshape(1, c),
            bx.reshape(1, c), lam.reshape(1, c)] + [a for a, _ in extra]
    return pl.pallas_call(
        functools.partial(_lru_kernel, reverse=reverse, n_blk=n_blk, combine=bool(extra),
                          cols=cols),
        grid=(batch, nct, n_blk),
        in_specs=in_specs,
        out_specs=[out_spec, pl.BlockSpec((SUBLANE, ct), lambda b, k, j: (b, k))],
        out_shape=[out_shape, jax.ShapeDtypeStruct((batch * SUBLANE, c), F32)],
        scratch_shapes=[pltpu.VMEM((cols, lb, ct), F32), pltpu.VMEM((cols, lb, ct), F32),
                        pltpu.VMEM((cols, lb, ct), F32), pltpu.VMEM((1, ct), F32)],
        compiler_params=_cparams(("parallel", "parallel", "arbitrary")),
    )(*args)


def _rglru(hw_ctx, gate_grid, x_grid, conv_w, conv_b, wa, ba, wx, bx, lam, *, batch, n_lat, seq,
           ctx_len):
    c = conv_w.shape[1]
    ct = LRU_CHANNEL_TILE
    rows = seq // GRID_W
    lbc = 128
    ncb = ctx_len // lbc
    xcol = c // ct

    wcat = [_bf(jnp.concatenate([wa[d], wx[d]], axis=-1)) for d in range(2)]
    zeros_state = jnp.zeros((batch * SUBLANE, c), F32)

    def ctx_specs(reverse):
        blk = (lambda j: ncb - 1 - j) if reverse else (lambda j: j)
        per8 = lbc // SUBLANE
        cur = pl.BlockSpec((lbc, ct), lambda b, k, j: (b * ncb + blk(j), xcol + k))
        prv = pl.BlockSpec((SUBLANE, ct), lambda b, k, j: (
            b * ncb * per8 + jnp.maximum(blk(j) * per8 - 1, 0), xcol + k))
        nxt = pl.BlockSpec((SUBLANE, ct), lambda b, k, j: (
            b * ncb * per8 + jnp.minimum((blk(j) + 1) * per8, ncb * per8 - 1), xcol + k))
        return cur, prv, nxt

    assert ct == c
    def lat_specs(reverse, kc):
        ncg = GRID_W // kc
        grp = (lambda j: ncg - 1 - j) if reverse else (lambda j: j)
        r8 = rows // SUBLANE
        cur = pl.BlockSpec((rows, kc * ct), lambda b, k, j: (b, grp(j)))
        prv = pl.BlockSpec((SUBLANE, ct), lambda b, k, j: (
            b * r8 + r8 - 1, jnp.maximum(grp(j) * kc - 1, 0)))
        nxt = pl.BlockSpec((SUBLANE, ct), lambda b, k, j: (
            b * r8, jnp.minimum(grp(j) * kc + kc, GRID_W - 1)))
        hf = pl.BlockSpec((kc * rows, ct), lambda b, k, j: (b * ncg + grp(j), 0))
        return (cur, prv, nxt), hf, cur

    common = dict(batch=batch)
    ctx_scratch_sds = jax.ShapeDtypeStruct((batch * ctx_len, c), F32)
    ctx_out = lambda reverse: pl.BlockSpec(
        (lbc, ct), lambda b, k, j: (b * ncb + ((ncb - 1 - j) if reverse else j), k))
    _, st = _lru_pass(hw_ctx, ctx_specs(False), zeros_state, conv_w, conv_b, wcat[0], ba[0], bx[0],
                      lam[0], ctx_scratch_sds, ctx_out(False), [], n_blk=ncb, lb=lbc, cols=1,
                      reverse=False, **common)
    hf_sds = jax.ShapeDtypeStruct((n_lat, c), F32)
    kc = LRU_COLS_FWD
    x_specs, hf_spec, _ = lat_specs(False, kc)
    hf, _ = _lru_pass(x_grid, x_specs, st, conv_w, conv_b, wcat[0], ba[0], bx[0], lam[0], hf_sds,
                      hf_spec, [], n_blk=GRID_W // kc, lb=rows, cols=kc, reverse=False, **common)
    _, st = _lru_pass(hw_ctx, ctx_specs(True), zeros_state, conv_w, conv_b, wcat[1], ba[1], bx[1],
                      lam[1], ctx_scratch_sds, ctx_out(True), [], n_blk=ncb, lb=lbc, cols=1,
                      reverse=True, **common)
    kc = LRU_COLS_BWD
    x_specs, hf_spec, grid_spec = lat_specs(True, kc)
    y_sds = jax.ShapeDtypeStruct((n_lat // GRID_W, GRID_W * c), F32)
    y, _ = _lru_pass(x_grid, x_specs, st, conv_w, conv_b, wcat[1], ba[1], bx[1], lam[1],
                     y_sds, grid_spec, [(hf, hf_spec), (gate_grid, grid_spec)],
                     n_blk=GRID_W // kc, lb=rows, cols=kc, reverse=True, **common)
    return y


def _pad_cols(w, n):
    return jnp.pad(w, ((0, 0),) * (w.ndim - 1) + ((0, n - w.shape[-1]),))


def kernel(x, c, ctx, c_ctx, mod_w, mod_b, norm1, norm2, mlp_w1, mlp_w2, ab_w_in, ab_w_out, rw_mu,
           rw_w0, rw_w2, rw_a0, rw_a2, rw_g2, rw_kk, rw_ka, rw_rk, rw_ln_w, rw_ln_b, gla_gw2,
           gla_gb, gla_norm, lru_w_in, lru_w_out, lru_conv_w, lru_conv_b, lru_wa, lru_ba, lru_wx,
           lru_bx, lru_lam, final_norm):
    batch, seq, d = x.shape
    ctx_len = ctx.shape[1]
    depth = mod_w.shape[0]
    n_lat = batch * seq
    aw = rw_w0.shape[-1]
    wk = gla_gb.shape[-1]
    wv = ab_w_out.shape[1] - aw
    a_cols = rw_mu.shape[-1]
    b_cols = ab_w_in.shape[-1] - a_cols
    assert depth == 2 and batch + 1 <= SUBLANE
    assert seq % ROW_TILE == 0 and (batch * ctx_len) % ROW_TILE == 0
    assert ctx_len % PREP_TILE == 0 and ctx_len % 128 == 0 and seq % (GRID_W * SUBLANE) == 0
    assert aw % GROUP_W == 0 and wv == aw and 2 * wk == wv

    tiles_per_seq = seq // ROW_TILE
    seg_of_tile = lambda i: jnp.minimum(i // tiles_per_seq, batch)

    x2 = x.reshape(n_lat, d)
    ctx2 = ctx.reshape(batch * ctx_len, d)
    cvec = jnp.concatenate(
        [c, c_ctx[None, :], jnp.zeros((SUBLANE - batch - 1, d), F32)], axis=0)
    mods = _mods(cvec, mod_w, mod_b)

    hid = jnp.arange(GROUP_W) // A_HEAD_DIM
    e_bf = _bf(hid[:, None] == hid[None, :])

    a_pad = -(-a_cols // LANE) * LANE
    col0 = -(-a_cols // (2 * wk)) * (2 * wk)
    n_tiles = 3
    tn = -(-(col0 + b_cols) // (n_tiles * LANE)) * LANE
    w_ab = _bf(jnp.concatenate(
        [_pad_cols(ab_w_in[0][:, :a_cols], col0),
         _pad_cols(ab_w_in[0][:, a_cols:], n_tiles * tn - col0)], axis=1))
    pa = pb = _norm_matmul(x2, ctx2, norm1[0], mods[0], w_ab, seg_of_tile, tn=tn)

    lora_w = 2 * A_LORA_W + 2 * A_LORA_A
    w2p = jnp.zeros((2, lora_w, aw), F32)
    a2p = jnp.zeros((2, lora_w, aw), F32)
    for dd in range(2):
        w2p = w2p.at[dd, dd * A_LORA_W:(dd + 1) * A_LORA_W].set(rw_w2[0, dd])
        a2p = a2p.at[dd, 2 * A_LORA_W + dd * A_LORA_A:2 * A_LORA_W + (dd + 1) * A_LORA_A].set(
            rw_a2[0, dd])
    (rvk, d0, d1, lw0, lw1, g_, bonus) = _rwkv_prep(
        pa, _pad_cols(rw_mu[0], a_pad), rw_w0[0], rw_a0[0], _bf(w2p), _bf(a2p), _bf(rw_g2[0]),
        rw_kk[0], rw_ka[0], rw_rk[0].reshape(-1), e_bf, n_lat=n_lat, seq=seq, ctx_len=ctx_len)
    gw2p = jnp.zeros((2, LANE, wk), F32)
    for dd in range(2):
        gw2p = gw2p.at[dd, dd * B_LORA:(dd + 1) * B_LORA].set(gla_gw2[0, dd])
    yf, yb, of, ob = _mix_scan(rvk, d0, d1, lw0, lw1, pb, _bf(gw2p), gla_gb[0],
                               batch=batch, n_lat=n_lat, seq=seq, ctx_len=ctx_len, wk=wk, wv=wv,
                               col0=col0)

    xs = _mixer_out(yf, yb, bonus, g_, of, ob, pb, x2, ctx2, mods[0], rw_ln_w[0], rw_ln_b[0],
                    gla_norm[0], e_bf, _bf(ab_w_out[0]), seg_of_tile, wk=wk, col0=col0)
    seg_of_mlp_tile = lambda i: jnp.minimum(i // (seq // MLP_TILE), batch)
    ctx_tiles = (batch * ctx_len) // ROW_TILE
    w1_bf, w2_bf = _bf(mlp_w1[0]), _bf(mlp_w2[0])
    x_lat = _mlp(xs, norm2[0], mods[0], w1_bf, w2_bf, final_norm, seg_of_mlp_tile, tm=MLP_TILE,
                 tile0=0, n_tiles=n_lat // MLP_TILE, final_norm=False)
    x_ctx = _mlp(xs, norm2[0], mods[0], w1_bf, w2_bf, final_norm, seg_of_tile, tm=ROW_TILE,
                 tile0=n_lat // ROW_TILE, n_tiles=ctx_tiles, final_norm=False)

    w_lru = _bf(lru_w_in[0])
    hw_ctx = _norm_matmul(x_ctx, None, norm1[1], mods[1], w_lru, lambda i: batch, tn=2048)
    gate_grid, x_grid = _norm_matmul_grid(x_lat, n_lat, seq // GRID_W, norm1[1], mods[1], w_lru)
    y = _rglru(hw_ctx, gate_grid, x_grid, lru_conv_w[0], lru_conv_b[0], lru_wa[0], lru_ba[0],
               lru_wx[0], lru_bx[0], lru_lam[0], batch=batch, n_lat=n_lat, seq=seq,
               ctx_len=ctx_len)
    xl = _proj_res(y, x_lat, n_lat, seq // GRID_W, mods[1], _bf(lru_w_out[0]))
    out = _mlp(xl, norm2[1], mods[1], _bf(mlp_w1[1]), _bf(mlp_w2[1]), final_norm, seg_of_mlp_tile,
               tm=MLP_TILE, tile0=0, n_tiles=n_lat // MLP_TILE, final_norm=True)
    return out.reshape(batch, seq, d)
```

```python
import functools

import jax
import jax.numpy as jnp
from jax import lax
from jax.experimental import pallas as pl
from jax.experimental.pallas import tpu as pltpu

F32 = jnp.float32
BF16 = jnp.bfloat16

NORM_EPS = 1e-6
GRID_W = 64
N_MOD = 6

A_HEAD_DIM = 64
A_LORA_W = 96
A_LORA_A = 96
A_LORA_G = 256
A_LN_EPS = 64e-5
HEADS_PER_GROUP = 4
GROUP_W = HEADS_PER_GROUP * A_HEAD_DIM
CHUNK = 64
RWKV_GROUPS_PER_STEP = 4

B_HEADS = 4
B_LORA = 16
B_GATE_TAU = 16.0

C_BLOCKS = 8
C_CONV = 4
C_CONST = 8.0

LANE = 128
SUBLANE = 8
ROW_TILE = 512
MLP_TILE = 1024
MLP_STEP_ELEMS = 1024 * 512
LRU_CHANNEL_TILE = 2048
LRU_COLS_FWD = 4
LRU_COLS_BWD = 1
CAST_BLOCK_ELEMS = 2 * 1024 * 1024
PREP_TILE = 256
VMEM_LIMIT = 56 * 1024 * 1024


def _cparams(sem):
    return pltpu.CompilerParams(dimension_semantics=sem, vmem_limit_bytes=VMEM_LIMIT)


def _bf(x):
    return x.astype(BF16)


def _dot(a, b):
    return jnp.dot(a, b, preferred_element_type=F32)


def _dot_nt(a, b):
    return lax.dot_general(a, b, (((1,), (1,)), ((), ())), preferred_element_type=F32)


def _dot_tn(a, b):
    return lax.dot_general(a, b, (((0,), (0,)), ((), ())), preferred_element_type=F32)


def _split3(x):
    hi = _bf(x)
    r1 = x - hi.astype(F32)
    mid = _bf(r1)
    lo = _bf(r1 - mid.astype(F32))
    return hi, mid, lo


def _dot_exact_lhs(m_bf, x):
    hi, mid, lo = _split3(x)
    return _dot(m_bf, hi) + _dot(m_bf, mid) + _dot(m_bf, lo)


def _dot_exact_rhs(x, m_bf):
    hi, mid, lo = _split3(x)
    return _dot(hi, m_bf) + _dot(mid, m_bf) + _dot(lo, m_bf)


def _softplus(x):
    return jnp.maximum(x, 0.0) + jnp.log(1.0 + jnp.exp(-jnp.abs(x)))


def _sigmoid(x):
    return 0.5 * jnp.tanh(0.5 * x) + 0.5


def _silu(x):
    return x * _sigmoid(x)


def _norm_mod(x, g, shift, scale):
    ms = jnp.mean(x * x, axis=-1, keepdims=True)
    return (x * lax.rsqrt(ms + NORM_EPS) * g) * (1.0 + scale) + shift


def _cast_kernel(w_ref, o_ref):
    o_ref[...] = _bf(w_ref[...])


def _layer_bf16(w, layer):
    _, r, c = w.shape
    tr = min(r, CAST_BLOCK_ELEMS // c)
    return pl.pallas_call(
        _cast_kernel,
        grid=(r // tr,),
        in_specs=[pl.BlockSpec((None, tr, c), lambda i: (layer, i, 0))],
        out_specs=pl.BlockSpec((tr, c), lambda i: (i, 0)),
        out_shape=jax.ShapeDtypeStruct((r, c), BF16),
        compiler_params=_cparams(("parallel",)),
    )(w)


def _mods_kernel(c_ref, w_ref, b_ref, o_ref):
    s = _silu(c_ref[...])
    o_ref[...] = _dot(_bf(s), _bf(w_ref[...])) + b_ref[...]


def _mods(cvec, mod_w, mod_b):
    depth, d, n = mod_w.shape
    tn = 1024
    out = pl.pallas_call(
        _mods_kernel,
        grid=(depth, n // tn),
        in_specs=[
            pl.BlockSpec((SUBLANE, d), lambda l, j: (0, 0)),
            pl.BlockSpec((None, d, tn), lambda l, j: (l, 0, j)),
            pl.BlockSpec((None, 1, tn), lambda l, j: (l, 0, j)),
        ],
        out_specs=pl.BlockSpec((None, SUBLANE, tn), lambda l, j: (l, 0, j)),
        out_shape=jax.ShapeDtypeStruct((depth, SUBLANE, n), F32),
        compiler_params=_cparams(("parallel", "parallel")),
    )(cvec, mod_w, mod_b.reshape(depth, 1, n))
    return out.reshape(depth, SUBLANE, N_MOD, d)


def _norm_matmul_kernel(xa_ref, xb_ref, g_ref, mod_ref, w_ref, o_ref, h_ref, *, n_a_tiles):
    i = pl.program_id(0)

    @pl.when(pl.program_id(1) == 0)
    def _():
        x = jnp.where(i < n_a_tiles, xa_ref[...], xb_ref[...])
        h_ref[...] = _bf(_norm_mod(x, g_ref[...], mod_ref[0:1, :], mod_ref[1:2, :]))

    o_ref[...] = _dot(h_ref[...], w_ref[...])


def _two_source_specs(tm, d, n_a_tiles):
    return [pl.BlockSpec((tm, d), lambda i, *_: (jnp.minimum(i, n_a_tiles - 1), 0)),
            pl.BlockSpec((tm, d), lambda i, *_: (jnp.maximum(i - n_a_tiles, 0), 0))]


def _norm_matmul(xa, xb, g, mods_l, w_bf, seg_of_tile, *, tn):
    d = xa.shape[1]
    n = w_bf.shape[1]
    tm = ROW_TILE
    n_a_tiles = xa.shape[0] // tm
    n_tiles = n_a_tiles + (0 if xb is None else xb.shape[0] // tm)
    return pl.pallas_call(
        functools.partial(_norm_matmul_kernel, n_a_tiles=n_a_tiles),
        grid=(n_tiles, n // tn),
        in_specs=_two_source_specs(tm, d, n_a_tiles) + [
            pl.BlockSpec((1, d), lambda i, j: (0, 0)),
            pl.BlockSpec((None, N_MOD, d), lambda i, j: (seg_of_tile(i), 0, 0)),
            pl.BlockSpec((d, tn), lambda i, j: (0, j)),
        ],
        out_specs=pl.BlockSpec((tm, tn), lambda i, j: (i, j)),
        out_shape=jax.ShapeDtypeStruct((n_tiles * tm, n), F32),
        scratch_shapes=[pltpu.VMEM((tm, d), BF16)],
        compiler_params=_cparams(("parallel", "arbitrary")),
    )(xa, xa if xb is None else xb, g.reshape(1, d), mods_l, w_bf)


def _grid_tile(rows):
    gr = min(rows, 64)
    gc = ROW_TILE // gr
    assert gc % SUBLANE == 0 and GRID_W % gc == 0 and rows % gr == 0
    return gr, gc


def _norm_matmul_grid_kernel(x_ref, g_ref, mod_ref, w_ref, o0_ref, o1_ref, h_ref, res_ref):
    j = pl.program_id(1)
    gr, gc, d = x_ref.shape

    @pl.when(j == 0)
    def _():
        h = _norm_mod(x_ref[...].reshape(gr * gc, d), g_ref[...], mod_ref[0:1, :], mod_ref[1:2, :])
        h_ref[...] = _bf(h)

    res = _dot(h_ref[...], w_ref[...])
    nh = w_ref.shape[1]
    nq = nh // LANE
    for q in range(nq):
        res_ref[q] = res[:, q * LANE:(q + 1) * LANE]

    def scatter(o_ref):
        for c in range(gc):
            for q in range(nq):
                o_ref[:, c * nh + q * LANE:c * nh + (q + 1) * LANE] = (
                    res_ref[q, pl.ds(c, gr, stride=gc), :])

    @pl.when(j == 0)
    def _():
        scatter(o0_ref)

    @pl.when(j == 1)
    def _():
        scatter(o1_ref)


def _norm_matmul_grid(x, n_lat, rows, g, mods_l, w_bf):
    d = x.shape[1]
    nh = w_bf.shape[1] // 2
    gr, gc = _grid_tile(rows)
    ncg = GRID_W // gc
    x3 = x.reshape(x.shape[0] // GRID_W, GRID_W, d)
    out_spec = pl.BlockSpec((gr, gc * nh), lambda i, j: (i // ncg, i % ncg))
    sds = jax.ShapeDtypeStruct((n_lat // GRID_W, GRID_W * nh), F32)
    return pl.pallas_call(
        _norm_matmul_grid_kernel,
        grid=(n_lat // ROW_TILE, 2),
        in_specs=[
            pl.BlockSpec((gr, gc, d), lambda i, j: (i // ncg, i % ncg, 0)),
            pl.BlockSpec((1, d), lambda i, j: (0, 0)),
            pl.BlockSpec((None, N_MOD, d), lambda i, j: ((i // ncg) * gr // rows, 0, 0)),
            pl.BlockSpec((d, nh), lambda i, j: (0, j)),
        ],
        out_specs=[out_spec, out_spec],
        out_shape=[sds, sds],
        scratch_shapes=[pltpu.VMEM((ROW_TILE, d), BF16),
                        pltpu.VMEM((nh // LANE, ROW_TILE, LANE), F32)],
        compiler_params=_cparams(("parallel", "arbitrary")),
    )(x3, g.reshape(1, d), mods_l, w_bf)


def _mlp_kernel(x_ref, g_ref, mod_ref, w1_ref, w2_ref, fn_ref, o_ref, h_ref, *, final_norm):
    j = pl.program_id(1)
    slab = 256
    n_slab = x_ref.shape[0] // slab

    @pl.when(j == 0)
    def _():
        for s in range(n_slab):
            rs = slice(s * slab, (s + 1) * slab)
            h_ref[rs, :] = _bf(_norm_mod(x_ref[rs, :], g_ref[...], mod_ref[3:4, :],
                                         mod_ref[4:5, :]))
        o_ref[...] = jnp.zeros_like(o_ref)

    a = _dot(h_ref[...], w1_ref[...])
    a = _bf(jnp.square(jnp.maximum(a, 0.0)))
    nsplit = 2
    wn = o_ref.shape[1] // nsplit
    for s in range(nsplit):
        o_ref[:, s * wn:(s + 1) * wn] += _dot(a, w2_ref[:, s * wn:(s + 1) * wn])

    @pl.when(j == pl.num_programs(1) - 1)
    def _():
        for s in range(n_slab):
            rs = slice(s * slab, (s + 1) * slab)
            xo = x_ref[rs, :] + mod_ref[5:6, :] * o_ref[rs, :]
            if final_norm:
                ms = jnp.mean(xo * xo, axis=-1, keepdims=True)
                xo = xo * lax.rsqrt(ms + NORM_EPS) * fn_ref[...]
            o_ref[rs, :] = xo


def _mlp(x, g, mods_l, w1_bf, w2_bf, fn, seg_of_tile, *, tm, tile0, n_tiles, final_norm):
    d = x.shape[1]
    dff = w1_bf.shape[1]
    tf = MLP_STEP_ELEMS // tm
    return pl.pallas_call(
        functools.partial(_mlp_kernel, final_norm=final_norm),
        grid=(n_tiles, dff // tf),
        in_specs=[
            pl.BlockSpec((tm, d), lambda i, j: (i + tile0, 0), pipeline_mode=pl.Buffered(1)),
            pl.BlockSpec((1, d), lambda i, j: (0, 0)),
            pl.BlockSpec((None, N_MOD, d), lambda i, j: (seg_of_tile(i + tile0), 0, 0)),
            pl.BlockSpec((d, tf), lambda i, j: (0, j)),
            pl.BlockSpec((tf, d), lambda i, j: (j, 0)),
            pl.BlockSpec((1, d), lambda i, j: (0, 0)),
        ],
        out_specs=pl.BlockSpec((tm, d), lambda i, j: (i, 0)),
        out_shape=jax.ShapeDtypeStruct((n_tiles * tm, d), F32),
        scratch_shapes=[pltpu.VMEM((tm, d), BF16)],
        compiler_params=_cparams(("parallel", "arbitrary")),
    )(x, g.reshape(1, d), mods_l, w1_bf, w2_bf, fn.reshape(1, d))


def _proj_res_kernel(y_ref, x_ref, mod_ref, w_ref, o_ref, yt_ref):
    k = w_ref.shape[0]
    gr, gc, d = x_ref.shape
    nq = k // LANE
    for c in range(gc):
        for q in range(nq):
            yt_ref[q, pl.ds(c, gr, stride=gc), :] = (
                y_ref[:, c * k + q * LANE:c * k + (q + 1) * LANE])
    yt = jnp.concatenate([_bf(yt_ref[q]) for q in range(nq)], axis=1)
    out = x_ref[...].reshape(gr * gc, d) + mod_ref[2:3, :] * _dot(yt, w_ref[...])
    o_ref[...] = out.reshape(gr, gc, d)


def _proj_res(y_grid, x, n_lat, rows, mods_l, w_bf):
    k, d = w_bf.shape
    gr, gc = _grid_tile(rows)
    ncg = GRID_W // gc
    x3 = x.reshape(x.shape[0] // GRID_W, GRID_W, d)
    tok_spec = pl.BlockSpec((gr, gc, d), lambda i: (i // ncg, i % ncg, 0))
    out = pl.pallas_call(
        _proj_res_kernel,
        grid=(n_lat // ROW_TILE,),
        in_specs=[
            pl.BlockSpec((gr, gc * k), lambda i: (i // ncg, i % ncg)),
            tok_spec,
            pl.BlockSpec((None, N_MOD, d), lambda i: ((i // ncg) * gr // rows, 0, 0)),
            pl.BlockSpec((k, d), lambda i: (0, 0)),
        ],
        out_specs=tok_spec,
        out_shape=jax.ShapeDtypeStruct((n_lat // GRID_W, GRID_W, d), F32),
        scratch_shapes=[pltpu.VMEM((k // LANE, ROW_TILE, LANE), F32)],
        compiler_params=_cparams(("parallel",)),
    )(y_grid, x3, mods_l, w_bf)
    return out.reshape(n_lat, d)


def _head_sum(z, e_bf):
    parts = []
    for gi in range(z.shape[1] // GROUP_W):
        zs = z[:, gi * GROUP_W:(gi + 1) * GROUP_W]
        parts.append(_dot_exact_rhs(zs, e_bf))
    return jnp.concatenate(parts, axis=1)


def _rwkv_prep_kernel(p_ref, pp_ref, pn_ref, mu_ref, w0_ref, a0_ref, w2_ref, a2_ref, g2_ref,
                      kk_ref, ka_ref, rk_ref, e_ref,
                      rvk_o, d0_o, d1_o, lw0_o, lw1_o, g_o, bonus_o,
                      *, n_lat_tiles, lat_tiles_per_seq, ctx_tiles_per_seq, aw):
    i = pl.program_id(0)
    tp = p_ref.shape[0]
    in_ctx = i >= n_lat_tiles
    pos = jnp.where(in_ctx, (i - n_lat_tiles) % ctx_tiles_per_seq, i % lat_tiles_per_seq)
    last = jnp.where(in_ctx, ctx_tiles_per_seq - 1, lat_tiles_per_seq - 1)
    is_first = pos == 0
    is_last = pos == last
    rowid = lax.broadcasted_iota(jnp.int32, (tp, 1), 0)

    def shifted(c0, c1):
        x = p_ref[:, c0:c1]
        prow = jnp.where(is_first, 0.0, pp_ref[SUBLANE - 1:SUBLANE, c0:c1])
        nrow = jnp.where(is_last, 0.0, pn_ref[0:1, c0:c1])
        xp = jnp.where(rowid == 0, prow, pltpu.roll(x, 1, 0))
        xn = jnp.where(rowid == tp - 1, nrow, pltpu.roll(x, tp - 1, 0))
        return x + mu_ref[0:1, c0:c1] * (xp - x) + mu_ref[1:2, c0:c1] * (xn - x)

    e_bf = e_ref[...]
    r = shifted(0, aw)
    k = shifted(aw, 2 * aw)
    v = shifted(2 * aw, 3 * aw)
    gd = shifted(3 * aw, 3 * aw + A_LORA_G)
    lo0 = 3 * aw + A_LORA_G
    xl = shifted(lo0, lo0 + 2 * A_LORA_W + 2 * A_LORA_A)

    rvk_o[:, :aw] = r.astype(rvk_o.dtype)
    rvk_o[:, aw:2 * aw] = v.astype(rvk_o.dtype)
    g_o[...] = _dot(_bf(_sigmoid(gd)), g2_ref[...])

    kx = k * kk_ref[...]
    kn = jnp.sqrt(_head_sum(kx * kx, e_bf))
    kk = kx / jnp.maximum(kn, 1e-12)
    rvk_o[:, 2 * aw:] = kk.astype(rvk_o.dtype)

    xl_t = _bf(jnp.tanh(xl))
    xl_b = _bf(xl)
    kd_sum = None
    for d, (lw_o, d_o) in enumerate(((lw0_o, d0_o), (lw1_o, d1_o))):
        w_log = -_softplus(-(w0_ref[d:d + 1, :] + _dot(xl_t, w2_ref[d]))) - 0.5
        lw_o[...] = -jnp.exp(w_log)
        asig = _sigmoid(a0_ref[d:d + 1, :] + _dot(xl_b, a2_ref[d]))
        kd = k * (1.0 + (asig - 1.0) * ka_ref[...])
        d_o[:, :aw] = kd.astype(d_o.dtype)
        d_o[:, aw:] = (kk * asig).astype(d_o.dtype)
        kd_sum = kd if kd_sum is None else kd_sum + kd
    bonus_o[...] = _head_sum(r * rk_ref[...] * kd_sum, e_bf) * v


def _rwkv_prep(pa, mu_p, w0, a0, w2p, a2p, g2, k_k, k_a, r_k, e_bf, *, n_lat, seq, ctx_len):
    r = pa.shape[0]
    ncol = mu_p.shape[1]
    aw = w0.shape[1]
    tp = PREP_TILE
    hb = tp // SUBLANE
    nblk8 = r // SUBLANE
    lw = 2 * A_LORA_W + 2 * A_LORA_A
    row = lambda a: a.reshape(1, aw)
    full = lambda shape: pl.BlockSpec(shape, lambda i: (0,) * len(shape))
    outs = [(3 * aw, BF16), (2 * aw, BF16), (2 * aw, BF16), (aw, F32), (aw, F32), (aw, F32),
            (aw, F32)]
    return pl.pallas_call(
        functools.partial(_rwkv_prep_kernel, n_lat_tiles=n_lat // tp, lat_tiles_per_seq=seq // tp,
                          ctx_tiles_per_seq=ctx_len // tp, aw=aw),
        grid=(r // tp,),
        in_specs=[
            pl.BlockSpec((tp, ncol), lambda i: (i, 0)),
            pl.BlockSpec((SUBLANE, ncol), lambda i: (jnp.maximum(i * hb - 1, 0), 0)),
            pl.BlockSpec((SUBLANE, ncol), lambda i: (jnp.minimum((i + 1) * hb, nblk8 - 1), 0)),
            full((2, ncol)), full((2, aw)), full((2, aw)),
            full((2, lw, aw)), full((2, lw, aw)), full((A_LORA_G, aw)),
            full((1, aw)), full((1, aw)), full((1, aw)), full((GROUP_W, GROUP_W)),
        ],
        out_specs=[pl.BlockSpec((tp, w), lambda i: (i, 0)) for w, _ in outs],
        out_shape=[jax.ShapeDtypeStruct((r, w), dt) for w, dt in outs],
        compiler_params=_cparams(("parallel",)),
    )(pa, pa, pa, mu_p, w0, a0, w2p, a2p, g2, row(k_k), row(k_a), row(r_k), e_bf)


def _rwkv_masks(reverse):
    L = CHUNK
    gw = GROUP_W
    row = lax.broadcasted_iota(jnp.int32, (gw, gw), 0)
    col = lax.broadcasted_iota(jnp.int32, (gw, gw), 1)
    same_head = (row >> 6) == (col >> 6)
    t_n = lax.broadcasted_iota(jnp.int32, (L, gw), 0)
    s_n = lax.broadcasted_iota(jnp.int32, (L, gw), 1) & (L - 1)
    strict_n = (t_n < s_n) if reverse else (t_n > s_n)
    incl_n = (t_n <= s_n) if reverse else (t_n >= s_n)
    t_r = lax.broadcasted_iota(jnp.int32, (L, L), 0)
    t_c = lax.broadcasted_iota(jnp.int32, (L, L), 1)
    tri = _bf(jnp.where((t_c >= t_r) if reverse else (t_c <= t_r), 1.0, 0.0))
    return dict(same_head=same_head, strict_n=strict_n, incl_n=incl_n, eye_n=t_n == s_n, tri=tri)


def _rwkv_chunks(probs, side=None):
    L = CHUNK
    gw = GROUP_W
    tick = (lambda: None) if side is None else (lambda: next(side, None))
    masks = {rv: _rwkv_masks(rv) for rv in sorted({p[7] for p in probs})}
    mk = [masks[p[7]] for p in probs]
    each = lambda f, *cols: [f(*args) for args in zip(*cols)]
    rep = lambda x: jnp.concatenate([x] * HEADS_PER_GROUP, axis=0)

    c = [_dot_exact_lhs(m["tri"], p[3]) for m, p in zip(mk, probs)]
    c_l = [ci[(0 if p[7] else L - 1):(1 if p[7] else L), :] for ci, p in zip(c, probs)]

    blocks = [lambda x, sh=m["same_head"]: _bf(jnp.where(sh, rep(x), 0.0)) for m in mk]
    an, rn, vn, v_bd, bk_end, btk_bd, e_l = [], [], [], [], [], [], []
    for (r, v, kk, lw, kd, b, _, _), bd, ci, cl in zip(probs, blocks, c, c_l):
        e_nc = jnp.exp(-ci)
        e_end = jnp.exp(cl - ci)
        an.append(_bf(-kk * jnp.exp(ci - lw)))
        rn.append(_bf(r * jnp.exp(ci)))
        btk_bd.append(jnp.concatenate([bd(b * e_nc), bd(kd * e_nc)], axis=0))
        vn.append(_bf(v))
        v_bd.append(bd(v))
        bk_end.append(jnp.concatenate([_bf(b * e_end), _bf(kd * e_end)], axis=0))
        e_l.append(jnp.exp(cl))

    stack = lambda x, z: jnp.concatenate([x, z], axis=0)
    a_pow, t, akr, r_b = [], [], [], []
    for m, a_n, r_n, bk in zip(mk, an, rn, btk_bd):
        sn = _dot_nt(stack(a_n, r_n), bk)
        a_ab = jnp.where(m["strict_n"], sn[:L, :gw], 0.0)
        a_pow.append(a_ab)
        t.append(jnp.where(m["eye_n"], 1.0, a_ab))
        akr.append(stack(_bf(jnp.where(m["strict_n"], sn[:L, gw:], 0.0)),
                         _bf(jnp.where(m["incl_n"], sn[L:, gw:], 0.0))))
        r_b.append(_bf(jnp.where(m["incl_n"], sn[L:, :gw], 0.0)))
    tick()
    kv = each(_dot, akr, v_bd)
    a_bd = [bd(a) for bd, a in zip(blocks, a_pow)]
    tb = [_bf(ti) for ti in t]
    a_pow = [_dot(_bf(a), ab) for a, ab in zip(a_pow, a_bd)]
    for level in range(5):
        a_bd = [bd(a) for bd, a in zip(blocks, a_pow)]
        tick()
        if level < 4:
            res = [_dot(stack(tbi, _bf(a)), ab) for tbi, a, ab in zip(tb, a_pow, a_bd)]
            t = [ti + ri[:L] for ti, ri in zip(t, res)]
            a_pow = [ri[L:] for ri in res]
        else:
            t = [ti + _dot(tbi, ab) for ti, tbi, ab in zip(t, tb, a_bd)]
        tb = [_bf(ti) for ti in t]
    h_bf = [_bf(p[6]) for p in probs]
    hx = [_dot(stack(a_n, r_n), h) for a_n, r_n, h in zip(an, rn, h_bf)]
    x_bd = [bd(hxi[:L] + kvi[:L]) for bd, hxi, kvi in zip(blocks, hx, kv)]
    tick()
    u = [_dot(tbi, xi) for tbi, xi in zip(tb, x_bd)]
    tick()
    y = [hxi[L:] + _dot(rb, bd(ui)) + kvi[L:]
         for bd, hxi, rb, ui, kvi in zip(blocks, hx, r_b, u, kv)]
    e_col = [jnp.broadcast_to(el, (gw, gw)).T for el in e_l]
    h_new = []
    for m, ec, p, bke, ui, v_n in zip(mk, e_col, probs, bk_end, u, vn):
        upd = _dot_tn(bke, jnp.concatenate([_bf(ui), v_n], axis=0))
        h_new.append(ec * p[6] + jnp.where(m["same_head"], upd, 0.0))
    if side is not None:
        for _ in side:
            pass
    return list(zip(y, h_new))


def _chunk_row_block(b, i, *, reverse, n_lat, seq, ctx_len):
    nc = ctx_len // CHUNK
    nl = seq // CHUNK
    ci = (nc - 1 - i) if reverse else i
    li = (nl - 1 - (i - nc)) if reverse else (i - nc)
    return jnp.where(i < nc, (n_lat + b * ctx_len) // CHUNK + ci, b * nl + li)


def _gla_stages(qkvf, gdf, qkvb, gdb, gw2_ref, gb_ref, of_ref, ob_ref, sf_ref, sb_ref):
    dv, dk = sf_ref.shape[1:]
    L = CHUNK
    wk = B_HEADS * dk
    t_r = lax.broadcasted_iota(jnp.int32, (L, L), 0)
    t_c = lax.broadcasted_iota(jnp.int32, (L, L), 1)

    qh, kh, keh, vh, dech, keeph, sth = [], [], [], [], [], [], []
    for d, (qkv_ref, gd_ref, st_ref) in enumerate(((qkvf, gdf, sf_ref), (qkvb, gdb, sb_ref))):
        reverse = d == 1
        keep = (t_c >= t_r) if reverse else (t_c <= t_r)
        tri = _bf(jnp.where(keep, 1.0, 0.0))
        z = _dot(_bf(gd_ref[...]), gw2_ref[d]) + gb_ref[d:d + 1, :]
        la = -_softplus(-z) * (1.0 / B_GATE_TAU)
        cum = _dot_exact_lhs(tri, la)
        last = 0 if reverse else L - 1
        cl = cum[last:last + 1, :]
        q_in = _bf(qkv_ref[:, :wk] * (dk ** -0.5) * jnp.exp(cum))
        k_in = _bf(qkv_ref[:, wk:2 * wk] * jnp.exp(-cum))
        k_end = _bf(qkv_ref[:, wk:2 * wk] * jnp.exp(cl - cum))
        dec = jnp.exp(cl)
        for h in range(B_HEADS):
            ks = slice(h * dk, (h + 1) * dk)
            qh.append(q_in[:, ks])
            kh.append(k_in[:, ks])
            keh.append(k_end[:, ks])
            vh.append(_bf(qkv_ref[:, 2 * wk + h * dv:2 * wk + (h + 1) * dv]))
            dech.append(dec[:, ks])
            keeph.append(keep)
            sth.append(st_ref[h])
        yield

    sc = [_bf(jnp.where(kp, _dot_nt(q, k), 0.0)) for kp, q, k in zip(keeph, qh, kh)]
    yield
    inter = [_dot_nt(q, _bf(st)) for q, st in zip(qh, sth)]
    yield
    kvt = [_dot_tn(v, ke) for v, ke in zip(vh, keh)]
    yield
    out = [_dot(s_, v) + it for s_, v, it in zip(sc, vh, inter)]
    yield
    for d, (o_ref, st_ref) in enumerate(((of_ref, sf_ref), (ob_ref, sb_ref))):
        for h in range(B_HEADS):
            i = d * B_HEADS + h
            o_ref[:, h * dv:(h + 1) * dv] = out[i]
            st_ref[h] = sth[i] * dech[i] + kvt[i]


def _mix_scan_kernel(rvkf, lwf, df, rvkb, lwb, db, qkvf, gdf, qkvb, gdb, gw2_ref, gb_ref,
                     yf_ref, yb_ref, of_ref, ob_ref, hf_ref, hb_ref, sf_ref, sb_ref):
    @pl.when(pl.program_id(1) == 0)
    def _():
        for ref in (hf_ref, hb_ref, sf_ref, sb_ref):
            ref[...] = jnp.zeros_like(ref)

    gs = hf_ref.shape[0]
    aw = lwf.shape[1]
    probs = []
    for gi in range(gs):
        sl = slice(gi * GROUP_W, (gi + 1) * GROUP_W)
        f32 = lambda ref, j: ref[:, j * aw + gi * GROUP_W:j * aw + (gi + 1) * GROUP_W].astype(F32)
        for rvk, lw, dd, h_ref, rev in ((rvkf, lwf, df, hf_ref, False), (rvkb, lwb, db, hb_ref, True)):
            probs.append((f32(rvk, 0), f32(rvk, 1), f32(rvk, 2), lw[:, sl], f32(dd, 0), f32(dd, 1),
                          h_ref[gi], rev))
    gla = _gla_stages(qkvf, gdf, qkvb, gdb, gw2_ref, gb_ref, of_ref, ob_ref, sf_ref, sb_ref)
    res = _rwkv_chunks(probs, side=gla)
    for gi in range(gs):
        sl = slice(gi * GROUP_W, (gi + 1) * GROUP_W)
        yf_ref[:, sl], hf_ref[gi] = res[2 * gi]
        yb_ref[:, sl], hb_ref[gi] = res[2 * gi + 1]


def _mix_scan(rvk, d0, d1, lw0, lw1, pb, gw2p, gb, *, batch, n_lat, seq, ctx_len, wk, wv, col0):
    rows, aw = lw0.shape
    assert aw == GROUP_W * RWKV_GROUPS_PER_STEP
    dk, dv = wk // B_HEADS, wv // B_HEADS
    qkv_w = 2 * wk + wv
    assert col0 % qkv_w == 0
    geo = dict(n_lat=n_lat, seq=seq, ctx_len=ctx_len)
    gd_blk = (col0 + 2 * wk + 2 * wv) // LANE

    def specs(reverse):
        rb = lambda b, i: _chunk_row_block(b, i, reverse=reverse, **geo)
        row_blk = lambda w, cb=0: pl.BlockSpec((CHUNK, w), lambda b, i: (rb(b, i), cb))
        rw = [row_blk(3 * aw), row_blk(aw), row_blk(2 * aw)]
        gla = [row_blk(qkv_w, col0 // qkv_w), row_blk(LANE, gd_blk)]
        return rw, gla, row_blk(aw), row_blk(wv)

    rwf, glaf, yf, of = specs(False)
    rwb, glab, yb, ob = specs(True)
    sds_a = jax.ShapeDtypeStruct((rows, aw), F32)
    sds_b = jax.ShapeDtypeStruct((rows, wv), F32)
    gs = RWKV_GROUPS_PER_STEP
    return pl.pallas_call(
        _mix_scan_kernel,
        grid=(batch, (seq + ctx_len) // CHUNK),
        in_specs=rwf + rwb + glaf + glab + [
            pl.BlockSpec((2, LANE, wk), lambda b, i: (0, 0, 0)),
            pl.BlockSpec((2, wk), lambda b, i: (0, 0)),
        ],
        out_specs=[yf, yb, of, ob],
        out_shape=[sds_a, sds_a, sds_b, sds_b],
        scratch_shapes=[pltpu.VMEM((gs, GROUP_W, GROUP_W), F32),
                        pltpu.VMEM((gs, GROUP_W, GROUP_W), F32),
                        pltpu.VMEM((B_HEADS, dv, dk), F32), pltpu.VMEM((B_HEADS, dv, dk), F32)],
        compiler_params=_cparams(("parallel", "arbitrary")),
    )(rvk, lw0, d0, rvk, lw1, d1, pb, pb, pb, pb, gw2p, gb)


def _mixer_out_kernel(yf, yb, bonus, g, of, ob, gate, xa_ref, xb_ref, mod_ref, lnw, lnb, gn, e_ref,
                      w_ref, o_ref, h_ref, *, aw, dv, n_a_tiles):
    e_bf = e_ref[...]
    y = yf[...] + yb[...]
    inv = 1.0 / A_HEAD_DIM
    mean = _head_sum(y, e_bf) * inv
    dlt = y - mean
    var = _head_sum(dlt * dlt, e_bf) * inv
    yn = dlt * lax.rsqrt(var + A_LN_EPS) * lnw[...] + lnb[...]
    h_ref[:, :aw] = _bf((yn + bonus[...]) * g[...])

    o = of[...] + ob[...]
    gt = _silu(gate[...])
    for h in range(B_HEADS):
        sl = slice(h * dv, (h + 1) * dv)
        oh = o[:, sl]
        ms = jnp.mean(oh * oh, axis=-1, keepdims=True)
        h_ref[:, aw + h * dv:aw + (h + 1) * dv] = _bf(oh * lax.rsqrt(ms + NORM_EPS) * gn[...] * gt[:, sl])

    x = jnp.where(pl.program_id(0) < n_a_tiles, xa_ref[...], xb_ref[...])
    o_ref[...] = x + mod_ref[2:3, :] * _dot(h_ref[...], w_ref[...])


def _mixer_out(yf, yb, bonus, g, of, ob, pb, xa, xb, mods_l, ln_w, ln_b, gla_norm, e_bf, w_bf,
               seg_of_tile, *, wk, col0):
    d = xa.shape[1]
    rows = xa.shape[0] + xb.shape[0]
    aw = yf.shape[1]
    wv = of.shape[1]
    dv = wv // B_HEADS
    tm = 256
    tiles_per_row_tile = ROW_TILE // tm
    n_a_tiles = xa.shape[0] // tm
    rs = lambda w: pl.BlockSpec((tm, w), lambda i: (i, 0))
    full = lambda shape: pl.BlockSpec(shape, lambda i: (0,) * len(shape))
    return pl.pallas_call(
        functools.partial(_mixer_out_kernel, aw=aw, dv=dv, n_a_tiles=n_a_tiles),
        grid=(rows // tm,),
        in_specs=[
            rs(aw), rs(aw), rs(aw), rs(aw), rs(wv), rs(wv),
            pl.BlockSpec((tm, wv), lambda i: (i, (col0 + 2 * wk + wv) // wv)),
        ] + _two_source_specs(tm, d, n_a_tiles) + [
            pl.BlockSpec((None, N_MOD, d), lambda i: (seg_of_tile(i // tiles_per_row_tile), 0, 0)),
            full((1, aw)), full((1, aw)), full((1, dv)), full((GROUP_W, GROUP_W)),
            full((aw + wv, d)),
        ],
        out_specs=rs(d),
        out_shape=jax.ShapeDtypeStruct((rows, d), F32),
        scratch_shapes=[pltpu.VMEM((tm, aw + wv), BF16)],
        compiler_params=_cparams(("parallel",)),
    )(yf, yb, bonus, g, of, ob, pb, xa, xb, mods_l, ln_w.reshape(1, aw), ln_b.reshape(1, aw),
      gla_norm.reshape(1, dv), e_bf, w_bf)


def _lru_kernel(*refs, reverse, n_blk, combine, cols):
    if combine:
        (x_ref, xp_ref, xn_ref, h0_ref, cw_ref, cb_ref, wc_ref, ba_ref, bx_ref, lam_ref,
         hf_ref, gate_ref, out_ref, hl_ref, a_sc, u_sc, hs_sc, carry_sc) = refs
    else:
        (x_ref, xp_ref, xn_ref, h0_ref, cw_ref, cb_ref, wc_ref, ba_ref, bx_ref, lam_ref,
         out_ref, hl_ref, a_sc, u_sc, hs_sc, carry_sc) = refs
    j = pl.program_id(2)
    blk = (n_blk - 1 - j) if reverse else j
    lb = x_ref.shape[0]
    ct = cw_ref.shape[1]
    cblk = wc_ref.shape[1]
    rowid = lax.broadcasted_iota(jnp.int32, (lb, 1), 0)
    decay_rate = -C_CONST * _softplus(-lam_ref[...])

    @pl.when(j == 0)
    def _():
        carry_sc[...] = h0_ref[0:1, :]

    for q in (range(cols - 1, -1, -1) if reverse else range(cols)):
        lanes = slice(q * ct, (q + 1) * ct)
        x = x_ref[:, lanes]
        if q > 0:
            prev = slice((q - 1) * ct, q * ct)
            p6, p7 = x_ref[lb - 2:lb - 1, prev], x_ref[lb - 1:lb, prev]
        else:
            p6 = jnp.where(blk == 0, 0.0, xp_ref[SUBLANE - 2:SUBLANE - 1, :])
            p7 = jnp.where(blk == 0, 0.0, xp_ref[SUBLANE - 1:SUBLANE, :])
        if q < cols - 1:
            n0 = x_ref[0:1, (q + 1) * ct:(q + 2) * ct]
        else:
            n0 = jnp.where(blk == n_blk - 1, 0.0, xn_ref[0:1, :])
        xm1 = jnp.where(rowid == 0, p7, pltpu.roll(x, 1, 0))
        xm2 = jnp.where(rowid == 0, p6, jnp.where(rowid == 1, p7, pltpu.roll(x, 2, 0)))
        xp1 = jnp.where(rowid == lb - 1, n0, pltpu.roll(x, lb - 1, 0))
        xs = (cb_ref[...] + cw_ref[0:1, :] * xm2 + cw_ref[1:2, :] * xm1 + cw_ref[2:3, :] * x
              + cw_ref[3:4, :] * xp1)

        for n in range(ct // cblk):
            sl = slice(n * cblk, (n + 1) * cblk)
            xb = xs[:, sl]
            ri = _dot(_bf(xb), wc_ref[n])
            rg = _sigmoid(ri[:, :cblk] + ba_ref[:, sl])
            ig = _sigmoid(ri[:, cblk:] + bx_ref[:, sl])
            a = jnp.exp(decay_rate[:, sl] * rg)
            a_sc[q, :, sl] = a
            u_sc[q, :, sl] = jnp.sqrt(1.0 - a * a) * (ig * xb)

        def step(t, h, q=q):
            tt = (lb - 1 - t) if reverse else t
            h = a_sc[q, pl.ds(tt, 1), :] * h + u_sc[q, pl.ds(tt, 1), :]
            hs_sc[q, pl.ds(tt, 1), :] = h
            return h

        carry_sc[...] = lax.fori_loop(0, lb, step, carry_sc[...], unroll=8)
        if combine:
            out_ref[:, lanes] = ((hs_sc[q] + hf_ref[q * lb:(q + 1) * lb, :])
                                 * jax.nn.gelu(gate_ref[:, lanes]))
        else:
            out_ref[q * lb:(q + 1) * lb, :] = hs_sc[q]
    hl_ref[...] = jnp.broadcast_to(carry_sc[...], hl_ref.shape)


def _lru_pass(x_arr, x_specs, h0, conv_w, conv_b, wcat, ba, bx, lam, out_shape, out_spec, extra,
              *, batch, n_blk, lb, cols, reverse):
    c = conv_w.shape[1]
    ct = LRU_CHANNEL_TILE
    nct = c // ct
    cblk = wcat.shape[1]
    chan = lambda rows_: pl.BlockSpec((rows_, ct), lambda b, k, j: (0, k))
    in_specs = list(x_specs) + [
        pl.BlockSpec((SUBLANE, ct), lambda b, k, j: (b, k)),
        chan(C_CONV), chan(1),
        pl.BlockSpec((ct // cblk, cblk, 2 * cblk), lambda b, k, j: (k, 0, 0)),
        chan(1), chan(1), chan(1),
    ] + [s for _, s in extra]
    args = [x_arr, x_arr, x_arr, h0, conv_w, conv_b.reshape(1, c), wcat, ba.reshape(1, c),
            bx.reshape(1, c), lam.reshape(1, c)] + [a for a, _ in extra]
    return pl.pallas_call(
        functools.partial(_lru_kernel, reverse=reverse, n_blk=n_blk, combine=bool(extra),
                          cols=cols),
        grid=(batch, nct, n_blk),
        in_specs=in_specs,
        out_specs=[out_spec, pl.BlockSpec((SUBLANE, ct), lambda b, k, j: (b, k))],
        out_shape=[out_shape, jax.ShapeDtypeStruct((batch * SUBLANE, c), F32)],
        scratch_shapes=[pltpu.VMEM((cols, lb, ct), F32), pltpu.VMEM((cols, lb, ct), F32),
                        pltpu.VMEM((cols, lb, ct), F32), pltpu.VMEM((1, ct), F32)],
        compiler_params=_cparams(("parallel", "parallel", "arbitrary")),
    )(*args)


def _rglru(hw_ctx, gate_grid, x_grid, conv_w, conv_b, wa, ba, wx, bx, lam, *, batch, n_lat, seq,
           ctx_len):
    c = conv_w.shape[1]
    ct = LRU_CHANNEL_TILE
    rows = seq // GRID_W
    lbc = 128
    ncb = ctx_len // lbc
    xcol = c // ct

    wcat = [_bf(jnp.concatenate([wa[d], wx[d]], axis=-1)) for d in range(2)]
    zeros_state = jnp.zeros((batch * SUBLANE, c), F32)

    def ctx_specs(reverse):
        blk = (lambda j: ncb - 1 - j) if reverse else (lambda j: j)
        per8 = lbc // SUBLANE
        cur = pl.BlockSpec((lbc, ct), lambda b, k, j: (b * ncb + blk(j), xcol + k))
        prv = pl.BlockSpec((SUBLANE, ct), lambda b, k, j: (
            b * ncb * per8 + jnp.maximum(blk(j) * per8 - 1, 0), xcol + k))
        nxt = pl.BlockSpec((SUBLANE, ct), lambda b, k, j: (
            b * ncb * per8 + jnp.minimum((blk(j) + 1) * per8, ncb * per8 - 1), xcol + k))
        return cur, prv, nxt

    assert ct == c
    def lat_specs(reverse, kc):
        ncg = GRID_W // kc
        grp = (lambda j: ncg - 1 - j) if reverse else (lambda j: j)
        r8 = rows // SUBLANE
        cur = pl.BlockSpec((rows, kc * ct), lambda b, k, j: (b, grp(j)))
        prv = pl.BlockSpec((SUBLANE, ct), lambda b, k, j: (
            b * r8 + r8 - 1, jnp.maximum(grp(j) * kc - 1, 0)))
        nxt = pl.BlockSpec((SUBLANE, ct), lambda b, k, j: (
            b * r8, jnp.minimum(grp(j) * kc + kc, GRID_W - 1)))
        hf = pl.BlockSpec((kc * rows, ct), lambda b, k, j: (b * ncg + grp(j), 0))
        return (cur, prv, nxt), hf, cur

    common = dict(batch=batch)
    ctx_scratch_sds = jax.ShapeDtypeStruct((batch * ctx_len, c), F32)
    ctx_out = lambda reverse: pl.BlockSpec(
        (lbc, ct), lambda b, k, j: (b * ncb + ((ncb - 1 - j) if reverse else j), k))
    _, st = _lru_pass(hw_ctx, ctx_specs(False), zeros_state, conv_w, conv_b, wcat[0], ba[0], bx[0],
                      lam[0], ctx_scratch_sds, ctx_out(False), [], n_blk=ncb, lb=lbc, cols=1,
                      reverse=False, **common)
    hf_sds = jax.ShapeDtypeStruct((n_lat, c), F32)
    kc = LRU_COLS_FWD
    x_specs, hf_spec, _ = lat_specs(False, kc)
    hf, _ = _lru_pass(x_grid, x_specs, st, conv_w, conv_b, wcat[0], ba[0], bx[0], lam[0], hf_sds,
                      hf_spec, [], n_blk=GRID_W // kc, lb=rows, cols=kc, reverse=False, **common)
    _, st = _lru_pass(hw_ctx, ctx_specs(True), zeros_state, conv_w, conv_b, wcat[1], ba[1], bx[1],
                      lam[1], ctx_scratch_sds, ctx_out(True), [], n_blk=ncb, lb=lbc, cols=1,
                      reverse=True, **common)
    kc = LRU_COLS_BWD
    x_specs, hf_spec, grid_spec = lat_specs(True, kc)
    y_sds = jax.ShapeDtypeStruct((n_lat // GRID_W, GRID_W * c), F32)
    y, _ = _lru_pass(x_grid, x_specs, st, conv_w, conv_b, wcat[1], ba[1], bx[1], lam[1],
                     y_sds, grid_spec, [(hf, hf_spec), (gate_grid, grid_spec)],
                     n_blk=GRID_W // kc, lb=rows, cols=kc, reverse=True, **common)
    return y


def _pad_cols(w, n):
    return jnp.pad(w, ((0, 0),) * (w.ndim - 1) + ((0, n - w.shape[-1]),))


def kernel(x, c, ctx, c_ctx, mod_w, mod_b, norm1, norm2, mlp_w1, mlp_w2, ab_w_in, ab_w_out, rw_mu,
           rw_w0, rw_w2, rw_a0, rw_a2, rw_g2, rw_kk, rw_ka, rw_rk, rw_ln_w, rw_ln_b, gla_gw2,
           gla_gb, gla_norm, lru_w_in, lru_w_out, lru_conv_w, lru_conv_b, lru_wa, lru_ba, lru_wx,
           lru_bx, lru_lam, final_norm):
    batch, seq, d = x.shape
    ctx_len = ctx.shape[1]
    depth = mod_w.shape[0]
    n_lat = batch * seq
    aw = rw_w0.shape[-1]
    wk = gla_gb.shape[-1]
    wv = ab_w_out.shape[1] - aw
    a_cols = rw_mu.shape[-1]
    b_cols = ab_w_in.shape[-1] - a_cols
    assert depth == 2 and batch + 1 <= SUBLANE
    assert seq % ROW_TILE == 0 and (batch * ctx_len) % ROW_TILE == 0
    assert ctx_len % PREP_TILE == 0 and ctx_len % 128 == 0 and seq % (GRID_W * SUBLANE) == 0
    assert aw % GROUP_W == 0 and wv == aw and 2 * wk == wv

    tiles_per_seq = seq // ROW_TILE
    seg_of_tile = lambda i: jnp.minimum(i // tiles_per_seq, batch)

    x2 = x.reshape(n_lat, d)
    ctx2 = ctx.reshape(batch * ctx_len, d)
    cvec = jnp.concatenate(
        [c, c_ctx[None, :], jnp.zeros((SUBLANE - batch - 1, d), F32)], axis=0)
    mods = _mods(cvec, mod_w, mod_b)

    hid = jnp.arange(GROUP_W) // A_HEAD_DIM
    e_bf = _bf(hid[:, None] == hid[None, :])

    a_pad = -(-a_cols // LANE) * LANE
    col0 = -(-a_cols // (2 * wk)) * (2 * wk)
    n_tiles = 3
    tn = -(-(col0 + b_cols) // (n_tiles * LANE)) * LANE
    w_ab = _bf(jnp.concatenate(
        [_pad_cols(ab_w_in[0][:, :a_cols], col0),
         _pad_cols(ab_w_in[0][:, a_cols:], n_tiles * tn - col0)], axis=1))
    pa = pb = _norm_matmul(x2, ctx2, norm1[0], mods[0], w_ab, seg_of_tile, tn=tn)

    lora_w = 2 * A_LORA_W + 2 * A_LORA_A
    w2p = jnp.zeros((2, lora_w, aw), F32)
    a2p = jnp.zeros((2, lora_w, aw), F32)
    for dd in range(2):
        w2p = w2p.at[dd, dd * A_LORA_W:(dd + 1) * A_LORA_W].set(rw_w2[0, dd])
        a2p = a2p.at[dd, 2 * A_LORA_W + dd * A_LORA_A:2 * A_LORA_W + (dd + 1) * A_LORA_A].set(
            rw_a2[0, dd])
    (rvk, d0, d1, lw0, lw1, g_, bonus) = _rwkv_prep(
        pa, _pad_cols(rw_mu[0], a_pad), rw_w0[0], rw_a0[0], _bf(w2p), _bf(a2p), _bf(rw_g2[0]),
        rw_kk[0], rw_ka[0], rw_rk[0].reshape(-1), e_bf, n_lat=n_lat, seq=seq, ctx_len=ctx_len)
    gw2p = jnp.zeros((2, LANE, wk), F32)
    for dd in range(2):
        gw2p = gw2p.at[dd, dd * B_LORA:(dd + 1) * B_LORA].set(gla_gw2[0, dd])
    yf, yb, of, ob = _mix_scan(rvk, d0, d1, lw0, lw1, pb, _bf(gw2p), gla_gb[0],
                               batch=batch, n_lat=n_lat, seq=seq, ctx_len=ctx_len, wk=wk, wv=wv,
                               col0=col0)

    xs = _mixer_out(yf, yb, bonus, g_, of, ob, pb, x2, ctx2, mods[0], rw_ln_w[0], rw_ln_b[0],
                    gla_norm[0], e_bf, _layer_bf16(ab_w_out, 0), seg_of_tile, wk=wk, col0=col0)
    seg_of_mlp_tile = lambda i: jnp.minimum(i // (seq // MLP_TILE), batch)
    ctx_tiles = (batch * ctx_len) // ROW_TILE
    w1_bf, w2_bf = _layer_bf16(mlp_w1, 0), _layer_bf16(mlp_w2, 0)
    x_lat = _mlp(xs, norm2[0], mods[0], w1_bf, w2_bf, final_norm, seg_of_mlp_tile, tm=MLP_TILE,
                 tile0=0, n_tiles=n_lat // MLP_TILE, final_norm=False)
    x_ctx = _mlp(xs, norm2[0], mods[0], w1_bf, w2_bf, final_norm, seg_of_tile, tm=ROW_TILE,
                 tile0=n_lat // ROW_TILE, n_tiles=ctx_tiles, final_norm=False)

    w_lru = _layer_bf16(lru_w_in, 0)
    hw_ctx = _norm_matmul(x_ctx, None, norm1[1], mods[1], w_lru, lambda i: batch, tn=2048)
    gate_grid, x_grid = _norm_matmul_grid(x_lat, n_lat, seq // GRID_W, norm1[1], mods[1], w_lru)
    y = _rglru(hw_ctx, gate_grid, x_grid, lru_conv_w[0], lru_conv_b[0], lru_wa[0], lru_ba[0],
               lru_wx[0], lru_bx[0], lru_lam[0], batch=batch, n_lat=n_lat, seq=seq,
               ctx_len=ctx_len)
    xl = _proj_res(y, x_lat, n_lat, seq // GRID_W, mods[1], _layer_bf16(lru_w_out, 0))
    out = _mlp(xl, norm2[1], mods[1], _layer_bf16(mlp_w1, 1), _layer_bf16(mlp_w2, 1), final_norm,
               seg_of_mlp_tile,
               tm=MLP_TILE, tile0=0, n_tiles=n_lat // MLP_TILE, final_norm=True)
    return out.reshape(batch, seq, d)
```

```python
import functools

import jax
import jax.numpy as jnp
from jax import lax
from jax.experimental import pallas as pl
from jax.experimental.pallas import tpu as pltpu

F32 = jnp.float32
BF16 = jnp.bfloat16

NORM_EPS = 1e-6
GRID_W = 64
N_MOD = 6

A_HEAD_DIM = 64
A_LORA_W = 96
A_LORA_A = 96
A_LORA_G = 256
A_LN_EPS = 64e-5
HEADS_PER_GROUP = 4
GROUP_W = HEADS_PER_GROUP * A_HEAD_DIM
CHUNK = 64
RWKV_GROUPS_PER_STEP = 4

B_HEADS = 4
B_LORA = 16
B_GATE_TAU = 16.0

C_BLOCKS = 8
C_CONV = 4
C_CONST = 8.0

LANE = 128
SUBLANE = 8
ROW_TILE = 512
MLP_TILE = 1024
MLP_STEP_ELEMS = 1024 * 512
LRU_CHANNEL_TILE = 2048
LRU_COLS_FWD = 4
LRU_COLS_BWD = 1
CAST_BLOCK_ELEMS = 2 * 1024 * 1024
PREP_TILE = 256
VMEM_LIMIT = 56 * 1024 * 1024


def _cparams(sem):
    return pltpu.CompilerParams(dimension_semantics=sem, vmem_limit_bytes=VMEM_LIMIT)


def _bf(x):
    return x.astype(BF16)


def _dot(a, b):
    return jnp.dot(a, b, preferred_element_type=F32)


def _dot_nt(a, b):
    return lax.dot_general(a, b, (((1,), (1,)), ((), ())), preferred_element_type=F32)


def _dot_tn(a, b):
    return lax.dot_general(a, b, (((0,), (0,)), ((), ())), preferred_element_type=F32)


def _split3(x):
    hi = _bf(x)
    r1 = x - hi.astype(F32)
    mid = _bf(r1)
    lo = _bf(r1 - mid.astype(F32))
    return hi, mid, lo


def _dot_exact_lhs(m_bf, x):
    hi, mid, lo = _split3(x)
    return _dot(m_bf, hi) + _dot(m_bf, mid) + _dot(m_bf, lo)


def _dot_exact_rhs(x, m_bf):
    hi, mid, lo = _split3(x)
    return _dot(hi, m_bf) + _dot(mid, m_bf) + _dot(lo, m_bf)


def _softplus(x):
    return jnp.maximum(x, 0.0) + jnp.log(1.0 + jnp.exp(-jnp.abs(x)))


def _sigmoid(x):
    return 0.5 * jnp.tanh(0.5 * x) + 0.5


def _silu(x):
    return x * _sigmoid(x)


def _norm_mod(x, g, shift, scale):
    ms = jnp.mean(x * x, axis=-1, keepdims=True)
    return (x * lax.rsqrt(ms + NORM_EPS) * g) * (1.0 + scale) + shift


def _cast_kernel(w_ref, o_ref):
    o_ref[...] = _bf(w_ref[...])


def _layer_bf16(w, layer):
    _, r, c = w.shape
    tr = min(r, CAST_BLOCK_ELEMS // c)
    return pl.pallas_call(
        _cast_kernel,
        grid=(r // tr,),
        in_specs=[pl.BlockSpec((None, tr, c), lambda i: (layer, i, 0))],
        out_specs=pl.BlockSpec((tr, c), lambda i: (i, 0)),
        out_shape=jax.ShapeDtypeStruct((r, c), BF16),
        compiler_params=_cparams(("parallel",)),
    )(w)


def _split_cast_kernel(w_ref, o_ref, *, a_cols, col0):
    b_cols = w_ref.shape[1] - a_cols
    o_ref[...] = jnp.zeros_like(o_ref)
    o_ref[:, :a_cols] = _bf(w_ref[:, :a_cols])
    o_ref[:, col0:col0 + b_cols] = _bf(w_ref[:, a_cols:])


def _layer_split_bf16(w, layer, a_cols, col0, n_out):
    _, r, c = w.shape
    tr = 256
    return pl.pallas_call(
        functools.partial(_split_cast_kernel, a_cols=a_cols, col0=col0),
        grid=(r // tr,),
        in_specs=[pl.BlockSpec((None, tr, c), lambda i: (layer, i, 0))],
        out_specs=pl.BlockSpec((tr, n_out), lambda i: (i, 0)),
        out_shape=jax.ShapeDtypeStruct((r, n_out), BF16),
        compiler_params=_cparams(("parallel",)),
    )(w)


def _mods_kernel(c_ref, w_ref, b_ref, o_ref):
    s = _silu(c_ref[...])
    o_ref[...] = _dot(_bf(s), _bf(w_ref[...])) + b_ref[...]


def _mods(cvec, mod_w, mod_b):
    depth, d, n = mod_w.shape
    tn = 1024
    out = pl.pallas_call(
        _mods_kernel,
        grid=(depth, n // tn),
        in_specs=[
            pl.BlockSpec((SUBLANE, d), lambda l, j: (0, 0)),
            pl.BlockSpec((None, d, tn), lambda l, j: (l, 0, j)),
            pl.BlockSpec((None, 1, tn), lambda l, j: (l, 0, j)),
        ],
        out_specs=pl.BlockSpec((None, SUBLANE, tn), lambda l, j: (l, 0, j)),
        out_shape=jax.ShapeDtypeStruct((depth, SUBLANE, n), F32),
        compiler_params=_cparams(("parallel", "parallel")),
    )(cvec, mod_w, mod_b.reshape(depth, 1, n))
    return out.reshape(depth, SUBLANE, N_MOD, d)


def _norm_matmul_kernel(xa_ref, xb_ref, g_ref, mod_ref, w_ref, o_ref, h_ref, *, n_a_tiles):
    i = pl.program_id(0)

    @pl.when(pl.program_id(1) == 0)
    def _():
        x = jnp.where(i < n_a_tiles, xa_ref[...], xb_ref[...])
        h_ref[...] = _bf(_norm_mod(x, g_ref[...], mod_ref[0:1, :], mod_ref[1:2, :]))

    o_ref[...] = _dot(h_ref[...], w_ref[...])


def _two_source_specs(tm, d, n_a_tiles):
    return [pl.BlockSpec((tm, d), lambda i, *_: (jnp.minimum(i, n_a_tiles - 1), 0)),
            pl.BlockSpec((tm, d), lambda i, *_: (jnp.maximum(i - n_a_tiles, 0), 0))]


def _norm_matmul(xa, xb, g, mods_l, w_bf, seg_of_tile, *, tn):
    d = xa.shape[1]
    n = w_bf.shape[1]
    tm = ROW_TILE
    n_a_tiles = xa.shape[0] // tm
    n_tiles = n_a_tiles + (0 if xb is None else xb.shape[0] // tm)
    return pl.pallas_call(
        functools.partial(_norm_matmul_kernel, n_a_tiles=n_a_tiles),
        grid=(n_tiles, n // tn),
        in_specs=_two_source_specs(tm, d, n_a_tiles) + [
            pl.BlockSpec((1, d), lambda i, j: (0, 0)),
            pl.BlockSpec((None, N_MOD, d), lambda i, j: (seg_of_tile(i), 0, 0)),
            pl.BlockSpec((d, tn), lambda i, j: (0, j)),
        ],
        out_specs=pl.BlockSpec((tm, tn), lambda i, j: (i, j)),
        out_shape=jax.ShapeDtypeStruct((n_tiles * tm, n), F32),
        scratch_shapes=[pltpu.VMEM((tm, d), BF16)],
        compiler_params=_cparams(("parallel", "arbitrary")),
    )(xa, xa if xb is None else xb, g.reshape(1, d), mods_l, w_bf)


def _grid_tile(rows):
    gr = min(rows, 64)
    gc = ROW_TILE // gr
    assert gc % SUBLANE == 0 and GRID_W % gc == 0 and rows % gr == 0
    return gr, gc


def _norm_matmul_grid_kernel(x_ref, g_ref, mod_ref, w_ref, o0_ref, o1_ref, h_ref, res_ref):
    j = pl.program_id(1)
    gr, gc, d = x_ref.shape

    @pl.when(j == 0)
    def _():
        h = _norm_mod(x_ref[...].reshape(gr * gc, d), g_ref[...], mod_ref[0:1, :], mod_ref[1:2, :])
        h_ref[...] = _bf(h)

    res = _dot(h_ref[...], w_ref[...])
    nh = w_ref.shape[1]
    nq = nh // LANE
    for q in range(nq):
        res_ref[q] = res[:, q * LANE:(q + 1) * LANE]

    def scatter(o_ref):
        for c in range(gc):
            for q in range(nq):
                o_ref[:, c * nh + q * LANE:c * nh + (q + 1) * LANE] = (
                    res_ref[q, pl.ds(c, gr, stride=gc), :])

    @pl.when(j == 0)
    def _():
        scatter(o0_ref)

    @pl.when(j == 1)
    def _():
        scatter(o1_ref)


def _norm_matmul_grid(x, n_lat, rows, g, mods_l, w_bf):
    d = x.shape[1]
    nh = w_bf.shape[1] // 2
    gr, gc = _grid_tile(rows)
    ncg = GRID_W // gc
    x3 = x.reshape(x.shape[0] // GRID_W, GRID_W, d)
    out_spec = pl.BlockSpec((gr, gc * nh), lambda i, j: (i // ncg, i % ncg))
    sds = jax.ShapeDtypeStruct((n_lat // GRID_W, GRID_W * nh), F32)
    return pl.pallas_call(
        _norm_matmul_grid_kernel,
        grid=(n_lat // ROW_TILE, 2),
        in_specs=[
            pl.BlockSpec((gr, gc, d), lambda i, j: (i // ncg, i % ncg, 0)),
            pl.BlockSpec((1, d), lambda i, j: (0, 0)),
            pl.BlockSpec((None, N_MOD, d), lambda i, j: ((i // ncg) * gr // rows, 0, 0)),
            pl.BlockSpec((d, nh), lambda i, j: (0, j)),
        ],
        out_specs=[out_spec, out_spec],
        out_shape=[sds, sds],
        scratch_shapes=[pltpu.VMEM((ROW_TILE, d), BF16),
                        pltpu.VMEM((nh // LANE, ROW_TILE, LANE), F32)],
        compiler_params=_cparams(("parallel", "arbitrary")),
    )(x3, g.reshape(1, d), mods_l, w_bf)


def _mlp_kernel(x_ref, g_ref, mod_ref, w1_ref, w2_ref, fn_ref, o_ref, h_ref, *, final_norm):
    j = pl.program_id(1)
    slab = 256
    n_slab = x_ref.shape[0] // slab

    @pl.when(j == 0)
    def _():
        for s in range(n_slab):
            rs = slice(s * slab, (s + 1) * slab)
            h_ref[rs, :] = _bf(_norm_mod(x_ref[rs, :], g_ref[...], mod_ref[3:4, :],
                                         mod_ref[4:5, :]))
        o_ref[...] = jnp.zeros_like(o_ref)

    a = _dot(h_ref[...], w1_ref[...])
    a = _bf(jnp.square(jnp.maximum(a, 0.0)))
    nsplit = 2
    wn = o_ref.shape[1] // nsplit
    for s in range(nsplit):
        o_ref[:, s * wn:(s + 1) * wn] += _dot(a, w2_ref[:, s * wn:(s + 1) * wn])

    @pl.when(j == pl.num_programs(1) - 1)
    def _():
        for s in range(n_slab):
            rs = slice(s * slab, (s + 1) * slab)
            xo = x_ref[rs, :] + mod_ref[5:6, :] * o_ref[rs, :]
            if final_norm:
                ms = jnp.mean(xo * xo, axis=-1, keepdims=True)
                xo = xo * lax.rsqrt(ms + NORM_EPS) * fn_ref[...]
            o_ref[rs, :] = xo


def _mlp(x, g, mods_l, w1_bf, w2_bf, fn, seg_of_tile, *, tm, tile0, n_tiles, final_norm):
    d = x.shape[1]
    dff = w1_bf.shape[1]
    tf = MLP_STEP_ELEMS // tm
    return pl.pallas_call(
        functools.partial(_mlp_kernel, final_norm=final_norm),
        grid=(n_tiles, dff // tf),
        in_specs=[
            pl.BlockSpec((tm, d), lambda i, j: (i + tile0, 0), pipeline_mode=pl.Buffered(1)),
            pl.BlockSpec((1, d), lambda i, j: (0, 0)),
            pl.BlockSpec((None, N_MOD, d), lambda i, j: (seg_of_tile(i + tile0), 0, 0)),
            pl.BlockSpec((d, tf), lambda i, j: (0, j)),
            pl.BlockSpec((tf, d), lambda i, j: (j, 0)),
            pl.BlockSpec((1, d), lambda i, j: (0, 0)),
        ],
        out_specs=pl.BlockSpec((tm, d), lambda i, j: (i, 0)),
        out_shape=jax.ShapeDtypeStruct((n_tiles * tm, d), F32),
        scratch_shapes=[pltpu.VMEM((tm, d), BF16)],
        compiler_params=_cparams(("parallel", "arbitrary")),
    )(x, g.reshape(1, d), mods_l, w1_bf, w2_bf, fn.reshape(1, d))


def _proj_res_kernel(y_ref, x_ref, mod_ref, w_ref, o_ref, yt_ref):
    k = w_ref.shape[0]
    gr, gc, d = x_ref.shape
    nq = k // LANE
    for c in range(gc):
        for q in range(nq):
            yt_ref[q, pl.ds(c, gr, stride=gc), :] = (
                y_ref[:, c * k + q * LANE:c * k + (q + 1) * LANE])
    yt = jnp.concatenate([_bf(yt_ref[q]) for q in range(nq)], axis=1)
    out = x_ref[...].reshape(gr * gc, d) + mod_ref[2:3, :] * _dot(yt, w_ref[...])
    o_ref[...] = out.reshape(gr, gc, d)


def _proj_res(y_grid, x, n_lat, rows, mods_l, w_bf):
    k, d = w_bf.shape
    gr, gc = _grid_tile(rows)
    ncg = GRID_W // gc
    x3 = x.reshape(x.shape[0] // GRID_W, GRID_W, d)
    tok_spec = pl.BlockSpec((gr, gc, d), lambda i: (i // ncg, i % ncg, 0))
    out = pl.pallas_call(
        _proj_res_kernel,
        grid=(n_lat // ROW_TILE,),
        in_specs=[
            pl.BlockSpec((gr, gc * k), lambda i: (i // ncg, i % ncg)),
            tok_spec,
            pl.BlockSpec((None, N_MOD, d), lambda i: ((i // ncg) * gr // rows, 0, 0)),
            pl.BlockSpec((k, d), lambda i: (0, 0)),
        ],
        out_specs=tok_spec,
        out_shape=jax.ShapeDtypeStruct((n_lat // GRID_W, GRID_W, d), F32),
        scratch_shapes=[pltpu.VMEM((k // LANE, ROW_TILE, LANE), F32)],
        compiler_params=_cparams(("parallel",)),
    )(y_grid, x3, mods_l, w_bf)
    return out.reshape(n_lat, d)


def _head_sum(z, e_bf):
    parts = []
    for gi in range(z.shape[1] // GROUP_W):
        zs = z[:, gi * GROUP_W:(gi + 1) * GROUP_W]
        parts.append(_dot_exact_rhs(zs, e_bf))
    return jnp.concatenate(parts, axis=1)


def _rwkv_prep_kernel(p_ref, pp_ref, pn_ref, mu_ref, w0_ref, a0_ref, w2_ref, a2_ref, g2_ref,
                      kk_ref, ka_ref, rk_ref, e_ref,
                      rvk_o, d0_o, d1_o, lw0_o, lw1_o, g_o, bonus_o,
                      *, n_lat_tiles, lat_tiles_per_seq, ctx_tiles_per_seq, aw):
    i = pl.program_id(0)
    tp = p_ref.shape[0]
    in_ctx = i >= n_lat_tiles
    pos = jnp.where(in_ctx, (i - n_lat_tiles) % ctx_tiles_per_seq, i % lat_tiles_per_seq)
    last = jnp.where(in_ctx, ctx_tiles_per_seq - 1, lat_tiles_per_seq - 1)
    is_first = pos == 0
    is_last = pos == last
    rowid = lax.broadcasted_iota(jnp.int32, (tp, 1), 0)

    def shifted(c0, c1):
        x = p_ref[:, c0:c1]
        prow = jnp.where(is_first, 0.0, pp_ref[SUBLANE - 1:SUBLANE, c0:c1])
        nrow = jnp.where(is_last, 0.0, pn_ref[0:1, c0:c1])
        xp = jnp.where(rowid == 0, prow, pltpu.roll(x, 1, 0))
        xn = jnp.where(rowid == tp - 1, nrow, pltpu.roll(x, tp - 1, 0))
        return x + mu_ref[0:1, c0:c1] * (xp - x) + mu_ref[1:2, c0:c1] * (xn - x)

    e_bf = e_ref[...]
    r = shifted(0, aw)
    k = shifted(aw, 2 * aw)
    v = shifted(2 * aw, 3 * aw)
    gd = shifted(3 * aw, 3 * aw + A_LORA_G)
    lo0 = 3 * aw + A_LORA_G
    xl = shifted(lo0, lo0 + 2 * A_LORA_W + 2 * A_LORA_A)

    rvk_o[:, :aw] = r.astype(rvk_o.dtype)
    rvk_o[:, aw:2 * aw] = v.astype(rvk_o.dtype)
    g_o[...] = _dot(_bf(_sigmoid(gd)), g2_ref[...])

    kx = k * kk_ref[...]
    kn = jnp.sqrt(_head_sum(kx * kx, e_bf))
    kk = kx / jnp.maximum(kn, 1e-12)
    rvk_o[:, 2 * aw:] = kk.astype(rvk_o.dtype)

    xl_t = _bf(jnp.tanh(xl))
    xl_b = _bf(xl)
    kd_sum = None
    for d, (lw_o, d_o) in enumerate(((lw0_o, d0_o), (lw1_o, d1_o))):
        w_log = -_softplus(-(w0_ref[d:d + 1, :] + _dot(xl_t, w2_ref[d]))) - 0.5
        lw_o[...] = -jnp.exp(w_log)
        asig = _sigmoid(a0_ref[d:d + 1, :] + _dot(xl_b, a2_ref[d]))
        kd = k * (1.0 + (asig - 1.0) * ka_ref[...])
        d_o[:, :aw] = kd.astype(d_o.dtype)
        d_o[:, aw:] = (kk * asig).astype(d_o.dtype)
        kd_sum = kd if kd_sum is None else kd_sum + kd
    bonus_o[...] = _head_sum(r * rk_ref[...] * kd_sum, e_bf) * v


def _rwkv_prep(pa, mu_p, w0, a0, w2p, a2p, g2, k_k, k_a, r_k, e_bf, *, n_lat, seq, ctx_len):
    r = pa.shape[0]
    ncol = mu_p.shape[1]
    aw = w0.shape[1]
    tp = PREP_TILE
    hb = tp // SUBLANE
    nblk8 = r // SUBLANE
    lw = 2 * A_LORA_W + 2 * A_LORA_A
    row = lambda a: a.reshape(1, aw)
    full = lambda shape: pl.BlockSpec(shape, lambda i: (0,) * len(shape))
    outs = [(3 * aw, BF16), (2 * aw, BF16), (2 * aw, BF16), (aw, F32), (aw, F32), (aw, F32),
            (aw, F32)]
    return pl.pallas_call(
        functools.partial(_rwkv_prep_kernel, n_lat_tiles=n_lat // tp, lat_tiles_per_seq=seq // tp,
                          ctx_tiles_per_seq=ctx_len // tp, aw=aw),
        grid=(r // tp,),
        in_specs=[
            pl.BlockSpec((tp, ncol), lambda i: (i, 0)),
            pl.BlockSpec((SUBLANE, ncol), lambda i: (jnp.maximum(i * hb - 1, 0), 0)),
            pl.BlockSpec((SUBLANE, ncol), lambda i: (jnp.minimum((i + 1) * hb, nblk8 - 1), 0)),
            full((2, ncol)), full((2, aw)), full((2, aw)),
            full((2, lw, aw)), full((2, lw, aw)), full((A_LORA_G, aw)),
            full((1, aw)), full((1, aw)), full((1, aw)), full((GROUP_W, GROUP_W)),
        ],
        out_specs=[pl.BlockSpec((tp, w), lambda i: (i, 0)) for w, _ in outs],
        out_shape=[jax.ShapeDtypeStruct((r, w), dt) for w, dt in outs],
        compiler_params=_cparams(("parallel",)),
    )(pa, pa, pa, mu_p, w0, a0, w2p, a2p, g2, row(k_k), row(k_a), row(r_k), e_bf)


def _rwkv_masks(reverse):
    L = CHUNK
    gw = GROUP_W
    row = lax.broadcasted_iota(jnp.int32, (gw, gw), 0)
    col = lax.broadcasted_iota(jnp.int32, (gw, gw), 1)
    same_head = (row >> 6) == (col >> 6)
    t_n = lax.broadcasted_iota(jnp.int32, (L, gw), 0)
    s_n = lax.broadcasted_iota(jnp.int32, (L, gw), 1) & (L - 1)
    strict_n = (t_n < s_n) if reverse else (t_n > s_n)
    incl_n = (t_n <= s_n) if reverse else (t_n >= s_n)
    t_r = lax.broadcasted_iota(jnp.int32, (L, L), 0)
    t_c = lax.broadcasted_iota(jnp.int32, (L, L), 1)
    tri = _bf(jnp.where((t_c >= t_r) if reverse else (t_c <= t_r), 1.0, 0.0))
    return dict(same_head=same_head, strict_n=strict_n, incl_n=incl_n, eye_n=t_n == s_n, tri=tri)


def _rwkv_chunks(probs, side=None):
    L = CHUNK
    gw = GROUP_W
    tick = (lambda: None) if side is None else (lambda: next(side, None))
    masks = {rv: _rwkv_masks(rv) for rv in sorted({p[7] for p in probs})}
    mk = [masks[p[7]] for p in probs]
    each = lambda f, *cols: [f(*args) for args in zip(*cols)]
    rep = lambda x: jnp.concatenate([x] * HEADS_PER_GROUP, axis=0)

    c = [_dot_exact_lhs(m["tri"], p[3]) for m, p in zip(mk, probs)]
    c_l = [ci[(0 if p[7] else L - 1):(1 if p[7] else L), :] for ci, p in zip(c, probs)]

    blocks = [lambda x, sh=m["same_head"]: _bf(jnp.where(sh, rep(x), 0.0)) for m in mk]
    an, rn, vn, v_bd, bk_end, btk_bd, e_l = [], [], [], [], [], [], []
    for (r, v, kk, lw, kd, b, _, _), bd, ci, cl in zip(probs, blocks, c, c_l):
        e_nc = jnp.exp(-ci)
        e_end = jnp.exp(cl - ci)
        an.append(_bf(-kk * jnp.exp(ci - lw)))
        rn.append(_bf(r * jnp.exp(ci)))
        btk_bd.append(jnp.concatenate([bd(b * e_nc), bd(kd * e_nc)], axis=0))
        vn.append(_bf(v))
        v_bd.append(bd(v))
        bk_end.append(jnp.concatenate([_bf(b * e_end), _bf(kd * e_end)], axis=0))
        e_l.append(jnp.exp(cl))

    stack = lambda x, z: jnp.concatenate([x, z], axis=0)
    a_pow, t, akr, r_b = [], [], [], []
    for m, a_n, r_n, bk in zip(mk, an, rn, btk_bd):
        sn = _dot_nt(stack(a_n, r_n), bk)
        a_ab = jnp.where(m["strict_n"], sn[:L, :gw], 0.0)
        a_pow.append(a_ab)
        t.append(jnp.where(m["eye_n"], 1.0, a_ab))
        akr.append(stack(_bf(jnp.where(m["strict_n"], sn[:L, gw:], 0.0)),
                         _bf(jnp.where(m["incl_n"], sn[L:, gw:], 0.0))))
        r_b.append(_bf(jnp.where(m["incl_n"], sn[L:, :gw], 0.0)))
    tick()
    kv = each(_dot, akr, v_bd)
    a_bd = [bd(a) for bd, a in zip(blocks, a_pow)]
    tb = [_bf(ti) for ti in t]
    a_pow = [_dot(_bf(a), ab) for a, ab in zip(a_pow, a_bd)]
    for level in range(5):
        a_bd = [bd(a) for bd, a in zip(blocks, a_pow)]
        tick()
        if level < 4:
            res = [_dot(stack(tbi, _bf(a)), ab) for tbi, a, ab in zip(tb, a_pow, a_bd)]
            t = [ti + ri[:L] for ti, ri in zip(t, res)]
            a_pow = [ri[L:] for ri in res]
        else:
            t = [ti + _dot(tbi, ab) for ti, tbi, ab in zip(t, tb, a_bd)]
        tb = [_bf(ti) for ti in t]
    h_bf = [_bf(p[6]) for p in probs]
    hx = [_dot(stack(a_n, r_n), h) for a_n, r_n, h in zip(an, rn, h_bf)]
    x_bd = [bd(hxi[:L] + kvi[:L]) for bd, hxi, kvi in zip(blocks, hx, kv)]
    tick()
    u = [_dot(tbi, xi) for tbi, xi in zip(tb, x_bd)]
    tick()
    y = [hxi[L:] + _dot(rb, bd(ui)) + kvi[L:]
         for bd, hxi, rb, ui, kvi in zip(blocks, hx, r_b, u, kv)]
    e_col = [jnp.broadcast_to(el, (gw, gw)).T for el in e_l]
    h_new = []
    for m, ec, p, bke, ui, v_n in zip(mk, e_col, probs, bk_end, u, vn):
        upd = _dot_tn(bke, jnp.concatenate([_bf(ui), v_n], axis=0))
        h_new.append(ec * p[6] + jnp.where(m["same_head"], upd, 0.0))
    if side is not None:
        for _ in side:
            pass
    return list(zip(y, h_new))


def _chunk_row_block(b, i, *, reverse, n_lat, seq, ctx_len):
    nc = ctx_len // CHUNK
    nl = seq // CHUNK
    ci = (nc - 1 - i) if reverse else i
    li = (nl - 1 - (i - nc)) if reverse else (i - nc)
    return jnp.where(i < nc, (n_lat + b * ctx_len) // CHUNK + ci, b * nl + li)


def _gla_stages(qkvf, gdf, qkvb, gdb, gw2_ref, gb_ref, of_ref, ob_ref, sf_ref, sb_ref):
    dv, dk = sf_ref.shape[1:]
    L = CHUNK
    wk = B_HEADS * dk
    t_r = lax.broadcasted_iota(jnp.int32, (L, L), 0)
    t_c = lax.broadcasted_iota(jnp.int32, (L, L), 1)

    qh, kh, keh, vh, dech, keeph, sth = [], [], [], [], [], [], []
    for d, (qkv_ref, gd_ref, st_ref) in enumerate(((qkvf, gdf, sf_ref), (qkvb, gdb, sb_ref))):
        reverse = d == 1
        keep = (t_c >= t_r) if reverse else (t_c <= t_r)
        tri = _bf(jnp.where(keep, 1.0, 0.0))
        z = _dot(_bf(gd_ref[...]), gw2_ref[d]) + gb_ref[d:d + 1, :]
        la = -_softplus(-z) * (1.0 / B_GATE_TAU)
        cum = _dot_exact_lhs(tri, la)
        last = 0 if reverse else L - 1
        cl = cum[last:last + 1, :]
        q_in = _bf(qkv_ref[:, :wk] * (dk ** -0.5) * jnp.exp(cum))
        k_in = _bf(qkv_ref[:, wk:2 * wk] * jnp.exp(-cum))
        k_end = _bf(qkv_ref[:, wk:2 * wk] * jnp.exp(cl - cum))
        dec = jnp.exp(cl)
        for h in range(B_HEADS):
            ks = slice(h * dk, (h + 1) * dk)
            qh.append(q_in[:, ks])
            kh.append(k_in[:, ks])
            keh.append(k_end[:, ks])
            vh.append(_bf(qkv_ref[:, 2 * wk + h * dv:2 * wk + (h + 1) * dv]))
            dech.append(dec[:, ks])
            keeph.append(keep)
            sth.append(st_ref[h])
        yield

    sc = [_bf(jnp.where(kp, _dot_nt(q, k), 0.0)) for kp, q, k in zip(keeph, qh, kh)]
    yield
    inter = [_dot_nt(q, _bf(st)) for q, st in zip(qh, sth)]
    yield
    kvt = [_dot_tn(v, ke) for v, ke in zip(vh, keh)]
    yield
    out = [_dot(s_, v) + it for s_, v, it in zip(sc, vh, inter)]
    yield
    for d, (o_ref, st_ref) in enumerate(((of_ref, sf_ref), (ob_ref, sb_ref))):
        for h in range(B_HEADS):
            i = d * B_HEADS + h
            o_ref[:, h * dv:(h + 1) * dv] = out[i]
            st_ref[h] = sth[i] * dech[i] + kvt[i]


def _mix_scan_kernel(rvkf, lwf, df, rvkb, lwb, db, qkvf, gdf, qkvb, gdb, gw2_ref, gb_ref,
                     yf_ref, yb_ref, of_ref, ob_ref, hf_ref, hb_ref, sf_ref, sb_ref):
    @pl.when(pl.program_id(1) == 0)
    def _():
        for ref in (hf_ref, hb_ref, sf_ref, sb_ref):
            ref[...] = jnp.zeros_like(ref)

    gs = hf_ref.shape[0]
    aw = lwf.shape[1]
    probs = []
    for gi in range(gs):
        sl = slice(gi * GROUP_W, (gi + 1) * GROUP_W)
        f32 = lambda ref, j: ref[:, j * aw + gi * GROUP_W:j * aw + (gi + 1) * GROUP_W].astype(F32)
        for rvk, lw, dd, h_ref, rev in ((rvkf, lwf, df, hf_ref, False), (rvkb, lwb, db, hb_ref, True)):
            probs.append((f32(rvk, 0), f32(rvk, 1), f32(rvk, 2), lw[:, sl], f32(dd, 0), f32(dd, 1),
                          h_ref[gi], rev))
    gla = _gla_stages(qkvf, gdf, qkvb, gdb, gw2_ref, gb_ref, of_ref, ob_ref, sf_ref, sb_ref)
    res = _rwkv_chunks(probs, side=gla)
    for gi in range(gs):
        sl = slice(gi * GROUP_W, (gi + 1) * GROUP_W)
        yf_ref[:, sl], hf_ref[gi] = res[2 * gi]
        yb_ref[:, sl], hb_ref[gi] = res[2 * gi + 1]


def _mix_scan(rvk, d0, d1, lw0, lw1, pb, gw2p, gb, *, batch, n_lat, seq, ctx_len, wk, wv, col0):
    rows, aw = lw0.shape
    assert aw == GROUP_W * RWKV_GROUPS_PER_STEP
    dk, dv = wk // B_HEADS, wv // B_HEADS
    qkv_w = 2 * wk + wv
    assert col0 % qkv_w == 0
    geo = dict(n_lat=n_lat, seq=seq, ctx_len=ctx_len)
    gd_blk = (col0 + 2 * wk + 2 * wv) // LANE

    def specs(reverse):
        rb = lambda b, i: _chunk_row_block(b, i, reverse=reverse, **geo)
        row_blk = lambda w, cb=0: pl.BlockSpec((CHUNK, w), lambda b, i: (rb(b, i), cb))
        rw = [row_blk(3 * aw), row_blk(aw), row_blk(2 * aw)]
        gla = [row_blk(qkv_w, col0 // qkv_w), row_blk(LANE, gd_blk)]
        return rw, gla, row_blk(aw), row_blk(wv)

    rwf, glaf, yf, of = specs(False)
    rwb, glab, yb, ob = specs(True)
    sds_a = jax.ShapeDtypeStruct((rows, aw), F32)
    sds_b = jax.ShapeDtypeStruct((rows, wv), F32)
    gs = RWKV_GROUPS_PER_STEP
    return pl.pallas_call(
        _mix_scan_kernel,
        grid=(batch, (seq + ctx_len) // CHUNK),
        in_specs=rwf + rwb + glaf + glab + [
            pl.BlockSpec((2, LANE, wk), lambda b, i: (0, 0, 0)),
            pl.BlockSpec((2, wk), lambda b, i: (0, 0)),
        ],
        out_specs=[yf, yb, of, ob],
        out_shape=[sds_a, sds_a, sds_b, sds_b],
        scratch_shapes=[pltpu.VMEM((gs, GROUP_W, GROUP_W), F32),
                        pltpu.VMEM((gs, GROUP_W, GROUP_W), F32),
                        pltpu.VMEM((B_HEADS, dv, dk), F32), pltpu.VMEM((B_HEADS, dv, dk), F32)],
        compiler_params=_cparams(("parallel", "arbitrary")),
    )(rvk, lw0, d0, rvk, lw1, d1, pb, pb, pb, pb, gw2p, gb)


def _mixer_out_kernel(yf, yb, bonus, g, of, ob, gate, xa_ref, xb_ref, mod_ref, lnw, lnb, gn, e_ref,
                      w_ref, o_ref, h_ref, *, aw, dv, n_a_tiles):
    e_bf = e_ref[...]
    y = yf[...] + yb[...]
    inv = 1.0 / A_HEAD_DIM
    mean = _head_sum(y, e_bf) * inv
    dlt = y - mean
    var = _head_sum(dlt * dlt, e_bf) * inv
    yn = dlt * lax.rsqrt(var + A_LN_EPS) * lnw[...] + lnb[...]
    h_ref[:, :aw] = _bf((yn + bonus[...]) * g[...])

    o = of[...] + ob[...]
    gt = _silu(gate[...])
    for h in range(B_HEADS):
        sl = slice(h * dv, (h + 1) * dv)
        oh = o[:, sl]
        ms = jnp.mean(oh * oh, axis=-1, keepdims=True)
        h_ref[:, aw + h * dv:aw + (h + 1) * dv] = _bf(oh * lax.rsqrt(ms + NORM_EPS) * gn[...] * gt[:, sl])

    x = jnp.where(pl.program_id(0) < n_a_tiles, xa_ref[...], xb_ref[...])
    o_ref[...] = x + mod_ref[2:3, :] * _dot(h_ref[...], w_ref[...])


def _mixer_out(yf, yb, bonus, g, of, ob, pb, xa, xb, mods_l, ln_w, ln_b, gla_norm, e_bf, w_bf,
               seg_of_tile, *, wk, col0):
    d = xa.shape[1]
    rows = xa.shape[0] + xb.shape[0]
    aw = yf.shape[1]
    wv = of.shape[1]
    dv = wv // B_HEADS
    tm = 256
    tiles_per_row_tile = ROW_TILE // tm
    n_a_tiles = xa.shape[0] // tm
    rs = lambda w: pl.BlockSpec((tm, w), lambda i: (i, 0))
    full = lambda shape: pl.BlockSpec(shape, lambda i: (0,) * len(shape))
    return pl.pallas_call(
        functools.partial(_mixer_out_kernel, aw=aw, dv=dv, n_a_tiles=n_a_tiles),
        grid=(rows // tm,),
        in_specs=[
            rs(aw), rs(aw), rs(aw), rs(aw), rs(wv), rs(wv),
            pl.BlockSpec((tm, wv), lambda i: (i, (col0 + 2 * wk + wv) // wv)),
        ] + _two_source_specs(tm, d, n_a_tiles) + [
            pl.BlockSpec((None, N_MOD, d), lambda i: (seg_of_tile(i // tiles_per_row_tile), 0, 0)),
            full((1, aw)), full((1, aw)), full((1, dv)), full((GROUP_W, GROUP_W)),
            full((aw + wv, d)),
        ],
        out_specs=rs(d),
        out_shape=jax.ShapeDtypeStruct((rows, d), F32),
        scratch_shapes=[pltpu.VMEM((tm, aw + wv), BF16)],
        compiler_params=_cparams(("parallel",)),
    )(yf, yb, bonus, g, of, ob, pb, xa, xb, mods_l, ln_w.reshape(1, aw), ln_b.reshape(1, aw),
      gla_norm.reshape(1, dv), e_bf, w_bf)


def _lru_kernel(*refs, reverse, n_blk, combine, cols):
    if combine:
        (x_ref, xp_ref, xn_ref, h0_ref, cw_ref, cb_ref, wc_ref, ba_ref, bx_ref, lam_ref,
         hf_ref, gate_ref, out_ref, hl_ref, a_sc, u_sc, hs_sc, carry_sc) = refs
    else:
        (x_ref, xp_ref, xn_ref, h0_ref, cw_ref, cb_ref, wc_ref, ba_ref, bx_ref, lam_ref,
         out_ref, hl_ref, a_sc, u_sc, hs_sc, carry_sc) = refs
    j = pl.program_id(2)
    blk = (n_blk - 1 - j) if reverse else j
    lb = x_ref.shape[0]
    ct = cw_ref.shape[1]
    cblk = wc_ref.shape[1]
    rowid = lax.broadcasted_iota(jnp.int32, (lb, 1), 0)
    decay_rate = -C_CONST * _softplus(-lam_ref[...])

    @pl.when(j == 0)
    def _():
        carry_sc[...] = h0_ref[0:1, :]

    for q in (range(cols - 1, -1, -1) if reverse else range(cols)):
        lanes = slice(q * ct, (q + 1) * ct)
        x = x_ref[:, lanes]
        if q > 0:
            prev = slice((q - 1) * ct, q * ct)
            p6, p7 = x_ref[lb - 2:lb - 1, prev], x_ref[lb - 1:lb, prev]
        else:
            p6 = jnp.where(blk == 0, 0.0, xp_ref[SUBLANE - 2:SUBLANE - 1, :])
            p7 = jnp.where(blk == 0, 0.0, xp_ref[SUBLANE - 1:SUBLANE, :])
        if q < cols - 1:
            n0 = x_ref[0:1, (q + 1) * ct:(q + 2) * ct]
        else:
            n0 = jnp.where(blk == n_blk - 1, 0.0, xn_ref[0:1, :])
        xm1 = jnp.where(rowid == 0, p7, pltpu.roll(x, 1, 0))
        xm2 = jnp.where(rowid == 0, p6, jnp.where(rowid == 1, p7, pltpu.roll(x, 2, 0)))
        xp1 = jnp.where(rowid == lb - 1, n0, pltpu.roll(x, lb - 1, 0))
        xs = (cb_ref[...] + cw_ref[0:1, :] * xm2 + cw_ref[1:2, :] * xm1 + cw_ref[2:3, :] * x
              + cw_ref[3:4, :] * xp1)

        for n in range(ct // cblk):
            sl = slice(n * cblk, (n + 1) * cblk)
            xb = xs[:, sl]
            ri = _dot(_bf(xb), wc_ref[n])
            rg = _sigmoid(ri[:, :cblk] + ba_ref[:, sl])
            ig = _sigmoid(ri[:, cblk:] + bx_ref[:, sl])
            a = jnp.exp(decay_rate[:, sl] * rg)
            a_sc[q, :, sl] = a
            u_sc[q, :, sl] = jnp.sqrt(1.0 - a * a) * (ig * xb)

        def step(t, h, q=q):
            tt = (lb - 1 - t) if reverse else t
            h = a_sc[q, pl.ds(tt, 1), :] * h + u_sc[q, pl.ds(tt, 1), :]
            hs_sc[q, pl.ds(tt, 1), :] = h
            return h

        carry_sc[...] = lax.fori_loop(0, lb, step, carry_sc[...], unroll=8)
        if combine:
            out_ref[:, lanes] = ((hs_sc[q] + hf_ref[q * lb:(q + 1) * lb, :])
                                 * jax.nn.gelu(gate_ref[:, lanes]))
        else:
            out_ref[q * lb:(q + 1) * lb, :] = hs_sc[q]
    hl_ref[...] = jnp.broadcast_to(carry_sc[...], hl_ref.shape)


def _lru_pass(x_arr, x_specs, h0, conv_w, conv_b, wcat, ba, bx, lam, out_shape, out_spec, extra,
              *, batch, n_blk, lb, cols, reverse):
    c = conv_w.shape[1]
    ct = LRU_CHANNEL_TILE
    nct = c // ct
    cblk = wcat.shape[1]
    chan = lambda rows_: pl.BlockSpec((rows_, ct), lambda b, k, j: (0, k))
    in_specs = list(x_specs) + [
        pl.BlockSpec((SUBLANE, ct), lambda b, k, j: (b, k)),
        chan(C_CONV), chan(1),
        pl.BlockSpec((ct // cblk, cblk, 2 * cblk), lambda b, k, j: (k, 0, 0)),
        chan(1), chan(1), chan(1),
    ] + [s for _, s in extra]
    args = [x_arr, x_arr, x_arr, h0, conv_w, conv_b.reshape(1, c), wcat, ba.reshape(1, c),
            bx.reshape(1, c), lam.reshape(1, c)] + [a for a, _ in extra]
    return pl.pallas_call(
        functools.partial(_lru_kernel, reverse=reverse, n_blk=n_blk, combine=bool(extra),
                          cols=cols),
        grid=(batch, nct, n_blk),
        in_specs=in_specs,
        out_specs=[out_spec, pl.BlockSpec((SUBLANE, ct), lambda b, k, j: (b, k))],
        out_shape=[out_shape, jax.ShapeDtypeStruct((batch * SUBLANE, c), F32)],
        scratch_shapes=[pltpu.VMEM((cols, lb, ct), F32), pltpu.VMEM((cols, lb, ct), F32),
                        pltpu.VMEM((cols, lb, ct), F32), pltpu.VMEM((1, ct), F32)],
        compiler_params=_cparams(("parallel", "parallel", "arbitrary")),
    )(*args)


def _rglru(hw_ctx, gate_grid, x_grid, conv_w, conv_b, wa, ba, wx, bx, lam, *, batch, n_lat, seq,
           ctx_len):
    c = conv_w.shape[1]
    ct = LRU_CHANNEL_TILE
    rows = seq // GRID_W
    lbc = 128
    ncb = ctx_len // lbc
    xcol = c // ct

    wcat = [_bf(jnp.concatenate([wa[d], wx[d]], axis=-1)) for d in range(2)]
    zeros_state = jnp.zeros((batch * SUBLANE, c), F32)

    def ctx_specs(reverse):
        blk = (lambda j: ncb - 1 - j) if reverse else (lambda j: j)
        per8 = lbc // SUBLANE
        cur = pl.BlockSpec((lbc, ct), lambda b, k, j: (b * ncb + blk(j), xcol + k))
        prv = pl.BlockSpec((SUBLANE, ct), lambda b, k, j: (
            b * ncb * per8 + jnp.maximum(blk(j) * per8 - 1, 0), xcol + k))
        nxt = pl.BlockSpec((SUBLANE, ct), lambda b, k, j: (
            b * ncb * per8 + jnp.minimum((blk(j) + 1) * per8, ncb * per8 - 1), xcol + k))
        return cur, prv, nxt

    assert ct == c
    def lat_specs(reverse, kc):
        ncg = GRID_W // kc
        grp = (lambda j: ncg - 1 - j) if reverse else (lambda j: j)
        r8 = rows // SUBLANE
        cur = pl.BlockSpec((rows, kc * ct), lambda b, k, j: (b, grp(j)))
        prv = pl.BlockSpec((SUBLANE, ct), lambda b, k, j: (
            b * r8 + r8 - 1, jnp.maximum(grp(j) * kc - 1, 0)))
        nxt = pl.BlockSpec((SUBLANE, ct), lambda b, k, j: (
            b * r8, jnp.minimum(grp(j) * kc + kc, GRID_W - 1)))
        hf = pl.BlockSpec((kc * rows, ct), lambda b, k, j: (b * ncg + grp(j), 0))
        return (cur, prv, nxt), hf, cur

    common = dict(batch=batch)
    ctx_scratch_sds = jax.ShapeDtypeStruct((batch * ctx_len, c), F32)
    ctx_out = lambda reverse: pl.BlockSpec(
        (lbc, ct), lambda b, k, j: (b * ncb + ((ncb - 1 - j) if reverse else j), k))
    _, st = _lru_pass(hw_ctx, ctx_specs(False), zeros_state, conv_w, conv_b, wcat[0], ba[0], bx[0],
                      lam[0], ctx_scratch_sds, ctx_out(False), [], n_blk=ncb, lb=lbc, cols=1,
                      reverse=False, **common)
    hf_sds = jax.ShapeDtypeStruct((n_lat, c), F32)
    kc = LRU_COLS_FWD
    x_specs, hf_spec, _ = lat_specs(False, kc)
    hf, _ = _lru_pass(x_grid, x_specs, st, conv_w, conv_b, wcat[0], ba[0], bx[0], lam[0], hf_sds,
                      hf_spec, [], n_blk=GRID_W // kc, lb=rows, cols=kc, reverse=False, **common)
    _, st = _lru_pass(hw_ctx, ctx_specs(True), zeros_state, conv_w, conv_b, wcat[1], ba[1], bx[1],
                      lam[1], ctx_scratch_sds, ctx_out(True), [], n_blk=ncb, lb=lbc, cols=1,
                      reverse=True, **common)
    kc = LRU_COLS_BWD
    x_specs, hf_spec, grid_spec = lat_specs(True, kc)
    y_sds = jax.ShapeDtypeStruct((n_lat // GRID_W, GRID_W * c), F32)
    y, _ = _lru_pass(x_grid, x_specs, st, conv_w, conv_b, wcat[1], ba[1], bx[1], lam[1],
                     y_sds, grid_spec, [(hf, hf_spec), (gate_grid, grid_spec)],
                     n_blk=GRID_W // kc, lb=rows, cols=kc, reverse=True, **common)
    return y


def _pad_cols(w, n):
    return jnp.pad(w, ((0, 0),) * (w.ndim - 1) + ((0, n - w.shape[-1]),))


def kernel(x, c, ctx, c_ctx, mod_w, mod_b, norm1, norm2, mlp_w1, mlp_w2, ab_w_in, ab_w_out, rw_mu,
           rw_w0, rw_w2, rw_a0, rw_a2, rw_g2, rw_kk, rw_ka, rw_rk, rw_ln_w, rw_ln_b, gla_gw2,
           gla_gb, gla_norm, lru_w_in, lru_w_out, lru_conv_w, lru_conv_b, lru_wa, lru_ba, lru_wx,
           lru_bx, lru_lam, final_norm):
    batch, seq, d = x.shape
    ctx_len = ctx.shape[1]
    depth = mod_w.shape[0]
    n_lat = batch * seq
    aw = rw_w0.shape[-1]
    wk = gla_gb.shape[-1]
    wv = ab_w_out.shape[1] - aw
    a_cols = rw_mu.shape[-1]
    b_cols = ab_w_in.shape[-1] - a_cols
    assert depth == 2 and batch + 1 <= SUBLANE
    assert seq % ROW_TILE == 0 and (batch * ctx_len) % ROW_TILE == 0
    assert ctx_len % PREP_TILE == 0 and ctx_len % 128 == 0 and seq % (GRID_W * SUBLANE) == 0
    assert aw % GROUP_W == 0 and wv == aw and 2 * wk == wv

    tiles_per_seq = seq // ROW_TILE
    seg_of_tile = lambda i: jnp.minimum(i // tiles_per_seq, batch)

    x2 = x.reshape(n_lat, d)
    ctx2 = ctx.reshape(batch * ctx_len, d)
    cvec = jnp.concatenate(
        [c, c_ctx[None, :], jnp.zeros((SUBLANE - batch - 1, d), F32)], axis=0)
    mods = _mods(cvec, mod_w, mod_b)

    hid = jnp.arange(GROUP_W) // A_HEAD_DIM
    e_bf = _bf(hid[:, None] == hid[None, :])

    a_pad = -(-a_cols // LANE) * LANE
    col0 = -(-a_cols // (2 * wk)) * (2 * wk)
    n_tiles = 3
    tn = -(-(col0 + b_cols) // (n_tiles * LANE)) * LANE
    w_ab = _layer_split_bf16(ab_w_in, 0, a_cols, col0, n_tiles * tn)
    pa = pb = _norm_matmul(x2, ctx2, norm1[0], mods[0], w_ab, seg_of_tile, tn=tn)

    lora_w = 2 * A_LORA_W + 2 * A_LORA_A
    w2p = jnp.zeros((2, lora_w, aw), F32)
    a2p = jnp.zeros((2, lora_w, aw), F32)
    for dd in range(2):
        w2p = w2p.at[dd, dd * A_LORA_W:(dd + 1) * A_LORA_W].set(rw_w2[0, dd])
        a2p = a2p.at[dd, 2 * A_LORA_W + dd * A_LORA_A:2 * A_LORA_W + (dd + 1) * A_LORA_A].set(
            rw_a2[0, dd])
    (rvk, d0, d1, lw0, lw1, g_, bonus) = _rwkv_prep(
        pa, _pad_cols(rw_mu[0], a_pad), rw_w0[0], rw_a0[0], _bf(w2p), _bf(a2p), _bf(rw_g2[0]),
        rw_kk[0], rw_ka[0], rw_rk[0].reshape(-1), e_bf, n_lat=n_lat, seq=seq, ctx_len=ctx_len)
    gw2p = jnp.zeros((2, LANE, wk), F32)
    for dd in range(2):
        gw2p = gw2p.at[dd, dd * B_LORA:(dd + 1) * B_LORA].set(gla_gw2[0, dd])
    yf, yb, of, ob = _mix_scan(rvk, d0, d1, lw0, lw1, pb, _bf(gw2p), gla_gb[0],
                               batch=batch, n_lat=n_lat, seq=seq, ctx_len=ctx_len, wk=wk, wv=wv,
                               col0=col0)

    xs = _mixer_out(yf, yb, bonus, g_, of, ob, pb, x2, ctx2, mods[0], rw_ln_w[0], rw_ln_b[0],
                    gla_norm[0], e_bf, _layer_bf16(ab_w_out, 0), seg_of_tile, wk=wk, col0=col0)
    seg_of_mlp_tile = lambda i: jnp.minimum(i // (seq // MLP_TILE), batch)
    ctx_tiles = (batch * ctx_len) // ROW_TILE
    w1_bf, w2_bf = _layer_bf16(mlp_w1, 0), _layer_bf16(mlp_w2, 0)
    x_lat = _mlp(xs, norm2[0], mods[0], w1_bf, w2_bf, final_norm, seg_of_mlp_tile, tm=MLP_TILE,
                 tile0=0, n_tiles=n_lat // MLP_TILE, final_norm=False)
    x_ctx = _mlp(xs, norm2[0], mods[0], w1_bf, w2_bf, final_norm, seg_of_tile, tm=ROW_TILE,
                 tile0=n_lat // ROW_TILE, n_tiles=ctx_tiles, final_norm=False)

    w_lru = _layer_bf16(lru_w_in, 0)
    hw_ctx = _norm_matmul(x_ctx, None, norm1[1], mods[1], w_lru, lambda i: batch, tn=2048)
    gate_grid, x_grid = _norm_matmul_grid(x_lat, n_lat, seq // GRID_W, norm1[1], mods[1], w_lru)
    y = _rglru(hw_ctx, gate_grid, x_grid, lru_conv_w[0], lru_conv_b[0], lru_wa[0], lru_ba[0],
               lru_wx[0], lru_bx[0], lru_lam[0], batch=batch, n_lat=n_lat, seq=seq,
               ctx_len=ctx_len)
    xl = _proj_res(y, x_lat, n_lat, seq // GRID_W, mods[1], _layer_bf16(lru_w_out, 0))
    out = _mlp(xl, norm2[1], mods[1], _layer_bf16(mlp_w1, 1), _layer_bf16(mlp_w2, 1), final_norm,
               seg_of_mlp_tile,
               tm=MLP_TILE, tile0=0, n_tiles=n_lat // MLP_TILE, final_norm=True)
    return out.reshape(batch, seq, d)
```

```python
import functools

import jax
import jax.numpy as jnp
from jax import lax
from jax.experimental import pallas as pl
from jax.experimental.pallas import tpu as pltpu

F32 = jnp.float32
BF16 = jnp.bfloat16

NORM_EPS = 1e-6
GRID_W = 64
N_MOD = 6

A_HEAD_DIM = 64
A_LORA_W = 96
A_LORA_A = 96
A_LORA_G = 256
A_LN_EPS = 64e-5
HEADS_PER_GROUP = 4
GROUP_W = HEADS_PER_GROUP * A_HEAD_DIM
CHUNK = 64
RWKV_GROUPS_PER_STEP = 4

B_HEADS = 4
B_LORA = 16
B_GATE_TAU = 16.0

C_BLOCKS = 8
C_CONV = 4
C_CONST = 8.0

LANE = 128
SUBLANE = 8
ROW_TILE = 512
MLP_TILE = 1024
MLP_STEP_ELEMS = 1024 * 512
LRU_CHANNEL_TILE = 2048
LRU_COLS_FWD = 4
LRU_COLS_BWD = 1
CAST_BLOCK_ELEMS = 2 * 1024 * 1024
PREP_TILE = 256
VMEM_LIMIT = 56 * 1024 * 1024


def _cparams(sem):
    return pltpu.CompilerParams(dimension_semantics=sem, vmem_limit_bytes=VMEM_LIMIT)


def _bf(x):
    return x.astype(BF16)


def _dot(a, b):
    return jnp.dot(a, b, preferred_element_type=F32)


def _dot_nt(a, b):
    return lax.dot_general(a, b, (((1,), (1,)), ((), ())), preferred_element_type=F32)


def _dot_tn(a, b):
    return lax.dot_general(a, b, (((0,), (0,)), ((), ())), preferred_element_type=F32)


def _split3(x):
    hi = _bf(x)
    r1 = x - hi.astype(F32)
    mid = _bf(r1)
    lo = _bf(r1 - mid.astype(F32))
    return hi, mid, lo


def _dot_exact_lhs(m_bf, x):
    hi, mid, lo = _split3(x)
    return _dot(m_bf, hi) + _dot(m_bf, mid) + _dot(m_bf, lo)


def _dot_exact_rhs(x, m_bf):
    hi, mid, lo = _split3(x)
    return _dot(hi, m_bf) + _dot(mid, m_bf) + _dot(lo, m_bf)


def _softplus(x):
    return jnp.maximum(x, 0.0) + jnp.log(1.0 + jnp.exp(-jnp.abs(x)))


def _sigmoid(x):
    return 0.5 * jnp.tanh(0.5 * x) + 0.5


def _silu(x):
    return x * _sigmoid(x)


def _norm_mod(x, g, shift, scale):
    ms = jnp.mean(x * x, axis=-1, keepdims=True)
    return (x * lax.rsqrt(ms + NORM_EPS) * g) * (1.0 + scale) + shift


def _cast_kernel(w_ref, o_ref):
    o_ref[...] = _bf(w_ref[...])


def _layer_bf16(w, layer):
    _, r, c = w.shape
    tr = min(r, CAST_BLOCK_ELEMS // c)
    return pl.pallas_call(
        _cast_kernel,
        grid=(r // tr,),
        in_specs=[pl.BlockSpec((None, tr, c), lambda i: (layer, i, 0))],
        out_specs=pl.BlockSpec((tr, c), lambda i: (i, 0)),
        out_shape=jax.ShapeDtypeStruct((r, c), BF16),
        compiler_params=_cparams(("parallel",)),
    )(w)


def _mods_kernel(c_ref, w_ref, b_ref, o_ref):
    s = _silu(c_ref[...])
    o_ref[...] = _dot(_bf(s), _bf(w_ref[...])) + b_ref[...]


def _mods(cvec, mod_w, mod_b):
    depth, d, n = mod_w.shape
    tn = 1024
    out = pl.pallas_call(
        _mods_kernel,
        grid=(depth, n // tn),
        in_specs=[
            pl.BlockSpec((SUBLANE, d), lambda l, j: (0, 0)),
            pl.BlockSpec((None, d, tn), lambda l, j: (l, 0, j)),
            pl.BlockSpec((None, 1, tn), lambda l, j: (l, 0, j)),
        ],
        out_specs=pl.BlockSpec((None, SUBLANE, tn), lambda l, j: (l, 0, j)),
        out_shape=jax.ShapeDtypeStruct((depth, SUBLANE, n), F32),
        compiler_params=_cparams(("parallel", "parallel")),
    )(cvec, mod_w, mod_b.reshape(depth, 1, n))
    return out.reshape(depth, SUBLANE, N_MOD, d)


def _norm_matmul_kernel(xa_ref, xb_ref, g_ref, mod_ref, w_ref, o_ref, h_ref, *, n_a_tiles):
    i = pl.program_id(0)

    @pl.when(pl.program_id(1) == 0)
    def _():
        x = jnp.where(i < n_a_tiles, xa_ref[...], xb_ref[...])
        h_ref[...] = _bf(_norm_mod(x, g_ref[...], mod_ref[0:1, :], mod_ref[1:2, :]))

    o_ref[...] = _dot(h_ref[...], w_ref[...])


def _two_source_specs(tm, d, n_a_tiles):
    return [pl.BlockSpec((tm, d), lambda i, *_: (jnp.minimum(i, n_a_tiles - 1), 0)),
            pl.BlockSpec((tm, d), lambda i, *_: (jnp.maximum(i - n_a_tiles, 0), 0))]


def _norm_matmul(xa, xb, g, mods_l, w_bf, seg_of_tile, *, tn):
    d = xa.shape[1]
    n = w_bf.shape[1]
    tm = ROW_TILE
    n_a_tiles = xa.shape[0] // tm
    n_tiles = n_a_tiles + (0 if xb is None else xb.shape[0] // tm)
    return pl.pallas_call(
        functools.partial(_norm_matmul_kernel, n_a_tiles=n_a_tiles),
        grid=(n_tiles, n // tn),
        in_specs=_two_source_specs(tm, d, n_a_tiles) + [
            pl.BlockSpec((1, d), lambda i, j: (0, 0)),
            pl.BlockSpec((None, N_MOD, d), lambda i, j: (seg_of_tile(i), 0, 0)),
            pl.BlockSpec((d, tn), lambda i, j: (0, j)),
        ],
        out_specs=pl.BlockSpec((tm, tn), lambda i, j: (i, j)),
        out_shape=jax.ShapeDtypeStruct((n_tiles * tm, n), F32),
        scratch_shapes=[pltpu.VMEM((tm, d), BF16)],
        compiler_params=_cparams(("parallel", "arbitrary")),
    )(xa, xa if xb is None else xb, g.reshape(1, d), mods_l, w_bf)


def _grid_tile(rows):
    gr = min(rows, 64)
    gc = ROW_TILE // gr
    assert gc % SUBLANE == 0 and GRID_W % gc == 0 and rows % gr == 0
    return gr, gc


def _norm_matmul_grid_kernel(x_ref, g_ref, mod_ref, w_ref, o0_ref, o1_ref, h_ref, res_ref):
    j = pl.program_id(1)
    gr, gc, d = x_ref.shape

    @pl.when(j == 0)
    def _():
        h = _norm_mod(x_ref[...].reshape(gr * gc, d), g_ref[...], mod_ref[0:1, :], mod_ref[1:2, :])
        h_ref[...] = _bf(h)

    res = _dot(h_ref[...], w_ref[...])
    nh = w_ref.shape[1]
    nq = nh // LANE
    for q in range(nq):
        res_ref[q] = res[:, q * LANE:(q + 1) * LANE]

    def scatter(o_ref):
        for c in range(gc):
            for q in range(nq):
                o_ref[:, c * nh + q * LANE:c * nh + (q + 1) * LANE] = (
                    res_ref[q, pl.ds(c, gr, stride=gc), :])

    @pl.when(j == 0)
    def _():
        scatter(o0_ref)

    @pl.when(j == 1)
    def _():
        scatter(o1_ref)


def _norm_matmul_grid(x, n_lat, rows, g, mods_l, w_bf):
    d = x.shape[1]
    nh = w_bf.shape[1] // 2
    gr, gc = _grid_tile(rows)
    ncg = GRID_W // gc
    x3 = x.reshape(x.shape[0] // GRID_W, GRID_W, d)
    out_spec = pl.BlockSpec((gr, gc * nh), lambda i, j: (i // ncg, i % ncg))
    sds = jax.ShapeDtypeStruct((n_lat // GRID_W, GRID_W * nh), F32)
    return pl.pallas_call(
        _norm_matmul_grid_kernel,
        grid=(n_lat // ROW_TILE, 2),
        in_specs=[
            pl.BlockSpec((gr, gc, d), lambda i, j: (i // ncg, i % ncg, 0)),
            pl.BlockSpec((1, d), lambda i, j: (0, 0)),
            pl.BlockSpec((None, N_MOD, d), lambda i, j: ((i // ncg) * gr // rows, 0, 0)),
            pl.BlockSpec((d, nh), lambda i, j: (0, j)),
        ],
        out_specs=[out_spec, out_spec],
        out_shape=[sds, sds],
        scratch_shapes=[pltpu.VMEM((ROW_TILE, d), BF16),
                        pltpu.VMEM((nh // LANE, ROW_TILE, LANE), F32)],
        compiler_params=_cparams(("parallel", "arbitrary")),
    )(x3, g.reshape(1, d), mods_l, w_bf)


def _mlp_kernel(x_ref, g_ref, mod_ref, w1_ref, w2_ref, fn_ref, o_ref, h_ref, *, final_norm):
    j = pl.program_id(1)
    slab = 256
    n_slab = x_ref.shape[0] // slab

    @pl.when(j == 0)
    def _():
        for s in range(n_slab):
            rs = slice(s * slab, (s + 1) * slab)
            h = _bf(_norm_mod(x_ref[rs, :], g_ref[...], mod_ref[3:4, :], mod_ref[4:5, :]))
            h_ref[rs, :] = h
            a = _bf(jnp.square(jnp.maximum(_dot(h, w1_ref[...]), 0.0)))
            o_ref[rs, :] = _dot(a, w2_ref[...])

    @pl.when(j > 0)
    def _():
        a = _dot(h_ref[...], w1_ref[...])
        a = _bf(jnp.square(jnp.maximum(a, 0.0)))
        nsplit = 2
        wn = o_ref.shape[1] // nsplit
        for s in range(nsplit):
            o_ref[:, s * wn:(s + 1) * wn] += _dot(a, w2_ref[:, s * wn:(s + 1) * wn])

    @pl.when(j == pl.num_programs(1) - 1)
    def _():
        for s in range(n_slab):
            rs = slice(s * slab, (s + 1) * slab)
            xo = x_ref[rs, :] + mod_ref[5:6, :] * o_ref[rs, :]
            if final_norm:
                ms = jnp.mean(xo * xo, axis=-1, keepdims=True)
                xo = xo * lax.rsqrt(ms + NORM_EPS) * fn_ref[...]
            o_ref[rs, :] = xo


def _mlp(x, g, mods_l, w1_bf, w2_bf, fn, seg_of_tile, *, tm, tile0, n_tiles, final_norm):
    d = x.shape[1]
    dff = w1_bf.shape[1]
    tf = MLP_STEP_ELEMS // tm
    return pl.pallas_call(
        functools.partial(_mlp_kernel, final_norm=final_norm),
        grid=(n_tiles, dff // tf),
        in_specs=[
            pl.BlockSpec((tm, d), lambda i, j: (i + tile0, 0), pipeline_mode=pl.Buffered(1)),
            pl.BlockSpec((1, d), lambda i, j: (0, 0)),
            pl.BlockSpec((None, N_MOD, d), lambda i, j: (seg_of_tile(i + tile0), 0, 0)),
            pl.BlockSpec((d, tf), lambda i, j: (0, j)),
            pl.BlockSpec((tf, d), lambda i, j: (j, 0)),
            pl.BlockSpec((1, d), lambda i, j: (0, 0)),
        ],
        out_specs=pl.BlockSpec((tm, d), lambda i, j: (i, 0)),
        out_shape=jax.ShapeDtypeStruct((n_tiles * tm, d), F32),
        scratch_shapes=[pltpu.VMEM((tm, d), BF16)],
        compiler_params=_cparams(("parallel", "arbitrary")),
    )(x, g.reshape(1, d), mods_l, w1_bf, w2_bf, fn.reshape(1, d))


def _proj_res_kernel(y_ref, x_ref, mod_ref, w_ref, o_ref, yt_ref):
    k = w_ref.shape[0]
    gr, gc, d = x_ref.shape
    nq = k // LANE
    for c in range(gc):
        for q in range(nq):
            yt_ref[q, pl.ds(c, gr, stride=gc), :] = (
                y_ref[:, c * k + q * LANE:c * k + (q + 1) * LANE])
    yt = jnp.concatenate([_bf(yt_ref[q]) for q in range(nq)], axis=1)
    out = x_ref[...].reshape(gr * gc, d) + mod_ref[2:3, :] * _dot(yt, w_ref[...])
    o_ref[...] = out.reshape(gr, gc, d)


def _proj_res(y_grid, x, n_lat, rows, mods_l, w_bf):
    k, d = w_bf.shape
    gr, gc = _grid_tile(rows)
    ncg = GRID_W // gc
    x3 = x.reshape(x.shape[0] // GRID_W, GRID_W, d)
    tok_spec = pl.BlockSpec((gr, gc, d), lambda i: (i // ncg, i % ncg, 0))
    out = pl.pallas_call(
        _proj_res_kernel,
        grid=(n_lat // ROW_TILE,),
        in_specs=[
            pl.BlockSpec((gr, gc * k), lambda i: (i // ncg, i % ncg)),
            tok_spec,
            pl.BlockSpec((None, N_MOD, d), lambda i: ((i // ncg) * gr // rows, 0, 0)),
            pl.BlockSpec((k, d), lambda i: (0, 0)),
        ],
        out_specs=tok_spec,
        out_shape=jax.ShapeDtypeStruct((n_lat // GRID_W, GRID_W, d), F32),
        scratch_shapes=[pltpu.VMEM((k // LANE, ROW_TILE, LANE), F32)],
        compiler_params=_cparams(("parallel",)),
    )(y_grid, x3, mods_l, w_bf)
    return out.reshape(n_lat, d)


def _head_sum(z, e_bf):
    parts = []
    for gi in range(z.shape[1] // GROUP_W):
        zs = z[:, gi * GROUP_W:(gi + 1) * GROUP_W]
        parts.append(_dot_exact_rhs(zs, e_bf))
    return jnp.concatenate(parts, axis=1)


def _rwkv_prep_kernel(p_ref, pp_ref, pn_ref, mu_ref, w0_ref, a0_ref, w2_ref, a2_ref, g2_ref,
                      kk_ref, ka_ref, rk_ref, e_ref,
                      rvk_o, d0_o, d1_o, lw0_o, lw1_o, g_o, bonus_o,
                      *, n_lat_tiles, lat_tiles_per_seq, ctx_tiles_per_seq, aw):
    i = pl.program_id(0)
    tp = p_ref.shape[0]
    in_ctx = i >= n_lat_tiles
    pos = jnp.where(in_ctx, (i - n_lat_tiles) % ctx_tiles_per_seq, i % lat_tiles_per_seq)
    last = jnp.where(in_ctx, ctx_tiles_per_seq - 1, lat_tiles_per_seq - 1)
    is_first = pos == 0
    is_last = pos == last
    rowid = lax.broadcasted_iota(jnp.int32, (tp, 1), 0)

    def shifted(c0, c1):
        x = p_ref[:, c0:c1]
        prow = jnp.where(is_first, 0.0, pp_ref[SUBLANE - 1:SUBLANE, c0:c1])
        nrow = jnp.where(is_last, 0.0, pn_ref[0:1, c0:c1])
        xp = jnp.where(rowid == 0, prow, pltpu.roll(x, 1, 0))
        xn = jnp.where(rowid == tp - 1, nrow, pltpu.roll(x, tp - 1, 0))
        return x + mu_ref[0:1, c0:c1] * (xp - x) + mu_ref[1:2, c0:c1] * (xn - x)

    e_bf = e_ref[...]
    r = shifted(0, aw)
    k = shifted(aw, 2 * aw)
    v = shifted(2 * aw, 3 * aw)
    gd = shifted(3 * aw, 3 * aw + A_LORA_G)
    lo0 = 3 * aw + A_LORA_G
    xl = shifted(lo0, lo0 + 2 * A_LORA_W + 2 * A_LORA_A)

    rvk_o[:, :aw] = r.astype(rvk_o.dtype)
    rvk_o[:, aw:2 * aw] = v.astype(rvk_o.dtype)
    g_o[...] = _dot(_bf(_sigmoid(gd)), g2_ref[...])

    kx = k * kk_ref[...]
    kn = jnp.sqrt(_head_sum(kx * kx, e_bf))
    kk = kx / jnp.maximum(kn, 1e-12)
    rvk_o[:, 2 * aw:] = kk.astype(rvk_o.dtype)

    xl_t = _bf(jnp.tanh(xl))
    xl_b = _bf(xl)
    kd_sum = None
    for d, (lw_o, d_o) in enumerate(((lw0_o, d0_o), (lw1_o, d1_o))):
        w_log = -_softplus(-(w0_ref[d:d + 1, :] + _dot(xl_t, w2_ref[d]))) - 0.5
        lw_o[...] = -jnp.exp(w_log)
        asig = _sigmoid(a0_ref[d:d + 1, :] + _dot(xl_b, a2_ref[d]))
        kd = k * (1.0 + (asig - 1.0) * ka_ref[...])
        d_o[:, :aw] = kd.astype(d_o.dtype)
        d_o[:, aw:] = (kk * asig).astype(d_o.dtype)
        kd_sum = kd if kd_sum is None else kd_sum + kd
    bonus_o[...] = _head_sum(r * rk_ref[...] * kd_sum, e_bf) * v


def _rwkv_prep(pa, mu_p, w0, a0, w2p, a2p, g2, k_k, k_a, r_k, e_bf, *, n_lat, seq, ctx_len):
    r = pa.shape[0]
    ncol = mu_p.shape[1]
    aw = w0.shape[1]
    tp = PREP_TILE
    hb = tp // SUBLANE
    nblk8 = r // SUBLANE
    lw = 2 * A_LORA_W + 2 * A_LORA_A
    row = lambda a: a.reshape(1, aw)
    full = lambda shape: pl.BlockSpec(shape, lambda i: (0,) * len(shape))
    outs = [(3 * aw, BF16), (2 * aw, BF16), (2 * aw, BF16), (aw, F32), (aw, F32), (aw, F32),
            (aw, F32)]
    return pl.pallas_call(
        functools.partial(_rwkv_prep_kernel, n_lat_tiles=n_lat // tp, lat_tiles_per_seq=seq // tp,
                          ctx_tiles_per_seq=ctx_len // tp, aw=aw),
        grid=(r // tp,),
        in_specs=[
            pl.BlockSpec((tp, ncol), lambda i: (i, 0)),
            pl.BlockSpec((SUBLANE, ncol), lambda i: (jnp.maximum(i * hb - 1, 0), 0)),
            pl.BlockSpec((SUBLANE, ncol), lambda i: (jnp.minimum((i + 1) * hb, nblk8 - 1), 0)),
            full((2, ncol)), full((2, aw)), full((2, aw)),
            full((2, lw, aw)), full((2, lw, aw)), full((A_LORA_G, aw)),
            full((1, aw)), full((1, aw)), full((1, aw)), full((GROUP_W, GROUP_W)),
        ],
        out_specs=[pl.BlockSpec((tp, w), lambda i: (i, 0)) for w, _ in outs],
        out_shape=[jax.ShapeDtypeStruct((r, w), dt) for w, dt in outs],
        compiler_params=_cparams(("parallel",)),
    )(pa, pa, pa, mu_p, w0, a0, w2p, a2p, g2, row(k_k), row(k_a), row(r_k), e_bf)


def _rwkv_masks(reverse):
    L = CHUNK
    gw = GROUP_W
    row = lax.broadcasted_iota(jnp.int32, (gw, gw), 0)
    col = lax.broadcasted_iota(jnp.int32, (gw, gw), 1)
    same_head = (row >> 6) == (col >> 6)
    t_n = lax.broadcasted_iota(jnp.int32, (L, gw), 0)
    s_n = lax.broadcasted_iota(jnp.int32, (L, gw), 1) & (L - 1)
    strict_n = (t_n < s_n) if reverse else (t_n > s_n)
    incl_n = (t_n <= s_n) if reverse else (t_n >= s_n)
    t_r = lax.broadcasted_iota(jnp.int32, (L, L), 0)
    t_c = lax.broadcasted_iota(jnp.int32, (L, L), 1)
    tri = _bf(jnp.where((t_c >= t_r) if reverse else (t_c <= t_r), 1.0, 0.0))
    return dict(same_head=same_head, strict_n=strict_n, incl_n=incl_n, eye_n=t_n == s_n, tri=tri)


def _rwkv_chunks(probs, side=None):
    L = CHUNK
    gw = GROUP_W
    tick = (lambda: None) if side is None else (lambda: next(side, None))
    masks = {rv: _rwkv_masks(rv) for rv in sorted({p[7] for p in probs})}
    mk = [masks[p[7]] for p in probs]
    each = lambda f, *cols: [f(*args) for args in zip(*cols)]
    rep = lambda x: jnp.concatenate([x] * HEADS_PER_GROUP, axis=0)

    c = [_dot_exact_lhs(m["tri"], p[3]) for m, p in zip(mk, probs)]
    c_l = [ci[(0 if p[7] else L - 1):(1 if p[7] else L), :] for ci, p in zip(c, probs)]

    blocks = [lambda x, sh=m["same_head"]: _bf(jnp.where(sh, rep(x), 0.0)) for m in mk]
    an, rn, vn, v_bd, bk_end, btk_bd, e_l = [], [], [], [], [], [], []
    for (r, v, kk, lw, kd, b, _, _), bd, ci, cl in zip(probs, blocks, c, c_l):
        e_nc = jnp.exp(-ci)
        e_end = jnp.exp(cl - ci)
        an.append(_bf(-kk * jnp.exp(ci - lw)))
        rn.append(_bf(r * jnp.exp(ci)))
        btk_bd.append(jnp.concatenate([bd(b * e_nc), bd(kd * e_nc)], axis=0))
        vn.append(_bf(v))
        v_bd.append(bd(v))
        bk_end.append(jnp.concatenate([_bf(b * e_end), _bf(kd * e_end)], axis=0))
        e_l.append(jnp.exp(cl))

    stack = lambda x, z: jnp.concatenate([x, z], axis=0)
    a_pow, t, akr, r_b = [], [], [], []
    for m, a_n, r_n, bk in zip(mk, an, rn, btk_bd):
        sn = _dot_nt(stack(a_n, r_n), bk)
        a_ab = jnp.where(m["strict_n"], sn[:L, :gw], 0.0)
        a_pow.append(a_ab)
        t.append(jnp.where(m["eye_n"], 1.0, a_ab))
        akr.append(stack(_bf(jnp.where(m["strict_n"], sn[:L, gw:], 0.0)),
                         _bf(jnp.where(m["incl_n"], sn[L:, gw:], 0.0))))
        r_b.append(_bf(jnp.where(m["incl_n"], sn[L:, :gw], 0.0)))
    tick()
    kv = each(_dot, akr, v_bd)
    a_bd = [bd(a) for bd, a in zip(blocks, a_pow)]
    tb = [_bf(ti) for ti in t]
    a_pow = [_dot(_bf(a), ab) for a, ab in zip(a_pow, a_bd)]
    for level in range(5):
        a_bd = [bd(a) for bd, a in zip(blocks, a_pow)]
        tick()
        if level < 4:
            res = [_dot(stack(tbi, _bf(a)), ab) for tbi, a, ab in zip(tb, a_pow, a_bd)]
            t = [ti + ri[:L] for ti, ri in zip(t, res)]
            a_pow = [ri[L:] for ri in res]
        else:
            t = [ti + _dot(tbi, ab) for ti, tbi, ab in zip(t, tb, a_bd)]
        tb = [_bf(ti) for ti in t]
    h_bf = [_bf(p[6]) for p in probs]
    hx = [_dot(stack(a_n, r_n), h) for a_n, r_n, h in zip(an, rn, h_bf)]
    x_bd = [bd(hxi[:L] + kvi[:L]) for bd, hxi, kvi in zip(blocks, hx, kv)]
    tick()
    u = [_dot(tbi, xi) for tbi, xi in zip(tb, x_bd)]
    tick()
    y = [hxi[L:] + _dot(rb, bd(ui)) + kvi[L:]
         for bd, hxi, rb, ui, kvi in zip(blocks, hx, r_b, u, kv)]
    e_col = [jnp.broadcast_to(el, (gw, gw)).T for el in e_l]
    h_new = []
    for m, ec, p, bke, ui, v_n in zip(mk, e_col, probs, bk_end, u, vn):
        upd = _dot_tn(bke, jnp.concatenate([_bf(ui), v_n], axis=0))
        h_new.append(ec * p[6] + jnp.where(m["same_head"], upd, 0.0))
    if side is not None:
        for _ in side:
            pass
    return list(zip(y, h_new))


def _chunk_row_block(b, i, *, reverse, n_lat, seq, ctx_len):
    nc = ctx_len // CHUNK
    nl = seq // CHUNK
    ci = (nc - 1 - i) if reverse else i
    li = (nl - 1 - (i - nc)) if reverse else (i - nc)
    return jnp.where(i < nc, (n_lat + b * ctx_len) // CHUNK + ci, b * nl + li)


def _gla_stages(qkvf, gdf, qkvb, gdb, gw2_ref, gb_ref, of_ref, ob_ref, sf_ref, sb_ref):
    dv, dk = sf_ref.shape[1:]
    L = CHUNK
    wk = B_HEADS * dk
    t_r = lax.broadcasted_iota(jnp.int32, (L, L), 0)
    t_c = lax.broadcasted_iota(jnp.int32, (L, L), 1)

    qh, kh, keh, vh, dech, keeph, sth = [], [], [], [], [], [], []
    for d, (qkv_ref, gd_ref, st_ref) in enumerate(((qkvf, gdf, sf_ref), (qkvb, gdb, sb_ref))):
        reverse = d == 1
        keep = (t_c >= t_r) if reverse else (t_c <= t_r)
        tri = _bf(jnp.where(keep, 1.0, 0.0))
        z = _dot(_bf(gd_ref[...]), gw2_ref[d]) + gb_ref[d:d + 1, :]
        la = -_softplus(-z) * (1.0 / B_GATE_TAU)
        cum = _dot_exact_lhs(tri, la)
        last = 0 if reverse else L - 1
        cl = cum[last:last + 1, :]
        q_in = _bf(qkv_ref[:, :wk] * (dk ** -0.5) * jnp.exp(cum))
        k_in = _bf(qkv_ref[:, wk:2 * wk] * jnp.exp(-cum))
        k_end = _bf(qkv_ref[:, wk:2 * wk] * jnp.exp(cl - cum))
        dec = jnp.exp(cl)
        for h in range(B_HEADS):
            ks = slice(h * dk, (h + 1) * dk)
            qh.append(q_in[:, ks])
            kh.append(k_in[:, ks])
            keh.append(k_end[:, ks])
            vh.append(_bf(qkv_ref[:, 2 * wk + h * dv:2 * wk + (h + 1) * dv]))
            dech.append(dec[:, ks])
            keeph.append(keep)
            sth.append(st_ref[h])
        yield

    sc = [_bf(jnp.where(kp, _dot_nt(q, k), 0.0)) for kp, q, k in zip(keeph, qh, kh)]
    yield
    inter = [_dot_nt(q, _bf(st)) for q, st in zip(qh, sth)]
    yield
    kvt = [_dot_tn(v, ke) for v, ke in zip(vh, keh)]
    yield
    out = [_dot(s_, v) + it for s_, v, it in zip(sc, vh, inter)]
    yield
    for d, (o_ref, st_ref) in enumerate(((of_ref, sf_ref), (ob_ref, sb_ref))):
        for h in range(B_HEADS):
            i = d * B_HEADS + h
            o_ref[:, h * dv:(h + 1) * dv] = out[i]
            st_ref[h] = sth[i] * dech[i] + kvt[i]


def _mix_scan_kernel(rvkf, lwf, df, rvkb, lwb, db, qkvf, gdf, qkvb, gdb, gw2_ref, gb_ref,
                     yf_ref, yb_ref, of_ref, ob_ref, hf_ref, hb_ref, sf_ref, sb_ref):
    @pl.when(pl.program_id(1) == 0)
    def _():
        for ref in (hf_ref, hb_ref, sf_ref, sb_ref):
            ref[...] = jnp.zeros_like(ref)

    gs = hf_ref.shape[0]
    aw = lwf.shape[1]
    probs = []
    for gi in range(gs):
        sl = slice(gi * GROUP_W, (gi + 1) * GROUP_W)
        f32 = lambda ref, j: ref[:, j * aw + gi * GROUP_W:j * aw + (gi + 1) * GROUP_W].astype(F32)
        for rvk, lw, dd, h_ref, rev in ((rvkf, lwf, df, hf_ref, False), (rvkb, lwb, db, hb_ref, True)):
            probs.append((f32(rvk, 0), f32(rvk, 1), f32(rvk, 2), lw[:, sl], f32(dd, 0), f32(dd, 1),
                          h_ref[gi], rev))
    gla = _gla_stages(qkvf, gdf, qkvb, gdb, gw2_ref, gb_ref, of_ref, ob_ref, sf_ref, sb_ref)
    res = _rwkv_chunks(probs, side=gla)
    for gi in range(gs):
        sl = slice(gi * GROUP_W, (gi + 1) * GROUP_W)
        yf_ref[:, sl], hf_ref[gi] = res[2 * gi]
        yb_ref[:, sl], hb_ref[gi] = res[2 * gi + 1]


def _mix_scan(rvk, d0, d1, lw0, lw1, pb, gw2p, gb, *, batch, n_lat, seq, ctx_len, wk, wv, col0):
    rows, aw = lw0.shape
    assert aw == GROUP_W * RWKV_GROUPS_PER_STEP
    dk, dv = wk // B_HEADS, wv // B_HEADS
    qkv_w = 2 * wk + wv
    assert col0 % qkv_w == 0
    geo = dict(n_lat=n_lat, seq=seq, ctx_len=ctx_len)
    gd_blk = (col0 + 2 * wk + 2 * wv) // LANE

    def specs(reverse):
        rb = lambda b, i: _chunk_row_block(b, i, reverse=reverse, **geo)
        row_blk = lambda w, cb=0: pl.BlockSpec((CHUNK, w), lambda b, i: (rb(b, i), cb))
        rw = [row_blk(3 * aw), row_blk(aw), row_blk(2 * aw)]
        gla = [row_blk(qkv_w, col0 // qkv_w), row_blk(LANE, gd_blk)]
        return rw, gla, row_blk(aw), row_blk(wv)

    rwf, glaf, yf, of = specs(False)
    rwb, glab, yb, ob = specs(True)
    sds_a = jax.ShapeDtypeStruct((rows, aw), F32)
    sds_b = jax.ShapeDtypeStruct((rows, wv), F32)
    gs = RWKV_GROUPS_PER_STEP
    return pl.pallas_call(
        _mix_scan_kernel,
        grid=(batch, (seq + ctx_len) // CHUNK),
        in_specs=rwf + rwb + glaf + glab + [
            pl.BlockSpec((2, LANE, wk), lambda b, i: (0, 0, 0)),
            pl.BlockSpec((2, wk), lambda b, i: (0, 0)),
        ],
        out_specs=[yf, yb, of, ob],
        out_shape=[sds_a, sds_a, sds_b, sds_b],
        scratch_shapes=[pltpu.VMEM((gs, GROUP_W, GROUP_W), F32),
                        pltpu.VMEM((gs, GROUP_W, GROUP_W), F32),
                        pltpu.VMEM((B_HEADS, dv, dk), F32), pltpu.VMEM((B_HEADS, dv, dk), F32)],
        compiler_params=_cparams(("parallel", "arbitrary")),
    )(rvk, lw0, d0, rvk, lw1, d1, pb, pb, pb, pb, gw2p, gb)


def _mixer_out_kernel(yf, yb, bonus, g, of, ob, gate, xa_ref, xb_ref, mod_ref, lnw, lnb, gn, e_ref,
                      w_ref, o_ref, h_ref, *, aw, dv, n_a_tiles):
    e_bf = e_ref[...]
    y = yf[...] + yb[...]
    inv = 1.0 / A_HEAD_DIM
    mean = _head_sum(y, e_bf) * inv
    dlt = y - mean
    var = _head_sum(dlt * dlt, e_bf) * inv
    yn = dlt * lax.rsqrt(var + A_LN_EPS) * lnw[...] + lnb[...]
    h_ref[:, :aw] = _bf((yn + bonus[...]) * g[...])

    o = of[...] + ob[...]
    gt = _silu(gate[...])
    for h in range(B_HEADS):
        sl = slice(h * dv, (h + 1) * dv)
        oh = o[:, sl]
        ms = jnp.mean(oh * oh, axis=-1, keepdims=True)
        h_ref[:, aw + h * dv:aw + (h + 1) * dv] = _bf(oh * lax.rsqrt(ms + NORM_EPS) * gn[...] * gt[:, sl])

    x = jnp.where(pl.program_id(0) < n_a_tiles, xa_ref[...], xb_ref[...])
    o_ref[...] = x + mod_ref[2:3, :] * _dot(h_ref[...], w_ref[...])


def _mixer_out(yf, yb, bonus, g, of, ob, pb, xa, xb, mods_l, ln_w, ln_b, gla_norm, e_bf, w_bf,
               seg_of_tile, *, wk, col0):
    d = xa.shape[1]
    rows = xa.shape[0] + xb.shape[0]
    aw = yf.shape[1]
    wv = of.shape[1]
    dv = wv // B_HEADS
    tm = 256
    tiles_per_row_tile = ROW_TILE // tm
    n_a_tiles = xa.shape[0] // tm
    rs = lambda w: pl.BlockSpec((tm, w), lambda i: (i, 0))
    full = lambda shape: pl.BlockSpec(shape, lambda i: (0,) * len(shape))
    return pl.pallas_call(
        functools.partial(_mixer_out_kernel, aw=aw, dv=dv, n_a_tiles=n_a_tiles),
        grid=(rows // tm,),
        in_specs=[
            rs(aw), rs(aw), rs(aw), rs(aw), rs(wv), rs(wv),
            pl.BlockSpec((tm, wv), lambda i: (i, (col0 + 2 * wk + wv) // wv)),
        ] + _two_source_specs(tm, d, n_a_tiles) + [
            pl.BlockSpec((None, N_MOD, d), lambda i: (seg_of_tile(i // tiles_per_row_tile), 0, 0)),
            full((1, aw)), full((1, aw)), full((1, dv)), full((GROUP_W, GROUP_W)),
            full((aw + wv, d)),
        ],
        out_specs=rs(d),
        out_shape=jax.ShapeDtypeStruct((rows, d), F32),
        scratch_shapes=[pltpu.VMEM((tm, aw + wv), BF16)],
        compiler_params=_cparams(("parallel",)),
    )(yf, yb, bonus, g, of, ob, pb, xa, xb, mods_l, ln_w.reshape(1, aw), ln_b.reshape(1, aw),
      gla_norm.reshape(1, dv), e_bf, w_bf)


def _lru_kernel(*refs, reverse, n_blk, combine, cols):
    if combine:
        (x_ref, xp_ref, xn_ref, h0_ref, cw_ref, cb_ref, wc_ref, ba_ref, bx_ref, lam_ref,
         hf_ref, gate_ref, out_ref, hl_ref, a_sc, u_sc, hs_sc, carry_sc) = refs
    else:
        (x_ref, xp_ref, xn_ref, h0_ref, cw_ref, cb_ref, wc_ref, ba_ref, bx_ref, lam_ref,
         out_ref, hl_ref, a_sc, u_sc, hs_sc, carry_sc) = refs
    j = pl.program_id(2)
    blk = (n_blk - 1 - j) if reverse else j
    lb = x_ref.shape[0]
    ct = cw_ref.shape[1]
    cblk = wc_ref.shape[1]
    rowid = lax.broadcasted_iota(jnp.int32, (lb, 1), 0)
    decay_rate = -C_CONST * _softplus(-lam_ref[...])

    @pl.when(j == 0)
    def _():
        carry_sc[...] = h0_ref[0:1, :]

    for q in (range(cols - 1, -1, -1) if reverse else range(cols)):
        lanes = slice(q * ct, (q + 1) * ct)
        x = x_ref[:, lanes]
        if q > 0:
            prev = slice((q - 1) * ct, q * ct)
            p6, p7 = x_ref[lb - 2:lb - 1, prev], x_ref[lb - 1:lb, prev]
        else:
            p6 = jnp.where(blk == 0, 0.0, xp_ref[SUBLANE - 2:SUBLANE - 1, :])
            p7 = jnp.where(blk == 0, 0.0, xp_ref[SUBLANE - 1:SUBLANE, :])
        if q < cols - 1:
            n0 = x_ref[0:1, (q + 1) * ct:(q + 2) * ct]
        else:
            n0 = jnp.where(blk == n_blk - 1, 0.0, xn_ref[0:1, :])
        xm1 = jnp.where(rowid == 0, p7, pltpu.roll(x, 1, 0))
        xm2 = jnp.where(rowid == 0, p6, jnp.where(rowid == 1, p7, pltpu.roll(x, 2, 0)))
        xp1 = jnp.where(rowid == lb - 1, n0, pltpu.roll(x, lb - 1, 0))
        xs = (cb_ref[...] + cw_ref[0:1, :] * xm2 + cw_ref[1:2, :] * xm1 + cw_ref[2:3, :] * x
              + cw_ref[3:4, :] * xp1)

        for n in range(ct // cblk):
            sl = slice(n * cblk, (n + 1) * cblk)
            xb = xs[:, sl]
            ri = _dot(_bf(xb), wc_ref[n])
            rg = _sigmoid(ri[:, :cblk] + ba_ref[:, sl])
            ig = _sigmoid(ri[:, cblk:] + bx_ref[:, sl])
            a = jnp.exp(decay_rate[:, sl] * rg)
            a_sc[q, :, sl] = a
            u_sc[q, :, sl] = jnp.sqrt(1.0 - a * a) * (ig * xb)

        def step(t, h, q=q):
            tt = (lb - 1 - t) if reverse else t
            h = a_sc[q, pl.ds(tt, 1), :] * h + u_sc[q, pl.ds(tt, 1), :]
            hs_sc[q, pl.ds(tt, 1), :] = h
            return h

        carry_sc[...] = lax.fori_loop(0, lb, step, carry_sc[...], unroll=8)
        if combine:
            out_ref[:, lanes] = ((hs_sc[q] + hf_ref[q * lb:(q + 1) * lb, :])
                                 * jax.nn.gelu(gate_ref[:, lanes]))
        else:
            out_ref[q * lb:(q + 1) * lb, :] = hs_sc[q]
    hl_ref[...] = jnp.broadcast_to(carry_sc[...], hl_ref.shape)


def _lru_pass(x_arr, x_specs, h0, conv_w, conv_b, wcat, ba, bx, lam, out_shape, out_spec, extra,
              *, batch, n_blk, lb, cols, reverse):
    c = conv_w.shape[1]
    ct = LRU_CHANNEL_TILE
    nct = c // ct
    cblk = wcat.shape[1]
    chan = lambda rows_: pl.BlockSpec((rows_, ct), lambda b, k, j: (0, k))
    in_specs = list(x_specs) + [
        pl.BlockSpec((SUBLANE, ct), lambda b, k, j: (b, k)),
        chan(C_CONV), chan(1),
        pl.BlockSpec((ct // cblk, cblk, 2 * cblk), lambda b, k, j: (k, 0, 0)),
        chan(1), chan(1), chan(1),
    ] + [s for _, s in extra]
    args = [x_arr, x_arr, x_arr, h0, conv_w, conv_b.reshape(1, c), wcat, ba.reshape(1, c),
            bx.reshape(1, c), lam.reshape(1, c)] + [a for a, _ in extra]
    return pl.pallas_call(
        functools.partial(_lru_kernel, reverse=reverse, n_blk=n_blk, combine=bool(extra),
                          cols=cols),
        grid=(batch, nct, n_blk),
        in_specs=in_specs,
        out_specs=[out_spec, pl.BlockSpec((SUBLANE, ct), lambda b, k, j: (b, k))],
        out_shape=[out_shape, jax.ShapeDtypeStruct((batch * SUBLANE, c), F32)],
        scratch_shapes=[pltpu.VMEM((cols, lb, ct), F32), pltpu.VMEM((cols, lb, ct), F32),
                        pltpu.VMEM((cols, lb, ct), F32), pltpu.VMEM((1, ct), F32)],
        compiler_params=_cparams(("parallel", "parallel", "arbitrary")),
    )(*args)


def _rglru(hw_ctx, gate_grid, x_grid, conv_w, conv_b, wa, ba, wx, bx, lam, *, batch, n_lat, seq,
           ctx_len):
    c = conv_w.shape[1]
    ct = LRU_CHANNEL_TILE
    rows = seq // GRID_W
    lbc = 128
    ncb = ctx_len // lbc
    xcol = c // ct

    wcat = [_bf(jnp.concatenate([wa[d], wx[d]], axis=-1)) for d in range(2)]
    zeros_state = jnp.zeros((batch * SUBLANE, c), F32)

    def ctx_specs(reverse):
        blk = (lambda j: ncb - 1 - j) if reverse else (lambda j: j)
        per8 = lbc // SUBLANE
        cur = pl.BlockSpec((lbc, ct), lambda b, k, j: (b * ncb + blk(j), xcol + k))
        prv = pl.BlockSpec((SUBLANE, ct), lambda b, k, j: (
            b * ncb * per8 + jnp.maximum(blk(j) * per8 - 1, 0), xcol + k))
        nxt = pl.BlockSpec((SUBLANE, ct), lambda b, k, j: (
            b * ncb * per8 + jnp.minimum((blk(j) + 1) * per8, ncb * per8 - 1), xcol + k))
        return cur, prv, nxt

    assert ct == c
    def lat_specs(reverse, kc):
        ncg = GRID_W // kc
        grp = (lambda j: ncg - 1 - j) if reverse else (lambda j: j)
        r8 = rows // SUBLANE
        cur = pl.BlockSpec((rows, kc * ct), lambda b, k, j: (b, grp(j)))
        prv = pl.BlockSpec((SUBLANE, ct), lambda b, k, j: (
            b * r8 + r8 - 1, jnp.maximum(grp(j) * kc - 1, 0)))
        nxt = pl.BlockSpec((SUBLANE, ct), lambda b, k, j: (
            b * r8, jnp.minimum(grp(j) * kc + kc, GRID_W - 1)))
        hf = pl.BlockSpec((kc * rows, ct), lambda b, k, j: (b * ncg + grp(j), 0))
        return (cur, prv, nxt), hf, cur

    common = dict(batch=batch)
    ctx_scratch_sds = jax.ShapeDtypeStruct((batch * ctx_len, c), F32)
    ctx_out = lambda reverse: pl.BlockSpec(
        (lbc, ct), lambda b, k, j: (b * ncb + ((ncb - 1 - j) if reverse else j), k))
    _, st = _lru_pass(hw_ctx, ctx_specs(False), zeros_state, conv_w, conv_b, wcat[0], ba[0], bx[0],
                      lam[0], ctx_scratch_sds, ctx_out(False), [], n_blk=ncb, lb=lbc, cols=1,
                      reverse=False, **common)
    hf_sds = jax.ShapeDtypeStruct((n_lat, c), F32)
    kc = LRU_COLS_FWD
    x_specs, hf_spec, _ = lat_specs(False, kc)
    hf, _ = _lru_pass(x_grid, x_specs, st, conv_w, conv_b, wcat[0], ba[0], bx[0], lam[0], hf_sds,
                      hf_spec, [], n_blk=GRID_W // kc, lb=rows, cols=kc, reverse=False, **common)
    _, st = _lru_pass(hw_ctx, ctx_specs(True), zeros_state, conv_w, conv_b, wcat[1], ba[1], bx[1],
                      lam[1], ctx_scratch_sds, ctx_out(True), [], n_blk=ncb, lb=lbc, cols=1,
                      reverse=True, **common)
    kc = LRU_COLS_BWD
    x_specs, hf_spec, grid_spec = lat_specs(True, kc)
    y_sds = jax.ShapeDtypeStruct((n_lat // GRID_W, GRID_W * c), F32)
    y, _ = _lru_pass(x_grid, x_specs, st, conv_w, conv_b, wcat[1], ba[1], bx[1], lam[1],
                     y_sds, grid_spec, [(hf, hf_spec), (gate_grid, grid_spec)],
                     n_blk=GRID_W // kc, lb=rows, cols=kc, reverse=True, **common)
    return y


def _pad_cols(w, n):
    return jnp.pad(w, ((0, 0),) * (w.ndim - 1) + ((0, n - w.shape[-1]),))


def kernel(x, c, ctx, c_ctx, mod_w, mod_b, norm1, norm2, mlp_w1, mlp_w2, ab_w_in, ab_w_out, rw_mu,
           rw_w0, rw_w2, rw_a0, rw_a2, rw_g2, rw_kk, rw_ka, rw_rk, rw_ln_w, rw_ln_b, gla_gw2,
           gla_gb, gla_norm, lru_w_in, lru_w_out, lru_conv_w, lru_conv_b, lru_wa, lru_ba, lru_wx,
           lru_bx, lru_lam, final_norm):
    batch, seq, d = x.shape
    ctx_len = ctx.shape[1]
    depth = mod_w.shape[0]
    n_lat = batch * seq
    aw = rw_w0.shape[-1]
    wk = gla_gb.shape[-1]
    wv = ab_w_out.shape[1] - aw
    a_cols = rw_mu.shape[-1]
    b_cols = ab_w_in.shape[-1] - a_cols
    assert depth == 2 and batch + 1 <= SUBLANE
    assert seq % ROW_TILE == 0 and (batch * ctx_len) % ROW_TILE == 0
    assert ctx_len % PREP_TILE == 0 and ctx_len % 128 == 0 and seq % (GRID_W * SUBLANE) == 0
    assert aw % GROUP_W == 0 and wv == aw and 2 * wk == wv

    tiles_per_seq = seq // ROW_TILE
    seg_of_tile = lambda i: jnp.minimum(i // tiles_per_seq, batch)

    x2 = x.reshape(n_lat, d)
    ctx2 = ctx.reshape(batch * ctx_len, d)
    cvec = jnp.concatenate(
        [c, c_ctx[None, :], jnp.zeros((SUBLANE - batch - 1, d), F32)], axis=0)
    mods = _mods(cvec, mod_w, mod_b)

    hid = jnp.arange(GROUP_W) // A_HEAD_DIM
    e_bf = _bf(hid[:, None] == hid[None, :])

    a_pad = -(-a_cols // LANE) * LANE
    col0 = -(-a_cols // (2 * wk)) * (2 * wk)
    n_tiles = 3
    tn = -(-(col0 + b_cols) // (n_tiles * LANE)) * LANE
    w_ab = _bf(jnp.concatenate(
        [_pad_cols(ab_w_in[0][:, :a_cols], col0),
         _pad_cols(ab_w_in[0][:, a_cols:], n_tiles * tn - col0)], axis=1))
    pa = pb = _norm_matmul(x2, ctx2, norm1[0], mods[0], w_ab, seg_of_tile, tn=tn)

    lora_w = 2 * A_LORA_W + 2 * A_LORA_A
    w2p = jnp.zeros((2, lora_w, aw), F32)
    a2p = jnp.zeros((2, lora_w, aw), F32)
    for dd in range(2):
        w2p = w2p.at[dd, dd * A_LORA_W:(dd + 1) * A_LORA_W].set(rw_w2[0, dd])
        a2p = a2p.at[dd, 2 * A_LORA_W + dd * A_LORA_A:2 * A_LORA_W + (dd + 1) * A_LORA_A].set(
            rw_a2[0, dd])
    (rvk, d0, d1, lw0, lw1, g_, bonus) = _rwkv_prep(
        pa, _pad_cols(rw_mu[0], a_pad), rw_w0[0], rw_a0[0], _bf(w2p), _bf(a2p), _bf(rw_g2[0]),
        rw_kk[0], rw_ka[0], rw_rk[0].reshape(-1), e_bf, n_lat=n_lat, seq=seq, ctx_len=ctx_len)
    gw2p = jnp.zeros((2, LANE, wk), F32)
    for dd in range(2):
        gw2p = gw2p.at[dd, dd * B_LORA:(dd + 1) * B_LORA].set(gla_gw2[0, dd])
    yf, yb, of, ob = _mix_scan(rvk, d0, d1, lw0, lw1, pb, _bf(gw2p), gla_gb[0],
                               batch=batch, n_lat=n_lat, seq=seq, ctx_len=ctx_len, wk=wk, wv=wv,
                               col0=col0)

    xs = _mixer_out(yf, yb, bonus, g_, of, ob, pb, x2, ctx2, mods[0], rw_ln_w[0], rw_ln_b[0],
                    gla_norm[0], e_bf, _layer_bf16(ab_w_out, 0), seg_of_tile, wk=wk, col0=col0)
    seg_of_mlp_tile = lambda i: jnp.minimum(i // (seq // MLP_TILE), batch)
    ctx_tiles = (batch * ctx_len) // ROW_TILE
    w1_bf, w2_bf = _layer_bf16(mlp_w1, 0), _layer_bf16(mlp_w2, 0)
    x_lat = _mlp(xs, norm2[0], mods[0], w1_bf, w2_bf, final_norm, seg_of_mlp_tile, tm=MLP_TILE,
                 tile0=0, n_tiles=n_lat // MLP_TILE, final_norm=False)
    x_ctx = _mlp(xs, norm2[0], mods[0], w1_bf, w2_bf, final_norm, seg_of_tile, tm=ROW_TILE,
                 tile0=n_lat // ROW_TILE, n_tiles=ctx_tiles, final_norm=False)

    w_lru = _layer_bf16(lru_w_in, 0)
    hw_ctx = _norm_matmul(x_ctx, None, norm1[1], mods[1], w_lru, lambda i: batch, tn=2048)
    gate_grid, x_grid = _norm_matmul_grid(x_lat, n_lat, seq // GRID_W, norm1[1], mods[1], w_lru)
    y = _rglru(hw_ctx, gate_grid, x_grid, lru_conv_w[0], lru_conv_b[0], lru_wa[0], lru_ba[0],
               lru_wx[0], lru_bx[0], lru_lam[0], batch=batch, n_lat=n_lat, seq=seq,
               ctx_len=ctx_len)
    xl = _proj_res(y, x_lat, n_lat, seq // GRID_W, mods[1], _layer_bf16(lru_w_out, 0))
    out = _mlp(xl, norm2[1], mods[1], _layer_bf16(mlp_w1, 1), _layer_bf16(mlp_w2, 1), final_norm,
               seg_of_mlp_tile,
               tm=MLP_TILE, tile0=0, n_tiles=n_lat // MLP_TILE, final_norm=True)
    return out.reshape(batch, seq, d)
```

```python
import functools

import jax
import jax.numpy as jnp
from jax import lax
from jax.experimental import pallas as pl
from jax.experimental.pallas import tpu as pltpu

F32 = jnp.float32
BF16 = jnp.bfloat16

NORM_EPS = 1e-6
GRID_W = 64
N_MOD = 6

A_HEAD_DIM = 64
A_LORA_W = 96
A_LORA_A = 96
A_LORA_G = 256
A_LN_EPS = 64e-5
HEADS_PER_GROUP = 4
GROUP_W = HEADS_PER_GROUP * A_HEAD_DIM
CHUNK = 64
RWKV_GROUPS_PER_STEP = 4

B_HEADS = 4
B_LORA = 16
B_GATE_TAU = 16.0

C_BLOCKS = 8
C_CONV = 4
C_CONST = 8.0

LANE = 128
SUBLANE = 8
ROW_TILE = 512
MLP_TILE = 1024
MLP_STEP_ELEMS = 1024 * 512
LRU_CHANNEL_TILE = 2048
LRU_COLS_FWD = 4
LRU_COLS_BWD = 1
CAST_BLOCK_ELEMS = 2 * 1024 * 1024
PREP_TILE = 256
VMEM_LIMIT = 56 * 1024 * 1024


def _cparams(sem):
    return pltpu.CompilerParams(dimension_semantics=sem, vmem_limit_bytes=VMEM_LIMIT)


def _bf(x):
    return x.astype(BF16)


def _dot(a, b):
    return jnp.dot(a, b, preferred_element_type=F32)


def _dot_nt(a, b):
    return lax.dot_general(a, b, (((1,), (1,)), ((), ())), preferred_element_type=F32)


def _dot_tn(a, b):
    return lax.dot_general(a, b, (((0,), (0,)), ((), ())), preferred_element_type=F32)


def _split3(x):
    hi = _bf(x)
    r1 = x - hi.astype(F32)
    mid = _bf(r1)
    lo = _bf(r1 - mid.astype(F32))
    return hi, mid, lo


def _dot_exact_lhs(m_bf, x):
    hi, mid, lo = _split3(x)
    return _dot(m_bf, hi) + _dot(m_bf, mid) + _dot(m_bf, lo)


def _dot_exact_rhs(x, m_bf):
    hi, mid, lo = _split3(x)
    return _dot(hi, m_bf) + _dot(mid, m_bf) + _dot(lo, m_bf)


def _softplus(x):
    return jnp.maximum(x, 0.0) + jnp.log(1.0 + jnp.exp(-jnp.abs(x)))


def _sigmoid(x):
    return 0.5 * jnp.tanh(0.5 * x) + 0.5


def _silu(x):
    return x * _sigmoid(x)


def _norm_mod(x, g, shift, scale):
    ms = jnp.mean(x * x, axis=-1, keepdims=True)
    return (x * lax.rsqrt(ms + NORM_EPS) * g) * (1.0 + scale) + shift


def _cast_kernel(w_ref, o_ref):
    o_ref[...] = _bf(w_ref[...])


def _layer_bf16(w, layer):
    _, r, c = w.shape
    tr = min(r, CAST_BLOCK_ELEMS // c)
    return pl.pallas_call(
        _cast_kernel,
        grid=(r // tr,),
        in_specs=[pl.BlockSpec((None, tr, c), lambda i: (layer, i, 0))],
        out_specs=pl.BlockSpec((tr, c), lambda i: (i, 0)),
        out_shape=jax.ShapeDtypeStruct((r, c), BF16),
        compiler_params=_cparams(("parallel",)),
    )(w)


def _mods_kernel(c_ref, w_ref, b_ref, o_ref):
    s = _silu(c_ref[...])
    o_ref[...] = _dot(_bf(s), _bf(w_ref[...])) + b_ref[...]


def _mods(cvec, mod_w, mod_b):
    depth, d, n = mod_w.shape
    tn = 1024
    out = pl.pallas_call(
        _mods_kernel,
        grid=(depth, n // tn),
        in_specs=[
            pl.BlockSpec((SUBLANE, d), lambda l, j: (0, 0)),
            pl.BlockSpec((None, d, tn), lambda l, j: (l, 0, j)),
            pl.BlockSpec((None, 1, tn), lambda l, j: (l, 0, j)),
        ],
        out_specs=pl.BlockSpec((None, SUBLANE, tn), lambda l, j: (l, 0, j)),
        out_shape=jax.ShapeDtypeStruct((depth, SUBLANE, n), F32),
        compiler_params=_cparams(("parallel", "parallel")),
    )(cvec, mod_w, mod_b.reshape(depth, 1, n))
    return out.reshape(depth, SUBLANE, N_MOD, d)


def _norm_matmul_kernel(xa_ref, xb_ref, g_ref, mod_ref, w_ref, o_ref, h_ref, *, n_a_tiles):
    i = pl.program_id(0)

    slab = 128
    first = pl.program_id(1) == 0

    @pl.when(first)
    def _():
        for s in range(h_ref.shape[0] // slab):
            rs = slice(s * slab, (s + 1) * slab)
            x = jnp.where(i < n_a_tiles, xa_ref[rs, :], xb_ref[rs, :])
            h = _bf(_norm_mod(x, g_ref[...], mod_ref[0:1, :], mod_ref[1:2, :]))
            h_ref[rs, :] = h
            o_ref[rs, :] = _dot(h, w_ref[...])

    @pl.when(jnp.logical_not(first))
    def _():
        o_ref[...] = _dot(h_ref[...], w_ref[...])


def _two_source_specs(tm, d, n_a_tiles):
    return [pl.BlockSpec((tm, d), lambda i, *_: (jnp.minimum(i, n_a_tiles - 1), 0)),
            pl.BlockSpec((tm, d), lambda i, *_: (jnp.maximum(i - n_a_tiles, 0), 0))]


def _norm_matmul(xa, xb, g, mods_l, w_bf, seg_of_tile, *, tn):
    d = xa.shape[1]
    n = w_bf.shape[1]
    tm = ROW_TILE
    n_a_tiles = xa.shape[0] // tm
    n_tiles = n_a_tiles + (0 if xb is None else xb.shape[0] // tm)
    return pl.pallas_call(
        functools.partial(_norm_matmul_kernel, n_a_tiles=n_a_tiles),
        grid=(n_tiles, n // tn),
        in_specs=_two_source_specs(tm, d, n_a_tiles) + [
            pl.BlockSpec((1, d), lambda i, j: (0, 0)),
            pl.BlockSpec((None, N_MOD, d), lambda i, j: (seg_of_tile(i), 0, 0)),
            pl.BlockSpec((d, tn), lambda i, j: (0, j)),
        ],
        out_specs=pl.BlockSpec((tm, tn), lambda i, j: (i, j)),
        out_shape=jax.ShapeDtypeStruct((n_tiles * tm, n), F32),
        scratch_shapes=[pltpu.VMEM((tm, d), BF16)],
        compiler_params=_cparams(("parallel", "arbitrary")),
    )(xa, xa if xb is None else xb, g.reshape(1, d), mods_l, w_bf)


def _grid_tile(rows):
    gr = min(rows, 64)
    gc = ROW_TILE // gr
    assert gc % SUBLANE == 0 and GRID_W % gc == 0 and rows % gr == 0
    return gr, gc


def _norm_matmul_grid_kernel(x_ref, g_ref, mod_ref, w_ref, o0_ref, o1_ref, h_ref, res_ref):
    j = pl.program_id(1)
    gr, gc, d = x_ref.shape

    @pl.when(j == 0)
    def _():
        h = _norm_mod(x_ref[...].reshape(gr * gc, d), g_ref[...], mod_ref[0:1, :], mod_ref[1:2, :])
        h_ref[...] = _bf(h)

    res = _dot(h_ref[...], w_ref[...])
    nh = w_ref.shape[1]
    nq = nh // LANE
    for q in range(nq):
        res_ref[q] = res[:, q * LANE:(q + 1) * LANE]

    def scatter(o_ref):
        for c in range(gc):
            for q in range(nq):
                o_ref[:, c * nh + q * LANE:c * nh + (q + 1) * LANE] = (
                    res_ref[q, pl.ds(c, gr, stride=gc), :])

    @pl.when(j == 0)
    def _():
        scatter(o0_ref)

    @pl.when(j == 1)
    def _():
        scatter(o1_ref)


def _norm_matmul_grid(x, n_lat, rows, g, mods_l, w_bf):
    d = x.shape[1]
    nh = w_bf.shape[1] // 2
    gr, gc = _grid_tile(rows)
    ncg = GRID_W // gc
    x3 = x.reshape(x.shape[0] // GRID_W, GRID_W, d)
    out_spec = pl.BlockSpec((gr, gc * nh), lambda i, j: (i // ncg, i % ncg))
    sds = jax.ShapeDtypeStruct((n_lat // GRID_W, GRID_W * nh), F32)
    return pl.pallas_call(
        _norm_matmul_grid_kernel,
        grid=(n_lat // ROW_TILE, 2),
        in_specs=[
            pl.BlockSpec((gr, gc, d), lambda i, j: (i // ncg, i % ncg, 0)),
            pl.BlockSpec((1, d), lambda i, j: (0, 0)),
            pl.BlockSpec((None, N_MOD, d), lambda i, j: ((i // ncg) * gr // rows, 0, 0)),
            pl.BlockSpec((d, nh), lambda i, j: (0, j)),
        ],
        out_specs=[out_spec, out_spec],
        out_shape=[sds, sds],
        scratch_shapes=[pltpu.VMEM((ROW_TILE, d), BF16),
                        pltpu.VMEM((nh // LANE, ROW_TILE, LANE), F32)],
        compiler_params=_cparams(("parallel", "arbitrary")),
    )(x3, g.reshape(1, d), mods_l, w_bf)


def _mlp_kernel(x_ref, g_ref, mod_ref, w1_ref, w2_ref, fn_ref, o_ref, h_ref, *, final_norm):
    j = pl.program_id(1)
    slab = 256
    n_slab = x_ref.shape[0] // slab

    @pl.when(j == 0)
    def _():
        for s in range(n_slab):
            rs = slice(s * slab, (s + 1) * slab)
            h = _bf(_norm_mod(x_ref[rs, :], g_ref[...], mod_ref[3:4, :], mod_ref[4:5, :]))
            h_ref[rs, :] = h
            a = _bf(jnp.square(jnp.maximum(_dot(h, w1_ref[...]), 0.0)))
            o_ref[rs, :] = _dot(a, w2_ref[...])

    @pl.when(j > 0)
    def _():
        a = _dot(h_ref[...], w1_ref[...])
        a = _bf(jnp.square(jnp.maximum(a, 0.0)))
        nsplit = 2
        wn = o_ref.shape[1] // nsplit
        for s in range(nsplit):
            o_ref[:, s * wn:(s + 1) * wn] += _dot(a, w2_ref[:, s * wn:(s + 1) * wn])

    @pl.when(j == pl.num_programs(1) - 1)
    def _():
        for s in range(n_slab):
            rs = slice(s * slab, (s + 1) * slab)
            xo = x_ref[rs, :] + mod_ref[5:6, :] * o_ref[rs, :]
            if final_norm:
                ms = jnp.mean(xo * xo, axis=-1, keepdims=True)
                xo = xo * lax.rsqrt(ms + NORM_EPS) * fn_ref[...]
            o_ref[rs, :] = xo


def _mlp(x, g, mods_l, w1_bf, w2_bf, fn, seg_of_tile, *, tm, tile0, n_tiles, final_norm):
    d = x.shape[1]
    dff = w1_bf.shape[1]
    tf = MLP_STEP_ELEMS // tm
    return pl.pallas_call(
        functools.partial(_mlp_kernel, final_norm=final_norm),
        grid=(n_tiles, dff // tf),
        in_specs=[
            pl.BlockSpec((tm, d), lambda i, j: (i + tile0, 0), pipeline_mode=pl.Buffered(1)),
            pl.BlockSpec((1, d), lambda i, j: (0, 0)),
            pl.BlockSpec((None, N_MOD, d), lambda i, j: (seg_of_tile(i + tile0), 0, 0)),
            pl.BlockSpec((d, tf), lambda i, j: (0, j)),
            pl.BlockSpec((tf, d), lambda i, j: (j, 0)),
            pl.BlockSpec((1, d), lambda i, j: (0, 0)),
        ],
        out_specs=pl.BlockSpec((tm, d), lambda i, j: (i, 0)),
        out_shape=jax.ShapeDtypeStruct((n_tiles * tm, d), F32),
        scratch_shapes=[pltpu.VMEM((tm, d), BF16)],
        compiler_params=_cparams(("parallel", "arbitrary")),
    )(x, g.reshape(1, d), mods_l, w1_bf, w2_bf, fn.reshape(1, d))


def _proj_res_kernel(y_ref, x_ref, mod_ref, w_ref, o_ref, yt_ref):
    k = w_ref.shape[0]
    gr, gc, d = x_ref.shape
    nq = k // LANE
    for c in range(gc):
        for q in range(nq):
            yt_ref[q, pl.ds(c, gr, stride=gc), :] = (
                y_ref[:, c * k + q * LANE:c * k + (q + 1) * LANE])
    yt = jnp.concatenate([_bf(yt_ref[q]) for q in range(nq)], axis=1)
    out = x_ref[...].reshape(gr * gc, d) + mod_ref[2:3, :] * _dot(yt, w_ref[...])
    o_ref[...] = out.reshape(gr, gc, d)


def _proj_res(y_grid, x, n_lat, rows, mods_l, w_bf):
    k, d = w_bf.shape
    gr, gc = _grid_tile(rows)
    ncg = GRID_W // gc
    x3 = x.reshape(x.shape[0] // GRID_W, GRID_W, d)
    tok_spec = pl.BlockSpec((gr, gc, d), lambda i: (i // ncg, i % ncg, 0))
    out = pl.pallas_call(
        _proj_res_kernel,
        grid=(n_lat // ROW_TILE,),
        in_specs=[
            pl.BlockSpec((gr, gc * k), lambda i: (i // ncg, i % ncg)),
            tok_spec,
            pl.BlockSpec((None, N_MOD, d), lambda i: ((i // ncg) * gr // rows, 0, 0)),
            pl.BlockSpec((k, d), lambda i: (0, 0)),
        ],
        out_specs=tok_spec,
        out_shape=jax.ShapeDtypeStruct((n_lat // GRID_W, GRID_W, d), F32),
        scratch_shapes=[pltpu.VMEM((k // LANE, ROW_TILE, LANE), F32)],
        compiler_params=_cparams(("parallel",)),
    )(y_grid, x3, mods_l, w_bf)
    return out.reshape(n_lat, d)


def _head_sum(z, e_bf):
    parts = []
    for gi in range(z.shape[1] // GROUP_W):
        zs = z[:, gi * GROUP_W:(gi + 1) * GROUP_W]
        parts.append(_dot_exact_rhs(zs, e_bf))
    return jnp.concatenate(parts, axis=1)


def _rwkv_prep_kernel(p_ref, pp_ref, pn_ref, mu_ref, w0_ref, a0_ref, w2_ref, a2_ref, g2_ref,
                      kk_ref, ka_ref, rk_ref, e_ref,
                      rvk_o, d0_o, d1_o, lw0_o, lw1_o, g_o, bonus_o,
                      *, n_lat_tiles, lat_tiles_per_seq, ctx_tiles_per_seq, aw):
    i = pl.program_id(0)
    tp = p_ref.shape[0]
    in_ctx = i >= n_lat_tiles
    pos = jnp.where(in_ctx, (i - n_lat_tiles) % ctx_tiles_per_seq, i % lat_tiles_per_seq)
    last = jnp.where(in_ctx, ctx_tiles_per_seq - 1, lat_tiles_per_seq - 1)
    is_first = pos == 0
    is_last = pos == last
    rowid = lax.broadcasted_iota(jnp.int32, (tp, 1), 0)

    def shifted(c0, c1):
        x = p_ref[:, c0:c1]
        prow = jnp.where(is_first, 0.0, pp_ref[SUBLANE - 1:SUBLANE, c0:c1])
        nrow = jnp.where(is_last, 0.0, pn_ref[0:1, c0:c1])
        xp = jnp.where(rowid == 0, prow, pltpu.roll(x, 1, 0))
        xn = jnp.where(rowid == tp - 1, nrow, pltpu.roll(x, tp - 1, 0))
        return x + mu_ref[0:1, c0:c1] * (xp - x) + mu_ref[1:2, c0:c1] * (xn - x)

    e_bf = e_ref[...]
    r = shifted(0, aw)
    k = shifted(aw, 2 * aw)
    v = shifted(2 * aw, 3 * aw)
    gd = shifted(3 * aw, 3 * aw + A_LORA_G)
    lo0 = 3 * aw + A_LORA_G
    xl = shifted(lo0, lo0 + 2 * A_LORA_W + 2 * A_LORA_A)

    rvk_o[:, :aw] = r.astype(rvk_o.dtype)
    rvk_o[:, aw:2 * aw] = v.astype(rvk_o.dtype)
    g_o[...] = _dot(_bf(_sigmoid(gd)), g2_ref[...])

    kx = k * kk_ref[...]
    kn = jnp.sqrt(_head_sum(kx * kx, e_bf))
    kk = kx / jnp.maximum(kn, 1e-12)
    rvk_o[:, 2 * aw:] = kk.astype(rvk_o.dtype)

    xl_t = _bf(jnp.tanh(xl))
    xl_b = _bf(xl)
    kd_sum = None
    for d, (lw_o, d_o) in enumerate(((lw0_o, d0_o), (lw1_o, d1_o))):
        w_log = -_softplus(-(w0_ref[d:d + 1, :] + _dot(xl_t, w2_ref[d]))) - 0.5
        lw_o[...] = -jnp.exp(w_log)
        asig = _sigmoid(a0_ref[d:d + 1, :] + _dot(xl_b, a2_ref[d]))
        kd = k * (1.0 + (asig - 1.0) * ka_ref[...])
        d_o[:, :aw] = kd.astype(d_o.dtype)
        d_o[:, aw:] = (kk * asig).astype(d_o.dtype)
        kd_sum = kd if kd_sum is None else kd_sum + kd
    bonus_o[...] = _head_sum(r * rk_ref[...] * kd_sum, e_bf) * v


def _rwkv_prep(pa, mu_p, w0, a0, w2p, a2p, g2, k_k, k_a, r_k, e_bf, *, n_lat, seq, ctx_len):
    r = pa.shape[0]
    ncol = mu_p.shape[1]
    aw = w0.shape[1]
    tp = PREP_TILE
    hb = tp // SUBLANE
    nblk8 = r // SUBLANE
    lw = 2 * A_LORA_W + 2 * A_LORA_A
    row = lambda a: a.reshape(1, aw)
    full = lambda shape: pl.BlockSpec(shape, lambda i: (0,) * len(shape))
    outs = [(3 * aw, BF16), (2 * aw, BF16), (2 * aw, BF16), (aw, F32), (aw, F32), (aw, F32),
            (aw, F32)]
    return pl.pallas_call(
        functools.partial(_rwkv_prep_kernel, n_lat_tiles=n_lat // tp, lat_tiles_per_seq=seq // tp,
                          ctx_tiles_per_seq=ctx_len // tp, aw=aw),
        grid=(r // tp,),
        in_specs=[
            pl.BlockSpec((tp, ncol), lambda i: (i, 0)),
            pl.BlockSpec((SUBLANE, ncol), lambda i: (jnp.maximum(i * hb - 1, 0), 0)),
            pl.BlockSpec((SUBLANE, ncol), lambda i: (jnp.minimum((i + 1) * hb, nblk8 - 1), 0)),
            full((2, ncol)), full((2, aw)), full((2, aw)),
            full((2, lw, aw)), full((2, lw, aw)), full((A_LORA_G, aw)),
            full((1, aw)), full((1, aw)), full((1, aw)), full((GROUP_W, GROUP_W)),
        ],
        out_specs=[pl.BlockSpec((tp, w), lambda i: (i, 0)) for w, _ in outs],
        out_shape=[jax.ShapeDtypeStruct((r, w), dt) for w, dt in outs],
        compiler_params=_cparams(("parallel",)),
    )(pa, pa, pa, mu_p, w0, a0, w2p, a2p, g2, row(k_k), row(k_a), row(r_k), e_bf)


def _rwkv_masks(reverse):
    L = CHUNK
    gw = GROUP_W
    row = lax.broadcasted_iota(jnp.int32, (gw, gw), 0)
    col = lax.broadcasted_iota(jnp.int32, (gw, gw), 1)
    same_head = (row >> 6) == (col >> 6)
    t_n = lax.broadcasted_iota(jnp.int32, (L, gw), 0)
    s_n = lax.broadcasted_iota(jnp.int32, (L, gw), 1) & (L - 1)
    strict_n = (t_n < s_n) if reverse else (t_n > s_n)
    incl_n = (t_n <= s_n) if reverse else (t_n >= s_n)
    t_r = lax.broadcasted_iota(jnp.int32, (L, L), 0)
    t_c = lax.broadcasted_iota(jnp.int32, (L, L), 1)
    tri = _bf(jnp.where((t_c >= t_r) if reverse else (t_c <= t_r), 1.0, 0.0))
    return dict(same_head=same_head, strict_n=strict_n, incl_n=incl_n, eye_n=t_n == s_n, tri=tri)


def _rwkv_chunks(probs, side=None):
    L = CHUNK
    gw = GROUP_W
    tick = (lambda: None) if side is None else (lambda: next(side, None))
    masks = {rv: _rwkv_masks(rv) for rv in sorted({p[7] for p in probs})}
    mk = [masks[p[7]] for p in probs]
    each = lambda f, *cols: [f(*args) for args in zip(*cols)]
    rep = lambda x: jnp.concatenate([x] * HEADS_PER_GROUP, axis=0)

    c = [_dot_exact_lhs(m["tri"], p[3]) for m, p in zip(mk, probs)]
    c_l = [ci[(0 if p[7] else L - 1):(1 if p[7] else L), :] for ci, p in zip(c, probs)]

    blocks = [lambda x, sh=m["same_head"]: _bf(jnp.where(sh, rep(x), 0.0)) for m in mk]
    an, rn, vn, v_bd, bk_end, btk_bd, e_l = [], [], [], [], [], [], []
    for (r, v, kk, lw, kd, b, _, _), bd, ci, cl in zip(probs, blocks, c, c_l):
        e_nc = jnp.exp(-ci)
        e_end = jnp.exp(cl - ci)
        an.append(_bf(-kk * jnp.exp(ci - lw)))
        rn.append(_bf(r * jnp.exp(ci)))
        btk_bd.append(jnp.concatenate([bd(b * e_nc), bd(kd * e_nc)], axis=0))
        vn.append(_bf(v))
        v_bd.append(bd(v))
        bk_end.append(jnp.concatenate([_bf(b * e_end), _bf(kd * e_end)], axis=0))
        e_l.append(jnp.exp(cl))

    stack = lambda x, z: jnp.concatenate([x, z], axis=0)
    a_pow, t, akr, r_b = [], [], [], []
    for m, a_n, r_n, bk in zip(mk, an, rn, btk_bd):
        sn = _dot_nt(stack(a_n, r_n), bk)
        a_ab = jnp.where(m["strict_n"], sn[:L, :gw], 0.0)
        a_pow.append(a_ab)
        t.append(jnp.where(m["eye_n"], 1.0, a_ab))
        akr.append(stack(_bf(jnp.where(m["strict_n"], sn[:L, gw:], 0.0)),
                         _bf(jnp.where(m["incl_n"], sn[L:, gw:], 0.0))))
        r_b.append(_bf(jnp.where(m["incl_n"], sn[L:, :gw], 0.0)))
    tick()
    kv = each(_dot, akr, v_bd)
    a_bd = [bd(a) for bd, a in zip(blocks, a_pow)]
    tb = [_bf(ti) for ti in t]
    a_pow = [_dot(_bf(a), ab) for a, ab in zip(a_pow, a_bd)]
    for level in range(5):
        a_bd = [bd(a) for bd, a in zip(blocks, a_pow)]
        tick()
        if level < 4:
            res = [_dot(stack(tbi, _bf(a)), ab) for tbi, a, ab in zip(tb, a_pow, a_bd)]
            t = [ti + ri[:L] for ti, ri in zip(t, res)]
            a_pow = [ri[L:] for ri in res]
        else:
            t = [ti + _dot(tbi, ab) for ti, tbi, ab in zip(t, tb, a_bd)]
        tb = [_bf(ti) for ti in t]
    h_bf = [_bf(p[6]) for p in probs]
    hx = [_dot(stack(a_n, r_n), h) for a_n, r_n, h in zip(an, rn, h_bf)]
    x_bd = [bd(hxi[:L] + kvi[:L]) for bd, hxi, kvi in zip(blocks, hx, kv)]
    tick()
    u = [_dot(tbi, xi) for tbi, xi in zip(tb, x_bd)]
    tick()
    y = [hxi[L:] + _dot(rb, bd(ui)) + kvi[L:]
         for bd, hxi, rb, ui, kvi in zip(blocks, hx, r_b, u, kv)]
    e_col = [jnp.broadcast_to(el, (gw, gw)).T for el in e_l]
    h_new = []
    for m, ec, p, bke, ui, v_n in zip(mk, e_col, probs, bk_end, u, vn):
        upd = _dot_tn(bke, jnp.concatenate([_bf(ui), v_n], axis=0))
        h_new.append(ec * p[6] + jnp.where(m["same_head"], upd, 0.0))
    if side is not None:
        for _ in side:
            pass
    return list(zip(y, h_new))


def _chunk_row_block(b, i, *, reverse, n_lat, seq, ctx_len):
    nc = ctx_len // CHUNK
    nl = seq // CHUNK
    ci = (nc - 1 - i) if reverse else i
    li = (nl - 1 - (i - nc)) if reverse else (i - nc)
    return jnp.where(i < nc, (n_lat + b * ctx_len) // CHUNK + ci, b * nl + li)


def _gla_stages(qkvf, gdf, qkvb, gdb, gw2_ref, gb_ref, of_ref, ob_ref, sf_ref, sb_ref):
    dv, dk = sf_ref.shape[1:]
    L = CHUNK
    wk = B_HEADS * dk
    t_r = lax.broadcasted_iota(jnp.int32, (L, L), 0)
    t_c = lax.broadcasted_iota(jnp.int32, (L, L), 1)

    qh, kh, keh, vh, dech, keeph, sth = [], [], [], [], [], [], []
    for d, (qkv_ref, gd_ref, st_ref) in enumerate(((qkvf, gdf, sf_ref), (qkvb, gdb, sb_ref))):
        reverse = d == 1
        keep = (t_c >= t_r) if reverse else (t_c <= t_r)
        tri = _bf(jnp.where(keep, 1.0, 0.0))
        z = _dot(_bf(gd_ref[...]), gw2_ref[d]) + gb_ref[d:d + 1, :]
        la = -_softplus(-z) * (1.0 / B_GATE_TAU)
        cum = _dot_exact_lhs(tri, la)
        last = 0 if reverse else L - 1
        cl = cum[last:last + 1, :]
        q_in = _bf(qkv_ref[:, :wk] * (dk ** -0.5) * jnp.exp(cum))
        k_in = _bf(qkv_ref[:, wk:2 * wk] * jnp.exp(-cum))
        k_end = _bf(qkv_ref[:, wk:2 * wk] * jnp.exp(cl - cum))
        dec = jnp.exp(cl)
        for h in range(B_HEADS):
            ks = slice(h * dk, (h + 1) * dk)
            qh.append(q_in[:, ks])
            kh.append(k_in[:, ks])
            keh.append(k_end[:, ks])
            vh.append(_bf(qkv_ref[:, 2 * wk + h * dv:2 * wk + (h + 1) * dv]))
            dech.append(dec[:, ks])
            keeph.append(keep)
            sth.append(st_ref[h])
        yield

    sc = [_bf(jnp.where(kp, _dot_nt(q, k), 0.0)) for kp, q, k in zip(keeph, qh, kh)]
    yield
    inter = [_dot_nt(q, _bf(st)) for q, st in zip(qh, sth)]
    yield
    kvt = [_dot_tn(v, ke) for v, ke in zip(vh, keh)]
    yield
    out = [_dot(s_, v) + it for s_, v, it in zip(sc, vh, inter)]
    yield
    for d, (o_ref, st_ref) in enumerate(((of_ref, sf_ref), (ob_ref, sb_ref))):
        for h in range(B_HEADS):
            i = d * B_HEADS + h
            o_ref[:, h * dv:(h + 1) * dv] = out[i]
            st_ref[h] = sth[i] * dech[i] + kvt[i]


def _mix_scan_kernel(rvkf, lwf, df, rvkb, lwb, db, qkvf, gdf, qkvb, gdb, gw2_ref, gb_ref,
                     yf_ref, yb_ref, of_ref, ob_ref, hf_ref, hb_ref, sf_ref, sb_ref):
    @pl.when(pl.program_id(1) == 0)
    def _():
        for ref in (hf_ref, hb_ref, sf_ref, sb_ref):
            ref[...] = jnp.zeros_like(ref)

    gs = hf_ref.shape[0]
    aw = lwf.shape[1]
    probs = []
    for gi in range(gs):
        sl = slice(gi * GROUP_W, (gi + 1) * GROUP_W)
        f32 = lambda ref, j: ref[:, j * aw + gi * GROUP_W:j * aw + (gi + 1) * GROUP_W].astype(F32)
        for rvk, lw, dd, h_ref, rev in ((rvkf, lwf, df, hf_ref, False), (rvkb, lwb, db, hb_ref, True)):
            probs.append((f32(rvk, 0), f32(rvk, 1), f32(rvk, 2), lw[:, sl], f32(dd, 0), f32(dd, 1),
                          h_ref[gi], rev))
    gla = _gla_stages(qkvf, gdf, qkvb, gdb, gw2_ref, gb_ref, of_ref, ob_ref, sf_ref, sb_ref)
    res = _rwkv_chunks(probs, side=gla)
    for gi in range(gs):
        sl = slice(gi * GROUP_W, (gi + 1) * GROUP_W)
        yf_ref[:, sl], hf_ref[gi] = res[2 * gi]
        yb_ref[:, sl], hb_ref[gi] = res[2 * gi + 1]


def _mix_scan(rvk, d0, d1, lw0, lw1, pb, gw2p, gb, *, batch, n_lat, seq, ctx_len, wk, wv, col0):
    rows, aw = lw0.shape
    assert aw == GROUP_W * RWKV_GROUPS_PER_STEP
    dk, dv = wk // B_HEADS, wv // B_HEADS
    qkv_w = 2 * wk + wv
    assert col0 % qkv_w == 0
    geo = dict(n_lat=n_lat, seq=seq, ctx_len=ctx_len)
    gd_blk = (col0 + 2 * wk + 2 * wv) // LANE

    def specs(reverse):
        rb = lambda b, i: _chunk_row_block(b, i, reverse=reverse, **geo)
        row_blk = lambda w, cb=0: pl.BlockSpec((CHUNK, w), lambda b, i: (rb(b, i), cb))
        rw = [row_blk(3 * aw), row_blk(aw), row_blk(2 * aw)]
        gla = [row_blk(qkv_w, col0 // qkv_w), row_blk(LANE, gd_blk)]
        return rw, gla, row_blk(aw), row_blk(wv)

    rwf, glaf, yf, of = specs(False)
    rwb, glab, yb, ob = specs(True)
    sds_a = jax.ShapeDtypeStruct((rows, aw), F32)
    sds_b = jax.ShapeDtypeStruct((rows, wv), F32)
    gs = RWKV_GROUPS_PER_STEP
    return pl.pallas_call(
        _mix_scan_kernel,
        grid=(batch, (seq + ctx_len) // CHUNK),
        in_specs=rwf + rwb + glaf + glab + [
            pl.BlockSpec((2, LANE, wk), lambda b, i: (0, 0, 0)),
            pl.BlockSpec((2, wk), lambda b, i: (0, 0)),
        ],
        out_specs=[yf, yb, of, ob],
        out_shape=[sds_a, sds_a, sds_b, sds_b],
        scratch_shapes=[pltpu.VMEM((gs, GROUP_W, GROUP_W), F32),
                        pltpu.VMEM((gs, GROUP_W, GROUP_W), F32),
                        pltpu.VMEM((B_HEADS, dv, dk), F32), pltpu.VMEM((B_HEADS, dv, dk), F32)],
        compiler_params=_cparams(("parallel", "arbitrary")),
    )(rvk, lw0, d0, rvk, lw1, d1, pb, pb, pb, pb, gw2p, gb)


def _mixer_out_kernel(yf, yb, bonus, g, of, ob, gate, xa_ref, xb_ref, mod_ref, lnw, lnb, gn, e_ref,
                      w_ref, o_ref, h_ref, *, aw, dv, n_a_tiles):
    e_bf = e_ref[...]
    y = yf[...] + yb[...]
    inv = 1.0 / A_HEAD_DIM
    mean = _head_sum(y, e_bf) * inv
    dlt = y - mean
    var = _head_sum(dlt * dlt, e_bf) * inv
    yn = dlt * lax.rsqrt(var + A_LN_EPS) * lnw[...] + lnb[...]
    h_ref[:, :aw] = _bf((yn + bonus[...]) * g[...])

    o = of[...] + ob[...]
    gt = _silu(gate[...])
    for h in range(B_HEADS):
        sl = slice(h * dv, (h + 1) * dv)
        oh = o[:, sl]
        ms = jnp.mean(oh * oh, axis=-1, keepdims=True)
        h_ref[:, aw + h * dv:aw + (h + 1) * dv] = _bf(oh * lax.rsqrt(ms + NORM_EPS) * gn[...] * gt[:, sl])

    x = jnp.where(pl.program_id(0) < n_a_tiles, xa_ref[...], xb_ref[...])
    o_ref[...] = x + mod_ref[2:3, :] * _dot(h_ref[...], w_ref[...])


def _mixer_out(yf, yb, bonus, g, of, ob, pb, xa, xb, mods_l, ln_w, ln_b, gla_norm, e_bf, w_bf,
               seg_of_tile, *, wk, col0):
    d = xa.shape[1]
    rows = xa.shape[0] + xb.shape[0]
    aw = yf.shape[1]
    wv = of.shape[1]
    dv = wv // B_HEADS
    tm = 256
    tiles_per_row_tile = ROW_TILE // tm
    n_a_tiles = xa.shape[0] // tm
    rs = lambda w: pl.BlockSpec((tm, w), lambda i: (i, 0))
    full = lambda shape: pl.BlockSpec(shape, lambda i: (0,) * len(shape))
    return pl.pallas_call(
        functools.partial(_mixer_out_kernel, aw=aw, dv=dv, n_a_tiles=n_a_tiles),
        grid=(rows // tm,),
        in_specs=[
            rs(aw), rs(aw), rs(aw), rs(aw), rs(wv), rs(wv),
            pl.BlockSpec((tm, wv), lambda i: (i, (col0 + 2 * wk + wv) // wv)),
        ] + _two_source_specs(tm, d, n_a_tiles) + [
            pl.BlockSpec((None, N_MOD, d), lambda i: (seg_of_tile(i // tiles_per_row_tile), 0, 0)),
            full((1, aw)), full((1, aw)), full((1, dv)), full((GROUP_W, GROUP_W)),
            full((aw + wv, d)),
        ],
        out_specs=rs(d),
        out_shape=jax.ShapeDtypeStruct((rows, d), F32),
        scratch_shapes=[pltpu.VMEM((tm, aw + wv), BF16)],
        compiler_params=_cparams(("parallel",)),
    )(yf, yb, bonus, g, of, ob, pb, xa, xb, mods_l, ln_w.reshape(1, aw), ln_b.reshape(1, aw),
      gla_norm.reshape(1, dv), e_bf, w_bf)


def _lru_kernel(*refs, reverse, n_blk, combine, cols):
    if combine:
        (x_ref, xp_ref, xn_ref, h0_ref, cw_ref, cb_ref, wc_ref, ba_ref, bx_ref, lam_ref,
         hf_ref, gate_ref, out_ref, hl_ref, a_sc, u_sc, hs_sc, carry_sc) = refs
    else:
        (x_ref, xp_ref, xn_ref, h0_ref, cw_ref, cb_ref, wc_ref, ba_ref, bx_ref, lam_ref,
         out_ref, hl_ref, a_sc, u_sc, hs_sc, carry_sc) = refs
    j = pl.program_id(2)
    blk = (n_blk - 1 - j) if reverse else j
    lb = x_ref.shape[0]
    ct = cw_ref.shape[1]
    cblk = wc_ref.shape[1]
    rowid = lax.broadcasted_iota(jnp.int32, (lb, 1), 0)
    decay_rate = -C_CONST * _softplus(-lam_ref[...])

    @pl.when(j == 0)
    def _():
        carry_sc[...] = h0_ref[0:1, :]

    for q in (range(cols - 1, -1, -1) if reverse else range(cols)):
        lanes = slice(q * ct, (q + 1) * ct)
        x = x_ref[:, lanes]
        if q > 0:
            prev = slice((q - 1) * ct, q * ct)
            p6, p7 = x_ref[lb - 2:lb - 1, prev], x_ref[lb - 1:lb, prev]
        else:
            p6 = jnp.where(blk == 0, 0.0, xp_ref[SUBLANE - 2:SUBLANE - 1, :])
            p7 = jnp.where(blk == 0, 0.0, xp_ref[SUBLANE - 1:SUBLANE, :])
        if q < cols - 1:
            n0 = x_ref[0:1, (q + 1) * ct:(q + 2) * ct]
        else:
            n0 = jnp.where(blk == n_blk - 1, 0.0, xn_ref[0:1, :])
        xm1 = jnp.where(rowid == 0, p7, pltpu.roll(x, 1, 0))
        xm2 = jnp.where(rowid == 0, p6, jnp.where(rowid == 1, p7, pltpu.roll(x, 2, 0)))
        xp1 = jnp.where(rowid == lb - 1, n0, pltpu.roll(x, lb - 1, 0))
        xs = (cb_ref[...] + cw_ref[0:1, :] * xm2 + cw_ref[1:2, :] * xm1 + cw_ref[2:3, :] * x
              + cw_ref[3:4, :] * xp1)

        for n in range(ct // cblk):
            sl = slice(n * cblk, (n + 1) * cblk)
            xb = xs[:, sl]
            ri = _dot(_bf(xb), wc_ref[n])
            rg = _sigmoid(ri[:, :cblk] + ba_ref[:, sl])
            ig = _sigmoid(ri[:, cblk:] + bx_ref[:, sl])
            a = jnp.exp(decay_rate[:, sl] * rg)
            a_sc[q, :, sl] = a
            u_sc[q, :, sl] = jnp.sqrt(1.0 - a * a) * (ig * xb)

        def step(t, h, q=q):
            tt = (lb - 1 - t) if reverse else t
            h = a_sc[q, pl.ds(tt, 1), :] * h + u_sc[q, pl.ds(tt, 1), :]
            hs_sc[q, pl.ds(tt, 1), :] = h
            return h

        carry_sc[...] = lax.fori_loop(0, lb, step, carry_sc[...], unroll=8)
        if combine:
            out_ref[:, lanes] = ((hs_sc[q] + hf_ref[q * lb:(q + 1) * lb, :])
                                 * jax.nn.gelu(gate_ref[:, lanes]))
        else:
            out_ref[q * lb:(q + 1) * lb, :] = hs_sc[q]
    hl_ref[...] = jnp.broadcast_to(carry_sc[...], hl_ref.shape)


def _lru_pass(x_arr, x_specs, h0, conv_w, conv_b, wcat, ba, bx, lam, out_shape, out_spec, extra,
              *, batch, n_blk, lb, cols, reverse):
    c = conv_w.shape[1]
    ct = LRU_CHANNEL_TILE
    nct = c // ct
    cblk = wcat.shape[1]
    chan = lambda rows_: pl.BlockSpec((rows_, ct), lambda b, k, j: (0, k))
    in_specs = list(x_specs) + [
        pl.BlockSpec((SUBLANE, ct), lambda b, k, j: (b, k)),
        chan(C_CONV), chan(1),
        pl.BlockSpec((ct // cblk, cblk, 2 * cblk), lambda b, k, j: (k, 0, 0)),
        chan(1), chan(1), chan(1),
    ] + [s for _, s in extra]
    args = [x_arr, x_arr, x_arr, h0, conv_w, conv_b.reshape(1, c), wcat, ba.reshape(1, c),
            bx.reshape(1, c), lam.reshape(1, c)] + [a for a, _ in extra]
    return pl.pallas_call(
        functools.partial(_lru_kernel, reverse=reverse, n_blk=n_blk, combine=bool(extra),
                          cols=cols),
        grid=(batch, nct, n_blk),
        in_specs=in_specs,
        out_specs=[out_spec, pl.BlockSpec((SUBLANE, ct), lambda b, k, j: (b, k))],
        out_shape=[out_shape, jax.ShapeDtypeStruct((batch * SUBLANE, c), F32)],
        scratch_shapes=[pltpu.VMEM((cols, lb, ct), F32), pltpu.VMEM((cols, lb, ct), F32),
                        pltpu.VMEM((cols, lb, ct), F32), pltpu.VMEM((1, ct), F32)],
        compiler_params=_cparams(("parallel", "parallel", "arbitrary")),
    )(*args)


def _rglru(hw_ctx, gate_grid, x_grid, conv_w, conv_b, wa, ba, wx, bx, lam, *, batch, n_lat, seq,
           ctx_len):
    c = conv_w.shape[1]
    ct = LRU_CHANNEL_TILE
    rows = seq // GRID_W
    lbc = 128
    ncb = ctx_len // lbc
    xcol = c // ct

    wcat = [_bf(jnp.concatenate([wa[d], wx[d]], axis=-1)) for d in range(2)]
    zeros_state = jnp.zeros((batch * SUBLANE, c), F32)

    def ctx_specs(reverse):
        blk = (lambda j: ncb - 1 - j) if reverse else (lambda j: j)
        per8 = lbc // SUBLANE
        cur = pl.BlockSpec((lbc, ct), lambda b, k, j: (b * ncb + blk(j), xcol + k))
        prv = pl.BlockSpec((SUBLANE, ct), lambda b, k, j: (
            b * ncb * per8 + jnp.maximum(blk(j) * per8 - 1, 0), xcol + k))
        nxt = pl.BlockSpec((SUBLANE, ct), lambda b, k, j: (
            b * ncb * per8 + jnp.minimum((blk(j) + 1) * per8, ncb * per8 - 1), xcol + k))
        return cur, prv, nxt

    assert ct == c
    def lat_specs(reverse, kc):
        ncg = GRID_W // kc
        grp = (lambda j: ncg - 1 - j) if reverse else (lambda j: j)
        r8 = rows // SUBLANE
        cur = pl.BlockSpec((rows, kc * ct), lambda b, k, j: (b, grp(j)))
        prv = pl.BlockSpec((SUBLANE, ct), lambda b, k, j: (
            b * r8 + r8 - 1, jnp.maximum(grp(j) * kc - 1, 0)))
        nxt = pl.BlockSpec((SUBLANE, ct), lambda b, k, j: (
            b * r8, jnp.minimum(grp(j) * kc + kc, GRID_W - 1)))
        hf = pl.BlockSpec((kc * rows, ct), lambda b, k, j: (b * ncg + grp(j), 0))
        return (cur, prv, nxt), hf, cur

    common = dict(batch=batch)
    ctx_scratch_sds = jax.ShapeDtypeStruct((batch * ctx_len, c), F32)
    ctx_out = lambda reverse: pl.BlockSpec(
        (lbc, ct), lambda b, k, j: (b * ncb + ((ncb - 1 - j) if reverse else j), k))
    _, st = _lru_pass(hw_ctx, ctx_specs(False), zeros_state, conv_w, conv_b, wcat[0], ba[0], bx[0],
                      lam[0], ctx_scratch_sds, ctx_out(False), [], n_blk=ncb, lb=lbc, cols=1,
                      reverse=False, **common)
    hf_sds = jax.ShapeDtypeStruct((n_lat, c), F32)
    kc = LRU_COLS_FWD
    x_specs, hf_spec, _ = lat_specs(False, kc)
    hf, _ = _lru_pass(x_grid, x_specs, st, conv_w, conv_b, wcat[0], ba[0], bx[0], lam[0], hf_sds,
                      hf_spec, [], n_blk=GRID_W // kc, lb=rows, cols=kc, reverse=False, **common)
    _, st = _lru_pass(hw_ctx, ctx_specs(True), zeros_state, conv_w, conv_b, wcat[1], ba[1], bx[1],
                      lam[1], ctx_scratch_sds, ctx_out(True), [], n_blk=ncb, lb=lbc, cols=1,
                      reverse=True, **common)
    kc = LRU_COLS_BWD
    x_specs, hf_spec, grid_spec = lat_specs(True, kc)
    y_sds = jax.ShapeDtypeStruct((n_lat // GRID_W, GRID_W * c), F32)
    y, _ = _lru_pass(x_grid, x_specs, st, conv_w, conv_b, wcat[1], ba[1], bx[1], lam[1],
                     y_sds, grid_spec, [(hf, hf_spec), (gate_grid, grid_spec)],
                     n_blk=GRID_W // kc, lb=rows, cols=kc, reverse=True, **common)
    return y


def _pad_cols(w, n):
    return jnp.pad(w, ((0, 0),) * (w.ndim - 1) + ((0, n - w.shape[-1]),))


def kernel(x, c, ctx, c_ctx, mod_w, mod_b, norm1, norm2, mlp_w1, mlp_w2, ab_w_in, ab_w_out, rw_mu,
           rw_w0, rw_w2, rw_a0, rw_a2, rw_g2, rw_kk, rw_ka, rw_rk, rw_ln_w, rw_ln_b, gla_gw2,
           gla_gb, gla_norm, lru_w_in, lru_w_out, lru_conv_w, lru_conv_b, lru_wa, lru_ba, lru_wx,
           lru_bx, lru_lam, final_norm):
    batch, seq, d = x.shape
    ctx_len = ctx.shape[1]
    depth = mod_w.shape[0]
    n_lat = batch * seq
    aw = rw_w0.shape[-1]
    wk = gla_gb.shape[-1]
    wv = ab_w_out.shape[1] - aw
    a_cols = rw_mu.shape[-1]
    b_cols = ab_w_in.shape[-1] - a_cols
    assert depth == 2 and batch + 1 <= SUBLANE
    assert seq % ROW_TILE == 0 and (batch * ctx_len) % ROW_TILE == 0
    assert ctx_len % PREP_TILE == 0 and ctx_len % 128 == 0 and seq % (GRID_W * SUBLANE) == 0
    assert aw % GROUP_W == 0 and wv == aw and 2 * wk == wv

    tiles_per_seq = seq // ROW_TILE
    seg_of_tile = lambda i: jnp.minimum(i // tiles_per_seq, batch)

    x2 = x.reshape(n_lat, d)
    ctx2 = ctx.reshape(batch * ctx_len, d)
    cvec = jnp.concatenate(
        [c, c_ctx[None, :], jnp.zeros((SUBLANE - batch - 1, d), F32)], axis=0)
    mods = _mods(cvec, mod_w, mod_b)

    hid = jnp.arange(GROUP_W) // A_HEAD_DIM
    e_bf = _bf(hid[:, None] == hid[None, :])

    a_pad = -(-a_cols // LANE) * LANE
    col0 = -(-a_cols // (2 * wk)) * (2 * wk)
    n_tiles = 3
    tn = -(-(col0 + b_cols) // (n_tiles * LANE)) * LANE
    w_ab = _bf(jnp.concatenate(
        [_pad_cols(ab_w_in[0][:, :a_cols], col0),
         _pad_cols(ab_w_in[0][:, a_cols:], n_tiles * tn - col0)], axis=1))
    pa = pb = _norm_matmul(x2, ctx2, norm1[0], mods[0], w_ab, seg_of_tile, tn=tn)

    lora_w = 2 * A_LORA_W + 2 * A_LORA_A
    w2p = jnp.zeros((2, lora_w, aw), F32)
    a2p = jnp.zeros((2, lora_w, aw), F32)
    for dd in range(2):
        w2p = w2p.at[dd, dd * A_LORA_W:(dd + 1) * A_LORA_W].set(rw_w2[0, dd])
        a2p = a2p.at[dd, 2 * A_LORA_W + dd * A_LORA_A:2 * A_LORA_W + (dd + 1) * A_LORA_A].set(
            rw_a2[0, dd])
    (rvk, d0, d1, lw0, lw1, g_, bonus) = _rwkv_prep(
        pa, _pad_cols(rw_mu[0], a_pad), rw_w0[0], rw_a0[0], _bf(w2p), _bf(a2p), _bf(rw_g2[0]),
        rw_kk[0], rw_ka[0], rw_rk[0].reshape(-1), e_bf, n_lat=n_lat, seq=seq, ctx_len=ctx_len)
    gw2p = jnp.zeros((2, LANE, wk), F32)
    for dd in range(2):
        gw2p = gw2p.at[dd, dd * B_LORA:(dd + 1) * B_LORA].set(gla_gw2[0, dd])
    yf, yb, of, ob = _mix_scan(rvk, d0, d1, lw0, lw1, pb, _bf(gw2p), gla_gb[0],
                               batch=batch, n_lat=n_lat, seq=seq, ctx_len=ctx_len, wk=wk, wv=wv,
                               col0=col0)

    xs = _mixer_out(yf, yb, bonus, g_, of, ob, pb, x2, ctx2, mods[0], rw_ln_w[0], rw_ln_b[0],
                    gla_norm[0], e_bf, _layer_bf16(ab_w_out, 0), seg_of_tile, wk=wk, col0=col0)
    seg_of_mlp_tile = lambda i: jnp.minimum(i // (seq // MLP_TILE), batch)
    ctx_tiles = (batch * ctx_len) // ROW_TILE
    w1_bf, w2_bf = _layer_bf16(mlp_w1, 0), _layer_bf16(mlp_w2, 0)
    x_lat = _mlp(xs, norm2[0], mods[0], w1_bf, w2_bf, final_norm, seg_of_mlp_tile, tm=MLP_TILE,
                 tile0=0, n_tiles=n_lat // MLP_TILE, final_norm=False)
    x_ctx = _mlp(xs, norm2[0], mods[0], w1_bf, w2_bf, final_norm, seg_of_tile, tm=ROW_TILE,
                 tile0=n_lat // ROW_TILE, n_tiles=ctx_tiles, final_norm=False)

    w_lru = _layer_bf16(lru_w_in, 0)
    hw_ctx = _norm_matmul(x_ctx, None, norm1[1], mods[1], w_lru, lambda i: batch, tn=2048)
    gate_grid, x_grid = _norm_matmul_grid(x_lat, n_lat, seq // GRID_W, norm1[1], mods[1], w_lru)
    y = _rglru(hw_ctx, gate_grid, x_grid, lru_conv_w[0], lru_conv_b[0], lru_wa[0], lru_ba[0],
               lru_wx[0], lru_bx[0], lru_lam[0], batch=batch, n_lat=n_lat, seq=seq,
               ctx_len=ctx_len)
    xl = _proj_res(y, x_lat, n_lat, seq // GRID_W, mods[1], _layer_bf16(lru_w_out, 0))
    out = _mlp(xl, norm2[1], mods[1], _layer_bf16(mlp_w1, 1), _layer_bf16(mlp_w2, 1), final_norm,
               seg_of_mlp_tile,
               tm=MLP_TILE, tile0=0, n_tiles=n_lat // MLP_TILE, final_norm=True)
    return out.reshape(batch, seq, d)
```
